```python
import math
import jax, jax.numpy as jnp
from jax import lax
import numpy as np

D_MODEL = 1024
BATCH = 8
SEQ = 4096
DEPTH = 2

D_G = D_MODEL // 4
N_GROUPS = 5
D_MIX = N_GROUPS * D_G
N_SUB = 4
HEAD_DIM = D_G // N_SUB
CONV_A = 3
CONV_D = 31
CHUNK = 128
POOL_WINDOWS = (2, 4, 8, 16)
MEM_LEN = 256
LN_EPS = 1e-5
DEEPNORM_ALPHA = (2.0 * DEPTH) ** 0.25
DEEPNORM_BETA = (8.0 * DEPTH) ** -0.25

SPLIT_SIZES = (D_G, D_G, D_G, D_G, D_G, D_G, D_G, D_G, D_G, D_MIX)
D_IN = sum(SPLIT_SIZES)
SPLIT_OFFSETS = tuple(int(o) for o in np.cumsum(SPLIT_SIZES)[:-1])

kernel_name = "hybrid_parallel_conv_sgu_pool_conformer_memxattn"


def layer_norm(x, g, b):
    xf = x.astype(jnp.float32)
    mu = jnp.mean(xf, axis=-1, keepdims=True)
    var = jnp.mean(jnp.square(xf - mu), axis=-1, keepdims=True)
    y = (xf - mu) * lax.rsqrt(var + LN_EPS) * g.astype(jnp.float32) + b.astype(jnp.float32)
    return y.astype(x.dtype)


def causal_depthwise_conv(x, w):
    k, c = w.shape
    return lax.conv_general_dilated(
        x, w[:, None, :].astype(x.dtype), window_strides=(1,), padding=[(k - 1, 0)],
        dimension_numbers=("NWC", "WIO", "NWC"), feature_group_count=c)


def short_gated_conv(xa, ba, ca, w_conv):
    return ba * causal_depthwise_conv(ca * xa, w_conv)


def spatial_gating(u, v, ln_g, ln_b, w_s, b_s):
    bn, s, _ = v.shape
    u = jax.nn.gelu(u)
    v = layer_norm(jax.nn.gelu(v), ln_g, ln_b)
    v = v.reshape(bn, s // CHUNK, CHUNK, N_SUB, HEAD_DIM)
    mask = jnp.tril(jnp.ones((CHUNK, CHUNK), dtype=bool))
    w = jnp.where(mask[None], w_s, jnp.zeros_like(w_s))
    mixed = jnp.einsum("hts,bcshd->bcthd", w, v) + b_s.T[:, :, None]
    return u * mixed.reshape(bn, s, D_G)


def multiscale_pool(xc, w_grp, scale):
    bn, s, _ = xc.shape
    xf = xc.astype(jnp.float32).reshape(bn, s, N_SUB, HEAD_DIM)
    cs = jnp.pad(jnp.cumsum(xf, axis=1), ((0, 0), (1, 0), (0, 0), (0, 0)))
    t = jnp.arange(s)
    win = jnp.array(POOL_WINDOWS, dtype=jnp.int32)
    lo = jnp.maximum(t[:, None] + 1 - win[None, :], 0)
    window_sum = cs[:, 1:] - cs[:, lo, jnp.arange(N_SUB)[None, :]]
    count = jnp.minimum(t[:, None] + 1, win[None, :]).astype(jnp.float32)
    y = (window_sum / count[None, :, :, None] - xf).astype(xc.dtype)
    y = jnp.einsum("bsgc,gcd->bsgd", y, w_grp)
    return y.reshape(bn, s, D_G) * scale


def conformer_conv(a, g, w_dw, b_dw, ln_g, ln_b, w_pw):
    h = a * jax.nn.sigmoid(g)
    h = causal_depthwise_conv(h, w_dw) + b_dw
    h = jax.nn.silu(layer_norm(h, ln_g, ln_b))
    return h @ w_pw


def memory_cross_attention(q, mem, w_kv):
    bn, s, _ = q.shape
    k, v = jnp.split(mem @ w_kv, 2, axis=-1)
    q = q.reshape(bn, s, N_SUB, HEAD_DIM)
    k = k.reshape(bn, -1, N_SUB, HEAD_DIM)
    v = v.reshape(bn, -1, N_SUB, HEAD_DIM)
    scores = jnp.einsum("bshd,bmhd->bhsm", q, k).astype(jnp.float32) * (1.0 / math.sqrt(HEAD_DIM))
    p = jax.nn.softmax(scores, axis=-1).astype(v.dtype)
    return jnp.einsum("bhsm,bmhd->bshd", p, v).reshape(bn, s, D_G)


def hybrid_mixer(x, mem, w_in, conv_a_w, sg_ln_g, sg_ln_b, sg_w, sg_b, pool_w, pool_scale,
                 cc_dw_w, cc_dw_b, cc_ln_g, cc_ln_b, cc_pw_w, w_kv, w_out):
    proj = x @ w_in
    xa, ba, ca, u, v, xc, da, dg, q, gate = jnp.split(proj, SPLIT_OFFSETS, axis=-1)
    y_a = short_gated_conv(xa, ba, ca, conv_a_w)
    y_b = spatial_gating(u, v, sg_ln_g, sg_ln_b, sg_w, sg_b)
    y_c = multiscale_pool(xc, pool_w, pool_scale)
    y_d = conformer_conv(da, dg, cc_dw_w, cc_dw_b, cc_ln_g, cc_ln_b, cc_pw_w)
    y_e = memory_cross_attention(q, mem, w_kv)
    h = jnp.concatenate([y_a, y_b, y_c, y_d, y_e], axis=-1) * jax.nn.silu(gate)
    return h @ w_out


def _fwd_setup_inputs(seed: int = 0) -> dict:
    key = jax.random.key(seed)
    ks = jax.random.split(key, 20)
    L = DEPTH

    def nrm(k, shape, scale):
        return jax.random.normal(k, shape, jnp.float32) * scale

    return {
        "x": nrm(ks[0], (BATCH, SEQ, D_MODEL), 1.0),
        "mem": nrm(ks[1], (BATCH, MEM_LEN, D_MODEL), 1.0),
        "w_in": nrm(ks[2], (L, D_MODEL, D_IN), D_MODEL ** -0.5),
        "conv_a_w": nrm(ks[3], (L, CONV_A, D_G), CONV_A ** -0.5),
        "sg_ln_g": 1.0 + nrm(ks[4], (L, D_G), 0.05),
        "sg_ln_b": nrm(ks[5], (L, D_G), 0.05),
        "sg_w": nrm(ks[6], (L, N_SUB, CHUNK, CHUNK), CHUNK ** -0.5),
        "sg_b": 1.0 + nrm(ks[7], (L, N_SUB, CHUNK), 0.1),
        "pool_w": nrm(ks[8], (L, N_SUB, HEAD_DIM, HEAD_DIM), HEAD_DIM ** -0.5),
        "pool_scale": 1.0 + nrm(ks[9], (L, D_G), 0.1),
        "cc_dw_w": nrm(ks[10], (L, CONV_D, D_G), CONV_D ** -0.5),
        "cc_dw_b": nrm(ks[11], (L, D_G), 0.02),
        "cc_ln_g": 1.0 + nrm(ks[12], (L, D_G), 0.05),
        "cc_ln_b": nrm(ks[13], (L, D_G), 0.05),
        "cc_pw_w": nrm(ks[14], (L, D_G, D_G), D_G ** -0.5),
        "w_kv": nrm(ks[15], (L, D_MODEL, 2 * D_G), D_MODEL ** -0.5),
        "w_out": nrm(ks[16], (L, D_MIX, D_MODEL), D_MIX ** -0.5 * DEEPNORM_BETA),
        "ln_g": 1.0 + nrm(ks[17], (L, D_MODEL), 0.05),
        "ln_b": nrm(ks[18], (L, D_MODEL), 0.05),
    }


def _fwd_reference(x, mem, w_in, conv_a_w, sg_ln_g, sg_ln_b, sg_w, sg_b, pool_w, pool_scale,
              cc_dw_w, cc_dw_b, cc_ln_g, cc_ln_b, cc_pw_w, w_kv, w_out, ln_g, ln_b):
    for l in range(DEPTH):
        y = hybrid_mixer(x, mem, w_in[l], conv_a_w[l], sg_ln_g[l], sg_ln_b[l], sg_w[l], sg_b[l],
                         pool_w[l], pool_scale[l], cc_dw_w[l], cc_dw_b[l], cc_ln_g[l], cc_ln_b[l],
                         cc_pw_w[l], w_kv[l], w_out[l])
        x = layer_norm(DEEPNORM_ALPHA * x + y, ln_g[l], ln_b[l])
    return x


import jax as _jax
import jax.numpy as _jnp

TWIN_FORMAT = 'train_step'
FWD_PARAMS = ['x', 'mem', 'w_in', 'conv_a_w', 'sg_ln_g', 'sg_ln_b', 'sg_w', 'sg_b', 'pool_w', 'pool_scale', 'cc_dw_w', 'cc_dw_b', 'cc_ln_g', 'cc_ln_b', 'cc_pw_w', 'w_kv', 'w_out', 'ln_g', 'ln_b']
TWIN_WEIGHTS = ['w_in', 'conv_a_w', 'sg_ln_g', 'sg_ln_b', 'sg_w', 'sg_b', 'pool_w', 'pool_scale', 'cc_dw_w', 'cc_dw_b', 'cc_ln_g', 'cc_ln_b', 'cc_pw_w', 'w_kv', 'w_out', 'ln_g', 'ln_b']
TWIN_DIFF_INPUT = 'x'
TWIN_INPUTS = ['x', 'mem', 'w_in', 'conv_a_w', 'sg_ln_g', 'sg_ln_b', 'sg_w', 'sg_b', 'pool_w', 'pool_scale', 'cc_dw_w', 'cc_dw_b', 'cc_ln_g', 'cc_ln_b', 'cc_pw_w', 'w_kv', 'w_out', 'ln_g', 'ln_b', 'loss_target', 'm_w_in', 'm_conv_a_w', 'm_sg_ln_g', 'm_sg_ln_b', 'm_sg_w', 'm_sg_b', 'm_pool_w', 'm_pool_scale', 'm_cc_dw_w', 'm_cc_dw_b', 'm_cc_ln_g', 'm_cc_ln_b', 'm_cc_pw_w', 'm_w_kv', 'm_w_out', 'm_ln_g', 'm_ln_b', 'v_w_in', 'v_conv_a_w', 'v_sg_ln_g', 'v_sg_ln_b', 'v_sg_w', 'v_sg_b', 'v_pool_w', 'v_pool_scale', 'v_cc_dw_w', 'v_cc_dw_b', 'v_cc_ln_g', 'v_cc_ln_b', 'v_cc_pw_w', 'v_w_kv', 'v_w_out', 'v_ln_g', 'v_ln_b']
TWIN_OUTPUTS = ['loss', 'grad_x', 'grad_w_in', 'grad_conv_a_w', 'grad_sg_ln_g', 'grad_sg_ln_b', 'grad_sg_w', 'grad_sg_b', 'grad_pool_w', 'grad_pool_scale', 'grad_cc_dw_w', 'grad_cc_dw_b', 'grad_cc_ln_g', 'grad_cc_ln_b', 'grad_cc_pw_w', 'grad_w_kv', 'grad_w_out', 'grad_ln_g', 'grad_ln_b', 'delta_w_in', 'delta_conv_a_w', 'delta_sg_ln_g', 'delta_sg_ln_b', 'delta_sg_w', 'delta_sg_b', 'delta_pool_w', 'delta_pool_scale', 'delta_cc_dw_w', 'delta_cc_dw_b', 'delta_cc_ln_g', 'delta_cc_ln_b', 'delta_cc_pw_w', 'delta_w_kv', 'delta_w_out', 'delta_ln_g', 'delta_ln_b', 'new_m_w_in', 'new_m_conv_a_w', 'new_m_sg_ln_g', 'new_m_sg_ln_b', 'new_m_sg_w', 'new_m_sg_b', 'new_m_pool_w', 'new_m_pool_scale', 'new_m_cc_dw_w', 'new_m_cc_dw_b', 'new_m_cc_ln_g', 'new_m_cc_ln_b', 'new_m_cc_pw_w', 'new_m_w_kv', 'new_m_w_out', 'new_m_ln_g', 'new_m_ln_b', 'new_v_w_in', 'new_v_conv_a_w', 'new_v_sg_ln_g', 'new_v_sg_ln_b', 'new_v_sg_w', 'new_v_sg_b', 'new_v_pool_w', 'new_v_pool_scale', 'new_v_cc_dw_w', 'new_v_cc_dw_b', 'new_v_cc_ln_g', 'new_v_cc_ln_b', 'new_v_cc_pw_w', 'new_v_w_kv', 'new_v_w_out', 'new_v_ln_g', 'new_v_ln_b']
TWIN_LEAF_KINDS = {'loss': 'loss', 'grad_x': 'grad_x', 'grad_w_in': 'grad_w', 'grad_conv_a_w': 'grad_w', 'grad_sg_ln_g': 'grad_w', 'grad_sg_ln_b': 'grad_w', 'grad_sg_w': 'grad_w', 'grad_sg_b': 'grad_w', 'grad_pool_w': 'grad_w', 'grad_pool_scale': 'grad_w', 'grad_cc_dw_w': 'grad_w', 'grad_cc_dw_b': 'grad_w', 'grad_cc_ln_g': 'grad_w', 'grad_cc_ln_b': 'grad_w', 'grad_cc_pw_w': 'grad_w', 'grad_w_kv': 'grad_w', 'grad_w_out': 'grad_w', 'grad_ln_g': 'grad_w', 'grad_ln_b': 'grad_w', 'delta_w_in': 'delta_w', 'delta_conv_a_w': 'delta_w', 'delta_sg_ln_g': 'delta_w', 'delta_sg_ln_b': 'delta_w', 'delta_sg_w': 'delta_w', 'delta_sg_b': 'delta_w', 'delta_pool_w': 'delta_w', 'delta_pool_scale': 'delta_w', 'delta_cc_dw_w': 'delta_w', 'delta_cc_dw_b': 'delta_w', 'delta_cc_ln_g': 'delta_w', 'delta_cc_ln_b': 'delta_w', 'delta_cc_pw_w': 'delta_w', 'delta_w_kv': 'delta_w', 'delta_w_out': 'delta_w', 'delta_ln_g': 'delta_w', 'delta_ln_b': 'delta_w', 'new_m_w_in': 'new_m', 'new_m_conv_a_w': 'new_m', 'new_m_sg_ln_g': 'new_m', 'new_m_sg_ln_b': 'new_m', 'new_m_sg_w': 'new_m', 'new_m_sg_b': 'new_m', 'new_m_pool_w': 'new_m', 'new_m_pool_scale': 'new_m', 'new_m_cc_dw_w': 'new_m', 'new_m_cc_dw_b': 'new_m', 'new_m_cc_ln_g': 'new_m', 'new_m_cc_ln_b': 'new_m', 'new_m_cc_pw_w': 'new_m', 'new_m_w_kv': 'new_m', 'new_m_w_out': 'new_m', 'new_m_ln_g': 'new_m', 'new_m_ln_b': 'new_m', 'new_v_w_in': 'new_v', 'new_v_conv_a_w': 'new_v', 'new_v_sg_ln_g': 'new_v', 'new_v_sg_ln_b': 'new_v', 'new_v_sg_w': 'new_v', 'new_v_sg_b': 'new_v', 'new_v_pool_w': 'new_v', 'new_v_pool_scale': 'new_v', 'new_v_cc_dw_w': 'new_v', 'new_v_cc_dw_b': 'new_v', 'new_v_cc_ln_g': 'new_v', 'new_v_cc_ln_b': 'new_v', 'new_v_cc_pw_w': 'new_v', 'new_v_w_kv': 'new_v', 'new_v_w_out': 'new_v', 'new_v_ln_g': 'new_v', 'new_v_ln_b': 'new_v'}


def _forward(args):
    return _fwd_reference(*[args[k] for k in FWD_PARAMS])


def _output_shape():
    out = _jax.eval_shape(lambda: _forward(_fwd_setup_inputs(0)))
    return out.shape, out.dtype

N_MICROBATCH = 1
ADAM_LR = 0.001
ADAM_B1 = 0.9
ADAM_B2 = 0.999
ADAM_EPS = 1e-08
ADAM_WD = 0.01
ADAM_STEP = 10
PER_EXAMPLE_BATCH_AXIS = {'x': 0, 'mem': 0, 'loss_target': 0}
SHARED_INPUTS = []
_WEIGHT_DTYPES = {'w_in': _jnp.float32, 'conv_a_w': _jnp.float32, 'sg_ln_g': _jnp.float32, 'sg_ln_b': _jnp.float32, 'sg_w': _jnp.float32, 'sg_b': _jnp.float32, 'pool_w': _jnp.float32, 'pool_scale': _jnp.float32, 'cc_dw_w': _jnp.float32, 'cc_dw_b': _jnp.float32, 'cc_ln_g': _jnp.float32, 'cc_ln_b': _jnp.float32, 'cc_pw_w': _jnp.float32, 'w_kv': _jnp.float32, 'w_out': _jnp.float32, 'ln_g': _jnp.float32, 'ln_b': _jnp.float32}
MOMENT_SCALE = {'w_in': 2.540984e-02, 'conv_a_w': 3.532089e-02, 'sg_ln_g': 1.672458e-02, 'sg_ln_b': 1.702632e-02, 'sg_w': 1.125771e-02, 'sg_b': 1.552225e-02, 'pool_w': 2.982623e-02, 'pool_scale': 3.093264e-02, 'cc_dw_w': 2.148073e-02, 'cc_dw_b': 7.295029e-02, 'cc_ln_g': 3.379076e-02, 'cc_ln_b': 4.427576e-02, 'cc_pw_w': 2.417726e-02, 'w_kv': 3.799686e-03, 'w_out': 5.862488e-02, 'ln_g': 2.283902e+01, 'ln_b': 1.760292e+00}


def _to_microbatches(a, axis):
    t = _jnp.moveaxis(a, axis, 0)
    t = t.reshape((N_MICROBATCH, t.shape[0] // N_MICROBATCH) + t.shape[1:])
    return _jnp.moveaxis(t, 1, axis + 1)


def setup_inputs(seed: int = 0) -> dict:
    inp = _fwd_setup_inputs(seed)
    key = _jax.random.fold_in(_jax.random.key(seed), 7919)
    shape, _ = _output_shape()
    out = dict(inp)
    out["loss_target"] = _jax.random.normal(_jax.random.fold_in(key, 0), shape, _jnp.float32)
    for i, name in enumerate(TWIN_WEIGHTS):
        w = inp[name].astype(_jnp.float32)
        if MOMENT_SCALE is None:
            s = _jnp.sqrt(_jnp.mean(_jnp.square(w)) + 1e-30)
        else:
            s = MOMENT_SCALE[name]
        km, kv = _jax.random.split(_jax.random.fold_in(key, i + 1))
        out[name] = w
        out["m_" + name] = s * _jax.random.normal(km, w.shape, _jnp.float32)
        out["v_" + name] = (s * s) * _jax.random.uniform(kv, w.shape, _jnp.float32, 0.5, 1.5)
    if N_MICROBATCH > 1:
        for name, axis in PER_EXAMPLE_BATCH_AXIS.items():
            out[name] = _to_microbatches(out[name], axis)
    return {'x': out['x'], 'mem': out['mem'], 'w_in': out['w_in'], 'conv_a_w': out['conv_a_w'], 'sg_ln_g': out['sg_ln_g'], 'sg_ln_b': out['sg_ln_b'], 'sg_w': out['sg_w'], 'sg_b': out['sg_b'], 'pool_w': out['pool_w'], 'pool_scale': out['pool_scale'], 'cc_dw_w': out['cc_dw_w'], 'cc_dw_b': out['cc_dw_b'], 'cc_ln_g': out['cc_ln_g'], 'cc_ln_b': out['cc_ln_b'], 'cc_pw_w': out['cc_pw_w'], 'w_kv': out['w_kv'], 'w_out': out['w_out'], 'ln_g': out['ln_g'], 'ln_b': out['ln_b'], 'loss_target': out['loss_target'], 'm_w_in': out['m_w_in'], 'm_conv_a_w': out['m_conv_a_w'], 'm_sg_ln_g': out['m_sg_ln_g'], 'm_sg_ln_b': out['m_sg_ln_b'], 'm_sg_w': out['m_sg_w'], 'm_sg_b': out['m_sg_b'], 'm_pool_w': out['m_pool_w'], 'm_pool_scale': out['m_pool_scale'], 'm_cc_dw_w': out['m_cc_dw_w'], 'm_cc_dw_b': out['m_cc_dw_b'], 'm_cc_ln_g': out['m_cc_ln_g'], 'm_cc_ln_b': out['m_cc_ln_b'], 'm_cc_pw_w': out['m_cc_pw_w'], 'm_w_kv': out['m_w_kv'], 'm_w_out': out['m_w_out'], 'm_ln_g': out['m_ln_g'], 'm_ln_b': out['m_ln_b'], 'v_w_in': out['v_w_in'], 'v_conv_a_w': out['v_conv_a_w'], 'v_sg_ln_g': out['v_sg_ln_g'], 'v_sg_ln_b': out['v_sg_ln_b'], 'v_sg_w': out['v_sg_w'], 'v_sg_b': out['v_sg_b'], 'v_pool_w': out['v_pool_w'], 'v_pool_scale': out['v_pool_scale'], 'v_cc_dw_w': out['v_cc_dw_w'], 'v_cc_dw_b': out['v_cc_dw_b'], 'v_cc_ln_g': out['v_cc_ln_g'], 'v_cc_ln_b': out['v_cc_ln_b'], 'v_cc_pw_w': out['v_cc_pw_w'], 'v_w_kv': out['v_w_kv'], 'v_w_out': out['v_w_out'], 'v_ln_g': out['v_ln_g'], 'v_ln_b': out['v_ln_b']}


def _loss(weights, diff, rest, loss_target):
    with _jax.named_scope("forward"):
        args = {**rest, TWIN_DIFF_INPUT: diff, **{k: w.astype(_WEIGHT_DTYPES[k]) for k, w in weights.items()}}
        y = _forward(args)
    with _jax.named_scope("loss_head"):
        err = _jnp.square(y.astype(_jnp.float32) - loss_target)
        return 0.5 * _jnp.sum(_jnp.mean(err, axis=-1)) if err.ndim else 0.5 * err


def _adamw(w, g, m, v):
    m = ADAM_B1 * m + (1.0 - ADAM_B1) * g
    v = ADAM_B2 * v + (1.0 - ADAM_B2) * _jnp.square(g)
    m_hat = m / (1.0 - ADAM_B1 ** ADAM_STEP)
    v_hat = v / (1.0 - ADAM_B2 ** ADAM_STEP)
    delta = -ADAM_LR * (m_hat / (_jnp.sqrt(v_hat) + ADAM_EPS) + ADAM_WD * w)
    return delta, m, v


def reference(x, mem, w_in, conv_a_w, sg_ln_g, sg_ln_b, sg_w, sg_b, pool_w, pool_scale, cc_dw_w, cc_dw_b, cc_ln_g, cc_ln_b, cc_pw_w, w_kv, w_out, ln_g, ln_b, loss_target, m_w_in, m_conv_a_w, m_sg_ln_g, m_sg_ln_b, m_sg_w, m_sg_b, m_pool_w, m_pool_scale, m_cc_dw_w, m_cc_dw_b, m_cc_ln_g, m_cc_ln_b, m_cc_pw_w, m_w_kv, m_w_out, m_ln_g, m_ln_b, v_w_in, v_conv_a_w, v_sg_ln_g, v_sg_ln_b, v_sg_w, v_sg_b, v_pool_w, v_pool_scale, v_cc_dw_w, v_cc_dw_b, v_cc_ln_g, v_cc_ln_b, v_cc_pw_w, v_w_kv, v_w_out, v_ln_g, v_ln_b):
    given = dict(x=x, mem=mem, w_in=w_in, conv_a_w=conv_a_w, sg_ln_g=sg_ln_g, sg_ln_b=sg_ln_b, sg_w=sg_w, sg_b=sg_b, pool_w=pool_w, pool_scale=pool_scale, cc_dw_w=cc_dw_w, cc_dw_b=cc_dw_b, cc_ln_g=cc_ln_g, cc_ln_b=cc_ln_b, cc_pw_w=cc_pw_w, w_kv=w_kv, w_out=w_out, ln_g=ln_g, ln_b=ln_b, loss_target=loss_target, m_w_in=m_w_in, m_conv_a_w=m_conv_a_w, m_sg_ln_g=m_sg_ln_g, m_sg_ln_b=m_sg_ln_b, m_sg_w=m_sg_w, m_sg_b=m_sg_b, m_pool_w=m_pool_w, m_pool_scale=m_pool_scale, m_cc_dw_w=m_cc_dw_w, m_cc_dw_b=m_cc_dw_b, m_cc_ln_g=m_cc_ln_g, m_cc_ln_b=m_cc_ln_b, m_cc_pw_w=m_cc_pw_w, m_w_kv=m_w_kv, m_w_out=m_w_out, m_ln_g=m_ln_g, m_ln_b=m_ln_b, v_w_in=v_w_in, v_conv_a_w=v_conv_a_w, v_sg_ln_g=v_sg_ln_g, v_sg_ln_b=v_sg_ln_b, v_sg_w=v_sg_w, v_sg_b=v_sg_b, v_pool_w=v_pool_w, v_pool_scale=v_pool_scale, v_cc_dw_w=v_cc_dw_w, v_cc_dw_b=v_cc_dw_b, v_cc_ln_g=v_cc_ln_g, v_cc_ln_b=v_cc_ln_b, v_cc_pw_w=v_cc_pw_w, v_w_kv=v_w_kv, v_w_out=v_w_out, v_ln_g=v_ln_g, v_ln_b=v_ln_b)
    weights = {n: given[n] for n in TWIN_WEIGHTS}
    shared = {n: given[n] for n in SHARED_INPUTS}
    per_example = {n: given[n] for n in ['x', 'mem']}
    grad_fn = _jax.value_and_grad(_loss, argnums=(0, 1))

    def one_microbatch(ex, loss_target):
        ex = dict(ex)
        diff = ex.pop(TWIN_DIFF_INPUT)
        return grad_fn(weights, diff, {**shared, **ex}, loss_target)

    if N_MICROBATCH == 1:
        loss, (grad_w, grad_x) = one_microbatch(per_example, given["loss_target"])
    else:
        def body(carry, xs):
            loss_sum, grad_sum = carry
            l_k, (gw_k, gx_k) = one_microbatch(xs[0], xs[1])
            with _jax.named_scope("update"):
                return (loss_sum + l_k, _jax.tree.map(_jnp.add, grad_sum, gw_k)), gx_k

        init = (_jnp.zeros((), _jnp.float32), _jax.tree.map(_jnp.zeros_like, weights))
        (loss, grad_w), grad_x = _jax.lax.scan(body, init, (per_example, given["loss_target"]))
    with _jax.named_scope("update"):
        delta_w, new_m, new_v = {}, {}, {}
        for n in TWIN_WEIGHTS:
            delta_w[n], new_m[n], new_v[n] = _adamw(weights[n], grad_w[n], given["m_" + n], given["v_" + n])
    return (loss, grad_x, *[grad_w[n] for n in TWIN_WEIGHTS], *[delta_w[n] for n in TWIN_WEIGHTS],
            *[new_m[n] for n in TWIN_WEIGHTS], *[new_v[n] for n in TWIN_WEIGHTS])
```

```python
import functools
import math

import jax
import jax.numpy as jnp
from jax import lax
from jax.experimental import pallas as pl
from jax.experimental.pallas import tpu as pltpu

F32 = jnp.float32
BF16 = jnp.bfloat16
MM_DTYPE = jnp.bfloat16

D_MODEL = 1024
DEPTH = 2
D_G = 256
N_GROUPS = 5
D_MIX = N_GROUPS * D_G
N_SUB = 4
HEAD_DIM = D_G // N_SUB
CONV_A = 3
CONV_D = 31
CHUNK = 128
POOL_WINDOWS = (2, 4, 8, 16)
MEM_LEN = 256
LN_EPS = 1e-5
ALPHA = (2.0 * DEPTH) ** 0.25
D_IN = 9 * D_G + D_MIX
ATT_SCALE = 1.0 / math.sqrt(HEAD_DIM)

ADAM_LR = 0.001
ADAM_B1 = 0.9
ADAM_B2 = 0.999
ADAM_EPS = 1e-08
ADAM_WD = 0.01
ADAM_STEP = 10

N_DEV = 8
HALO = 32
TILE = 256
VMEM_LIMIT = 56 * 1024 * 1024

C_XA, C_BA, C_CA, C_U, C_V, C_XC, C_DA, C_DG, C_Q = range(9)
C_GATE = 9 * D_G


def _mm(a, b):
    return jnp.dot(a.astype(MM_DTYPE), b.astype(MM_DTYPE), preferred_element_type=F32)


def _mm_nt(a, b):
    return lax.dot_general(a.astype(MM_DTYPE), b.astype(MM_DTYPE), (((1,), (1,)), ((), ())),
                           preferred_element_type=F32)


def _mm_tn(a, b):
    return lax.dot_general(a.astype(MM_DTYPE), b.astype(MM_DTYPE), (((0,), (0,)), ((), ())),
                           preferred_element_type=F32)


def _sigmoid(x):
    return 1.0 / (1.0 + jnp.exp(-x))


_GELU_C = math.sqrt(2.0 / math.pi)
_GELU_A = 0.044715


def _gelu(x):
    th = jnp.tanh(_GELU_C * (x + _GELU_A * (x * x * x)))
    return 0.5 * x * (1.0 + th), th


def _dgelu(x, th):
    return 0.5 * (1.0 + th) + 0.5 * x * (1.0 - th * th) * (_GELU_C * (1.0 + 3.0 * _GELU_A * (x * x)))


def _ln_fwd(x, g, b):
    mu = jnp.mean(x, axis=-1, keepdims=True)
    xc = x - mu
    var = jnp.mean(xc * xc, axis=-1, keepdims=True)
    rstd = lax.rsqrt(var + LN_EPS)
    xhat = xc * rstd
    return xhat * g + b, xhat, rstd


def _ln_bwd(dy, xhat, rstd, g):
    dxhat = dy * g
    m1 = jnp.mean(dxhat, axis=-1, keepdims=True)
    m2 = jnp.mean(dxhat * xhat, axis=-1, keepdims=True)
    return rstd * (dxhat - m1 - xhat * m2)


def _rowsum(x):
    return jnp.sum(x, axis=0, keepdims=True)


def _col(ref, k):
    return ref[:, k * D_G:(k + 1) * D_G]


def _head_of_lane(shape):
    return jnp.right_shift(lax.broadcasted_iota(jnp.int32, shape, len(shape) - 1), HEAD_DIM.bit_length() - 1)


def _pool_select(lane_grp, s2, s4, s8, s16):
    return jnp.where(lane_grp == 0, s2, jnp.where(lane_grp == 1, s4, jnp.where(lane_grp == 2, s8, s16)))


def _branch_forward(p_ref, ph_ref, first, row0, km_ref, vm_ref, w, ext_a, ext_c, ext_d, tile):
    r = {}
    xa, ba, ca = _col(p_ref, C_XA), _col(p_ref, C_BA), _col(p_ref, C_CA)
    ext_a[0:HALO] = jnp.where(first, 0.0, _col(ph_ref, C_CA) * _col(ph_ref, C_XA))
    ext_a[HALO:HALO + tile] = ca * xa
    conv_a = w["conv_a"][0:1, :] * ext_a[pl.ds(HALO - 2, tile), :]
    for k in range(1, CONV_A):
        conv_a = conv_a + w["conv_a"][k:k + 1, :] * ext_a[pl.ds(HALO - 2 + k, tile), :]
    r.update(xa=xa, ba=ba, ca=ca, conv_a=conv_a)
    ya = ba * conv_a

    xc = _col(p_ref, C_XC)
    ext_c[0:HALO] = jnp.where(first, 0.0, _col(ph_ref, C_XC))
    ext_c[HALO:HALO + tile] = xc
    acc = xc
    sums = {}
    for k in range(1, POOL_WINDOWS[-1]):
        acc = acc + ext_c[pl.ds(HALO - k, tile), :]
        if k + 1 in POOL_WINDOWS:
            sums[k + 1] = acc
    lane_grp = _head_of_lane((tile, D_G))
    trow = row0 + lax.broadcasted_iota(jnp.int32, (tile, D_G), 0)
    win = _pool_select(lane_grp, 2, 4, 8, 16)
    cnt = jnp.minimum(trow + 1, win).astype(F32)
    ypre = _pool_select(lane_grp, sums[2], sums[4], sums[8], sums[16]) / cnt - xc
    pool_mm = _mm(ypre, w["pool_wbd"][...])
    yc = pool_mm * w["pool_scale"][...]
    r.update(lane_grp=lane_grp, cnt=cnt, ypre=ypre, pool_mm=pool_mm)

    da, dg = _col(p_ref, C_DA), _col(p_ref, C_DG)
    sig_dg = _sigmoid(dg)
    ext_d[0:HALO] = jnp.where(first, 0.0, _col(ph_ref, C_DA) * _sigmoid(_col(ph_ref, C_DG)))
    ext_d[HALO:HALO + tile] = da * sig_dg
    conv_d = w["cc_dw_b"][...] + w["cc_dw_w"][0:1, :] * ext_d[pl.ds(HALO - (CONV_D - 1), tile), :]
    for j in range(1, CONV_D):
        conv_d = conv_d + w["cc_dw_w"][j:j + 1, :] * ext_d[pl.ds(HALO - (CONV_D - 1) + j, tile), :]
    ln_d, xhat_d, rstd_d = _ln_fwd(conv_d, w["cc_ln_g"][...], w["cc_ln_b"][...])
    sig_ln = _sigmoid(ln_d)
    act_d = ln_d * sig_ln
    yd = _mm(act_d, w["cc_pw_w"][...])
    r.update(da=da, sig_dg=sig_dg, ln_d=ln_d, xhat_d=xhat_d, rstd_d=rstd_d, sig_ln=sig_ln, act_d=act_d)

    u, v = _col(p_ref, C_U), _col(p_ref, C_V)
    ug, th_u = _gelu(u)
    vg, th_v = _gelu(v)
    vn, xhat_v, rstd_v = _ln_fwd(vg, w["sg_ln_g"][...], w["sg_ln_b"][...])
    tri = (lax.broadcasted_iota(jnp.int32, (CHUNK, CHUNK), 0)
           >= lax.broadcasted_iota(jnp.int32, (CHUNK, CHUNK), 1))
    wm = [jnp.where(tri, w["sg_w"][h], 0.0).astype(MM_DTYPE) for h in range(N_SUB)]
    lo = lax.broadcasted_iota(jnp.int32, (CHUNK, 2 * HEAD_DIM), 1) < HEAD_DIM
    chunks = []
    for c in range(tile // CHUNK):
        halves = []
        for hf in range(2):
            vh = vn[c * CHUNK:(c + 1) * CHUNK, hf * 128:(hf + 1) * 128]
            halves.append(_mm(wm[2 * hf], jnp.where(lo, vh, 0.0)) + _mm(wm[2 * hf + 1], jnp.where(lo, 0.0, vh)))
        chunks.append(jnp.concatenate(halves, axis=1) + w["sg_bias"][...])
    mixed = jnp.concatenate(chunks, axis=0)
    yb = ug * mixed
    r.update(u=u, v=v, ug=ug, th_u=th_u, th_v=th_v, vn=vn, xhat_v=xhat_v, rstd_v=rstd_v, wm=wm, lo=lo,
             mixed=mixed, tri=tri)

    q = _col(p_ref, C_Q)
    ye = jnp.zeros((tile, D_G), F32)
    probs = []
    for h in range(N_SUB):
        s = _mm_nt(q, km_ref[h]) * ATT_SCALE
        e = jnp.exp(s - jnp.max(s, axis=-1, keepdims=True))
        p = e / jnp.sum(e, axis=-1, keepdims=True)
        probs.append(p)
        ye = ye + _mm(p, vm_ref[h])
    r.update(q=q, probs=probs)

    gate = p_ref[:, C_GATE:C_GATE + D_MIX]
    sig_gate = _sigmoid(gate)
    concat = jnp.concatenate([ya, yb, yc, yd, ye], axis=1)
    r.update(gate=gate, sig_gate=sig_gate, concat=concat)
    return r


_SMALL_W = ("conv_a", "sg_ln_g", "sg_ln_b", "sg_w", "sg_bias", "pool_wbd", "pool_scale",
            "cc_dw_w", "cc_dw_b", "cc_ln_g", "cc_ln_b", "cc_pw_w")


def _full_spec(a):
    nd = a.ndim
    return pl.BlockSpec(a.shape, lambda i, _nd=nd: (0,) * _nd)


def _proj_matmul(x, w_in, tm=256):
    s, k = x.shape
    n = w_in.shape[1]

    def body(x_ref, w_ref, o_ref):
        o_ref[...] = _mm(x_ref[...], w_ref[...])

    return pl.pallas_call(
        body, grid=(s // tm,),
        in_specs=[pl.BlockSpec((tm, k), lambda i: (i, 0)), pl.BlockSpec((k, n), lambda i: (0, 0))],
        out_specs=pl.BlockSpec((tm, n), lambda i: (i, 0)),
        out_shape=jax.ShapeDtypeStruct((s, n), F32), name="proj_mm",
        compiler_params=pltpu.CompilerParams(dimension_semantics=("arbitrary",), vmem_limit_bytes=VMEM_LIMIT),
    )(x, w_in)


def _kv_project(mem, w_kv):
    def body(mem_ref, w_ref, km_ref, vm_ref):
        kv = _mm(mem_ref[...], w_ref[...])
        k, v = kv[:, :D_G], kv[:, D_G:]
        grp = _head_of_lane((MEM_LEN, D_G))
        for h in range(N_SUB):
            km_ref[h] = jnp.where(grp == h, k, 0.0).astype(km_ref.dtype)
            vm_ref[h] = jnp.where(grp == h, v, 0.0).astype(vm_ref.dtype)

    shp = jax.ShapeDtypeStruct((N_SUB, MEM_LEN, D_G), MM_DTYPE)
    return pl.pallas_call(body, out_shape=(shp, shp), name="kv_project",
                          compiler_params=pltpu.CompilerParams(vmem_limit_bytes=VMEM_LIMIT))(mem, w_kv)


def _branch_fwd(proj, km, vm, sw, tile=TILE):
    s = proj.shape[0]
    hb = tile // HALO
    small = [sw[n] for n in _SMALL_W]

    def body(p_ref, ph_ref, km_ref, vm_ref, *rest):
        w = dict(zip(_SMALL_W, rest[:len(_SMALL_W)]))
        h_ref, ext_a, ext_c, ext_d = rest[len(_SMALL_W):]
        i = pl.program_id(0)
        r = _branch_forward(p_ref, ph_ref, i == 0, i * tile, km_ref, vm_ref, w, ext_a, ext_c, ext_d, tile)
        h_ref[...] = (r["concat"] * (r["gate"] * r["sig_gate"])).astype(h_ref.dtype)

    return pl.pallas_call(
        body, grid=(s // tile,),
        in_specs=[pl.BlockSpec((tile, D_IN), lambda i: (i, 0)),
                  pl.BlockSpec((HALO, D_IN), lambda i: (jnp.maximum(i * hb - 1, 0), 0)),
                  _full_spec(km), _full_spec(vm)] + [_full_spec(a) for a in small],
        out_specs=pl.BlockSpec((tile, D_MIX), lambda i: (i, 0)),
        out_shape=jax.ShapeDtypeStruct((s, D_MIX), MM_DTYPE),
        scratch_shapes=[pltpu.VMEM((HALO + tile, D_G), F32)] * 3, name="branch_fwd",
        compiler_params=pltpu.CompilerParams(dimension_semantics=("arbitrary",), vmem_limit_bytes=VMEM_LIMIT),
    )(proj, proj, km, vm, *small)


def _out_matmul_ln(h, w_out, x, ln_g, ln_b, tm=256):
    s = h.shape[0]

    def body(h_ref, w_ref, x_ref, g_ref, b_ref, z_ref, xn_ref):
        z = ALPHA * x_ref[...] + _mm(h_ref[...], w_ref[...])
        z_ref[...] = z
        xn_ref[...] = _ln_fwd(z, g_ref[...], b_ref[...])[0]

    row = lambda i: (i, 0)
    shp = jax.ShapeDtypeStruct((s, D_MODEL), F32)
    return pl.pallas_call(
        body, grid=(s // tm,),
        in_specs=[pl.BlockSpec((tm, D_MIX), row), _full_spec(w_out), pl.BlockSpec((tm, D_MODEL), row),
                  _full_spec(ln_g), _full_spec(ln_b)],
        out_specs=(pl.BlockSpec((tm, D_MODEL), row), pl.BlockSpec((tm, D_MODEL), row)),
        out_shape=(shp, shp), name="out_mm_ln",
        compiler_params=pltpu.CompilerParams(dimension_semantics=("arbitrary",), vmem_limit_bytes=VMEM_LIMIT),
    )(h, w_out, x, ln_g, ln_b)


def _ln_out_bwd(up, target, z, ln_g, ln_b, w_out, h, tm=256):
    s = z.shape[0]
    from_loss = target is not None
    other = target if from_loss else up

    def body(o_ref, z_ref, g_ref, b_ref, w_ref, h_ref, dz_ref, dh_ref, gw_ref, gg_ref, gb_ref, loss_ref, lacc):
        i = pl.program_id(0)

        @pl.when(i == 0)
        def _():
            gw_ref[...] = jnp.zeros_like(gw_ref)
            gg_ref[...] = jnp.zeros_like(gg_ref)
            gb_ref[...] = jnp.zeros_like(gb_ref)
            lacc[...] = jnp.zeros_like(lacc)
            loss_ref[...] = jnp.zeros_like(loss_ref)

        g = g_ref[...]
        xn, xhat, rstd = _ln_fwd(z_ref[...], g, b_ref[...])
        if from_loss:
            err = xn - o_ref[...]
            lacc[...] += _rowsum(err * err)
            dxn = err * (1.0 / D_MODEL)
        else:
            dxn = o_ref[...]
        gg_ref[...] += _rowsum(dxn * xhat)
        gb_ref[...] += _rowsum(dxn)
        dz = _ln_bwd(dxn, xhat, rstd, g)
        dz_ref[...] = dz
        dh_ref[...] = _mm_nt(dz, w_ref[...])
        gw_ref[...] += _mm_tn(h_ref[...], dz)

        if from_loss:
            @pl.when(i == pl.num_programs(0) - 1)
            def _():
                total = jnp.sum(lacc[...], axis=-1, keepdims=True) * (0.5 / D_MODEL)
                loss_ref[...] = jnp.broadcast_to(total, loss_ref.shape)

    row = lambda i: (i, 0)
    return pl.pallas_call(
        body, grid=(s // tm,),
        in_specs=[pl.BlockSpec((tm, D_MODEL), row), pl.BlockSpec((tm, D_MODEL), row), _full_spec(ln_g),
                  _full_spec(ln_b), _full_spec(w_out), pl.BlockSpec((tm, D_MIX), row)],
        out_specs=(pl.BlockSpec((tm, D_MODEL), row), pl.BlockSpec((tm, D_MIX), row),
                   pl.BlockSpec((D_MIX, D_MODEL), lambda i: (0, 0)), pl.BlockSpec((1, D_MODEL), lambda i: (0, 0)),
                   pl.BlockSpec((1, D_MODEL), lambda i: (0, 0)), pl.BlockSpec((1, 128), lambda i: (0, 0))),
        out_shape=(jax.ShapeDtypeStruct((s, D_MODEL), F32), jax.ShapeDtypeStruct((s, D_MIX), F32),
                   jax.ShapeDtypeStruct((D_MIX, D_MODEL), F32), jax.ShapeDtypeStruct((1, D_MODEL), F32),
                   jax.ShapeDtypeStruct((1, D_MODEL), F32), jax.ShapeDtypeStruct((1, 128), F32)),
        scratch_shapes=[pltpu.VMEM((1, D_MODEL), F32)],
        name="ln_out_bwd_loss" if from_loss else "ln_out_bwd",
        compiler_params=pltpu.CompilerParams(dimension_semantics=("arbitrary",), vmem_limit_bytes=VMEM_LIMIT),
    )(other, z, ln_g, ln_b, w_out, h)


_BRANCH_GRADS = (("conv_a", (CONV_A, D_G)), ("sg_ln_g", (1, D_G)), ("sg_ln_b", (1, D_G)),
                 ("sg_w", (N_SUB, CHUNK, CHUNK)), ("sg_b", (CHUNK, 128)), ("pool_wbd", (D_G, D_G)),
                 ("pool_scale", (1, D_G)), ("cc_dw_w", (CONV_D, D_G)), ("cc_dw_b", (1, D_G)),
                 ("cc_ln_g", (1, D_G)), ("cc_ln_b", (1, D_G)), ("cc_pw_w", (D_G, D_G)),
                 ("dk", (N_SUB, MEM_LEN, D_G)), ("dv", (N_SUB, MEM_LEN, D_G)))


def _branch_bwd(proj, dh, km, vm, sw, tile=TILE):
    s = proj.shape[0]
    nt = s // tile
    hb = tile // HALO
    small = [sw[n] for n in _SMALL_W]
    n_small = len(_SMALL_W)
    n_grads = len(_BRANCH_GRADS)

    def body(p_ref, ph_ref, dh_ref, km_ref, vm_ref, *rest):
        w = dict(zip(_SMALL_W, rest[:n_small]))
        dp_ref = rest[n_small]
        g = dict(zip([n for n, _ in _BRANCH_GRADS], rest[n_small + 1:n_small + 1 + n_grads]))
        ext_a, ext_c, ext_d, rev_a, rev_c, rev_d, bias_acc = rest[n_small + 1 + n_grads:]
        i = pl.program_id(0)
        t = nt - 1 - i

        @pl.when(i == 0)
        def _():
            for ref in g.values():
                ref[...] = jnp.zeros_like(ref)
            bias_acc[...] = jnp.zeros_like(bias_acc)
            for ref in (rev_a, rev_c, rev_d):
                ref[...] = jnp.zeros_like(ref)

        r = _branch_forward(p_ref, ph_ref, t == 0, t * tile, km_ref, vm_ref, w, ext_a, ext_c, ext_d, tile)

        def put(k, val, width=D_G):
            dp_ref[:, k:k + width] = val.astype(dp_ref.dtype)

        def push_rev(rev, val):
            head = rev[0:HALO]
            rev[tile:tile + HALO] = head
            rev[0:tile] = val

        dh_all = dh_ref[...]
        gate, sig_gate, concat = r["gate"], r["sig_gate"], r["concat"]
        put(C_GATE, dh_all * concat * (sig_gate * (1.0 + gate * (1.0 - sig_gate))), D_MIX)
        dconcat = dh_all * (gate * sig_gate)
        dya, dyb, dyc, dyd, dye = [dconcat[:, k * D_G:(k + 1) * D_G] for k in range(N_GROUPS)]

        put(C_BA * D_G, dya * r["conv_a"])
        dconv_a = dya * r["ba"]
        for k in range(CONV_A):
            g["conv_a"][k:k + 1, :] += _rowsum(dconv_a * ext_a[pl.ds(HALO - 2 + k, tile), :])
        push_rev(rev_a, dconv_a)
        dga = w["conv_a"][0:1, :] * rev_a[pl.ds(2, tile), :]
        for k in range(1, CONV_A):
            dga = dga + w["conv_a"][k:k + 1, :] * rev_a[pl.ds(2 - k, tile), :]
        put(C_CA * D_G, dga * r["xa"])
        put(C_XA * D_G, dga * r["ca"])

        g["pool_scale"][...] += _rowsum(dyc * r["pool_mm"])
        dmm = dyc * w["pool_scale"][...]
        g["pool_wbd"][...] += _mm_tn(r["ypre"], dmm)
        dypre = _mm_nt(dmm, w["pool_wbd"][...])
        dws = dypre / r["cnt"]
        push_rev(rev_c, dws)
        acc = dws
        sums = {}
        for k in range(1, POOL_WINDOWS[-1]):
            acc = acc + rev_c[pl.ds(k, tile), :]
            if k + 1 in POOL_WINDOWS:
                sums[k + 1] = acc
        put(C_XC * D_G, _pool_select(r["lane_grp"], sums[2], sums[4], sums[8], sums[16]) - dypre)

        g["cc_pw_w"][...] += _mm_tn(r["act_d"], dyd)
        dact = _mm_nt(dyd, w["cc_pw_w"][...])
        sig_ln, ln_d = r["sig_ln"], r["ln_d"]
        dln = dact * (sig_ln * (1.0 + ln_d * (1.0 - sig_ln)))
        g["cc_ln_g"][...] += _rowsum(dln * r["xhat_d"])
        g["cc_ln_b"][...] += _rowsum(dln)
        dconv_d = _ln_bwd(dln, r["xhat_d"], r["rstd_d"], w["cc_ln_g"][...])
        g["cc_dw_b"][...] += _rowsum(dconv_d)
        for j in range(CONV_D):
            g["cc_dw_w"][j:j + 1, :] += _rowsum(dconv_d * ext_d[pl.ds(HALO - (CONV_D - 1) + j, tile), :])
        push_rev(rev_d, dconv_d)
        dhd = w["cc_dw_w"][0:1, :] * rev_d[pl.ds(CONV_D - 1, tile), :]
        for j in range(1, CONV_D):
            dhd = dhd + w["cc_dw_w"][j:j + 1, :] * rev_d[pl.ds(CONV_D - 1 - j, tile), :]
        sig_dg = r["sig_dg"]
        put(C_DA * D_G, dhd * sig_dg)
        put(C_DG * D_G, dhd * r["da"] * sig_dg * (1.0 - sig_dg))

        dug = dyb * r["mixed"]
        dmixed = dyb * r["ug"]
        wm, lo, vn = r["wm"], r["lo"], r["vn"]
        dvn_chunks = []
        for c in range(tile // CHUNK):
            rows = slice(c * CHUNK, (c + 1) * CHUNK)
            bias_acc[...] += dmixed[rows, :]
            halves = []
            for hf in range(2):
                cols = slice(hf * 128, (hf + 1) * 128)
                dm = dmixed[rows, cols]
                dm_a, dm_b = jnp.where(lo, dm, 0.0), jnp.where(lo, 0.0, dm)
                vh = vn[rows, cols]
                g["sg_w"][2 * hf] += _mm_nt(dm_a, vh)
                g["sg_w"][2 * hf + 1] += _mm_nt(dm_b, vh)
                halves.append(_mm_tn(wm[2 * hf], dm_a) + _mm_tn(wm[2 * hf + 1], dm_b))
            dvn_chunks.append(jnp.concatenate(halves, axis=1))
        dvn = jnp.concatenate(dvn_chunks, axis=0)
        g["sg_ln_g"][...] += _rowsum(dvn * r["xhat_v"])
        g["sg_ln_b"][...] += _rowsum(dvn)
        dvg = _ln_bwd(dvn, r["xhat_v"], r["rstd_v"], w["sg_ln_g"][...])
        put(C_V * D_G, dvg * _dgelu(r["v"], r["th_v"]))
        put(C_U * D_G, dug * _dgelu(r["u"], r["th_u"]))

        q = r["q"]
        dq = jnp.zeros((tile, D_G), F32)
        for h in range(N_SUB):
            p = r["probs"][h]
            dp = _mm_nt(dye, vm_ref[h])
            g["dv"][h] += _mm_tn(p, dye)
            ds = p * (dp - jnp.sum(dp * p, axis=-1, keepdims=True)) * ATT_SCALE
            dq = dq + _mm(ds, km_ref[h])
            g["dk"][h] += _mm_tn(ds, q)
        put(C_Q * D_G, dq)

        @pl.when(i == nt - 1)
        def _():
            for h in range(N_SUB):
                g["sg_w"][h] = jnp.where(r["tri"], g["sg_w"][h], 0.0)
            lane_head = _head_of_lane((CHUNK, D_G))
            out_lane = lax.broadcasted_iota(jnp.int32, (CHUNK, 128), 1)
            ba = bias_acc[...]
            sgb = jnp.zeros((CHUNK, 128), F32)
            for h in range(N_SUB):
                col = jnp.sum(jnp.where(lane_head == h, ba, 0.0), axis=-1, keepdims=True)
                sgb = jnp.where(out_lane == h, col, sgb)
            g["sg_b"][...] = sgb

    rev = lambda i: (nt - 1 - i, 0)
    grad_specs = tuple(pl.BlockSpec(shape, lambda i, _nd=len(shape): (0,) * _nd) for _, shape in _BRANCH_GRADS)
    grad_shapes = tuple(jax.ShapeDtypeStruct(shape, F32) for _, shape in _BRANCH_GRADS)
    outs = pl.pallas_call(
        body, grid=(nt,),
        in_specs=[pl.BlockSpec((tile, D_IN), rev),
                  pl.BlockSpec((HALO, D_IN), lambda i: (jnp.maximum((nt - 1 - i) * hb - 1, 0), 0)),
                  pl.BlockSpec((tile, D_MIX), rev), _full_spec(km), _full_spec(vm)]
        + [_full_spec(a) for a in small],
        out_specs=(pl.BlockSpec((tile, D_IN), rev),) + grad_specs,
        out_shape=(jax.ShapeDtypeStruct((s, D_IN), MM_DTYPE),) + grad_shapes,
        scratch_shapes=[pltpu.VMEM((HALO + tile, D_G), F32)] * 6 + [pltpu.VMEM((CHUNK, D_G), F32)],
        name="branch_bwd",
        compiler_params=pltpu.CompilerParams(dimension_semantics=("arbitrary",), vmem_limit_bytes=VMEM_LIMIT),
    )(proj, proj, dh, km, vm, *small)
    return outs[0], dict(zip([n for n, _ in _BRANCH_GRADS], outs[1:]))


def _dx_matmul(dproj, w_in, dz, tm=256):
    s = dproj.shape[0]

    def body(dp_ref, w_ref, dz_ref, o_ref):
        o_ref[...] = _mm_nt(dp_ref[...], w_ref[...]) + ALPHA * dz_ref[...]

    row = lambda i: (i, 0)
    return pl.pallas_call(
        body, grid=(s // tm,),
        in_specs=[pl.BlockSpec((tm, D_IN), row), _full_spec(w_in), pl.BlockSpec((tm, D_MODEL), row)],
        out_specs=pl.BlockSpec((tm, D_MODEL), row),
        out_shape=jax.ShapeDtypeStruct((s, D_MODEL), F32), name="dx_mm",
        compiler_params=pltpu.CompilerParams(dimension_semantics=("arbitrary",), vmem_limit_bytes=VMEM_LIMIT),
    )(dproj, w_in, dz)


def _dw_in_matmul(x, dproj, tn=896, tk=512):
    s = x.shape[0]

    def body(x_ref, dp_ref, o_ref):
        @pl.when(pl.program_id(1) == 0)
        def _():
            o_ref[...] = jnp.zeros_like(o_ref)

        o_ref[...] += _mm_tn(x_ref[...], dp_ref[...])

    return pl.pallas_call(
        body, grid=(D_IN // tn, s // tk),
        in_specs=[pl.BlockSpec((tk, D_MODEL), lambda j, k: (k, 0)), pl.BlockSpec((tk, tn), lambda j, k: (k, j))],
        out_specs=pl.BlockSpec((D_MODEL, tn), lambda j, k: (0, j)),
        out_shape=jax.ShapeDtypeStruct((D_MODEL, D_IN), F32), name="dw_in_mm",
        compiler_params=pltpu.CompilerParams(dimension_semantics=("arbitrary", "arbitrary"),
                                             vmem_limit_bytes=VMEM_LIMIT),
    )(x, dproj)


def _kv_bwd(mem, dk, dv):
    def body(mem_ref, dk_ref, dv_ref, o_ref):
        grp = _head_of_lane((MEM_LEN, D_G))
        dk_sum = jnp.zeros((MEM_LEN, D_G), F32)
        dv_sum = jnp.zeros((MEM_LEN, D_G), F32)
        for h in range(N_SUB):
            dk_sum = dk_sum + jnp.where(grp == h, dk_ref[h], 0.0)
            dv_sum = dv_sum + jnp.where(grp == h, dv_ref[h], 0.0)
        o_ref[...] = _mm_tn(mem_ref[...], jnp.concatenate([dk_sum, dv_sum], axis=1))

    return pl.pallas_call(body, out_shape=jax.ShapeDtypeStruct((D_MODEL, 2 * D_G), F32), name="kv_bwd",
                          compiler_params=pltpu.CompilerParams(vmem_limit_bytes=VMEM_LIMIT))(mem, dk, dv)


def _prep_small(lw):
    wbd = jnp.zeros((D_G, D_G), F32)
    for gi in range(N_SUB):
        wbd = wbd.at[gi * HEAD_DIM:(gi + 1) * HEAD_DIM, gi * HEAD_DIM:(gi + 1) * HEAD_DIM].set(lw["pool_w"][gi])
    row = lambda a: a.reshape(1, -1)
    return dict(conv_a=lw["conv_a_w"], sg_ln_g=row(lw["sg_ln_g"]), sg_ln_b=row(lw["sg_ln_b"]), sg_w=lw["sg_w"],
                sg_bias=jnp.repeat(lw["sg_b"].T, HEAD_DIM, axis=1), pool_wbd=wbd, pool_scale=row(lw["pool_scale"]),
                cc_dw_w=lw["cc_dw_w"], cc_dw_b=row(lw["cc_dw_b"]), cc_ln_g=row(lw["cc_ln_g"]),
                cc_ln_b=row(lw["cc_ln_b"]), cc_pw_w=lw["cc_pw_w"])


def _local_step(x, mem, target, layers):
    saved = []
    for lw in layers:
        sw = _prep_small(lw)
        km, vm = _kv_project(mem, lw["w_kv"])
        proj = _proj_matmul(x, lw["w_in"])
        h = _branch_fwd(proj, km, vm, sw)
        z, xn = _out_matmul_ln(h, lw["w_out"], x, lw["ln_g"].reshape(1, -1), lw["ln_b"].reshape(1, -1))
        saved.append((x, proj, h, z, km, vm, sw))
        x = xn

    grads = [None] * len(layers)
    up, loss = None, None
    for li in reversed(range(len(layers))):
        lw = layers[li]
        x_in, proj, h, z, km, vm, sw = saved[li]
        dz, dh, g_w_out, g_ln_g, g_ln_b, loss_l = _ln_out_bwd(
            up, target if up is None else None, z, lw["ln_g"].reshape(1, -1), lw["ln_b"].reshape(1, -1), lw["w_out"], h)
        if up is None:
            loss = loss_l
        dproj, bg = _branch_bwd(proj, dh, km, vm, sw)
        up = _dx_matmul(dproj, lw["w_in"], dz)
        g_w_in = _dw_in_matmul(x_in, dproj)
        g_w_kv = _kv_bwd(mem, bg["dk"], bg["dv"])
        pool_w = jnp.stack([bg["pool_wbd"][gi * HEAD_DIM:(gi + 1) * HEAD_DIM, gi * HEAD_DIM:(gi + 1) * HEAD_DIM]
                            for gi in range(N_SUB)])
        grads[li] = dict(
            w_in=g_w_in, conv_a_w=bg["conv_a"], sg_ln_g=bg["sg_ln_g"][0], sg_ln_b=bg["sg_ln_b"][0], sg_w=bg["sg_w"],
            sg_b=bg["sg_b"][:, :N_SUB].T, pool_w=pool_w, pool_scale=bg["pool_scale"][0], cc_dw_w=bg["cc_dw_w"],
            cc_dw_b=bg["cc_dw_b"][0], cc_ln_g=bg["cc_ln_g"][0], cc_ln_b=bg["cc_ln_b"][0], cc_pw_w=bg["cc_pw_w"],
            w_kv=g_w_kv, w_out=g_w_out, ln_g=g_ln_g[0], ln_b=g_ln_b[0])
    return loss, up, grads


def _exchange(arrays, scatter, name):
    n = len(arrays)
    out_shapes = tuple(jax.ShapeDtypeStruct(a.shape if sc else (N_DEV,) + a.shape, a.dtype)
                       for a, sc in zip(arrays, scatter))

    def body(*refs):
        ins, outs = refs[:n], refs[n:2 * n]
        send_sems, recv_sems, local_sems = refs[2 * n:]
        x, y, c = lax.axis_index("x"), lax.axis_index("y"), lax.axis_index("c")
        me = 4 * x + 2 * y + c

        def src_of(a, dest):
            return ins[a].at[dest] if scatter[a] else ins[a]

        local = [pltpu.make_async_copy(src_of(a, me), outs[a].at[me], local_sems.at[a]) for a in range(n)]
        for cp in local:
            cp.start()
        sends, recvs = [], []
        for k in range(1, N_DEV):
            px = 1 - x if k & 4 else x
            py = 1 - y if k & 2 else y
            pc = 1 - c if k & 1 else c
            peer = 4 * px + 2 * py + pc
            for a in range(n):
                sems = dict(send_sem=send_sems.at[k - 1, a], recv_sem=recv_sems.at[k - 1, a],
                            device_id=(px, py, pc), device_id_type=pl.DeviceIdType.MESH)
                sends.append(pltpu.make_async_remote_copy(src_ref=src_of(a, peer), dst_ref=outs[a].at[me], **sems))
                recvs.append(pltpu.make_async_remote_copy(src_ref=src_of(a, peer), dst_ref=outs[a].at[peer], **sems))
        for cp in sends:
            cp.start()
        for cp in recvs:
            cp.wait_recv()
        for cp in sends:
            cp.wait_send()
        for cp in local:
            cp.wait()

    any_spec = pl.BlockSpec(memory_space=pl.ANY)
    return pl.pallas_call(
        body, in_specs=[any_spec] * n, out_specs=(any_spec,) * n, out_shape=out_shapes,
        scratch_shapes=[pltpu.SemaphoreType.DMA((N_DEV - 1, n)), pltpu.SemaphoreType.DMA((N_DEV - 1, n)),
                        pltpu.SemaphoreType.DMA((n,))],
        name=name,
    )(*arrays)


def _adamw(parts, w, m, v, name, tr=None):
    rows, cols = w.shape
    tr = rows if tr is None else tr

    def body(p_ref, w_ref, m_ref, v_ref, g_out, d_out, m_out, v_out):
        g = p_ref[0]
        for q in range(1, N_DEV):
            g = g + p_ref[q]
        m_new = ADAM_B1 * m_ref[...] + (1.0 - ADAM_B1) * g
        v_new = ADAM_B2 * v_ref[...] + (1.0 - ADAM_B2) * (g * g)
        m_hat = m_new / (1.0 - ADAM_B1 ** ADAM_STEP)
        v_hat = v_new / (1.0 - ADAM_B2 ** ADAM_STEP)
        g_out[...] = g
        d_out[...] = -ADAM_LR * (m_hat / (jnp.sqrt(v_hat) + ADAM_EPS) + ADAM_WD * w_ref[...])
        m_out[...] = m_new
        v_out[...] = v_new

    row = lambda i: (i, 0)
    blk = pl.BlockSpec((tr, cols), row)
    shp = jax.ShapeDtypeStruct((rows, cols), F32)
    return pl.pallas_call(
        body, grid=(rows // tr,),
        in_specs=[pl.BlockSpec((N_DEV, tr, cols), lambda i: (0, i, 0)), blk, blk, blk],
        out_specs=(blk, blk, blk, blk), out_shape=(shp, shp, shp, shp), name=name,
        compiler_params=pltpu.CompilerParams(dimension_semantics=("arbitrary",), vmem_limit_bytes=VMEM_LIMIT),
    )(parts, w, m, v)


_SHARDED_SMALL = ("conv_a_w", "cc_dw_w", "cc_pw_w")
_REPLICATED = ("sg_ln_g", "sg_ln_b", "sg_w", "sg_b", "pool_w", "pool_scale", "cc_dw_b", "cc_ln_g", "cc_ln_b",
               "ln_g", "ln_b")


def _pack(tensors, lead=0):
    flat = jnp.concatenate([t.reshape(t.shape[:lead] + (-1,)) for t in tensors], axis=-1)
    size = flat.shape[-1]
    rows = -(-size // 1024) * 8
    flat = jnp.pad(flat, [(0, 0)] * lead + [(0, rows * 128 - size)])
    return flat.reshape(flat.shape[:lead] + (rows, 128))


def _unpack(slab, shapes):
    flat = slab.reshape(-1)
    out, off = [], 0
    for shp in shapes:
        size = math.prod(shp)
        out.append(flat[off:off + size].reshape(shp))
        off += size
    return out


def kernel(x, mem, w_in, conv_a_w, sg_ln_g, sg_ln_b, sg_w, sg_b, pool_w, pool_scale, cc_dw_w, cc_dw_b, cc_ln_g, cc_ln_b, cc_pw_w, w_kv, w_out, ln_g, ln_b, loss_target, m_w_in, m_conv_a_w, m_sg_ln_g, m_sg_ln_b, m_sg_w, m_sg_b, m_pool_w, m_pool_scale, m_cc_dw_w, m_cc_dw_b, m_cc_ln_g, m_cc_ln_b, m_cc_pw_w, m_w_kv, m_w_out, m_ln_g, m_ln_b, v_w_in, v_conv_a_w, v_sg_ln_g, v_sg_ln_b, v_sg_w, v_sg_b, v_pool_w, v_pool_scale, v_cc_dw_w, v_cc_dw_b, v_cc_ln_g, v_cc_ln_b, v_cc_pw_w, v_w_kv, v_w_out, v_ln_g, v_ln_b):
    names = ("w_in", "conv_a_w", "sg_ln_g", "sg_ln_b", "sg_w", "sg_b", "pool_w", "pool_scale", "cc_dw_w", "cc_dw_b",
             "cc_ln_g", "cc_ln_b", "cc_pw_w", "w_kv", "w_out", "ln_g", "ln_b")
    wts = dict(zip(names, (w_in, conv_a_w, sg_ln_g, sg_ln_b, sg_w, sg_b, pool_w, pool_scale, cc_dw_w, cc_dw_b,
                           cc_ln_g, cc_ln_b, cc_pw_w, w_kv, w_out, ln_g, ln_b)))
    mom = dict(zip(names, (m_w_in, m_conv_a_w, m_sg_ln_g, m_sg_ln_b, m_sg_w, m_sg_b, m_pool_w, m_pool_scale,
                           m_cc_dw_w, m_cc_dw_b, m_cc_ln_g, m_cc_ln_b, m_cc_pw_w, m_w_kv, m_w_out, m_ln_g, m_ln_b)))
    var = dict(zip(names, (v_w_in, v_conv_a_w, v_sg_ln_g, v_sg_ln_b, v_sg_w, v_sg_b, v_pool_w, v_pool_scale,
                           v_cc_dw_w, v_cc_dw_b, v_cc_ln_g, v_cc_ln_b, v_cc_pw_w, v_w_kv, v_w_out, v_ln_g, v_ln_b)))
    L = DEPTH
    cs = D_IN // N_DEV
    gs = D_G // N_DEV
    ks = D_MODEL // N_DEV
    os_ = D_MIX // N_DEV

    conv_pack = _pack([conv_a_w, cc_dw_w])
    gathered = _exchange(
        [w_in.astype(MM_DTYPE).reshape(L * D_MODEL, cs), w_out.astype(MM_DTYPE).reshape(L * os_, D_MODEL),
         w_kv.astype(MM_DTYPE).reshape(L * ks, 2 * D_G), cc_pw_w.astype(MM_DTYPE).reshape(L * gs, D_G), conv_pack],
        [False] * 5, "gather_weights")
    w_in_f = gathered[0].reshape(N_DEV, L, D_MODEL, cs).transpose(1, 2, 0, 3).reshape(L, D_MODEL, D_IN)
    w_out_f = gathered[1].reshape(N_DEV, L, os_, D_MODEL).transpose(1, 0, 2, 3).reshape(L, D_MIX, D_MODEL)
    w_kv_f = gathered[2].reshape(N_DEV, L, ks, 2 * D_G).transpose(1, 0, 2, 3).reshape(L, D_MODEL, 2 * D_G)
    pw_f = gathered[3].reshape(N_DEV, L, gs, D_G).transpose(1, 0, 2, 3).reshape(L, D_G, D_G)
    conv_flat = gathered[4].reshape(N_DEV, -1)
    n_a = L * CONV_A * gs
    conv_a_f = conv_flat[:, :n_a].reshape(N_DEV, L, CONV_A, gs).transpose(1, 2, 0, 3).reshape(L, CONV_A, D_G)
    dw_f = conv_flat[:, n_a:n_a + L * CONV_D * gs].reshape(N_DEV, L, CONV_D, gs).transpose(1, 2, 0, 3).reshape(
        L, CONV_D, D_G)

    layers = []
    for l in range(L):
        lw = {n: wts[n][l] for n in _REPLICATED}
        lw.update(w_in=w_in_f[l], w_out=w_out_f[l], w_kv=w_kv_f[l], cc_pw_w=pw_f[l], conv_a_w=conv_a_f[l],
                  cc_dw_w=dw_f[l])
        layers.append(lw)

    loss_part, grad_x, grads = _local_step(x[0], mem[0], loss_target[0], layers)
    full = {n: jnp.stack([grads[l][n] for l in range(L)]) for n in names}

    g_w_in = full["w_in"].reshape(L, D_MODEL, N_DEV, cs).transpose(2, 0, 1, 3).reshape(N_DEV, L * D_MODEL, cs)
    g_w_out = full["w_out"].reshape(L, N_DEV, os_, D_MODEL).transpose(1, 0, 2, 3).reshape(N_DEV, L * os_, D_MODEL)
    g_w_kv = full["w_kv"].reshape(L, N_DEV, ks, 2 * D_G).transpose(1, 0, 2, 3).reshape(N_DEV, L * ks, 2 * D_G)
    g_small = _pack([full["conv_a_w"].reshape(L, CONV_A, N_DEV, gs).transpose(2, 0, 1, 3),
                     full["cc_dw_w"].reshape(L, CONV_D, N_DEV, gs).transpose(2, 0, 1, 3),
                     full["cc_pw_w"].reshape(L, N_DEV, gs, D_G).transpose(1, 0, 2, 3)], lead=1)
    g_repl = _pack([full[n] for n in _REPLICATED] + [loss_part[0, :1]])
    p_w_in, p_w_out, p_w_kv, p_small, p_repl = _exchange(
        [g_w_in, g_w_out, g_w_kv, g_small, g_repl], [True, True, True, True, False], "exchange_grads")

    out = {}

    def big(n, parts, shape2d, tr):
        res = _adamw(parts, wts[n].reshape(shape2d), mom[n].reshape(shape2d), var[n].reshape(shape2d),
                     "adamw_" + n, tr)
        out[n] = [r.reshape(wts[n].shape) for r in res]

    big("w_in", p_w_in, (L * D_MODEL, cs), 256)
    big("w_out", p_w_out, (L * os_, D_MODEL), 64)
    big("w_kv", p_w_kv, (L * ks, 2 * D_G), 64)

    small_names = _SHARDED_SMALL + _REPLICATED
    rows_sh = p_small.shape[1]
    parts_small = jnp.concatenate([p_small, p_repl], axis=1)

    def pack_small(src):
        sh = _pack([src[n] for n in _SHARDED_SMALL])
        rp = _pack([src[n] for n in _REPLICATED] + [jnp.zeros((1,), F32)])
        return jnp.concatenate([sh, rp], axis=0)

    res = _adamw(parts_small, pack_small(wts), pack_small(mom), pack_small(var), "adamw_small")
    loss = None
    for kind, slab in enumerate(res):
        sh = _unpack(slab[:rows_sh], [wts[n].shape for n in _SHARDED_SMALL])
        rp = _unpack(slab[rows_sh:], [wts[n].shape for n in _REPLICATED] + [(1,)])
        if kind == 0:
            loss = rp[-1][0]
        for n, t in zip(small_names, sh + rp[:-1]):
            out.setdefault(n, [None] * 4)[kind] = t

    return (loss, grad_x[None], *[out[n][0] for n in names], *[out[n][1] for n in names],
            *[out[n][2] for n in names], *[out[n][3] for n in names])
```

```python
import functools
import math

import jax
import jax.numpy as jnp
from jax import lax
from jax.experimental import pallas as pl
from jax.experimental.pallas import tpu as pltpu

F32 = jnp.float32
BF16 = jnp.bfloat16
MM_DTYPE = jnp.bfloat16

D_MODEL = 1024
DEPTH = 2
D_G = 256
N_GROUPS = 5
D_MIX = N_GROUPS * D_G
N_SUB = 4
HEAD_DIM = D_G // N_SUB
CONV_A = 3
CONV_D = 31
CHUNK = 128
POOL_WINDOWS = (2, 4, 8, 16)
MEM_LEN = 256
LN_EPS = 1e-5
ALPHA = (2.0 * DEPTH) ** 0.25
D_IN = 9 * D_G + D_MIX
ATT_SCALE = 1.0 / math.sqrt(HEAD_DIM)

ADAM_LR = 0.001
ADAM_B1 = 0.9
ADAM_B2 = 0.999
ADAM_EPS = 1e-08
ADAM_WD = 0.01
ADAM_STEP = 10

N_DEV = 8
HALO = 32
TILE = 256
VMEM_LIMIT = 56 * 1024 * 1024

C_XA, C_BA, C_CA, C_U, C_V, C_XC, C_DA, C_DG, C_Q = range(9)
C_GATE = 9 * D_G


def _mm(a, b):
    return jnp.dot(a.astype(MM_DTYPE), b.astype(MM_DTYPE), preferred_element_type=F32)


def _mm_nt(a, b):
    return lax.dot_general(a.astype(MM_DTYPE), b.astype(MM_DTYPE), (((1,), (1,)), ((), ())),
                           preferred_element_type=F32)


def _mm_tn(a, b):
    return lax.dot_general(a.astype(MM_DTYPE), b.astype(MM_DTYPE), (((0,), (0,)), ((), ())),
                           preferred_element_type=F32)


def _sigmoid(x):
    return 1.0 / (1.0 + jnp.exp(-x))


_GELU_C = math.sqrt(2.0 / math.pi)
_GELU_A = 0.044715


def _gelu(x):
    th = jnp.tanh(_GELU_C * (x + _GELU_A * (x * x * x)))
    return 0.5 * x * (1.0 + th), th


def _dgelu(x, th):
    return 0.5 * (1.0 + th) + 0.5 * x * (1.0 - th * th) * (_GELU_C * (1.0 + 3.0 * _GELU_A * (x * x)))


def _ln_fwd(x, g, b):
    mu = jnp.mean(x, axis=-1, keepdims=True)
    xc = x - mu
    var = jnp.mean(xc * xc, axis=-1, keepdims=True)
    rstd = lax.rsqrt(var + LN_EPS)
    xhat = xc * rstd
    return xhat * g + b, xhat, rstd


def _ln_bwd(dy, xhat, rstd, g):
    dxhat = dy * g
    m1 = jnp.mean(dxhat, axis=-1, keepdims=True)
    m2 = jnp.mean(dxhat * xhat, axis=-1, keepdims=True)
    return rstd * (dxhat - m1 - xhat * m2)


def _rowsum(x):
    return jnp.sum(x, axis=0, keepdims=True)


def _col(ref, k):
    return ref[:, k * D_G:(k + 1) * D_G]


def _head_of_lane(shape):
    return jnp.right_shift(lax.broadcasted_iota(jnp.int32, shape, len(shape) - 1), HEAD_DIM.bit_length() - 1)


def _pool_select(lane_grp, s2, s4, s8, s16):
    return jnp.where(lane_grp == 0, s2, jnp.where(lane_grp == 1, s4, jnp.where(lane_grp == 2, s8, s16)))


def _branch_forward(p_ref, ph_ref, first, row0, km_ref, vm_ref, w, ext_a, ext_c, ext_d, tile):
    r = {}
    xa, ba, ca = _col(p_ref, C_XA), _col(p_ref, C_BA), _col(p_ref, C_CA)
    ext_a[0:HALO] = jnp.where(first, 0.0, _col(ph_ref, C_CA) * _col(ph_ref, C_XA))
    ext_a[HALO:HALO + tile] = ca * xa
    conv_a = w["conv_a"][0:1, :] * ext_a[pl.ds(HALO - 2, tile), :]
    for k in range(1, CONV_A):
        conv_a = conv_a + w["conv_a"][k:k + 1, :] * ext_a[pl.ds(HALO - 2 + k, tile), :]
    r.update(xa=xa, ba=ba, ca=ca, conv_a=conv_a)
    ya = ba * conv_a

    xc = _col(p_ref, C_XC)
    ext_c[0:HALO] = jnp.where(first, 0.0, _col(ph_ref, C_XC))
    ext_c[HALO:HALO + tile] = xc
    acc = xc
    sums = {}
    for k in range(1, POOL_WINDOWS[-1]):
        acc = acc + ext_c[pl.ds(HALO - k, tile), :]
        if k + 1 in POOL_WINDOWS:
            sums[k + 1] = acc
    lane_grp = _head_of_lane((tile, D_G))
    trow = row0 + lax.broadcasted_iota(jnp.int32, (tile, D_G), 0)
    win = _pool_select(lane_grp, 2, 4, 8, 16)
    cnt = jnp.minimum(trow + 1, win).astype(F32)
    ypre = _pool_select(lane_grp, sums[2], sums[4], sums[8], sums[16]) / cnt - xc
    pool_mm = _mm(ypre, w["pool_wbd"][...])
    yc = pool_mm * w["pool_scale"][...]
    r.update(lane_grp=lane_grp, cnt=cnt, ypre=ypre, pool_mm=pool_mm)

    da, dg = _col(p_ref, C_DA), _col(p_ref, C_DG)
    sig_dg = _sigmoid(dg)
    ext_d[0:HALO] = jnp.where(first, 0.0, _col(ph_ref, C_DA) * _sigmoid(_col(ph_ref, C_DG)))
    ext_d[HALO:HALO + tile] = da * sig_dg
    conv_d = w["cc_dw_b"][...] + w["cc_dw_w"][0:1, :] * ext_d[pl.ds(HALO - (CONV_D - 1), tile), :]
    for j in range(1, CONV_D):
        conv_d = conv_d + w["cc_dw_w"][j:j + 1, :] * ext_d[pl.ds(HALO - (CONV_D - 1) + j, tile), :]
    ln_d, xhat_d, rstd_d = _ln_fwd(conv_d, w["cc_ln_g"][...], w["cc_ln_b"][...])
    sig_ln = _sigmoid(ln_d)
    act_d = ln_d * sig_ln
    yd = _mm(act_d, w["cc_pw_w"][...])
    r.update(da=da, sig_dg=sig_dg, ln_d=ln_d, xhat_d=xhat_d, rstd_d=rstd_d, sig_ln=sig_ln, act_d=act_d)

    u, v = _col(p_ref, C_U), _col(p_ref, C_V)
    ug, th_u = _gelu(u)
    vg, th_v = _gelu(v)
    vn, xhat_v, rstd_v = _ln_fwd(vg, w["sg_ln_g"][...], w["sg_ln_b"][...])
    tri = (lax.broadcasted_iota(jnp.int32, (CHUNK, CHUNK), 0)
           >= lax.broadcasted_iota(jnp.int32, (CHUNK, CHUNK), 1))
    wm = [jnp.where(tri, w["sg_w"][h], 0.0).astype(MM_DTYPE) for h in range(N_SUB)]
    lo = lax.broadcasted_iota(jnp.int32, (CHUNK, 2 * HEAD_DIM), 1) < HEAD_DIM
    chunks = []
    for c in range(tile // CHUNK):
        halves = []
        for hf in range(2):
            vh = vn[c * CHUNK:(c + 1) * CHUNK, hf * 128:(hf + 1) * 128]
            halves.append(_mm(wm[2 * hf], jnp.where(lo, vh, 0.0)) + _mm(wm[2 * hf + 1], jnp.where(lo, 0.0, vh)))
        chunks.append(jnp.concatenate(halves, axis=1) + w["sg_bias"][...])
    mixed = jnp.concatenate(chunks, axis=0)
    yb = ug * mixed
    r.update(u=u, v=v, ug=ug, th_u=th_u, th_v=th_v, vn=vn, xhat_v=xhat_v, rstd_v=rstd_v, wm=wm, lo=lo,
             mixed=mixed, tri=tri)

    q = _col(p_ref, C_Q)
    ye = jnp.zeros((tile, D_G), F32)
    probs = []
    for h in range(N_SUB):
        s = _mm_nt(q, km_ref[h]) * ATT_SCALE
        e = jnp.exp(s - jnp.max(s, axis=-1, keepdims=True))
        p = e / jnp.sum(e, axis=-1, keepdims=True)
        probs.append(p)
        ye = ye + _mm(p, vm_ref[h])
    r.update(q=q, probs=probs)

    gate = p_ref[:, C_GATE:C_GATE + D_MIX]
    sig_gate = _sigmoid(gate)
    concat = jnp.concatenate([ya, yb, yc, yd, ye], axis=1)
    r.update(gate=gate, sig_gate=sig_gate, concat=concat)
    return r


_SMALL_W = ("conv_a", "sg_ln_g", "sg_ln_b", "sg_w", "sg_bias", "pool_wbd", "pool_scale",
            "cc_dw_w", "cc_dw_b", "cc_ln_g", "cc_ln_b", "cc_pw_w")


def _full_spec(a):
    nd = a.ndim
    return pl.BlockSpec(a.shape, lambda i, _nd=nd: (0,) * _nd)


def _proj_matmul(x, w_in, tm=256):
    s, k = x.shape
    n = w_in.shape[1]

    def body(x_ref, w_ref, o_ref):
        o_ref[...] = _mm(x_ref[...], w_ref[...])

    return pl.pallas_call(
        body, grid=(s // tm,),
        in_specs=[pl.BlockSpec((tm, k), lambda i: (i, 0)), pl.BlockSpec((k, n), lambda i: (0, 0))],
        out_specs=pl.BlockSpec((tm, n), lambda i: (i, 0)),
        out_shape=jax.ShapeDtypeStruct((s, n), F32), name="proj_mm",
        compiler_params=pltpu.CompilerParams(dimension_semantics=("arbitrary",), vmem_limit_bytes=VMEM_LIMIT),
    )(x, w_in)


def _kv_project(mem, w_kv):
    def body(mem_ref, w_ref, km_ref, vm_ref):
        kv = _mm(mem_ref[...], w_ref[...])
        k, v = kv[:, :D_G], kv[:, D_G:]
        grp = _head_of_lane((MEM_LEN, D_G))
        for h in range(N_SUB):
            km_ref[h] = jnp.where(grp == h, k, 0.0).astype(km_ref.dtype)
            vm_ref[h] = jnp.where(grp == h, v, 0.0).astype(vm_ref.dtype)

    shp = jax.ShapeDtypeStruct((N_SUB, MEM_LEN, D_G), MM_DTYPE)
    return pl.pallas_call(body, out_shape=(shp, shp), name="kv_project",
                          compiler_params=pltpu.CompilerParams(vmem_limit_bytes=VMEM_LIMIT))(mem, w_kv)


def _branch_fwd(proj, km, vm, sw, tile=TILE):
    s = proj.shape[0]
    hb = tile // HALO
    small = [sw[n] for n in _SMALL_W]

    def body(p_ref, ph_ref, km_ref, vm_ref, *rest):
        w = dict(zip(_SMALL_W, rest[:len(_SMALL_W)]))
        h_ref, ext_a, ext_c, ext_d = rest[len(_SMALL_W):]
        i = pl.program_id(0)
        r = _branch_forward(p_ref, ph_ref, i == 0, i * tile, km_ref, vm_ref, w, ext_a, ext_c, ext_d, tile)
        h_ref[...] = (r["concat"] * (r["gate"] * r["sig_gate"])).astype(h_ref.dtype)

    return pl.pallas_call(
        body, grid=(s // tile,),
        in_specs=[pl.BlockSpec((tile, D_IN), lambda i: (i, 0)),
                  pl.BlockSpec((HALO, D_IN), lambda i: (jnp.maximum(i * hb - 1, 0), 0)),
                  _full_spec(km), _full_spec(vm)] + [_full_spec(a) for a in small],
        out_specs=pl.BlockSpec((tile, D_MIX), lambda i: (i, 0)),
        out_shape=jax.ShapeDtypeStruct((s, D_MIX), MM_DTYPE),
        scratch_shapes=[pltpu.VMEM((HALO + tile, D_G), F32)] * 3, name="branch_fwd",
        compiler_params=pltpu.CompilerParams(dimension_semantics=("arbitrary",), vmem_limit_bytes=VMEM_LIMIT),
    )(proj, proj, km, vm, *small)


def _out_matmul_ln(h, w_out, x, ln_g, ln_b, tm=256):
    s = h.shape[0]

    def body(h_ref, w_ref, x_ref, g_ref, b_ref, z_ref, xn_ref):
        z = ALPHA * x_ref[...] + _mm(h_ref[...], w_ref[...])
        z_ref[...] = z
        xn_ref[...] = _ln_fwd(z, g_ref[...], b_ref[...])[0]

    row = lambda i: (i, 0)
    shp = jax.ShapeDtypeStruct((s, D_MODEL), F32)
    return pl.pallas_call(
        body, grid=(s // tm,),
        in_specs=[pl.BlockSpec((tm, D_MIX), row), _full_spec(w_out), pl.BlockSpec((tm, D_MODEL), row),
                  _full_spec(ln_g), _full_spec(ln_b)],
        out_specs=(pl.BlockSpec((tm, D_MODEL), row), pl.BlockSpec((tm, D_MODEL), row)),
        out_shape=(shp, shp), name="out_mm_ln",
        compiler_params=pltpu.CompilerParams(dimension_semantics=("arbitrary",), vmem_limit_bytes=VMEM_LIMIT),
    )(h, w_out, x, ln_g, ln_b)


def _ln_out_bwd(up, target, z, ln_g, ln_b, w_out, h, tm=256):
    s = z.shape[0]
    from_loss = target is not None
    other = target if from_loss else up

    def body(o_ref, z_ref, g_ref, b_ref, w_ref, h_ref, dz_ref, dh_ref, gw_ref, gg_ref, gb_ref, loss_ref, lacc):
        i = pl.program_id(0)

        @pl.when(i == 0)
        def _():
            gw_ref[...] = jnp.zeros_like(gw_ref)
            gg_ref[...] = jnp.zeros_like(gg_ref)
            gb_ref[...] = jnp.zeros_like(gb_ref)
            lacc[...] = jnp.zeros_like(lacc)
            loss_ref[...] = jnp.zeros_like(loss_ref)

        g = g_ref[...]
        xn, xhat, rstd = _ln_fwd(z_ref[...], g, b_ref[...])
        if from_loss:
            err = xn - o_ref[...]
            lacc[...] += _rowsum(err * err)
            dxn = err * (1.0 / D_MODEL)
        else:
            dxn = o_ref[...]
        gg_ref[...] += _rowsum(dxn * xhat)
        gb_ref[...] += _rowsum(dxn)
        dz = _ln_bwd(dxn, xhat, rstd, g)
        dz_ref[...] = dz
        dh_ref[...] = _mm_nt(dz, w_ref[...])
        gw_ref[...] += _mm_tn(h_ref[...], dz)

        if from_loss:
            @pl.when(i == pl.num_programs(0) - 1)
            def _():
                total = jnp.sum(lacc[...], axis=-1, keepdims=True) * (0.5 / D_MODEL)
                loss_ref[...] = jnp.broadcast_to(total, loss_ref.shape)

    row = lambda i: (i, 0)
    return pl.pallas_call(
        body, grid=(s // tm,),
        in_specs=[pl.BlockSpec((tm, D_MODEL), row), pl.BlockSpec((tm, D_MODEL), row), _full_spec(ln_g),
                  _full_spec(ln_b), _full_spec(w_out), pl.BlockSpec((tm, D_MIX), row)],
        out_specs=(pl.BlockSpec((tm, D_MODEL), row), pl.BlockSpec((tm, D_MIX), row),
                   pl.BlockSpec((D_MIX, D_MODEL), lambda i: (0, 0)), pl.BlockSpec((1, D_MODEL), lambda i: (0, 0)),
                   pl.BlockSpec((1, D_MODEL), lambda i: (0, 0)), pl.BlockSpec((1, 128), lambda i: (0, 0))),
        out_shape=(jax.ShapeDtypeStruct((s, D_MODEL), F32), jax.ShapeDtypeStruct((s, D_MIX), F32),
                   jax.ShapeDtypeStruct((D_MIX, D_MODEL), F32), jax.ShapeDtypeStruct((1, D_MODEL), F32),
                   jax.ShapeDtypeStruct((1, D_MODEL), F32), jax.ShapeDtypeStruct((1, 128), F32)),
        scratch_shapes=[pltpu.VMEM((1, D_MODEL), F32)],
        name="ln_out_bwd_loss" if from_loss else "ln_out_bwd",
        compiler_params=pltpu.CompilerParams(dimension_semantics=("arbitrary",), vmem_limit_bytes=VMEM_LIMIT),
    )(other, z, ln_g, ln_b, w_out, h)


_BRANCH_GRADS = (("conv_a", (CONV_A, D_G)), ("sg_ln_g", (1, D_G)), ("sg_ln_b", (1, D_G)),
                 ("sg_w", (N_SUB, CHUNK, CHUNK)), ("sg_b", (CHUNK, 128)), ("pool_wbd", (D_G, D_G)),
                 ("pool_scale", (1, D_G)), ("cc_dw_w", (CONV_D, D_G)), ("cc_dw_b", (1, D_G)),
                 ("cc_ln_g", (1, D_G)), ("cc_ln_b", (1, D_G)), ("cc_pw_w", (D_G, D_G)),
                 ("dk", (N_SUB, MEM_LEN, D_G)), ("dv", (N_SUB, MEM_LEN, D_G)))


def _branch_bwd(proj, dh, km, vm, sw, tile=TILE):
    s = proj.shape[0]
    nt = s // tile
    hb = tile // HALO
    small = [sw[n] for n in _SMALL_W]
    n_small = len(_SMALL_W)
    n_grads = len(_BRANCH_GRADS)

    def body(p_ref, ph_ref, dh_ref, km_ref, vm_ref, *rest):
        w = dict(zip(_SMALL_W, rest[:n_small]))
        dp_ref = rest[n_small]
        g = dict(zip([n for n, _ in _BRANCH_GRADS], rest[n_small + 1:n_small + 1 + n_grads]))
        ext_a, ext_c, ext_d, rev_a, rev_c, rev_d, bias_acc = rest[n_small + 1 + n_grads:]
        i = pl.program_id(0)
        t = nt - 1 - i

        @pl.when(i == 0)
        def _():
            for ref in g.values():
                ref[...] = jnp.zeros_like(ref)
            bias_acc[...] = jnp.zeros_like(bias_acc)
            for ref in (rev_a, rev_c, rev_d):
                ref[...] = jnp.zeros_like(ref)

        r = _branch_forward(p_ref, ph_ref, t == 0, t * tile, km_ref, vm_ref, w, ext_a, ext_c, ext_d, tile)

        def put(k, val, width=D_G):
            dp_ref[:, k:k + width] = val.astype(dp_ref.dtype)

        def push_rev(rev, val):
            head = rev[0:HALO]
            rev[tile:tile + HALO] = head
            rev[0:tile] = val

        dh_all = dh_ref[...]
        gate, sig_gate, concat = r["gate"], r["sig_gate"], r["concat"]
        put(C_GATE, dh_all * concat * (sig_gate * (1.0 + gate * (1.0 - sig_gate))), D_MIX)
        dconcat = dh_all * (gate * sig_gate)
        dya, dyb, dyc, dyd, dye = [dconcat[:, k * D_G:(k + 1) * D_G] for k in range(N_GROUPS)]

        put(C_BA * D_G, dya * r["conv_a"])
        dconv_a = dya * r["ba"]
        for k in range(CONV_A):
            g["conv_a"][k:k + 1, :] += _rowsum(dconv_a * ext_a[pl.ds(HALO - 2 + k, tile), :])
        push_rev(rev_a, dconv_a)
        dga = w["conv_a"][0:1, :] * rev_a[pl.ds(2, tile), :]
        for k in range(1, CONV_A):
            dga = dga + w["conv_a"][k:k + 1, :] * rev_a[pl.ds(2 - k, tile), :]
        put(C_CA * D_G, dga * r["xa"])
        put(C_XA * D_G, dga * r["ca"])

        g["pool_scale"][...] += _rowsum(dyc * r["pool_mm"])
        dmm = dyc * w["pool_scale"][...]
        g["pool_wbd"][...] += _mm_tn(r["ypre"], dmm)
        dypre = _mm_nt(dmm, w["pool_wbd"][...])
        dws = dypre / r["cnt"]
        push_rev(rev_c, dws)
        acc = dws
        sums = {}
        for k in range(1, POOL_WINDOWS[-1]):
            acc = acc + rev_c[pl.ds(k, tile), :]
            if k + 1 in POOL_WINDOWS:
                sums[k + 1] = acc
        put(C_XC * D_G, _pool_select(r["lane_grp"], sums[2], sums[4], sums[8], sums[16]) - dypre)

        g["cc_pw_w"][...] += _mm_tn(r["act_d"], dyd)
        dact = _mm_nt(dyd, w["cc_pw_w"][...])
        sig_ln, ln_d = r["sig_ln"], r["ln_d"]
        dln = dact * (sig_ln * (1.0 + ln_d * (1.0 - sig_ln)))
        g["cc_ln_g"][...] += _rowsum(dln * r["xhat_d"])
        g["cc_ln_b"][...] += _rowsum(dln)
        dconv_d = _ln_bwd(dln, r["xhat_d"], r["rstd_d"], w["cc_ln_g"][...])
        g["cc_dw_b"][...] += _rowsum(dconv_d)
        for j in range(CONV_D):
            g["cc_dw_w"][j:j + 1, :] += _rowsum(dconv_d * ext_d[pl.ds(HALO - (CONV_D - 1) + j, tile), :])
        push_rev(rev_d, dconv_d)
        dhd = w["cc_dw_w"][0:1, :] * rev_d[pl.ds(CONV_D - 1, tile), :]
        for j in range(1, CONV_D):
            dhd = dhd + w["cc_dw_w"][j:j + 1, :] * rev_d[pl.ds(CONV_D - 1 - j, tile), :]
        sig_dg = r["sig_dg"]
        put(C_DA * D_G, dhd * sig_dg)
        put(C_DG * D_G, dhd * r["da"] * sig_dg * (1.0 - sig_dg))

        dug = dyb * r["mixed"]
        dmixed = dyb * r["ug"]
        wm, lo, vn = r["wm"], r["lo"], r["vn"]
        dvn_chunks = []
        for c in range(tile // CHUNK):
            rows = slice(c * CHUNK, (c + 1) * CHUNK)
            bias_acc[...] += dmixed[rows, :]
            halves = []
            for hf in range(2):
                cols = slice(hf * 128, (hf + 1) * 128)
                dm = dmixed[rows, cols]
                dm_a, dm_b = jnp.where(lo, dm, 0.0), jnp.where(lo, 0.0, dm)
                vh = vn[rows, cols]
                g["sg_w"][2 * hf] += _mm_nt(dm_a, vh)
                g["sg_w"][2 * hf + 1] += _mm_nt(dm_b, vh)
                halves.append(_mm_tn(wm[2 * hf], dm_a) + _mm_tn(wm[2 * hf + 1], dm_b))
            dvn_chunks.append(jnp.concatenate(halves, axis=1))
        dvn = jnp.concatenate(dvn_chunks, axis=0)
        g["sg_ln_g"][...] += _rowsum(dvn * r["xhat_v"])
        g["sg_ln_b"][...] += _rowsum(dvn)
        dvg = _ln_bwd(dvn, r["xhat_v"], r["rstd_v"], w["sg_ln_g"][...])
        put(C_V * D_G, dvg * _dgelu(r["v"], r["th_v"]))
        put(C_U * D_G, dug * _dgelu(r["u"], r["th_u"]))

        q = r["q"]
        dq = jnp.zeros((tile, D_G), F32)
        for h in range(N_SUB):
            p = r["probs"][h]
            dp = _mm_nt(dye, vm_ref[h])
            g["dv"][h] += _mm_tn(p, dye)
            ds = p * (dp - jnp.sum(dp * p, axis=-1, keepdims=True)) * ATT_SCALE
            dq = dq + _mm(ds, km_ref[h])
            g["dk"][h] += _mm_tn(ds, q)
        put(C_Q * D_G, dq)

        @pl.when(i == nt - 1)
        def _():
            for h in range(N_SUB):
                g["sg_w"][h] = jnp.where(r["tri"], g["sg_w"][h], 0.0)
            lane_head = _head_of_lane((CHUNK, D_G))
            out_lane = lax.broadcasted_iota(jnp.int32, (CHUNK, 128), 1)
            ba = bias_acc[...]
            sgb = jnp.zeros((CHUNK, 128), F32)
            for h in range(N_SUB):
                col = jnp.sum(jnp.where(lane_head == h, ba, 0.0), axis=-1, keepdims=True)
                sgb = jnp.where(out_lane == h, col, sgb)
            g["sg_b"][...] = sgb

    rev = lambda i: (nt - 1 - i, 0)
    grad_specs = tuple(pl.BlockSpec(shape, lambda i, _nd=len(shape): (0,) * _nd) for _, shape in _BRANCH_GRADS)
    grad_shapes = tuple(jax.ShapeDtypeStruct(shape, F32) for _, shape in _BRANCH_GRADS)
    outs = pl.pallas_call(
        body, grid=(nt,),
        in_specs=[pl.BlockSpec((tile, D_IN), rev),
                  pl.BlockSpec((HALO, D_IN), lambda i: (jnp.maximum((nt - 1 - i) * hb - 1, 0), 0)),
                  pl.BlockSpec((tile, D_MIX), rev), _full_spec(km), _full_spec(vm)]
        + [_full_spec(a) for a in small],
        out_specs=(pl.BlockSpec((tile, D_IN), rev),) + grad_specs,
        out_shape=(jax.ShapeDtypeStruct((s, D_IN), MM_DTYPE),) + grad_shapes,
        scratch_shapes=[pltpu.VMEM((HALO + tile, D_G), F32)] * 6 + [pltpu.VMEM((CHUNK, D_G), F32)],
        name="branch_bwd",
        compiler_params=pltpu.CompilerParams(dimension_semantics=("arbitrary",), vmem_limit_bytes=VMEM_LIMIT),
    )(proj, proj, dh, km, vm, *small)
    return outs[0], dict(zip([n for n, _ in _BRANCH_GRADS], outs[1:]))


def _dx_matmul(dproj, w_in, dz, tm=256):
    s = dproj.shape[0]

    def body(dp_ref, w_ref, dz_ref, o_ref):
        o_ref[...] = _mm_nt(dp_ref[...], w_ref[...]) + ALPHA * dz_ref[...]

    row = lambda i: (i, 0)
    return pl.pallas_call(
        body, grid=(s // tm,),
        in_specs=[pl.BlockSpec((tm, D_IN), row), _full_spec(w_in), pl.BlockSpec((tm, D_MODEL), row)],
        out_specs=pl.BlockSpec((tm, D_MODEL), row),
        out_shape=jax.ShapeDtypeStruct((s, D_MODEL), F32), name="dx_mm",
        compiler_params=pltpu.CompilerParams(dimension_semantics=("arbitrary",), vmem_limit_bytes=VMEM_LIMIT),
    )(dproj, w_in, dz)


def _dw_in_matmul(x, dproj, tn=896, tk=512):
    s = x.shape[0]

    def body(x_ref, dp_ref, o_ref):
        @pl.when(pl.program_id(1) == 0)
        def _():
            o_ref[...] = jnp.zeros_like(o_ref)

        o_ref[...] += _mm_tn(x_ref[...], dp_ref[...])

    return pl.pallas_call(
        body, grid=(D_IN // tn, s // tk),
        in_specs=[pl.BlockSpec((tk, D_MODEL), lambda j, k: (k, 0)), pl.BlockSpec((tk, tn), lambda j, k: (k, j))],
        out_specs=pl.BlockSpec((D_MODEL, tn), lambda j, k: (0, j)),
        out_shape=jax.ShapeDtypeStruct((D_MODEL, D_IN), F32), name="dw_in_mm",
        compiler_params=pltpu.CompilerParams(dimension_semantics=("arbitrary", "arbitrary"),
                                             vmem_limit_bytes=VMEM_LIMIT),
    )(x, dproj)


def _kv_bwd(mem, dk, dv):
    def body(mem_ref, dk_ref, dv_ref, o_ref):
        grp = _head_of_lane((MEM_LEN, D_G))
        dk_sum = jnp.zeros((MEM_LEN, D_G), F32)
        dv_sum = jnp.zeros((MEM_LEN, D_G), F32)
        for h in range(N_SUB):
            dk_sum = dk_sum + jnp.where(grp == h, dk_ref[h], 0.0)
            dv_sum = dv_sum + jnp.where(grp == h, dv_ref[h], 0.0)
        o_ref[...] = _mm_tn(mem_ref[...], jnp.concatenate([dk_sum, dv_sum], axis=1))

    return pl.pallas_call(body, out_shape=jax.ShapeDtypeStruct((D_MODEL, 2 * D_G), F32), name="kv_bwd",
                          compiler_params=pltpu.CompilerParams(vmem_limit_bytes=VMEM_LIMIT))(mem, dk, dv)


def _prep_small(lw):
    wbd = jnp.zeros((D_G, D_G), F32)
    for gi in range(N_SUB):
        wbd = wbd.at[gi * HEAD_DIM:(gi + 1) * HEAD_DIM, gi * HEAD_DIM:(gi + 1) * HEAD_DIM].set(lw["pool_w"][gi])
    row = lambda a: a.reshape(1, -1)
    return dict(conv_a=lw["conv_a_w"], sg_ln_g=row(lw["sg_ln_g"]), sg_ln_b=row(lw["sg_ln_b"]), sg_w=lw["sg_w"],
                sg_bias=jnp.repeat(lw["sg_b"].T, HEAD_DIM, axis=1), pool_wbd=wbd, pool_scale=row(lw["pool_scale"]),
                cc_dw_w=lw["cc_dw_w"], cc_dw_b=row(lw["cc_dw_b"]), cc_ln_g=row(lw["cc_ln_g"]),
                cc_ln_b=row(lw["cc_ln_b"]), cc_pw_w=lw["cc_pw_w"])


def _local_step(x, mem, target, layers):
    saved = []
    for lw in layers:
        x, sv = _layer_fwd(x, mem, lw)
        saved.append(sv)
    grads = [None] * len(layers)
    up, loss = None, None
    for li in reversed(range(len(layers))):
        up, loss_l, grads[li] = _layer_bwd(up, target if up is None else None, mem, layers[li], saved[li])
        loss = loss_l if loss is None else loss
    return loss, up, grads


def _layer_fwd(x, mem, lw):
    sw = _prep_small(lw)
    km, vm = _kv_project(mem, lw["w_kv"])
    proj = _proj_matmul(x, lw["w_in"])
    h = _branch_fwd(proj, km, vm, sw)
    z, xn = _out_matmul_ln(h, lw["w_out"], x, lw["ln_g"].reshape(1, -1), lw["ln_b"].reshape(1, -1))
    return xn, (x, proj, h, z, km, vm, sw)


def _layer_bwd(up, target, mem, lw, saved):
    x_in, proj, h, z, km, vm, sw = saved
    dz, dh, g_w_out, g_ln_g, g_ln_b, loss = _ln_out_bwd(
        up, target, z, lw["ln_g"].reshape(1, -1), lw["ln_b"].reshape(1, -1), lw["w_out"], h)
    dproj, bg = _branch_bwd(proj, dh, km, vm, sw)
    dx = _dx_matmul(dproj, lw["w_in"], dz)
    g_w_in = _dw_in_matmul(x_in, dproj)
    g_w_kv = _kv_bwd(mem, bg["dk"], bg["dv"])
    pool_w = jnp.stack([bg["pool_wbd"][gi * HEAD_DIM:(gi + 1) * HEAD_DIM, gi * HEAD_DIM:(gi + 1) * HEAD_DIM]
                        for gi in range(N_SUB)])
    grads = dict(
        w_in=g_w_in, conv_a_w=bg["conv_a"], sg_ln_g=bg["sg_ln_g"][0], sg_ln_b=bg["sg_ln_b"][0], sg_w=bg["sg_w"],
        sg_b=bg["sg_b"][:, :N_SUB].T, pool_w=pool_w, pool_scale=bg["pool_scale"][0], cc_dw_w=bg["cc_dw_w"],
        cc_dw_b=bg["cc_dw_b"][0], cc_ln_g=bg["cc_ln_g"][0], cc_ln_b=bg["cc_ln_b"][0], cc_pw_w=bg["cc_pw_w"],
        w_kv=g_w_kv, w_out=g_w_out, ln_g=g_ln_g[0], ln_b=g_ln_b[0])
    return dx, loss, grads


def _exchange(arrays, scatter, name):
    n = len(arrays)

    def body(*refs):
        ins, outs = refs[:n], refs[n:2 * n]
        local, sends, recvs = _exchange_copies(ins, outs, scatter, *refs[2 * n:])
        for cp in local + sends:
            cp.start()
        for cp in recvs:
            cp.wait_recv()
        for cp in sends:
            cp.wait_send()
        for cp in local:
            cp.wait()

    any_spec = pl.BlockSpec(memory_space=pl.ANY)
    return pl.pallas_call(
        body, in_specs=[any_spec] * n, out_specs=(any_spec,) * n, out_shape=_landing_shapes(arrays, scatter),
        scratch_shapes=_exchange_sems(n), name=name,
    )(*arrays)


def _landing_shapes(arrays, scatter):
    return tuple(jax.ShapeDtypeStruct(a.shape if sc else (N_DEV,) + a.shape, a.dtype)
                 for a, sc in zip(arrays, scatter))


def _exchange_sems(n):
    return [pltpu.SemaphoreType.DMA(((N_DEV - 1) * n,)), pltpu.SemaphoreType.DMA(((N_DEV - 1) * n,)),
            pltpu.SemaphoreType.DMA((n,))]


def _exchange_copies(ins, outs, scatter, send_sems, recv_sems, local_sems):
    n = len(ins)
    x, y, c = lax.axis_index("x"), lax.axis_index("y"), lax.axis_index("c")
    me = 4 * x + 2 * y + c

    def src_of(a, dest):
        return ins[a].at[dest] if scatter[a] else ins[a]

    local = [pltpu.make_async_copy(src_of(a, me), outs[a].at[me], local_sems.at[a]) for a in range(n)]
    sends, recvs = [], []
    for k in range(1, N_DEV):
        px = 1 - x if k & 4 else x
        py = 1 - y if k & 2 else y
        pc = 1 - c if k & 1 else c
        peer = 4 * px + 2 * py + pc
        for a in range(n):
            sems = dict(send_sem=send_sems.at[(k - 1) * n + a], recv_sem=recv_sems.at[(k - 1) * n + a],
                        device_id=(px, py, pc), device_id_type=pl.DeviceIdType.MESH)
            sends.append(pltpu.make_async_remote_copy(src_ref=src_of(a, peer), dst_ref=outs[a].at[me], **sems))
            recvs.append(pltpu.make_async_remote_copy(src_ref=src_of(a, peer), dst_ref=outs[a].at[peer], **sems))
    return local, sends, recvs


_HBM_SPEC = pl.BlockSpec(memory_space=pltpu.HBM)
_SEM_SPEC = pl.BlockSpec(memory_space=pltpu.SEMAPHORE)


def _exchange_start(arrays, scatter, name):
    n = len(arrays)
    shapes = _landing_shapes(arrays, scatter)
    lands = [pltpu.with_memory_space_constraint(lax.empty(s.shape, s.dtype), pltpu.HBM) for s in shapes]
    srcs = [pltpu.with_memory_space_constraint(a, pltpu.HBM) for a in arrays]

    def body(*refs):
        ins, outs = refs[:n], refs[n:2 * n]
        local, sends, _ = _exchange_copies(ins, outs, scatter, *refs[2 * n:2 * n + 3])
        for cp in local + sends:
            cp.start()
        token = refs[-1]
        token[...] = jnp.zeros_like(token)

    res = pl.pallas_call(
        body, name=name, in_specs=[_HBM_SPEC] * (2 * n),
        out_shape=tuple(_exchange_sems(n)) + tuple(pltpu.HBM(a.shape, a.dtype) for a in srcs)
        + tuple(pltpu.HBM(s.shape, s.dtype) for s in shapes) + (jax.ShapeDtypeStruct((8, 128), F32),),
        out_specs=(_SEM_SPEC,) * 3 + (_HBM_SPEC,) * (2 * n) + (pl.BlockSpec(memory_space=pltpu.VMEM),),
        input_output_aliases={i: 3 + i for i in range(2 * n)},
        compiler_params=pltpu.CompilerParams(has_side_effects=pltpu.SideEffectType.DATAFLOW_SIDE_EFFECTING),
    )(*srcs, *lands)
    return dict(sems=res[:3], srcs=res[3:3 + n], lands=res[3 + n:3 + 2 * n], token=res[-1], scatter=scatter)


def _exchange_wait(ticket, after, name):
    n = len(ticket["srcs"])
    scatter = ticket["scatter"]

    def body(*refs):
        ins, outs = refs[:n], refs[n:2 * n]
        local, sends, recvs = _exchange_copies(ins, outs, scatter, *refs[2 * n:2 * n + 3])
        for cp in recvs:
            cp.wait_recv()
        for cp in sends:
            cp.wait_send()
        for cp in local:
            cp.wait()

    both = list(ticket["srcs"]) + list(ticket["lands"])
    res = pl.pallas_call(
        body, name=name, in_specs=[_HBM_SPEC] * (2 * n) + [_SEM_SPEC] * 3 + [pl.BlockSpec(memory_space=pl.ANY)],
        out_shape=tuple(pltpu.HBM(a.shape, a.dtype) for a in both), out_specs=(_HBM_SPEC,) * (2 * n),
        input_output_aliases={i: i for i in range(2 * n)},
        compiler_params=pltpu.CompilerParams(has_side_effects=pltpu.SideEffectType.DATAFLOW_SIDE_EFFECTING),
    )(*both, *ticket["sems"], after)
    return res[n:]


def _adamw(parts, w, m, v, name, tr=None):
    rows, cols = w.shape
    tr = rows if tr is None else tr

    def body(p_ref, w_ref, m_ref, v_ref, g_out, d_out, m_out, v_out):
        g = p_ref[0]
        for q in range(1, N_DEV):
            g = g + p_ref[q]
        m_new = ADAM_B1 * m_ref[...] + (1.0 - ADAM_B1) * g
        v_new = ADAM_B2 * v_ref[...] + (1.0 - ADAM_B2) * (g * g)
        m_hat = m_new / (1.0 - ADAM_B1 ** ADAM_STEP)
        v_hat = v_new / (1.0 - ADAM_B2 ** ADAM_STEP)
        g_out[...] = g
        d_out[...] = -ADAM_LR * (m_hat / (jnp.sqrt(v_hat) + ADAM_EPS) + ADAM_WD * w_ref[...])
        m_out[...] = m_new
        v_out[...] = v_new

    row = lambda i: (i, 0)
    blk = pl.BlockSpec((tr, cols), row)
    shp = jax.ShapeDtypeStruct((rows, cols), F32)
    return pl.pallas_call(
        body, grid=(rows // tr,),
        in_specs=[pl.BlockSpec((N_DEV, tr, cols), lambda i: (0, i, 0)), blk, blk, blk],
        out_specs=(blk, blk, blk, blk), out_shape=(shp, shp, shp, shp), name=name,
        compiler_params=pltpu.CompilerParams(dimension_semantics=("arbitrary",), vmem_limit_bytes=VMEM_LIMIT),
    )(parts, w, m, v)


_SHARDED_SMALL = ("conv_a_w", "cc_dw_w", "cc_pw_w")
_REPLICATED = ("sg_ln_g", "sg_ln_b", "sg_w", "sg_b", "pool_w", "pool_scale", "cc_dw_b", "cc_ln_g", "cc_ln_b",
               "ln_g", "ln_b")


def _pack(tensors, lead=0):
    flat = jnp.concatenate([t.reshape(t.shape[:lead] + (-1,)) for t in tensors], axis=-1)
    size = flat.shape[-1]
    rows = -(-size // 1024) * 8
    flat = jnp.pad(flat, [(0, 0)] * lead + [(0, rows * 128 - size)])
    return flat.reshape(flat.shape[:lead] + (rows, 128))


def _unpack(slab, shapes):
    flat = slab.reshape(-1)
    out, off = [], 0
    for shp in shapes:
        size = math.prod(shp)
        out.append(flat[off:off + size].reshape(shp))
        off += size
    return out


def kernel(x, mem, w_in, conv_a_w, sg_ln_g, sg_ln_b, sg_w, sg_b, pool_w, pool_scale, cc_dw_w, cc_dw_b, cc_ln_g, cc_ln_b, cc_pw_w, w_kv, w_out, ln_g, ln_b, loss_target, m_w_in, m_conv_a_w, m_sg_ln_g, m_sg_ln_b, m_sg_w, m_sg_b, m_pool_w, m_pool_scale, m_cc_dw_w, m_cc_dw_b, m_cc_ln_g, m_cc_ln_b, m_cc_pw_w, m_w_kv, m_w_out, m_ln_g, m_ln_b, v_w_in, v_conv_a_w, v_sg_ln_g, v_sg_ln_b, v_sg_w, v_sg_b, v_pool_w, v_pool_scale, v_cc_dw_w, v_cc_dw_b, v_cc_ln_g, v_cc_ln_b, v_cc_pw_w, v_w_kv, v_w_out, v_ln_g, v_ln_b):
    names = ("w_in", "conv_a_w", "sg_ln_g", "sg_ln_b", "sg_w", "sg_b", "pool_w", "pool_scale", "cc_dw_w", "cc_dw_b",
             "cc_ln_g", "cc_ln_b", "cc_pw_w", "w_kv", "w_out", "ln_g", "ln_b")
    wts = dict(zip(names, (w_in, conv_a_w, sg_ln_g, sg_ln_b, sg_w, sg_b, pool_w, pool_scale, cc_dw_w, cc_dw_b,
                           cc_ln_g, cc_ln_b, cc_pw_w, w_kv, w_out, ln_g, ln_b)))
    mom = dict(zip(names, (m_w_in, m_conv_a_w, m_sg_ln_g, m_sg_ln_b, m_sg_w, m_sg_b, m_pool_w, m_pool_scale,
                           m_cc_dw_w, m_cc_dw_b, m_cc_ln_g, m_cc_ln_b, m_cc_pw_w, m_w_kv, m_w_out, m_ln_g, m_ln_b)))
    var = dict(zip(names, (v_w_in, v_conv_a_w, v_sg_ln_g, v_sg_ln_b, v_sg_w, v_sg_b, v_pool_w, v_pool_scale,
                           v_cc_dw_w, v_cc_dw_b, v_cc_ln_g, v_cc_ln_b, v_cc_pw_w, v_w_kv, v_w_out, v_ln_g, v_ln_b)))
    L = DEPTH
    cs = D_IN // N_DEV
    gs = D_G // N_DEV
    ks = D_MODEL // N_DEV
    os_ = D_MIX // N_DEV

    n_a, n_d = CONV_A * gs, CONV_D * gs

    def weight_shards(l):
        return [w_in[l].astype(MM_DTYPE), w_out[l].astype(MM_DTYPE), w_kv[l].astype(MM_DTYPE),
                cc_pw_w[l].astype(MM_DTYPE), _pack([conv_a_w[l], cc_dw_w[l]])]

    def full_weights(l, gathered, tie=None):
        conv_flat = gathered[4].reshape(N_DEV, -1)
        if tie is not None:
            conv_flat = conv_flat + tie
        lw = {n: wts[n][l] for n in _REPLICATED}
        lw.update(w_in=gathered[0].transpose(1, 0, 2).reshape(D_MODEL, D_IN),
                  w_out=gathered[1].reshape(D_MIX, D_MODEL), w_kv=gathered[2].reshape(D_MODEL, 2 * D_G),
                  cc_pw_w=gathered[3].reshape(D_G, D_G),
                  conv_a_w=conv_flat[:, :n_a].reshape(N_DEV, CONV_A, gs).transpose(1, 0, 2).reshape(CONV_A, D_G),
                  cc_dw_w=conv_flat[:, n_a:n_a + n_d].reshape(N_DEV, CONV_D, gs).transpose(1, 0, 2).reshape(
                      CONV_D, D_G))
        return lw

    def grad_slabs(g, loss_part):
        small = _pack([g["conv_a_w"].reshape(CONV_A, N_DEV, gs).transpose(1, 0, 2),
                       g["cc_dw_w"].reshape(CONV_D, N_DEV, gs).transpose(1, 0, 2),
                       g["cc_pw_w"].reshape(N_DEV, gs, D_G)], lead=1)
        repl = _pack([g[n] for n in _REPLICATED] + [loss_part])
        return [g["w_in"].reshape(D_MODEL, N_DEV, cs).transpose(1, 0, 2), g["w_out"].reshape(N_DEV, os_, D_MODEL),
                g["w_kv"].reshape(N_DEV, ks, 2 * D_G), small, repl]

    grad_modes = [True, True, True, True, False]
    xs, mems, tgt = x[0], mem[0], loss_target[0]

    lw0 = full_weights(0, _exchange(weight_shards(0), [False] * 5, "gather_weights_0"))
    gather_1 = _exchange_start(weight_shards(1), [False] * 5, "gather_weights_1_start")
    lw0["conv_a_w"] = lw0["conv_a_w"] + gather_1["token"][0, 0]
    x1, saved0 = _layer_fwd(xs, mems, lw0)
    lw1 = full_weights(1, _exchange_wait(gather_1, x1, "gather_weights_1_wait"))
    _, saved1 = _layer_fwd(x1, mems, lw1)

    up, loss_part, g1 = _layer_bwd(None, tgt, mems, lw1, saved1)
    grads_1 = _exchange_start(grad_slabs(g1, loss_part[0, :1]), grad_modes, "exchange_grads_1_start")
    lw0["ln_g"] = lw0["ln_g"] + grads_1["token"][0, 0]
    grad_x, _, g0 = _layer_bwd(up, None, mems, lw0, saved0)
    parts = [None, _exchange_wait(grads_1, grad_x, "exchange_grads_1_wait")]
    parts[0] = _exchange(grad_slabs(g0, jnp.zeros((1,), F32)), grad_modes, "exchange_grads_0")

    small_names = _SHARDED_SMALL + _REPLICATED
    per_layer = []
    for l in range(L):
        p_w_in, p_w_out, p_w_kv, p_small, p_repl = parts[l]
        out = {}
        for n, p, tr in (("w_in", p_w_in, 256), ("w_out", p_w_out, 32), ("w_kv", p_w_kv, 32)):
            out[n] = _adamw(p, wts[n][l], mom[n][l], var[n][l], "adamw_" + n, tr)
        rows_sh = p_small.shape[1]

        def pack_small(src):
            sh = _pack([src[n][l] for n in _SHARDED_SMALL])
            rp = _pack([src[n][l] for n in _REPLICATED] + [jnp.zeros((1,), F32)])
            return jnp.concatenate([sh, rp], axis=0)

        res = _adamw(jnp.concatenate([p_small, p_repl], axis=1), pack_small(wts), pack_small(mom), pack_small(var),
                     "adamw_small")
        for kind, slab in enumerate(res):
            sh = _unpack(slab[:rows_sh], [wts[n].shape[1:] for n in _SHARDED_SMALL])
            rp = _unpack(slab[rows_sh:], [wts[n].shape[1:] for n in _REPLICATED] + [(1,)])
            if kind == 0 and l == L - 1:
                loss = rp[-1][0]
            for n, t in zip(small_names, sh + rp[:-1]):
                out.setdefault(n, [None] * 4)[kind] = t
        per_layer.append(out)

    def stacked(kind):
        return [jnp.stack([per_layer[l][n][kind] for l in range(L)]) for n in names]

    return (loss, grad_x[None], *stacked(0), *stacked(1), *stacked(2), *stacked(3))
```

```python
import math

import jax
import jax.numpy as jnp
from jax import lax
from jax.experimental import pallas as pl
from jax.experimental.pallas import tpu as pltpu

F32 = jnp.float32
MM_DTYPE = jnp.bfloat16
GRAD_DTYPE = jnp.bfloat16

D_MODEL = 1024
DEPTH = 2
D_G = 256
N_GROUPS = 5
D_MIX = N_GROUPS * D_G
N_SUB = 4
HEAD_DIM = D_G // N_SUB
CONV_A = 3
CONV_D = 31
CHUNK = 128
POOL_WINDOWS = (2, 4, 8, 16)
MEM_LEN = 256
LN_EPS = 1e-5
ALPHA = (2.0 * DEPTH) ** 0.25
D_IN = 9 * D_G + D_MIX
ATT_SCALE = 1.0 / math.sqrt(HEAD_DIM)

ADAM_LR = 0.001
ADAM_B1 = 0.9
ADAM_B2 = 0.999
ADAM_EPS = 1e-08
ADAM_WD = 0.01
ADAM_STEP = 10

N_DEV = 8
W_IN_COLS = D_IN // N_DEV
CONV_CH = D_G // N_DEV
HALO = 32
TILE = 256
VMEM_LIMIT = 56 * 1024 * 1024

C_XA, C_BA, C_CA, C_U, C_V, C_XC, C_DA, C_DG, C_Q = range(9)
C_GATE = 9 * D_G

G256_ROWS = ("sg_ln_g", "sg_ln_b", "pool_scale", "cc_dw_b", "cc_ln_g", "cc_ln_b")
G1024_ROWS = ("ln_g", "ln_b")
LOSS_ROW = 2


def _mm(a, b):
    return jnp.dot(a.astype(MM_DTYPE), b.astype(MM_DTYPE), preferred_element_type=F32)


def _mm_nt(a, b):
    return lax.dot_general(a.astype(MM_DTYPE), b.astype(MM_DTYPE), (((1,), (1,)), ((), ())),
                           preferred_element_type=F32)


def _mm_tn(a, b):
    return lax.dot_general(a.astype(MM_DTYPE), b.astype(MM_DTYPE), (((0,), (0,)), ((), ())),
                           preferred_element_type=F32)


def _sigmoid(x):
    return 1.0 / (1.0 + jnp.exp(-x))


_GELU_C = math.sqrt(2.0 / math.pi)
_GELU_A = 0.044715


def _gelu(x):
    th = jnp.tanh(_GELU_C * (x + _GELU_A * (x * x * x)))
    return 0.5 * x * (1.0 + th), th


def _dgelu(x, th):
    return 0.5 * (1.0 + th) + 0.5 * x * (1.0 - th * th) * (_GELU_C * (1.0 + 3.0 * _GELU_A * (x * x)))


def _ln_fwd(x, g, b):
    mu = jnp.mean(x, axis=-1, keepdims=True)
    xc = x - mu
    var = jnp.mean(xc * xc, axis=-1, keepdims=True)
    rstd = lax.rsqrt(var + LN_EPS)
    xhat = xc * rstd
    return xhat * g + b, xhat, rstd


def _ln_bwd(dy, xhat, rstd, g):
    dxhat = dy * g
    m1 = jnp.mean(dxhat, axis=-1, keepdims=True)
    m2 = jnp.mean(dxhat * xhat, axis=-1, keepdims=True)
    return rstd * (dxhat - m1 - xhat * m2)


def _rowsum(x):
    return jnp.sum(x, axis=0, keepdims=True)


def _col(ref, k):
    return ref[:, k * D_G:(k + 1) * D_G]


def _head_of_lane(shape):
    return jnp.right_shift(lax.broadcasted_iota(jnp.int32, shape, len(shape) - 1), HEAD_DIM.bit_length() - 1)


def _pool_select(lane_grp, s2, s4, s8, s16):
    return jnp.where(lane_grp == 0, s2, jnp.where(lane_grp == 1, s4, jnp.where(lane_grp == 2, s8, s16)))


def _row_view(ref, layer):
    return ref.at[pl.ds(layer, 1)]


def _assemble_cols(dst_ref, src_ref):
    cols = src_ref.shape[2]
    for p in range(N_DEV):
        dst_ref[:, p * cols:(p + 1) * cols] = src_ref[p]


def _branch_forward(p_ref, ph_ref, first, row0, km_ref, vm_ref, w, ext_a, ext_c, ext_d, tile):
    r = {}
    xa, ba, ca = _col(p_ref, C_XA), _col(p_ref, C_BA), _col(p_ref, C_CA)
    ext_a[0:HALO] = jnp.where(first, 0.0, _col(ph_ref, C_CA) * _col(ph_ref, C_XA))
    ext_a[HALO:HALO + tile] = ca * xa
    conv_a = w["conv_a"][0:1, :] * ext_a[pl.ds(HALO - 2, tile), :]
    for k in range(1, CONV_A):
        conv_a = conv_a + w["conv_a"][k:k + 1, :] * ext_a[pl.ds(HALO - 2 + k, tile), :]
    r.update(xa=xa, ba=ba, ca=ca, conv_a=conv_a)
    ya = ba * conv_a

    xc = _col(p_ref, C_XC)
    ext_c[0:HALO] = jnp.where(first, 0.0, _col(ph_ref, C_XC))
    ext_c[HALO:HALO + tile] = xc
    acc = xc
    sums = {}
    for k in range(1, POOL_WINDOWS[-1]):
        acc = acc + ext_c[pl.ds(HALO - k, tile), :]
        if k + 1 in POOL_WINDOWS:
            sums[k + 1] = acc
    lane_grp = _head_of_lane((tile, D_G))
    trow = row0 + lax.broadcasted_iota(jnp.int32, (tile, D_G), 0)
    win = _pool_select(lane_grp, 2, 4, 8, 16)
    cnt = jnp.minimum(trow + 1, win).astype(F32)
    ypre = _pool_select(lane_grp, sums[2], sums[4], sums[8], sums[16]) / cnt - xc
    pool_mm = _mm(ypre, w["pool_wbd"][...])
    yc = pool_mm * w["pool_scale"][...]
    r.update(lane_grp=lane_grp, cnt=cnt, ypre=ypre, pool_mm=pool_mm)

    da, dg = _col(p_ref, C_DA), _col(p_ref, C_DG)
    sig_dg = _sigmoid(dg)
    ext_d[0:HALO] = jnp.where(first, 0.0, _col(ph_ref, C_DA) * _sigmoid(_col(ph_ref, C_DG)))
    ext_d[HALO:HALO + tile] = da * sig_dg
    conv_d = w["cc_dw_b"][...] + w["cc_dw_w"][0:1, :] * ext_d[pl.ds(HALO - (CONV_D - 1), tile), :]
    for j in range(1, CONV_D):
        conv_d = conv_d + w["cc_dw_w"][j:j + 1, :] * ext_d[pl.ds(HALO - (CONV_D - 1) + j, tile), :]
    ln_d, xhat_d, rstd_d = _ln_fwd(conv_d, w["cc_ln_g"][...], w["cc_ln_b"][...])
    sig_ln = _sigmoid(ln_d)
    act_d = ln_d * sig_ln
    yd = _mm(act_d, w["cc_pw_w"][...])
    r.update(da=da, sig_dg=sig_dg, ln_d=ln_d, xhat_d=xhat_d, rstd_d=rstd_d, sig_ln=sig_ln, act_d=act_d)

    u, v = _col(p_ref, C_U), _col(p_ref, C_V)
    ug, th_u = _gelu(u)
    vg, th_v = _gelu(v)
    vn, xhat_v, rstd_v = _ln_fwd(vg, w["sg_ln_g"][...], w["sg_ln_b"][...])
    tri = (lax.broadcasted_iota(jnp.int32, (CHUNK, CHUNK), 0)
           >= lax.broadcasted_iota(jnp.int32, (CHUNK, CHUNK), 1))
    wm = [jnp.where(tri, w["sg_w"][h], 0.0).astype(MM_DTYPE) for h in range(N_SUB)]
    lo = lax.broadcasted_iota(jnp.int32, (CHUNK, 2 * HEAD_DIM), 1) < HEAD_DIM
    chunks = []
    for c in range(tile // CHUNK):
        halves = []
        for hf in range(2):
            vh = vn[c * CHUNK:(c + 1) * CHUNK, hf * 128:(hf + 1) * 128]
            halves.append(_mm(wm[2 * hf], jnp.where(lo, vh, 0.0)) + _mm(wm[2 * hf + 1], jnp.where(lo, 0.0, vh)))
        chunks.append(jnp.concatenate(halves, axis=1) + w["sg_bias"][...])
    mixed = jnp.concatenate(chunks, axis=0)
    yb = ug * mixed
    r.update(u=u, v=v, ug=ug, th_u=th_u, th_v=th_v, vn=vn, xhat_v=xhat_v, rstd_v=rstd_v, wm=wm, lo=lo,
             mixed=mixed, tri=tri)

    q = _col(p_ref, C_Q)
    ye = jnp.zeros((tile, D_G), F32)
    probs = []
    for h in range(N_SUB):
        s = _mm_nt(q, km_ref[h]) * ATT_SCALE
        e = jnp.exp(s - jnp.max(s, axis=-1, keepdims=True))
        p = e / jnp.sum(e, axis=-1, keepdims=True)
        probs.append(p)
        ye = ye + _mm(p, vm_ref[h])
    r.update(q=q, probs=probs)

    gate = p_ref[:, C_GATE:C_GATE + D_MIX]
    sig_gate = _sigmoid(gate)
    concat = jnp.concatenate([ya, yb, yc, yd, ye], axis=1)
    r.update(gate=gate, sig_gate=sig_gate, concat=concat)
    return r


_BRANCH_REPL = ("sg_ln_g", "sg_ln_b", "sg_w", "sg_b", "pool_w", "pool_scale", "cc_dw_b", "cc_ln_g", "cc_ln_b")
_BRANCH_W_SCRATCH = (("conv_a", (CONV_A, D_G)), ("cc_dw_w", (CONV_D, D_G)), ("sg_bias", (CHUNK, D_G)),
                     ("pool_wbd", (D_G, D_G)), ("sgb8", (8, CHUNK)))


def _branch_weights(layer, nat, pw_ref, ca_ref, dw_ref, scr, init):
    @pl.when(init)
    def _():
        for p in range(N_DEV):
            scr["conv_a"][:, p * CONV_CH:(p + 1) * CONV_CH] = ca_ref[p, layer]
            scr["cc_dw_w"][:, p * CONV_CH:(p + 1) * CONV_CH] = dw_ref[p, layer]
        scr["sgb8"][...] = jnp.zeros((8, CHUNK), F32)
        scr["sgb8"][0:N_SUB] = nat["sg_b"][layer]
        sgb_t = scr["sgb8"][...].T
        head = _head_of_lane((CHUNK, D_G))
        bias = jnp.zeros((CHUNK, D_G), F32)
        for h in range(N_SUB):
            bias = jnp.where(head == h, sgb_t[:, h:h + 1], bias)
        scr["sg_bias"][...] = bias
        scr["pool_wbd"][...] = jnp.zeros((D_G, D_G), F32)
        for gi in range(N_SUB):
            sl = slice(gi * HEAD_DIM, (gi + 1) * HEAD_DIM)
            scr["pool_wbd"][sl, sl] = nat["pool_w"][layer, gi]

    w = {n: _row_view(nat[n], layer) for n in ("sg_ln_g", "sg_ln_b", "pool_scale", "cc_dw_b", "cc_ln_g", "cc_ln_b")}
    w.update(conv_a=scr["conv_a"], cc_dw_w=scr["cc_dw_w"], sg_bias=scr["sg_bias"], pool_wbd=scr["pool_wbd"],
             sg_w=nat["sg_w"].at[layer], cc_pw_w=pw_ref)
    return w


def _full_spec(a):
    nd = a.ndim
    return pl.BlockSpec(a.shape, lambda *_, _nd=nd: (0,) * _nd)


def _tie_specs(ties):
    return [pl.BlockSpec((8, 128), lambda *_: (0, 0)) for _ in ties]


def _params(*sem):
    return pltpu.CompilerParams(dimension_semantics=sem or None, vmem_limit_bytes=VMEM_LIMIT)


def _proj_matmul(x, wi8, ties=(), tm=256):
    s, k = x.shape
    nt = len(ties)

    def body(x_ref, w8_ref, *rest):
        o_ref, w_full = rest[nt:]

        @pl.when(pl.program_id(0) == 0)
        def _():
            _assemble_cols(w_full, w8_ref)

        o_ref[...] = _mm(x_ref[...], w_full[...])

    return pl.pallas_call(
        body, grid=(s // tm,),
        in_specs=[pl.BlockSpec((tm, k), lambda i: (i, 0)), _full_spec(wi8)] + _tie_specs(ties),
        out_specs=pl.BlockSpec((tm, D_IN), lambda i: (i, 0)),
        out_shape=jax.ShapeDtypeStruct((s, D_IN), F32),
        scratch_shapes=[pltpu.VMEM((k, D_IN), wi8.dtype)], name="proj_mm", compiler_params=_params("arbitrary"),
    )(x, wi8, *ties)


def _kv_project(mem, w_kv):
    def body(mem_ref, w_ref, km_ref, vm_ref):
        kv = _mm(mem_ref[...], w_ref[...])
        k, v = kv[:, :D_G], kv[:, D_G:]
        grp = _head_of_lane((MEM_LEN, D_G))
        for h in range(N_SUB):
            km_ref[h] = jnp.where(grp == h, k, 0.0).astype(km_ref.dtype)
            vm_ref[h] = jnp.where(grp == h, v, 0.0).astype(vm_ref.dtype)

    shp = jax.ShapeDtypeStruct((N_SUB, MEM_LEN, D_G), MM_DTYPE)
    return pl.pallas_call(body, out_shape=(shp, shp), name="kv_project", compiler_params=_params())(mem, w_kv)


def _branch_fwd(proj, km, vm, layer, repl, pw, ca8, dw8, ties=(), tile=TILE):
    s = proj.shape[0]
    hb = tile // HALO
    nat_arrays = [repl[n] for n in _BRANCH_REPL]
    n_nat, nt = len(nat_arrays), len(ties)

    def body(p_ref, ph_ref, km_ref, vm_ref, *rest):
        nat = dict(zip(_BRANCH_REPL, rest[:n_nat]))
        pw_ref, ca_ref, dw_ref = rest[n_nat:n_nat + 3]
        h_ref, ext_a, ext_c, ext_d = rest[n_nat + 3 + nt:n_nat + 7 + nt]
        scr = dict(zip([n for n, _ in _BRANCH_W_SCRATCH], rest[n_nat + 7 + nt:]))
        i = pl.program_id(0)
        w = _branch_weights(layer, nat, pw_ref, ca_ref, dw_ref, scr, i == 0)
        r = _branch_forward(p_ref, ph_ref, i == 0, i * tile, km_ref, vm_ref, w, ext_a, ext_c, ext_d, tile)
        h_ref[...] = (r["concat"] * (r["gate"] * r["sig_gate"])).astype(h_ref.dtype)

    return pl.pallas_call(
        body, grid=(s // tile,),
        in_specs=[pl.BlockSpec((tile, D_IN), lambda i: (i, 0)),
                  pl.BlockSpec((HALO, D_IN), lambda i: (jnp.maximum(i * hb - 1, 0), 0)),
                  _full_spec(km), _full_spec(vm)] + [_full_spec(a) for a in nat_arrays + [pw, ca8, dw8]]
        + _tie_specs(ties),
        out_specs=pl.BlockSpec((tile, D_MIX), lambda i: (i, 0)),
        out_shape=jax.ShapeDtypeStruct((s, D_MIX), MM_DTYPE),
        scratch_shapes=[pltpu.VMEM((HALO + tile, D_G), F32)] * 3
        + [pltpu.VMEM(shape, F32) for _, shape in _BRANCH_W_SCRATCH],
        name="branch_fwd", compiler_params=_params("arbitrary"),
    )(proj, proj, km, vm, *nat_arrays, pw, ca8, dw8, *ties)


def _out_matmul_ln(h, w_out, x, ln_g, ln_b, layer, tm=256):
    s = h.shape[0]

    def body(h_ref, w_ref, x_ref, g_ref, b_ref, z_ref, xn_ref):
        z = ALPHA * x_ref[...] + _mm(h_ref[...], w_ref[...])
        z_ref[...] = z
        xn_ref[...] = _ln_fwd(z, _row_view(g_ref, layer)[...], _row_view(b_ref, layer)[...])[0]

    row = lambda i: (i, 0)
    shp = jax.ShapeDtypeStruct((s, D_MODEL), F32)
    return pl.pallas_call(
        body, grid=(s // tm,),
        in_specs=[pl.BlockSpec((tm, D_MIX), row), _full_spec(w_out), pl.BlockSpec((tm, D_MODEL), row),
                  _full_spec(ln_g), _full_spec(ln_b)],
        out_specs=(pl.BlockSpec((tm, D_MODEL), row), pl.BlockSpec((tm, D_MODEL), row)),
        out_shape=(shp, shp), name="out_mm_ln", compiler_params=_params("arbitrary"),
    )(h, w_out, x, ln_g, ln_b)


def _ln_out_bwd(up, target, z, ln_g, ln_b, layer, w_out, h, ties=(), tm=256):
    s = z.shape[0]
    from_loss = target is not None
    other = target if from_loss else up
    nt = len(ties)

    def body(o_ref, z_ref, g_ref, b_ref, w_ref, h_ref, *rest):
        dz_ref, dh_ref, gw_ref, slab_ref, gw_acc, lacc = rest[nt:]
        i = pl.program_id(0)

        @pl.when(i == 0)
        def _():
            gw_acc[...] = jnp.zeros_like(gw_acc)
            slab_ref[...] = jnp.zeros_like(slab_ref)
            lacc[...] = jnp.zeros_like(lacc)

        g = _row_view(g_ref, layer)[...]
        xn, xhat, rstd = _ln_fwd(z_ref[...], g, _row_view(b_ref, layer)[...])
        if from_loss:
            err = xn - o_ref[...]
            lacc[...] += _rowsum(err * err)
            dxn = err * (1.0 / D_MODEL)
        else:
            dxn = o_ref[...]
        slab_ref[0:1, :] += _rowsum(dxn * xhat)
        slab_ref[1:2, :] += _rowsum(dxn)
        dz = _ln_bwd(dxn, xhat, rstd, g)
        dz_ref[...] = dz
        dh_ref[...] = _mm_nt(dz, w_ref[...])
        gw_acc[...] += _mm_tn(h_ref[...], dz)

        @pl.when(i == pl.num_programs(0) - 1)
        def _():
            gw_ref[...] = gw_acc[...].astype(gw_ref.dtype)
            if from_loss:
                total = jnp.sum(lacc[...], axis=-1, keepdims=True) * (0.5 / D_MODEL)
                slab_ref[LOSS_ROW:LOSS_ROW + 1, :] = jnp.broadcast_to(total, (1, D_MODEL))

    row = lambda i: (i, 0)
    fixed = lambda i: (0, 0)
    return pl.pallas_call(
        body, grid=(s // tm,),
        in_specs=[pl.BlockSpec((tm, D_MODEL), row), pl.BlockSpec((tm, D_MODEL), row), _full_spec(ln_g),
                  _full_spec(ln_b), _full_spec(w_out), pl.BlockSpec((tm, D_MIX), row)] + _tie_specs(ties),
        out_specs=(pl.BlockSpec((tm, D_MODEL), row), pl.BlockSpec((tm, D_MIX), row),
                   pl.BlockSpec((D_MIX, D_MODEL), fixed), pl.BlockSpec((8, D_MODEL), fixed)),
        out_shape=(jax.ShapeDtypeStruct((s, D_MODEL), F32), jax.ShapeDtypeStruct((s, D_MIX), F32),
                   jax.ShapeDtypeStruct((D_MIX, D_MODEL), GRAD_DTYPE), jax.ShapeDtypeStruct((8, D_MODEL), F32)),
        scratch_shapes=[pltpu.VMEM((D_MIX, D_MODEL), F32), pltpu.VMEM((1, D_MODEL), F32)],
        name="ln_out_bwd_loss" if from_loss else "ln_out_bwd", compiler_params=_params("arbitrary"),
    )(other, z, ln_g, ln_b, w_out, h, *ties)


_BRANCH_GRADS = (("g256", (8, D_G)), ("sg_w", (N_SUB, CHUNK, CHUNK)), ("sg_b", (8, CHUNK)),
                 ("pool_w", (N_SUB, HEAD_DIM, HEAD_DIM)), ("conv_a_w", (N_DEV, CONV_A, CONV_CH)),
                 ("cc_dw_w", (N_DEV, CONV_D, CONV_CH)), ("cc_pw_w", (D_G, D_G)),
                 ("dk", (N_SUB, MEM_LEN, D_G)), ("dv", (N_SUB, MEM_LEN, D_G)))
_BRANCH_ACC = (("conv_a", (CONV_A, D_G)), ("cc_dw_w", (CONV_D, D_G)), ("pool_wbd", (D_G, D_G)),
               ("sg_bias", (CHUNK, D_G)))


def _branch_bwd(proj, dh, km, vm, layer, repl, pw, ca8, dw8, tile=TILE):
    s = proj.shape[0]
    nt = s // tile
    hb = tile // HALO
    nat_arrays = [repl[n] for n in _BRANCH_REPL]
    n_nat, n_grads, n_acc = len(nat_arrays), len(_BRANCH_GRADS), len(_BRANCH_ACC)
    row_of = {n: k for k, n in enumerate(G256_ROWS)}

    def body(p_ref, ph_ref, dh_ref, km_ref, vm_ref, *rest):
        nat = dict(zip(_BRANCH_REPL, rest[:n_nat]))
        pw_ref, ca_ref, dw_ref = rest[n_nat:n_nat + 3]
        rest = rest[n_nat + 3:]
        dp_ref = rest[0]
        g = dict(zip([n for n, _ in _BRANCH_GRADS], rest[1:1 + n_grads]))
        rest = rest[1 + n_grads:]
        ext_a, ext_c, ext_d, rev_a, rev_c, rev_d = rest[:6]
        acc = dict(zip([n for n, _ in _BRANCH_ACC], rest[6:6 + n_acc]))
        scr = dict(zip([n for n, _ in _BRANCH_W_SCRATCH], rest[6 + n_acc:]))
        i = pl.program_id(0)
        t = nt - 1 - i

        @pl.when(i == 0)
        def _():
            for ref in list(g.values()) + list(acc.values()) + [rev_a, rev_c, rev_d]:
                ref[...] = jnp.zeros_like(ref)

        w = _branch_weights(layer, nat, pw_ref, ca_ref, dw_ref, scr, i == 0)
        r = _branch_forward(p_ref, ph_ref, t == 0, t * tile, km_ref, vm_ref, w, ext_a, ext_c, ext_d, tile)

        def put(k, val, width=D_G):
            dp_ref[:, k:k + width] = val.astype(dp_ref.dtype)

        def add_row(name, val):
            k = row_of[name]
            g["g256"][k:k + 1, :] += val

        def push_rev(rev, val):
            head = rev[0:HALO]
            rev[tile:tile + HALO] = head
            rev[0:tile] = val

        dh_all = dh_ref[...]
        gate, sig_gate, concat = r["gate"], r["sig_gate"], r["concat"]
        put(C_GATE, dh_all * concat * (sig_gate * (1.0 + gate * (1.0 - sig_gate))), D_MIX)
        dconcat = dh_all * (gate * sig_gate)
        dya, dyb, dyc, dyd, dye = [dconcat[:, k * D_G:(k + 1) * D_G] for k in range(N_GROUPS)]

        put(C_BA * D_G, dya * r["conv_a"])
        dconv_a = dya * r["ba"]
        for k in range(CONV_A):
            acc["conv_a"][k:k + 1, :] += _rowsum(dconv_a * ext_a[pl.ds(HALO - 2 + k, tile), :])
        push_rev(rev_a, dconv_a)
        dga = w["conv_a"][0:1, :] * rev_a[pl.ds(2, tile), :]
        for k in range(1, CONV_A):
            dga = dga + w["conv_a"][k:k + 1, :] * rev_a[pl.ds(2 - k, tile), :]
        put(C_CA * D_G, dga * r["xa"])
        put(C_XA * D_G, dga * r["ca"])

        add_row("pool_scale", _rowsum(dyc * r["pool_mm"]))
        dmm = dyc * w["pool_scale"][...]
        acc["pool_wbd"][...] += _mm_tn(r["ypre"], dmm)
        dypre = _mm_nt(dmm, w["pool_wbd"][...])
        dws = dypre / r["cnt"]
        push_rev(rev_c, dws)
        run = dws
        sums = {}
        for k in range(1, POOL_WINDOWS[-1]):
            run = run + rev_c[pl.ds(k, tile), :]
            if k + 1 in POOL_WINDOWS:
                sums[k + 1] = run
        put(C_XC * D_G, _pool_select(r["lane_grp"], sums[2], sums[4], sums[8], sums[16]) - dypre)

        g["cc_pw_w"][...] += _mm_tn(r["act_d"], dyd)
        dact = _mm_nt(dyd, w["cc_pw_w"][...])
        sig_ln, ln_d = r["sig_ln"], r["ln_d"]
        dln = dact * (sig_ln * (1.0 + ln_d * (1.0 - sig_ln)))
        add_row("cc_ln_g", _rowsum(dln * r["xhat_d"]))
        add_row("cc_ln_b", _rowsum(dln))
        dconv_d = _ln_bwd(dln, r["xhat_d"], r["rstd_d"], w["cc_ln_g"][...])
        add_row("cc_dw_b", _rowsum(dconv_d))
        for j in range(CONV_D):
            acc["cc_dw_w"][j:j + 1, :] += _rowsum(dconv_d * ext_d[pl.ds(HALO - (CONV_D - 1) + j, tile), :])
        push_rev(rev_d, dconv_d)
        dhd = w["cc_dw_w"][0:1, :] * rev_d[pl.ds(CONV_D - 1, tile), :]
        for j in range(1, CONV_D):
            dhd = dhd + w["cc_dw_w"][j:j + 1, :] * rev_d[pl.ds(CONV_D - 1 - j, tile), :]
        sig_dg = r["sig_dg"]
        put(C_DA * D_G, dhd * sig_dg)
        put(C_DG * D_G, dhd * r["da"] * sig_dg * (1.0 - sig_dg))

        dug = dyb * r["mixed"]
        dmixed = dyb * r["ug"]
        wm, lo, vn = r["wm"], r["lo"], r["vn"]
        dvn_chunks = []
        for c in range(tile // CHUNK):
            rows = slice(c * CHUNK, (c + 1) * CHUNK)
            acc["sg_bias"][...] += dmixed[rows, :]
            halves = []
            for hf in range(2):
                cols = slice(hf * 128, (hf + 1) * 128)
                dm = dmixed[rows, cols]
                dm_a, dm_b = jnp.where(lo, dm, 0.0), jnp.where(lo, 0.0, dm)
                vh = vn[rows, cols]
                g["sg_w"][2 * hf] += _mm_nt(dm_a, vh)
                g["sg_w"][2 * hf + 1] += _mm_nt(dm_b, vh)
                halves.append(_mm_tn(wm[2 * hf], dm_a) + _mm_tn(wm[2 * hf + 1], dm_b))
            dvn_chunks.append(jnp.concatenate(halves, axis=1))
        dvn = jnp.concatenate(dvn_chunks, axis=0)
        add_row("sg_ln_g", _rowsum(dvn * r["xhat_v"]))
        add_row("sg_ln_b", _rowsum(dvn))
        dvg = _ln_bwd(dvn, r["xhat_v"], r["rstd_v"], w["sg_ln_g"][...])
        put(C_V * D_G, dvg * _dgelu(r["v"], r["th_v"]))
        put(C_U * D_G, dug * _dgelu(r["u"], r["th_u"]))

        q = r["q"]
        dq = jnp.zeros((tile, D_G), F32)
        for h in range(N_SUB):
            p = r["probs"][h]
            dp = _mm_nt(dye, vm_ref[h])
            g["dv"][h] += _mm_tn(p, dye)
            ds = p * (dp - jnp.sum(dp * p, axis=-1, keepdims=True)) * ATT_SCALE
            dq = dq + _mm(ds, km_ref[h])
            g["dk"][h] += _mm_tn(ds, q)
        put(C_Q * D_G, dq)

        @pl.when(i == nt - 1)
        def _():
            for h in range(N_SUB):
                g["sg_w"][h] = jnp.where(r["tri"], g["sg_w"][h], 0.0)
            lane_head = _head_of_lane((CHUNK, D_G))
            col_of = lax.broadcasted_iota(jnp.int32, (CHUNK, 8), 1)
            ba = acc["sg_bias"][...]
            sgb_t = jnp.zeros((CHUNK, 8), F32)
            for h in range(N_SUB):
                col = jnp.sum(jnp.where(lane_head == h, ba, 0.0), axis=-1, keepdims=True)
                sgb_t = jnp.where(col_of == h, col, sgb_t)
            g["sg_b"][...] = sgb_t.T
            wbd = acc["pool_wbd"][...]
            for gi in range(N_SUB):
                sl = slice(gi * HEAD_DIM, (gi + 1) * HEAD_DIM)
                g["pool_w"][gi] = wbd[sl, sl]
            ca, dw = acc["conv_a"][...], acc["cc_dw_w"][...]
            for p in range(N_DEV):
                g["conv_a_w"][p] = ca[:, p * CONV_CH:(p + 1) * CONV_CH]
                g["cc_dw_w"][p] = dw[:, p * CONV_CH:(p + 1) * CONV_CH]

    rev = lambda i: (nt - 1 - i, 0)
    grad_specs = tuple(pl.BlockSpec(shape, lambda i, _nd=len(shape): (0,) * _nd) for _, shape in _BRANCH_GRADS)
    grad_shapes = tuple(jax.ShapeDtypeStruct(shape, F32) for _, shape in _BRANCH_GRADS)
    outs = pl.pallas_call(
        body, grid=(nt,),
        in_specs=[pl.BlockSpec((tile, D_IN), rev),
                  pl.BlockSpec((HALO, D_IN), lambda i: (jnp.maximum((nt - 1 - i) * hb - 1, 0), 0)),
                  pl.BlockSpec((tile, D_MIX), rev), _full_spec(km), _full_spec(vm)]
        + [_full_spec(a) for a in nat_arrays + [pw, ca8, dw8]],
        out_specs=(pl.BlockSpec((tile, D_IN), rev),) + grad_specs,
        out_shape=(jax.ShapeDtypeStruct((s, D_IN), MM_DTYPE),) + grad_shapes,
        scratch_shapes=[pltpu.VMEM((HALO + tile, D_G), F32)] * 6
        + [pltpu.VMEM(shape, F32) for _, shape in _BRANCH_ACC + _BRANCH_W_SCRATCH],
        name="branch_bwd", compiler_params=_params("arbitrary"),
    )(proj, proj, dh, km, vm, *nat_arrays, pw, ca8, dw8)
    return outs[0], dict(zip([n for n, _ in _BRANCH_GRADS], outs[1:]))


def _dx_matmul(dproj, wi8, dz, ties=(), tm=256):
    s = dproj.shape[0]
    nt = len(ties)

    def body(dp_ref, w8_ref, dz_ref, *rest):
        o_ref, w_full = rest[nt:]

        @pl.when(pl.program_id(0) == 0)
        def _():
            _assemble_cols(w_full, w8_ref)

        o_ref[...] = _mm_nt(dp_ref[...], w_full[...]) + ALPHA * dz_ref[...]

    row = lambda i: (i, 0)
    return pl.pallas_call(
        body, grid=(s // tm,),
        in_specs=[pl.BlockSpec((tm, D_IN), row), _full_spec(wi8), pl.BlockSpec((tm, D_MODEL), row)]
        + _tie_specs(ties),
        out_specs=pl.BlockSpec((tm, D_MODEL), row),
        out_shape=jax.ShapeDtypeStruct((s, D_MODEL), F32),
        scratch_shapes=[pltpu.VMEM((D_MODEL, D_IN), wi8.dtype)], name="dx_mm", compiler_params=_params("arbitrary"),
    )(dproj, wi8, dz, *ties)


def _dw_in_matmul(x, dproj, ties=(), tk=512):
    s = x.shape[0]
    tn = 2 * W_IN_COLS
    nk = s // tk

    def body(x_ref, dp_ref, *rest):
        o_ref, acc = rest[len(ties):]
        k = pl.program_id(1)

        @pl.when(k == 0)
        def _():
            acc[...] = jnp.zeros_like(acc)

        acc[...] += _mm_tn(x_ref[...], dp_ref[...])

        @pl.when(k == nk - 1)
        def _():
            a = acc[...]
            o_ref[0] = a[:, :W_IN_COLS].astype(o_ref.dtype)
            o_ref[1] = a[:, W_IN_COLS:].astype(o_ref.dtype)

    return pl.pallas_call(
        body, grid=(D_IN // tn, nk),
        in_specs=[pl.BlockSpec((tk, D_MODEL), lambda j, k: (k, 0)), pl.BlockSpec((tk, tn), lambda j, k: (k, j))]
        + _tie_specs(ties),
        out_specs=pl.BlockSpec((2, D_MODEL, W_IN_COLS), lambda j, k: (j, 0, 0)),
        out_shape=jax.ShapeDtypeStruct((N_DEV, D_MODEL, W_IN_COLS), GRAD_DTYPE),
        scratch_shapes=[pltpu.VMEM((D_MODEL, tn), F32)], name="dw_in_mm",
        compiler_params=_params("arbitrary", "arbitrary"),
    )(x, dproj, *ties)


def _kv_bwd(mem, dk, dv):
    def body(mem_ref, dk_ref, dv_ref, o_ref):
        grp = _head_of_lane((MEM_LEN, D_G))
        dk_sum = jnp.zeros((MEM_LEN, D_G), F32)
        dv_sum = jnp.zeros((MEM_LEN, D_G), F32)
        for h in range(N_SUB):
            dk_sum = dk_sum + jnp.where(grp == h, dk_ref[h], 0.0)
            dv_sum = dv_sum + jnp.where(grp == h, dv_ref[h], 0.0)
        o_ref[...] = _mm_tn(mem_ref[...], jnp.concatenate([dk_sum, dv_sum], axis=1)).astype(o_ref.dtype)

    return pl.pallas_call(body, out_shape=jax.ShapeDtypeStruct((D_MODEL, 2 * D_G), GRAD_DTYPE), name="kv_bwd",
                          compiler_params=_params())(mem, dk, dv)


def _layer_fwd_a(x, mem, gw, ties=()):
    km, vm = _kv_project(mem, gw["w_kv"])
    return km, vm, _proj_matmul(x, gw["wi8"], ties)


def _layer_fwd_b(x, proj, km, vm, layer, repl, gw, ties=()):
    h = _branch_fwd(proj, km, vm, layer, repl, gw["pw"], gw["ca8"], gw["dw8"], ties)
    z, xn = _out_matmul_ln(h, gw["w_out"], x, repl["ln_g"], repl["ln_b"], layer)
    return xn, (x, proj, h, z, km, vm)


def _layer_bwd_a(up, target, mem, layer, repl, gw, saved, ties=()):
    x_in, proj, h, z, km, vm = saved
    dz, dh, g_w_out, g1024 = _ln_out_bwd(up, target, z, repl["ln_g"], repl["ln_b"], layer, gw["w_out"], h, ties)
    dproj, bg = _branch_bwd(proj, dh, km, vm, layer, repl, gw["pw"], gw["ca8"], gw["dw8"])
    grads = {n: bg[n] for n in ("g256", "sg_w", "sg_b", "pool_w", "conv_a_w", "cc_dw_w")}
    grads.update(w_out=g_w_out.reshape(N_DEV, D_MIX // N_DEV, D_MODEL), g1024=g1024,
                 w_kv=_kv_bwd(mem, bg["dk"], bg["dv"]).reshape(N_DEV, D_MODEL // N_DEV, 2 * D_G),
                 cc_pw_w=bg["cc_pw_w"].reshape(N_DEV, CONV_CH, D_G))
    return dz, dproj, grads


def _landing_shapes(items):
    out = []
    for a, scatter, pick in items:
        shape = a.shape if scatter else (N_DEV,) + (a.shape if pick is None else a.shape[1:])
        out.append(jax.ShapeDtypeStruct(shape, a.dtype))
    return tuple(out)


def _exchange_sems(n):
    return [pltpu.SemaphoreType.DMA(((N_DEV - 1) * n,)), pltpu.SemaphoreType.DMA(((N_DEV - 1) * n,)),
            pltpu.SemaphoreType.DMA((n,))]


def _exchange_copies(modes, ins, outs, send_sems, recv_sems, local_sems):
    n = len(ins)
    x, y, c = lax.axis_index("x"), lax.axis_index("y"), lax.axis_index("c")
    me = 4 * x + 2 * y + c

    def src_of(a, dest):
        scatter, pick = modes[a]
        if scatter:
            return ins[a].at[dest]
        return ins[a] if pick is None else ins[a].at[pick]

    local = [pltpu.make_async_copy(src_of(a, me), outs[a].at[me], local_sems.at[a]) for a in range(n)]
    sends, recvs = [], []
    for k in range(1, N_DEV):
        px = 1 - x if k & 4 else x
        py = 1 - y if k & 2 else y
        pc = 1 - c if k & 1 else c
        peer = 4 * px + 2 * py + pc
        for a in range(n):
            sems = dict(send_sem=send_sems.at[(k - 1) * n + a], recv_sem=recv_sems.at[(k - 1) * n + a],
                        device_id=(px, py, pc), device_id_type=pl.DeviceIdType.MESH)
            sends.append(pltpu.make_async_remote_copy(src_ref=src_of(a, peer), dst_ref=outs[a].at[me], **sems))
            recvs.append(pltpu.make_async_remote_copy(src_ref=src_of(a, peer), dst_ref=outs[a].at[peer], **sems))
    return local, sends, recvs


def _exchange(items, name):
    n = len(items)
    modes = [(scatter, pick) for _, scatter, pick in items]

    def body(*refs):
        local, sends, recvs = _exchange_copies(modes, refs[:n], refs[n:2 * n], *refs[2 * n:])
        for cp in local + sends:
            cp.start()
        for cp in recvs:
            cp.wait_recv()
        for cp in sends:
            cp.wait_send()
        for cp in local:
            cp.wait()

    any_spec = pl.BlockSpec(memory_space=pl.ANY)
    return pl.pallas_call(
        body, in_specs=[any_spec] * n, out_specs=(any_spec,) * n, out_shape=_landing_shapes(items),
        scratch_shapes=_exchange_sems(n), name=name,
    )(*[a for a, _, _ in items])


_HBM_SPEC = pl.BlockSpec(memory_space=pltpu.HBM)
_SEM_SPEC = pl.BlockSpec(memory_space=pltpu.SEMAPHORE)
_SPLIT_PARAMS = pltpu.CompilerParams(has_side_effects=pltpu.SideEffectType.DATAFLOW_SIDE_EFFECTING)


def _exchange_start(items, name):
    n = len(items)
    modes = [(scatter, pick) for _, scatter, pick in items]
    shapes = _landing_shapes(items)
    lands = [pltpu.with_memory_space_constraint(lax.empty(s.shape, s.dtype), pltpu.HBM) for s in shapes]
    srcs = [pltpu.with_memory_space_constraint(a, pltpu.HBM) for a, _, _ in items]

    def body(*refs):
        local, sends, _ = _exchange_copies(modes, refs[:n], refs[n:2 * n], *refs[2 * n:2 * n + 3])
        for cp in local + sends:
            cp.start()
        token = refs[-1]
        token[...] = jnp.zeros_like(token)

    res = pl.pallas_call(
        body, name=name, in_specs=[_HBM_SPEC] * (2 * n),
        out_shape=tuple(_exchange_sems(n)) + tuple(pltpu.HBM(a.shape, a.dtype) for a in srcs)
        + tuple(pltpu.HBM(s.shape, s.dtype) for s in shapes) + (jax.ShapeDtypeStruct((8, 128), F32),),
        out_specs=(_SEM_SPEC,) * 3 + (_HBM_SPEC,) * (2 * n) + (pl.BlockSpec(memory_space=pltpu.VMEM),),
        input_output_aliases={i: 3 + i for i in range(2 * n)}, compiler_params=_SPLIT_PARAMS,
    )(*srcs, *lands)
    return dict(sems=res[:3], srcs=res[3:3 + n], lands=res[3 + n:3 + 2 * n], token=res[-1], modes=modes)


def _exchange_wait(ticket, after, name):
    n = len(ticket["srcs"])
    modes = ticket["modes"]

    def body(*refs):
        local, sends, recvs = _exchange_copies(modes, refs[:n], refs[n:2 * n], *refs[2 * n:2 * n + 3])
        for cp in recvs:
            cp.wait_recv()
        for cp in sends:
            cp.wait_send()
        for cp in local:
            cp.wait()

    both = list(ticket["srcs"]) + list(ticket["lands"])
    res = pl.pallas_call(
        body, name=name, in_specs=[_HBM_SPEC] * (2 * n) + [_SEM_SPEC] * 3 + [pl.BlockSpec(memory_space=pl.ANY)],
        out_shape=tuple(pltpu.HBM(a.shape, a.dtype) for a in both), out_specs=(_HBM_SPEC,) * (2 * n),
        input_output_aliases={i: i for i in range(2 * n)}, compiler_params=_SPLIT_PARAMS,
    )(*both, *ticket["sems"], after)
    return res[n:]


def _adam_math(g, w, m, v):
    m_new = ADAM_B1 * m + (1.0 - ADAM_B1) * g
    v_new = ADAM_B2 * v + (1.0 - ADAM_B2) * (g * g)
    m_hat = m_new / (1.0 - ADAM_B1 ** ADAM_STEP)
    v_hat = v_new / (1.0 - ADAM_B2 ** ADAM_STEP)
    return -ADAM_LR * (m_hat / (jnp.sqrt(v_hat) + ADAM_EPS) + ADAM_WD * w), m_new, v_new


def _adamw_big(parts, w, m, v, layer, prev, name, tr):
    depth, rows, cols = w.shape

    def body(p_ref, w_ref, m_ref, v_ref, *rest):
        g_out, d_out, m_out, v_out = rest[len(prev):]
        g = p_ref[0].astype(F32)
        for q in range(1, N_DEV):
            g = g + p_ref[q].astype(F32)
        d, m_new, v_new = _adam_math(g, w_ref[...], m_ref[...], v_ref[...])
        g_out[...] = g
        d_out[...] = d
        m_out[...] = m_new
        v_out[...] = v_new

    blk = pl.BlockSpec((None, tr, cols), lambda i: (layer, i, 0))
    shp = jax.ShapeDtypeStruct((depth, rows, cols), F32)
    return pl.pallas_call(
        body, grid=(rows // tr,),
        in_specs=[pl.BlockSpec((N_DEV, tr, cols), lambda i: (0, i, 0)), blk, blk, blk]
        + [pl.BlockSpec(memory_space=pl.ANY)] * len(prev),
        out_specs=(blk,) * 4, out_shape=(shp,) * 4,
        input_output_aliases={4 + j: j for j in range(len(prev))},
        name=name, compiler_params=_params("arbitrary"),
    )(parts, w, m, v, *prev)


_SMALL_TENSORS = (("conv_a_w", "conv_a_w", None), ("cc_dw_w", "cc_dw_w", None), ("cc_pw_w", "cc_pw_w", None),
                  ("sg_w", "sg_w", None), ("pool_w", "pool_w", None), ("sg_b", "sg_b", None)) \
    + tuple((n, "g256", k) for k, n in enumerate(G256_ROWS)) + tuple((n, "g1024", k) for k, n in enumerate(G1024_ROWS))
_SMALL_LANDINGS = ("conv_a_w", "cc_dw_w", "cc_pw_w", "sg_w", "pool_w", "sg_b", "g256", "g1024")


def _adamw_small(landings, wts, mom, var):
    names = [n for n, _, _ in _SMALL_TENSORS]
    n_land = DEPTH * len(_SMALL_LANDINGS)
    n_t = len(names)

    def body(*refs):
        land = [dict(zip(_SMALL_LANDINGS, refs[l * len(_SMALL_LANDINGS):(l + 1) * len(_SMALL_LANDINGS)]))
                for l in range(DEPTH)]
        w_refs = dict(zip(names, refs[n_land:n_land + n_t]))
        m_refs = dict(zip(names, refs[n_land + n_t:n_land + 2 * n_t]))
        v_refs = dict(zip(names, refs[n_land + 2 * n_t:n_land + 3 * n_t]))
        outs = refs[n_land + 3 * n_t:]
        out_refs = {n: outs[4 * k:4 * k + 4] for k, n in enumerate(names)}
        loss_ref = outs[4 * n_t]
        for name, key, row in _SMALL_TENSORS:
            for l in range(DEPTH):
                src = land[l][key]
                if row is not None:
                    part = lambda q: src[q, row:row + 1, :]
                    view = lambda ref: ref.at[pl.ds(l, 1)]
                elif name == "sg_b":
                    part = lambda q: src[q, 0:N_SUB, :]
                    view = lambda ref: ref.at[l]
                else:
                    part = lambda q: src[q]
                    view = lambda ref: ref.at[l]
                g = part(0)
                for q in range(1, N_DEV):
                    g = g + part(q)
                d, m_new, v_new = _adam_math(g, view(w_refs[name])[...], view(m_refs[name])[...],
                                             view(v_refs[name])[...])
                for ref, val in zip(out_refs[name], (g, d, m_new, v_new)):
                    view(ref)[...] = val
        src = land[DEPTH - 1]["g1024"]
        loss = src[0, LOSS_ROW:LOSS_ROW + 1, 0:128]
        for q in range(1, N_DEV):
            loss = loss + src[q, LOSS_ROW:LOSS_ROW + 1, 0:128]
        loss_ref[...] = loss

    ins = [landings[l][k] for l in range(DEPTH) for k in _SMALL_LANDINGS] \
        + [src[n] for src in (wts, mom, var) for n in names]
    out_shape = tuple(jax.ShapeDtypeStruct(wts[n].shape, F32) for n in names for _ in range(4)) \
        + (jax.ShapeDtypeStruct((1, 128), F32),)
    res = pl.pallas_call(body, out_shape=out_shape, name="adamw_small", compiler_params=_params())(*ins)
    return {n: res[4 * k:4 * k + 4] for k, n in enumerate(names)}, res[4 * n_t]


_BIG = (("w_in", 256), ("w_out", 32), ("w_kv", 32))
_GRAD_ITEMS_EARLY = ("w_out", "w_kv", "cc_pw_w", "conv_a_w", "cc_dw_w")
_GRAD_ITEMS_REPL = ("g256", "sg_w", "sg_b", "pool_w", "g1024")


def _grad_items(grads, with_w_in):
    items = [(grads[n], True, None) for n in (("w_in",) if with_w_in else ()) + _GRAD_ITEMS_EARLY]
    return items + [(grads[n], False, None) for n in _GRAD_ITEMS_REPL]


def _landed(parts, with_w_in):
    names = (("w_in",) if with_w_in else ()) + _GRAD_ITEMS_EARLY + _GRAD_ITEMS_REPL
    return dict(zip(names, parts))


def _gathered_weights(wi8, w_kv8, w_out8, pw8, ca8, dw8):
    return dict(wi8=wi8, w_kv=w_kv8.reshape(D_MODEL, 2 * D_G), w_out=w_out8.reshape(D_MIX, D_MODEL),
                pw=pw8.reshape(D_G, D_G), ca8=ca8, dw8=dw8)


def kernel(x, mem, w_in, conv_a_w, sg_ln_g, sg_ln_b, sg_w, sg_b, pool_w, pool_scale, cc_dw_w, cc_dw_b, cc_ln_g, cc_ln_b, cc_pw_w, w_kv, w_out, ln_g, ln_b, loss_target, m_w_in, m_conv_a_w, m_sg_ln_g, m_sg_ln_b, m_sg_w, m_sg_b, m_pool_w, m_pool_scale, m_cc_dw_w, m_cc_dw_b, m_cc_ln_g, m_cc_ln_b, m_cc_pw_w, m_w_kv, m_w_out, m_ln_g, m_ln_b, v_w_in, v_conv_a_w, v_sg_ln_g, v_sg_ln_b, v_sg_w, v_sg_b, v_pool_w, v_pool_scale, v_cc_dw_w, v_cc_dw_b, v_cc_ln_g, v_cc_ln_b, v_cc_pw_w, v_w_kv, v_w_out, v_ln_g, v_ln_b):
    names = ("w_in", "conv_a_w", "sg_ln_g", "sg_ln_b", "sg_w", "sg_b", "pool_w", "pool_scale", "cc_dw_w", "cc_dw_b",
             "cc_ln_g", "cc_ln_b", "cc_pw_w", "w_kv", "w_out", "ln_g", "ln_b")
    wts = dict(zip(names, (w_in, conv_a_w, sg_ln_g, sg_ln_b, sg_w, sg_b, pool_w, pool_scale, cc_dw_w, cc_dw_b,
                           cc_ln_g, cc_ln_b, cc_pw_w, w_kv, w_out, ln_g, ln_b)))
    mom = dict(zip(names, (m_w_in, m_conv_a_w, m_sg_ln_g, m_sg_ln_b, m_sg_w, m_sg_b, m_pool_w, m_pool_scale,
                           m_cc_dw_w, m_cc_dw_b, m_cc_ln_g, m_cc_ln_b, m_cc_pw_w, m_w_kv, m_w_out, m_ln_g, m_ln_b)))
    var = dict(zip(names, (v_w_in, v_conv_a_w, v_sg_ln_g, v_sg_ln_b, v_sg_w, v_sg_b, v_pool_w, v_pool_scale,
                           v_cc_dw_w, v_cc_dw_b, v_cc_ln_g, v_cc_ln_b, v_cc_pw_w, v_w_kv, v_w_out, v_ln_g, v_ln_b)))
    repl = wts
    xs, mems, tgt = x[0], mem[0], loss_target[0]
    wb = {n: wts[n].astype(MM_DTYPE) for n in ("w_in", "w_kv", "w_out", "cc_pw_w")}

    wi8_0, wkv8_0 = _exchange([(wb["w_in"], False, 0), (wb["w_kv"], False, 0)], "gather_weights_0a")
    rest_0 = _exchange_start([(wb["w_out"], False, 0), (wb["cc_pw_w"], False, 0), (conv_a_w, False, None),
                              (cc_dw_w, False, None)], "gather_weights_0b_start")
    km0, vm0, proj0 = _layer_fwd_a(xs, mems, dict(wi8=wi8_0, w_kv=wkv8_0.reshape(D_MODEL, 2 * D_G)),
                                   (rest_0["token"],))
    wo8_0, pw8_0, ca8, dw8 = _exchange_wait(rest_0, proj0, "gather_weights_0b_wait")
    gw0 = _gathered_weights(wi8_0, wkv8_0, wo8_0, pw8_0, ca8, dw8)
    all_1 = _exchange_start([(wb["w_in"], False, 1), (wb["w_kv"], False, 1), (wb["w_out"], False, 1),
                             (wb["cc_pw_w"], False, 1)], "gather_weights_1_start")
    x1, saved0 = _layer_fwd_b(xs, proj0, km0, vm0, 0, repl, gw0, (all_1["token"],))
    gw1 = _gathered_weights(*_exchange_wait(all_1, x1, "gather_weights_1_wait"), ca8, dw8)
    km1, vm1, proj1 = _layer_fwd_a(x1, mems, gw1)
    _, saved1 = _layer_fwd_b(x1, proj1, km1, vm1, 1, repl, gw1)

    dz1, dproj1, g1 = _layer_bwd_a(None, tgt, mems, 1, repl, gw1, saved1)
    g1["w_in"] = _dw_in_matmul(saved1[0], dproj1)
    up = _dx_matmul(dproj1, gw1["wi8"], dz1)
    grads_1 = _exchange_start(_grad_items(g1, True), "exchange_grads_1_start")
    dz0, dproj0, g0 = _layer_bwd_a(up, None, mems, 0, repl, gw0, saved0, (grads_1["token"],))
    early_0 = _exchange_start(_grad_items(g0, False), "exchange_grads_0a_start")
    g_w_in_0 = _dw_in_matmul(saved0[0], dproj0, (early_0["token"],))
    late_0 = _exchange_start([(g_w_in_0, True, None)], "exchange_grads_0b_start")
    grad_x = _dx_matmul(dproj0, gw0["wi8"], dz0, (late_0["token"],))

    landed = [None, _landed(_exchange_wait(grads_1, grad_x, "exchange_grads_1_wait"), True)]
    big = {}
    for n, tr in _BIG:
        big[n] = _adamw_big(landed[1][n], wts[n], mom[n], var[n], 1, (), "adamw_" + n + "_1", tr)
    landed[0] = _landed(_exchange_wait(early_0, big["w_kv"][0], "exchange_grads_0a_wait"), False)
    for n, tr in _BIG[1:]:
        big[n] = _adamw_big(landed[0][n], wts[n], mom[n], var[n], 0, big[n], "adamw_" + n + "_0", tr)
    (landed[0]["w_in"],) = _exchange_wait(late_0, big["w_kv"][0], "exchange_grads_0b_wait")
    big["w_in"] = _adamw_big(landed[0]["w_in"], wts["w_in"], mom["w_in"], var["w_in"], 0, big["w_in"],
                             "adamw_w_in_0", _BIG[0][1])
    small, loss = _adamw_small(landed, wts, mom, var)

    res = {**small, **big}
    return (loss[0, 0], grad_x[None], *[res[n][0] for n in names], *[res[n][1] for n in names],
            *[res[n][2] for n in names], *[res[n][3] for n in names])
```

```python
import math

import jax
import jax.numpy as jnp
from jax import lax
from jax.experimental import pallas as pl
from jax.experimental.pallas import tpu as pltpu

F32 = jnp.float32
MM_DTYPE = jnp.bfloat16
GRAD_DTYPE = jnp.bfloat16

D_MODEL = 1024
DEPTH = 2
D_G = 256
N_GROUPS = 5
D_MIX = N_GROUPS * D_G
N_SUB = 4
HEAD_DIM = D_G // N_SUB
CONV_A = 3
CONV_D = 31
CHUNK = 128
POOL_WINDOWS = (2, 4, 8, 16)
MEM_LEN = 256
LN_EPS = 1e-5
ALPHA = (2.0 * DEPTH) ** 0.25
D_IN = 9 * D_G + D_MIX
ATT_SCALE = 1.0 / math.sqrt(HEAD_DIM)

ADAM_LR = 0.001
ADAM_B1 = 0.9
ADAM_B2 = 0.999
ADAM_EPS = 1e-08
ADAM_WD = 0.01
ADAM_STEP = 10

N_DEV = 8
W_IN_COLS = D_IN // N_DEV
CONV_CH = D_G // N_DEV
HALO = 32
TILE = 256
VMEM_LIMIT = 56 * 1024 * 1024

C_XA, C_BA, C_CA, C_U, C_V, C_XC, C_DA, C_DG, C_Q = range(9)
C_GATE = 9 * D_G

G256_ROWS = ("sg_ln_g", "sg_ln_b", "pool_scale", "cc_dw_b", "cc_ln_g", "cc_ln_b")
G1024_ROWS = ("ln_g", "ln_b")
LOSS_ROW = 2


def _mm(a, b):
    return jnp.dot(a.astype(MM_DTYPE), b.astype(MM_DTYPE), preferred_element_type=F32)


def _mm_nt(a, b):
    return lax.dot_general(a.astype(MM_DTYPE), b.astype(MM_DTYPE), (((1,), (1,)), ((), ())),
                           preferred_element_type=F32)


def _mm_tn(a, b):
    return lax.dot_general(a.astype(MM_DTYPE), b.astype(MM_DTYPE), (((0,), (0,)), ((), ())),
                           preferred_element_type=F32)


def _sigmoid(x):
    return 1.0 / (1.0 + jnp.exp(-x))


_GELU_C = math.sqrt(2.0 / math.pi)
_GELU_A = 0.044715


def _gelu(x):
    th = jnp.tanh(_GELU_C * (x + _GELU_A * (x * x * x)))
    return 0.5 * x * (1.0 + th), th


def _dgelu(x, th):
    return 0.5 * (1.0 + th) + 0.5 * x * (1.0 - th * th) * (_GELU_C * (1.0 + 3.0 * _GELU_A * (x * x)))


def _ln_fwd(x, g, b):
    mu = jnp.mean(x, axis=-1, keepdims=True)
    xc = x - mu
    var = jnp.mean(xc * xc, axis=-1, keepdims=True)
    rstd = lax.rsqrt(var + LN_EPS)
    xhat = xc * rstd
    return xhat * g + b, xhat, rstd


def _ln_bwd(dy, xhat, rstd, g):
    dxhat = dy * g
    m1 = jnp.mean(dxhat, axis=-1, keepdims=True)
    m2 = jnp.mean(dxhat * xhat, axis=-1, keepdims=True)
    return rstd * (dxhat - m1 - xhat * m2)


def _rowsum(x):
    return jnp.sum(x, axis=0, keepdims=True)


def _col(ref, k):
    return ref[:, k * D_G:(k + 1) * D_G]


def _head_of_lane(shape):
    return jnp.right_shift(lax.broadcasted_iota(jnp.int32, shape, len(shape) - 1), HEAD_DIM.bit_length() - 1)


def _pool_select(lane_grp, s2, s4, s8, s16):
    return jnp.where(lane_grp == 0, s2, jnp.where(lane_grp == 1, s4, jnp.where(lane_grp == 2, s8, s16)))


def _row_view(ref, layer):
    return ref.at[pl.ds(layer, 1)]


def _assemble_cols(dst_ref, src_ref):
    cols = src_ref.shape[2]
    for p in range(N_DEV):
        dst_ref[:, p * cols:(p + 1) * cols] = src_ref[p]


def _make_residues(ext_ref, res_ref):
    rows = res_ref.shape[1]
    for r in range(1, 8):
        res_ref[r - 1] = ext_ref[pl.ds(r, rows), :]


def _rows_at(ext_ref, res_ref, off, tile):
    a, r = divmod(off, 8)
    if r == 0:
        return ext_ref[pl.ds(off, tile), :]
    return res_ref[r - 1, pl.ds(8 * a, tile), :]


def _residue_scratch(tile):
    return pltpu.VMEM((7, HALO + tile - 8, D_G), F32)


def _branch_forward(p_ref, ph_ref, first, row0, km_ref, vm_ref, w, ext_a, ext_c, ext_d, res_c, res_d, tile):
    r = {}
    xa, ba, ca = _col(p_ref, C_XA), _col(p_ref, C_BA), _col(p_ref, C_CA)
    ext_a[0:HALO] = jnp.where(first, 0.0, _col(ph_ref, C_CA) * _col(ph_ref, C_XA))
    ext_a[HALO:HALO + tile] = ca * xa
    conv_a = w["conv_a"][0:1, :] * ext_a[pl.ds(HALO - 2, tile), :]
    for k in range(1, CONV_A):
        conv_a = conv_a + w["conv_a"][k:k + 1, :] * ext_a[pl.ds(HALO - 2 + k, tile), :]
    r.update(xa=xa, ba=ba, ca=ca, conv_a=conv_a)
    ya = ba * conv_a

    xc = _col(p_ref, C_XC)
    ext_c[0:HALO] = jnp.where(first, 0.0, _col(ph_ref, C_XC))
    ext_c[HALO:HALO + tile] = xc
    _make_residues(ext_c, res_c)
    acc = xc
    sums = {}
    for k in range(1, POOL_WINDOWS[-1]):
        acc = acc + _rows_at(ext_c, res_c, HALO - k, tile)
        if k + 1 in POOL_WINDOWS:
            sums[k + 1] = acc
    lane_grp = _head_of_lane((tile, D_G))
    trow = row0 + lax.broadcasted_iota(jnp.int32, (tile, D_G), 0)
    win = _pool_select(lane_grp, 2, 4, 8, 16)
    cnt = jnp.minimum(trow + 1, win).astype(F32)
    ypre = _pool_select(lane_grp, sums[2], sums[4], sums[8], sums[16]) / cnt - xc
    pool_mm = _mm(ypre, w["pool_wbd"][...])
    yc = pool_mm * w["pool_scale"][...]
    r.update(lane_grp=lane_grp, cnt=cnt, ypre=ypre, pool_mm=pool_mm)

    da, dg = _col(p_ref, C_DA), _col(p_ref, C_DG)
    sig_dg = _sigmoid(dg)
    ext_d[0:HALO] = jnp.where(first, 0.0, _col(ph_ref, C_DA) * _sigmoid(_col(ph_ref, C_DG)))
    ext_d[HALO:HALO + tile] = da * sig_dg
    _make_residues(ext_d, res_d)
    conv_d = w["cc_dw_b"][...] + w["cc_dw_w"][0:1, :] * _rows_at(ext_d, res_d, HALO - (CONV_D - 1), tile)
    for j in range(1, CONV_D):
        conv_d = conv_d + w["cc_dw_w"][j:j + 1, :] * _rows_at(ext_d, res_d, HALO - (CONV_D - 1) + j, tile)
    ln_d, xhat_d, rstd_d = _ln_fwd(conv_d, w["cc_ln_g"][...], w["cc_ln_b"][...])
    sig_ln = _sigmoid(ln_d)
    act_d = ln_d * sig_ln
    yd = _mm(act_d, w["cc_pw_w"][...])
    r.update(da=da, sig_dg=sig_dg, ln_d=ln_d, xhat_d=xhat_d, rstd_d=rstd_d, sig_ln=sig_ln, act_d=act_d)

    u, v = _col(p_ref, C_U), _col(p_ref, C_V)
    ug, th_u = _gelu(u)
    vg, th_v = _gelu(v)
    vn, xhat_v, rstd_v = _ln_fwd(vg, w["sg_ln_g"][...], w["sg_ln_b"][...])
    tri = (lax.broadcasted_iota(jnp.int32, (CHUNK, CHUNK), 0)
           >= lax.broadcasted_iota(jnp.int32, (CHUNK, CHUNK), 1))
    wm = [jnp.where(tri, w["sg_w"][h], 0.0).astype(MM_DTYPE) for h in range(N_SUB)]
    lo = lax.broadcasted_iota(jnp.int32, (CHUNK, 2 * HEAD_DIM), 1) < HEAD_DIM
    chunks = []
    for c in range(tile // CHUNK):
        halves = []
        for hf in range(2):
            vh = vn[c * CHUNK:(c + 1) * CHUNK, hf * 128:(hf + 1) * 128]
            halves.append(_mm(wm[2 * hf], jnp.where(lo, vh, 0.0)) + _mm(wm[2 * hf + 1], jnp.where(lo, 0.0, vh)))
        chunks.append(jnp.concatenate(halves, axis=1) + w["sg_bias"][...])
    mixed = jnp.concatenate(chunks, axis=0)
    yb = ug * mixed
    r.update(u=u, v=v, ug=ug, th_u=th_u, th_v=th_v, vn=vn, xhat_v=xhat_v, rstd_v=rstd_v, wm=wm, lo=lo,
             mixed=mixed, tri=tri)

    q = _col(p_ref, C_Q)
    ye = jnp.zeros((tile, D_G), F32)
    probs = []
    for h in range(N_SUB):
        s = _mm_nt(q, km_ref[h]) * ATT_SCALE
        e = jnp.exp(s - jnp.max(s, axis=-1, keepdims=True))
        p = e / jnp.sum(e, axis=-1, keepdims=True)
        probs.append(p)
        ye = ye + _mm(p, vm_ref[h])
    r.update(q=q, probs=probs)

    gate = p_ref[:, C_GATE:C_GATE + D_MIX]
    sig_gate = _sigmoid(gate)
    concat = jnp.concatenate([ya, yb, yc, yd, ye], axis=1)
    r.update(gate=gate, sig_gate=sig_gate, concat=concat)
    return r


_BRANCH_REPL = ("sg_ln_g", "sg_ln_b", "sg_w", "sg_b", "pool_w", "pool_scale", "cc_dw_b", "cc_ln_g", "cc_ln_b")
_BRANCH_W_SCRATCH = (("conv_a", (CONV_A, D_G)), ("cc_dw_w", (CONV_D, D_G)), ("sg_bias", (CHUNK, D_G)),
                     ("pool_wbd", (D_G, D_G)), ("sgb8", (8, CHUNK)))


def _branch_weights(layer, nat, pw_ref, ca_ref, dw_ref, scr, init):
    @pl.when(init)
    def _():
        for p in range(N_DEV):
            scr["conv_a"][:, p * CONV_CH:(p + 1) * CONV_CH] = ca_ref[p, layer]
            scr["cc_dw_w"][:, p * CONV_CH:(p + 1) * CONV_CH] = dw_ref[p, layer]
        scr["sgb8"][...] = jnp.zeros((8, CHUNK), F32)
        scr["sgb8"][0:N_SUB] = nat["sg_b"][layer]
        sgb_t = scr["sgb8"][...].T
        head = _head_of_lane((CHUNK, D_G))
        bias = jnp.zeros((CHUNK, D_G), F32)
        for h in range(N_SUB):
            bias = jnp.where(head == h, sgb_t[:, h:h + 1], bias)
        scr["sg_bias"][...] = bias
        scr["pool_wbd"][...] = jnp.zeros((D_G, D_G), F32)
        for gi in range(N_SUB):
            sl = slice(gi * HEAD_DIM, (gi + 1) * HEAD_DIM)
            scr["pool_wbd"][sl, sl] = nat["pool_w"][layer, gi]

    w = {n: _row_view(nat[n], layer) for n in ("sg_ln_g", "sg_ln_b", "pool_scale", "cc_dw_b", "cc_ln_g", "cc_ln_b")}
    w.update(conv_a=scr["conv_a"], cc_dw_w=scr["cc_dw_w"], sg_bias=scr["sg_bias"], pool_wbd=scr["pool_wbd"],
             sg_w=nat["sg_w"].at[layer], cc_pw_w=pw_ref)
    return w


def _full_spec(a):
    nd = a.ndim
    return pl.BlockSpec(a.shape, lambda *_, _nd=nd: (0,) * _nd)


def _tie_specs(ties):
    return [pl.BlockSpec((8, 128), lambda *_: (0, 0)) for _ in ties]


def _params(*sem):
    return pltpu.CompilerParams(dimension_semantics=sem or None, vmem_limit_bytes=VMEM_LIMIT)


def _proj_matmul(x, wi8, ties=(), tm=256):
    s, k = x.shape
    nt = len(ties)

    def body(x_ref, w8_ref, *rest):
        o_ref, w_full = rest[nt:]

        @pl.when(pl.program_id(0) == 0)
        def _():
            _assemble_cols(w_full, w8_ref)

        o_ref[...] = _mm(x_ref[...], w_full[...])

    return pl.pallas_call(
        body, grid=(s // tm,),
        in_specs=[pl.BlockSpec((tm, k), lambda i: (i, 0)), _full_spec(wi8)] + _tie_specs(ties),
        out_specs=pl.BlockSpec((tm, D_IN), lambda i: (i, 0)),
        out_shape=jax.ShapeDtypeStruct((s, D_IN), F32),
        scratch_shapes=[pltpu.VMEM((k, D_IN), wi8.dtype)], name="proj_mm", compiler_params=_params("arbitrary"),
    )(x, wi8, *ties)


def _kv_project(mem, w_kv):
    def body(mem_ref, w_ref, km_ref, vm_ref):
        kv = _mm(mem_ref[...], w_ref[...])
        k, v = kv[:, :D_G], kv[:, D_G:]
        grp = _head_of_lane((MEM_LEN, D_G))
        for h in range(N_SUB):
            km_ref[h] = jnp.where(grp == h, k, 0.0).astype(km_ref.dtype)
            vm_ref[h] = jnp.where(grp == h, v, 0.0).astype(vm_ref.dtype)

    shp = jax.ShapeDtypeStruct((N_SUB, MEM_LEN, D_G), MM_DTYPE)
    return pl.pallas_call(body, out_shape=(shp, shp), name="kv_project", compiler_params=_params())(mem, w_kv)


def _branch_fwd(proj, km, vm, layer, repl, pw, ca8, dw8, ties=(), tile=TILE):
    s = proj.shape[0]
    hb = tile // HALO
    nat_arrays = [repl[n] for n in _BRANCH_REPL]
    n_nat, nt = len(nat_arrays), len(ties)

    def body(p_ref, ph_ref, km_ref, vm_ref, *rest):
        nat = dict(zip(_BRANCH_REPL, rest[:n_nat]))
        pw_ref, ca_ref, dw_ref = rest[n_nat:n_nat + 3]
        h_ref, ext_a, ext_c, ext_d, res_c, res_d = rest[n_nat + 3 + nt:n_nat + 9 + nt]
        scr = dict(zip([n for n, _ in _BRANCH_W_SCRATCH], rest[n_nat + 9 + nt:]))
        i = pl.program_id(0)
        w = _branch_weights(layer, nat, pw_ref, ca_ref, dw_ref, scr, i == 0)
        r = _branch_forward(p_ref, ph_ref, i == 0, i * tile, km_ref, vm_ref, w, ext_a, ext_c, ext_d, res_c, res_d,
                            tile)
        h_ref[...] = (r["concat"] * (r["gate"] * r["sig_gate"])).astype(h_ref.dtype)

    return pl.pallas_call(
        body, grid=(s // tile,),
        in_specs=[pl.BlockSpec((tile, D_IN), lambda i: (i, 0)),
                  pl.BlockSpec((HALO, D_IN), lambda i: (jnp.maximum(i * hb - 1, 0), 0)),
                  _full_spec(km), _full_spec(vm)] + [_full_spec(a) for a in nat_arrays + [pw, ca8, dw8]]
        + _tie_specs(ties),
        out_specs=pl.BlockSpec((tile, D_MIX), lambda i: (i, 0)),
        out_shape=jax.ShapeDtypeStruct((s, D_MIX), MM_DTYPE),
        scratch_shapes=[pltpu.VMEM((HALO + tile, D_G), F32)] * 3 + [_residue_scratch(tile)] * 2
        + [pltpu.VMEM(shape, F32) for _, shape in _BRANCH_W_SCRATCH],
        name="branch_fwd", compiler_params=_params("arbitrary"),
    )(proj, proj, km, vm, *nat_arrays, pw, ca8, dw8, *ties)


def _out_matmul_ln(h, w_out, x, ln_g, ln_b, layer, tm=256):
    s = h.shape[0]

    def body(h_ref, w_ref, x_ref, g_ref, b_ref, z_ref, xn_ref):
        z = ALPHA * x_ref[...] + _mm(h_ref[...], w_ref[...])
        z_ref[...] = z
        xn_ref[...] = _ln_fwd(z, _row_view(g_ref, layer)[...], _row_view(b_ref, layer)[...])[0]

    row = lambda i: (i, 0)
    shp = jax.ShapeDtypeStruct((s, D_MODEL), F32)
    return pl.pallas_call(
        body, grid=(s // tm,),
        in_specs=[pl.BlockSpec((tm, D_MIX), row), _full_spec(w_out), pl.BlockSpec((tm, D_MODEL), row),
                  _full_spec(ln_g), _full_spec(ln_b)],
        out_specs=(pl.BlockSpec((tm, D_MODEL), row), pl.BlockSpec((tm, D_MODEL), row)),
        out_shape=(shp, shp), name="out_mm_ln", compiler_params=_params("arbitrary"),
    )(h, w_out, x, ln_g, ln_b)


def _ln_out_bwd(up, target, z, ln_g, ln_b, layer, w_out, h, ties=(), tm=256):
    s = z.shape[0]
    from_loss = target is not None
    other = target if from_loss else up
    nt = len(ties)

    def body(o_ref, z_ref, g_ref, b_ref, w_ref, h_ref, *rest):
        dz_ref, dh_ref, gw_ref, slab_ref, gw_acc, lacc = rest[nt:]
        i = pl.program_id(0)

        @pl.when(i == 0)
        def _():
            gw_acc[...] = jnp.zeros_like(gw_acc)
            slab_ref[...] = jnp.zeros_like(slab_ref)
            lacc[...] = jnp.zeros_like(lacc)

        g = _row_view(g_ref, layer)[...]
        xn, xhat, rstd = _ln_fwd(z_ref[...], g, _row_view(b_ref, layer)[...])
        if from_loss:
            err = xn - o_ref[...]
            lacc[...] += _rowsum(err * err)
            dxn = err * (1.0 / D_MODEL)
        else:
            dxn = o_ref[...]
        slab_ref[0:1, :] += _rowsum(dxn * xhat)
        slab_ref[1:2, :] += _rowsum(dxn)
        dz = _ln_bwd(dxn, xhat, rstd, g)
        dz_ref[...] = dz
        dh_ref[...] = _mm_nt(dz, w_ref[...])
        gw_acc[...] += _mm_tn(h_ref[...], dz)

        @pl.when(i == pl.num_programs(0) - 1)
        def _():
            gw_ref[...] = gw_acc[...].astype(gw_ref.dtype)
            if from_loss:
                total = jnp.sum(lacc[...], axis=-1, keepdims=True) * (0.5 / D_MODEL)
                slab_ref[LOSS_ROW:LOSS_ROW + 1, :] = jnp.broadcast_to(total, (1, D_MODEL))

    row = lambda i: (i, 0)
    fixed = lambda i: (0, 0)
    return pl.pallas_call(
        body, grid=(s // tm,),
        in_specs=[pl.BlockSpec((tm, D_MODEL), row), pl.BlockSpec((tm, D_MODEL), row), _full_spec(ln_g),
                  _full_spec(ln_b), _full_spec(w_out), pl.BlockSpec((tm, D_MIX), row)] + _tie_specs(ties),
        out_specs=(pl.BlockSpec((tm, D_MODEL), row), pl.BlockSpec((tm, D_MIX), row),
                   pl.BlockSpec((D_MIX, D_MODEL), fixed), pl.BlockSpec((8, D_MODEL), fixed)),
        out_shape=(jax.ShapeDtypeStruct((s, D_MODEL), F32), jax.ShapeDtypeStruct((s, D_MIX), F32),
                   jax.ShapeDtypeStruct((D_MIX, D_MODEL), GRAD_DTYPE), jax.ShapeDtypeStruct((8, D_MODEL), F32)),
        scratch_shapes=[pltpu.VMEM((D_MIX, D_MODEL), F32), pltpu.VMEM((1, D_MODEL), F32)],
        name="ln_out_bwd_loss" if from_loss else "ln_out_bwd", compiler_params=_params("arbitrary"),
    )(other, z, ln_g, ln_b, w_out, h, *ties)


_BRANCH_GRADS = (("g256", (8, D_G)), ("sg_w", (N_SUB, CHUNK, CHUNK)), ("sg_b", (8, CHUNK)),
                 ("pool_w", (N_SUB, HEAD_DIM, HEAD_DIM)), ("conv_a_w", (N_DEV, CONV_A, CONV_CH)),
                 ("cc_dw_w", (N_DEV, CONV_D, CONV_CH)), ("cc_pw_w", (D_G, D_G)),
                 ("dk", (N_SUB, MEM_LEN, D_G)), ("dv", (N_SUB, MEM_LEN, D_G)))
_BRANCH_ACC = (("conv_a", (CONV_A, D_G)), ("cc_dw_w", (CONV_D, D_G)), ("pool_wbd", (D_G, D_G)),
               ("sg_bias", (CHUNK, D_G)))


def _branch_bwd(proj, dh, km, vm, layer, repl, pw, ca8, dw8, tile=TILE):
    s = proj.shape[0]
    nt = s // tile
    hb = tile // HALO
    nat_arrays = [repl[n] for n in _BRANCH_REPL]
    n_nat, n_grads, n_acc = len(nat_arrays), len(_BRANCH_GRADS), len(_BRANCH_ACC)
    row_of = {n: k for k, n in enumerate(G256_ROWS)}

    def body(p_ref, ph_ref, dh_ref, km_ref, vm_ref, *rest):
        nat = dict(zip(_BRANCH_REPL, rest[:n_nat]))
        pw_ref, ca_ref, dw_ref = rest[n_nat:n_nat + 3]
        rest = rest[n_nat + 3:]
        dp_ref = rest[0]
        g = dict(zip([n for n, _ in _BRANCH_GRADS], rest[1:1 + n_grads]))
        rest = rest[1 + n_grads:]
        ext_a, ext_c, ext_d, rev_a, rev_c, rev_d, res_c, res_d, res_rc, res_rd = rest[:10]
        acc = dict(zip([n for n, _ in _BRANCH_ACC], rest[10:10 + n_acc]))
        scr = dict(zip([n for n, _ in _BRANCH_W_SCRATCH], rest[10 + n_acc:]))
        i = pl.program_id(0)
        t = nt - 1 - i

        @pl.when(i == 0)
        def _():
            for ref in list(g.values()) + list(acc.values()) + [rev_a, rev_c, rev_d]:
                ref[...] = jnp.zeros_like(ref)

        w = _branch_weights(layer, nat, pw_ref, ca_ref, dw_ref, scr, i == 0)
        r = _branch_forward(p_ref, ph_ref, t == 0, t * tile, km_ref, vm_ref, w, ext_a, ext_c, ext_d, res_c, res_d,
                            tile)

        def put(k, val, width=D_G):
            dp_ref[:, k:k + width] = val.astype(dp_ref.dtype)

        def add_row(name, val):
            k = row_of[name]
            g["g256"][k:k + 1, :] += val

        def push_rev(rev, val):
            head = rev[0:HALO]
            rev[tile:tile + HALO] = head
            rev[0:tile] = val

        dh_all = dh_ref[...]
        gate, sig_gate, concat = r["gate"], r["sig_gate"], r["concat"]
        put(C_GATE, dh_all * concat * (sig_gate * (1.0 + gate * (1.0 - sig_gate))), D_MIX)
        dconcat = dh_all * (gate * sig_gate)
        dya, dyb, dyc, dyd, dye = [dconcat[:, k * D_G:(k + 1) * D_G] for k in range(N_GROUPS)]

        put(C_BA * D_G, dya * r["conv_a"])
        dconv_a = dya * r["ba"]
        for k in range(CONV_A):
            acc["conv_a"][k:k + 1, :] += _rowsum(dconv_a * ext_a[pl.ds(HALO - 2 + k, tile), :])
        push_rev(rev_a, dconv_a)
        dga = w["conv_a"][0:1, :] * rev_a[pl.ds(2, tile), :]
        for k in range(1, CONV_A):
            dga = dga + w["conv_a"][k:k + 1, :] * rev_a[pl.ds(2 - k, tile), :]
        put(C_CA * D_G, dga * r["xa"])
        put(C_XA * D_G, dga * r["ca"])

        add_row("pool_scale", _rowsum(dyc * r["pool_mm"]))
        dmm = dyc * w["pool_scale"][...]
        acc["pool_wbd"][...] += _mm_tn(r["ypre"], dmm)
        dypre = _mm_nt(dmm, w["pool_wbd"][...])
        dws = dypre / r["cnt"]
        push_rev(rev_c, dws)
        _make_residues(rev_c, res_rc)
        run = dws
        sums = {}
        for k in range(1, POOL_WINDOWS[-1]):
            run = run + _rows_at(rev_c, res_rc, k, tile)
            if k + 1 in POOL_WINDOWS:
                sums[k + 1] = run
        put(C_XC * D_G, _pool_select(r["lane_grp"], sums[2], sums[4], sums[8], sums[16]) - dypre)

        g["cc_pw_w"][...] += _mm_tn(r["act_d"], dyd)
        dact = _mm_nt(dyd, w["cc_pw_w"][...])
        sig_ln, ln_d = r["sig_ln"], r["ln_d"]
        dln = dact * (sig_ln * (1.0 + ln_d * (1.0 - sig_ln)))
        add_row("cc_ln_g", _rowsum(dln * r["xhat_d"]))
        add_row("cc_ln_b", _rowsum(dln))
        dconv_d = _ln_bwd(dln, r["xhat_d"], r["rstd_d"], w["cc_ln_g"][...])
        add_row("cc_dw_b", _rowsum(dconv_d))
        for j in range(CONV_D):
            acc["cc_dw_w"][j:j + 1, :] += _rowsum(dconv_d * _rows_at(ext_d, res_d, HALO - (CONV_D - 1) + j, tile))
        push_rev(rev_d, dconv_d)
        _make_residues(rev_d, res_rd)
        dhd = w["cc_dw_w"][0:1, :] * _rows_at(rev_d, res_rd, CONV_D - 1, tile)
        for j in range(1, CONV_D):
            dhd = dhd + w["cc_dw_w"][j:j + 1, :] * _rows_at(rev_d, res_rd, CONV_D - 1 - j, tile)
        sig_dg = r["sig_dg"]
        put(C_DA * D_G, dhd * sig_dg)
        put(C_DG * D_G, dhd * r["da"] * sig_dg * (1.0 - sig_dg))

        dug = dyb * r["mixed"]
        dmixed = dyb * r["ug"]
        wm, lo, vn = r["wm"], r["lo"], r["vn"]
        dvn_chunks = []
        for c in range(tile // CHUNK):
            rows = slice(c * CHUNK, (c + 1) * CHUNK)
            acc["sg_bias"][...] += dmixed[rows, :]
            halves = []
            for hf in range(2):
                cols = slice(hf * 128, (hf + 1) * 128)
                dm = dmixed[rows, cols]
                dm_a, dm_b = jnp.where(lo, dm, 0.0), jnp.where(lo, 0.0, dm)
                vh = vn[rows, cols]
                g["sg_w"][2 * hf] += _mm_nt(dm_a, vh)
                g["sg_w"][2 * hf + 1] += _mm_nt(dm_b, vh)
                halves.append(_mm_tn(wm[2 * hf], dm_a) + _mm_tn(wm[2 * hf + 1], dm_b))
            dvn_chunks.append(jnp.concatenate(halves, axis=1))
        dvn = jnp.concatenate(dvn_chunks, axis=0)
        add_row("sg_ln_g", _rowsum(dvn * r["xhat_v"]))
        add_row("sg_ln_b", _rowsum(dvn))
        dvg = _ln_bwd(dvn, r["xhat_v"], r["rstd_v"], w["sg_ln_g"][...])
        put(C_V * D_G, dvg * _dgelu(r["v"], r["th_v"]))
        put(C_U * D_G, dug * _dgelu(r["u"], r["th_u"]))

        q = r["q"]
        dq = jnp.zeros((tile, D_G), F32)
        for h in range(N_SUB):
            p = r["probs"][h]
            dp = _mm_nt(dye, vm_ref[h])
            g["dv"][h] += _mm_tn(p, dye)
            ds = p * (dp - jnp.sum(dp * p, axis=-1, keepdims=True)) * ATT_SCALE
            dq = dq + _mm(ds, km_ref[h])
            g["dk"][h] += _mm_tn(ds, q)
        put(C_Q * D_G, dq)

        @pl.when(i == nt - 1)
        def _():
            for h in range(N_SUB):
                g["sg_w"][h] = jnp.where(r["tri"], g["sg_w"][h], 0.0)
            lane_head = _head_of_lane((CHUNK, D_G))
            col_of = lax.broadcasted_iota(jnp.int32, (CHUNK, 8), 1)
            ba = acc["sg_bias"][...]
            sgb_t = jnp.zeros((CHUNK, 8), F32)
            for h in range(N_SUB):
                col = jnp.sum(jnp.where(lane_head == h, ba, 0.0), axis=-1, keepdims=True)
                sgb_t = jnp.where(col_of == h, col, sgb_t)
            g["sg_b"][...] = sgb_t.T
            wbd = acc["pool_wbd"][...]
            for gi in range(N_SUB):
                sl = slice(gi * HEAD_DIM, (gi + 1) * HEAD_DIM)
                g["pool_w"][gi] = wbd[sl, sl]
            ca, dw = acc["conv_a"][...], acc["cc_dw_w"][...]
            for p in range(N_DEV):
                g["conv_a_w"][p] = ca[:, p * CONV_CH:(p + 1) * CONV_CH]
                g["cc_dw_w"][p] = dw[:, p * CONV_CH:(p + 1) * CONV_CH]

    rev = lambda i: (nt - 1 - i, 0)
    grad_specs = tuple(pl.BlockSpec(shape, lambda i, _nd=len(shape): (0,) * _nd) for _, shape in _BRANCH_GRADS)
    grad_shapes = tuple(jax.ShapeDtypeStruct(shape, F32) for _, shape in _BRANCH_GRADS)
    outs = pl.pallas_call(
        body, grid=(nt,),
        in_specs=[pl.BlockSpec((tile, D_IN), rev),
                  pl.BlockSpec((HALO, D_IN), lambda i: (jnp.maximum((nt - 1 - i) * hb - 1, 0), 0)),
                  pl.BlockSpec((tile, D_MIX), rev), _full_spec(km), _full_spec(vm)]
        + [_full_spec(a) for a in nat_arrays + [pw, ca8, dw8]],
        out_specs=(pl.BlockSpec((tile, D_IN), rev),) + grad_specs,
        out_shape=(jax.ShapeDtypeStruct((s, D_IN), MM_DTYPE),) + grad_shapes,
        scratch_shapes=[pltpu.VMEM((HALO + tile, D_G), F32)] * 6 + [_residue_scratch(tile)] * 4
        + [pltpu.VMEM(shape, F32) for _, shape in _BRANCH_ACC + _BRANCH_W_SCRATCH],
        name="branch_bwd", compiler_params=_params("arbitrary"),
    )(proj, proj, dh, km, vm, *nat_arrays, pw, ca8, dw8)
    return outs[0], dict(zip([n for n, _ in _BRANCH_GRADS], outs[1:]))


def _dx_matmul(dproj, wi8, dz, ties=(), tm=256):
    s = dproj.shape[0]
    nt = len(ties)

    def body(dp_ref, w8_ref, dz_ref, *rest):
        o_ref, w_full = rest[nt:]

        @pl.when(pl.program_id(0) == 0)
        def _():
            _assemble_cols(w_full, w8_ref)

        o_ref[...] = _mm_nt(dp_ref[...], w_full[...]) + ALPHA * dz_ref[...]

    row = lambda i: (i, 0)
    return pl.pallas_call(
        body, grid=(s // tm,),
        in_specs=[pl.BlockSpec((tm, D_IN), row), _full_spec(wi8), pl.BlockSpec((tm, D_MODEL), row)]
        + _tie_specs(ties),
        out_specs=pl.BlockSpec((tm, D_MODEL), row),
        out_shape=jax.ShapeDtypeStruct((s, D_MODEL), F32),
        scratch_shapes=[pltpu.VMEM((D_MODEL, D_IN), wi8.dtype)], name="dx_mm", compiler_params=_params("arbitrary"),
    )(dproj, wi8, dz, *ties)


def _dw_in_matmul(x, dproj, ties=(), tk=512):
    s = x.shape[0]
    tn = 2 * W_IN_COLS
    nk = s // tk

    def body(x_ref, dp_ref, *rest):
        o_ref, acc = rest[len(ties):]
        k = pl.program_id(1)

        @pl.when(k == 0)
        def _():
            acc[...] = jnp.zeros_like(acc)

        acc[...] += _mm_tn(x_ref[...], dp_ref[...])

        @pl.when(k == nk - 1)
        def _():
            a = acc[...]
            o_ref[0] = a[:, :W_IN_COLS].astype(o_ref.dtype)
            o_ref[1] = a[:, W_IN_COLS:].astype(o_ref.dtype)

    return pl.pallas_call(
        body, grid=(D_IN // tn, nk),
        in_specs=[pl.BlockSpec((tk, D_MODEL), lambda j, k: (k, 0)), pl.BlockSpec((tk, tn), lambda j, k: (k, j))]
        + _tie_specs(ties),
        out_specs=pl.BlockSpec((2, D_MODEL, W_IN_COLS), lambda j, k: (j, 0, 0)),
        out_shape=jax.ShapeDtypeStruct((N_DEV, D_MODEL, W_IN_COLS), GRAD_DTYPE),
        scratch_shapes=[pltpu.VMEM((D_MODEL, tn), F32)], name="dw_in_mm",
        compiler_params=_params("arbitrary", "arbitrary"),
    )(x, dproj, *ties)


def _in_bwd(x, dproj, wi8, dz, tm=256):
    s = x.shape[0]
    n_steps = s // tm

    def body(x_ref, dp_ref, w8_hbm, dz_ref, o_ref, gw_hbm, w_full, slab8, acc, sem):
        i = pl.program_id(0)

        @pl.when(i == 0)
        def _():
            fetch = pltpu.make_async_copy(w8_hbm, slab8, sem)
            fetch.start()
            fetch.wait()
            _assemble_cols(w_full, slab8)
            acc[...] = jnp.zeros_like(acc)

        dp = dp_ref[...]
        o_ref[...] = _mm_nt(dp, w_full[...]) + ALPHA * dz_ref[...]
        acc[...] += _mm_tn(x_ref[...], dp)

        @pl.when(i == n_steps - 1)
        def _():
            for p in range(N_DEV):
                slab8[p] = acc[:, p * W_IN_COLS:(p + 1) * W_IN_COLS].astype(slab8.dtype)
            emit = pltpu.make_async_copy(slab8, gw_hbm, sem)
            emit.start()
            emit.wait()

    row = lambda i: (i, 0)
    any_spec = pl.BlockSpec(memory_space=pl.ANY)
    return pl.pallas_call(
        body, grid=(n_steps,),
        in_specs=[pl.BlockSpec((tm, D_MODEL), row), pl.BlockSpec((tm, D_IN), row), any_spec,
                  pl.BlockSpec((tm, D_MODEL), row)],
        out_specs=(pl.BlockSpec((tm, D_MODEL), row), any_spec),
        out_shape=(jax.ShapeDtypeStruct((s, D_MODEL), F32),
                   jax.ShapeDtypeStruct((N_DEV, D_MODEL, W_IN_COLS), GRAD_DTYPE)),
        scratch_shapes=[pltpu.VMEM((D_MODEL, D_IN), wi8.dtype), pltpu.VMEM((N_DEV, D_MODEL, W_IN_COLS), wi8.dtype),
                        pltpu.VMEM((D_MODEL, D_IN), F32), pltpu.SemaphoreType.DMA],
        name="in_bwd", compiler_params=_params("arbitrary"),
    )(x, dproj, wi8, dz)


def _kv_bwd(mem, dk, dv):
    def body(mem_ref, dk_ref, dv_ref, o_ref):
        grp = _head_of_lane((MEM_LEN, D_G))
        dk_sum = jnp.zeros((MEM_LEN, D_G), F32)
        dv_sum = jnp.zeros((MEM_LEN, D_G), F32)
        for h in range(N_SUB):
            dk_sum = dk_sum + jnp.where(grp == h, dk_ref[h], 0.0)
            dv_sum = dv_sum + jnp.where(grp == h, dv_ref[h], 0.0)
        o_ref[...] = _mm_tn(mem_ref[...], jnp.concatenate([dk_sum, dv_sum], axis=1)).astype(o_ref.dtype)

    return pl.pallas_call(body, out_shape=jax.ShapeDtypeStruct((D_MODEL, 2 * D_G), GRAD_DTYPE), name="kv_bwd",
                          compiler_params=_params())(mem, dk, dv)


def _layer_fwd_a(x, mem, gw, ties=()):
    km, vm = _kv_project(mem, gw["w_kv"])
    return km, vm, _proj_matmul(x, gw["wi8"], ties)


def _layer_fwd_b(x, proj, km, vm, layer, repl, gw, ties=()):
    h = _branch_fwd(proj, km, vm, layer, repl, gw["pw"], gw["ca8"], gw["dw8"], ties)
    z, xn = _out_matmul_ln(h, gw["w_out"], x, repl["ln_g"], repl["ln_b"], layer)
    return xn, (x, proj, h, z, km, vm)


def _layer_bwd_a(up, target, mem, layer, repl, gw, saved, ties=()):
    x_in, proj, h, z, km, vm = saved
    dz, dh, g_w_out, g1024 = _ln_out_bwd(up, target, z, repl["ln_g"], repl["ln_b"], layer, gw["w_out"], h, ties)
    dproj, bg = _branch_bwd(proj, dh, km, vm, layer, repl, gw["pw"], gw["ca8"], gw["dw8"])
    grads = {n: bg[n] for n in ("g256", "sg_w", "sg_b", "pool_w", "conv_a_w", "cc_dw_w")}
    grads.update(w_out=g_w_out.reshape(N_DEV, D_MIX // N_DEV, D_MODEL), g1024=g1024,
                 w_kv=_kv_bwd(mem, bg["dk"], bg["dv"]).reshape(N_DEV, D_MODEL // N_DEV, 2 * D_G),
                 cc_pw_w=bg["cc_pw_w"].reshape(N_DEV, CONV_CH, D_G))
    return dz, dproj, grads


def _landing_shapes(items):
    out = []
    for a, scatter, pick in items:
        shape = a.shape if scatter else (N_DEV,) + (a.shape if pick is None else a.shape[1:])
        out.append(jax.ShapeDtypeStruct(shape, a.dtype))
    return tuple(out)


def _exchange_sems(n):
    return [pltpu.SemaphoreType.DMA(((N_DEV - 1) * n,)), pltpu.SemaphoreType.DMA(((N_DEV - 1) * n,)),
            pltpu.SemaphoreType.DMA((n,))]


def _exchange_copies(modes, ins, outs, send_sems, recv_sems, local_sems):
    n = len(ins)
    x, y, c = lax.axis_index("x"), lax.axis_index("y"), lax.axis_index("c")
    me = 4 * x + 2 * y + c

    def src_of(a, dest):
        scatter, pick = modes[a]
        if scatter:
            return ins[a].at[dest]
        return ins[a] if pick is None else ins[a].at[pick]

    local = [pltpu.make_async_copy(src_of(a, me), outs[a].at[me], local_sems.at[a]) for a in range(n)]
    sends, recvs = [], []
    for k in range(1, N_DEV):
        px = 1 - x if k & 4 else x
        py = 1 - y if k & 2 else y
        pc = 1 - c if k & 1 else c
        peer = 4 * px + 2 * py + pc
        for a in range(n):
            sems = dict(send_sem=send_sems.at[(k - 1) * n + a], recv_sem=recv_sems.at[(k - 1) * n + a],
                        device_id=(px, py, pc), device_id_type=pl.DeviceIdType.MESH)
            sends.append(pltpu.make_async_remote_copy(src_ref=src_of(a, peer), dst_ref=outs[a].at[me], **sems))
            recvs.append(pltpu.make_async_remote_copy(src_ref=src_of(a, peer), dst_ref=outs[a].at[peer], **sems))
    return local, sends, recvs


def _exchange(items, name):
    n = len(items)
    modes = [(scatter, pick) for _, scatter, pick in items]

    def body(*refs):
        local, sends, recvs = _exchange_copies(modes, refs[:n], refs[n:2 * n], *refs[2 * n:])
        for cp in local + sends:
            cp.start()
        for cp in recvs:
            cp.wait_recv()
        for cp in sends:
            cp.wait_send()
        for cp in local:
            cp.wait()

    any_spec = pl.BlockSpec(memory_space=pl.ANY)
    return pl.pallas_call(
        body, in_specs=[any_spec] * n, out_specs=(any_spec,) * n, out_shape=_landing_shapes(items),
        scratch_shapes=_exchange_sems(n), name=name,
    )(*[a for a, _, _ in items])


def _gather_two_level(items, name):
    n = len(items)
    assert not any(scatter for _, scatter, _ in items)
    picks = [pick for _, _, pick in items]

    def body(*refs):
        ins, outs = refs[:n], refs[n:2 * n]
        send_sems, recv_sems, local_sems = refs[2 * n:]
        x, y, c = lax.axis_index("x"), lax.axis_index("y"), lax.axis_index("c")
        sib = 1 - c
        chips = [(1 - x, y), (x, 1 - y), (1 - x, 1 - y)]

        def slot(a, px, py, pc):
            return outs[a].at[4 * px + 2 * py + pc]

        def copy(k, a, src, block, to):
            return pltpu.make_async_remote_copy(
                src_ref=src, dst_ref=slot(a, *block), send_sem=send_sems.at[k * n + a],
                recv_sem=recv_sems.at[k * n + a], device_id=to, device_id_type=pl.DeviceIdType.MESH)

        own = [ins[a] if picks[a] is None else ins[a].at[picks[a]] for a in range(n)]
        local = [pltpu.make_async_copy(own[a], slot(a, x, y, c), local_sems.at[a]) for a in range(n)]
        first = [copy(0, a, own[a], (x, y, c), (x, y, sib)) for a in range(n)]
        first += [copy(1 + j, a, own[a], (x, y, c), (*chip, c)) for j, chip in enumerate(chips) for a in range(n)]
        for cp in local + first:
            cp.start()
        passed = []
        for j, chip in enumerate(chips):
            for a in range(n):
                copy(1 + j, a, own[a], (*chip, c), (x, y, c)).wait_recv()
                fwd = copy(4 + j, a, slot(a, *chip, c), (*chip, c), (x, y, sib))
                fwd.start()
                passed.append(fwd)
        for a in range(n):
            copy(0, a, own[a], (x, y, sib), (x, y, c)).wait_recv()
        for j, chip in enumerate(chips):
            for a in range(n):
                copy(4 + j, a, own[a], (*chip, sib), (x, y, c)).wait_recv()
        for cp in first + passed:
            cp.wait_send()
        for cp in local:
            cp.wait()

    any_spec = pl.BlockSpec(memory_space=pl.ANY)
    return pl.pallas_call(
        body, in_specs=[any_spec] * n, out_specs=(any_spec,) * n, out_shape=_landing_shapes(items),
        scratch_shapes=[pltpu.SemaphoreType.DMA((7 * n,)), pltpu.SemaphoreType.DMA((7 * n,)),
                        pltpu.SemaphoreType.DMA((n,))],
        name=name,
    )(*[a for a, _, _ in items])


_HBM_SPEC = pl.BlockSpec(memory_space=pltpu.HBM)
_SEM_SPEC = pl.BlockSpec(memory_space=pltpu.SEMAPHORE)
_SPLIT_PARAMS = pltpu.CompilerParams(has_side_effects=pltpu.SideEffectType.DATAFLOW_SIDE_EFFECTING)


def _exchange_start(items, name):
    n = len(items)
    modes = [(scatter, pick) for _, scatter, pick in items]
    shapes = _landing_shapes(items)
    lands = [pltpu.with_memory_space_constraint(lax.empty(s.shape, s.dtype), pltpu.HBM) for s in shapes]
    srcs = [pltpu.with_memory_space_constraint(a, pltpu.HBM) for a, _, _ in items]

    def body(*refs):
        local, sends, _ = _exchange_copies(modes, refs[:n], refs[n:2 * n], *refs[2 * n:2 * n + 3])
        for cp in local + sends:
            cp.start()
        token = refs[-1]
        token[...] = jnp.zeros_like(token)

    res = pl.pallas_call(
        body, name=name, in_specs=[_HBM_SPEC] * (2 * n),
        out_shape=tuple(_exchange_sems(n)) + tuple(pltpu.HBM(a.shape, a.dtype) for a in srcs)
        + tuple(pltpu.HBM(s.shape, s.dtype) for s in shapes) + (jax.ShapeDtypeStruct((8, 128), F32),),
        out_specs=(_SEM_SPEC,) * 3 + (_HBM_SPEC,) * (2 * n) + (pl.BlockSpec(memory_space=pltpu.VMEM),),
        input_output_aliases={i: 3 + i for i in range(2 * n)}, compiler_params=_SPLIT_PARAMS,
    )(*srcs, *lands)
    return dict(sems=res[:3], srcs=res[3:3 + n], lands=res[3 + n:3 + 2 * n], token=res[-1], modes=modes)


def _exchange_wait(ticket, after, name):
    n = len(ticket["srcs"])
    modes = ticket["modes"]

    def body(*refs):
        local, sends, recvs = _exchange_copies(modes, refs[:n], refs[n:2 * n], *refs[2 * n:2 * n + 3])
        for cp in recvs:
            cp.wait_recv()
        for cp in sends:
            cp.wait_send()
        for cp in local:
            cp.wait()

    both = list(ticket["srcs"]) + list(ticket["lands"])
    res = pl.pallas_call(
        body, name=name, in_specs=[_HBM_SPEC] * (2 * n) + [_SEM_SPEC] * 3 + [pl.BlockSpec(memory_space=pl.ANY)],
        out_shape=tuple(pltpu.HBM(a.shape, a.dtype) for a in both), out_specs=(_HBM_SPEC,) * (2 * n),
        input_output_aliases={i: i for i in range(2 * n)}, compiler_params=_SPLIT_PARAMS,
    )(*both, *ticket["sems"], after)
    return res[n:]


def _adam_math(g, w, m, v):
    m_new = ADAM_B1 * m + (1.0 - ADAM_B1) * g
    v_new = ADAM_B2 * v + (1.0 - ADAM_B2) * (g * g)
    m_hat = m_new / (1.0 - ADAM_B1 ** ADAM_STEP)
    v_hat = v_new / (1.0 - ADAM_B2 ** ADAM_STEP)
    return -ADAM_LR * (m_hat / (jnp.sqrt(v_hat) + ADAM_EPS) + ADAM_WD * w), m_new, v_new


def _adamw_big(parts, w, m, v, layer, prev, name, tr):
    depth, rows, cols = w.shape

    def body(p_ref, w_ref, m_ref, v_ref, *rest):
        g_out, d_out, m_out, v_out = rest[len(prev):]
        g = p_ref[0].astype(F32)
        for q in range(1, N_DEV):
            g = g + p_ref[q].astype(F32)
        d, m_new, v_new = _adam_math(g, w_ref[...], m_ref[...], v_ref[...])
        g_out[...] = g
        d_out[...] = d
        m_out[...] = m_new
        v_out[...] = v_new

    blk = pl.BlockSpec((None, tr, cols), lambda i: (layer, i, 0))
    shp = jax.ShapeDtypeStruct((depth, rows, cols), F32)
    return pl.pallas_call(
        body, grid=(rows // tr,),
        in_specs=[pl.BlockSpec((N_DEV, tr, cols), lambda i: (0, i, 0)), blk, blk, blk]
        + [pl.BlockSpec(memory_space=pl.ANY)] * len(prev),
        out_specs=(blk,) * 4, out_shape=(shp,) * 4,
        input_output_aliases={4 + j: j for j in range(len(prev))},
        name=name, compiler_params=_params("arbitrary"),
    )(parts, w, m, v, *prev)


_SMALL_TENSORS = (("conv_a_w", "conv_a_w", None), ("cc_dw_w", "cc_dw_w", None), ("cc_pw_w", "cc_pw_w", None),
                  ("sg_w", "sg_w", None), ("pool_w", "pool_w", None), ("sg_b", "sg_b", None)) \
    + tuple((n, "g256", k) for k, n in enumerate(G256_ROWS)) + tuple((n, "g1024", k) for k, n in enumerate(G1024_ROWS))
_SMALL_LANDINGS = ("conv_a_w", "cc_dw_w", "cc_pw_w", "sg_w", "pool_w", "sg_b", "g256", "g1024")


def _adamw_small(landings, wts, mom, var):
    names = [n for n, _, _ in _SMALL_TENSORS]
    n_land = DEPTH * len(_SMALL_LANDINGS)
    n_t = len(names)

    def body(*refs):
        land = [dict(zip(_SMALL_LANDINGS, refs[l * len(_SMALL_LANDINGS):(l + 1) * len(_SMALL_LANDINGS)]))
                for l in range(DEPTH)]
        w_refs = dict(zip(names, refs[n_land:n_land + n_t]))
        m_refs = dict(zip(names, refs[n_land + n_t:n_land + 2 * n_t]))
        v_refs = dict(zip(names, refs[n_land + 2 * n_t:n_land + 3 * n_t]))
        outs = refs[n_land + 3 * n_t:]
        out_refs = {n: outs[4 * k:4 * k + 4] for k, n in enumerate(names)}
        loss_ref = outs[4 * n_t]
        for name, key, row in _SMALL_TENSORS:
            for l in range(DEPTH):
                src = land[l][key]
                if row is not None:
                    part = lambda q: src[q, row:row + 1, :]
                    view = lambda ref: ref.at[pl.ds(l, 1)]
                elif name == "sg_b":
                    part = lambda q: src[q, 0:N_SUB, :]
                    view = lambda ref: ref.at[l]
                else:
                    part = lambda q: src[q]
                    view = lambda ref: ref.at[l]
                g = part(0)
                for q in range(1, N_DEV):
                    g = g + part(q)
                d, m_new, v_new = _adam_math(g, view(w_refs[name])[...], view(m_refs[name])[...],
                                             view(v_refs[name])[...])
                for ref, val in zip(out_refs[name], (g, d, m_new, v_new)):
                    view(ref)[...] = val
        src = land[DEPTH - 1]["g1024"]
        loss = src[0, LOSS_ROW:LOSS_ROW + 1, 0:128]
        for q in range(1, N_DEV):
            loss = loss + src[q, LOSS_ROW:LOSS_ROW + 1, 0:128]
        loss_ref[...] = loss

    ins = [landings[l][k] for l in range(DEPTH) for k in _SMALL_LANDINGS] \
        + [src[n] for src in (wts, mom, var) for n in names]
    out_shape = tuple(jax.ShapeDtypeStruct(wts[n].shape, F32) for n in names for _ in range(4)) \
        + (jax.ShapeDtypeStruct((1, 128), F32),)
    res = pl.pallas_call(body, out_shape=out_shape, name="adamw_small", compiler_params=_params())(*ins)
    return {n: res[4 * k:4 * k + 4] for k, n in enumerate(names)}, res[4 * n_t]


_BIG = (("w_in", 256), ("w_out", 32), ("w_kv", 32))
_GRAD_ITEMS_EARLY = ("w_out", "w_kv", "cc_pw_w", "conv_a_w", "cc_dw_w")
_GRAD_ITEMS_REPL = ("g256", "sg_w", "sg_b", "pool_w", "g1024")


def _grad_items(grads, with_w_in):
    items = [(grads[n], True, None) for n in (("w_in",) if with_w_in else ()) + _GRAD_ITEMS_EARLY]
    return items + [(grads[n], False, None) for n in _GRAD_ITEMS_REPL]


def _landed(parts, with_w_in):
    names = (("w_in",) if with_w_in else ()) + _GRAD_ITEMS_EARLY + _GRAD_ITEMS_REPL
    return dict(zip(names, parts))


def _gathered_weights(wi8, w_kv8, w_out8, pw8, ca8, dw8):
    return dict(wi8=wi8, w_kv=w_kv8.reshape(D_MODEL, 2 * D_G), w_out=w_out8.reshape(D_MIX, D_MODEL),
                pw=pw8.reshape(D_G, D_G), ca8=ca8, dw8=dw8)


def kernel(x, mem, w_in, conv_a_w, sg_ln_g, sg_ln_b, sg_w, sg_b, pool_w, pool_scale, cc_dw_w, cc_dw_b, cc_ln_g, cc_ln_b, cc_pw_w, w_kv, w_out, ln_g, ln_b, loss_target, m_w_in, m_conv_a_w, m_sg_ln_g, m_sg_ln_b, m_sg_w, m_sg_b, m_pool_w, m_pool_scale, m_cc_dw_w, m_cc_dw_b, m_cc_ln_g, m_cc_ln_b, m_cc_pw_w, m_w_kv, m_w_out, m_ln_g, m_ln_b, v_w_in, v_conv_a_w, v_sg_ln_g, v_sg_ln_b, v_sg_w, v_sg_b, v_pool_w, v_pool_scale, v_cc_dw_w, v_cc_dw_b, v_cc_ln_g, v_cc_ln_b, v_cc_pw_w, v_w_kv, v_w_out, v_ln_g, v_ln_b):
    names = ("w_in", "conv_a_w", "sg_ln_g", "sg_ln_b", "sg_w", "sg_b", "pool_w", "pool_scale", "cc_dw_w", "cc_dw_b",
             "cc_ln_g", "cc_ln_b", "cc_pw_w", "w_kv", "w_out", "ln_g", "ln_b")
    wts = dict(zip(names, (w_in, conv_a_w, sg_ln_g, sg_ln_b, sg_w, sg_b, pool_w, pool_scale, cc_dw_w, cc_dw_b,
                           cc_ln_g, cc_ln_b, cc_pw_w, w_kv, w_out, ln_g, ln_b)))
    mom = dict(zip(names, (m_w_in, m_conv_a_w, m_sg_ln_g, m_sg_ln_b, m_sg_w, m_sg_b, m_pool_w, m_pool_scale,
                           m_cc_dw_w, m_cc_dw_b, m_cc_ln_g, m_cc_ln_b, m_cc_pw_w, m_w_kv, m_w_out, m_ln_g, m_ln_b)))
    var = dict(zip(names, (v_w_in, v_conv_a_w, v_sg_ln_g, v_sg_ln_b, v_sg_w, v_sg_b, v_pool_w, v_pool_scale,
                           v_cc_dw_w, v_cc_dw_b, v_cc_ln_g, v_cc_ln_b, v_cc_pw_w, v_w_kv, v_w_out, v_ln_g, v_ln_b)))
    repl = wts
    xs, mems, tgt = x[0], mem[0], loss_target[0]
    wb = {n: wts[n].astype(MM_DTYPE) for n in ("w_in", "w_kv", "w_out", "cc_pw_w")}

    wi8_0, wkv8_0 = _gather_two_level([(wb["w_in"], False, 0), (wb["w_kv"], False, 0)], "gather_weights_0a")
    rest_0 = _exchange_start([(wb["w_out"], False, 0), (wb["cc_pw_w"], False, 0), (conv_a_w, False, None),
                              (cc_dw_w, False, None)], "gather_weights_0b_start")
    km0, vm0, proj0 = _layer_fwd_a(xs, mems, dict(wi8=wi8_0, w_kv=wkv8_0.reshape(D_MODEL, 2 * D_G)),
                                   (rest_0["token"],))
    wo8_0, pw8_0, ca8, dw8 = _exchange_wait(rest_0, proj0, "gather_weights_0b_wait")
    gw0 = _gathered_weights(wi8_0, wkv8_0, wo8_0, pw8_0, ca8, dw8)
    all_1 = _exchange_start([(wb["w_in"], False, 1), (wb["w_kv"], False, 1), (wb["w_out"], False, 1),
                             (wb["cc_pw_w"], False, 1)], "gather_weights_1_start")
    x1, saved0 = _layer_fwd_b(xs, proj0, km0, vm0, 0, repl, gw0, (all_1["token"],))
    gw1 = _gathered_weights(*_exchange_wait(all_1, x1, "gather_weights_1_wait"), ca8, dw8)
    km1, vm1, proj1 = _layer_fwd_a(x1, mems, gw1)
    _, saved1 = _layer_fwd_b(x1, proj1, km1, vm1, 1, repl, gw1)

    dz1, dproj1, g1 = _layer_bwd_a(None, tgt, mems, 1, repl, gw1, saved1)
    up, g1["w_in"] = _in_bwd(saved1[0], dproj1, gw1["wi8"], dz1)
    grads_1 = _exchange_start(_grad_items(g1, True), "exchange_grads_1_start")
    dz0, dproj0, g0 = _layer_bwd_a(up, None, mems, 0, repl, gw0, saved0, (grads_1["token"],))
    early_0 = _exchange_start(_grad_items(g0, False), "exchange_grads_0a_start")
    g_w_in_0 = _dw_in_matmul(saved0[0], dproj0, (early_0["token"],))
    late_0 = _exchange_start([(g_w_in_0, True, None)], "exchange_grads_0b_start")
    grad_x = _dx_matmul(dproj0, gw0["wi8"], dz0, (late_0["token"],))

    landed = [None, _landed(_exchange_wait(grads_1, grad_x, "exchange_grads_1_wait"), True)]
    big = {}
    for n, tr in _BIG:
        big[n] = _adamw_big(landed[1][n], wts[n], mom[n], var[n], 1, (), "adamw_" + n + "_1", tr)
    landed[0] = _landed(_exchange_wait(early_0, big["w_kv"][0], "exchange_grads_0a_wait"), False)
    for n, tr in _BIG[1:]:
        big[n] = _adamw_big(landed[0][n], wts[n], mom[n], var[n], 0, big[n], "adamw_" + n + "_0", tr)
    (landed[0]["w_in"],) = _exchange_wait(late_0, big["w_kv"][0], "exchange_grads_0b_wait")
    big["w_in"] = _adamw_big(landed[0]["w_in"], wts["w_in"], mom["w_in"], var["w_in"], 0, big["w_in"],
                             "adamw_w_in_0", _BIG[0][1])
    small, loss = _adamw_small(landed, wts, mom, var)

    res = {**small, **big}
    return (loss[0, 0], grad_x[None], *[res[n][0] for n in names], *[res[n][1] for n in names],
            *[res[n][2] for n in names], *[res[n][3] for n in names])
```

```python
import math

import jax
import jax.numpy as jnp
from jax import lax
from jax.experimental import pallas as pl
from jax.experimental.pallas import tpu as pltpu

F32 = jnp.float32
MM_DTYPE = jnp.bfloat16
GRAD_DTYPE = jnp.bfloat16

D_MODEL = 1024
DEPTH = 2
D_G = 256
N_GROUPS = 5
D_MIX = N_GROUPS * D_G
N_SUB = 4
HEAD_DIM = D_G // N_SUB
CONV_A = 3
CONV_D = 31
CHUNK = 128
POOL_WINDOWS = (2, 4, 8, 16)
MEM_LEN = 256
LN_EPS = 1e-5
ALPHA = (2.0 * DEPTH) ** 0.25
D_IN = 9 * D_G + D_MIX
ATT_SCALE = 1.0 / math.sqrt(HEAD_DIM)

ADAM_LR = 0.001
ADAM_B1 = 0.9
ADAM_B2 = 0.999
ADAM_EPS = 1e-08
ADAM_WD = 0.01
ADAM_STEP = 10

N_DEV = 8
W_IN_COLS = D_IN // N_DEV
CONV_CH = D_G // N_DEV
HALO = 32
TILE = 256
VMEM_LIMIT = 56 * 1024 * 1024

C_XA, C_BA, C_CA, C_U, C_V, C_XC, C_DA, C_DG, C_Q = range(9)
C_GATE = 9 * D_G

G256_ROWS = ("sg_ln_g", "sg_ln_b", "pool_scale", "cc_dw_b", "cc_ln_g", "cc_ln_b")
G1024_ROWS = ("ln_g", "ln_b")
LOSS_ROW = 2


def _mm(a, b):
    return jnp.dot(a.astype(MM_DTYPE), b.astype(MM_DTYPE), preferred_element_type=F32)


def _mm_nt(a, b):
    return lax.dot_general(a.astype(MM_DTYPE), b.astype(MM_DTYPE), (((1,), (1,)), ((), ())),
                           preferred_element_type=F32)


def _mm_tn(a, b):
    return lax.dot_general(a.astype(MM_DTYPE), b.astype(MM_DTYPE), (((0,), (0,)), ((), ())),
                           preferred_element_type=F32)


def _sigmoid(x):
    return 1.0 / (1.0 + jnp.exp(-x))


_GELU_C = math.sqrt(2.0 / math.pi)
_GELU_A = 0.044715


def _gelu(x):
    th = jnp.tanh(_GELU_C * (x + _GELU_A * (x * x * x)))
    return 0.5 * x * (1.0 + th), th


def _dgelu(x, th):
    return 0.5 * (1.0 + th) + 0.5 * x * (1.0 - th * th) * (_GELU_C * (1.0 + 3.0 * _GELU_A * (x * x)))


def _ln_fwd(x, g, b):
    mu = jnp.mean(x, axis=-1, keepdims=True)
    xc = x - mu
    var = jnp.mean(xc * xc, axis=-1, keepdims=True)
    rstd = lax.rsqrt(var + LN_EPS)
    xhat = xc * rstd
    return xhat * g + b, xhat, rstd


def _ln_bwd(dy, xhat, rstd, g):
    dxhat = dy * g
    m1 = jnp.mean(dxhat, axis=-1, keepdims=True)
    m2 = jnp.mean(dxhat * xhat, axis=-1, keepdims=True)
    return rstd * (dxhat - m1 - xhat * m2)


def _rowsum(x):
    return jnp.sum(x, axis=0, keepdims=True)


def _col(ref, k):
    return ref[:, k * D_G:(k + 1) * D_G]


def _head_of_lane(shape):
    return jnp.right_shift(lax.broadcasted_iota(jnp.int32, shape, len(shape) - 1), HEAD_DIM.bit_length() - 1)


def _pool_select(lane_grp, s2, s4, s8, s16):
    return jnp.where(lane_grp == 0, s2, jnp.where(lane_grp == 1, s4, jnp.where(lane_grp == 2, s8, s16)))


def _row_view(ref, layer):
    return ref.at[pl.ds(layer, 1)]


def _make_residues(ext_ref, res_ref):
    rows = res_ref.shape[1]
    for r in range(1, 8):
        res_ref[r - 1] = ext_ref[pl.ds(r, rows), :]


def _rows_at(ext_ref, res_ref, off, tile):
    a, r = divmod(off, 8)
    if r == 0:
        return ext_ref[pl.ds(off, tile), :]
    return res_ref[r - 1, pl.ds(8 * a, tile), :]


def _residue_scratch(tile):
    return pltpu.VMEM((7, HALO + tile - 8, D_G), F32)


def _branch_forward(p_ref, ph_ref, first, row0, km_ref, vm_ref, w, ext_a, ext_c, ext_d, res_c, res_d, tile):
    r = {}
    xa, ba, ca = _col(p_ref, C_XA), _col(p_ref, C_BA), _col(p_ref, C_CA)
    ext_a[0:HALO] = jnp.where(first, 0.0, _col(ph_ref, C_CA) * _col(ph_ref, C_XA))
    ext_a[HALO:HALO + tile] = ca * xa
    conv_a = w["conv_a"][0:1, :] * ext_a[pl.ds(HALO - 2, tile), :]
    for k in range(1, CONV_A):
        conv_a = conv_a + w["conv_a"][k:k + 1, :] * ext_a[pl.ds(HALO - 2 + k, tile), :]
    r.update(xa=xa, ba=ba, ca=ca, conv_a=conv_a)
    ya = ba * conv_a

    xc = _col(p_ref, C_XC)
    ext_c[0:HALO] = jnp.where(first, 0.0, _col(ph_ref, C_XC))
    ext_c[HALO:HALO + tile] = xc
    _make_residues(ext_c, res_c)
    acc = xc
    sums = {}
    for k in range(1, POOL_WINDOWS[-1]):
        acc = acc + _rows_at(ext_c, res_c, HALO - k, tile)
        if k + 1 in POOL_WINDOWS:
            sums[k + 1] = acc
    lane_grp = _head_of_lane((tile, D_G))
    trow = row0 + lax.broadcasted_iota(jnp.int32, (tile, D_G), 0)
    win = _pool_select(lane_grp, 2, 4, 8, 16)
    cnt = jnp.minimum(trow + 1, win).astype(F32)
    ypre = _pool_select(lane_grp, sums[2], sums[4], sums[8], sums[16]) / cnt - xc
    pool_mm = _mm(ypre, w["pool_wbd"][...])
    yc = pool_mm * w["pool_scale"][...]
    r.update(lane_grp=lane_grp, cnt=cnt, ypre=ypre, pool_mm=pool_mm)

    da, dg = _col(p_ref, C_DA), _col(p_ref, C_DG)
    sig_dg = _sigmoid(dg)
    ext_d[0:HALO] = jnp.where(first, 0.0, _col(ph_ref, C_DA) * _sigmoid(_col(ph_ref, C_DG)))
    ext_d[HALO:HALO + tile] = da * sig_dg
    _make_residues(ext_d, res_d)
    conv_d = w["cc_dw_b"][...] + w["cc_dw_w"][0:1, :] * _rows_at(ext_d, res_d, HALO - (CONV_D - 1), tile)
    for j in range(1, CONV_D):
        conv_d = conv_d + w["cc_dw_w"][j:j + 1, :] * _rows_at(ext_d, res_d, HALO - (CONV_D - 1) + j, tile)
    ln_d, xhat_d, rstd_d = _ln_fwd(conv_d, w["cc_ln_g"][...], w["cc_ln_b"][...])
    sig_ln = _sigmoid(ln_d)
    act_d = ln_d * sig_ln
    yd = _mm(act_d, w["cc_pw_w"][...])
    r.update(da=da, sig_dg=sig_dg, ln_d=ln_d, xhat_d=xhat_d, rstd_d=rstd_d, sig_ln=sig_ln, act_d=act_d)

    u, v = _col(p_ref, C_U), _col(p_ref, C_V)
    ug, th_u = _gelu(u)
    vg, th_v = _gelu(v)
    vn, xhat_v, rstd_v = _ln_fwd(vg, w["sg_ln_g"][...], w["sg_ln_b"][...])
    tri = (lax.broadcasted_iota(jnp.int32, (CHUNK, CHUNK), 0)
           >= lax.broadcasted_iota(jnp.int32, (CHUNK, CHUNK), 1))
    wm = [jnp.where(tri, w["sg_w"][h], 0.0).astype(MM_DTYPE) for h in range(N_SUB)]
    lo = lax.broadcasted_iota(jnp.int32, (CHUNK, 2 * HEAD_DIM), 1) < HEAD_DIM
    chunks = []
    for c in range(tile // CHUNK):
        halves = []
        for hf in range(2):
            vh = vn[c * CHUNK:(c + 1) * CHUNK, hf * 128:(hf + 1) * 128]
            halves.append(_mm(wm[2 * hf], jnp.where(lo, vh, 0.0)) + _mm(wm[2 * hf + 1], jnp.where(lo, 0.0, vh)))
        chunks.append(jnp.concatenate(halves, axis=1) + w["sg_bias"][...])
    mixed = jnp.concatenate(chunks, axis=0)
    yb = ug * mixed
    r.update(u=u, v=v, ug=ug, th_u=th_u, th_v=th_v, vn=vn, xhat_v=xhat_v, rstd_v=rstd_v, wm=wm, lo=lo,
             mixed=mixed, tri=tri)

    q = _col(p_ref, C_Q)
    ye = jnp.zeros((tile, D_G), F32)
    probs = []
    for h in range(N_SUB):
        s = _mm_nt(q, km_ref[h]) * ATT_SCALE
        e = jnp.exp(s - jnp.max(s, axis=-1, keepdims=True))
        p = e / jnp.sum(e, axis=-1, keepdims=True)
        probs.append(p)
        ye = ye + _mm(p, vm_ref[h])
    r.update(q=q, probs=probs)

    gate = p_ref[:, C_GATE:C_GATE + D_MIX]
    sig_gate = _sigmoid(gate)
    concat = jnp.concatenate([ya, yb, yc, yd, ye], axis=1)
    r.update(gate=gate, sig_gate=sig_gate, concat=concat)
    return r


_BRANCH_REPL = ("sg_ln_g", "sg_ln_b", "sg_w", "sg_b", "pool_w", "pool_scale", "cc_dw_b", "cc_ln_g", "cc_ln_b")
_BRANCH_W_SCRATCH = (("conv_a", (CONV_A, D_G)), ("cc_dw_w", (CONV_D, D_G)), ("sg_bias", (CHUNK, D_G)),
                     ("pool_wbd", (D_G, D_G)), ("sgb8", (8, CHUNK)))


def _branch_weights(layer, nat, pw_ref, ca_ref, dw_ref, scr, init):
    @pl.when(init)
    def _():
        for p in range(N_DEV):
            scr["conv_a"][:, p * CONV_CH:(p + 1) * CONV_CH] = ca_ref[p, :, layer, :]
            scr["cc_dw_w"][:, p * CONV_CH:(p + 1) * CONV_CH] = dw_ref[p, :, layer, :]
        scr["sgb8"][...] = jnp.zeros((8, CHUNK), F32)
        scr["sgb8"][0:N_SUB] = nat["sg_b"][layer]
        sgb_t = scr["sgb8"][...].T
        head = _head_of_lane((CHUNK, D_G))
        bias = jnp.zeros((CHUNK, D_G), F32)
        for h in range(N_SUB):
            bias = jnp.where(head == h, sgb_t[:, h:h + 1], bias)
        scr["sg_bias"][...] = bias
        scr["pool_wbd"][...] = jnp.zeros((D_G, D_G), F32)
        for gi in range(N_SUB):
            sl = slice(gi * HEAD_DIM, (gi + 1) * HEAD_DIM)
            scr["pool_wbd"][sl, sl] = nat["pool_w"][layer, gi]

    w = {n: _row_view(nat[n], layer) for n in ("sg_ln_g", "sg_ln_b", "pool_scale", "cc_dw_b", "cc_ln_g", "cc_ln_b")}
    w.update(conv_a=scr["conv_a"], cc_dw_w=scr["cc_dw_w"], sg_bias=scr["sg_bias"], pool_wbd=scr["pool_wbd"],
             sg_w=nat["sg_w"].at[layer], cc_pw_w=pw_ref)
    return w


def _full_spec(a):
    nd = a.ndim
    return pl.BlockSpec(a.shape, lambda *_, _nd=nd: (0,) * _nd)


def _tie_specs(ties):
    return [pl.BlockSpec((8, 128), lambda *_: (0, 0)) for _ in ties]


def _params(*sem):
    return pltpu.CompilerParams(dimension_semantics=sem or None, vmem_limit_bytes=VMEM_LIMIT)


def _proj_matmul(x, wt_in, ties=(), tm=256):
    s, k = x.shape

    def body(x_ref, w_ref, *rest):
        o_ref = rest[len(ties)]
        o_ref[...] = _mm_nt(x_ref[...], w_ref[...])

    return pl.pallas_call(
        body, grid=(s // tm,),
        in_specs=[pl.BlockSpec((tm, k), lambda i: (i, 0)), _full_spec(wt_in)] + _tie_specs(ties),
        out_specs=pl.BlockSpec((tm, D_IN), lambda i: (i, 0)),
        out_shape=jax.ShapeDtypeStruct((s, D_IN), F32), name="proj_mm", compiler_params=_params("arbitrary"),
    )(x, wt_in, *ties)


def _kv_project(mem, w_kv):
    def body(mem_ref, w_ref, km_ref, vm_ref):
        kv = _mm(mem_ref[...], w_ref[...])
        k, v = kv[:, :D_G], kv[:, D_G:]
        grp = _head_of_lane((MEM_LEN, D_G))
        for h in range(N_SUB):
            km_ref[h] = jnp.where(grp == h, k, 0.0).astype(km_ref.dtype)
            vm_ref[h] = jnp.where(grp == h, v, 0.0).astype(vm_ref.dtype)

    shp = jax.ShapeDtypeStruct((N_SUB, MEM_LEN, D_G), MM_DTYPE)
    return pl.pallas_call(body, out_shape=(shp, shp), name="kv_project", compiler_params=_params())(mem, w_kv)


def _branch_fwd(proj, km, vm, layer, repl, pw, ca8, dw8, ties=(), tile=TILE):
    s = proj.shape[0]
    hb = tile // HALO
    nat_arrays = [repl[n] for n in _BRANCH_REPL]
    n_nat, nt = len(nat_arrays), len(ties)

    def body(p_ref, ph_ref, km_ref, vm_ref, *rest):
        nat = dict(zip(_BRANCH_REPL, rest[:n_nat]))
        pw_ref, ca_ref, dw_ref = rest[n_nat:n_nat + 3]
        h_ref, ext_a, ext_c, ext_d, res_c, res_d = rest[n_nat + 3 + nt:n_nat + 9 + nt]
        scr = dict(zip([n for n, _ in _BRANCH_W_SCRATCH], rest[n_nat + 9 + nt:]))
        i = pl.program_id(0)
        w = _branch_weights(layer, nat, pw_ref, ca_ref, dw_ref, scr, i == 0)
        r = _branch_forward(p_ref, ph_ref, i == 0, i * tile, km_ref, vm_ref, w, ext_a, ext_c, ext_d, res_c, res_d,
                            tile)
        h_ref[...] = (r["concat"] * (r["gate"] * r["sig_gate"])).astype(h_ref.dtype)

    return pl.pallas_call(
        body, grid=(s // tile,),
        in_specs=[pl.BlockSpec((tile, D_IN), lambda i: (i, 0)),
                  pl.BlockSpec((HALO, D_IN), lambda i: (jnp.maximum(i * hb - 1, 0), 0)),
                  _full_spec(km), _full_spec(vm)] + [_full_spec(a) for a in nat_arrays + [pw, ca8, dw8]]
        + _tie_specs(ties),
        out_specs=pl.BlockSpec((tile, D_MIX), lambda i: (i, 0)),
        out_shape=jax.ShapeDtypeStruct((s, D_MIX), MM_DTYPE),
        scratch_shapes=[pltpu.VMEM((HALO + tile, D_G), F32)] * 3 + [_residue_scratch(tile)] * 2
        + [pltpu.VMEM(shape, F32) for _, shape in _BRANCH_W_SCRATCH],
        name="branch_fwd", compiler_params=_params("arbitrary"),
    )(proj, proj, km, vm, *nat_arrays, pw, ca8, dw8, *ties)


def _out_matmul_ln(h, w_out, x, ln_g, ln_b, layer, tm=256):
    s = h.shape[0]

    def body(h_ref, w_ref, x_ref, g_ref, b_ref, z_ref, xn_ref):
        z = ALPHA * x_ref[...] + _mm(h_ref[...], w_ref[...])
        z_ref[...] = z
        xn_ref[...] = _ln_fwd(z, _row_view(g_ref, layer)[...], _row_view(b_ref, layer)[...])[0]

    row = lambda i: (i, 0)
    shp = jax.ShapeDtypeStruct((s, D_MODEL), F32)
    return pl.pallas_call(
        body, grid=(s // tm,),
        in_specs=[pl.BlockSpec((tm, D_MIX), row), _full_spec(w_out), pl.BlockSpec((tm, D_MODEL), row),
                  _full_spec(ln_g), _full_spec(ln_b)],
        out_specs=(pl.BlockSpec((tm, D_MODEL), row), pl.BlockSpec((tm, D_MODEL), row)),
        out_shape=(shp, shp), name="out_mm_ln", compiler_params=_params("arbitrary"),
    )(h, w_out, x, ln_g, ln_b)


def _ln_out_bwd(up, target, z, ln_g, ln_b, layer, w_out, h, ties=(), tm=256):
    s = z.shape[0]
    from_loss = target is not None
    other = target if from_loss else up
    nt = len(ties)

    def body(o_ref, z_ref, g_ref, b_ref, w_ref, h_ref, *rest):
        dz_ref, dh_ref, gw_ref, slab_ref, gw_acc, lacc = rest[nt:]
        i = pl.program_id(0)

        @pl.when(i == 0)
        def _():
            gw_acc[...] = jnp.zeros_like(gw_acc)
            slab_ref[...] = jnp.zeros_like(slab_ref)
            lacc[...] = jnp.zeros_like(lacc)

        g = _row_view(g_ref, layer)[...]
        xn, xhat, rstd = _ln_fwd(z_ref[...], g, _row_view(b_ref, layer)[...])
        if from_loss:
            err = xn - o_ref[...]
            lacc[...] += _rowsum(err * err)
            dxn = err * (1.0 / D_MODEL)
        else:
            dxn = o_ref[...]
        slab_ref[0:1, :] += _rowsum(dxn * xhat)
        slab_ref[1:2, :] += _rowsum(dxn)
        dz = _ln_bwd(dxn, xhat, rstd, g)
        dz_ref[...] = dz
        dh_ref[...] = _mm_nt(dz, w_ref[...])
        gw_acc[...] += _mm_tn(h_ref[...], dz)

        @pl.when(i == pl.num_programs(0) - 1)
        def _():
            gw_ref[...] = gw_acc[...].astype(gw_ref.dtype)
            if from_loss:
                total = jnp.sum(lacc[...], axis=-1, keepdims=True) * (0.5 / D_MODEL)
                slab_ref[LOSS_ROW:LOSS_ROW + 1, :] = jnp.broadcast_to(total, (1, D_MODEL))

    row = lambda i: (i, 0)
    fixed = lambda i: (0, 0)
    return pl.pallas_call(
        body, grid=(s // tm,),
        in_specs=[pl.BlockSpec((tm, D_MODEL), row), pl.BlockSpec((tm, D_MODEL), row), _full_spec(ln_g),
                  _full_spec(ln_b), _full_spec(w_out), pl.BlockSpec((tm, D_MIX), row)] + _tie_specs(ties),
        out_specs=(pl.BlockSpec((tm, D_MODEL), row), pl.BlockSpec((tm, D_MIX), row),
                   pl.BlockSpec((D_MIX, D_MODEL), fixed), pl.BlockSpec((8, D_MODEL), fixed)),
        out_shape=(jax.ShapeDtypeStruct((s, D_MODEL), F32), jax.ShapeDtypeStruct((s, D_MIX), F32),
                   jax.ShapeDtypeStruct((D_MIX, D_MODEL), GRAD_DTYPE), jax.ShapeDtypeStruct((8, D_MODEL), F32)),
        scratch_shapes=[pltpu.VMEM((D_MIX, D_MODEL), F32), pltpu.VMEM((1, D_MODEL), F32)],
        name="ln_out_bwd_loss" if from_loss else "ln_out_bwd", compiler_params=_params("arbitrary"),
    )(other, z, ln_g, ln_b, w_out, h, *ties)


_BRANCH_GRADS = (("g256", (8, D_G)), ("sg_w", (N_SUB, CHUNK, CHUNK)), ("sg_b", (8, CHUNK)),
                 ("pool_w", (N_SUB, HEAD_DIM, HEAD_DIM)), ("conv_a_w", (N_DEV, CONV_A, CONV_CH)),
                 ("cc_dw_w", (N_DEV, CONV_D, CONV_CH)), ("cc_pw_w", (D_G, D_G)),
                 ("dk", (N_SUB, MEM_LEN, D_G)), ("dv", (N_SUB, MEM_LEN, D_G)))
_BRANCH_ACC = (("conv_a", (CONV_A, D_G)), ("cc_dw_w", (CONV_D, D_G)), ("pool_wbd", (D_G, D_G)),
               ("sg_bias", (CHUNK, D_G)))


def _branch_bwd(proj, dh, km, vm, layer, repl, pw, ca8, dw8, tile=TILE):
    s = proj.shape[0]
    nt = s // tile
    hb = tile // HALO
    nat_arrays = [repl[n] for n in _BRANCH_REPL]
    n_nat, n_grads, n_acc = len(nat_arrays), len(_BRANCH_GRADS), len(_BRANCH_ACC)
    row_of = {n: k for k, n in enumerate(G256_ROWS)}

    def body(p_ref, ph_ref, dh_ref, km_ref, vm_ref, *rest):
        nat = dict(zip(_BRANCH_REPL, rest[:n_nat]))
        pw_ref, ca_ref, dw_ref = rest[n_nat:n_nat + 3]
        rest = rest[n_nat + 3:]
        dp_ref = rest[0]
        g = dict(zip([n for n, _ in _BRANCH_GRADS], rest[1:1 + n_grads]))
        rest = rest[1 + n_grads:]
        ext_a, ext_c, ext_d, rev_a, rev_c, rev_d, res_c, res_d, res_rc, res_rd = rest[:10]
        acc = dict(zip([n for n, _ in _BRANCH_ACC], rest[10:10 + n_acc]))
        scr = dict(zip([n for n, _ in _BRANCH_W_SCRATCH], rest[10 + n_acc:]))
        i = pl.program_id(0)
        t = nt - 1 - i

        @pl.when(i == 0)
        def _():
            for ref in list(g.values()) + list(acc.values()) + [rev_a, rev_c, rev_d]:
                ref[...] = jnp.zeros_like(ref)

        w = _branch_weights(layer, nat, pw_ref, ca_ref, dw_ref, scr, i == 0)
        r = _branch_forward(p_ref, ph_ref, t == 0, t * tile, km_ref, vm_ref, w, ext_a, ext_c, ext_d, res_c, res_d,
                            tile)

        def put(k, val, width=D_G):
            dp_ref[:, k:k + width] = val.astype(dp_ref.dtype)

        def add_row(name, val):
            k = row_of[name]
            g["g256"][k:k + 1, :] += val

        def push_rev(rev, val):
            head = rev[0:HALO]
            rev[tile:tile + HALO] = head
            rev[0:tile] = val

        dh_all = dh_ref[...]
        gate, sig_gate, concat = r["gate"], r["sig_gate"], r["concat"]
        put(C_GATE, dh_all * concat * (sig_gate * (1.0 + gate * (1.0 - sig_gate))), D_MIX)
        dconcat = dh_all * (gate * sig_gate)
        dya, dyb, dyc, dyd, dye = [dconcat[:, k * D_G:(k + 1) * D_G] for k in range(N_GROUPS)]

        put(C_BA * D_G, dya * r["conv_a"])
        dconv_a = dya * r["ba"]
        for k in range(CONV_A):
            acc["conv_a"][k:k + 1, :] += _rowsum(dconv_a * ext_a[pl.ds(HALO - 2 + k, tile), :])
        push_rev(rev_a, dconv_a)
        dga = w["conv_a"][0:1, :] * rev_a[pl.ds(2, tile), :]
        for k in range(1, CONV_A):
            dga = dga + w["conv_a"][k:k + 1, :] * rev_a[pl.ds(2 - k, tile), :]
        put(C_CA * D_G, dga * r["xa"])
        put(C_XA * D_G, dga * r["ca"])

        add_row("pool_scale", _rowsum(dyc * r["pool_mm"]))
        dmm = dyc * w["pool_scale"][...]
        acc["pool_wbd"][...] += _mm_tn(r["ypre"], dmm)
        dypre = _mm_nt(dmm, w["pool_wbd"][...])
        dws = dypre / r["cnt"]
        push_rev(rev_c, dws)
        _make_residues(rev_c, res_rc)
        run = dws
        sums = {}
        for k in range(1, POOL_WINDOWS[-1]):
            run = run + _rows_at(rev_c, res_rc, k, tile)
            if k + 1 in POOL_WINDOWS:
                sums[k + 1] = run
        put(C_XC * D_G, _pool_select(r["lane_grp"], sums[2], sums[4], sums[8], sums[16]) - dypre)

        g["cc_pw_w"][...] += _mm_tn(r["act_d"], dyd)
        dact = _mm_nt(dyd, w["cc_pw_w"][...])
        sig_ln, ln_d = r["sig_ln"], r["ln_d"]
        dln = dact * (sig_ln * (1.0 + ln_d * (1.0 - sig_ln)))
        add_row("cc_ln_g", _rowsum(dln * r["xhat_d"]))
        add_row("cc_ln_b", _rowsum(dln))
        dconv_d = _ln_bwd(dln, r["xhat_d"], r["rstd_d"], w["cc_ln_g"][...])
        add_row("cc_dw_b", _rowsum(dconv_d))
        for j in range(CONV_D):
            acc["cc_dw_w"][j:j + 1, :] += _rowsum(dconv_d * _rows_at(ext_d, res_d, HALO - (CONV_D - 1) + j, tile))
        push_rev(rev_d, dconv_d)
        _make_residues(rev_d, res_rd)
        dhd = w["cc_dw_w"][0:1, :] * _rows_at(rev_d, res_rd, CONV_D - 1, tile)
        for j in range(1, CONV_D):
            dhd = dhd + w["cc_dw_w"][j:j + 1, :] * _rows_at(rev_d, res_rd, CONV_D - 1 - j, tile)
        sig_dg = r["sig_dg"]
        put(C_DA * D_G, dhd * sig_dg)
        put(C_DG * D_G, dhd * r["da"] * sig_dg * (1.0 - sig_dg))

        dug = dyb * r["mixed"]
        dmixed = dyb * r["ug"]
        wm, lo, vn = r["wm"], r["lo"], r["vn"]
        dvn_chunks = []
        for c in range(tile // CHUNK):
            rows = slice(c * CHUNK, (c + 1) * CHUNK)
            acc["sg_bias"][...] += dmixed[rows, :]
            halves = []
            for hf in range(2):
                cols = slice(hf * 128, (hf + 1) * 128)
                dm = dmixed[rows, cols]
                dm_a, dm_b = jnp.where(lo, dm, 0.0), jnp.where(lo, 0.0, dm)
                vh = vn[rows, cols]
                g["sg_w"][2 * hf] += _mm_nt(dm_a, vh)
                g["sg_w"][2 * hf + 1] += _mm_nt(dm_b, vh)
                halves.append(_mm_tn(wm[2 * hf], dm_a) + _mm_tn(wm[2 * hf + 1], dm_b))
            dvn_chunks.append(jnp.concatenate(halves, axis=1))
        dvn = jnp.concatenate(dvn_chunks, axis=0)
        add_row("sg_ln_g", _rowsum(dvn * r["xhat_v"]))
        add_row("sg_ln_b", _rowsum(dvn))
        dvg = _ln_bwd(dvn, r["xhat_v"], r["rstd_v"], w["sg_ln_g"][...])
        put(C_V * D_G, dvg * _dgelu(r["v"], r["th_v"]))
        put(C_U * D_G, dug * _dgelu(r["u"], r["th_u"]))

        q = r["q"]
        dq = jnp.zeros((tile, D_G), F32)
        for h in range(N_SUB):
            p = r["probs"][h]
            dp = _mm_nt(dye, vm_ref[h])
            g["dv"][h] += _mm_tn(p, dye)
            ds = p * (dp - jnp.sum(dp * p, axis=-1, keepdims=True)) * ATT_SCALE
            dq = dq + _mm(ds, km_ref[h])
            g["dk"][h] += _mm_tn(ds, q)
        put(C_Q * D_G, dq)

        @pl.when(i == nt - 1)
        def _():
            for h in range(N_SUB):
                g["sg_w"][h] = jnp.where(r["tri"], g["sg_w"][h], 0.0)
            lane_head = _head_of_lane((CHUNK, D_G))
            col_of = lax.broadcasted_iota(jnp.int32, (CHUNK, 8), 1)
            ba = acc["sg_bias"][...]
            sgb_t = jnp.zeros((CHUNK, 8), F32)
            for h in range(N_SUB):
                col = jnp.sum(jnp.where(lane_head == h, ba, 0.0), axis=-1, keepdims=True)
                sgb_t = jnp.where(col_of == h, col, sgb_t)
            g["sg_b"][...] = sgb_t.T
            wbd = acc["pool_wbd"][...]
            for gi in range(N_SUB):
                sl = slice(gi * HEAD_DIM, (gi + 1) * HEAD_DIM)
                g["pool_w"][gi] = wbd[sl, sl]
            ca, dw = acc["conv_a"][...], acc["cc_dw_w"][...]
            for p in range(N_DEV):
                g["conv_a_w"][p] = ca[:, p * CONV_CH:(p + 1) * CONV_CH]
                g["cc_dw_w"][p] = dw[:, p * CONV_CH:(p + 1) * CONV_CH]

    rev = lambda i: (nt - 1 - i, 0)
    grad_specs = tuple(pl.BlockSpec(shape, lambda i, _nd=len(shape): (0,) * _nd) for _, shape in _BRANCH_GRADS)
    grad_shapes = tuple(jax.ShapeDtypeStruct(shape, F32) for _, shape in _BRANCH_GRADS)
    outs = pl.pallas_call(
        body, grid=(nt,),
        in_specs=[pl.BlockSpec((tile, D_IN), rev),
                  pl.BlockSpec((HALO, D_IN), lambda i: (jnp.maximum((nt - 1 - i) * hb - 1, 0), 0)),
                  pl.BlockSpec((tile, D_MIX), rev), _full_spec(km), _full_spec(vm)]
        + [_full_spec(a) for a in nat_arrays + [pw, ca8, dw8]],
        out_specs=(pl.BlockSpec((tile, D_IN), rev),) + grad_specs,
        out_shape=(jax.ShapeDtypeStruct((s, D_IN), MM_DTYPE),) + grad_shapes,
        scratch_shapes=[pltpu.VMEM((HALO + tile, D_G), F32)] * 6 + [_residue_scratch(tile)] * 4
        + [pltpu.VMEM(shape, F32) for _, shape in _BRANCH_ACC + _BRANCH_W_SCRATCH],
        name="branch_bwd", compiler_params=_params("arbitrary"),
    )(proj, proj, dh, km, vm, *nat_arrays, pw, ca8, dw8)
    return outs[0], dict(zip([n for n, _ in _BRANCH_GRADS], outs[1:]))


def _dx_matmul(dproj, wt_in, dz, ties=(), tm=256):
    s = dproj.shape[0]

    def body(dp_ref, w_ref, dz_ref, *rest):
        o_ref = rest[len(ties)]
        o_ref[...] = _mm(dp_ref[...], w_ref[...]) + ALPHA * dz_ref[...]

    row = lambda i: (i, 0)
    return pl.pallas_call(
        body, grid=(s // tm,),
        in_specs=[pl.BlockSpec((tm, D_IN), row), _full_spec(wt_in), pl.BlockSpec((tm, D_MODEL), row)]
        + _tie_specs(ties),
        out_specs=pl.BlockSpec((tm, D_MODEL), row),
        out_shape=jax.ShapeDtypeStruct((s, D_MODEL), F32), name="dx_mm", compiler_params=_params("arbitrary"),
    )(dproj, wt_in, dz, *ties)


def _dw_in_matmul(x, dproj, ties=(), tk=512):
    s = x.shape[0]
    tn = 2 * W_IN_COLS
    nk = s // tk

    def body(x_ref, dp_ref, *rest):
        o_ref, acc = rest[len(ties):]
        k = pl.program_id(1)

        @pl.when(k == 0)
        def _():
            acc[...] = jnp.zeros_like(acc)

        acc[...] += _mm_tn(dp_ref[...], x_ref[...])

        @pl.when(k == nk - 1)
        def _():
            o_ref[...] = acc[...].astype(o_ref.dtype)

    return pl.pallas_call(
        body, grid=(D_IN // tn, nk),
        in_specs=[pl.BlockSpec((tk, D_MODEL), lambda j, k: (k, 0)), pl.BlockSpec((tk, tn), lambda j, k: (k, j))]
        + _tie_specs(ties),
        out_specs=pl.BlockSpec((tn, D_MODEL), lambda j, k: (j, 0)),
        out_shape=jax.ShapeDtypeStruct((D_IN, D_MODEL), GRAD_DTYPE),
        scratch_shapes=[pltpu.VMEM((tn, D_MODEL), F32)], name="dw_in_mm",
        compiler_params=_params("arbitrary", "arbitrary"),
    )(x, dproj, *ties)


def _in_bwd(x, dproj, wt_in, dz, tm=256):
    s = x.shape[0]
    n_steps = s // tm

    assert wt_in.dtype == GRAD_DTYPE
    blk = 2 * W_IN_COLS

    def body(x_ref, dp_ref, w_hbm, dz_ref, o_ref, gw_hbm, w_vmem, acc, sem):
        i = pl.program_id(0)

        @pl.when(i == 0)
        def _():
            fetch = pltpu.make_async_copy(w_hbm, w_vmem, sem)
            fetch.start()
            acc[...] = jnp.zeros_like(acc)
            fetch.wait()

        o_ref[...] = _mm(dp_ref[...], w_vmem[...]) + ALPHA * dz_ref[...]
        xb = x_ref[...].astype(MM_DTYPE)
        for j in range(D_IN // blk):
            acc[j * blk:(j + 1) * blk, :] += _mm_tn(dp_ref[:, j * blk:(j + 1) * blk], xb)

        @pl.when(i == n_steps - 1)
        def _():
            w_vmem[...] = acc[...].astype(w_vmem.dtype)
            emit = pltpu.make_async_copy(w_vmem, gw_hbm, sem)
            emit.start()
            emit.wait()

    row = lambda i: (i, 0)
    any_spec = pl.BlockSpec(memory_space=pl.ANY)
    return pl.pallas_call(
        body, grid=(n_steps,),
        in_specs=[pl.BlockSpec((tm, D_MODEL), row), pl.BlockSpec((tm, D_IN), row), any_spec,
                  pl.BlockSpec((tm, D_MODEL), row)],
        out_specs=(pl.BlockSpec((tm, D_MODEL), row), any_spec),
        out_shape=(jax.ShapeDtypeStruct((s, D_MODEL), F32), jax.ShapeDtypeStruct((D_IN, D_MODEL), GRAD_DTYPE)),
        scratch_shapes=[pltpu.VMEM((D_IN, D_MODEL), wt_in.dtype), pltpu.VMEM((D_IN, D_MODEL), F32),
                        pltpu.SemaphoreType.DMA],
        name="in_bwd", compiler_params=_params("arbitrary"),
    )(x, dproj, wt_in, dz)


def _kv_bwd(mem, dk, dv):
    def body(mem_ref, dk_ref, dv_ref, o_ref):
        grp = _head_of_lane((MEM_LEN, D_G))
        dk_sum = jnp.zeros((MEM_LEN, D_G), F32)
        dv_sum = jnp.zeros((MEM_LEN, D_G), F32)
        for h in range(N_SUB):
            dk_sum = dk_sum + jnp.where(grp == h, dk_ref[h], 0.0)
            dv_sum = dv_sum + jnp.where(grp == h, dv_ref[h], 0.0)
        o_ref[...] = _mm_tn(mem_ref[...], jnp.concatenate([dk_sum, dv_sum], axis=1)).astype(o_ref.dtype)

    return pl.pallas_call(body, out_shape=jax.ShapeDtypeStruct((D_MODEL, 2 * D_G), GRAD_DTYPE), name="kv_bwd",
                          compiler_params=_params())(mem, dk, dv)


def _layer_fwd_a(x, mem, gw, ties=()):
    km, vm = _kv_project(mem, gw["w_kv"])
    return km, vm, _proj_matmul(x, gw["wt_in"], ties)


def _layer_fwd_b(x, proj, km, vm, layer, repl, gw, ties=()):
    h = _branch_fwd(proj, km, vm, layer, repl, gw["pw"], gw["ca8"], gw["dw8"], ties)
    z, xn = _out_matmul_ln(h, gw["w_out"], x, repl["ln_g"], repl["ln_b"], layer)
    return xn, (x, proj, h, z, km, vm)


def _layer_bwd_a(up, target, mem, layer, repl, gw, saved, ties=()):
    x_in, proj, h, z, km, vm = saved
    dz, dh, g_w_out, g1024 = _ln_out_bwd(up, target, z, repl["ln_g"], repl["ln_b"], layer, gw["w_out"], h, ties)
    dproj, bg = _branch_bwd(proj, dh, km, vm, layer, repl, gw["pw"], gw["ca8"], gw["dw8"])
    grads = {n: bg[n] for n in ("g256", "sg_w", "sg_b", "pool_w", "conv_a_w", "cc_dw_w")}
    grads.update(w_out=g_w_out.reshape(N_DEV, D_MIX // N_DEV, D_MODEL), g1024=g1024,
                 w_kv=_kv_bwd(mem, bg["dk"], bg["dv"]).reshape(N_DEV, D_MODEL // N_DEV, 2 * D_G),
                 cc_pw_w=bg["cc_pw_w"].reshape(N_DEV, CONV_CH, D_G))
    return dz, dproj, grads


def _landing_shapes(items):
    out = []
    for a, scatter, pick in items:
        shape = a.shape if scatter else (N_DEV,) + (a.shape if pick is None else a.shape[1:])
        out.append(jax.ShapeDtypeStruct(shape, a.dtype))
    return tuple(out)


def _exchange_sems(n):
    return [pltpu.SemaphoreType.DMA(((N_DEV - 1) * n,)), pltpu.SemaphoreType.DMA(((N_DEV - 1) * n,)),
            pltpu.SemaphoreType.DMA((n,))]


def _exchange_copies(modes, ins, outs, send_sems, recv_sems, local_sems):
    n = len(ins)
    x, y, c = lax.axis_index("x"), lax.axis_index("y"), lax.axis_index("c")
    me = 4 * x + 2 * y + c

    def src_of(a, dest):
        scatter, pick = modes[a]
        if scatter:
            return ins[a].at[dest]
        return ins[a] if pick is None else ins[a].at[pick]

    local = [pltpu.make_async_copy(src_of(a, me), outs[a].at[me], local_sems.at[a]) for a in range(n)]
    sends, recvs = [], []
    for k in range(1, N_DEV):
        px = 1 - x if k & 4 else x
        py = 1 - y if k & 2 else y
        pc = 1 - c if k & 1 else c
        peer = 4 * px + 2 * py + pc
        for a in range(n):
            sems = dict(send_sem=send_sems.at[(k - 1) * n + a], recv_sem=recv_sems.at[(k - 1) * n + a],
                        device_id=(px, py, pc), device_id_type=pl.DeviceIdType.MESH)
            sends.append(pltpu.make_async_remote_copy(src_ref=src_of(a, peer), dst_ref=outs[a].at[me], **sems))
            recvs.append(pltpu.make_async_remote_copy(src_ref=src_of(a, peer), dst_ref=outs[a].at[peer], **sems))
    return local, sends, recvs


def _exchange(items, name):
    n = len(items)
    modes = [(scatter, pick) for _, scatter, pick in items]

    def body(*refs):
        local, sends, recvs = _exchange_copies(modes, refs[:n], refs[n:2 * n], *refs[2 * n:])
        for cp in local + sends:
            cp.start()
        for cp in recvs:
            cp.wait_recv()
        for cp in sends:
            cp.wait_send()
        for cp in local:
            cp.wait()

    any_spec = pl.BlockSpec(memory_space=pl.ANY)
    return pl.pallas_call(
        body, in_specs=[any_spec] * n, out_specs=(any_spec,) * n, out_shape=_landing_shapes(items),
        scratch_shapes=_exchange_sems(n), name=name,
    )(*[a for a, _, _ in items])


def _gather_two_level(items, name):
    n = len(items)
    assert not any(scatter for _, scatter, _ in items)
    picks = [pick for _, _, pick in items]

    def body(*refs):
        ins, outs = refs[:n], refs[n:2 * n]
        send_sems, recv_sems, local_sems = refs[2 * n:]
        x, y, c = lax.axis_index("x"), lax.axis_index("y"), lax.axis_index("c")
        sib = 1 - c
        chips = [(1 - x, y), (x, 1 - y), (1 - x, 1 - y)]

        def slot(a, px, py, pc):
            return outs[a].at[4 * px + 2 * py + pc]

        def copy(k, a, src, block, to):
            return pltpu.make_async_remote_copy(
                src_ref=src, dst_ref=slot(a, *block), send_sem=send_sems.at[k * n + a],
                recv_sem=recv_sems.at[k * n + a], device_id=to, device_id_type=pl.DeviceIdType.MESH)

        own = [ins[a] if picks[a] is None else ins[a].at[picks[a]] for a in range(n)]
        local = [pltpu.make_async_copy(own[a], slot(a, x, y, c), local_sems.at[a]) for a in range(n)]
        first = [copy(0, a, own[a], (x, y, c), (x, y, sib)) for a in range(n)]
        first += [copy(1 + j, a, own[a], (x, y, c), (*chip, c)) for j, chip in enumerate(chips) for a in range(n)]
        for cp in local + first:
            cp.start()
        passed = []
        for j, chip in enumerate(chips):
            for a in range(n):
                copy(1 + j, a, own[a], (*chip, c), (x, y, c)).wait_recv()
                fwd = copy(4 + j, a, slot(a, *chip, c), (*chip, c), (x, y, sib))
                fwd.start()
                passed.append(fwd)
        for a in range(n):
            copy(0, a, own[a], (x, y, sib), (x, y, c)).wait_recv()
        for j, chip in enumerate(chips):
            for a in range(n):
                copy(4 + j, a, own[a], (*chip, sib), (x, y, c)).wait_recv()
        for cp in first + passed:
            cp.wait_send()
        for cp in local:
            cp.wait()

    any_spec = pl.BlockSpec(memory_space=pl.ANY)
    return pl.pallas_call(
        body, in_specs=[any_spec] * n, out_specs=(any_spec,) * n, out_shape=_landing_shapes(items),
        scratch_shapes=[pltpu.SemaphoreType.DMA((7 * n,)), pltpu.SemaphoreType.DMA((7 * n,)),
                        pltpu.SemaphoreType.DMA((n,))],
        name=name,
    )(*[a for a, _, _ in items])


_HBM_SPEC = pl.BlockSpec(memory_space=pltpu.HBM)
_SEM_SPEC = pl.BlockSpec(memory_space=pltpu.SEMAPHORE)
_SPLIT_PARAMS = pltpu.CompilerParams(has_side_effects=pltpu.SideEffectType.DATAFLOW_SIDE_EFFECTING)


def _exchange_start(items, name):
    n = len(items)
    modes = [(scatter, pick) for _, scatter, pick in items]
    shapes = _landing_shapes(items)
    lands = [pltpu.with_memory_space_constraint(lax.empty(s.shape, s.dtype), pltpu.HBM) for s in shapes]
    srcs = [pltpu.with_memory_space_constraint(a, pltpu.HBM) for a, _, _ in items]

    def body(*refs):
        local, sends, _ = _exchange_copies(modes, refs[:n], refs[n:2 * n], *refs[2 * n:2 * n + 3])
        for cp in local + sends:
            cp.start()
        token = refs[-1]
        token[...] = jnp.zeros_like(token)

    res = pl.pallas_call(
        body, name=name, in_specs=[_HBM_SPEC] * (2 * n),
        out_shape=tuple(_exchange_sems(n)) + tuple(pltpu.HBM(a.shape, a.dtype) for a in srcs)
        + tuple(pltpu.HBM(s.shape, s.dtype) for s in shapes) + (jax.ShapeDtypeStruct((8, 128), F32),),
        out_specs=(_SEM_SPEC,) * 3 + (_HBM_SPEC,) * (2 * n) + (pl.BlockSpec(memory_space=pltpu.VMEM),),
        input_output_aliases={i: 3 + i for i in range(2 * n)}, compiler_params=_SPLIT_PARAMS,
    )(*srcs, *lands)
    return dict(sems=res[:3], srcs=res[3:3 + n], lands=res[3 + n:3 + 2 * n], token=res[-1], modes=modes)


def _exchange_wait(ticket, after, name):
    n = len(ticket["srcs"])
    modes = ticket["modes"]

    def body(*refs):
        local, sends, recvs = _exchange_copies(modes, refs[:n], refs[n:2 * n], *refs[2 * n:2 * n + 3])
        for cp in recvs:
            cp.wait_recv()
        for cp in sends:
            cp.wait_send()
        for cp in local:
            cp.wait()

    both = list(ticket["srcs"]) + list(ticket["lands"])
    res = pl.pallas_call(
        body, name=name, in_specs=[_HBM_SPEC] * (2 * n) + [_SEM_SPEC] * 3 + [pl.BlockSpec(memory_space=pl.ANY)],
        out_shape=tuple(pltpu.HBM(a.shape, a.dtype) for a in both), out_specs=(_HBM_SPEC,) * (2 * n),
        input_output_aliases={i: i for i in range(2 * n)}, compiler_params=_SPLIT_PARAMS,
    )(*both, *ticket["sems"], after)
    return res[n:]


def _adam_math(g, w, m, v):
    m_new = ADAM_B1 * m + (1.0 - ADAM_B1) * g
    v_new = ADAM_B2 * v + (1.0 - ADAM_B2) * (g * g)
    m_hat = m_new / (1.0 - ADAM_B1 ** ADAM_STEP)
    v_hat = v_new / (1.0 - ADAM_B2 ** ADAM_STEP)
    return -ADAM_LR * (m_hat / (jnp.sqrt(v_hat) + ADAM_EPS) + ADAM_WD * w), m_new, v_new


def _adamw_big(parts, w, m, v, layer, prev, name, tr):
    depth, rows, cols = w.shape

    def body(p_ref, w_ref, m_ref, v_ref, *rest):
        g_out, d_out, m_out, v_out = rest[len(prev):]
        g = p_ref[0].astype(F32)
        for q in range(1, N_DEV):
            g = g + p_ref[q].astype(F32)
        d, m_new, v_new = _adam_math(g, w_ref[...], m_ref[...], v_ref[...])
        g_out[...] = g
        d_out[...] = d
        m_out[...] = m_new
        v_out[...] = v_new

    blk = pl.BlockSpec((None, tr, cols), lambda i: (layer, i, 0))
    shp = jax.ShapeDtypeStruct((depth, rows, cols), F32)
    return pl.pallas_call(
        body, grid=(rows // tr,),
        in_specs=[pl.BlockSpec((N_DEV, tr, cols), lambda i: (0, i, 0)), blk, blk, blk]
        + [pl.BlockSpec(memory_space=pl.ANY)] * len(prev),
        out_specs=(blk,) * 4, out_shape=(shp,) * 4,
        input_output_aliases={4 + j: j for j in range(len(prev))},
        name=name, compiler_params=_params("arbitrary"),
    )(parts, w, m, v, *prev)


_SMALL_TENSORS = (("conv_a_w", "conv_a_w", None), ("cc_dw_w", "cc_dw_w", None), ("cc_pw_w", "cc_pw_w", None),
                  ("sg_w", "sg_w", None), ("pool_w", "pool_w", None), ("sg_b", "sg_b", None)) \
    + tuple((n, "g256", k) for k, n in enumerate(G256_ROWS)) + tuple((n, "g1024", k) for k, n in enumerate(G1024_ROWS))
_SMALL_LANDINGS = ("conv_a_w", "cc_dw_w", "cc_pw_w", "sg_w", "pool_w", "sg_b", "g256", "g1024")
_TAPS_FIRST = ("conv_a_w", "cc_dw_w")


def _adamw_small(landings, wts, mom, var):
    names = [n for n, _, _ in _SMALL_TENSORS]
    n_land = DEPTH * len(_SMALL_LANDINGS)
    n_t = len(names)

    def body(*refs):
        land = [dict(zip(_SMALL_LANDINGS, refs[l * len(_SMALL_LANDINGS):(l + 1) * len(_SMALL_LANDINGS)]))
                for l in range(DEPTH)]
        w_refs = dict(zip(names, refs[n_land:n_land + n_t]))
        m_refs = dict(zip(names, refs[n_land + n_t:n_land + 2 * n_t]))
        v_refs = dict(zip(names, refs[n_land + 2 * n_t:n_land + 3 * n_t]))
        outs = refs[n_land + 3 * n_t:]
        out_refs = {n: outs[4 * k:4 * k + 4] for k, n in enumerate(names)}
        loss_ref = outs[4 * n_t]
        for name, key, row in _SMALL_TENSORS:
            for l in range(DEPTH):
                src = land[l][key]
                if row is not None:
                    part = lambda q: src[q, row:row + 1, :]
                    at = (slice(l, l + 1),)
                elif name == "sg_b":
                    part = lambda q: src[q, 0:N_SUB, :]
                    at = (l,)
                elif name in _TAPS_FIRST:
                    part = lambda q: src[q]
                    at = (slice(None), l)
                else:
                    part = lambda q: src[q]
                    at = (l,)
                g = part(0)
                for q in range(1, N_DEV):
                    g = g + part(q)
                d, m_new, v_new = _adam_math(g, w_refs[name][at], m_refs[name][at], v_refs[name][at])
                for ref, val in zip(out_refs[name], (g, d, m_new, v_new)):
                    ref[at] = val
        src = land[DEPTH - 1]["g1024"]
        loss = src[0, LOSS_ROW:LOSS_ROW + 1, 0:128]
        for q in range(1, N_DEV):
            loss = loss + src[q, LOSS_ROW:LOSS_ROW + 1, 0:128]
        loss_ref[...] = loss

    ins = [landings[l][k] for l in range(DEPTH) for k in _SMALL_LANDINGS] \
        + [src[n] for src in (wts, mom, var) for n in names]
    out_shape = tuple(jax.ShapeDtypeStruct(wts[n].shape, F32) for n in names for _ in range(4)) \
        + (jax.ShapeDtypeStruct((1, 128), F32),)
    res = pl.pallas_call(body, out_shape=out_shape, name="adamw_small", compiler_params=_params())(*ins)
    return {n: res[4 * k:4 * k + 4] for k, n in enumerate(names)}, res[4 * n_t]


_BIG = (("w_in", 64), ("w_out", 32), ("w_kv", 32))
_GRAD_ITEMS_EARLY = ("w_out", "w_kv", "cc_pw_w", "conv_a_w", "cc_dw_w")
_GRAD_ITEMS_REPL = ("g256", "sg_w", "sg_b", "pool_w", "g1024")


def _grad_items(grads, with_w_in):
    items = [(grads[n], True, None) for n in (("w_in",) if with_w_in else ()) + _GRAD_ITEMS_EARLY]
    return items + [(grads[n], False, None) for n in _GRAD_ITEMS_REPL]


def _landed(parts, with_w_in):
    names = (("w_in",) if with_w_in else ()) + _GRAD_ITEMS_EARLY + _GRAD_ITEMS_REPL
    return dict(zip(names, parts))


def _gathered_weights(wt_in8, w_kv8, w_out8, pw8, ca8, dw8):
    return dict(wt_in=wt_in8.reshape(D_IN, D_MODEL), w_kv=w_kv8.reshape(D_MODEL, 2 * D_G),
                w_out=w_out8.reshape(D_MIX, D_MODEL), pw=pw8.reshape(D_G, D_G), ca8=ca8, dw8=dw8)


def kernel(x, mem, w_in, conv_a_w, sg_ln_g, sg_ln_b, sg_w, sg_b, pool_w, pool_scale, cc_dw_w, cc_dw_b, cc_ln_g, cc_ln_b, cc_pw_w, w_kv, w_out, ln_g, ln_b, loss_target, m_w_in, m_conv_a_w, m_sg_ln_g, m_sg_ln_b, m_sg_w, m_sg_b, m_pool_w, m_pool_scale, m_cc_dw_w, m_cc_dw_b, m_cc_ln_g, m_cc_ln_b, m_cc_pw_w, m_w_kv, m_w_out, m_ln_g, m_ln_b, v_w_in, v_conv_a_w, v_sg_ln_g, v_sg_ln_b, v_sg_w, v_sg_b, v_pool_w, v_pool_scale, v_cc_dw_w, v_cc_dw_b, v_cc_ln_g, v_cc_ln_b, v_cc_pw_w, v_w_kv, v_w_out, v_ln_g, v_ln_b):
    names = ("w_in", "conv_a_w", "sg_ln_g", "sg_ln_b", "sg_w", "sg_b", "pool_w", "pool_scale", "cc_dw_w", "cc_dw_b",
             "cc_ln_g", "cc_ln_b", "cc_pw_w", "w_kv", "w_out", "ln_g", "ln_b")
    wts = dict(zip(names, (w_in, conv_a_w, sg_ln_g, sg_ln_b, sg_w, sg_b, pool_w, pool_scale, cc_dw_w, cc_dw_b,
                           cc_ln_g, cc_ln_b, cc_pw_w, w_kv, w_out, ln_g, ln_b)))
    mom = dict(zip(names, (m_w_in, m_conv_a_w, m_sg_ln_g, m_sg_ln_b, m_sg_w, m_sg_b, m_pool_w, m_pool_scale,
                           m_cc_dw_w, m_cc_dw_b, m_cc_ln_g, m_cc_ln_b, m_cc_pw_w, m_w_kv, m_w_out, m_ln_g, m_ln_b)))
    var = dict(zip(names, (v_w_in, v_conv_a_w, v_sg_ln_g, v_sg_ln_b, v_sg_w, v_sg_b, v_pool_w, v_pool_scale,
                           v_cc_dw_w, v_cc_dw_b, v_cc_ln_g, v_cc_ln_b, v_cc_pw_w, v_w_kv, v_w_out, v_ln_g, v_ln_b)))
    repl = wts
    xs, mems, tgt = x[0], mem[0], loss_target[0]
    turned = {"w_in": (0, 2, 1), "conv_a_w": (1, 0, 2), "cc_dw_w": (1, 0, 2)}
    wts, mom, var = [{n: (jnp.transpose(a, turned[n]) if n in turned else a) for n, a in src.items()}
                     for src in (wts, mom, var)]
    wb = {n: wts[n].astype(MM_DTYPE) for n in ("w_in", "w_kv", "w_out", "cc_pw_w")}

    wt8_0, wkv8_0 = _gather_two_level([(wb["w_in"], False, 0), (wb["w_kv"], False, 0)], "gather_weights_0a")
    rest_0 = _exchange_start([(wb["w_out"], False, 0), (wb["cc_pw_w"], False, 0), (wts["conv_a_w"], False, None),
                              (wts["cc_dw_w"], False, None)], "gather_weights_0b_start")
    km0, vm0, proj0 = _layer_fwd_a(xs, mems, dict(wt_in=wt8_0.reshape(D_IN, D_MODEL),
                                                  w_kv=wkv8_0.reshape(D_MODEL, 2 * D_G)), (rest_0["token"],))
    wo8_0, pw8_0, ca8, dw8 = _exchange_wait(rest_0, proj0, "gather_weights_0b_wait")
    gw0 = _gathered_weights(wt8_0, wkv8_0, wo8_0, pw8_0, ca8, dw8)
    all_1 = _exchange_start([(wb["w_in"], False, 1), (wb["w_kv"], False, 1), (wb["w_out"], False, 1),
                             (wb["cc_pw_w"], False, 1)], "gather_weights_1_start")
    x1, saved0 = _layer_fwd_b(xs, proj0, km0, vm0, 0, repl, gw0, (all_1["token"],))
    gw1 = _gathered_weights(*_exchange_wait(all_1, x1, "gather_weights_1_wait"), ca8, dw8)
    km1, vm1, proj1 = _layer_fwd_a(x1, mems, gw1)
    _, saved1 = _layer_fwd_b(x1, proj1, km1, vm1, 1, repl, gw1)

    dz1, dproj1, g1 = _layer_bwd_a(None, tgt, mems, 1, repl, gw1, saved1)
    shards = lambda g: g.reshape(N_DEV, W_IN_COLS, D_MODEL)
    up, g_wt_in_1 = _in_bwd(saved1[0], dproj1, gw1["wt_in"], dz1)
    g1["w_in"] = shards(g_wt_in_1)
    grads_1 = _exchange_start(_grad_items(g1, True), "exchange_grads_1_start")
    dz0, dproj0, g0 = _layer_bwd_a(up, None, mems, 0, repl, gw0, saved0, (grads_1["token"],))
    early_0 = _exchange_start(_grad_items(g0, False), "exchange_grads_0a_start")
    g_wt_in_0 = _dw_in_matmul(saved0[0], dproj0, (early_0["token"],))
    late_0 = _exchange_start([(shards(g_wt_in_0), True, None)], "exchange_grads_0b_start")
    grad_x = _dx_matmul(dproj0, gw0["wt_in"], dz0, (late_0["token"],))

    landed = [None, _landed(_exchange_wait(grads_1, grad_x, "exchange_grads_1_wait"), True)]
    big = {}
    for n, tr in _BIG:
        big[n] = _adamw_big(landed[1][n], wts[n], mom[n], var[n], 1, (), "adamw_" + n + "_1", tr)
    landed[0] = _landed(_exchange_wait(early_0, big["w_kv"][0], "exchange_grads_0a_wait"), False)
    for n, tr in _BIG[1:]:
        big[n] = _adamw_big(landed[0][n], wts[n], mom[n], var[n], 0, big[n], "adamw_" + n + "_0", tr)
    (landed[0]["w_in"],) = _exchange_wait(late_0, big["w_kv"][0], "exchange_grads_0b_wait")
    big["w_in"] = _adamw_big(landed[0]["w_in"], wts["w_in"], mom["w_in"], var["w_in"], 0, big["w_in"],
                             "adamw_w_in_0", _BIG[0][1])
    small, loss = _adamw_small(landed, wts, mom, var)

    res = {**small, **big}
    res = {n: ([jnp.transpose(a, turned[n]) for a in r] if n in turned else r) for n, r in res.items()}
    return (loss[0, 0], grad_x[None], *[res[n][0] for n in names], *[res[n][1] for n in names],
            *[res[n][2] for n in names], *[res[n][3] for n in names])
```

```python
import math

import jax
import jax.numpy as jnp
from jax import lax
from jax.experimental import pallas as pl
from jax.experimental.pallas import tpu as pltpu

F32 = jnp.float32
MM_DTYPE = jnp.bfloat16
GRAD_DTYPE = jnp.bfloat16

D_MODEL = 1024
DEPTH = 2
D_G = 256
N_GROUPS = 5
D_MIX = N_GROUPS * D_G
N_SUB = 4
HEAD_DIM = D_G // N_SUB
CONV_A = 3
CONV_D = 31
CHUNK = 128
POOL_WINDOWS = (2, 4, 8, 16)
MEM_LEN = 256
LN_EPS = 1e-5
ALPHA = (2.0 * DEPTH) ** 0.25
D_IN = 9 * D_G + D_MIX
ATT_SCALE = 1.0 / math.sqrt(HEAD_DIM)

ADAM_LR = 0.001
ADAM_B1 = 0.9
ADAM_B2 = 0.999
ADAM_EPS = 1e-08
ADAM_WD = 0.01
ADAM_STEP = 10

N_DEV = 8
W_IN_COLS = D_IN // N_DEV
CONV_CH = D_G // N_DEV
HALO = 32
TILE = 256
VMEM_LIMIT = 56 * 1024 * 1024

C_XA, C_BA, C_CA, C_U, C_V, C_XC, C_DA, C_DG, C_Q = range(9)
C_GATE = 9 * D_G

G256_ROWS = ("sg_ln_g", "sg_ln_b", "pool_scale", "cc_dw_b", "cc_ln_g", "cc_ln_b")
G1024_ROWS = ("ln_g", "ln_b")
LOSS_ROW = 2


def _mm(a, b):
    return jnp.dot(a.astype(MM_DTYPE), b.astype(MM_DTYPE), preferred_element_type=F32)


def _mm_nt(a, b):
    return lax.dot_general(a.astype(MM_DTYPE), b.astype(MM_DTYPE), (((1,), (1,)), ((), ())),
                           preferred_element_type=F32)


def _mm_tn(a, b):
    return lax.dot_general(a.astype(MM_DTYPE), b.astype(MM_DTYPE), (((0,), (0,)), ((), ())),
                           preferred_element_type=F32)


def _sigmoid(x):
    return 0.5 * jnp.tanh(0.5 * x) + 0.5


_GELU_C = math.sqrt(2.0 / math.pi)
_GELU_A = 0.044715


def _gelu(x):
    th = jnp.tanh(_GELU_C * (x + _GELU_A * (x * x * x)))
    return 0.5 * x * (1.0 + th), th


def _dgelu(x, th):
    return 0.5 * (1.0 + th) + 0.5 * x * (1.0 - th * th) * (_GELU_C * (1.0 + 3.0 * _GELU_A * (x * x)))


def _ln_fwd(x, g, b):
    mu = jnp.mean(x, axis=-1, keepdims=True)
    xc = x - mu
    var = jnp.mean(xc * xc, axis=-1, keepdims=True)
    rstd = lax.rsqrt(var + LN_EPS)
    xhat = xc * rstd
    return xhat * g + b, xhat, rstd


def _ln_bwd(dy, xhat, rstd, g):
    dxhat = dy * g
    m1 = jnp.mean(dxhat, axis=-1, keepdims=True)
    m2 = jnp.mean(dxhat * xhat, axis=-1, keepdims=True)
    return rstd * (dxhat - m1 - xhat * m2)


def _rowsum(x):
    return jnp.sum(x, axis=0, keepdims=True)


def _col(ref, k):
    return ref[:, k * D_G:(k + 1) * D_G]


def _head_of_lane(shape):
    return jnp.right_shift(lax.broadcasted_iota(jnp.int32, shape, len(shape) - 1), HEAD_DIM.bit_length() - 1)


def _pool_select(lane_grp, s2, s4, s8, s16):
    return jnp.where(lane_grp == 0, s2, jnp.where(lane_grp == 1, s4, jnp.where(lane_grp == 2, s8, s16)))


def _row_view(ref, layer):
    return ref.at[pl.ds(layer, 1)]


def _make_residues(ext_ref, res_ref):
    rows = res_ref.shape[1]
    for r in range(1, 8):
        res_ref[r - 1] = ext_ref[pl.ds(r, rows), :]


def _rows_at(ext_ref, res_ref, off, tile):
    a, r = divmod(off, 8)
    if r == 0:
        return ext_ref[pl.ds(off, tile), :]
    return res_ref[r - 1, pl.ds(8 * a, tile), :]


def _residue_scratch(tile):
    return pltpu.VMEM((7, HALO + tile - 8, D_G), F32)


def _branch_forward(p_ref, ph_ref, first, row0, km_ref, vm_ref, w, ext_a, ext_c, ext_d, res_c, res_d, tile,
                    conv_d_ref=None):
    r = {}
    xa, ba, ca = _col(p_ref, C_XA), _col(p_ref, C_BA), _col(p_ref, C_CA)
    ext_a[0:HALO] = jnp.where(first, 0.0, _col(ph_ref, C_CA) * _col(ph_ref, C_XA))
    ext_a[HALO:HALO + tile] = ca * xa
    conv_a = w["conv_a"][0:1, :] * ext_a[pl.ds(HALO - 2, tile), :]
    for k in range(1, CONV_A):
        conv_a = conv_a + w["conv_a"][k:k + 1, :] * ext_a[pl.ds(HALO - 2 + k, tile), :]
    r.update(xa=xa, ba=ba, ca=ca, conv_a=conv_a)
    ya = ba * conv_a

    xc = _col(p_ref, C_XC)
    ext_c[0:HALO] = jnp.where(first, 0.0, _col(ph_ref, C_XC))
    ext_c[HALO:HALO + tile] = xc
    _make_residues(ext_c, res_c)
    acc = xc
    sums = {}
    for k in range(1, POOL_WINDOWS[-1]):
        acc = acc + _rows_at(ext_c, res_c, HALO - k, tile)
        if k + 1 in POOL_WINDOWS:
            sums[k + 1] = acc
    lane_grp = _head_of_lane((tile, D_G))
    trow = row0 + lax.broadcasted_iota(jnp.int32, (tile, D_G), 0)
    win = _pool_select(lane_grp, 2, 4, 8, 16)
    cnt = jnp.minimum(trow + 1, win).astype(F32)
    inv_cnt = 1.0 / cnt
    ypre = _pool_select(lane_grp, sums[2], sums[4], sums[8], sums[16]) * inv_cnt - xc
    pool_mm = _mm(ypre, w["pool_wbd"][...])
    yc = pool_mm * w["pool_scale"][...]
    r.update(lane_grp=lane_grp, inv_cnt=inv_cnt, ypre=ypre, pool_mm=pool_mm)

    da, dg = _col(p_ref, C_DA), _col(p_ref, C_DG)
    sig_dg = _sigmoid(dg)
    ext_d[0:HALO] = jnp.where(first, 0.0, _col(ph_ref, C_DA) * _sigmoid(_col(ph_ref, C_DG)))
    ext_d[HALO:HALO + tile] = da * sig_dg
    _make_residues(ext_d, res_d)
    if conv_d_ref is None:
        conv_d = w["cc_dw_b"][...] + w["cc_dw_w"][0:1, :] * _rows_at(ext_d, res_d, HALO - (CONV_D - 1), tile)
        for j in range(1, CONV_D):
            conv_d = conv_d + w["cc_dw_w"][j:j + 1, :] * _rows_at(ext_d, res_d, HALO - (CONV_D - 1) + j, tile)
    else:
        conv_d = conv_d_ref[...]
    r["conv_d"] = conv_d
    ln_d, xhat_d, rstd_d = _ln_fwd(conv_d, w["cc_ln_g"][...], w["cc_ln_b"][...])
    sig_ln = _sigmoid(ln_d)
    act_d = ln_d * sig_ln
    yd = _mm(act_d, w["cc_pw_w"][...])
    r.update(da=da, sig_dg=sig_dg, ln_d=ln_d, xhat_d=xhat_d, rstd_d=rstd_d, sig_ln=sig_ln, act_d=act_d)

    u, v = _col(p_ref, C_U), _col(p_ref, C_V)
    ug, th_u = _gelu(u)
    vg, th_v = _gelu(v)
    vn, xhat_v, rstd_v = _ln_fwd(vg, w["sg_ln_g"][...], w["sg_ln_b"][...])
    tri = (lax.broadcasted_iota(jnp.int32, (CHUNK, CHUNK), 0)
           >= lax.broadcasted_iota(jnp.int32, (CHUNK, CHUNK), 1))
    wm = [jnp.where(tri, w["sg_w"][h], 0.0).astype(MM_DTYPE) for h in range(N_SUB)]
    lo = lax.broadcasted_iota(jnp.int32, (CHUNK, 2 * HEAD_DIM), 1) < HEAD_DIM
    chunks = []
    for c in range(tile // CHUNK):
        halves = []
        for hf in range(2):
            vh = vn[c * CHUNK:(c + 1) * CHUNK, hf * 128:(hf + 1) * 128]
            halves.append(_mm(wm[2 * hf], jnp.where(lo, vh, 0.0)) + _mm(wm[2 * hf + 1], jnp.where(lo, 0.0, vh)))
        chunks.append(jnp.concatenate(halves, axis=1) + w["sg_bias"][...])
    mixed = jnp.concatenate(chunks, axis=0)
    yb = ug * mixed
    r.update(u=u, v=v, ug=ug, th_u=th_u, th_v=th_v, vn=vn, xhat_v=xhat_v, rstd_v=rstd_v, wm=wm, lo=lo,
             mixed=mixed, tri=tri)

    q = _col(p_ref, C_Q)
    ye = jnp.zeros((tile, D_G), F32)
    probs = []
    for h in range(N_SUB):
        s = _mm_nt(q, km_ref[h]) * ATT_SCALE
        e = jnp.exp(s - jnp.max(s, axis=-1, keepdims=True))
        p = e * (1.0 / jnp.sum(e, axis=-1, keepdims=True))
        probs.append(p)
        ye = ye + _mm(p, vm_ref[h])
    r.update(q=q, probs=probs)

    gate = p_ref[:, C_GATE:C_GATE + D_MIX]
    sig_gate = _sigmoid(gate)
    concat = jnp.concatenate([ya, yb, yc, yd, ye], axis=1)
    r.update(gate=gate, sig_gate=sig_gate, concat=concat)
    return r


_BRANCH_REPL = ("sg_ln_g", "sg_ln_b", "sg_w", "sg_b", "pool_w", "pool_scale", "cc_dw_b", "cc_ln_g", "cc_ln_b")
_BRANCH_W_SCRATCH = (("conv_a", (CONV_A, D_G)), ("cc_dw_w", (CONV_D, D_G)), ("sg_bias", (CHUNK, D_G)),
                     ("pool_wbd", (D_G, D_G)), ("sgb8", (8, CHUNK)))


def _branch_weights(layer, nat, pw_ref, ca_ref, dw_ref, scr, init):
    @pl.when(init)
    def _():
        for p in range(N_DEV):
            scr["conv_a"][:, p * CONV_CH:(p + 1) * CONV_CH] = ca_ref[p, :, layer, :]
            scr["cc_dw_w"][:, p * CONV_CH:(p + 1) * CONV_CH] = dw_ref[p, :, layer, :]
        scr["sgb8"][...] = jnp.zeros((8, CHUNK), F32)
        scr["sgb8"][0:N_SUB] = nat["sg_b"][layer]
        sgb_t = scr["sgb8"][...].T
        head = _head_of_lane((CHUNK, D_G))
        bias = jnp.zeros((CHUNK, D_G), F32)
        for h in range(N_SUB):
            bias = jnp.where(head == h, sgb_t[:, h:h + 1], bias)
        scr["sg_bias"][...] = bias
        scr["pool_wbd"][...] = jnp.zeros((D_G, D_G), F32)
        for gi in range(N_SUB):
            sl = slice(gi * HEAD_DIM, (gi + 1) * HEAD_DIM)
            scr["pool_wbd"][sl, sl] = nat["pool_w"][layer, gi]

    w = {n: _row_view(nat[n], layer) for n in ("sg_ln_g", "sg_ln_b", "pool_scale", "cc_dw_b", "cc_ln_g", "cc_ln_b")}
    w.update(conv_a=scr["conv_a"], cc_dw_w=scr["cc_dw_w"], sg_bias=scr["sg_bias"], pool_wbd=scr["pool_wbd"],
             sg_w=nat["sg_w"].at[layer], cc_pw_w=pw_ref)
    return w


def _full_spec(a):
    nd = a.ndim
    return pl.BlockSpec(a.shape, lambda *_, _nd=nd: (0,) * _nd)


def _tie_specs(ties):
    return [pl.BlockSpec((8, 128), lambda *_: (0, 0)) for _ in ties]


def _params(*sem):
    return pltpu.CompilerParams(dimension_semantics=sem or None, vmem_limit_bytes=VMEM_LIMIT)


def _proj_matmul(x, wt_in, ties=(), tm=256):
    s, k = x.shape

    def body(x_ref, w_ref, *rest):
        o_ref = rest[len(ties)]
        o_ref[...] = _mm_nt(x_ref[...], w_ref[...])

    return pl.pallas_call(
        body, grid=(s // tm,),
        in_specs=[pl.BlockSpec((tm, k), lambda i: (i, 0)), _full_spec(wt_in)] + _tie_specs(ties),
        out_specs=pl.BlockSpec((tm, D_IN), lambda i: (i, 0)),
        out_shape=jax.ShapeDtypeStruct((s, D_IN), F32), name="proj_mm", compiler_params=_params("arbitrary"),
    )(x, wt_in, *ties)


def _kv_project(mem, w_kv):
    def body(mem_ref, w_ref, km_ref, vm_ref):
        kv = _mm(mem_ref[...], w_ref[...])
        k, v = kv[:, :D_G], kv[:, D_G:]
        grp = _head_of_lane((MEM_LEN, D_G))
        for h in range(N_SUB):
            km_ref[h] = jnp.where(grp == h, k, 0.0).astype(km_ref.dtype)
            vm_ref[h] = jnp.where(grp == h, v, 0.0).astype(vm_ref.dtype)

    shp = jax.ShapeDtypeStruct((N_SUB, MEM_LEN, D_G), MM_DTYPE)
    return pl.pallas_call(body, out_shape=(shp, shp), name="kv_project", compiler_params=_params())(mem, w_kv)


def _layer_fwd_fused(x, wt_in, proj, km, vm, layer, repl, pw, ca8, dw8, w_out, ties=(), tile=TILE):
    s = x.shape[0]
    hb = tile // HALO
    nat_arrays = [repl[n] for n in _BRANCH_REPL]
    n_nat, nt = len(nat_arrays), len(ties)
    given = proj is not None

    def body(x_ref, *rest):
        if given:
            p_ref, ph_ref = rest[:2]
            rest = rest[2:]
        else:
            wt_ref = rest[0]
            rest = rest[1:]
        km_ref, vm_ref = rest[:2]
        nat = dict(zip(_BRANCH_REPL, rest[2:2 + n_nat]))
        pw_ref, ca_ref, dw_ref, wo_ref, g_ref, b_ref = rest[2 + n_nat:8 + n_nat]
        rest = rest[8 + n_nat + nt:]
        if not given:
            p_ref, rest = rest[0], rest[1:]
        h_ref, z_ref, xn_ref, cd_ref = rest[:4]
        rest = rest[4:]
        if not given:
            ph_ref, rest = rest[0], rest[1:]
        ext_a, ext_c, ext_d, res_c, res_d = rest[:5]
        scr = dict(zip([n for n, _ in _BRANCH_W_SCRATCH], rest[5:]))
        i = pl.program_id(0)
        xt = x_ref[...]
        if not given:
            @pl.when(i == 0)
            def _():
                ph_ref[...] = jnp.zeros_like(ph_ref)

            p_ref[...] = _mm_nt(xt, wt_ref[...])
        w = _branch_weights(layer, nat, pw_ref, ca_ref, dw_ref, scr, i == 0)
        r = _branch_forward(p_ref, ph_ref, i == 0, i * tile, km_ref, vm_ref, w, ext_a, ext_c, ext_d, res_c, res_d,
                            tile)
        if not given:
            ph_ref[...] = p_ref[tile - HALO:tile, :]
        h = (r["concat"] * (r["gate"] * r["sig_gate"])).astype(h_ref.dtype)
        h_ref[...] = h
        cd_ref[...] = r["conv_d"]
        z = ALPHA * xt + _mm(h, wo_ref[...])
        z_ref[...] = z
        xn_ref[...] = _ln_fwd(z, _row_view(g_ref, layer)[...], _row_view(b_ref, layer)[...])[0]

    row = lambda i: (i, 0)
    consts = [km, vm] + nat_arrays + [pw, ca8, dw8, w_out, repl["ln_g"], repl["ln_b"]]
    act = jax.ShapeDtypeStruct((s, D_MODEL), F32)
    act_spec = pl.BlockSpec((tile, D_MODEL), row)
    if given:
        lead = [proj, proj]
        lead_specs = [pl.BlockSpec((tile, D_IN), row),
                      pl.BlockSpec((HALO, D_IN), lambda i: (jnp.maximum(i * hb - 1, 0), 0))]
        out_specs, out_shape, scratch = (), (), []
    else:
        lead = [wt_in]
        lead_specs = [_full_spec(wt_in)]
        out_specs = (pl.BlockSpec((tile, D_IN), row),)
        out_shape = (jax.ShapeDtypeStruct((s, D_IN), F32),)
        scratch = [pltpu.VMEM((HALO, D_IN), F32)]
    res = pl.pallas_call(
        body, grid=(s // tile,),
        in_specs=[act_spec] + lead_specs + [_full_spec(a) for a in consts] + _tie_specs(ties),
        out_specs=out_specs + (pl.BlockSpec((tile, D_MIX), row), act_spec, act_spec, pl.BlockSpec((tile, D_G), row)),
        out_shape=out_shape + (jax.ShapeDtypeStruct((s, D_MIX), MM_DTYPE), act, act,
                               jax.ShapeDtypeStruct((s, D_G), F32)),
        scratch_shapes=scratch + [pltpu.VMEM((HALO + tile, D_G), F32)] * 3
        + [_residue_scratch(tile)] * 2 + [pltpu.VMEM(shape, F32) for _, shape in _BRANCH_W_SCRATCH],
        name="layer_fwd_given_proj" if given else "layer_fwd", compiler_params=_params("arbitrary"),
    )(x, *lead, *consts, *ties)
    return ((proj,) + tuple(res)) if given else tuple(res)


def _ln_out_bwd(up, target, z, ln_g, ln_b, layer, w_out, h, ties=(), tm=256):
    s = z.shape[0]
    from_loss = target is not None
    other = target if from_loss else up
    nt = len(ties)

    def body(o_ref, z_ref, g_ref, b_ref, w_ref, h_ref, *rest):
        dz_ref, dh_ref, gw_ref, slab_ref, gw_acc, lacc = rest[nt:]
        i = pl.program_id(0)

        @pl.when(i == 0)
        def _():
            gw_acc[...] = jnp.zeros_like(gw_acc)
            slab_ref[...] = jnp.zeros_like(slab_ref)
            lacc[...] = jnp.zeros_like(lacc)

        g = _row_view(g_ref, layer)[...]
        xn, xhat, rstd = _ln_fwd(z_ref[...], g, _row_view(b_ref, layer)[...])
        if from_loss:
            err = xn - o_ref[...]
            lacc[...] += _rowsum(err * err)
            dxn = err * (1.0 / D_MODEL)
        else:
            dxn = o_ref[...]
        slab_ref[0:1, :] += _rowsum(dxn * xhat)
        slab_ref[1:2, :] += _rowsum(dxn)
        dz = _ln_bwd(dxn, xhat, rstd, g)
        dz_ref[...] = dz
        dh_ref[...] = _mm_nt(dz, w_ref[...])
        gw_acc[...] += _mm_tn(h_ref[...], dz)

        @pl.when(i == pl.num_programs(0) - 1)
        def _():
            gw_ref[...] = gw_acc[...].astype(gw_ref.dtype)
            if from_loss:
                total = jnp.sum(lacc[...], axis=-1, keepdims=True) * (0.5 / D_MODEL)
                slab_ref[LOSS_ROW:LOSS_ROW + 1, :] = jnp.broadcast_to(total, (1, D_MODEL))

    row = lambda i: (i, 0)
    fixed = lambda i: (0, 0)
    return pl.pallas_call(
        body, grid=(s // tm,),
        in_specs=[pl.BlockSpec((tm, D_MODEL), row), pl.BlockSpec((tm, D_MODEL), row), _full_spec(ln_g),
                  _full_spec(ln_b), _full_spec(w_out), pl.BlockSpec((tm, D_MIX), row)] + _tie_specs(ties),
        out_specs=(pl.BlockSpec((tm, D_MODEL), row), pl.BlockSpec((tm, D_MIX), row),
                   pl.BlockSpec((D_MIX, D_MODEL), fixed), pl.BlockSpec((8, D_MODEL), fixed)),
        out_shape=(jax.ShapeDtypeStruct((s, D_MODEL), F32), jax.ShapeDtypeStruct((s, D_MIX), F32),
                   jax.ShapeDtypeStruct((D_MIX, D_MODEL), GRAD_DTYPE), jax.ShapeDtypeStruct((8, D_MODEL), F32)),
        scratch_shapes=[pltpu.VMEM((D_MIX, D_MODEL), F32), pltpu.VMEM((1, D_MODEL), F32)],
        name="ln_out_bwd_loss" if from_loss else "ln_out_bwd", compiler_params=_params("arbitrary"),
    )(other, z, ln_g, ln_b, w_out, h, *ties)


_BRANCH_GRADS = (("g256", (8, D_G)), ("sg_w", (N_SUB, CHUNK, CHUNK)), ("sg_b", (8, CHUNK)),
                 ("pool_w", (N_SUB, HEAD_DIM, HEAD_DIM)), ("conv_a_w", (N_DEV, CONV_A, CONV_CH)),
                 ("cc_dw_w", (N_DEV, CONV_D, CONV_CH)), ("cc_pw_w", (D_G, D_G)),
                 ("dk", (N_SUB, MEM_LEN, D_G)), ("dv", (N_SUB, MEM_LEN, D_G)))
_BRANCH_ACC = (("conv_a", (CONV_A, D_G)), ("cc_dw_w", (CONV_D, D_G)), ("pool_wbd", (D_G, D_G)),
               ("sg_bias", (CHUNK, D_G)))


def _branch_bwd(proj, conv_d, dh, km, vm, layer, repl, pw, ca8, dw8, tile=TILE):
    s = proj.shape[0]
    nt = s // tile
    hb = tile // HALO
    nat_arrays = [repl[n] for n in _BRANCH_REPL]
    n_nat, n_grads, n_acc = len(nat_arrays), len(_BRANCH_GRADS), len(_BRANCH_ACC)
    row_of = {n: k for k, n in enumerate(G256_ROWS)}

    def body(p_ref, ph_ref, cd_ref, dh_ref, km_ref, vm_ref, *rest):
        nat = dict(zip(_BRANCH_REPL, rest[:n_nat]))
        pw_ref, ca_ref, dw_ref = rest[n_nat:n_nat + 3]
        rest = rest[n_nat + 3:]
        dp_ref = rest[0]
        g = dict(zip([n for n, _ in _BRANCH_GRADS], rest[1:1 + n_grads]))
        rest = rest[1 + n_grads:]
        ext_a, ext_c, ext_d, rev_a, rev_c, rev_d, res_c, res_d, res_rc, res_rd = rest[:10]
        acc = dict(zip([n for n, _ in _BRANCH_ACC], rest[10:10 + n_acc]))
        scr = dict(zip([n for n, _ in _BRANCH_W_SCRATCH], rest[10 + n_acc:]))
        i = pl.program_id(0)
        t = nt - 1 - i

        @pl.when(i == 0)
        def _():
            for ref in list(g.values()) + list(acc.values()) + [rev_a, rev_c, rev_d]:
                ref[...] = jnp.zeros_like(ref)

        w = _branch_weights(layer, nat, pw_ref, ca_ref, dw_ref, scr, i == 0)
        r = _branch_forward(p_ref, ph_ref, t == 0, t * tile, km_ref, vm_ref, w, ext_a, ext_c, ext_d, res_c, res_d,
                            tile, cd_ref)

        def put(k, val, width=D_G):
            dp_ref[:, k:k + width] = val.astype(dp_ref.dtype)

        def add_row(name, val):
            k = row_of[name]
            g["g256"][k:k + 1, :] += val

        def push_rev(rev, val):
            head = rev[0:HALO]
            rev[tile:tile + HALO] = head
            rev[0:tile] = val

        dh_all = dh_ref[...]
        gate, sig_gate, concat = r["gate"], r["sig_gate"], r["concat"]
        put(C_GATE, dh_all * concat * (sig_gate * (1.0 + gate * (1.0 - sig_gate))), D_MIX)
        dconcat = dh_all * (gate * sig_gate)
        dya, dyb, dyc, dyd, dye = [dconcat[:, k * D_G:(k + 1) * D_G] for k in range(N_GROUPS)]

        put(C_BA * D_G, dya * r["conv_a"])
        dconv_a = dya * r["ba"]
        for k in range(CONV_A):
            acc["conv_a"][k:k + 1, :] += _rowsum(dconv_a * ext_a[pl.ds(HALO - 2 + k, tile), :])
        push_rev(rev_a, dconv_a)
        dga = w["conv_a"][0:1, :] * rev_a[pl.ds(2, tile), :]
        for k in range(1, CONV_A):
            dga = dga + w["conv_a"][k:k + 1, :] * rev_a[pl.ds(2 - k, tile), :]
        put(C_CA * D_G, dga * r["xa"])
        put(C_XA * D_G, dga * r["ca"])

        add_row("pool_scale", _rowsum(dyc * r["pool_mm"]))
        dmm = dyc * w["pool_scale"][...]
        acc["pool_wbd"][...] += _mm_tn(r["ypre"], dmm)
        dypre = _mm_nt(dmm, w["pool_wbd"][...])
        dws = dypre * r["inv_cnt"]
        push_rev(rev_c, dws)
        _make_residues(rev_c, res_rc)
        run = dws
        sums = {}
        for k in range(1, POOL_WINDOWS[-1]):
            run = run + _rows_at(rev_c, res_rc, k, tile)
            if k + 1 in POOL_WINDOWS:
                sums[k + 1] = run
        put(C_XC * D_G, _pool_select(r["lane_grp"], sums[2], sums[4], sums[8], sums[16]) - dypre)

        g["cc_pw_w"][...] += _mm_tn(r["act_d"], dyd)
        dact = _mm_nt(dyd, w["cc_pw_w"][...])
        sig_ln, ln_d = r["sig_ln"], r["ln_d"]
        dln = dact * (sig_ln * (1.0 + ln_d * (1.0 - sig_ln)))
        add_row("cc_ln_g", _rowsum(dln * r["xhat_d"]))
        add_row("cc_ln_b", _rowsum(dln))
        dconv_d = _ln_bwd(dln, r["xhat_d"], r["rstd_d"], w["cc_ln_g"][...])
        add_row("cc_dw_b", _rowsum(dconv_d))
        for j in range(CONV_D):
            acc["cc_dw_w"][j:j + 1, :] += _rowsum(dconv_d * _rows_at(ext_d, res_d, HALO - (CONV_D - 1) + j, tile))
        push_rev(rev_d, dconv_d)
        _make_residues(rev_d, res_rd)
        dhd = w["cc_dw_w"][0:1, :] * _rows_at(rev_d, res_rd, CONV_D - 1, tile)
        for j in range(1, CONV_D):
            dhd = dhd + w["cc_dw_w"][j:j + 1, :] * _rows_at(rev_d, res_rd, CONV_D - 1 - j, tile)
        sig_dg = r["sig_dg"]
        put(C_DA * D_G, dhd * sig_dg)
        put(C_DG * D_G, dhd * r["da"] * sig_dg * (1.0 - sig_dg))

        dug = dyb * r["mixed"]
        dmixed = dyb * r["ug"]
        wm, lo, vn = r["wm"], r["lo"], r["vn"]
        dvn_chunks = []
        for c in range(tile // CHUNK):
            rows = slice(c * CHUNK, (c + 1) * CHUNK)
            acc["sg_bias"][...] += dmixed[rows, :]
            halves = []
            for hf in range(2):
                cols = slice(hf * 128, (hf + 1) * 128)
                dm = dmixed[rows, cols]
                dm_a, dm_b = jnp.where(lo, dm, 0.0), jnp.where(lo, 0.0, dm)
                vh = vn[rows, cols]
                g["sg_w"][2 * hf] += _mm_nt(dm_a, vh)
                g["sg_w"][2 * hf + 1] += _mm_nt(dm_b, vh)
                halves.append(_mm_tn(wm[2 * hf], dm_a) + _mm_tn(wm[2 * hf + 1], dm_b))
            dvn_chunks.append(jnp.concatenate(halves, axis=1))
        dvn = jnp.concatenate(dvn_chunks, axis=0)
        add_row("sg_ln_g", _rowsum(dvn * r["xhat_v"]))
        add_row("sg_ln_b", _rowsum(dvn))
        dvg = _ln_bwd(dvn, r["xhat_v"], r["rstd_v"], w["sg_ln_g"][...])
        put(C_V * D_G, dvg * _dgelu(r["v"], r["th_v"]))
        put(C_U * D_G, dug * _dgelu(r["u"], r["th_u"]))

        q = r["q"]
        dq = jnp.zeros((tile, D_G), F32)
        for h in range(N_SUB):
            p = r["probs"][h]
            dp = _mm_nt(dye, vm_ref[h])
            g["dv"][h] += _mm_tn(p, dye)
            ds = p * (dp - jnp.sum(dp * p, axis=-1, keepdims=True)) * ATT_SCALE
            dq = dq + _mm(ds, km_ref[h])
            g["dk"][h] += _mm_tn(ds, q)
        put(C_Q * D_G, dq)

        @pl.when(i == nt - 1)
        def _():
            for h in range(N_SUB):
                g["sg_w"][h] = jnp.where(r["tri"], g["sg_w"][h], 0.0)
            lane_head = _head_of_lane((CHUNK, D_G))
            col_of = lax.broadcasted_iota(jnp.int32, (CHUNK, 8), 1)
            ba = acc["sg_bias"][...]
            sgb_t = jnp.zeros((CHUNK, 8), F32)
            for h in range(N_SUB):
                col = jnp.sum(jnp.where(lane_head == h, ba, 0.0), axis=-1, keepdims=True)
                sgb_t = jnp.where(col_of == h, col, sgb_t)
            g["sg_b"][...] = sgb_t.T
            wbd = acc["pool_wbd"][...]
            for gi in range(N_SUB):
                sl = slice(gi * HEAD_DIM, (gi + 1) * HEAD_DIM)
                g["pool_w"][gi] = wbd[sl, sl]
            ca, dw = acc["conv_a"][...], acc["cc_dw_w"][...]
            for p in range(N_DEV):
                g["conv_a_w"][p] = ca[:, p * CONV_CH:(p + 1) * CONV_CH]
                g["cc_dw_w"][p] = dw[:, p * CONV_CH:(p + 1) * CONV_CH]

    rev = lambda i: (nt - 1 - i, 0)
    grad_specs = tuple(pl.BlockSpec(shape, lambda i, _nd=len(shape): (0,) * _nd) for _, shape in _BRANCH_GRADS)
    grad_shapes = tuple(jax.ShapeDtypeStruct(shape, F32) for _, shape in _BRANCH_GRADS)
    outs = pl.pallas_call(
        body, grid=(nt,),
        in_specs=[pl.BlockSpec((tile, D_IN), rev),
                  pl.BlockSpec((HALO, D_IN), lambda i: (jnp.maximum((nt - 1 - i) * hb - 1, 0), 0)),
                  pl.BlockSpec((tile, D_G), rev), pl.BlockSpec((tile, D_MIX), rev), _full_spec(km), _full_spec(vm)]
        + [_full_spec(a) for a in nat_arrays + [pw, ca8, dw8]],
        out_specs=(pl.BlockSpec((tile, D_IN), rev),) + grad_specs,
        out_shape=(jax.ShapeDtypeStruct((s, D_IN), MM_DTYPE),) + grad_shapes,
        scratch_shapes=[pltpu.VMEM((HALO + tile, D_G), F32)] * 6 + [_residue_scratch(tile)] * 4
        + [pltpu.VMEM(shape, F32) for _, shape in _BRANCH_ACC + _BRANCH_W_SCRATCH],
        name="branch_bwd", compiler_params=_params("arbitrary"),
    )(proj, proj, conv_d, dh, km, vm, *nat_arrays, pw, ca8, dw8)
    return outs[0], dict(zip([n for n, _ in _BRANCH_GRADS], outs[1:]))


def _dx_matmul(dproj, wt_in, dz, ties=(), tm=256):
    s = dproj.shape[0]

    def body(dp_ref, w_ref, dz_ref, *rest):
        o_ref = rest[len(ties)]
        o_ref[...] = _mm(dp_ref[...], w_ref[...]) + ALPHA * dz_ref[...]

    row = lambda i: (i, 0)
    return pl.pallas_call(
        body, grid=(s // tm,),
        in_specs=[pl.BlockSpec((tm, D_IN), row), _full_spec(wt_in), pl.BlockSpec((tm, D_MODEL), row)]
        + _tie_specs(ties),
        out_specs=pl.BlockSpec((tm, D_MODEL), row),
        out_shape=jax.ShapeDtypeStruct((s, D_MODEL), F32), name="dx_mm", compiler_params=_params("arbitrary"),
    )(dproj, wt_in, dz, *ties)


def _dw_in_matmul(x, dproj, ties=(), tk=512):
    s = x.shape[0]
    tn = 2 * W_IN_COLS
    nk = s // tk

    def body(x_ref, dp_ref, *rest):
        o_ref, acc = rest[len(ties):]
        k = pl.program_id(1)

        @pl.when(k == 0)
        def _():
            acc[...] = jnp.zeros_like(acc)

        acc[...] += _mm_tn(dp_ref[...], x_ref[...])

        @pl.when(k == nk - 1)
        def _():
            o_ref[...] = acc[...].astype(o_ref.dtype)

    return pl.pallas_call(
        body, grid=(D_IN // tn, nk),
        in_specs=[pl.BlockSpec((tk, D_MODEL), lambda j, k: (k, 0)), pl.BlockSpec((tk, tn), lambda j, k: (k, j))]
        + _tie_specs(ties),
        out_specs=pl.BlockSpec((tn, D_MODEL), lambda j, k: (j, 0)),
        out_shape=jax.ShapeDtypeStruct((D_IN, D_MODEL), GRAD_DTYPE),
        scratch_shapes=[pltpu.VMEM((tn, D_MODEL), F32)], name="dw_in_mm",
        compiler_params=_params("arbitrary", "arbitrary"),
    )(x, dproj, *ties)


def _in_bwd(x, dproj, wt_in, dz, tm=256):
    s = x.shape[0]
    n_steps = s // tm

    assert wt_in.dtype == GRAD_DTYPE
    blk = 2 * W_IN_COLS

    def body(x_ref, dp_ref, w_hbm, dz_ref, o_ref, gw_hbm, w_vmem, acc, sem):
        i = pl.program_id(0)

        @pl.when(i == 0)
        def _():
            fetch = pltpu.make_async_copy(w_hbm, w_vmem, sem)
            fetch.start()
            acc[...] = jnp.zeros_like(acc)
            fetch.wait()

        o_ref[...] = _mm(dp_ref[...], w_vmem[...]) + ALPHA * dz_ref[...]
        xb = x_ref[...].astype(MM_DTYPE)
        for j in range(D_IN // blk):
            acc[j * blk:(j + 1) * blk, :] += _mm_tn(dp_ref[:, j * blk:(j + 1) * blk], xb)

        @pl.when(i == n_steps - 1)
        def _():
            w_vmem[...] = acc[...].astype(w_vmem.dtype)
            emit = pltpu.make_async_copy(w_vmem, gw_hbm, sem)
            emit.start()
            emit.wait()

    row = lambda i: (i, 0)
    any_spec = pl.BlockSpec(memory_space=pl.ANY)
    return pl.pallas_call(
        body, grid=(n_steps,),
        in_specs=[pl.BlockSpec((tm, D_MODEL), row), pl.BlockSpec((tm, D_IN), row), any_spec,
                  pl.BlockSpec((tm, D_MODEL), row)],
        out_specs=(pl.BlockSpec((tm, D_MODEL), row), any_spec),
        out_shape=(jax.ShapeDtypeStruct((s, D_MODEL), F32), jax.ShapeDtypeStruct((D_IN, D_MODEL), GRAD_DTYPE)),
        scratch_shapes=[pltpu.VMEM((D_IN, D_MODEL), wt_in.dtype), pltpu.VMEM((D_IN, D_MODEL), F32),
                        pltpu.SemaphoreType.DMA],
        name="in_bwd", compiler_params=_params("arbitrary"),
    )(x, dproj, wt_in, dz)


def _kv_bwd(mem, dk, dv):
    def body(mem_ref, dk_ref, dv_ref, o_ref):
        grp = _head_of_lane((MEM_LEN, D_G))
        dk_sum = jnp.zeros((MEM_LEN, D_G), F32)
        dv_sum = jnp.zeros((MEM_LEN, D_G), F32)
        for h in range(N_SUB):
            dk_sum = dk_sum + jnp.where(grp == h, dk_ref[h], 0.0)
            dv_sum = dv_sum + jnp.where(grp == h, dv_ref[h], 0.0)
        o_ref[...] = _mm_tn(mem_ref[...], jnp.concatenate([dk_sum, dv_sum], axis=1)).astype(o_ref.dtype)

    return pl.pallas_call(body, out_shape=jax.ShapeDtypeStruct((D_MODEL, 2 * D_G), GRAD_DTYPE), name="kv_bwd",
                          compiler_params=_params())(mem, dk, dv)


def _layer_fwd(x, mem, layer, repl, gw, proj=None, km_vm=None, ties=()):
    km, vm = _kv_project(mem, gw["w_kv"]) if km_vm is None else km_vm
    proj, h, z, xn, conv_d = _layer_fwd_fused(x, gw["wt_in"] if proj is None else None, proj, km, vm, layer, repl,
                                              gw["pw"], gw["ca8"], gw["dw8"], gw["w_out"], ties)
    return xn, (x, proj, h, z, km, vm, conv_d)


def _layer_bwd_a(up, target, mem, layer, repl, gw, saved, ties=()):
    x_in, proj, h, z, km, vm, conv_d = saved
    dz, dh, g_w_out, g1024 = _ln_out_bwd(up, target, z, repl["ln_g"], repl["ln_b"], layer, gw["w_out"], h, ties)
    dproj, bg = _branch_bwd(proj, conv_d, dh, km, vm, layer, repl, gw["pw"], gw["ca8"], gw["dw8"])
    grads = {n: bg[n] for n in ("g256", "sg_w", "sg_b", "pool_w", "conv_a_w", "cc_dw_w")}
    grads.update(w_out=g_w_out.reshape(N_DEV, D_MIX // N_DEV, D_MODEL), g1024=g1024,
                 w_kv=_kv_bwd(mem, bg["dk"], bg["dv"]).reshape(N_DEV, D_MODEL // N_DEV, 2 * D_G),
                 cc_pw_w=bg["cc_pw_w"].reshape(N_DEV, CONV_CH, D_G))
    return dz, dproj, grads


def _landing_shapes(items):
    out = []
    for a, scatter, pick in items:
        shape = a.shape if scatter else (N_DEV,) + (a.shape if pick is None else a.shape[1:])
        out.append(jax.ShapeDtypeStruct(shape, a.dtype))
    return tuple(out)


def _exchange_sems(n):
    return [pltpu.SemaphoreType.DMA(((N_DEV - 1) * n,)), pltpu.SemaphoreType.DMA(((N_DEV - 1) * n,)),
            pltpu.SemaphoreType.DMA((n,))]


def _exchange_copies(modes, ins, outs, send_sems, recv_sems, local_sems):
    n = len(ins)
    x, y, c = lax.axis_index("x"), lax.axis_index("y"), lax.axis_index("c")
    me = 4 * x + 2 * y + c

    def src_of(a, dest):
        scatter, pick = modes[a]
        if scatter:
            return ins[a].at[dest]
        return ins[a] if pick is None else ins[a].at[pick]

    local = [pltpu.make_async_copy(src_of(a, me), outs[a].at[me], local_sems.at[a]) for a in range(n)]
    sends, recvs = [], []
    for k in range(1, N_DEV):
        px = 1 - x if k & 4 else x
        py = 1 - y if k & 2 else y
        pc = 1 - c if k & 1 else c
        peer = 4 * px + 2 * py + pc
        for a in range(n):
            sems = dict(send_sem=send_sems.at[(k - 1) * n + a], recv_sem=recv_sems.at[(k - 1) * n + a],
                        device_id=(px, py, pc), device_id_type=pl.DeviceIdType.MESH)
            sends.append(pltpu.make_async_remote_copy(src_ref=src_of(a, peer), dst_ref=outs[a].at[me], **sems))
            recvs.append(pltpu.make_async_remote_copy(src_ref=src_of(a, peer), dst_ref=outs[a].at[peer], **sems))
    return local, sends, recvs


def _exchange(items, name):
    n = len(items)
    modes = [(scatter, pick) for _, scatter, pick in items]

    def body(*refs):
        local, sends, recvs = _exchange_copies(modes, refs[:n], refs[n:2 * n], *refs[2 * n:])
        for cp in local + sends:
            cp.start()
        for cp in recvs:
            cp.wait_recv()
        for cp in sends:
            cp.wait_send()
        for cp in local:
            cp.wait()

    any_spec = pl.BlockSpec(memory_space=pl.ANY)
    return pl.pallas_call(
        body, in_specs=[any_spec] * n, out_specs=(any_spec,) * n, out_shape=_landing_shapes(items),
        scratch_shapes=_exchange_sems(n), name=name,
    )(*[a for a, _, _ in items])


def _gather_two_level(items, name):
    n = len(items)
    assert not any(scatter for _, scatter, _ in items)
    picks = [pick for _, _, pick in items]

    def body(*refs):
        ins, outs = refs[:n], refs[n:2 * n]
        send_sems, recv_sems, local_sems = refs[2 * n:]
        x, y, c = lax.axis_index("x"), lax.axis_index("y"), lax.axis_index("c")
        sib = 1 - c
        chips = [(1 - x, y), (x, 1 - y), (1 - x, 1 - y)]

        def slot(a, px, py, pc):
            return outs[a].at[4 * px + 2 * py + pc]

        def copy(k, a, src, block, to):
            return pltpu.make_async_remote_copy(
                src_ref=src, dst_ref=slot(a, *block), send_sem=send_sems.at[k * n + a],
                recv_sem=recv_sems.at[k * n + a], device_id=to, device_id_type=pl.DeviceIdType.MESH)

        own = [ins[a] if picks[a] is None else ins[a].at[picks[a]] for a in range(n)]
        local = [pltpu.make_async_copy(own[a], slot(a, x, y, c), local_sems.at[a]) for a in range(n)]
        first = [copy(0, a, own[a], (x, y, c), (x, y, sib)) for a in range(n)]
        first += [copy(1 + j, a, own[a], (x, y, c), (*chip, c)) for j, chip in enumerate(chips) for a in range(n)]
        for cp in local + first:
            cp.start()
        passed = []
        for j, chip in enumerate(chips):
            for a in range(n):
                copy(1 + j, a, own[a], (*chip, c), (x, y, c)).wait_recv()
                fwd = copy(4 + j, a, slot(a, *chip, c), (*chip, c), (x, y, sib))
                fwd.start()
                passed.append(fwd)
        for a in range(n):
            copy(0, a, own[a], (x, y, sib), (x, y, c)).wait_recv()
        for j, chip in enumerate(chips):
            for a in range(n):
                copy(4 + j, a, own[a], (*chip, sib), (x, y, c)).wait_recv()
        for cp in first + passed:
            cp.wait_send()
        for cp in local:
            cp.wait()

    any_spec = pl.BlockSpec(memory_space=pl.ANY)
    return pl.pallas_call(
        body, in_specs=[any_spec] * n, out_specs=(any_spec,) * n, out_shape=_landing_shapes(items),
        scratch_shapes=[pltpu.SemaphoreType.DMA((7 * n,)), pltpu.SemaphoreType.DMA((7 * n,)),
                        pltpu.SemaphoreType.DMA((n,))],
        name=name,
    )(*[a for a, _, _ in items])


_HBM_SPEC = pl.BlockSpec(memory_space=pltpu.HBM)
_SEM_SPEC = pl.BlockSpec(memory_space=pltpu.SEMAPHORE)
_SPLIT_PARAMS = pltpu.CompilerParams(has_side_effects=pltpu.SideEffectType.DATAFLOW_SIDE_EFFECTING)


def _exchange_start(items, name):
    n = len(items)
    modes = [(scatter, pick) for _, scatter, pick in items]
    shapes = _landing_shapes(items)
    lands = [pltpu.with_memory_space_constraint(lax.empty(s.shape, s.dtype), pltpu.HBM) for s in shapes]
    srcs = [pltpu.with_memory_space_constraint(a, pltpu.HBM) for a, _, _ in items]

    def body(*refs):
        local, sends, _ = _exchange_copies(modes, refs[:n], refs[n:2 * n], *refs[2 * n:2 * n + 3])
        for cp in local + sends:
            cp.start()
        token = refs[-1]
        token[...] = jnp.zeros_like(token)

    res = pl.pallas_call(
        body, name=name, in_specs=[_HBM_SPEC] * (2 * n),
        out_shape=tuple(_exchange_sems(n)) + tuple(pltpu.HBM(a.shape, a.dtype) for a in srcs)
        + tuple(pltpu.HBM(s.shape, s.dtype) for s in shapes) + (jax.ShapeDtypeStruct((8, 128), F32),),
        out_specs=(_SEM_SPEC,) * 3 + (_HBM_SPEC,) * (2 * n) + (pl.BlockSpec(memory_space=pltpu.VMEM),),
        input_output_aliases={i: 3 + i for i in range(2 * n)}, compiler_params=_SPLIT_PARAMS,
    )(*srcs, *lands)
    return dict(sems=res[:3], srcs=res[3:3 + n], lands=res[3 + n:3 + 2 * n], token=res[-1], modes=modes)


def _exchange_wait(ticket, after, name):
    n = len(ticket["srcs"])
    modes = ticket["modes"]

    def body(*refs):
        local, sends, recvs = _exchange_copies(modes, refs[:n], refs[n:2 * n], *refs[2 * n:2 * n + 3])
        for cp in recvs:
            cp.wait_recv()
        for cp in sends:
            cp.wait_send()
        for cp in local:
            cp.wait()

    both = list(ticket["srcs"]) + list(ticket["lands"])
    res = pl.pallas_call(
        body, name=name, in_specs=[_HBM_SPEC] * (2 * n) + [_SEM_SPEC] * 3 + [pl.BlockSpec(memory_space=pl.ANY)],
        out_shape=tuple(pltpu.HBM(a.shape, a.dtype) for a in both), out_specs=(_HBM_SPEC,) * (2 * n),
        input_output_aliases={i: i for i in range(2 * n)}, compiler_params=_SPLIT_PARAMS,
    )(*both, *ticket["sems"], after)
    return res[n:]


def _adam_math(g, w, m, v):
    m_new = ADAM_B1 * m + (1.0 - ADAM_B1) * g
    v_new = ADAM_B2 * v + (1.0 - ADAM_B2) * (g * g)
    m_hat = m_new / (1.0 - ADAM_B1 ** ADAM_STEP)
    v_hat = v_new / (1.0 - ADAM_B2 ** ADAM_STEP)
    return -ADAM_LR * (m_hat / (jnp.sqrt(v_hat) + ADAM_EPS) + ADAM_WD * w), m_new, v_new


def _adamw_big(parts, w, m, v, layer, prev, name, tr):
    depth, rows, cols = w.shape

    def body(p_ref, w_ref, m_ref, v_ref, *rest):
        g_out, d_out, m_out, v_out = rest[len(prev):]
        g = p_ref[0].astype(F32)
        for q in range(1, N_DEV):
            g = g + p_ref[q].astype(F32)
        d, m_new, v_new = _adam_math(g, w_ref[...], m_ref[...], v_ref[...])
        g_out[...] = g
        d_out[...] = d
        m_out[...] = m_new
        v_out[...] = v_new

    blk = pl.BlockSpec((None, tr, cols), lambda i: (layer, i, 0))
    shp = jax.ShapeDtypeStruct((depth, rows, cols), F32)
    return pl.pallas_call(
        body, grid=(rows // tr,),
        in_specs=[pl.BlockSpec((N_DEV, tr, cols), lambda i: (0, i, 0)), blk, blk, blk]
        + [pl.BlockSpec(memory_space=pl.ANY)] * len(prev),
        out_specs=(blk,) * 4, out_shape=(shp,) * 4,
        input_output_aliases={4 + j: j for j in range(len(prev))},
        name=name, compiler_params=_params("arbitrary"),
    )(parts, w, m, v, *prev)


_SMALL_TENSORS = (("conv_a_w", "conv_a_w", None), ("cc_dw_w", "cc_dw_w", None), ("cc_pw_w", "cc_pw_w", None),
                  ("sg_w", "sg_w", None), ("pool_w", "pool_w", None), ("sg_b", "sg_b", None)) \
    + tuple((n, "g256", k) for k, n in enumerate(G256_ROWS)) + tuple((n, "g1024", k) for k, n in enumerate(G1024_ROWS))
_SMALL_LANDINGS = ("conv_a_w", "cc_dw_w", "cc_pw_w", "sg_w", "pool_w", "sg_b", "g256", "g1024")
_TAPS_FIRST = ("conv_a_w", "cc_dw_w")


def _adamw_small(landings, wts, mom, var):
    names = [n for n, _, _ in _SMALL_TENSORS]
    n_land = DEPTH * len(_SMALL_LANDINGS)
    n_t = len(names)

    def body(*refs):
        land = [dict(zip(_SMALL_LANDINGS, refs[l * len(_SMALL_LANDINGS):(l + 1) * len(_SMALL_LANDINGS)]))
                for l in range(DEPTH)]
        w_refs = dict(zip(names, refs[n_land:n_land + n_t]))
        m_refs = dict(zip(names, refs[n_land + n_t:n_land + 2 * n_t]))
        v_refs = dict(zip(names, refs[n_land + 2 * n_t:n_land + 3 * n_t]))
        outs = refs[n_land + 3 * n_t:]
        out_refs = {n: outs[4 * k:4 * k + 4] for k, n in enumerate(names)}
        loss_ref = outs[4 * n_t]
        for name, key, row in _SMALL_TENSORS:
            for l in range(DEPTH):
                src = land[l][key]
                if row is not None:
                    part = lambda q: src[q, row:row + 1, :]
                    at = (slice(l, l + 1),)
                elif name == "sg_b":
                    part = lambda q: src[q, 0:N_SUB, :]
                    at = (l,)
                elif name in _TAPS_FIRST:
                    part = lambda q: src[q]
                    at = (slice(None), l)
                else:
                    part = lambda q: src[q]
                    at = (l,)
                g = part(0)
                for q in range(1, N_DEV):
                    g = g + part(q)
                d, m_new, v_new = _adam_math(g, w_refs[name][at], m_refs[name][at], v_refs[name][at])
                for ref, val in zip(out_refs[name], (g, d, m_new, v_new)):
                    ref[at] = val
        src = land[DEPTH - 1]["g1024"]
        loss = src[0, LOSS_ROW:LOSS_ROW + 1, 0:128]
        for q in range(1, N_DEV):
            loss = loss + src[q, LOSS_ROW:LOSS_ROW + 1, 0:128]
        loss_ref[...] = loss

    ins = [landings[l][k] for l in range(DEPTH) for k in _SMALL_LANDINGS] \
        + [src[n] for src in (wts, mom, var) for n in names]
    out_shape = tuple(jax.ShapeDtypeStruct(wts[n].shape, F32) for n in names for _ in range(4)) \
        + (jax.ShapeDtypeStruct((1, 128), F32),)
    res = pl.pallas_call(body, out_shape=out_shape, name="adamw_small", compiler_params=_params())(*ins)
    return {n: res[4 * k:4 * k + 4] for k, n in enumerate(names)}, res[4 * n_t]


_BIG = (("w_in", 64), ("w_out", 32), ("w_kv", 32))
_GRAD_ITEMS_EARLY = ("w_out", "w_kv", "cc_pw_w", "conv_a_w", "cc_dw_w")
_GRAD_ITEMS_REPL = ("g256", "sg_w", "sg_b", "pool_w", "g1024")


def _grad_items(grads, with_w_in):
    items = [(grads[n], True, None) for n in (("w_in",) if with_w_in else ()) + _GRAD_ITEMS_EARLY]
    return items + [(grads[n], False, None) for n in _GRAD_ITEMS_REPL]


def _landed(parts, with_w_in):
    names = (("w_in",) if with_w_in else ()) + _GRAD_ITEMS_EARLY + _GRAD_ITEMS_REPL
    return dict(zip(names, parts))


def _gathered_weights(wt_in8, w_kv8, w_out8, pw8, ca8, dw8):
    return dict(wt_in=wt_in8.reshape(D_IN, D_MODEL), w_kv=w_kv8.reshape(D_MODEL, 2 * D_G),
                w_out=w_out8.reshape(D_MIX, D_MODEL), pw=pw8.reshape(D_G, D_G), ca8=ca8, dw8=dw8)


def kernel(x, mem, w_in, conv_a_w, sg_ln_g, sg_ln_b, sg_w, sg_b, pool_w, pool_scale, cc_dw_w, cc_dw_b, cc_ln_g, cc_ln_b, cc_pw_w, w_kv, w_out, ln_g, ln_b, loss_target, m_w_in, m_conv_a_w, m_sg_ln_g, m_sg_ln_b, m_sg_w, m_sg_b, m_pool_w, m_pool_scale, m_cc_dw_w, m_cc_dw_b, m_cc_ln_g, m_cc_ln_b, m_cc_pw_w, m_w_kv, m_w_out, m_ln_g, m_ln_b, v_w_in, v_conv_a_w, v_sg_ln_g, v_sg_ln_b, v_sg_w, v_sg_b, v_pool_w, v_pool_scale, v_cc_dw_w, v_cc_dw_b, v_cc_ln_g, v_cc_ln_b, v_cc_pw_w, v_w_kv, v_w_out, v_ln_g, v_ln_b):
    names = ("w_in", "conv_a_w", "sg_ln_g", "sg_ln_b", "sg_w", "sg_b", "pool_w", "pool_scale", "cc_dw_w", "cc_dw_b",
             "cc_ln_g", "cc_ln_b", "cc_pw_w", "w_kv", "w_out", "ln_g", "ln_b")
    wts = dict(zip(names, (w_in, conv_a_w, sg_ln_g, sg_ln_b, sg_w, sg_b, pool_w, pool_scale, cc_dw_w, cc_dw_b,
                           cc_ln_g, cc_ln_b, cc_pw_w, w_kv, w_out, ln_g, ln_b)))
    mom = dict(zip(names, (m_w_in, m_conv_a_w, m_sg_ln_g, m_sg_ln_b, m_sg_w, m_sg_b, m_pool_w, m_pool_scale,
                           m_cc_dw_w, m_cc_dw_b, m_cc_ln_g, m_cc_ln_b, m_cc_pw_w, m_w_kv, m_w_out, m_ln_g, m_ln_b)))
    var = dict(zip(names, (v_w_in, v_conv_a_w, v_sg_ln_g, v_sg_ln_b, v_sg_w, v_sg_b, v_pool_w, v_pool_scale,
                           v_cc_dw_w, v_cc_dw_b, v_cc_ln_g, v_cc_ln_b, v_cc_pw_w, v_w_kv, v_w_out, v_ln_g, v_ln_b)))
    repl = wts
    xs, mems, tgt = x[0], mem[0], loss_target[0]
    turned = {"w_in": (0, 2, 1), "conv_a_w": (1, 0, 2), "cc_dw_w": (1, 0, 2)}
    wts, mom, var = [{n: (jnp.transpose(a, turned[n]) if n in turned else a) for n, a in src.items()}
                     for src in (wts, mom, var)]
    wb = {n: wts[n].astype(MM_DTYPE) for n in ("w_in", "w_kv", "w_out", "cc_pw_w")}

    wt8_0, wkv8_0 = _gather_two_level([(wb["w_in"], False, 0), (wb["w_kv"], False, 0)], "gather_weights_0a")
    rest_0 = _exchange_start([(wb["w_out"], False, 0), (wb["cc_pw_w"], False, 0), (wts["conv_a_w"], False, None),
                              (wts["cc_dw_w"], False, None)], "gather_weights_0b_start")
    all_1 = _exchange_start([(wb["w_in"], False, 1), (wb["w_kv"], False, 1), (wb["w_out"], False, 1),
                             (wb["cc_pw_w"], False, 1)], "gather_weights_1_start")
    km_vm0 = _kv_project(mems, wkv8_0.reshape(D_MODEL, 2 * D_G))
    proj0 = _proj_matmul(xs, wt8_0.reshape(D_IN, D_MODEL), (rest_0["token"], all_1["token"]))
    wo8_0, pw8_0, ca8, dw8 = _exchange_wait(rest_0, proj0, "gather_weights_0b_wait")
    gw0 = _gathered_weights(wt8_0, wkv8_0, wo8_0, pw8_0, ca8, dw8)
    x1, saved0 = _layer_fwd(xs, mems, 0, repl, gw0, proj0, km_vm0)
    gw1 = _gathered_weights(*_exchange_wait(all_1, x1, "gather_weights_1_wait"), ca8, dw8)
    _, saved1 = _layer_fwd(x1, mems, 1, repl, gw1)

    dz1, dproj1, g1 = _layer_bwd_a(None, tgt, mems, 1, repl, gw1, saved1)
    shards = lambda g: g.reshape(N_DEV, W_IN_COLS, D_MODEL)
    up, g_wt_in_1 = _in_bwd(saved1[0], dproj1, gw1["wt_in"], dz1)
    g1["w_in"] = shards(g_wt_in_1)
    grads_1 = _exchange_start(_grad_items(g1, True), "exchange_grads_1_start")
    dz0, dproj0, g0 = _layer_bwd_a(up, None, mems, 0, repl, gw0, saved0, (grads_1["token"],))
    early_0 = _exchange_start(_grad_items(g0, False), "exchange_grads_0a_start")
    g_wt_in_0 = _dw_in_matmul(saved0[0], dproj0, (early_0["token"],))
    late_0 = _exchange_start([(shards(g_wt_in_0), True, None)], "exchange_grads_0b_start")
    grad_x = _dx_matmul(dproj0, gw0["wt_in"], dz0, (late_0["token"],))

    landed = [None, _landed(_exchange_wait(grads_1, grad_x, "exchange_grads_1_wait"), True)]
    big = {}
    for n, tr in _BIG:
        big[n] = _adamw_big(landed[1][n], wts[n], mom[n], var[n], 1, (), "adamw_" + n + "_1", tr)
    landed[0] = _landed(_exchange_wait(early_0, big["w_kv"][0], "exchange_grads_0a_wait"), False)
    for n, tr in _BIG[1:]:
        big[n] = _adamw_big(landed[0][n], wts[n], mom[n], var[n], 0, big[n], "adamw_" + n + "_0", tr)
    (landed[0]["w_in"],) = _exchange_wait(late_0, big["w_kv"][0], "exchange_grads_0b_wait")
    big["w_in"] = _adamw_big(landed[0]["w_in"], wts["w_in"], mom["w_in"], var["w_in"], 0, big["w_in"],
                             "adamw_w_in_0", _BIG[0][1])
    small, loss = _adamw_small(landed, wts, mom, var)

    res = {**small, **big}
    res = {n: ([jnp.transpose(a, turned[n]) for a in r] if n in turned else r) for n, r in res.items()}
    return (loss[0, 0], grad_x[None], *[res[n][0] for n in names], *[res[n][1] for n in names],
            *[res[n][2] for n in names], *[res[n][3] for n in names])
```

```python
import math

import jax
import jax.numpy as jnp
from jax import lax
from jax.experimental import pallas as pl
from jax.experimental.pallas import tpu as pltpu

F32 = jnp.float32
MM_DTYPE = jnp.bfloat16
GRAD_DTYPE = jnp.bfloat16

D_MODEL = 1024
DEPTH = 2
D_G = 256
N_GROUPS = 5
D_MIX = N_GROUPS * D_G
N_SUB = 4
HEAD_DIM = D_G // N_SUB
CONV_A = 3
CONV_D = 31
CHUNK = 128
POOL_WINDOWS = (2, 4, 8, 16)
MEM_LEN = 256
LN_EPS = 1e-5
ALPHA = (2.0 * DEPTH) ** 0.25
D_IN = 9 * D_G + D_MIX
ATT_SCALE = 1.0 / math.sqrt(HEAD_DIM)

ADAM_LR = 0.001
ADAM_B1 = 0.9
ADAM_B2 = 0.999
ADAM_EPS = 1e-08
ADAM_WD = 0.01
ADAM_STEP = 10

N_DEV = 8
W_IN_COLS = D_IN // N_DEV
CONV_CH = D_G // N_DEV
HALO = 32
TILE = 256
VMEM_LIMIT = 56 * 1024 * 1024

C_XA, C_BA, C_CA, C_U, C_V, C_XC, C_DA, C_DG, C_Q = range(9)
C_GATE = 9 * D_G

G256_ROWS = ("sg_ln_g", "sg_ln_b", "pool_scale", "cc_dw_b", "cc_ln_g", "cc_ln_b")
G1024_ROWS = ("ln_g", "ln_b")
LOSS_ROW = 2


def _mm(a, b):
    return jnp.dot(a.astype(MM_DTYPE), b.astype(MM_DTYPE), preferred_element_type=F32)


def _mm_nt(a, b):
    return lax.dot_general(a.astype(MM_DTYPE), b.astype(MM_DTYPE), (((1,), (1,)), ((), ())),
                           preferred_element_type=F32)


def _mm_tn(a, b):
    return lax.dot_general(a.astype(MM_DTYPE), b.astype(MM_DTYPE), (((0,), (0,)), ((), ())),
                           preferred_element_type=F32)


def _sigmoid(x):
    return 0.5 * jnp.tanh(0.5 * x) + 0.5


_GELU_C = math.sqrt(2.0 / math.pi)
_GELU_A = 0.044715


def _gelu(x):
    th = jnp.tanh(_GELU_C * (x + _GELU_A * (x * x * x)))
    return 0.5 * x * (1.0 + th), th


def _dgelu(x, th):
    return 0.5 * (1.0 + th) + 0.5 * x * (1.0 - th * th) * (_GELU_C * (1.0 + 3.0 * _GELU_A * (x * x)))


def _ln_fwd(x, g, b):
    mu = jnp.mean(x, axis=-1, keepdims=True)
    xc = x - mu
    var = jnp.mean(xc * xc, axis=-1, keepdims=True)
    rstd = lax.rsqrt(var + LN_EPS)
    xhat = xc * rstd
    return xhat * g + b, xhat, rstd


def _ln_bwd(dy, xhat, rstd, g):
    dxhat = dy * g
    m1 = jnp.mean(dxhat, axis=-1, keepdims=True)
    m2 = jnp.mean(dxhat * xhat, axis=-1, keepdims=True)
    return rstd * (dxhat - m1 - xhat * m2)


def _rowsum(x):
    return jnp.sum(x, axis=0, keepdims=True)


def _col(ref, k):
    return ref[:, k * D_G:(k + 1) * D_G]


def _head_of_lane(shape):
    return jnp.right_shift(lax.broadcasted_iota(jnp.int32, shape, len(shape) - 1), HEAD_DIM.bit_length() - 1)


def _pool_select(lane_grp, s2, s4, s8, s16):
    return jnp.where(lane_grp == 0, s2, jnp.where(lane_grp == 1, s4, jnp.where(lane_grp == 2, s8, s16)))


def _row_view(ref, layer):
    return ref.at[pl.ds(layer, 1)]


def _make_residues(ext_ref, res_ref):
    rows = res_ref.shape[1]
    for r in range(1, 8):
        res_ref[r - 1] = ext_ref[pl.ds(r, rows), :]


def _rows_at(ext_ref, res_ref, off, tile):
    a, r = divmod(off, 8)
    if r == 0:
        return ext_ref[pl.ds(off, tile), :]
    return res_ref[r - 1, pl.ds(8 * a, tile), :]


def _residue_scratch(tile):
    return pltpu.VMEM((7, HALO + tile - 8, D_G), F32)


def _branch_forward(p_ref, ph_ref, first, row0, km_ref, vm_ref, w, ext_a, ext_c, ext_d, res_c, res_d, tile,
                    conv_d_ref=None):
    r = {}
    xa, ba, ca = _col(p_ref, C_XA), _col(p_ref, C_BA), _col(p_ref, C_CA)
    ext_a[0:HALO] = jnp.where(first, 0.0, _col(ph_ref, C_CA) * _col(ph_ref, C_XA))
    ext_a[HALO:HALO + tile] = ca * xa
    conv_a = w["conv_a"][0:1, :] * ext_a[pl.ds(HALO - 2, tile), :]
    for k in range(1, CONV_A):
        conv_a = conv_a + w["conv_a"][k:k + 1, :] * ext_a[pl.ds(HALO - 2 + k, tile), :]
    r.update(xa=xa, ba=ba, ca=ca, conv_a=conv_a)
    ya = ba * conv_a

    xc = _col(p_ref, C_XC)
    ext_c[0:HALO] = jnp.where(first, 0.0, _col(ph_ref, C_XC))
    ext_c[HALO:HALO + tile] = xc
    _make_residues(ext_c, res_c)
    acc = xc
    sums = {}
    for k in range(1, POOL_WINDOWS[-1]):
        acc = acc + _rows_at(ext_c, res_c, HALO - k, tile)
        if k + 1 in POOL_WINDOWS:
            sums[k + 1] = acc
    lane_grp = _head_of_lane((tile, D_G))
    trow = row0 + lax.broadcasted_iota(jnp.int32, (tile, D_G), 0)
    win = _pool_select(lane_grp, 2, 4, 8, 16)
    cnt = jnp.minimum(trow + 1, win).astype(F32)
    inv_cnt = 1.0 / cnt
    ypre = _pool_select(lane_grp, sums[2], sums[4], sums[8], sums[16]) * inv_cnt - xc
    pool_mm = _mm(ypre, w["pool_wbd"][...])
    yc = pool_mm * w["pool_scale"][...]
    r.update(lane_grp=lane_grp, inv_cnt=inv_cnt, ypre=ypre, pool_mm=pool_mm)

    da, dg = _col(p_ref, C_DA), _col(p_ref, C_DG)
    sig_dg = _sigmoid(dg)
    ext_d[0:HALO] = jnp.where(first, 0.0, _col(ph_ref, C_DA) * _sigmoid(_col(ph_ref, C_DG)))
    ext_d[HALO:HALO + tile] = da * sig_dg
    _make_residues(ext_d, res_d)
    if conv_d_ref is None:
        conv_d = w["cc_dw_b"][...] + w["cc_dw_w"][0:1, :] * _rows_at(ext_d, res_d, HALO - (CONV_D - 1), tile)
        for j in range(1, CONV_D):
            conv_d = conv_d + w["cc_dw_w"][j:j + 1, :] * _rows_at(ext_d, res_d, HALO - (CONV_D - 1) + j, tile)
    else:
        conv_d = conv_d_ref[...]
    r["conv_d"] = conv_d
    ln_d, xhat_d, rstd_d = _ln_fwd(conv_d, w["cc_ln_g"][...], w["cc_ln_b"][...])
    sig_ln = _sigmoid(ln_d)
    act_d = ln_d * sig_ln
    yd = _mm(act_d, w["cc_pw_w"][...])
    r.update(da=da, sig_dg=sig_dg, ln_d=ln_d, xhat_d=xhat_d, rstd_d=rstd_d, sig_ln=sig_ln, act_d=act_d)

    u, v = _col(p_ref, C_U), _col(p_ref, C_V)
    ug, th_u = _gelu(u)
    vg, th_v = _gelu(v)
    vn, xhat_v, rstd_v = _ln_fwd(vg, w["sg_ln_g"][...], w["sg_ln_b"][...])
    tri = (lax.broadcasted_iota(jnp.int32, (CHUNK, CHUNK), 0)
           >= lax.broadcasted_iota(jnp.int32, (CHUNK, CHUNK), 1))
    wm = [jnp.where(tri, w["sg_w"][h], 0.0).astype(MM_DTYPE) for h in range(N_SUB)]
    lo = lax.broadcasted_iota(jnp.int32, (CHUNK, 2 * HEAD_DIM), 1) < HEAD_DIM
    chunks = []
    for c in range(tile // CHUNK):
        halves = []
        for hf in range(2):
            vh = vn[c * CHUNK:(c + 1) * CHUNK, hf * 128:(hf + 1) * 128]
            halves.append(_mm(wm[2 * hf], jnp.where(lo, vh, 0.0)) + _mm(wm[2 * hf + 1], jnp.where(lo, 0.0, vh)))
        chunks.append(jnp.concatenate(halves, axis=1) + w["sg_bias"][...])
    mixed = jnp.concatenate(chunks, axis=0)
    yb = ug * mixed
    r.update(u=u, v=v, ug=ug, th_u=th_u, th_v=th_v, vn=vn, xhat_v=xhat_v, rstd_v=rstd_v, wm=wm, lo=lo,
             mixed=mixed, tri=tri)

    q = _col(p_ref, C_Q)
    ye = jnp.zeros((tile, D_G), F32)
    probs = []
    for h in range(N_SUB):
        s = _mm_nt(q, km_ref[h]) * ATT_SCALE
        e = jnp.exp(s - jnp.max(s, axis=-1, keepdims=True))
        p = e * (1.0 / jnp.sum(e, axis=-1, keepdims=True))
        probs.append(p)
        ye = ye + _mm(p, vm_ref[h])
    r.update(q=q, probs=probs)

    gate = p_ref[:, C_GATE:C_GATE + D_MIX]
    sig_gate = _sigmoid(gate)
    concat = jnp.concatenate([ya, yb, yc, yd, ye], axis=1)
    r.update(gate=gate, sig_gate=sig_gate, concat=concat)
    return r


_BRANCH_REPL = ("sg_ln_g", "sg_ln_b", "sg_w", "sg_b", "pool_w", "pool_scale", "cc_dw_b", "cc_ln_g", "cc_ln_b")
_BRANCH_W_SCRATCH = (("conv_a", (CONV_A, D_G)), ("cc_dw_w", (CONV_D, D_G)), ("sg_bias", (CHUNK, D_G)),
                     ("pool_wbd", (D_G, D_G)), ("sgb8", (8, CHUNK)))


def _branch_weights(layer, nat, pw_ref, ca_ref, dw_ref, scr, init):
    @pl.when(init)
    def _():
        for p in range(N_DEV):
            scr["conv_a"][:, p * CONV_CH:(p + 1) * CONV_CH] = ca_ref[p, :, layer, :]
            scr["cc_dw_w"][:, p * CONV_CH:(p + 1) * CONV_CH] = dw_ref[p, :, layer, :]
        scr["sgb8"][...] = jnp.zeros((8, CHUNK), F32)
        scr["sgb8"][0:N_SUB] = nat["sg_b"][layer]
        sgb_t = scr["sgb8"][...].T
        head = _head_of_lane((CHUNK, D_G))
        bias = jnp.zeros((CHUNK, D_G), F32)
        for h in range(N_SUB):
            bias = jnp.where(head == h, sgb_t[:, h:h + 1], bias)
        scr["sg_bias"][...] = bias
        scr["pool_wbd"][...] = jnp.zeros((D_G, D_G), F32)
        for gi in range(N_SUB):
            sl = slice(gi * HEAD_DIM, (gi + 1) * HEAD_DIM)
            scr["pool_wbd"][sl, sl] = nat["pool_w"][layer, gi]

    w = {n: _row_view(nat[n], layer) for n in ("sg_ln_g", "sg_ln_b", "pool_scale", "cc_dw_b", "cc_ln_g", "cc_ln_b")}
    w.update(conv_a=scr["conv_a"], cc_dw_w=scr["cc_dw_w"], sg_bias=scr["sg_bias"], pool_wbd=scr["pool_wbd"],
             sg_w=nat["sg_w"].at[layer], cc_pw_w=pw_ref)
    return w


def _full_spec(a):
    nd = a.ndim
    return pl.BlockSpec(a.shape, lambda *_, _nd=nd: (0,) * _nd)


def _tie_specs(ties):
    return [pl.BlockSpec((8, 128), lambda *_: (0, 0)) for _ in ties]


def _params(*sem):
    return pltpu.CompilerParams(dimension_semantics=sem or None, vmem_limit_bytes=VMEM_LIMIT)


def _proj_matmul(x, wt_in, ties=(), tm=256):
    s, k = x.shape

    def body(x_ref, w_ref, *rest):
        o_ref = rest[len(ties)]
        o_ref[...] = _mm_nt(x_ref[...], w_ref[...])

    return pl.pallas_call(
        body, grid=(s // tm,),
        in_specs=[pl.BlockSpec((tm, k), lambda i: (i, 0)), _full_spec(wt_in)] + _tie_specs(ties),
        out_specs=pl.BlockSpec((tm, D_IN), lambda i: (i, 0)),
        out_shape=jax.ShapeDtypeStruct((s, D_IN), F32), name="proj_mm", compiler_params=_params("arbitrary"),
    )(x, wt_in, *ties)


def _kv_project(mem, w_kv):
    def body(mem_ref, w_ref, km_ref, vm_ref):
        kv = _mm(mem_ref[...], w_ref[...])
        k, v = kv[:, :D_G], kv[:, D_G:]
        grp = _head_of_lane((MEM_LEN, D_G))
        for h in range(N_SUB):
            km_ref[h] = jnp.where(grp == h, k, 0.0).astype(km_ref.dtype)
            vm_ref[h] = jnp.where(grp == h, v, 0.0).astype(vm_ref.dtype)

    shp = jax.ShapeDtypeStruct((N_SUB, MEM_LEN, D_G), MM_DTYPE)
    return pl.pallas_call(body, out_shape=(shp, shp), name="kv_project", compiler_params=_params())(mem, w_kv)


def _layer_fwd_fused(x, wt_in, proj, km, vm, layer, repl, pw, ca8, dw8, w_out, ties=(), tile=TILE):
    s = x.shape[0]
    hb = tile // HALO
    nat_arrays = [repl[n] for n in _BRANCH_REPL]
    n_nat, nt = len(nat_arrays), len(ties)
    given = proj is not None

    def body(x_ref, *rest):
        if given:
            p_ref, ph_ref = rest[:2]
            rest = rest[2:]
        else:
            wt_ref = rest[0]
            rest = rest[1:]
        km_ref, vm_ref = rest[:2]
        nat = dict(zip(_BRANCH_REPL, rest[2:2 + n_nat]))
        pw_ref, ca_ref, dw_ref, wo_ref, g_ref, b_ref = rest[2 + n_nat:8 + n_nat]
        rest = rest[8 + n_nat + nt:]
        if not given:
            p_ref, rest = rest[0], rest[1:]
        h_ref, z_ref, xn_ref, cd_ref = rest[:4]
        rest = rest[4:]
        if not given:
            ph_ref, rest = rest[0], rest[1:]
        ext_a, ext_c, ext_d, res_c, res_d = rest[:5]
        scr = dict(zip([n for n, _ in _BRANCH_W_SCRATCH], rest[5:]))
        i = pl.program_id(0)
        xt = x_ref[...]
        if not given:
            @pl.when(i == 0)
            def _():
                ph_ref[...] = jnp.zeros_like(ph_ref)

            p_ref[...] = _mm_nt(xt, wt_ref[...])
        w = _branch_weights(layer, nat, pw_ref, ca_ref, dw_ref, scr, i == 0)
        r = _branch_forward(p_ref, ph_ref, i == 0, i * tile, km_ref, vm_ref, w, ext_a, ext_c, ext_d, res_c, res_d,
                            tile)
        if not given:
            ph_ref[...] = p_ref[tile - HALO:tile, :]
        h = (r["concat"] * (r["gate"] * r["sig_gate"])).astype(h_ref.dtype)
        h_ref[...] = h
        cd_ref[...] = r["conv_d"]
        z = ALPHA * xt + _mm(h, wo_ref[...])
        z_ref[...] = z
        xn_ref[...] = _ln_fwd(z, _row_view(g_ref, layer)[...], _row_view(b_ref, layer)[...])[0]

    row = lambda i: (i, 0)
    consts = [km, vm] + nat_arrays + [pw, ca8, dw8, w_out, repl["ln_g"], repl["ln_b"]]
    act = jax.ShapeDtypeStruct((s, D_MODEL), F32)
    act_spec = pl.BlockSpec((tile, D_MODEL), row)
    if given:
        lead = [proj, proj]
        lead_specs = [pl.BlockSpec((tile, D_IN), row),
                      pl.BlockSpec((HALO, D_IN), lambda i: (jnp.maximum(i * hb - 1, 0), 0))]
        out_specs, out_shape, scratch = (), (), []
    else:
        lead = [wt_in]
        lead_specs = [_full_spec(wt_in)]
        out_specs = (pl.BlockSpec((tile, D_IN), row),)
        out_shape = (jax.ShapeDtypeStruct((s, D_IN), F32),)
        scratch = [pltpu.VMEM((HALO, D_IN), F32)]
    res = pl.pallas_call(
        body, grid=(s // tile,),
        in_specs=[act_spec] + lead_specs + [_full_spec(a) for a in consts] + _tie_specs(ties),
        out_specs=out_specs + (pl.BlockSpec((tile, D_MIX), row), act_spec, act_spec, pl.BlockSpec((tile, D_G), row)),
        out_shape=out_shape + (jax.ShapeDtypeStruct((s, D_MIX), MM_DTYPE), act, act,
                               jax.ShapeDtypeStruct((s, D_G), F32)),
        scratch_shapes=scratch + [pltpu.VMEM((HALO + tile, D_G), F32)] * 3
        + [_residue_scratch(tile)] * 2 + [pltpu.VMEM(shape, F32) for _, shape in _BRANCH_W_SCRATCH],
        name="layer_fwd_given_proj" if given else "layer_fwd", compiler_params=_params("arbitrary"),
    )(x, *lead, *consts, *ties)
    return ((proj,) + tuple(res)) if given else tuple(res)


def _ln_out_bwd(up, target, z, ln_g, ln_b, layer, w_out, h, ties=(), tm=256):
    s = z.shape[0]
    from_loss = target is not None
    other = target if from_loss else up
    nt = len(ties)

    def body(o_ref, z_ref, g_ref, b_ref, w_ref, h_ref, *rest):
        dz_ref, dh_ref, gw_ref, slab_ref, gw_acc, lacc = rest[nt:]
        i = pl.program_id(0)

        @pl.when(i == 0)
        def _():
            gw_acc[...] = jnp.zeros_like(gw_acc)
            slab_ref[...] = jnp.zeros_like(slab_ref)
            lacc[...] = jnp.zeros_like(lacc)

        g = _row_view(g_ref, layer)[...]
        xn, xhat, rstd = _ln_fwd(z_ref[...], g, _row_view(b_ref, layer)[...])
        if from_loss:
            err = xn - o_ref[...]
            lacc[...] += _rowsum(err * err)
            dxn = err * (1.0 / D_MODEL)
        else:
            dxn = o_ref[...]
        slab_ref[0:1, :] += _rowsum(dxn * xhat)
        slab_ref[1:2, :] += _rowsum(dxn)
        dz = _ln_bwd(dxn, xhat, rstd, g)
        dz_ref[...] = dz
        dh_ref[...] = _mm_nt(dz, w_ref[...])
        gw_acc[...] += _mm_tn(h_ref[...], dz)

        @pl.when(i == pl.num_programs(0) - 1)
        def _():
            gw_ref[...] = gw_acc[...].astype(gw_ref.dtype)
            if from_loss:
                total = jnp.sum(lacc[...], axis=-1, keepdims=True) * (0.5 / D_MODEL)
                slab_ref[LOSS_ROW:LOSS_ROW + 1, :] = jnp.broadcast_to(total, (1, D_MODEL))

    row = lambda i: (i, 0)
    fixed = lambda i: (0, 0)
    return pl.pallas_call(
        body, grid=(s // tm,),
        in_specs=[pl.BlockSpec((tm, D_MODEL), row), pl.BlockSpec((tm, D_MODEL), row), _full_spec(ln_g),
                  _full_spec(ln_b), _full_spec(w_out), pl.BlockSpec((tm, D_MIX), row)] + _tie_specs(ties),
        out_specs=(pl.BlockSpec((tm, D_MODEL), row), pl.BlockSpec((tm, D_MIX), row),
                   pl.BlockSpec((D_MIX, D_MODEL), fixed), pl.BlockSpec((8, D_MODEL), fixed)),
        out_shape=(jax.ShapeDtypeStruct((s, D_MODEL), F32), jax.ShapeDtypeStruct((s, D_MIX), F32),
                   jax.ShapeDtypeStruct((D_MIX, D_MODEL), GRAD_DTYPE), jax.ShapeDtypeStruct((8, D_MODEL), F32)),
        scratch_shapes=[pltpu.VMEM((D_MIX, D_MODEL), F32), pltpu.VMEM((1, D_MODEL), F32)],
        name="ln_out_bwd_loss" if from_loss else "ln_out_bwd", compiler_params=_params("arbitrary"),
    )(other, z, ln_g, ln_b, w_out, h, *ties)


_BRANCH_GRADS = (("g256", (8, D_G)), ("sg_w", (N_SUB, CHUNK, CHUNK)), ("sg_b", (8, CHUNK)),
                 ("pool_w", (N_SUB, HEAD_DIM, HEAD_DIM)), ("conv_a_w", (N_DEV, CONV_A, CONV_CH)),
                 ("cc_dw_w", (N_DEV, CONV_D, CONV_CH)), ("cc_pw_w", (D_G, D_G)),
                 ("dk", (N_SUB, MEM_LEN, D_G)), ("dv", (N_SUB, MEM_LEN, D_G)))
_BRANCH_ACC = (("conv_a", (CONV_A, D_G)), ("cc_dw_w", (CONV_D, D_G)), ("pool_wbd", (D_G, D_G)),
               ("sg_bias", (CHUNK, D_G)))


def _branch_bwd(proj, conv_d, dh, km, vm, layer, repl, pw, ca8, dw8, tile=TILE):
    s = proj.shape[0]
    nt = s // tile
    hb = tile // HALO
    nat_arrays = [repl[n] for n in _BRANCH_REPL]
    n_nat, n_grads, n_acc = len(nat_arrays), len(_BRANCH_GRADS), len(_BRANCH_ACC)
    row_of = {n: k for k, n in enumerate(G256_ROWS)}

    def body(p_ref, ph_ref, cd_ref, dh_ref, km_ref, vm_ref, *rest):
        nat = dict(zip(_BRANCH_REPL, rest[:n_nat]))
        pw_ref, ca_ref, dw_ref = rest[n_nat:n_nat + 3]
        rest = rest[n_nat + 3:]
        dp_ref = rest[0]
        g = dict(zip([n for n, _ in _BRANCH_GRADS], rest[1:1 + n_grads]))
        rest = rest[1 + n_grads:]
        ext_a, ext_c, ext_d, rev_a, rev_c, rev_d, res_c, res_d, res_rc, res_rd = rest[:10]
        acc = dict(zip([n for n, _ in _BRANCH_ACC], rest[10:10 + n_acc]))
        scr = dict(zip([n for n, _ in _BRANCH_W_SCRATCH], rest[10 + n_acc:]))
        i = pl.program_id(0)
        t = nt - 1 - i

        @pl.when(i == 0)
        def _():
            for ref in list(g.values()) + list(acc.values()) + [rev_a, rev_c, rev_d]:
                ref[...] = jnp.zeros_like(ref)

        w = _branch_weights(layer, nat, pw_ref, ca_ref, dw_ref, scr, i == 0)
        r = _branch_forward(p_ref, ph_ref, t == 0, t * tile, km_ref, vm_ref, w, ext_a, ext_c, ext_d, res_c, res_d,
                            tile, cd_ref)

        def put(k, val, width=D_G):
            dp_ref[:, k:k + width] = val.astype(dp_ref.dtype)

        def add_row(name, val):
            k = row_of[name]
            g["g256"][k:k + 1, :] += val

        def push_rev(rev, val):
            head = rev[0:HALO]
            rev[tile:tile + HALO] = head
            rev[0:tile] = val

        dh_all = dh_ref[...]
        gate, sig_gate, concat = r["gate"], r["sig_gate"], r["concat"]
        put(C_GATE, dh_all * concat * (sig_gate * (1.0 + gate * (1.0 - sig_gate))), D_MIX)
        dconcat = dh_all * (gate * sig_gate)
        dya, dyb, dyc, dyd, dye = [dconcat[:, k * D_G:(k + 1) * D_G] for k in range(N_GROUPS)]

        put(C_BA * D_G, dya * r["conv_a"])
        dconv_a = dya * r["ba"]
        for k in range(CONV_A):
            acc["conv_a"][k:k + 1, :] += _rowsum(dconv_a * ext_a[pl.ds(HALO - 2 + k, tile), :])
        push_rev(rev_a, dconv_a)
        dga = w["conv_a"][0:1, :] * rev_a[pl.ds(2, tile), :]
        for k in range(1, CONV_A):
            dga = dga + w["conv_a"][k:k + 1, :] * rev_a[pl.ds(2 - k, tile), :]
        put(C_CA * D_G, dga * r["xa"])
        put(C_XA * D_G, dga * r["ca"])

        add_row("pool_scale", _rowsum(dyc * r["pool_mm"]))
        dmm = dyc * w["pool_scale"][...]
        acc["pool_wbd"][...] += _mm_tn(r["ypre"], dmm)
        dypre = _mm_nt(dmm, w["pool_wbd"][...])
        dws = dypre * r["inv_cnt"]
        push_rev(rev_c, dws)
        _make_residues(rev_c, res_rc)
        run = dws
        sums = {}
        for k in range(1, POOL_WINDOWS[-1]):
            run = run + _rows_at(rev_c, res_rc, k, tile)
            if k + 1 in POOL_WINDOWS:
                sums[k + 1] = run
        put(C_XC * D_G, _pool_select(r["lane_grp"], sums[2], sums[4], sums[8], sums[16]) - dypre)

        g["cc_pw_w"][...] += _mm_tn(r["act_d"], dyd)
        dact = _mm_nt(dyd, w["cc_pw_w"][...])
        sig_ln, ln_d = r["sig_ln"], r["ln_d"]
        dln = dact * (sig_ln * (1.0 + ln_d * (1.0 - sig_ln)))
        add_row("cc_ln_g", _rowsum(dln * r["xhat_d"]))
        add_row("cc_ln_b", _rowsum(dln))
        dconv_d = _ln_bwd(dln, r["xhat_d"], r["rstd_d"], w["cc_ln_g"][...])
        add_row("cc_dw_b", _rowsum(dconv_d))
        for j in range(CONV_D):
            acc["cc_dw_w"][j:j + 1, :] += _rowsum(dconv_d * _rows_at(ext_d, res_d, HALO - (CONV_D - 1) + j, tile))
        push_rev(rev_d, dconv_d)
        _make_residues(rev_d, res_rd)
        dhd = w["cc_dw_w"][0:1, :] * _rows_at(rev_d, res_rd, CONV_D - 1, tile)
        for j in range(1, CONV_D):
            dhd = dhd + w["cc_dw_w"][j:j + 1, :] * _rows_at(rev_d, res_rd, CONV_D - 1 - j, tile)
        sig_dg = r["sig_dg"]
        put(C_DA * D_G, dhd * sig_dg)
        put(C_DG * D_G, dhd * r["da"] * sig_dg * (1.0 - sig_dg))

        dug = dyb * r["mixed"]
        dmixed = dyb * r["ug"]
        wm, lo, vn = r["wm"], r["lo"], r["vn"]
        dvn_chunks = []
        for c in range(tile // CHUNK):
            rows = slice(c * CHUNK, (c + 1) * CHUNK)
            acc["sg_bias"][...] += dmixed[rows, :]
            halves = []
            for hf in range(2):
                cols = slice(hf * 128, (hf + 1) * 128)
                dm = dmixed[rows, cols]
                dm_a, dm_b = jnp.where(lo, dm, 0.0), jnp.where(lo, 0.0, dm)
                vh = vn[rows, cols]
                g["sg_w"][2 * hf] += _mm_nt(dm_a, vh)
                g["sg_w"][2 * hf + 1] += _mm_nt(dm_b, vh)
                halves.append(_mm_tn(wm[2 * hf], dm_a) + _mm_tn(wm[2 * hf + 1], dm_b))
            dvn_chunks.append(jnp.concatenate(halves, axis=1))
        dvn = jnp.concatenate(dvn_chunks, axis=0)
        add_row("sg_ln_g", _rowsum(dvn * r["xhat_v"]))
        add_row("sg_ln_b", _rowsum(dvn))
        dvg = _ln_bwd(dvn, r["xhat_v"], r["rstd_v"], w["sg_ln_g"][...])
        put(C_V * D_G, dvg * _dgelu(r["v"], r["th_v"]))
        put(C_U * D_G, dug * _dgelu(r["u"], r["th_u"]))

        q = r["q"]
        dq = jnp.zeros((tile, D_G), F32)
        for h in range(N_SUB):
            p = r["probs"][h]
            dp = _mm_nt(dye, vm_ref[h])
            g["dv"][h] += _mm_tn(p, dye)
            ds = p * (dp - jnp.sum(dp * p, axis=-1, keepdims=True)) * ATT_SCALE
            dq = dq + _mm(ds, km_ref[h])
            g["dk"][h] += _mm_tn(ds, q)
        put(C_Q * D_G, dq)

        @pl.when(i == nt - 1)
        def _():
            for h in range(N_SUB):
                g["sg_w"][h] = jnp.where(r["tri"], g["sg_w"][h], 0.0)
            lane_head = _head_of_lane((CHUNK, D_G))
            col_of = lax.broadcasted_iota(jnp.int32, (CHUNK, 8), 1)
            ba = acc["sg_bias"][...]
            sgb_t = jnp.zeros((CHUNK, 8), F32)
            for h in range(N_SUB):
                col = jnp.sum(jnp.where(lane_head == h, ba, 0.0), axis=-1, keepdims=True)
                sgb_t = jnp.where(col_of == h, col, sgb_t)
            g["sg_b"][...] = sgb_t.T
            wbd = acc["pool_wbd"][...]
            for gi in range(N_SUB):
                sl = slice(gi * HEAD_DIM, (gi + 1) * HEAD_DIM)
                g["pool_w"][gi] = wbd[sl, sl]
            ca, dw = acc["conv_a"][...], acc["cc_dw_w"][...]
            for p in range(N_DEV):
                g["conv_a_w"][p] = ca[:, p * CONV_CH:(p + 1) * CONV_CH]
                g["cc_dw_w"][p] = dw[:, p * CONV_CH:(p + 1) * CONV_CH]

    rev = lambda i: (nt - 1 - i, 0)
    grad_specs = tuple(pl.BlockSpec(shape, lambda i, _nd=len(shape): (0,) * _nd) for _, shape in _BRANCH_GRADS)
    grad_shapes = tuple(jax.ShapeDtypeStruct(shape, F32) for _, shape in _BRANCH_GRADS)
    outs = pl.pallas_call(
        body, grid=(nt,),
        in_specs=[pl.BlockSpec((tile, D_IN), rev),
                  pl.BlockSpec((HALO, D_IN), lambda i: (jnp.maximum((nt - 1 - i) * hb - 1, 0), 0)),
                  pl.BlockSpec((tile, D_G), rev), pl.BlockSpec((tile, D_MIX), rev), _full_spec(km), _full_spec(vm)]
        + [_full_spec(a) for a in nat_arrays + [pw, ca8, dw8]],
        out_specs=(pl.BlockSpec((tile, D_IN), rev),) + grad_specs,
        out_shape=(jax.ShapeDtypeStruct((s, D_IN), MM_DTYPE),) + grad_shapes,
        scratch_shapes=[pltpu.VMEM((HALO + tile, D_G), F32)] * 6 + [_residue_scratch(tile)] * 4
        + [pltpu.VMEM(shape, F32) for _, shape in _BRANCH_ACC + _BRANCH_W_SCRATCH],
        name="branch_bwd", compiler_params=_params("arbitrary"),
    )(proj, proj, conv_d, dh, km, vm, *nat_arrays, pw, ca8, dw8)
    return outs[0], dict(zip([n for n, _ in _BRANCH_GRADS], outs[1:]))


def _dx_matmul(dproj, wt_in, dz, ties=(), tm=256):
    s = dproj.shape[0]

    def body(dp_ref, w_ref, dz_ref, *rest):
        o_ref = rest[len(ties)]
        o_ref[...] = _mm(dp_ref[...], w_ref[...]) + ALPHA * dz_ref[...]

    row = lambda i: (i, 0)
    return pl.pallas_call(
        body, grid=(s // tm,),
        in_specs=[pl.BlockSpec((tm, D_IN), row), _full_spec(wt_in), pl.BlockSpec((tm, D_MODEL), row)]
        + _tie_specs(ties),
        out_specs=pl.BlockSpec((tm, D_MODEL), row),
        out_shape=jax.ShapeDtypeStruct((s, D_MODEL), F32), name="dx_mm", compiler_params=_params("arbitrary"),
    )(dproj, wt_in, dz, *ties)


def _dw_in_matmul(x, dproj, ties=(), tk=512):
    s = x.shape[0]
    tn = 2 * W_IN_COLS
    nk = s // tk

    def body(x_ref, dp_ref, *rest):
        o_ref, acc = rest[len(ties):]
        k = pl.program_id(1)

        @pl.when(k == 0)
        def _():
            acc[...] = jnp.zeros_like(acc)

        acc[...] += _mm_tn(dp_ref[...], x_ref[...])

        @pl.when(k == nk - 1)
        def _():
            o_ref[...] = acc[...].astype(o_ref.dtype)

    return pl.pallas_call(
        body, grid=(D_IN // tn, nk),
        in_specs=[pl.BlockSpec((tk, D_MODEL), lambda j, k: (k, 0)), pl.BlockSpec((tk, tn), lambda j, k: (k, j))]
        + _tie_specs(ties),
        out_specs=pl.BlockSpec((tn, D_MODEL), lambda j, k: (j, 0)),
        out_shape=jax.ShapeDtypeStruct((D_IN, D_MODEL), GRAD_DTYPE),
        scratch_shapes=[pltpu.VMEM((tn, D_MODEL), F32)], name="dw_in_mm",
        compiler_params=_params("arbitrary", "arbitrary"),
    )(x, dproj, *ties)


def _in_bwd(x, dproj, wt_in, dz, tm=256):
    s = x.shape[0]
    n_steps = s // tm

    assert wt_in.dtype == GRAD_DTYPE
    blk = 2 * W_IN_COLS

    def body(x_ref, dp_ref, w_hbm, dz_ref, o_ref, gw_hbm, w_vmem, acc, sem):
        i = pl.program_id(0)

        @pl.when(i == 0)
        def _():
            fetch = pltpu.make_async_copy(w_hbm, w_vmem, sem)
            fetch.start()
            acc[...] = jnp.zeros_like(acc)
            fetch.wait()

        o_ref[...] = _mm(dp_ref[...], w_vmem[...]) + ALPHA * dz_ref[...]
        xb = x_ref[...].astype(MM_DTYPE)
        for j in range(D_IN // blk):
            acc[j * blk:(j + 1) * blk, :] += _mm_tn(dp_ref[:, j * blk:(j + 1) * blk], xb)

        @pl.when(i == n_steps - 1)
        def _():
            w_vmem[...] = acc[...].astype(w_vmem.dtype)
            emit = pltpu.make_async_copy(w_vmem, gw_hbm, sem)
            emit.start()
            emit.wait()

    row = lambda i: (i, 0)
    any_spec = pl.BlockSpec(memory_space=pl.ANY)
    return pl.pallas_call(
        body, grid=(n_steps,),
        in_specs=[pl.BlockSpec((tm, D_MODEL), row), pl.BlockSpec((tm, D_IN), row), any_spec,
                  pl.BlockSpec((tm, D_MODEL), row)],
        out_specs=(pl.BlockSpec((tm, D_MODEL), row), any_spec),
        out_shape=(jax.ShapeDtypeStruct((s, D_MODEL), F32), jax.ShapeDtypeStruct((D_IN, D_MODEL), GRAD_DTYPE)),
        scratch_shapes=[pltpu.VMEM((D_IN, D_MODEL), wt_in.dtype), pltpu.VMEM((D_IN, D_MODEL), F32),
                        pltpu.SemaphoreType.DMA],
        name="in_bwd", compiler_params=_params("arbitrary"),
    )(x, dproj, wt_in, dz)


def _kv_bwd(mem, dk, dv):
    def body(mem_ref, dk_ref, dv_ref, o_ref):
        grp = _head_of_lane((MEM_LEN, D_G))
        dk_sum = jnp.zeros((MEM_LEN, D_G), F32)
        dv_sum = jnp.zeros((MEM_LEN, D_G), F32)
        for h in range(N_SUB):
            dk_sum = dk_sum + jnp.where(grp == h, dk_ref[h], 0.0)
            dv_sum = dv_sum + jnp.where(grp == h, dv_ref[h], 0.0)
        o_ref[...] = _mm_tn(mem_ref[...], jnp.concatenate([dk_sum, dv_sum], axis=1)).astype(o_ref.dtype)

    return pl.pallas_call(body, out_shape=jax.ShapeDtypeStruct((D_MODEL, 2 * D_G), GRAD_DTYPE), name="kv_bwd",
                          compiler_params=_params())(mem, dk, dv)


def _layer_fwd(x, mem, layer, repl, gw, proj=None, km_vm=None, ties=()):
    km, vm = _kv_project(mem, gw["w_kv"]) if km_vm is None else km_vm
    proj, h, z, xn, conv_d = _layer_fwd_fused(x, gw["wt_in"] if proj is None else None, proj, km, vm, layer, repl,
                                              gw["pw"], gw["ca8"], gw["dw8"], gw["w_out"], ties)
    return xn, (x, proj, h, z, km, vm, conv_d)


def _layer_bwd_a(up, target, mem, layer, repl, gw, saved, ties=()):
    x_in, proj, h, z, km, vm, conv_d = saved
    dz, dh, g_w_out, g1024 = _ln_out_bwd(up, target, z, repl["ln_g"], repl["ln_b"], layer, gw["w_out"], h, ties)
    dproj, bg = _branch_bwd(proj, conv_d, dh, km, vm, layer, repl, gw["pw"], gw["ca8"], gw["dw8"])
    grads = {n: bg[n] for n in ("g256", "sg_w", "sg_b", "pool_w", "conv_a_w", "cc_dw_w")}
    grads.update(w_out=g_w_out.reshape(N_DEV, D_MIX // N_DEV, D_MODEL), g1024=g1024,
                 w_kv=_kv_bwd(mem, bg["dk"], bg["dv"]).reshape(N_DEV, D_MODEL // N_DEV, 2 * D_G),
                 cc_pw_w=bg["cc_pw_w"].reshape(N_DEV, CONV_CH, D_G))
    return dz, dproj, grads


def _landing_shapes(items):
    out = []
    for a, scatter, pick in items:
        shape = a.shape if scatter else (N_DEV,) + (a.shape if pick is None else a.shape[1:])
        out.append(jax.ShapeDtypeStruct(shape, a.dtype))
    return tuple(out)


def _exchange_sems(n):
    return [pltpu.SemaphoreType.DMA(((N_DEV - 1) * n,)), pltpu.SemaphoreType.DMA(((N_DEV - 1) * n,)),
            pltpu.SemaphoreType.DMA((n,))]


def _exchange_copies(modes, ins, outs, send_sems, recv_sems, local_sems):
    n = len(ins)
    x, y, c = lax.axis_index("x"), lax.axis_index("y"), lax.axis_index("c")
    me = 4 * x + 2 * y + c

    def src_of(a, dest):
        scatter, pick = modes[a]
        if scatter:
            return ins[a].at[dest]
        return ins[a] if pick is None else ins[a].at[pick]

    local = [pltpu.make_async_copy(src_of(a, me), outs[a].at[me], local_sems.at[a]) for a in range(n)]
    sends, recvs = [], []
    for k in range(1, N_DEV):
        px = 1 - x if k & 4 else x
        py = 1 - y if k & 2 else y
        pc = 1 - c if k & 1 else c
        peer = 4 * px + 2 * py + pc
        for a in range(n):
            sems = dict(send_sem=send_sems.at[(k - 1) * n + a], recv_sem=recv_sems.at[(k - 1) * n + a],
                        device_id=(px, py, pc), device_id_type=pl.DeviceIdType.MESH)
            sends.append(pltpu.make_async_remote_copy(src_ref=src_of(a, peer), dst_ref=outs[a].at[me], **sems))
            recvs.append(pltpu.make_async_remote_copy(src_ref=src_of(a, peer), dst_ref=outs[a].at[peer], **sems))
    return local, sends, recvs


def _exchange(items, name):
    n = len(items)
    modes = [(scatter, pick) for _, scatter, pick in items]

    def body(*refs):
        local, sends, recvs = _exchange_copies(modes, refs[:n], refs[n:2 * n], *refs[2 * n:])
        for cp in local + sends:
            cp.start()
        for cp in recvs:
            cp.wait_recv()
        for cp in sends:
            cp.wait_send()
        for cp in local:
            cp.wait()

    any_spec = pl.BlockSpec(memory_space=pl.ANY)
    return pl.pallas_call(
        body, in_specs=[any_spec] * n, out_specs=(any_spec,) * n, out_shape=_landing_shapes(items),
        scratch_shapes=_exchange_sems(n), name=name,
    )(*[a for a, _, _ in items])


def _gather_two_level(items, name):
    n = len(items)
    assert not any(scatter for _, scatter, _ in items)
    picks = [pick for _, _, pick in items]

    def body(*refs):
        ins, outs = refs[:n], refs[n:2 * n]
        send_sems, recv_sems, local_sems = refs[2 * n:]
        x, y, c = lax.axis_index("x"), lax.axis_index("y"), lax.axis_index("c")
        sib = 1 - c
        chips = [(1 - x, y), (x, 1 - y), (1 - x, 1 - y)]

        def slot(a, px, py, pc):
            return outs[a].at[4 * px + 2 * py + pc]

        def copy(k, a, src, block, to):
            return pltpu.make_async_remote_copy(
                src_ref=src, dst_ref=slot(a, *block), send_sem=send_sems.at[k * n + a],
                recv_sem=recv_sems.at[k * n + a], device_id=to, device_id_type=pl.DeviceIdType.MESH)

        own = [ins[a] if picks[a] is None else ins[a].at[picks[a]] for a in range(n)]
        local = [pltpu.make_async_copy(own[a], slot(a, x, y, c), local_sems.at[a]) for a in range(n)]
        first = [copy(0, a, own[a], (x, y, c), (x, y, sib)) for a in range(n)]
        first += [copy(1 + j, a, own[a], (x, y, c), (*chip, c)) for j, chip in enumerate(chips) for a in range(n)]
        for cp in local + first:
            cp.start()
        passed = []
        for j, chip in enumerate(chips):
            for a in range(n):
                copy(1 + j, a, own[a], (*chip, c), (x, y, c)).wait_recv()
                fwd = copy(4 + j, a, slot(a, *chip, c), (*chip, c), (x, y, sib))
                fwd.start()
                passed.append(fwd)
        for a in range(n):
            copy(0, a, own[a], (x, y, sib), (x, y, c)).wait_recv()
        for j, chip in enumerate(chips):
            for a in range(n):
                copy(4 + j, a, own[a], (*chip, sib), (x, y, c)).wait_recv()
        for cp in first + passed:
            cp.wait_send()
        for cp in local:
            cp.wait()

    any_spec = pl.BlockSpec(memory_space=pl.ANY)
    return pl.pallas_call(
        body, in_specs=[any_spec] * n, out_specs=(any_spec,) * n, out_shape=_landing_shapes(items),
        scratch_shapes=[pltpu.SemaphoreType.DMA((7 * n,)), pltpu.SemaphoreType.DMA((7 * n,)),
                        pltpu.SemaphoreType.DMA((n,))],
        name=name,
    )(*[a for a, _, _ in items])


_HBM_SPEC = pl.BlockSpec(memory_space=pltpu.HBM)
_SEM_SPEC = pl.BlockSpec(memory_space=pltpu.SEMAPHORE)
_SPLIT_PARAMS = pltpu.CompilerParams(has_side_effects=pltpu.SideEffectType.DATAFLOW_SIDE_EFFECTING)


def _split_start(srcs, lands, plan, sem_shapes, name):
    n_src, n_land = len(srcs), len(lands)
    n_buf = n_src + n_land
    bufs = [pltpu.with_memory_space_constraint(a, pltpu.HBM) for a in list(srcs) + list(lands)]

    def body(*refs):
        local, sends, _ = plan(refs[:n_src], refs[n_src:n_buf], *refs[n_buf:n_buf + 3])
        for cp in local + sends:
            cp.start()
        token = refs[-1]
        token[...] = jnp.zeros_like(token)

    res = pl.pallas_call(
        body, name=name, in_specs=[_HBM_SPEC] * n_buf,
        out_shape=tuple(sem_shapes) + tuple(pltpu.HBM(a.shape, a.dtype) for a in bufs)
        + (jax.ShapeDtypeStruct((8, 128), F32),),
        out_specs=(_SEM_SPEC,) * 3 + (_HBM_SPEC,) * n_buf + (pl.BlockSpec(memory_space=pltpu.VMEM),),
        input_output_aliases={i: 3 + i for i in range(n_buf)}, compiler_params=_SPLIT_PARAMS,
    )(*bufs)
    return dict(sems=res[:3], srcs=res[3:3 + n_src], lands=res[3 + n_src:3 + n_buf], token=res[-1], plan=plan)


def _split_wait(ticket, after, name):
    n_src, n_land = len(ticket["srcs"]), len(ticket["lands"])
    n_buf = n_src + n_land
    plan = ticket["plan"]

    def body(*refs):
        local, sends, recvs = plan(refs[:n_src], refs[n_src:n_buf], *refs[n_buf:n_buf + 3])
        for cp in recvs:
            cp.wait_recv()
        for cp in sends:
            cp.wait_send()
        for cp in local:
            cp.wait()

    bufs = list(ticket["srcs"]) + list(ticket["lands"])
    res = pl.pallas_call(
        body, name=name, in_specs=[_HBM_SPEC] * n_buf + [_SEM_SPEC] * 3 + [pl.BlockSpec(memory_space=pl.ANY)],
        out_shape=tuple(pltpu.HBM(a.shape, a.dtype) for a in bufs), out_specs=(_HBM_SPEC,) * n_buf,
        input_output_aliases={i: i for i in range(n_buf)}, compiler_params=_SPLIT_PARAMS,
    )(*bufs, *ticket["sems"], after)
    return res[n_src:]


def _empty_landings(items):
    return [lax.empty(s.shape, s.dtype) for s in _landing_shapes(items)]


def _exchange_start(items, name):
    modes = [(scatter, pick) for _, scatter, pick in items]
    plan = lambda ins, outs, *sems: _exchange_copies(modes, ins, outs, *sems)
    return _split_start([a for a, _, _ in items], _empty_landings(items), plan, _exchange_sems(len(items)), name)


def _two_level_plans(picks):
    n = len(picks)

    def place():
        x, y, c = lax.axis_index("x"), lax.axis_index("y"), lax.axis_index("c")
        return x, y, c, 1 - c, [(1 - x, y), (x, 1 - y), (1 - x, 1 - y)]

    def copy(outs, send_sems, recv_sems, k, a, src, block, to):
        px, py, pc = block
        return pltpu.make_async_remote_copy(
            src_ref=src, dst_ref=outs[a].at[4 * px + 2 * py + pc], send_sem=send_sems.at[k * n + a],
            recv_sem=recv_sems.at[k * n + a], device_id=to, device_id_type=pl.DeviceIdType.MESH)

    def between_chips(ins, outs, send_sems, recv_sems, local_sems):
        x, y, c, sib, chips = place()
        own = [ins[a] if picks[a] is None else ins[a].at[picks[a]] for a in range(n)]
        mk = lambda *args: copy(outs, send_sems, recv_sems, *args)
        local = [pltpu.make_async_copy(own[a], outs[a].at[4 * x + 2 * y + c], local_sems.at[a]) for a in range(n)]
        sends = [mk(0, a, own[a], (x, y, c), (x, y, sib)) for a in range(n)]
        sends += [mk(1 + j, a, own[a], (x, y, c), (*chip, c)) for j, chip in enumerate(chips) for a in range(n)]
        recvs = [mk(0, a, own[a], (x, y, sib), (x, y, c)) for a in range(n)]
        recvs += [mk(1 + j, a, own[a], (*chip, c), (x, y, c)) for j, chip in enumerate(chips) for a in range(n)]
        return local, sends, recvs

    def within_chip(ins, outs, send_sems, recv_sems, local_sems):
        x, y, c, sib, chips = place()
        mk = lambda *args: copy(outs, send_sems, recv_sems, *args)
        slot = lambda a, px, py, pc: outs[a].at[4 * px + 2 * py + pc]
        sends = [mk(j, a, slot(a, *chip, c), (*chip, c), (x, y, sib)) for j, chip in enumerate(chips)
                 for a in range(n)]
        recvs = [mk(j, a, slot(a, *chip, c), (*chip, sib), (x, y, c)) for j, chip in enumerate(chips)
                 for a in range(n)]
        return [], sends, recvs

    sems = lambda k: [pltpu.SemaphoreType.DMA((k * n,)), pltpu.SemaphoreType.DMA((k * n,)),
                      pltpu.SemaphoreType.DMA((n,))]
    return between_chips, sems(4), within_chip, sems(3)


def _adam_math(g, w, m, v):
    m_new = ADAM_B1 * m + (1.0 - ADAM_B1) * g
    v_new = ADAM_B2 * v + (1.0 - ADAM_B2) * (g * g)
    m_hat = m_new / (1.0 - ADAM_B1 ** ADAM_STEP)
    v_hat = v_new / (1.0 - ADAM_B2 ** ADAM_STEP)
    return -ADAM_LR * (m_hat / (jnp.sqrt(v_hat) + ADAM_EPS) + ADAM_WD * w), m_new, v_new


def _adamw_big(parts, w, m, v, layer, prev, name, tr):
    depth, rows, cols = w.shape

    def body(p_ref, w_ref, m_ref, v_ref, *rest):
        g_out, d_out, m_out, v_out = rest[len(prev):]
        g = p_ref[0].astype(F32)
        for q in range(1, N_DEV):
            g = g + p_ref[q].astype(F32)
        d, m_new, v_new = _adam_math(g, w_ref[...], m_ref[...], v_ref[...])
        g_out[...] = g
        d_out[...] = d
        m_out[...] = m_new
        v_out[...] = v_new

    blk = pl.BlockSpec((None, tr, cols), lambda i: (layer, i, 0))
    shp = jax.ShapeDtypeStruct((depth, rows, cols), F32)
    return pl.pallas_call(
        body, grid=(rows // tr,),
        in_specs=[pl.BlockSpec((N_DEV, tr, cols), lambda i: (0, i, 0)), blk, blk, blk]
        + [pl.BlockSpec(memory_space=pl.ANY)] * len(prev),
        out_specs=(blk,) * 4, out_shape=(shp,) * 4,
        input_output_aliases={4 + j: j for j in range(len(prev))},
        name=name, compiler_params=_params("arbitrary"),
    )(parts, w, m, v, *prev)


_SMALL_TENSORS = (("conv_a_w", "conv_a_w", None), ("cc_dw_w", "cc_dw_w", None), ("cc_pw_w", "cc_pw_w", None),
                  ("sg_w", "sg_w", None), ("pool_w", "pool_w", None), ("sg_b", "sg_b", None)) \
    + tuple((n, "g256", k) for k, n in enumerate(G256_ROWS)) + tuple((n, "g1024", k) for k, n in enumerate(G1024_ROWS))
_SMALL_LANDINGS = ("conv_a_w", "cc_dw_w", "cc_pw_w", "sg_w", "pool_w", "sg_b", "g256", "g1024")
_TAPS_FIRST = ("conv_a_w", "cc_dw_w")


def _adamw_small(landings, wts, mom, var):
    names = [n for n, _, _ in _SMALL_TENSORS]
    n_land = DEPTH * len(_SMALL_LANDINGS)
    n_t = len(names)

    def body(*refs):
        land = [dict(zip(_SMALL_LANDINGS, refs[l * len(_SMALL_LANDINGS):(l + 1) * len(_SMALL_LANDINGS)]))
                for l in range(DEPTH)]
        w_refs = dict(zip(names, refs[n_land:n_land + n_t]))
        m_refs = dict(zip(names, refs[n_land + n_t:n_land + 2 * n_t]))
        v_refs = dict(zip(names, refs[n_land + 2 * n_t:n_land + 3 * n_t]))
        outs = refs[n_land + 3 * n_t:]
        out_refs = {n: outs[4 * k:4 * k + 4] for k, n in enumerate(names)}
        loss_ref = outs[4 * n_t]
        for name, key, row in _SMALL_TENSORS:
            for l in range(DEPTH):
                src = land[l][key]
                if row is not None:
                    part = lambda q: src[q, row:row + 1, :]
                    at = (slice(l, l + 1),)
                elif name == "sg_b":
                    part = lambda q: src[q, 0:N_SUB, :]
                    at = (l,)
                elif name in _TAPS_FIRST:
                    part = lambda q: src[q]
                    at = (slice(None), l)
                else:
                    part = lambda q: src[q]
                    at = (l,)
                g = part(0)
                for q in range(1, N_DEV):
                    g = g + part(q)
                d, m_new, v_new = _adam_math(g, w_refs[name][at], m_refs[name][at], v_refs[name][at])
                for ref, val in zip(out_refs[name], (g, d, m_new, v_new)):
                    ref[at] = val
        src = land[DEPTH - 1]["g1024"]
        loss = src[0, LOSS_ROW:LOSS_ROW + 1, 0:128]
        for q in range(1, N_DEV):
            loss = loss + src[q, LOSS_ROW:LOSS_ROW + 1, 0:128]
        loss_ref[...] = loss

    ins = [landings[l][k] for l in range(DEPTH) for k in _SMALL_LANDINGS] \
        + [src[n] for src in (wts, mom, var) for n in names]
    out_shape = tuple(jax.ShapeDtypeStruct(wts[n].shape, F32) for n in names for _ in range(4)) \
        + (jax.ShapeDtypeStruct((1, 128), F32),)
    res = pl.pallas_call(body, out_shape=out_shape, name="adamw_small", compiler_params=_params())(*ins)
    return {n: res[4 * k:4 * k + 4] for k, n in enumerate(names)}, res[4 * n_t]


_BIG = (("w_in", 64), ("w_out", 32), ("w_kv", 32))
_GRAD_ITEMS_EARLY = ("w_out", "w_kv", "cc_pw_w", "conv_a_w", "cc_dw_w")
_GRAD_ITEMS_REPL = ("g256", "sg_w", "sg_b", "pool_w", "g1024")


def _grad_items(grads, with_w_in):
    items = [(grads[n], True, None) for n in (("w_in",) if with_w_in else ()) + _GRAD_ITEMS_EARLY]
    return items + [(grads[n], False, None) for n in _GRAD_ITEMS_REPL]


def _landed(parts, with_w_in):
    names = (("w_in",) if with_w_in else ()) + _GRAD_ITEMS_EARLY + _GRAD_ITEMS_REPL
    return dict(zip(names, parts))


def _gathered_weights(wt_in8, w_kv8, w_out8, pw8, ca8, dw8):
    return dict(wt_in=wt_in8.reshape(D_IN, D_MODEL), w_kv=w_kv8.reshape(D_MODEL, 2 * D_G),
                w_out=w_out8.reshape(D_MIX, D_MODEL), pw=pw8.reshape(D_G, D_G), ca8=ca8, dw8=dw8)


def kernel(x, mem, w_in, conv_a_w, sg_ln_g, sg_ln_b, sg_w, sg_b, pool_w, pool_scale, cc_dw_w, cc_dw_b, cc_ln_g, cc_ln_b, cc_pw_w, w_kv, w_out, ln_g, ln_b, loss_target, m_w_in, m_conv_a_w, m_sg_ln_g, m_sg_ln_b, m_sg_w, m_sg_b, m_pool_w, m_pool_scale, m_cc_dw_w, m_cc_dw_b, m_cc_ln_g, m_cc_ln_b, m_cc_pw_w, m_w_kv, m_w_out, m_ln_g, m_ln_b, v_w_in, v_conv_a_w, v_sg_ln_g, v_sg_ln_b, v_sg_w, v_sg_b, v_pool_w, v_pool_scale, v_cc_dw_w, v_cc_dw_b, v_cc_ln_g, v_cc_ln_b, v_cc_pw_w, v_w_kv, v_w_out, v_ln_g, v_ln_b):
    names = ("w_in", "conv_a_w", "sg_ln_g", "sg_ln_b", "sg_w", "sg_b", "pool_w", "pool_scale", "cc_dw_w", "cc_dw_b",
             "cc_ln_g", "cc_ln_b", "cc_pw_w", "w_kv", "w_out", "ln_g", "ln_b")
    wts = dict(zip(names, (w_in, conv_a_w, sg_ln_g, sg_ln_b, sg_w, sg_b, pool_w, pool_scale, cc_dw_w, cc_dw_b,
                           cc_ln_g, cc_ln_b, cc_pw_w, w_kv, w_out, ln_g, ln_b)))
    mom = dict(zip(names, (m_w_in, m_conv_a_w, m_sg_ln_g, m_sg_ln_b, m_sg_w, m_sg_b, m_pool_w, m_pool_scale,
                           m_cc_dw_w, m_cc_dw_b, m_cc_ln_g, m_cc_ln_b, m_cc_pw_w, m_w_kv, m_w_out, m_ln_g, m_ln_b)))
    var = dict(zip(names, (v_w_in, v_conv_a_w, v_sg_ln_g, v_sg_ln_b, v_sg_w, v_sg_b, v_pool_w, v_pool_scale,
                           v_cc_dw_w, v_cc_dw_b, v_cc_ln_g, v_cc_ln_b, v_cc_pw_w, v_w_kv, v_w_out, v_ln_g, v_ln_b)))
    repl = wts
    xs, mems, tgt = x[0], mem[0], loss_target[0]
    turned = {"w_in": (0, 2, 1), "conv_a_w": (1, 0, 2), "cc_dw_w": (1, 0, 2)}
    wts, mom, var = [{n: (jnp.transpose(a, turned[n]) if n in turned else a) for n, a in src.items()}
                     for src in (wts, mom, var)]
    wb = {n: wts[n].astype(MM_DTYPE) for n in ("w_in", "w_kv", "w_out", "cc_pw_w")}

    wt8_0, wkv8_0 = _gather_two_level([(wb["w_in"], False, 0), (wb["w_kv"], False, 0)], "gather_weights_0a")
    rest_0 = _exchange_start([(wb["w_out"], False, 0), (wb["cc_pw_w"], False, 0), (wts["conv_a_w"], False, None),
                              (wts["cc_dw_w"], False, None)], "gather_weights_0b_start")
    km_vm0 = _kv_project(mems, wkv8_0.reshape(D_MODEL, 2 * D_G))
    proj0 = _proj_matmul(xs, wt8_0.reshape(D_IN, D_MODEL), (rest_0["token"],))
    wo8_0, pw8_0, ca8, dw8 = _split_wait(rest_0, proj0, "gather_weights_0b_wait")
    gw0 = _gathered_weights(wt8_0, wkv8_0, wo8_0, pw8_0, ca8, dw8)
    items_1 = [(wb[n], False, 1) for n in ("w_in", "w_kv", "w_out", "cc_pw_w")]
    between_chips, sems_a, within_chip, sems_b = _two_level_plans([1] * len(items_1))
    chips_1 = _split_start([a for a, _, _ in items_1], _empty_landings(items_1), between_chips, sems_a,
                           "gather_weights_1a_start")
    x1, saved0 = _layer_fwd(xs, mems, 0, repl, gw0, proj0, km_vm0, (chips_1["token"],))
    core_1 = _split_start([], _split_wait(chips_1, x1, "gather_weights_1a_wait"), within_chip, sems_b,
                          "gather_weights_1b_start")
    gw1 = _gathered_weights(*_split_wait(core_1, core_1["token"], "gather_weights_1b_wait"), ca8, dw8)
    _, saved1 = _layer_fwd(x1, mems, 1, repl, gw1)

    dz1, dproj1, g1 = _layer_bwd_a(None, tgt, mems, 1, repl, gw1, saved1)
    shards = lambda g: g.reshape(N_DEV, W_IN_COLS, D_MODEL)
    up, g_wt_in_1 = _in_bwd(saved1[0], dproj1, gw1["wt_in"], dz1)
    g1["w_in"] = shards(g_wt_in_1)
    grads_1 = _exchange_start(_grad_items(g1, True), "exchange_grads_1_start")
    dz0, dproj0, g0 = _layer_bwd_a(up, None, mems, 0, repl, gw0, saved0, (grads_1["token"],))
    early_0 = _exchange_start(_grad_items(g0, False), "exchange_grads_0a_start")
    g_wt_in_0 = _dw_in_matmul(saved0[0], dproj0, (early_0["token"],))
    late_0 = _exchange_start([(shards(g_wt_in_0), True, None)], "exchange_grads_0b_start")
    grad_x = _dx_matmul(dproj0, gw0["wt_in"], dz0, (late_0["token"],))

    landed = [None, _landed(_split_wait(grads_1, grad_x, "exchange_grads_1_wait"), True)]
    big = {}
    for n, tr in _BIG:
        big[n] = _adamw_big(landed[1][n], wts[n], mom[n], var[n], 1, (), "adamw_" + n + "_1", tr)
    landed[0] = _landed(_split_wait(early_0, big["w_kv"][0], "exchange_grads_0a_wait"), False)
    for n, tr in _BIG[1:]:
        big[n] = _adamw_big(landed[0][n], wts[n], mom[n], var[n], 0, big[n], "adamw_" + n + "_0", tr)
    (landed[0]["w_in"],) = _split_wait(late_0, big["w_kv"][0], "exchange_grads_0b_wait")
    big["w_in"] = _adamw_big(landed[0]["w_in"], wts["w_in"], mom["w_in"], var["w_in"], 0, big["w_in"],
                             "adamw_w_in_0", _BIG[0][1])
    small, loss = _adamw_small(landed, wts, mom, var)

    res = {**small, **big}
    res = {n: ([jnp.transpose(a, turned[n]) for a in r] if n in turned else r) for n, r in res.items()}
    return (loss[0, 0], grad_x[None], *[res[n][0] for n in names], *[res[n][1] for n in names],
            *[res[n][2] for n in names], *[res[n][3] for n in names])
```

```python
import math

import jax
import jax.numpy as jnp
from jax import lax
from jax.experimental import pallas as pl
from jax.experimental.pallas import tpu as pltpu

F32 = jnp.float32
MM_DTYPE = jnp.bfloat16
GRAD_DTYPE = jnp.bfloat16

D_MODEL = 1024
DEPTH = 2
D_G = 256
N_GROUPS = 5
D_MIX = N_GROUPS * D_G
N_SUB = 4
HEAD_DIM = D_G // N_SUB
CONV_A = 3
CONV_D = 31
CHUNK = 128
POOL_WINDOWS = (2, 4, 8, 16)
MEM_LEN = 256
LN_EPS = 1e-5
ALPHA = (2.0 * DEPTH) ** 0.25
D_IN = 9 * D_G + D_MIX
ATT_SCALE = 1.0 / math.sqrt(HEAD_DIM)

ADAM_LR = 0.001
ADAM_B1 = 0.9
ADAM_B2 = 0.999
ADAM_EPS = 1e-08
ADAM_WD = 0.01
ADAM_STEP = 10

N_DEV = 8
W_IN_COLS = D_IN // N_DEV
CONV_CH = D_G // N_DEV
HALO = 32
TILE = 256
VMEM_LIMIT = 56 * 1024 * 1024

C_XA, C_BA, C_CA, C_U, C_V, C_XC, C_DA, C_DG, C_Q = range(9)
C_GATE = 9 * D_G

G256_ROWS = ("sg_ln_g", "sg_ln_b", "pool_scale", "cc_dw_b", "cc_ln_g", "cc_ln_b")
G1024_ROWS = ("ln_g", "ln_b")
LOSS_ROW = 2


def _mm(a, b):
    return jnp.dot(a.astype(MM_DTYPE), b.astype(MM_DTYPE), preferred_element_type=F32)


def _mm_nt(a, b):
    return lax.dot_general(a.astype(MM_DTYPE), b.astype(MM_DTYPE), (((1,), (1,)), ((), ())),
                           preferred_element_type=F32)


def _mm_tn(a, b):
    return lax.dot_general(a.astype(MM_DTYPE), b.astype(MM_DTYPE), (((0,), (0,)), ((), ())),
                           preferred_element_type=F32)


def _sigmoid(x):
    return 0.5 * jnp.tanh(0.5 * x) + 0.5


_GELU_C = math.sqrt(2.0 / math.pi)
_GELU_A = 0.044715


def _gelu(x):
    th = jnp.tanh(_GELU_C * (x + _GELU_A * (x * x * x)))
    return 0.5 * x * (1.0 + th), th


def _dgelu(x, th):
    return 0.5 * (1.0 + th) + 0.5 * x * (1.0 - th * th) * (_GELU_C * (1.0 + 3.0 * _GELU_A * (x * x)))


def _ln_fwd(x, g, b):
    mu = jnp.mean(x, axis=-1, keepdims=True)
    xc = x - mu
    var = jnp.mean(xc * xc, axis=-1, keepdims=True)
    rstd = lax.rsqrt(var + LN_EPS)
    xhat = xc * rstd
    return xhat * g + b, xhat, rstd


def _ln_bwd(dy, xhat, rstd, g):
    dxhat = dy * g
    m1 = jnp.mean(dxhat, axis=-1, keepdims=True)
    m2 = jnp.mean(dxhat * xhat, axis=-1, keepdims=True)
    return rstd * (dxhat - m1 - xhat * m2)


def _rowsum(x):
    return jnp.sum(x, axis=0, keepdims=True)


def _col(ref, k):
    return ref[:, k * D_G:(k + 1) * D_G]


def _head_of_lane(shape):
    return jnp.right_shift(lax.broadcasted_iota(jnp.int32, shape, len(shape) - 1), HEAD_DIM.bit_length() - 1)


def _pool_select(lane_grp, s2, s4, s8, s16):
    return jnp.where(lane_grp == 0, s2, jnp.where(lane_grp == 1, s4, jnp.where(lane_grp == 2, s8, s16)))


def _row_view(ref, layer):
    return ref.at[pl.ds(layer, 1)]


def _make_residues(ext_ref, res_ref):
    rows = res_ref.shape[1]
    for r in range(1, 8):
        res_ref[r - 1] = ext_ref[pl.ds(r, rows), :]


def _rows_at(ext_ref, res_ref, off, tile):
    a, r = divmod(off, 8)
    if r == 0:
        return ext_ref[pl.ds(off, tile), :]
    return res_ref[r - 1, pl.ds(8 * a, tile), :]


def _residue_scratch(tile):
    return pltpu.VMEM((7, HALO + tile - 8, D_G), F32)


def _branch_forward(p_ref, ph_ref, first, row0, km_ref, vm_ref, w, ext_a, ext_c, ext_d, res_c, res_d, tile,
                    kept_ref=None):
    r = {}
    xa, ba, ca = _col(p_ref, C_XA), _col(p_ref, C_BA), _col(p_ref, C_CA)
    g_a = ca * xa
    ext_a[0:HALO] = jnp.where(first, 0.0, _col(ph_ref, C_CA) * _col(ph_ref, C_XA))
    ext_a[HALO:HALO + tile] = g_a
    conv_a = w["conv_a"][0:1, :] * ext_a[pl.ds(HALO - 2, tile), :]
    for k in range(1, CONV_A):
        conv_a = conv_a + w["conv_a"][k:k + 1, :] * ext_a[pl.ds(HALO - 2 + k, tile), :]
    r.update(xa=xa, ba=ba, ca=ca, g_a=g_a, conv_a=conv_a)
    ya = ba * conv_a

    lane_grp = _head_of_lane((tile, D_G))
    trow = row0 + lax.broadcasted_iota(jnp.int32, (tile, D_G), 0)
    win = _pool_select(lane_grp, 2, 4, 8, 16)
    inv_cnt = 1.0 / jnp.minimum(trow + 1, win).astype(F32)
    if kept_ref is None:
        xc = _col(p_ref, C_XC)
        ext_c[0:HALO] = jnp.where(first, 0.0, _col(ph_ref, C_XC))
        ext_c[HALO:HALO + tile] = xc
        _make_residues(ext_c, res_c)
        acc = xc
        sums = {}
        for k in range(1, POOL_WINDOWS[-1]):
            acc = acc + _rows_at(ext_c, res_c, HALO - k, tile)
            if k + 1 in POOL_WINDOWS:
                sums[k + 1] = acc
        ypre = _pool_select(lane_grp, sums[2], sums[4], sums[8], sums[16]) * inv_cnt - xc
    else:
        ypre = kept_ref[:, D_G:2 * D_G]
    pool_mm = _mm(ypre, w["pool_wbd"][...])
    yc = pool_mm * w["pool_scale"][...]
    r.update(lane_grp=lane_grp, inv_cnt=inv_cnt, ypre=ypre, pool_mm=pool_mm)

    da, dg = _col(p_ref, C_DA), _col(p_ref, C_DG)
    sig_dg = _sigmoid(dg)
    hd = da * sig_dg
    if kept_ref is None:
        ext_d[0:HALO] = jnp.where(first, 0.0, _col(ph_ref, C_DA) * _sigmoid(_col(ph_ref, C_DG)))
        ext_d[HALO:HALO + tile] = hd
        _make_residues(ext_d, res_d)
        conv_d = w["cc_dw_b"][...] + w["cc_dw_w"][0:1, :] * _rows_at(ext_d, res_d, HALO - (CONV_D - 1), tile)
        for j in range(1, CONV_D):
            conv_d = conv_d + w["cc_dw_w"][j:j + 1, :] * _rows_at(ext_d, res_d, HALO - (CONV_D - 1) + j, tile)
    else:
        conv_d = kept_ref[:, 0:D_G]
    r["kept"] = (conv_d, ypre)
    ln_d, xhat_d, rstd_d = _ln_fwd(conv_d, w["cc_ln_g"][...], w["cc_ln_b"][...])
    sig_ln = _sigmoid(ln_d)
    act_d = ln_d * sig_ln
    yd = _mm(act_d, w["cc_pw_w"][...])
    r.update(da=da, sig_dg=sig_dg, hd=hd, ln_d=ln_d, xhat_d=xhat_d, rstd_d=rstd_d, sig_ln=sig_ln, act_d=act_d)

    u, v = _col(p_ref, C_U), _col(p_ref, C_V)
    ug, th_u = _gelu(u)
    vg, th_v = _gelu(v)
    vn, xhat_v, rstd_v = _ln_fwd(vg, w["sg_ln_g"][...], w["sg_ln_b"][...])
    tri = (lax.broadcasted_iota(jnp.int32, (CHUNK, CHUNK), 0)
           >= lax.broadcasted_iota(jnp.int32, (CHUNK, CHUNK), 1))
    wm = [jnp.where(tri, w["sg_w"][h], 0.0).astype(MM_DTYPE) for h in range(N_SUB)]
    lo = lax.broadcasted_iota(jnp.int32, (CHUNK, 2 * HEAD_DIM), 1) < HEAD_DIM
    chunks = []
    for c in range(tile // CHUNK):
        halves = []
        for hf in range(2):
            vh = vn[c * CHUNK:(c + 1) * CHUNK, hf * 128:(hf + 1) * 128]
            halves.append(_mm(wm[2 * hf], jnp.where(lo, vh, 0.0)) + _mm(wm[2 * hf + 1], jnp.where(lo, 0.0, vh)))
        chunks.append(jnp.concatenate(halves, axis=1) + w["sg_bias"][...])
    mixed = jnp.concatenate(chunks, axis=0)
    yb = ug * mixed
    r.update(u=u, v=v, ug=ug, th_u=th_u, th_v=th_v, vn=vn, xhat_v=xhat_v, rstd_v=rstd_v, wm=wm, lo=lo,
             mixed=mixed, tri=tri)

    q = _col(p_ref, C_Q)
    ye = jnp.zeros((tile, D_G), F32)
    probs = []
    for h in range(N_SUB):
        s = _mm_nt(q, km_ref[h]) * ATT_SCALE
        e = jnp.exp(s - jnp.max(s, axis=-1, keepdims=True))
        p = e * (1.0 / jnp.sum(e, axis=-1, keepdims=True))
        probs.append(p)
        ye = ye + _mm(p, vm_ref[h])
    r.update(q=q, probs=probs)

    gate = p_ref[:, C_GATE:C_GATE + D_MIX]
    sig_gate = _sigmoid(gate)
    concat = jnp.concatenate([ya, yb, yc, yd, ye], axis=1)
    r.update(gate=gate, sig_gate=sig_gate, concat=concat)
    return r


_BRANCH_REPL = ("sg_ln_g", "sg_ln_b", "sg_w", "sg_b", "pool_w", "pool_scale", "cc_dw_b", "cc_ln_g", "cc_ln_b")
_BRANCH_W_SCRATCH = (("conv_a", (CONV_A, D_G)), ("cc_dw_w", (CONV_D, D_G)), ("sg_bias", (CHUNK, D_G)),
                     ("pool_wbd", (D_G, D_G)), ("sgb8", (8, CHUNK)))


def _branch_weights(layer, nat, pw_ref, ca_ref, dw_ref, scr, init):
    @pl.when(init)
    def _():
        for p in range(N_DEV):
            scr["conv_a"][:, p * CONV_CH:(p + 1) * CONV_CH] = ca_ref[p, :, layer, :]
            scr["cc_dw_w"][:, p * CONV_CH:(p + 1) * CONV_CH] = dw_ref[p, :, layer, :]
        scr["sgb8"][...] = jnp.zeros((8, CHUNK), F32)
        scr["sgb8"][0:N_SUB] = nat["sg_b"][layer]
        sgb_t = scr["sgb8"][...].T
        head = _head_of_lane((CHUNK, D_G))
        bias = jnp.zeros((CHUNK, D_G), F32)
        for h in range(N_SUB):
            bias = jnp.where(head == h, sgb_t[:, h:h + 1], bias)
        scr["sg_bias"][...] = bias
        scr["pool_wbd"][...] = jnp.zeros((D_G, D_G), F32)
        for gi in range(N_SUB):
            sl = slice(gi * HEAD_DIM, (gi + 1) * HEAD_DIM)
            scr["pool_wbd"][sl, sl] = nat["pool_w"][layer, gi]

    w = {n: _row_view(nat[n], layer) for n in ("sg_ln_g", "sg_ln_b", "pool_scale", "cc_dw_b", "cc_ln_g", "cc_ln_b")}
    w.update(conv_a=scr["conv_a"], cc_dw_w=scr["cc_dw_w"], sg_bias=scr["sg_bias"], pool_wbd=scr["pool_wbd"],
             sg_w=nat["sg_w"].at[layer], cc_pw_w=pw_ref)
    return w


def _full_spec(a):
    nd = a.ndim
    return pl.BlockSpec(a.shape, lambda *_, _nd=nd: (0,) * _nd)


def _tie_specs(ties):
    return [pl.BlockSpec((8, 128), lambda *_: (0, 0)) for _ in ties]


def _params(*sem):
    return pltpu.CompilerParams(dimension_semantics=sem or None, vmem_limit_bytes=VMEM_LIMIT)


def _proj_matmul(x, wt_in, ties=(), tm=256):
    s, k = x.shape

    def body(x_ref, w_ref, *rest):
        o_ref = rest[len(ties)]
        o_ref[...] = _mm_nt(x_ref[...], w_ref[...])

    return pl.pallas_call(
        body, grid=(s // tm,),
        in_specs=[pl.BlockSpec((tm, k), lambda i: (i, 0)), _full_spec(wt_in)] + _tie_specs(ties),
        out_specs=pl.BlockSpec((tm, D_IN), lambda i: (i, 0)),
        out_shape=jax.ShapeDtypeStruct((s, D_IN), F32), name="proj_mm", compiler_params=_params("arbitrary"),
    )(x, wt_in, *ties)


def _kv_project(mem, w_kv):
    def body(mem_ref, w_ref, km_ref, vm_ref):
        kv = _mm(mem_ref[...], w_ref[...])
        k, v = kv[:, :D_G], kv[:, D_G:]
        grp = _head_of_lane((MEM_LEN, D_G))
        for h in range(N_SUB):
            km_ref[h] = jnp.where(grp == h, k, 0.0).astype(km_ref.dtype)
            vm_ref[h] = jnp.where(grp == h, v, 0.0).astype(vm_ref.dtype)

    shp = jax.ShapeDtypeStruct((N_SUB, MEM_LEN, D_G), MM_DTYPE)
    return pl.pallas_call(body, out_shape=(shp, shp), name="kv_project", compiler_params=_params())(mem, w_kv)


def _layer_fwd_fused(x, wt_in, proj, km, vm, layer, repl, pw, ca8, dw8, w_out, ties=(), tile=TILE):
    s = x.shape[0]
    hb = tile // HALO
    nat_arrays = [repl[n] for n in _BRANCH_REPL]
    n_nat, nt = len(nat_arrays), len(ties)
    given = proj is not None

    def body(x_ref, *rest):
        if given:
            p_ref, ph_ref = rest[:2]
            rest = rest[2:]
        else:
            wt_ref = rest[0]
            rest = rest[1:]
        km_ref, vm_ref = rest[:2]
        nat = dict(zip(_BRANCH_REPL, rest[2:2 + n_nat]))
        pw_ref, ca_ref, dw_ref, wo_ref, g_ref, b_ref = rest[2 + n_nat:8 + n_nat]
        rest = rest[8 + n_nat + nt:]
        if not given:
            p_ref, rest = rest[0], rest[1:]
        h_ref, z_ref, xn_ref, cd_ref = rest[:4]
        rest = rest[4:]
        if not given:
            ph_ref, rest = rest[0], rest[1:]
        ext_a, ext_c, ext_d, res_c, res_d = rest[:5]
        scr = dict(zip([n for n, _ in _BRANCH_W_SCRATCH], rest[5:]))
        i = pl.program_id(0)
        xt = x_ref[...]
        if not given:
            @pl.when(i == 0)
            def _():
                ph_ref[...] = jnp.zeros_like(ph_ref)

            p_ref[...] = _mm_nt(xt, wt_ref[...])
        w = _branch_weights(layer, nat, pw_ref, ca_ref, dw_ref, scr, i == 0)
        r = _branch_forward(p_ref, ph_ref, i == 0, i * tile, km_ref, vm_ref, w, ext_a, ext_c, ext_d, res_c, res_d,
                            tile)
        if not given:
            ph_ref[...] = p_ref[tile - HALO:tile, :]
        h = (r["concat"] * (r["gate"] * r["sig_gate"])).astype(h_ref.dtype)
        h_ref[...] = h
        cd_ref[:, 0:D_G], cd_ref[:, D_G:2 * D_G] = r["kept"]
        z = ALPHA * xt + _mm(h, wo_ref[...])
        z_ref[...] = z
        xn_ref[...] = _ln_fwd(z, _row_view(g_ref, layer)[...], _row_view(b_ref, layer)[...])[0]

    row = lambda i: (i, 0)
    consts = [km, vm] + nat_arrays + [pw, ca8, dw8, w_out, repl["ln_g"], repl["ln_b"]]
    act = jax.ShapeDtypeStruct((s, D_MODEL), F32)
    act_spec = pl.BlockSpec((tile, D_MODEL), row)
    if given:
        lead = [proj, proj]
        lead_specs = [pl.BlockSpec((tile, D_IN), row),
                      pl.BlockSpec((HALO, D_IN), lambda i: (jnp.maximum(i * hb - 1, 0), 0))]
        out_specs, out_shape, scratch = (), (), []
    else:
        lead = [wt_in]
        lead_specs = [_full_spec(wt_in)]
        out_specs = (pl.BlockSpec((tile, D_IN), row),)
        out_shape = (jax.ShapeDtypeStruct((s, D_IN), F32),)
        scratch = [pltpu.VMEM((HALO, D_IN), F32)]
    res = pl.pallas_call(
        body, grid=(s // tile,),
        in_specs=[act_spec] + lead_specs + [_full_spec(a) for a in consts] + _tie_specs(ties),
        out_specs=out_specs + (pl.BlockSpec((tile, D_MIX), row), act_spec, act_spec,
                               pl.BlockSpec((tile, 2 * D_G), row)),
        out_shape=out_shape + (jax.ShapeDtypeStruct((s, D_MIX), MM_DTYPE), act, act,
                               jax.ShapeDtypeStruct((s, 2 * D_G), F32)),
        scratch_shapes=scratch + [pltpu.VMEM((HALO + tile, D_G), F32)] * 3
        + [_residue_scratch(tile)] * 2 + [pltpu.VMEM(shape, F32) for _, shape in _BRANCH_W_SCRATCH],
        name="layer_fwd_given_proj" if given else "layer_fwd", compiler_params=_params("arbitrary"),
    )(x, *lead, *consts, *ties)
    return ((proj,) + tuple(res)) if given else tuple(res)


def _ln_out_bwd(up, target, z, ln_g, ln_b, layer, w_out, h, ties=(), tm=512):
    s = z.shape[0]
    from_loss = target is not None
    other = target if from_loss else up
    nt = len(ties)

    def body(o_ref, z_ref, g_ref, b_ref, w_ref, h_ref, *rest):
        dz_ref, dh_ref, gw_ref, slab_ref, gw_acc, lacc = rest[nt:]
        i = pl.program_id(0)

        @pl.when(i == 0)
        def _():
            gw_acc[...] = jnp.zeros_like(gw_acc)
            slab_ref[...] = jnp.zeros_like(slab_ref)
            lacc[...] = jnp.zeros_like(lacc)

        g = _row_view(g_ref, layer)[...]
        xn, xhat, rstd = _ln_fwd(z_ref[...], g, _row_view(b_ref, layer)[...])
        if from_loss:
            err = xn - o_ref[...]
            lacc[...] += _rowsum(err * err)
            dxn = err * (1.0 / D_MODEL)
        else:
            dxn = o_ref[...]
        slab_ref[0:1, :] += _rowsum(dxn * xhat)
        slab_ref[1:2, :] += _rowsum(dxn)
        dz = _ln_bwd(dxn, xhat, rstd, g)
        dz_ref[...] = dz
        dh_ref[...] = _mm_nt(dz, w_ref[...])
        gw_acc[...] += _mm_tn(h_ref[...], dz)

        @pl.when(i == pl.num_programs(0) - 1)
        def _():
            gw_ref[...] = gw_acc[...].astype(gw_ref.dtype)
            if from_loss:
                total = jnp.sum(lacc[...], axis=-1, keepdims=True) * (0.5 / D_MODEL)
                slab_ref[LOSS_ROW:LOSS_ROW + 1, :] = jnp.broadcast_to(total, (1, D_MODEL))

    row = lambda i: (i, 0)
    fixed = lambda i: (0, 0)
    return pl.pallas_call(
        body, grid=(s // tm,),
        in_specs=[pl.BlockSpec((tm, D_MODEL), row), pl.BlockSpec((tm, D_MODEL), row), _full_spec(ln_g),
                  _full_spec(ln_b), _full_spec(w_out), pl.BlockSpec((tm, D_MIX), row)] + _tie_specs(ties),
        out_specs=(pl.BlockSpec((tm, D_MODEL), row), pl.BlockSpec((tm, D_MIX), row),
                   pl.BlockSpec((D_MIX, D_MODEL), fixed), pl.BlockSpec((8, D_MODEL), fixed)),
        out_shape=(jax.ShapeDtypeStruct((s, D_MODEL), F32), jax.ShapeDtypeStruct((s, D_MIX), F32),
                   jax.ShapeDtypeStruct((D_MIX, D_MODEL), GRAD_DTYPE), jax.ShapeDtypeStruct((8, D_MODEL), F32)),
        scratch_shapes=[pltpu.VMEM((D_MIX, D_MODEL), F32), pltpu.VMEM((1, D_MODEL), F32)],
        name="ln_out_bwd_loss" if from_loss else "ln_out_bwd", compiler_params=_params("arbitrary"),
    )(other, z, ln_g, ln_b, w_out, h, *ties)


_BRANCH_GRADS = (("g256", (8, D_G)), ("sg_w", (N_SUB, CHUNK, CHUNK)), ("sg_b", (8, CHUNK)),
                 ("pool_w", (N_SUB, HEAD_DIM, HEAD_DIM)), ("conv_a_w", (N_DEV, CONV_A, CONV_CH)),
                 ("cc_dw_w", (N_DEV, CONV_D, CONV_CH)), ("cc_pw_w", (D_G, D_G)),
                 ("dk", (N_SUB, MEM_LEN, D_G)), ("dv", (N_SUB, MEM_LEN, D_G)))
_BRANCH_ACC = (("conv_a", (CONV_A, D_G)), ("cc_dw_w", (CONV_D, D_G)), ("pool_wbd", (D_G, D_G)),
               ("sg_bias", (CHUNK, D_G)))


def _branch_bwd(proj, conv_d, dh, km, vm, layer, repl, pw, ca8, dw8, tile=TILE):
    s = proj.shape[0]
    nt = s // tile
    hb = tile // HALO
    nat_arrays = [repl[n] for n in _BRANCH_REPL]
    n_nat, n_grads, n_acc = len(nat_arrays), len(_BRANCH_GRADS), len(_BRANCH_ACC)
    row_of = {n: k for k, n in enumerate(G256_ROWS)}

    def body(p_ref, ph_ref, cd_ref, dh_ref, km_ref, vm_ref, *rest):
        nat = dict(zip(_BRANCH_REPL, rest[:n_nat]))
        pw_ref, ca_ref, dw_ref = rest[n_nat:n_nat + 3]
        rest = rest[n_nat + 3:]
        dp_ref = rest[0]
        g = dict(zip([n for n, _ in _BRANCH_GRADS], rest[1:1 + n_grads]))
        rest = rest[1 + n_grads:]
        ext_a, rev_a, rev_c, rev_d, res_rc, res_rd = rest[:6]
        acc = dict(zip([n for n, _ in _BRANCH_ACC], rest[6:6 + n_acc]))
        scr = dict(zip([n for n, _ in _BRANCH_W_SCRATCH], rest[6 + n_acc:]))
        i = pl.program_id(0)
        t = nt - 1 - i

        @pl.when(i == 0)
        def _():
            for ref in list(g.values()) + list(acc.values()) + [rev_a, rev_c, rev_d]:
                ref[...] = jnp.zeros_like(ref)

        w = _branch_weights(layer, nat, pw_ref, ca_ref, dw_ref, scr, i == 0)
        r = _branch_forward(p_ref, ph_ref, t == 0, t * tile, km_ref, vm_ref, w, ext_a, None, None, None, None,
                            tile, cd_ref)

        def put(k, val, width=D_G):
            dp_ref[:, k:k + width] = val.astype(dp_ref.dtype)

        def add_row(name, val):
            k = row_of[name]
            g["g256"][k:k + 1, :] += val

        def push_rev(rev, val):
            head = rev[0:HALO]
            rev[tile:tile + HALO] = head
            rev[0:tile] = val

        dh_all = dh_ref[...]
        gate, sig_gate, concat = r["gate"], r["sig_gate"], r["concat"]
        put(C_GATE, dh_all * concat * (sig_gate * (1.0 + gate * (1.0 - sig_gate))), D_MIX)
        dconcat = dh_all * (gate * sig_gate)
        dya, dyb, dyc, dyd, dye = [dconcat[:, k * D_G:(k + 1) * D_G] for k in range(N_GROUPS)]

        put(C_BA * D_G, dya * r["conv_a"])
        dconv_a = dya * r["ba"]
        push_rev(rev_a, dconv_a)
        dga = jnp.zeros((tile, D_G), F32)
        for k in range(CONV_A):
            ahead = rev_a[pl.ds(CONV_A - 1 - k, tile), :]
            dga = dga + w["conv_a"][k:k + 1, :] * ahead
            acc["conv_a"][k:k + 1, :] += _rowsum(r["g_a"] * ahead)
        put(C_CA * D_G, dga * r["xa"])
        put(C_XA * D_G, dga * r["ca"])

        add_row("pool_scale", _rowsum(dyc * r["pool_mm"]))
        dmm = dyc * w["pool_scale"][...]
        acc["pool_wbd"][...] += _mm_tn(r["ypre"], dmm)
        dypre = _mm_nt(dmm, w["pool_wbd"][...])
        dws = dypre * r["inv_cnt"]
        push_rev(rev_c, dws)
        _make_residues(rev_c, res_rc)
        run = dws
        sums = {}
        for k in range(1, POOL_WINDOWS[-1]):
            run = run + _rows_at(rev_c, res_rc, k, tile)
            if k + 1 in POOL_WINDOWS:
                sums[k + 1] = run
        put(C_XC * D_G, _pool_select(r["lane_grp"], sums[2], sums[4], sums[8], sums[16]) - dypre)

        g["cc_pw_w"][...] += _mm_tn(r["act_d"], dyd)
        dact = _mm_nt(dyd, w["cc_pw_w"][...])
        sig_ln, ln_d = r["sig_ln"], r["ln_d"]
        dln = dact * (sig_ln * (1.0 + ln_d * (1.0 - sig_ln)))
        add_row("cc_ln_g", _rowsum(dln * r["xhat_d"]))
        add_row("cc_ln_b", _rowsum(dln))
        dconv_d = _ln_bwd(dln, r["xhat_d"], r["rstd_d"], w["cc_ln_g"][...])
        add_row("cc_dw_b", _rowsum(dconv_d))
        push_rev(rev_d, dconv_d)
        _make_residues(rev_d, res_rd)
        dhd = jnp.zeros((tile, D_G), F32)
        for j in range(CONV_D):
            ahead = _rows_at(rev_d, res_rd, CONV_D - 1 - j, tile)
            dhd = dhd + w["cc_dw_w"][j:j + 1, :] * ahead
            acc["cc_dw_w"][j:j + 1, :] += _rowsum(r["hd"] * ahead)
        sig_dg = r["sig_dg"]
        put(C_DA * D_G, dhd * sig_dg)
        put(C_DG * D_G, dhd * r["da"] * sig_dg * (1.0 - sig_dg))

        dug = dyb * r["mixed"]
        dmixed = dyb * r["ug"]
        wm, lo, vn = r["wm"], r["lo"], r["vn"]
        dvn_chunks = []
        for c in range(tile // CHUNK):
            rows = slice(c * CHUNK, (c + 1) * CHUNK)
            acc["sg_bias"][...] += dmixed[rows, :]
            halves = []
            for hf in range(2):
                cols = slice(hf * 128, (hf + 1) * 128)
                dm = dmixed[rows, cols]
                dm_a, dm_b = jnp.where(lo, dm, 0.0), jnp.where(lo, 0.0, dm)
                vh = vn[rows, cols]
                g["sg_w"][2 * hf] += _mm_nt(dm_a, vh)
                g["sg_w"][2 * hf + 1] += _mm_nt(dm_b, vh)
                halves.append(_mm_tn(wm[2 * hf], dm_a) + _mm_tn(wm[2 * hf + 1], dm_b))
            dvn_chunks.append(jnp.concatenate(halves, axis=1))
        dvn = jnp.concatenate(dvn_chunks, axis=0)
        add_row("sg_ln_g", _rowsum(dvn * r["xhat_v"]))
        add_row("sg_ln_b", _rowsum(dvn))
        dvg = _ln_bwd(dvn, r["xhat_v"], r["rstd_v"], w["sg_ln_g"][...])
        put(C_V * D_G, dvg * _dgelu(r["v"], r["th_v"]))
        put(C_U * D_G, dug * _dgelu(r["u"], r["th_u"]))

        q = r["q"]
        dq = jnp.zeros((tile, D_G), F32)
        for h in range(N_SUB):
            p = r["probs"][h]
            dp = _mm_nt(dye, vm_ref[h])
            g["dv"][h] += _mm_tn(p, dye)
            ds = p * (dp - jnp.sum(dp * p, axis=-1, keepdims=True)) * ATT_SCALE
            dq = dq + _mm(ds, km_ref[h])
            g["dk"][h] += _mm_tn(ds, q)
        put(C_Q * D_G, dq)

        @pl.when(i == nt - 1)
        def _():
            for h in range(N_SUB):
                g["sg_w"][h] = jnp.where(r["tri"], g["sg_w"][h], 0.0)
            lane_head = _head_of_lane((CHUNK, D_G))
            col_of = lax.broadcasted_iota(jnp.int32, (CHUNK, 8), 1)
            ba = acc["sg_bias"][...]
            sgb_t = jnp.zeros((CHUNK, 8), F32)
            for h in range(N_SUB):
                col = jnp.sum(jnp.where(lane_head == h, ba, 0.0), axis=-1, keepdims=True)
                sgb_t = jnp.where(col_of == h, col, sgb_t)
            g["sg_b"][...] = sgb_t.T
            wbd = acc["pool_wbd"][...]
            for gi in range(N_SUB):
                sl = slice(gi * HEAD_DIM, (gi + 1) * HEAD_DIM)
                g["pool_w"][gi] = wbd[sl, sl]
            ca, dw = acc["conv_a"][...], acc["cc_dw_w"][...]
            for p in range(N_DEV):
                g["conv_a_w"][p] = ca[:, p * CONV_CH:(p + 1) * CONV_CH]
                g["cc_dw_w"][p] = dw[:, p * CONV_CH:(p + 1) * CONV_CH]

    rev = lambda i: (nt - 1 - i, 0)
    grad_specs = tuple(pl.BlockSpec(shape, lambda i, _nd=len(shape): (0,) * _nd) for _, shape in _BRANCH_GRADS)
    grad_shapes = tuple(jax.ShapeDtypeStruct(shape, F32) for _, shape in _BRANCH_GRADS)
    outs = pl.pallas_call(
        body, grid=(nt,),
        in_specs=[pl.BlockSpec((tile, D_IN), rev),
                  pl.BlockSpec((HALO, D_IN), lambda i: (jnp.maximum((nt - 1 - i) * hb - 1, 0), 0)),
                  pl.BlockSpec((tile, 2 * D_G), rev), pl.BlockSpec((tile, D_MIX), rev), _full_spec(km),
                  _full_spec(vm)]
        + [_full_spec(a) for a in nat_arrays + [pw, ca8, dw8]],
        out_specs=(pl.BlockSpec((tile, D_IN), rev),) + grad_specs,
        out_shape=(jax.ShapeDtypeStruct((s, D_IN), MM_DTYPE),) + grad_shapes,
        scratch_shapes=[pltpu.VMEM((HALO + tile, D_G), F32)] * 4 + [_residue_scratch(tile)] * 2
        + [pltpu.VMEM(shape, F32) for _, shape in _BRANCH_ACC + _BRANCH_W_SCRATCH],
        name="branch_bwd", compiler_params=_params("arbitrary"),
    )(proj, proj, conv_d, dh, km, vm, *nat_arrays, pw, ca8, dw8)
    return outs[0], dict(zip([n for n, _ in _BRANCH_GRADS], outs[1:]))


def _dx_matmul(dproj, wt_in, dz, ties=(), tm=256):
    s = dproj.shape[0]

    def body(dp_ref, w_ref, dz_ref, *rest):
        o_ref = rest[len(ties)]
        o_ref[...] = _mm(dp_ref[...], w_ref[...]) + ALPHA * dz_ref[...]

    row = lambda i: (i, 0)
    return pl.pallas_call(
        body, grid=(s // tm,),
        in_specs=[pl.BlockSpec((tm, D_IN), row), _full_spec(wt_in), pl.BlockSpec((tm, D_MODEL), row)]
        + _tie_specs(ties),
        out_specs=pl.BlockSpec((tm, D_MODEL), row),
        out_shape=jax.ShapeDtypeStruct((s, D_MODEL), F32), name="dx_mm", compiler_params=_params("arbitrary"),
    )(dproj, wt_in, dz, *ties)


def _dw_in_matmul(x, dproj, ties=(), tk=1024):
    s = x.shape[0]
    tn = 2 * W_IN_COLS
    nk = s // tk

    def body(x_ref, dp_ref, *rest):
        o_ref, acc = rest[len(ties):]
        k = pl.program_id(1)

        @pl.when(k == 0)
        def _():
            acc[...] = jnp.zeros_like(acc)

        acc[...] += _mm_tn(dp_ref[...], x_ref[...])

        @pl.when(k == nk - 1)
        def _():
            o_ref[...] = acc[...].astype(o_ref.dtype)

    return pl.pallas_call(
        body, grid=(D_IN // tn, nk),
        in_specs=[pl.BlockSpec((tk, D_MODEL), lambda j, k: (k, 0)), pl.BlockSpec((tk, tn), lambda j, k: (k, j))]
        + _tie_specs(ties),
        out_specs=pl.BlockSpec((tn, D_MODEL), lambda j, k: (j, 0)),
        out_shape=jax.ShapeDtypeStruct((D_IN, D_MODEL), GRAD_DTYPE),
        scratch_shapes=[pltpu.VMEM((tn, D_MODEL), F32)], name="dw_in_mm",
        compiler_params=_params("arbitrary", "arbitrary"),
    )(x, dproj, *ties)


def _in_bwd(x, dproj, wt_in, dz, tm=512):
    s = x.shape[0]
    n_steps = s // tm

    assert wt_in.dtype == GRAD_DTYPE
    blk = 2 * W_IN_COLS

    def body(x_ref, dp_ref, w_hbm, dz_ref, o_ref, gw_hbm, w_vmem, acc, sem):
        i = pl.program_id(0)

        @pl.when(i == 0)
        def _():
            fetch = pltpu.make_async_copy(w_hbm, w_vmem, sem)
            fetch.start()
            acc[...] = jnp.zeros_like(acc)
            fetch.wait()

        o_ref[...] = _mm(dp_ref[...], w_vmem[...]) + ALPHA * dz_ref[...]
        xb = x_ref[...].astype(MM_DTYPE)
        for j in range(D_IN // blk):
            acc[j * blk:(j + 1) * blk, :] += _mm_tn(dp_ref[:, j * blk:(j + 1) * blk], xb)

        @pl.when(i == n_steps - 1)
        def _():
            w_vmem[...] = acc[...].astype(w_vmem.dtype)
            emit = pltpu.make_async_copy(w_vmem, gw_hbm, sem)
            emit.start()
            emit.wait()

    row = lambda i: (i, 0)
    any_spec = pl.BlockSpec(memory_space=pl.ANY)
    return pl.pallas_call(
        body, grid=(n_steps,),
        in_specs=[pl.BlockSpec((tm, D_MODEL), row), pl.BlockSpec((tm, D_IN), row), any_spec,
                  pl.BlockSpec((tm, D_MODEL), row)],
        out_specs=(pl.BlockSpec((tm, D_MODEL), row), any_spec),
        out_shape=(jax.ShapeDtypeStruct((s, D_MODEL), F32), jax.ShapeDtypeStruct((D_IN, D_MODEL), GRAD_DTYPE)),
        scratch_shapes=[pltpu.VMEM((D_IN, D_MODEL), wt_in.dtype), pltpu.VMEM((D_IN, D_MODEL), F32),
                        pltpu.SemaphoreType.DMA],
        name="in_bwd", compiler_params=_params("arbitrary"),
    )(x, dproj, wt_in, dz)


def _kv_bwd(mem, dk, dv):
    def body(mem_ref, dk_ref, dv_ref, o_ref):
        grp = _head_of_lane((MEM_LEN, D_G))
        dk_sum = jnp.zeros((MEM_LEN, D_G), F32)
        dv_sum = jnp.zeros((MEM_LEN, D_G), F32)
        for h in range(N_SUB):
            dk_sum = dk_sum + jnp.where(grp == h, dk_ref[h], 0.0)
            dv_sum = dv_sum + jnp.where(grp == h, dv_ref[h], 0.0)
        o_ref[...] = _mm_tn(mem_ref[...], jnp.concatenate([dk_sum, dv_sum], axis=1)).astype(o_ref.dtype)

    return pl.pallas_call(body, out_shape=jax.ShapeDtypeStruct((D_MODEL, 2 * D_G), GRAD_DTYPE), name="kv_bwd",
                          compiler_params=_params())(mem, dk, dv)


def _layer_fwd(x, mem, layer, repl, gw, proj=None, km_vm=None, ties=()):
    km, vm = _kv_project(mem, gw["w_kv"]) if km_vm is None else km_vm
    proj, h, z, xn, conv_d = _layer_fwd_fused(x, gw["wt_in"] if proj is None else None, proj, km, vm, layer, repl,
                                              gw["pw"], gw["ca8"], gw["dw8"], gw["w_out"], ties)
    return xn, (x, proj, h, z, km, vm, conv_d)


def _layer_bwd_a(up, target, mem, layer, repl, gw, saved, ties=()):
    x_in, proj, h, z, km, vm, conv_d = saved
    dz, dh, g_w_out, g1024 = _ln_out_bwd(up, target, z, repl["ln_g"], repl["ln_b"], layer, gw["w_out"], h, ties)
    dproj, bg = _branch_bwd(proj, conv_d, dh, km, vm, layer, repl, gw["pw"], gw["ca8"], gw["dw8"])
    grads = {n: bg[n] for n in ("g256", "sg_w", "sg_b", "pool_w", "conv_a_w", "cc_dw_w")}
    grads.update(w_out=g_w_out.reshape(N_DEV, D_MIX // N_DEV, D_MODEL), g1024=g1024,
                 w_kv=_kv_bwd(mem, bg["dk"], bg["dv"]).reshape(N_DEV, D_MODEL // N_DEV, 2 * D_G),
                 cc_pw_w=bg["cc_pw_w"].reshape(N_DEV, CONV_CH, D_G))
    return dz, dproj, grads


def _landing_shapes(items):
    out = []
    for a, scatter, pick in items:
        shape = a.shape if scatter else (N_DEV,) + (a.shape if pick is None else a.shape[1:])
        out.append(jax.ShapeDtypeStruct(shape, a.dtype))
    return tuple(out)


def _exchange_sems(n):
    return [pltpu.SemaphoreType.DMA(((N_DEV - 1) * n,)), pltpu.SemaphoreType.DMA(((N_DEV - 1) * n,)),
            pltpu.SemaphoreType.DMA((n,))]


def _exchange_copies(modes, ins, outs, send_sems, recv_sems, local_sems):
    n = len(ins)
    x, y, c = lax.axis_index("x"), lax.axis_index("y"), lax.axis_index("c")
    me = 4 * x + 2 * y + c

    def src_of(a, dest):
        scatter, pick = modes[a]
        if scatter:
            return ins[a].at[dest]
        return ins[a] if pick is None else ins[a].at[pick]

    local = [pltpu.make_async_copy(src_of(a, me), outs[a].at[me], local_sems.at[a]) for a in range(n)]
    sends, recvs = [], []
    for k in range(1, N_DEV):
        px = 1 - x if k & 4 else x
        py = 1 - y if k & 2 else y
        pc = 1 - c if k & 1 else c
        peer = 4 * px + 2 * py + pc
        for a in range(n):
            sems = dict(send_sem=send_sems.at[(k - 1) * n + a], recv_sem=recv_sems.at[(k - 1) * n + a],
                        device_id=(px, py, pc), device_id_type=pl.DeviceIdType.MESH)
            sends.append(pltpu.make_async_remote_copy(src_ref=src_of(a, peer), dst_ref=outs[a].at[me], **sems))
            recvs.append(pltpu.make_async_remote_copy(src_ref=src_of(a, peer), dst_ref=outs[a].at[peer], **sems))
    return local, sends, recvs


def _exchange(items, name):
    n = len(items)
    modes = [(scatter, pick) for _, scatter, pick in items]

    def body(*refs):
        local, sends, recvs = _exchange_copies(modes, refs[:n], refs[n:2 * n], *refs[2 * n:])
        for cp in local + sends:
            cp.start()
        for cp in recvs:
            cp.wait_recv()
        for cp in sends:
            cp.wait_send()
        for cp in local:
            cp.wait()

    any_spec = pl.BlockSpec(memory_space=pl.ANY)
    return pl.pallas_call(
        body, in_specs=[any_spec] * n, out_specs=(any_spec,) * n, out_shape=_landing_shapes(items),
        scratch_shapes=_exchange_sems(n), name=name,
    )(*[a for a, _, _ in items])


def _gather_two_level(items, name):
    n = len(items)
    assert not any(scatter for _, scatter, _ in items)
    picks = [pick for _, _, pick in items]

    def body(*refs):
        ins, outs = refs[:n], refs[n:2 * n]
        send_sems, recv_sems, local_sems = refs[2 * n:]
        x, y, c = lax.axis_index("x"), lax.axis_index("y"), lax.axis_index("c")
        sib = 1 - c
        chips = [(1 - x, y), (x, 1 - y), (1 - x, 1 - y)]

        def slot(a, px, py, pc):
            return outs[a].at[4 * px + 2 * py + pc]

        def copy(k, a, src, block, to):
            return pltpu.make_async_remote_copy(
                src_ref=src, dst_ref=slot(a, *block), send_sem=send_sems.at[k * n + a],
                recv_sem=recv_sems.at[k * n + a], device_id=to, device_id_type=pl.DeviceIdType.MESH)

        own = [ins[a] if picks[a] is None else ins[a].at[picks[a]] for a in range(n)]
        local = [pltpu.make_async_copy(own[a], slot(a, x, y, c), local_sems.at[a]) for a in range(n)]
        first = [copy(0, a, own[a], (x, y, c), (x, y, sib)) for a in range(n)]
        first += [copy(1 + j, a, own[a], (x, y, c), (*chip, c)) for j, chip in enumerate(chips) for a in range(n)]
        for cp in local + first:
            cp.start()
        passed = []
        for j, chip in enumerate(chips):
            for a in range(n):
                copy(1 + j, a, own[a], (*chip, c), (x, y, c)).wait_recv()
                fwd = copy(4 + j, a, slot(a, *chip, c), (*chip, c), (x, y, sib))
                fwd.start()
                passed.append(fwd)
        for a in range(n):
            copy(0, a, own[a], (x, y, sib), (x, y, c)).wait_recv()
        for j, chip in enumerate(chips):
            for a in range(n):
                copy(4 + j, a, own[a], (*chip, sib), (x, y, c)).wait_recv()
        for cp in first + passed:
            cp.wait_send()
        for cp in local:
            cp.wait()

    any_spec = pl.BlockSpec(memory_space=pl.ANY)
    return pl.pallas_call(
        body, in_specs=[any_spec] * n, out_specs=(any_spec,) * n, out_shape=_landing_shapes(items),
        scratch_shapes=[pltpu.SemaphoreType.DMA((7 * n,)), pltpu.SemaphoreType.DMA((7 * n,)),
                        pltpu.SemaphoreType.DMA((n,))],
        name=name,
    )(*[a for a, _, _ in items])


_HBM_SPEC = pl.BlockSpec(memory_space=pltpu.HBM)
_SEM_SPEC = pl.BlockSpec(memory_space=pltpu.SEMAPHORE)
_SPLIT_PARAMS = pltpu.CompilerParams(has_side_effects=pltpu.SideEffectType.DATAFLOW_SIDE_EFFECTING)


def _split_start(srcs, lands, plan, sem_shapes, name):
    n_src, n_land = len(srcs), len(lands)
    n_buf = n_src + n_land
    bufs = [pltpu.with_memory_space_constraint(a, pltpu.HBM) for a in list(srcs) + list(lands)]

    def body(*refs):
        local, sends, _ = plan(refs[:n_src], refs[n_src:n_buf], *refs[n_buf:n_buf + 3])
        for cp in local + sends:
            cp.start()
        token = refs[-1]
        token[...] = jnp.zeros_like(token)

    res = pl.pallas_call(
        body, name=name, in_specs=[_HBM_SPEC] * n_buf,
        out_shape=tuple(sem_shapes) + tuple(pltpu.HBM(a.shape, a.dtype) for a in bufs)
        + (jax.ShapeDtypeStruct((8, 128), F32),),
        out_specs=(_SEM_SPEC,) * 3 + (_HBM_SPEC,) * n_buf + (pl.BlockSpec(memory_space=pltpu.VMEM),),
        input_output_aliases={i: 3 + i for i in range(n_buf)}, compiler_params=_SPLIT_PARAMS,
    )(*bufs)
    return dict(sems=res[:3], srcs=res[3:3 + n_src], lands=res[3 + n_src:3 + n_buf], token=res[-1], plan=plan)


def _split_wait(ticket, after, name):
    n_src, n_land = len(ticket["srcs"]), len(ticket["lands"])
    n_buf = n_src + n_land
    plan = ticket["plan"]

    def body(*refs):
        local, sends, recvs = plan(refs[:n_src], refs[n_src:n_buf], *refs[n_buf:n_buf + 3])
        for cp in recvs:
            cp.wait_recv()
        for cp in sends:
            cp.wait_send()
        for cp in local:
            cp.wait()

    bufs = list(ticket["srcs"]) + list(ticket["lands"])
    res = pl.pallas_call(
        body, name=name, in_specs=[_HBM_SPEC] * n_buf + [_SEM_SPEC] * 3 + [pl.BlockSpec(memory_space=pl.ANY)],
        out_shape=tuple(pltpu.HBM(a.shape, a.dtype) for a in bufs), out_specs=(_HBM_SPEC,) * n_buf,
        input_output_aliases={i: i for i in range(n_buf)}, compiler_params=_SPLIT_PARAMS,
    )(*bufs, *ticket["sems"], after)
    return res[n_src:]


def _empty_landings(items):
    return [lax.empty(s.shape, s.dtype) for s in _landing_shapes(items)]


def _exchange_start(items, name):
    modes = [(scatter, pick) for _, scatter, pick in items]
    plan = lambda ins, outs, *sems: _exchange_copies(modes, ins, outs, *sems)
    return _split_start([a for a, _, _ in items], _empty_landings(items), plan, _exchange_sems(len(items)), name)


def _two_level_plans(picks):
    n = len(picks)

    def place():
        x, y, c = lax.axis_index("x"), lax.axis_index("y"), lax.axis_index("c")
        return x, y, c, 1 - c, [(1 - x, y), (x, 1 - y), (1 - x, 1 - y)]

    def copy(outs, send_sems, recv_sems, k, a, src, block, to):
        px, py, pc = block
        return pltpu.make_async_remote_copy(
            src_ref=src, dst_ref=outs[a].at[4 * px + 2 * py + pc], send_sem=send_sems.at[k * n + a],
            recv_sem=recv_sems.at[k * n + a], device_id=to, device_id_type=pl.DeviceIdType.MESH)

    def between_chips(ins, outs, send_sems, recv_sems, local_sems):
        x, y, c, sib, chips = place()
        own = [ins[a] if picks[a] is None else ins[a].at[picks[a]] for a in range(n)]
        mk = lambda *args: copy(outs, send_sems, recv_sems, *args)
        local = [pltpu.make_async_copy(own[a], outs[a].at[4 * x + 2 * y + c], local_sems.at[a]) for a in range(n)]
        sends = [mk(0, a, own[a], (x, y, c), (x, y, sib)) for a in range(n)]
        sends += [mk(1 + j, a, own[a], (x, y, c), (*chip, c)) for j, chip in enumerate(chips) for a in range(n)]
        recvs = [mk(0, a, own[a], (x, y, sib), (x, y, c)) for a in range(n)]
        recvs += [mk(1 + j, a, own[a], (*chip, c), (x, y, c)) for j, chip in enumerate(chips) for a in range(n)]
        return local, sends, recvs

    def within_chip(ins, outs, send_sems, recv_sems, local_sems):
        x, y, c, sib, chips = place()
        mk = lambda *args: copy(outs, send_sems, recv_sems, *args)
        slot = lambda a, px, py, pc: outs[a].at[4 * px + 2 * py + pc]
        sends = [mk(j, a, slot(a, *chip, c), (*chip, c), (x, y, sib)) for j, chip in enumerate(chips)
                 for a in range(n)]
        recvs = [mk(j, a, slot(a, *chip, c), (*chip, sib), (x, y, c)) for j, chip in enumerate(chips)
                 for a in range(n)]
        return [], sends, recvs

    sems = lambda k: [pltpu.SemaphoreType.DMA((k * n,)), pltpu.SemaphoreType.DMA((k * n,)),
                      pltpu.SemaphoreType.DMA((n,))]
    return between_chips, sems(4), within_chip, sems(3)


def _adam_math(g, w, m, v):
    m_new = ADAM_B1 * m + (1.0 - ADAM_B1) * g
    v_new = ADAM_B2 * v + (1.0 - ADAM_B2) * (g * g)
    m_hat = m_new / (1.0 - ADAM_B1 ** ADAM_STEP)
    v_hat = v_new / (1.0 - ADAM_B2 ** ADAM_STEP)
    return -ADAM_LR * (m_hat / (jnp.sqrt(v_hat) + ADAM_EPS) + ADAM_WD * w), m_new, v_new


def _adamw_big(parts, w, m, v, layer, prev, name, tr):
    depth, rows, cols = w.shape

    def body(p_ref, w_ref, m_ref, v_ref, *rest):
        g_out, d_out, m_out, v_out = rest[len(prev):]
        g = p_ref[0].astype(F32)
        for q in range(1, N_DEV):
            g = g + p_ref[q].astype(F32)
        d, m_new, v_new = _adam_math(g, w_ref[...], m_ref[...], v_ref[...])
        g_out[...] = g
        d_out[...] = d
        m_out[...] = m_new
        v_out[...] = v_new

    blk = pl.BlockSpec((None, tr, cols), lambda i: (layer, i, 0))
    shp = jax.ShapeDtypeStruct((depth, rows, cols), F32)
    return pl.pallas_call(
        body, grid=(rows // tr,),
        in_specs=[pl.BlockSpec((N_DEV, tr, cols), lambda i: (0, i, 0)), blk, blk, blk]
        + [pl.BlockSpec(memory_space=pl.ANY)] * len(prev),
        out_specs=(blk,) * 4, out_shape=(shp,) * 4,
        input_output_aliases={4 + j: j for j in range(len(prev))},
        name=name, compiler_params=_params("arbitrary"),
    )(parts, w, m, v, *prev)


_SMALL_TENSORS = (("conv_a_w", "conv_a_w", None), ("cc_dw_w", "cc_dw_w", None), ("cc_pw_w", "cc_pw_w", None),
                  ("sg_w", "sg_w", None), ("pool_w", "pool_w", None), ("sg_b", "sg_b", None)) \
    + tuple((n, "g256", k) for k, n in enumerate(G256_ROWS)) + tuple((n, "g1024", k) for k, n in enumerate(G1024_ROWS))
_SMALL_LANDINGS = ("conv_a_w", "cc_dw_w", "cc_pw_w", "sg_w", "pool_w", "sg_b", "g256", "g1024")
_TAPS_FIRST = ("conv_a_w", "cc_dw_w")


def _adamw_small(landings, wts, mom, var):
    names = [n for n, _, _ in _SMALL_TENSORS]
    n_land = DEPTH * len(_SMALL_LANDINGS)
    n_t = len(names)

    def body(*refs):
        land = [dict(zip(_SMALL_LANDINGS, refs[l * len(_SMALL_LANDINGS):(l + 1) * len(_SMALL_LANDINGS)]))
                for l in range(DEPTH)]
        w_refs = dict(zip(names, refs[n_land:n_land + n_t]))
        m_refs = dict(zip(names, refs[n_land + n_t:n_land + 2 * n_t]))
        v_refs = dict(zip(names, refs[n_land + 2 * n_t:n_land + 3 * n_t]))
        outs = refs[n_land + 3 * n_t:]
        out_refs = {n: outs[4 * k:4 * k + 4] for k, n in enumerate(names)}
        loss_ref = outs[4 * n_t]
        for name, key, row in _SMALL_TENSORS:
            for l in range(DEPTH):
                src = land[l][key]
                if row is not None:
                    part = lambda q: src[q, row:row + 1, :]
                    at = (slice(l, l + 1),)
                elif name == "sg_b":
                    part = lambda q: src[q, 0:N_SUB, :]
                    at = (l,)
                elif name in _TAPS_FIRST:
                    part = lambda q: src[q]
                    at = (slice(None), l)
                else:
                    part = lambda q: src[q]
                    at = (l,)
                g = part(0)
                for q in range(1, N_DEV):
                    g = g + part(q)
                d, m_new, v_new = _adam_math(g, w_refs[name][at], m_refs[name][at], v_refs[name][at])
                for ref, val in zip(out_refs[name], (g, d, m_new, v_new)):
                    ref[at] = val
        src = land[DEPTH - 1]["g1024"]
        loss = src[0, LOSS_ROW:LOSS_ROW + 1, 0:128]
        for q in range(1, N_DEV):
            loss = loss + src[q, LOSS_ROW:LOSS_ROW + 1, 0:128]
        loss_ref[...] = loss

    ins = [landings[l][k] for l in range(DEPTH) for k in _SMALL_LANDINGS] \
        + [src[n] for src in (wts, mom, var) for n in names]
    out_shape = tuple(jax.ShapeDtypeStruct(wts[n].shape, F32) for n in names for _ in range(4)) \
        + (jax.ShapeDtypeStruct((1, 128), F32),)
    res = pl.pallas_call(body, out_shape=out_shape, name="adamw_small", compiler_params=_params())(*ins)
    return {n: res[4 * k:4 * k + 4] for k, n in enumerate(names)}, res[4 * n_t]


_BIG = (("w_in", 64), ("w_out", 32), ("w_kv", 32))
_GRAD_ITEMS_EARLY = ("w_out", "w_kv", "cc_pw_w", "conv_a_w", "cc_dw_w")
_GRAD_ITEMS_REPL = ("g256", "sg_w", "sg_b", "pool_w", "g1024")


def _grad_items(grads, with_w_in):
    items = [(grads[n], True, None) for n in (("w_in",) if with_w_in else ()) + _GRAD_ITEMS_EARLY]
    return items + [(grads[n], False, None) for n in _GRAD_ITEMS_REPL]


def _landed(parts, with_w_in):
    names = (("w_in",) if with_w_in else ()) + _GRAD_ITEMS_EARLY + _GRAD_ITEMS_REPL
    return dict(zip(names, parts))


def _gathered_weights(wt_in8, w_kv8, w_out8, pw8, ca8, dw8):
    return dict(wt_in=wt_in8.reshape(D_IN, D_MODEL), w_kv=w_kv8.reshape(D_MODEL, 2 * D_G),
                w_out=w_out8.reshape(D_MIX, D_MODEL), pw=pw8.reshape(D_G, D_G), ca8=ca8, dw8=dw8)


def kernel(x, mem, w_in, conv_a_w, sg_ln_g, sg_ln_b, sg_w, sg_b, pool_w, pool_scale, cc_dw_w, cc_dw_b, cc_ln_g, cc_ln_b, cc_pw_w, w_kv, w_out, ln_g, ln_b, loss_target, m_w_in, m_conv_a_w, m_sg_ln_g, m_sg_ln_b, m_sg_w, m_sg_b, m_pool_w, m_pool_scale, m_cc_dw_w, m_cc_dw_b, m_cc_ln_g, m_cc_ln_b, m_cc_pw_w, m_w_kv, m_w_out, m_ln_g, m_ln_b, v_w_in, v_conv_a_w, v_sg_ln_g, v_sg_ln_b, v_sg_w, v_sg_b, v_pool_w, v_pool_scale, v_cc_dw_w, v_cc_dw_b, v_cc_ln_g, v_cc_ln_b, v_cc_pw_w, v_w_kv, v_w_out, v_ln_g, v_ln_b):
    names = ("w_in", "conv_a_w", "sg_ln_g", "sg_ln_b", "sg_w", "sg_b", "pool_w", "pool_scale", "cc_dw_w", "cc_dw_b",
             "cc_ln_g", "cc_ln_b", "cc_pw_w", "w_kv", "w_out", "ln_g", "ln_b")
    wts = dict(zip(names, (w_in, conv_a_w, sg_ln_g, sg_ln_b, sg_w, sg_b, pool_w, pool_scale, cc_dw_w, cc_dw_b,
                           cc_ln_g, cc_ln_b, cc_pw_w, w_kv, w_out, ln_g, ln_b)))
    mom = dict(zip(names, (m_w_in, m_conv_a_w, m_sg_ln_g, m_sg_ln_b, m_sg_w, m_sg_b, m_pool_w, m_pool_scale,
                           m_cc_dw_w, m_cc_dw_b, m_cc_ln_g, m_cc_ln_b, m_cc_pw_w, m_w_kv, m_w_out, m_ln_g, m_ln_b)))
    var = dict(zip(names, (v_w_in, v_conv_a_w, v_sg_ln_g, v_sg_ln_b, v_sg_w, v_sg_b, v_pool_w, v_pool_scale,
                           v_cc_dw_w, v_cc_dw_b, v_cc_ln_g, v_cc_ln_b, v_cc_pw_w, v_w_kv, v_w_out, v_ln_g, v_ln_b)))
    repl = wts
    xs, mems, tgt = x[0], mem[0], loss_target[0]
    turned = {"w_in": (0, 2, 1), "conv_a_w": (1, 0, 2), "cc_dw_w": (1, 0, 2)}
    wts, mom, var = [{n: (jnp.transpose(a, turned[n]) if n in turned else a) for n, a in src.items()}
                     for src in (wts, mom, var)]
    wb = {n: wts[n].astype(MM_DTYPE) for n in ("w_in", "w_kv", "w_out", "cc_pw_w")}

    wt8_0, wkv8_0 = _gather_two_level([(wb["w_in"], False, 0), (wb["w_kv"], False, 0)], "gather_weights_0a")
    rest_0 = _exchange_start([(wb["w_out"], False, 0), (wb["cc_pw_w"], False, 0), (wts["conv_a_w"], False, None),
                              (wts["cc_dw_w"], False, None)], "gather_weights_0b_start")
    km_vm0 = _kv_project(mems, wkv8_0.reshape(D_MODEL, 2 * D_G))
    proj0 = _proj_matmul(xs, wt8_0.reshape(D_IN, D_MODEL), (rest_0["token"],))
    wo8_0, pw8_0, ca8, dw8 = _split_wait(rest_0, proj0, "gather_weights_0b_wait")
    gw0 = _gathered_weights(wt8_0, wkv8_0, wo8_0, pw8_0, ca8, dw8)
    items_1 = [(wb[n], False, 1) for n in ("w_in", "w_kv", "w_out", "cc_pw_w")]
    between_chips, sems_a, within_chip, sems_b = _two_level_plans([1] * len(items_1))
    chips_1 = _split_start([a for a, _, _ in items_1], _empty_landings(items_1), between_chips, sems_a,
                           "gather_weights_1a_start")
    x1, saved0 = _layer_fwd(xs, mems, 0, repl, gw0, proj0, km_vm0, (chips_1["token"],))
    core_1 = _split_start([], _split_wait(chips_1, x1, "gather_weights_1a_wait"), within_chip, sems_b,
                          "gather_weights_1b_start")
    gw1 = _gathered_weights(*_split_wait(core_1, core_1["token"], "gather_weights_1b_wait"), ca8, dw8)
    _, saved1 = _layer_fwd(x1, mems, 1, repl, gw1)

    dz1, dproj1, g1 = _layer_bwd_a(None, tgt, mems, 1, repl, gw1, saved1)
    shards = lambda g: g.reshape(N_DEV, W_IN_COLS, D_MODEL)
    up, g_wt_in_1 = _in_bwd(saved1[0], dproj1, gw1["wt_in"], dz1)
    g1["w_in"] = shards(g_wt_in_1)
    grads_1 = _exchange_start(_grad_items(g1, True), "exchange_grads_1_start")
    dz0, dproj0, g0 = _layer_bwd_a(up, None, mems, 0, repl, gw0, saved0, (grads_1["token"],))
    early_0 = _exchange_start(_grad_items(g0, False), "exchange_grads_0a_start")
    g_wt_in_0 = _dw_in_matmul(saved0[0], dproj0, (early_0["token"],))
    late_0 = _exchange_start([(shards(g_wt_in_0), True, None)], "exchange_grads_0b_start")
    grad_x = _dx_matmul(dproj0, gw0["wt_in"], dz0, (late_0["token"],))

    landed = [None, _landed(_split_wait(grads_1, grad_x, "exchange_grads_1_wait"), True)]
    big = {}
    for n, tr in _BIG:
        big[n] = _adamw_big(landed[1][n], wts[n], mom[n], var[n], 1, (), "adamw_" + n + "_1", tr)
    landed[0] = _landed(_split_wait(early_0, big["w_kv"][0], "exchange_grads_0a_wait"), False)
    for n, tr in _BIG[1:]:
        big[n] = _adamw_big(landed[0][n], wts[n], mom[n], var[n], 0, big[n], "adamw_" + n + "_0", tr)
    (landed[0]["w_in"],) = _split_wait(late_0, big["w_kv"][0], "exchange_grads_0b_wait")
    big["w_in"] = _adamw_big(landed[0]["w_in"], wts["w_in"], mom["w_in"], var["w_in"], 0, big["w_in"],
                             "adamw_w_in_0", _BIG[0][1])
    small, loss = _adamw_small(landed, wts, mom, var)

    res = {**small, **big}
    res = {n: ([jnp.transpose(a, turned[n]) for a in r] if n in turned else r) for n, r in res.items()}
    return (loss[0, 0], grad_x[None], *[res[n][0] for n in names], *[res[n][1] for n in names],
            *[res[n][2] for n in names], *[res[n][3] for n in names])
```

```python
import math

import jax
import jax.numpy as jnp
from jax import lax
from jax.experimental import pallas as pl
from jax.experimental.pallas import tpu as pltpu

F32 = jnp.float32
MM_DTYPE = jnp.bfloat16
GRAD_DTYPE = jnp.bfloat16

D_MODEL = 1024
DEPTH = 2
D_G = 256
N_GROUPS = 5
D_MIX = N_GROUPS * D_G
N_SUB = 4
HEAD_DIM = D_G // N_SUB
CONV_A = 3
CONV_D = 31
CHUNK = 128
POOL_WINDOWS = (2, 4, 8, 16)
MEM_LEN = 256
LN_EPS = 1e-5
ALPHA = (2.0 * DEPTH) ** 0.25
D_IN = 9 * D_G + D_MIX
ATT_SCALE = 1.0 / math.sqrt(HEAD_DIM)

ADAM_LR = 0.001
ADAM_B1 = 0.9
ADAM_B2 = 0.999
ADAM_EPS = 1e-08
ADAM_WD = 0.01
ADAM_STEP = 10

N_DEV = 8
W_IN_COLS = D_IN // N_DEV
CONV_CH = D_G // N_DEV
HALO = 32
TILE = 256
VMEM_LIMIT = 56 * 1024 * 1024

C_XA, C_BA, C_CA, C_U, C_V, C_XC, C_DA, C_DG, C_Q = range(9)
C_GATE = 9 * D_G

G256_ROWS = ("sg_ln_g", "sg_ln_b", "pool_scale", "cc_dw_b", "cc_ln_g", "cc_ln_b")
G1024_ROWS = ("ln_g", "ln_b")
LOSS_ROW = 2


def _mm(a, b):
    return jnp.dot(a.astype(MM_DTYPE), b.astype(MM_DTYPE), preferred_element_type=F32)


def _mm_nt(a, b):
    return lax.dot_general(a.astype(MM_DTYPE), b.astype(MM_DTYPE), (((1,), (1,)), ((), ())),
                           preferred_element_type=F32)


def _mm_tn(a, b):
    return lax.dot_general(a.astype(MM_DTYPE), b.astype(MM_DTYPE), (((0,), (0,)), ((), ())),
                           preferred_element_type=F32)


def _sigmoid(x):
    return 0.5 * jnp.tanh(0.5 * x) + 0.5


_GELU_C = math.sqrt(2.0 / math.pi)
_GELU_A = 0.044715


def _gelu(x):
    th = jnp.tanh(_GELU_C * (x + _GELU_A * (x * x * x)))
    return 0.5 * x * (1.0 + th), th


def _dgelu(x, th):
    return 0.5 * (1.0 + th) + 0.5 * x * (1.0 - th * th) * (_GELU_C * (1.0 + 3.0 * _GELU_A * (x * x)))


def _ln_fwd(x, g, b):
    mu = jnp.mean(x, axis=-1, keepdims=True)
    xc = x - mu
    var = jnp.mean(xc * xc, axis=-1, keepdims=True)
    rstd = lax.rsqrt(var + LN_EPS)
    xhat = xc * rstd
    return xhat * g + b, xhat, rstd


def _ln_bwd(dy, xhat, rstd, g):
    dxhat = dy * g
    m1 = jnp.mean(dxhat, axis=-1, keepdims=True)
    m2 = jnp.mean(dxhat * xhat, axis=-1, keepdims=True)
    return rstd * (dxhat - m1 - xhat * m2)


def _rowsum(x):
    return jnp.sum(x, axis=0, keepdims=True)


def _col(ref, k):
    return ref[:, k * D_G:(k + 1) * D_G]


def _head_of_lane(shape):
    return jnp.right_shift(lax.broadcasted_iota(jnp.int32, shape, len(shape) - 1), HEAD_DIM.bit_length() - 1)


def _pool_select(lane_grp, s2, s4, s8, s16):
    return jnp.where(lane_grp == 0, s2, jnp.where(lane_grp == 1, s4, jnp.where(lane_grp == 2, s8, s16)))


def _row_view(ref, layer):
    return ref.at[pl.ds(layer, 1)]


def _make_residues(ext_ref, res_ref):
    rows = res_ref.shape[1]
    for r in range(1, 8):
        res_ref[r - 1] = ext_ref[pl.ds(r, rows), :]


def _rows_at(ext_ref, res_ref, off, tile):
    a, r = divmod(off, 8)
    if r == 0:
        return ext_ref[pl.ds(off, tile), :]
    return res_ref[r - 1, pl.ds(8 * a, tile), :]


def _residue_scratch(tile):
    return pltpu.VMEM((7, HALO + tile - 8, D_G), F32)


def _branch_forward(p_ref, ph_ref, first, row0, km_ref, vm_ref, w, ext_a, ext_c, ext_d, res_c, res_d, tile,
                    kept_ref=None):
    r = {}
    xa, ba, ca = _col(p_ref, C_XA), _col(p_ref, C_BA), _col(p_ref, C_CA)
    g_a = ca * xa
    ext_a[0:HALO] = jnp.where(first, 0.0, _col(ph_ref, C_CA) * _col(ph_ref, C_XA))
    ext_a[HALO:HALO + tile] = g_a
    conv_a = w["conv_a"][0:1, :] * ext_a[pl.ds(HALO - 2, tile), :]
    for k in range(1, CONV_A):
        conv_a = conv_a + w["conv_a"][k:k + 1, :] * ext_a[pl.ds(HALO - 2 + k, tile), :]
    r.update(xa=xa, ba=ba, ca=ca, g_a=g_a, conv_a=conv_a)
    ya = ba * conv_a

    lane_grp = _head_of_lane((tile, D_G))
    trow = row0 + lax.broadcasted_iota(jnp.int32, (tile, D_G), 0)
    win = _pool_select(lane_grp, 2, 4, 8, 16)
    inv_cnt = 1.0 / jnp.minimum(trow + 1, win).astype(F32)
    if kept_ref is None:
        xc = _col(p_ref, C_XC)
        ext_c[0:HALO] = jnp.where(first, 0.0, _col(ph_ref, C_XC))
        ext_c[HALO:HALO + tile] = xc
        _make_residues(ext_c, res_c)
        acc = xc
        sums = {}
        for k in range(1, POOL_WINDOWS[-1]):
            acc = acc + _rows_at(ext_c, res_c, HALO - k, tile)
            if k + 1 in POOL_WINDOWS:
                sums[k + 1] = acc
        ypre = _pool_select(lane_grp, sums[2], sums[4], sums[8], sums[16]) * inv_cnt - xc
    else:
        ypre = kept_ref[:, D_G:2 * D_G]
    pool_mm = _mm(ypre, w["pool_wbd"][...])
    yc = pool_mm * w["pool_scale"][...]
    r.update(lane_grp=lane_grp, inv_cnt=inv_cnt, ypre=ypre, pool_mm=pool_mm)

    da, dg = _col(p_ref, C_DA), _col(p_ref, C_DG)
    sig_dg = _sigmoid(dg)
    hd = da * sig_dg
    if kept_ref is None:
        ext_d[0:HALO] = jnp.where(first, 0.0, _col(ph_ref, C_DA) * _sigmoid(_col(ph_ref, C_DG)))
        ext_d[HALO:HALO + tile] = hd
        _make_residues(ext_d, res_d)
        conv_d = w["cc_dw_b"][...] + w["cc_dw_w"][0:1, :] * _rows_at(ext_d, res_d, HALO - (CONV_D - 1), tile)
        for j in range(1, CONV_D):
            conv_d = conv_d + w["cc_dw_w"][j:j + 1, :] * _rows_at(ext_d, res_d, HALO - (CONV_D - 1) + j, tile)
    else:
        conv_d = kept_ref[:, 0:D_G]
    r["kept"] = (conv_d, ypre)
    ln_d, xhat_d, rstd_d = _ln_fwd(conv_d, w["cc_ln_g"][...], w["cc_ln_b"][...])
    sig_ln = _sigmoid(ln_d)
    act_d = ln_d * sig_ln
    yd = _mm(act_d, w["cc_pw_w"][...])
    r.update(da=da, sig_dg=sig_dg, hd=hd, ln_d=ln_d, xhat_d=xhat_d, rstd_d=rstd_d, sig_ln=sig_ln, act_d=act_d)

    u, v = _col(p_ref, C_U), _col(p_ref, C_V)
    ug, th_u = _gelu(u)
    vg, th_v = _gelu(v)
    vn, xhat_v, rstd_v = _ln_fwd(vg, w["sg_ln_g"][...], w["sg_ln_b"][...])
    tri = (lax.broadcasted_iota(jnp.int32, (CHUNK, CHUNK), 0)
           >= lax.broadcasted_iota(jnp.int32, (CHUNK, CHUNK), 1))
    wm = [jnp.where(tri, w["sg_w"][h], 0.0).astype(MM_DTYPE) for h in range(N_SUB)]
    lo = lax.broadcasted_iota(jnp.int32, (CHUNK, 2 * HEAD_DIM), 1) < HEAD_DIM
    chunks = []
    for c in range(tile // CHUNK):
        halves = []
        for hf in range(2):
            vh = vn[c * CHUNK:(c + 1) * CHUNK, hf * 128:(hf + 1) * 128]
            halves.append(_mm(wm[2 * hf], jnp.where(lo, vh, 0.0)) + _mm(wm[2 * hf + 1], jnp.where(lo, 0.0, vh)))
        chunks.append(jnp.concatenate(halves, axis=1) + w["sg_bias"][...])
    mixed = jnp.concatenate(chunks, axis=0)
    yb = ug * mixed
    r.update(u=u, v=v, ug=ug, th_u=th_u, th_v=th_v, vn=vn, xhat_v=xhat_v, rstd_v=rstd_v, wm=wm, lo=lo,
             mixed=mixed, tri=tri)

    q = _col(p_ref, C_Q)
    ye = jnp.zeros((tile, D_G), F32)
    probs = []
    for h in range(N_SUB):
        s = _mm_nt(q, km_ref[h]) * ATT_SCALE
        e = jnp.exp(s - jnp.max(s, axis=-1, keepdims=True))
        p = e * (1.0 / jnp.sum(e, axis=-1, keepdims=True))
        probs.append(p)
        ye = ye + _mm(p, vm_ref[h])
    r.update(q=q, probs=probs)

    gate = p_ref[:, C_GATE:C_GATE + D_MIX]
    sig_gate = _sigmoid(gate)
    concat = jnp.concatenate([ya, yb, yc, yd, ye], axis=1)
    r.update(gate=gate, sig_gate=sig_gate, concat=concat)
    return r


_BRANCH_REPL = ("sg_ln_g", "sg_ln_b", "sg_w", "sg_b", "pool_w", "pool_scale", "cc_dw_b", "cc_ln_g", "cc_ln_b")
_BRANCH_W_SCRATCH = (("conv_a", (CONV_A, D_G)), ("cc_dw_w", (CONV_D, D_G)), ("sg_bias", (CHUNK, D_G)),
                     ("pool_wbd", (D_G, D_G)), ("sgb8", (8, CHUNK)))


def _branch_weights(layer, nat, pw_ref, ca_ref, dw_ref, scr, init):
    @pl.when(init)
    def _():
        for p in range(N_DEV):
            scr["conv_a"][:, p * CONV_CH:(p + 1) * CONV_CH] = ca_ref[p, :, layer, :]
            scr["cc_dw_w"][:, p * CONV_CH:(p + 1) * CONV_CH] = dw_ref[p, :, layer, :]
        scr["sgb8"][...] = jnp.zeros((8, CHUNK), F32)
        scr["sgb8"][0:N_SUB] = nat["sg_b"][layer]
        sgb_t = scr["sgb8"][...].T
        head = _head_of_lane((CHUNK, D_G))
        bias = jnp.zeros((CHUNK, D_G), F32)
        for h in range(N_SUB):
            bias = jnp.where(head == h, sgb_t[:, h:h + 1], bias)
        scr["sg_bias"][...] = bias
        scr["pool_wbd"][...] = jnp.zeros((D_G, D_G), F32)
        for gi in range(N_SUB):
            sl = slice(gi * HEAD_DIM, (gi + 1) * HEAD_DIM)
            scr["pool_wbd"][sl, sl] = nat["pool_w"][layer, gi]

    w = {n: _row_view(nat[n], layer) for n in ("sg_ln_g", "sg_ln_b", "pool_scale", "cc_dw_b", "cc_ln_g", "cc_ln_b")}
    w.update(conv_a=scr["conv_a"], cc_dw_w=scr["cc_dw_w"], sg_bias=scr["sg_bias"], pool_wbd=scr["pool_wbd"],
             sg_w=nat["sg_w"].at[layer], cc_pw_w=pw_ref)
    return w


def _full_spec(a):
    nd = a.ndim
    return pl.BlockSpec(a.shape, lambda *_, _nd=nd: (0,) * _nd)


def _tie_specs(ties):
    return [pl.BlockSpec((8, 128), lambda *_: (0, 0)) for _ in ties]


def _params(*sem):
    return pltpu.CompilerParams(dimension_semantics=sem or None, vmem_limit_bytes=VMEM_LIMIT)


def _proj_matmul(x, wt_in, ties=(), tm=256):
    s, k = x.shape

    def body(x_ref, w_ref, *rest):
        o_ref = rest[len(ties)]
        o_ref[...] = _mm_nt(x_ref[...], w_ref[...])

    return pl.pallas_call(
        body, grid=(s // tm,),
        in_specs=[pl.BlockSpec((tm, k), lambda i: (i, 0)), _full_spec(wt_in)] + _tie_specs(ties),
        out_specs=pl.BlockSpec((tm, D_IN), lambda i: (i, 0)),
        out_shape=jax.ShapeDtypeStruct((s, D_IN), F32), name="proj_mm", compiler_params=_params("arbitrary"),
    )(x, wt_in, *ties)


def _kv_project(mem, w_kv):
    def body(mem_ref, w_ref, km_ref, vm_ref):
        kv = _mm(mem_ref[...], w_ref[...])
        k, v = kv[:, :D_G], kv[:, D_G:]
        grp = _head_of_lane((MEM_LEN, D_G))
        for h in range(N_SUB):
            km_ref[h] = jnp.where(grp == h, k, 0.0).astype(km_ref.dtype)
            vm_ref[h] = jnp.where(grp == h, v, 0.0).astype(vm_ref.dtype)

    shp = jax.ShapeDtypeStruct((N_SUB, MEM_LEN, D_G), MM_DTYPE)
    return pl.pallas_call(body, out_shape=(shp, shp), name="kv_project", compiler_params=_params())(mem, w_kv)


def _layer_fwd_fused(x, wt_in, proj, km, vm, layer, repl, pw, ca8, dw8, w_out, want_xn, ties=(), tile=TILE):
    s = x.shape[0]
    hb = tile // HALO
    nat_arrays = [repl[n] for n in _BRANCH_REPL]
    n_nat, nt = len(nat_arrays), len(ties)
    given = proj is not None

    def body(x_ref, *rest):
        if given:
            p_ref, ph_ref = rest[:2]
            rest = rest[2:]
        else:
            wt_ref = rest[0]
            rest = rest[1:]
        km_ref, vm_ref = rest[:2]
        nat = dict(zip(_BRANCH_REPL, rest[2:2 + n_nat]))
        pw_ref, ca_ref, dw_ref, wo_ref, g_ref, b_ref = rest[2 + n_nat:8 + n_nat]
        rest = rest[8 + n_nat + nt:]
        if not given:
            p_ref, rest = rest[0], rest[1:]
        h_ref, z_ref, cd_ref = rest[:3]
        rest = rest[3:]
        if want_xn:
            xn_ref, rest = rest[0], rest[1:]
        if not given:
            ph_ref, rest = rest[0], rest[1:]
        ext_a, ext_c, ext_d, res_c, res_d = rest[:5]
        scr = dict(zip([n for n, _ in _BRANCH_W_SCRATCH], rest[5:]))
        i = pl.program_id(0)
        xt = x_ref[...]
        if not given:
            @pl.when(i == 0)
            def _():
                ph_ref[...] = jnp.zeros_like(ph_ref)

            p_ref[...] = _mm_nt(xt, wt_ref[...])
        w = _branch_weights(layer, nat, pw_ref, ca_ref, dw_ref, scr, i == 0)
        r = _branch_forward(p_ref, ph_ref, i == 0, i * tile, km_ref, vm_ref, w, ext_a, ext_c, ext_d, res_c, res_d,
                            tile)
        if not given:
            ph_ref[...] = p_ref[tile - HALO:tile, :]
        h = (r["concat"] * (r["gate"] * r["sig_gate"])).astype(h_ref.dtype)
        h_ref[...] = h
        cd_ref[:, 0:D_G], cd_ref[:, D_G:2 * D_G] = r["kept"]
        z = ALPHA * xt + _mm(h, wo_ref[...])
        z_ref[...] = z
        if want_xn:
            xn_ref[...] = _ln_fwd(z, _row_view(g_ref, layer)[...], _row_view(b_ref, layer)[...])[0]

    row = lambda i: (i, 0)
    consts = [km, vm] + nat_arrays + [pw, ca8, dw8, w_out, repl["ln_g"], repl["ln_b"]]
    act = jax.ShapeDtypeStruct((s, D_MODEL), F32)
    act_spec = pl.BlockSpec((tile, D_MODEL), row)
    if given:
        lead = [proj, proj]
        lead_specs = [pl.BlockSpec((tile, D_IN), row),
                      pl.BlockSpec((HALO, D_IN), lambda i: (jnp.maximum(i * hb - 1, 0), 0))]
        out_specs, out_shape, scratch = (), (), []
    else:
        lead = [wt_in]
        lead_specs = [_full_spec(wt_in)]
        out_specs = (pl.BlockSpec((tile, D_IN), row),)
        out_shape = (jax.ShapeDtypeStruct((s, D_IN), F32),)
        scratch = [pltpu.VMEM((HALO, D_IN), F32)]
    res = pl.pallas_call(
        body, grid=(s // tile,),
        in_specs=[act_spec] + lead_specs + [_full_spec(a) for a in consts] + _tie_specs(ties),
        out_specs=out_specs + (pl.BlockSpec((tile, D_MIX), row), act_spec, pl.BlockSpec((tile, 2 * D_G), row))
        + ((act_spec,) if want_xn else ()),
        out_shape=out_shape + (jax.ShapeDtypeStruct((s, D_MIX), MM_DTYPE), act, jax.ShapeDtypeStruct((s, 2 * D_G), F32))
        + ((act,) if want_xn else ()),
        scratch_shapes=scratch + [pltpu.VMEM((HALO + tile, D_G), F32)] * 3
        + [_residue_scratch(tile)] * 2 + [pltpu.VMEM(shape, F32) for _, shape in _BRANCH_W_SCRATCH],
        name="layer_fwd_given_proj" if given else "layer_fwd", compiler_params=_params("arbitrary"),
    )(x, *lead, *consts, *ties)
    res = ((proj,) + tuple(res)) if given else tuple(res)
    return res if want_xn else res + (None,)


_BRANCH_GRADS = (("g256", (8, D_G)), ("sg_w", (N_SUB, CHUNK, CHUNK)), ("sg_b", (8, CHUNK)),
                 ("pool_w", (N_SUB, HEAD_DIM, HEAD_DIM)), ("conv_a_w", (N_DEV, CONV_A, CONV_CH)),
                 ("cc_dw_w", (N_DEV, CONV_D, CONV_CH)), ("cc_pw_w", (D_G, D_G)),
                 ("dk", (N_SUB, MEM_LEN, D_G)), ("dv", (N_SUB, MEM_LEN, D_G)))
_BRANCH_ACC = (("conv_a", (CONV_A, D_G)), ("cc_dw_w", (CONV_D, D_G)), ("pool_wbd", (D_G, D_G)),
               ("sg_bias", (CHUNK, D_G)))


def _layer_bwd_fused(up, target, z, h, proj, kept, km, vm, layer, repl, w_out, pw, ca8, dw8, ties=(), tile=TILE):
    s = proj.shape[0]
    nt = s // tile
    hb = tile // HALO
    nat_arrays = [repl[n] for n in _BRANCH_REPL]
    n_nat, n_grads, n_acc, n_ties = len(nat_arrays), len(_BRANCH_GRADS), len(_BRANCH_ACC), len(ties)
    row_of = {n: k for k, n in enumerate(G256_ROWS)}
    from_loss = target is not None

    def body(o_ref, z_ref, h_ref, p_ref, ph_ref, cd_ref, km_ref, vm_ref, *rest):
        nat = dict(zip(_BRANCH_REPL, rest[:n_nat]))
        pw_ref, ca_ref, dw_ref, lng_ref, lnb_ref, wo_ref = rest[n_nat:n_nat + 6]
        rest = rest[n_nat + 6 + n_ties:]
        dz_ref, dp_ref, gw_ref, slab_ref = rest[:4]
        g = dict(zip([n for n, _ in _BRANCH_GRADS], rest[4:4 + n_grads]))
        rest = rest[4 + n_grads:]
        ext_a, rev_a, rev_c, rev_d, res_rc, res_rd, gw_acc, lacc = rest[:8]
        acc = dict(zip([n for n, _ in _BRANCH_ACC], rest[8:8 + n_acc]))
        scr = dict(zip([n for n, _ in _BRANCH_W_SCRATCH], rest[8 + n_acc:]))
        i = pl.program_id(0)
        t = nt - 1 - i

        @pl.when(i == 0)
        def _():
            for ref in list(g.values()) + list(acc.values()) + [rev_a, rev_c, rev_d, gw_acc, slab_ref, lacc]:
                ref[...] = jnp.zeros_like(ref)

        g_ln = _row_view(lng_ref, layer)[...]
        xn, xhat, rstd = _ln_fwd(z_ref[...], g_ln, _row_view(lnb_ref, layer)[...])
        if from_loss:
            err = xn - o_ref[...]
            lacc[...] += _rowsum(err * err)
            dxn = err * (1.0 / D_MODEL)
        else:
            dxn = o_ref[...]
        slab_ref[0:1, :] += _rowsum(dxn * xhat)
        slab_ref[1:2, :] += _rowsum(dxn)
        dz = _ln_bwd(dxn, xhat, rstd, g_ln)
        dz_ref[...] = dz
        dh_all = _mm_nt(dz, wo_ref[...])
        gw_acc[...] += _mm_tn(h_ref[...], dz)

        w = _branch_weights(layer, nat, pw_ref, ca_ref, dw_ref, scr, i == 0)
        r = _branch_forward(p_ref, ph_ref, t == 0, t * tile, km_ref, vm_ref, w, ext_a, None, None, None, None,
                            tile, cd_ref)

        def put(k, val, width=D_G):
            dp_ref[:, k:k + width] = val.astype(dp_ref.dtype)

        def add_row(name, val):
            k = row_of[name]
            g["g256"][k:k + 1, :] += val

        def push_rev(rev, val):
            head = rev[0:HALO]
            rev[tile:tile + HALO] = head
            rev[0:tile] = val

        gate, sig_gate, concat = r["gate"], r["sig_gate"], r["concat"]
        put(C_GATE, dh_all * concat * (sig_gate * (1.0 + gate * (1.0 - sig_gate))), D_MIX)
        dconcat = dh_all * (gate * sig_gate)
        dya, dyb, dyc, dyd, dye = [dconcat[:, k * D_G:(k + 1) * D_G] for k in range(N_GROUPS)]

        put(C_BA * D_G, dya * r["conv_a"])
        dconv_a = dya * r["ba"]
        push_rev(rev_a, dconv_a)
        dga = jnp.zeros((tile, D_G), F32)
        for k in range(CONV_A):
            ahead = rev_a[pl.ds(CONV_A - 1 - k, tile), :]
            dga = dga + w["conv_a"][k:k + 1, :] * ahead
            acc["conv_a"][k:k + 1, :] += _rowsum(r["g_a"] * ahead)
        put(C_CA * D_G, dga * r["xa"])
        put(C_XA * D_G, dga * r["ca"])

        add_row("pool_scale", _rowsum(dyc * r["pool_mm"]))
        dmm = dyc * w["pool_scale"][...]
        acc["pool_wbd"][...] += _mm_tn(r["ypre"], dmm)
        dypre = _mm_nt(dmm, w["pool_wbd"][...])
        dws = dypre * r["inv_cnt"]
        push_rev(rev_c, dws)
        _make_residues(rev_c, res_rc)
        run = dws
        sums = {}
        for k in range(1, POOL_WINDOWS[-1]):
            run = run + _rows_at(rev_c, res_rc, k, tile)
            if k + 1 in POOL_WINDOWS:
                sums[k + 1] = run
        put(C_XC * D_G, _pool_select(r["lane_grp"], sums[2], sums[4], sums[8], sums[16]) - dypre)

        g["cc_pw_w"][...] += _mm_tn(r["act_d"], dyd)
        dact = _mm_nt(dyd, w["cc_pw_w"][...])
        sig_ln, ln_d = r["sig_ln"], r["ln_d"]
        dln = dact * (sig_ln * (1.0 + ln_d * (1.0 - sig_ln)))
        add_row("cc_ln_g", _rowsum(dln * r["xhat_d"]))
        add_row("cc_ln_b", _rowsum(dln))
        dconv_d = _ln_bwd(dln, r["xhat_d"], r["rstd_d"], w["cc_ln_g"][...])
        add_row("cc_dw_b", _rowsum(dconv_d))
        push_rev(rev_d, dconv_d)
        _make_residues(rev_d, res_rd)
        dhd = jnp.zeros((tile, D_G), F32)
        for j in range(CONV_D):
            ahead = _rows_at(rev_d, res_rd, CONV_D - 1 - j, tile)
            dhd = dhd + w["cc_dw_w"][j:j + 1, :] * ahead
            acc["cc_dw_w"][j:j + 1, :] += _rowsum(r["hd"] * ahead)
        sig_dg = r["sig_dg"]
        put(C_DA * D_G, dhd * sig_dg)
        put(C_DG * D_G, dhd * r["da"] * sig_dg * (1.0 - sig_dg))

        dug = dyb * r["mixed"]
        dmixed = dyb * r["ug"]
        wm, lo, vn = r["wm"], r["lo"], r["vn"]
        dvn_chunks = []
        for c in range(tile // CHUNK):
            rows = slice(c * CHUNK, (c + 1) * CHUNK)
            acc["sg_bias"][...] += dmixed[rows, :]
            halves = []
            for hf in range(2):
                cols = slice(hf * 128, (hf + 1) * 128)
                dm = dmixed[rows, cols]
                dm_a, dm_b = jnp.where(lo, dm, 0.0), jnp.where(lo, 0.0, dm)
                vh = vn[rows, cols]
                g["sg_w"][2 * hf] += _mm_nt(dm_a, vh)
                g["sg_w"][2 * hf + 1] += _mm_nt(dm_b, vh)
                halves.append(_mm_tn(wm[2 * hf], dm_a) + _mm_tn(wm[2 * hf + 1], dm_b))
            dvn_chunks.append(jnp.concatenate(halves, axis=1))
        dvn = jnp.concatenate(dvn_chunks, axis=0)
        add_row("sg_ln_g", _rowsum(dvn * r["xhat_v"]))
        add_row("sg_ln_b", _rowsum(dvn))
        dvg = _ln_bwd(dvn, r["xhat_v"], r["rstd_v"], w["sg_ln_g"][...])
        put(C_V * D_G, dvg * _dgelu(r["v"], r["th_v"]))
        put(C_U * D_G, dug * _dgelu(r["u"], r["th_u"]))

        q = r["q"]
        dq = jnp.zeros((tile, D_G), F32)
        for h in range(N_SUB):
            p = r["probs"][h]
            dp = _mm_nt(dye, vm_ref[h])
            g["dv"][h] += _mm_tn(p, dye)
            ds = p * (dp - jnp.sum(dp * p, axis=-1, keepdims=True)) * ATT_SCALE
            dq = dq + _mm(ds, km_ref[h])
            g["dk"][h] += _mm_tn(ds, q)
        put(C_Q * D_G, dq)

        @pl.when(i == nt - 1)
        def _():
            for h in range(N_SUB):
                g["sg_w"][h] = jnp.where(r["tri"], g["sg_w"][h], 0.0)
            lane_head = _head_of_lane((CHUNK, D_G))
            col_of = lax.broadcasted_iota(jnp.int32, (CHUNK, 8), 1)
            ba = acc["sg_bias"][...]
            sgb_t = jnp.zeros((CHUNK, 8), F32)
            for h in range(N_SUB):
                col = jnp.sum(jnp.where(lane_head == h, ba, 0.0), axis=-1, keepdims=True)
                sgb_t = jnp.where(col_of == h, col, sgb_t)
            g["sg_b"][...] = sgb_t.T
            wbd = acc["pool_wbd"][...]
            for gi in range(N_SUB):
                sl = slice(gi * HEAD_DIM, (gi + 1) * HEAD_DIM)
                g["pool_w"][gi] = wbd[sl, sl]
            ca, dw = acc["conv_a"][...], acc["cc_dw_w"][...]
            for p in range(N_DEV):
                g["conv_a_w"][p] = ca[:, p * CONV_CH:(p + 1) * CONV_CH]
                g["cc_dw_w"][p] = dw[:, p * CONV_CH:(p + 1) * CONV_CH]
            gw_ref[...] = gw_acc[...].astype(gw_ref.dtype)
            if from_loss:
                total = jnp.sum(lacc[...], axis=-1, keepdims=True) * (0.5 / D_MODEL)
                slab_ref[LOSS_ROW:LOSS_ROW + 1, :] = jnp.broadcast_to(total, (1, D_MODEL))

    rev = lambda i: (nt - 1 - i, 0)
    fixed = lambda i: (0, 0)
    act_spec = pl.BlockSpec((tile, D_MODEL), rev)
    grad_specs = tuple(pl.BlockSpec(shape, lambda i, _nd=len(shape): (0,) * _nd) for _, shape in _BRANCH_GRADS)
    grad_shapes = tuple(jax.ShapeDtypeStruct(shape, F32) for _, shape in _BRANCH_GRADS)
    consts = [km, vm] + nat_arrays + [pw, ca8, dw8, repl["ln_g"], repl["ln_b"], w_out]
    outs = pl.pallas_call(
        body, grid=(nt,),
        in_specs=[act_spec, act_spec, pl.BlockSpec((tile, D_MIX), rev), pl.BlockSpec((tile, D_IN), rev),
                  pl.BlockSpec((HALO, D_IN), lambda i: (jnp.maximum((nt - 1 - i) * hb - 1, 0), 0)),
                  pl.BlockSpec((tile, 2 * D_G), rev)]
        + [_full_spec(a) for a in consts] + _tie_specs(ties),
        out_specs=(act_spec, pl.BlockSpec((tile, D_IN), rev), pl.BlockSpec((D_MIX, D_MODEL), fixed),
                   pl.BlockSpec((8, D_MODEL), fixed)) + grad_specs,
        out_shape=(jax.ShapeDtypeStruct((s, D_MODEL), F32), jax.ShapeDtypeStruct((s, D_IN), MM_DTYPE),
                   jax.ShapeDtypeStruct((D_MIX, D_MODEL), GRAD_DTYPE), jax.ShapeDtypeStruct((8, D_MODEL), F32))
        + grad_shapes,
        scratch_shapes=[pltpu.VMEM((HALO + tile, D_G), F32)] * 4 + [_residue_scratch(tile)] * 2
        + [pltpu.VMEM((D_MIX, D_MODEL), F32), pltpu.VMEM((1, D_MODEL), F32)]
        + [pltpu.VMEM(shape, F32) for _, shape in _BRANCH_ACC + _BRANCH_W_SCRATCH],
        name="layer_bwd_loss" if from_loss else "layer_bwd", compiler_params=_params("arbitrary"),
    )(target if from_loss else up, z, h, proj, proj, kept, *consts, *ties)
    return outs[0], outs[1], outs[2], outs[3], dict(zip([n for n, _ in _BRANCH_GRADS], outs[4:]))


def _dx_matmul(dproj, wt_in, dz, ties=(), tm=256):
    s = dproj.shape[0]

    def body(dp_ref, w_ref, dz_ref, *rest):
        o_ref = rest[len(ties)]
        o_ref[...] = _mm(dp_ref[...], w_ref[...]) + ALPHA * dz_ref[...]

    row = lambda i: (i, 0)
    return pl.pallas_call(
        body, grid=(s // tm,),
        in_specs=[pl.BlockSpec((tm, D_IN), row), _full_spec(wt_in), pl.BlockSpec((tm, D_MODEL), row)]
        + _tie_specs(ties),
        out_specs=pl.BlockSpec((tm, D_MODEL), row),
        out_shape=jax.ShapeDtypeStruct((s, D_MODEL), F32), name="dx_mm", compiler_params=_params("arbitrary"),
    )(dproj, wt_in, dz, *ties)


def _dw_in_matmul(x, dproj, ties=(), tk=1024):
    s = x.shape[0]
    tn = 2 * W_IN_COLS
    nk = s // tk

    def body(x_ref, dp_ref, *rest):
        o_ref, acc = rest[len(ties):]
        k = pl.program_id(1)

        @pl.when(k == 0)
        def _():
            acc[...] = jnp.zeros_like(acc)

        acc[...] += _mm_tn(dp_ref[...], x_ref[...])

        @pl.when(k == nk - 1)
        def _():
            o_ref[...] = acc[...].astype(o_ref.dtype)

    return pl.pallas_call(
        body, grid=(D_IN // tn, nk),
        in_specs=[pl.BlockSpec((tk, D_MODEL), lambda j, k: (k, 0)), pl.BlockSpec((tk, tn), lambda j, k: (k, j))]
        + _tie_specs(ties),
        out_specs=pl.BlockSpec((tn, D_MODEL), lambda j, k: (j, 0)),
        out_shape=jax.ShapeDtypeStruct((D_IN, D_MODEL), GRAD_DTYPE),
        scratch_shapes=[pltpu.VMEM((tn, D_MODEL), F32)], name="dw_in_mm",
        compiler_params=_params("arbitrary", "arbitrary"),
    )(x, dproj, *ties)


def _in_bwd(x, dproj, wt_in, dz, tm=512):
    s = x.shape[0]
    n_steps = s // tm

    assert wt_in.dtype == GRAD_DTYPE
    blk = 2 * W_IN_COLS

    def body(x_ref, dp_ref, w_hbm, dz_ref, o_ref, gw_hbm, w_vmem, acc, sem):
        i = pl.program_id(0)

        @pl.when(i == 0)
        def _():
            fetch = pltpu.make_async_copy(w_hbm, w_vmem, sem)
            fetch.start()
            acc[...] = jnp.zeros_like(acc)
            fetch.wait()

        o_ref[...] = _mm(dp_ref[...], w_vmem[...]) + ALPHA * dz_ref[...]
        xb = x_ref[...].astype(MM_DTYPE)
        for j in range(D_IN // blk):
            acc[j * blk:(j + 1) * blk, :] += _mm_tn(dp_ref[:, j * blk:(j + 1) * blk], xb)

        @pl.when(i == n_steps - 1)
        def _():
            w_vmem[...] = acc[...].astype(w_vmem.dtype)
            emit = pltpu.make_async_copy(w_vmem, gw_hbm, sem)
            emit.start()
            emit.wait()

    row = lambda i: (i, 0)
    any_spec = pl.BlockSpec(memory_space=pl.ANY)
    return pl.pallas_call(
        body, grid=(n_steps,),
        in_specs=[pl.BlockSpec((tm, D_MODEL), row), pl.BlockSpec((tm, D_IN), row), any_spec,
                  pl.BlockSpec((tm, D_MODEL), row)],
        out_specs=(pl.BlockSpec((tm, D_MODEL), row), any_spec),
        out_shape=(jax.ShapeDtypeStruct((s, D_MODEL), F32), jax.ShapeDtypeStruct((D_IN, D_MODEL), GRAD_DTYPE)),
        scratch_shapes=[pltpu.VMEM((D_IN, D_MODEL), wt_in.dtype), pltpu.VMEM((D_IN, D_MODEL), F32),
                        pltpu.SemaphoreType.DMA],
        name="in_bwd", compiler_params=_params("arbitrary"),
    )(x, dproj, wt_in, dz)


def _kv_bwd(mem, dk, dv):
    def body(mem_ref, dk_ref, dv_ref, o_ref):
        grp = _head_of_lane((MEM_LEN, D_G))
        dk_sum = jnp.zeros((MEM_LEN, D_G), F32)
        dv_sum = jnp.zeros((MEM_LEN, D_G), F32)
        for h in range(N_SUB):
            dk_sum = dk_sum + jnp.where(grp == h, dk_ref[h], 0.0)
            dv_sum = dv_sum + jnp.where(grp == h, dv_ref[h], 0.0)
        o_ref[...] = _mm_tn(mem_ref[...], jnp.concatenate([dk_sum, dv_sum], axis=1)).astype(o_ref.dtype)

    return pl.pallas_call(body, out_shape=jax.ShapeDtypeStruct((D_MODEL, 2 * D_G), GRAD_DTYPE), name="kv_bwd",
                          compiler_params=_params())(mem, dk, dv)


def _layer_fwd(x, mem, layer, repl, gw, proj=None, km_vm=None, ties=()):
    km, vm = _kv_project(mem, gw["w_kv"]) if km_vm is None else km_vm
    proj, h, z, kept, xn = _layer_fwd_fused(x, gw["wt_in"] if proj is None else None, proj, km, vm, layer, repl,
                                            gw["pw"], gw["ca8"], gw["dw8"], gw["w_out"], layer < DEPTH - 1, ties)
    return xn, (x, proj, h, z, km, vm, kept)


def _layer_bwd_a(up, target, mem, layer, repl, gw, saved, ties=()):
    x_in, proj, h, z, km, vm, kept = saved
    dz, dproj, g_w_out, g1024, bg = _layer_bwd_fused(up, target, z, h, proj, kept, km, vm, layer, repl, gw["w_out"],
                                                     gw["pw"], gw["ca8"], gw["dw8"], ties)
    grads = {n: bg[n] for n in ("g256", "sg_w", "sg_b", "pool_w", "conv_a_w", "cc_dw_w")}
    grads.update(w_out=g_w_out.reshape(N_DEV, D_MIX // N_DEV, D_MODEL), g1024=g1024,
                 w_kv=_kv_bwd(mem, bg["dk"], bg["dv"]).reshape(N_DEV, D_MODEL // N_DEV, 2 * D_G),
                 cc_pw_w=bg["cc_pw_w"].reshape(N_DEV, CONV_CH, D_G))
    return dz, dproj, grads


def _landing_shapes(items):
    out = []
    for a, scatter, pick in items:
        shape = a.shape if scatter else (N_DEV,) + (a.shape if pick is None else a.shape[1:])
        out.append(jax.ShapeDtypeStruct(shape, a.dtype))
    return tuple(out)


def _exchange_sems(n):
    return [pltpu.SemaphoreType.DMA(((N_DEV - 1) * n,)), pltpu.SemaphoreType.DMA(((N_DEV - 1) * n,)),
            pltpu.SemaphoreType.DMA((n,))]


def _exchange_copies(modes, ins, outs, send_sems, recv_sems, local_sems):
    n = len(ins)
    x, y, c = lax.axis_index("x"), lax.axis_index("y"), lax.axis_index("c")
    me = 4 * x + 2 * y + c

    def src_of(a, dest):
        scatter, pick = modes[a]
        if scatter:
            return ins[a].at[dest]
        return ins[a] if pick is None else ins[a].at[pick]

    local = [pltpu.make_async_copy(src_of(a, me), outs[a].at[me], local_sems.at[a]) for a in range(n)]
    sends, recvs = [], []
    for k in range(1, N_DEV):
        px = 1 - x if k & 4 else x
        py = 1 - y if k & 2 else y
        pc = 1 - c if k & 1 else c
        peer = 4 * px + 2 * py + pc
        for a in range(n):
            sems = dict(send_sem=send_sems.at[(k - 1) * n + a], recv_sem=recv_sems.at[(k - 1) * n + a],
                        device_id=(px, py, pc), device_id_type=pl.DeviceIdType.MESH)
            sends.append(pltpu.make_async_remote_copy(src_ref=src_of(a, peer), dst_ref=outs[a].at[me], **sems))
            recvs.append(pltpu.make_async_remote_copy(src_ref=src_of(a, peer), dst_ref=outs[a].at[peer], **sems))
    return local, sends, recvs


def _exchange(items, name):
    n = len(items)
    modes = [(scatter, pick) for _, scatter, pick in items]

    def body(*refs):
        local, sends, recvs = _exchange_copies(modes, refs[:n], refs[n:2 * n], *refs[2 * n:])
        for cp in local + sends:
            cp.start()
        for cp in recvs:
            cp.wait_recv()
        for cp in sends:
            cp.wait_send()
        for cp in local:
            cp.wait()

    any_spec = pl.BlockSpec(memory_space=pl.ANY)
    return pl.pallas_call(
        body, in_specs=[any_spec] * n, out_specs=(any_spec,) * n, out_shape=_landing_shapes(items),
        scratch_shapes=_exchange_sems(n), name=name,
    )(*[a for a, _, _ in items])


def _gather_two_level(items, name):
    n = len(items)
    assert not any(scatter for _, scatter, _ in items)
    picks = [pick for _, _, pick in items]

    def body(*refs):
        ins, outs = refs[:n], refs[n:2 * n]
        send_sems, recv_sems, local_sems = refs[2 * n:]
        x, y, c = lax.axis_index("x"), lax.axis_index("y"), lax.axis_index("c")
        sib = 1 - c
        chips = [(1 - x, y), (x, 1 - y), (1 - x, 1 - y)]

        def slot(a, px, py, pc):
            return outs[a].at[4 * px + 2 * py + pc]

        def copy(k, a, src, block, to):
            return pltpu.make_async_remote_copy(
                src_ref=src, dst_ref=slot(a, *block), send_sem=send_sems.at[k * n + a],
                recv_sem=recv_sems.at[k * n + a], device_id=to, device_id_type=pl.DeviceIdType.MESH)

        own = [ins[a] if picks[a] is None else ins[a].at[picks[a]] for a in range(n)]
        local = [pltpu.make_async_copy(own[a], slot(a, x, y, c), local_sems.at[a]) for a in range(n)]
        first = [copy(0, a, own[a], (x, y, c), (x, y, sib)) for a in range(n)]
        first += [copy(1 + j, a, own[a], (x, y, c), (*chip, c)) for j, chip in enumerate(chips) for a in range(n)]
        for cp in local + first:
            cp.start()
        passed = []
        for j, chip in enumerate(chips):
            for a in range(n):
                copy(1 + j, a, own[a], (*chip, c), (x, y, c)).wait_recv()
                fwd = copy(4 + j, a, slot(a, *chip, c), (*chip, c), (x, y, sib))
                fwd.start()
                passed.append(fwd)
        for a in range(n):
            copy(0, a, own[a], (x, y, sib), (x, y, c)).wait_recv()
        for j, chip in enumerate(chips):
            for a in range(n):
                copy(4 + j, a, own[a], (*chip, sib), (x, y, c)).wait_recv()
        for cp in first + passed:
            cp.wait_send()
        for cp in local:
            cp.wait()

    any_spec = pl.BlockSpec(memory_space=pl.ANY)
    return pl.pallas_call(
        body, in_specs=[any_spec] * n, out_specs=(any_spec,) * n, out_shape=_landing_shapes(items),
        scratch_shapes=[pltpu.SemaphoreType.DMA((7 * n,)), pltpu.SemaphoreType.DMA((7 * n,)),
                        pltpu.SemaphoreType.DMA((n,))],
        name=name,
    )(*[a for a, _, _ in items])


_HBM_SPEC = pl.BlockSpec(memory_space=pltpu.HBM)
_SEM_SPEC = pl.BlockSpec(memory_space=pltpu.SEMAPHORE)
_SPLIT_PARAMS = pltpu.CompilerParams(has_side_effects=pltpu.SideEffectType.DATAFLOW_SIDE_EFFECTING)


def _split_start(srcs, lands, plan, sem_shapes, name):
    n_src, n_land = len(srcs), len(lands)
    n_buf = n_src + n_land
    bufs = [pltpu.with_memory_space_constraint(a, pltpu.HBM) for a in list(srcs) + list(lands)]

    def body(*refs):
        local, sends, _ = plan(refs[:n_src], refs[n_src:n_buf], *refs[n_buf:n_buf + 3])
        for cp in local + sends:
            cp.start()
        token = refs[-1]
        token[...] = jnp.zeros_like(token)

    res = pl.pallas_call(
        body, name=name, in_specs=[_HBM_SPEC] * n_buf,
        out_shape=tuple(sem_shapes) + tuple(pltpu.HBM(a.shape, a.dtype) for a in bufs)
        + (jax.ShapeDtypeStruct((8, 128), F32),),
        out_specs=(_SEM_SPEC,) * 3 + (_HBM_SPEC,) * n_buf + (pl.BlockSpec(memory_space=pltpu.VMEM),),
        input_output_aliases={i: 3 + i for i in range(n_buf)}, compiler_params=_SPLIT_PARAMS,
    )(*bufs)
    return dict(sems=res[:3], srcs=res[3:3 + n_src], lands=res[3 + n_src:3 + n_buf], token=res[-1], plan=plan)


def _split_wait(ticket, after, name):
    n_src, n_land = len(ticket["srcs"]), len(ticket["lands"])
    n_buf = n_src + n_land
    plan = ticket["plan"]

    def body(*refs):
        local, sends, recvs = plan(refs[:n_src], refs[n_src:n_buf], *refs[n_buf:n_buf + 3])
        for cp in recvs:
            cp.wait_recv()
        for cp in sends:
            cp.wait_send()
        for cp in local:
            cp.wait()

    bufs = list(ticket["srcs"]) + list(ticket["lands"])
    res = pl.pallas_call(
        body, name=name, in_specs=[_HBM_SPEC] * n_buf + [_SEM_SPEC] * 3 + [pl.BlockSpec(memory_space=pl.ANY)],
        out_shape=tuple(pltpu.HBM(a.shape, a.dtype) for a in bufs), out_specs=(_HBM_SPEC,) * n_buf,
        input_output_aliases={i: i for i in range(n_buf)}, compiler_params=_SPLIT_PARAMS,
    )(*bufs, *ticket["sems"], after)
    return res[n_src:]


def _empty_landings(items):
    return [lax.empty(s.shape, s.dtype) for s in _landing_shapes(items)]


def _exchange_start(items, name):
    modes = [(scatter, pick) for _, scatter, pick in items]
    plan = lambda ins, outs, *sems: _exchange_copies(modes, ins, outs, *sems)
    return _split_start([a for a, _, _ in items], _empty_landings(items), plan, _exchange_sems(len(items)), name)


def _two_level_plans(picks):
    n = len(picks)

    def place():
        x, y, c = lax.axis_index("x"), lax.axis_index("y"), lax.axis_index("c")
        return x, y, c, 1 - c, [(1 - x, y), (x, 1 - y), (1 - x, 1 - y)]

    def copy(outs, send_sems, recv_sems, k, a, src, block, to):
        px, py, pc = block
        return pltpu.make_async_remote_copy(
            src_ref=src, dst_ref=outs[a].at[4 * px + 2 * py + pc], send_sem=send_sems.at[k * n + a],
            recv_sem=recv_sems.at[k * n + a], device_id=to, device_id_type=pl.DeviceIdType.MESH)

    def between_chips(ins, outs, send_sems, recv_sems, local_sems):
        x, y, c, sib, chips = place()
        own = [ins[a] if picks[a] is None else ins[a].at[picks[a]] for a in range(n)]
        mk = lambda *args: copy(outs, send_sems, recv_sems, *args)
        local = [pltpu.make_async_copy(own[a], outs[a].at[4 * x + 2 * y + c], local_sems.at[a]) for a in range(n)]
        sends = [mk(0, a, own[a], (x, y, c), (x, y, sib)) for a in range(n)]
        sends += [mk(1 + j, a, own[a], (x, y, c), (*chip, c)) for j, chip in enumerate(chips) for a in range(n)]
        recvs = [mk(0, a, own[a], (x, y, sib), (x, y, c)) for a in range(n)]
        recvs += [mk(1 + j, a, own[a], (*chip, c), (x, y, c)) for j, chip in enumerate(chips) for a in range(n)]
        return local, sends, recvs

    def within_chip(ins, outs, send_sems, recv_sems, local_sems):
        x, y, c, sib, chips = place()
        mk = lambda *args: copy(outs, send_sems, recv_sems, *args)
        slot = lambda a, px, py, pc: outs[a].at[4 * px + 2 * py + pc]
        sends = [mk(j, a, slot(a, *chip, c), (*chip, c), (x, y, sib)) for j, chip in enumerate(chips)
                 for a in range(n)]
        recvs = [mk(j, a, slot(a, *chip, c), (*chip, sib), (x, y, c)) for j, chip in enumerate(chips)
                 for a in range(n)]
        return [], sends, recvs

    sems = lambda k: [pltpu.SemaphoreType.DMA((k * n,)), pltpu.SemaphoreType.DMA((k * n,)),
                      pltpu.SemaphoreType.DMA((n,))]
    return between_chips, sems(4), within_chip, sems(3)


def _adam_math(g, w, m, v):
    m_new = ADAM_B1 * m + (1.0 - ADAM_B1) * g
    v_new = ADAM_B2 * v + (1.0 - ADAM_B2) * (g * g)
    m_hat = m_new / (1.0 - ADAM_B1 ** ADAM_STEP)
    v_hat = v_new / (1.0 - ADAM_B2 ** ADAM_STEP)
    return -ADAM_LR * (m_hat / (jnp.sqrt(v_hat) + ADAM_EPS) + ADAM_WD * w), m_new, v_new


def _adamw_big(parts, w, m, v, layer, prev, name, tr):
    depth, rows, cols = w.shape

    def body(p_ref, w_ref, m_ref, v_ref, *rest):
        g_out, d_out, m_out, v_out = rest[len(prev):]
        g = p_ref[0].astype(F32)
        for q in range(1, N_DEV):
            g = g + p_ref[q].astype(F32)
        d, m_new, v_new = _adam_math(g, w_ref[...], m_ref[...], v_ref[...])
        g_out[...] = g
        d_out[...] = d
        m_out[...] = m_new
        v_out[...] = v_new

    blk = pl.BlockSpec((None, tr, cols), lambda i: (layer, i, 0))
    shp = jax.ShapeDtypeStruct((depth, rows, cols), F32)
    return pl.pallas_call(
        body, grid=(rows // tr,),
        in_specs=[pl.BlockSpec((N_DEV, tr, cols), lambda i: (0, i, 0)), blk, blk, blk]
        + [pl.BlockSpec(memory_space=pl.ANY)] * len(prev),
        out_specs=(blk,) * 4, out_shape=(shp,) * 4,
        input_output_aliases={4 + j: j for j in range(len(prev))},
        name=name, compiler_params=_params("arbitrary"),
    )(parts, w, m, v, *prev)


_SMALL_TENSORS = (("conv_a_w", "conv_a_w", None), ("cc_dw_w", "cc_dw_w", None), ("cc_pw_w", "cc_pw_w", None),
                  ("sg_w", "sg_w", None), ("pool_w", "pool_w", None), ("sg_b", "sg_b", None)) \
    + tuple((n, "g256", k) for k, n in enumerate(G256_ROWS)) + tuple((n, "g1024", k) for k, n in enumerate(G1024_ROWS))
_SMALL_LANDINGS = ("conv_a_w", "cc_dw_w", "cc_pw_w", "sg_w", "pool_w", "sg_b", "g256", "g1024")
_TAPS_FIRST = ("conv_a_w", "cc_dw_w")


def _adamw_small(landings, wts, mom, var):
    names = [n for n, _, _ in _SMALL_TENSORS]
    n_land = DEPTH * len(_SMALL_LANDINGS)
    n_t = len(names)

    def body(*refs):
        land = [dict(zip(_SMALL_LANDINGS, refs[l * len(_SMALL_LANDINGS):(l + 1) * len(_SMALL_LANDINGS)]))
                for l in range(DEPTH)]
        w_refs = dict(zip(names, refs[n_land:n_land + n_t]))
        m_refs = dict(zip(names, refs[n_land + n_t:n_land + 2 * n_t]))
        v_refs = dict(zip(names, refs[n_land + 2 * n_t:n_land + 3 * n_t]))
        outs = refs[n_land + 3 * n_t:]
        out_refs = {n: outs[4 * k:4 * k + 4] for k, n in enumerate(names)}
        loss_ref = outs[4 * n_t]
        for name, key, row in _SMALL_TENSORS:
            for l in range(DEPTH):
                src = land[l][key]
                if row is not None:
                    part = lambda q: src[q, row:row + 1, :]
                    at = (slice(l, l + 1),)
                elif name == "sg_b":
                    part = lambda q: src[q, 0:N_SUB, :]
                    at = (l,)
                elif name in _TAPS_FIRST:
                    part = lambda q: src[q]
                    at = (slice(None), l)
                else:
                    part = lambda q: src[q]
                    at = (l,)
                g = part(0)
                for q in range(1, N_DEV):
                    g = g + part(q)
                d, m_new, v_new = _adam_math(g, w_refs[name][at], m_refs[name][at], v_refs[name][at])
                for ref, val in zip(out_refs[name], (g, d, m_new, v_new)):
                    ref[at] = val
        src = land[DEPTH - 1]["g1024"]
        loss = src[0, LOSS_ROW:LOSS_ROW + 1, 0:128]
        for q in range(1, N_DEV):
            loss = loss + src[q, LOSS_ROW:LOSS_ROW + 1, 0:128]
        loss_ref[...] = loss

    ins = [landings[l][k] for l in range(DEPTH) for k in _SMALL_LANDINGS] \
        + [src[n] for src in (wts, mom, var) for n in names]
    out_shape = tuple(jax.ShapeDtypeStruct(wts[n].shape, F32) for n in names for _ in range(4)) \
        + (jax.ShapeDtypeStruct((1, 128), F32),)
    res = pl.pallas_call(body, out_shape=out_shape, name="adamw_small", compiler_params=_params())(*ins)
    return {n: res[4 * k:4 * k + 4] for k, n in enumerate(names)}, res[4 * n_t]


_BIG = (("w_in", 64), ("w_out", 32), ("w_kv", 32))
_GRAD_ITEMS_EARLY = ("w_out", "w_kv", "cc_pw_w", "conv_a_w", "cc_dw_w")
_GRAD_ITEMS_REPL = ("g256", "sg_w", "sg_b", "pool_w", "g1024")


def _grad_items(grads, with_w_in):
    items = [(grads[n], True, None) for n in (("w_in",) if with_w_in else ()) + _GRAD_ITEMS_EARLY]
    return items + [(grads[n], False, None) for n in _GRAD_ITEMS_REPL]


def _landed(parts, with_w_in):
    names = (("w_in",) if with_w_in else ()) + _GRAD_ITEMS_EARLY + _GRAD_ITEMS_REPL
    return dict(zip(names, parts))


def _gathered_weights(wt_in8, w_kv8, w_out8, pw8, ca8, dw8):
    return dict(wt_in=wt_in8.reshape(D_IN, D_MODEL), w_kv=w_kv8.reshape(D_MODEL, 2 * D_G),
                w_out=w_out8.reshape(D_MIX, D_MODEL), pw=pw8.reshape(D_G, D_G), ca8=ca8, dw8=dw8)


def kernel(x, mem, w_in, conv_a_w, sg_ln_g, sg_ln_b, sg_w, sg_b, pool_w, pool_scale, cc_dw_w, cc_dw_b, cc_ln_g, cc_ln_b, cc_pw_w, w_kv, w_out, ln_g, ln_b, loss_target, m_w_in, m_conv_a_w, m_sg_ln_g, m_sg_ln_b, m_sg_w, m_sg_b, m_pool_w, m_pool_scale, m_cc_dw_w, m_cc_dw_b, m_cc_ln_g, m_cc_ln_b, m_cc_pw_w, m_w_kv, m_w_out, m_ln_g, m_ln_b, v_w_in, v_conv_a_w, v_sg_ln_g, v_sg_ln_b, v_sg_w, v_sg_b, v_pool_w, v_pool_scale, v_cc_dw_w, v_cc_dw_b, v_cc_ln_g, v_cc_ln_b, v_cc_pw_w, v_w_kv, v_w_out, v_ln_g, v_ln_b):
    names = ("w_in", "conv_a_w", "sg_ln_g", "sg_ln_b", "sg_w", "sg_b", "pool_w", "pool_scale", "cc_dw_w", "cc_dw_b",
             "cc_ln_g", "cc_ln_b", "cc_pw_w", "w_kv", "w_out", "ln_g", "ln_b")
    wts = dict(zip(names, (w_in, conv_a_w, sg_ln_g, sg_ln_b, sg_w, sg_b, pool_w, pool_scale, cc_dw_w, cc_dw_b,
                           cc_ln_g, cc_ln_b, cc_pw_w, w_kv, w_out, ln_g, ln_b)))
    mom = dict(zip(names, (m_w_in, m_conv_a_w, m_sg_ln_g, m_sg_ln_b, m_sg_w, m_sg_b, m_pool_w, m_pool_scale,
                           m_cc_dw_w, m_cc_dw_b, m_cc_ln_g, m_cc_ln_b, m_cc_pw_w, m_w_kv, m_w_out, m_ln_g, m_ln_b)))
    var = dict(zip(names, (v_w_in, v_conv_a_w, v_sg_ln_g, v_sg_ln_b, v_sg_w, v_sg_b, v_pool_w, v_pool_scale,
                           v_cc_dw_w, v_cc_dw_b, v_cc_ln_g, v_cc_ln_b, v_cc_pw_w, v_w_kv, v_w_out, v_ln_g, v_ln_b)))
    repl = wts
    xs, mems, tgt = x[0], mem[0], loss_target[0]
    turned = {"w_in": (0, 2, 1), "conv_a_w": (1, 0, 2), "cc_dw_w": (1, 0, 2)}
    wts, mom, var = [{n: (jnp.transpose(a, turned[n]) if n in turned else a) for n, a in src.items()}
                     for src in (wts, mom, var)]
    wb = {n: wts[n].astype(MM_DTYPE) for n in ("w_in", "w_kv", "w_out", "cc_pw_w")}

    wt8_0, wkv8_0 = _gather_two_level([(wb["w_in"], False, 0), (wb["w_kv"], False, 0)], "gather_weights_0a")
    rest_0 = _exchange_start([(wb["w_out"], False, 0), (wb["cc_pw_w"], False, 0), (wts["conv_a_w"], False, None),
                              (wts["cc_dw_w"], False, None)], "gather_weights_0b_start")
    km_vm0 = _kv_project(mems, wkv8_0.reshape(D_MODEL, 2 * D_G))
    proj0 = _proj_matmul(xs, wt8_0.reshape(D_IN, D_MODEL), (rest_0["token"],))
    wo8_0, pw8_0, ca8, dw8 = _split_wait(rest_0, proj0, "gather_weights_0b_wait")
    gw0 = _gathered_weights(wt8_0, wkv8_0, wo8_0, pw8_0, ca8, dw8)
    items_1 = [(wb[n], False, 1) for n in ("w_in", "w_kv", "w_out", "cc_pw_w")]
    between_chips, sems_a, within_chip, sems_b = _two_level_plans([1] * len(items_1))
    chips_1 = _split_start([a for a, _, _ in items_1], _empty_landings(items_1), between_chips, sems_a,
                           "gather_weights_1a_start")
    x1, saved0 = _layer_fwd(xs, mems, 0, repl, gw0, proj0, km_vm0, (chips_1["token"],))
    core_1 = _split_start([], _split_wait(chips_1, x1, "gather_weights_1a_wait"), within_chip, sems_b,
                          "gather_weights_1b_start")
    gw1 = _gathered_weights(*_split_wait(core_1, core_1["token"], "gather_weights_1b_wait"), ca8, dw8)
    _, saved1 = _layer_fwd(x1, mems, 1, repl, gw1)

    dz1, dproj1, g1 = _layer_bwd_a(None, tgt, mems, 1, repl, gw1, saved1)
    shards = lambda g: g.reshape(N_DEV, W_IN_COLS, D_MODEL)
    up, g_wt_in_1 = _in_bwd(saved1[0], dproj1, gw1["wt_in"], dz1)
    g1["w_in"] = shards(g_wt_in_1)
    grads_1 = _exchange_start(_grad_items(g1, True), "exchange_grads_1_start")
    dz0, dproj0, g0 = _layer_bwd_a(up, None, mems, 0, repl, gw0, saved0, (grads_1["token"],))
    early_0 = _exchange_start(_grad_items(g0, False), "exchange_grads_0a_start")
    g_wt_in_0 = _dw_in_matmul(saved0[0], dproj0, (early_0["token"],))
    late_0 = _exchange_start([(shards(g_wt_in_0), True, None)], "exchange_grads_0b_start")
    grad_x = _dx_matmul(dproj0, gw0["wt_in"], dz0, (late_0["token"],))

    landed = [None, _landed(_split_wait(grads_1, grad_x, "exchange_grads_1_wait"), True)]
    big = {}
    for n, tr in _BIG:
        big[n] = _adamw_big(landed[1][n], wts[n], mom[n], var[n], 1, (), "adamw_" + n + "_1", tr)
    landed[0] = _landed(_split_wait(early_0, big["w_kv"][0], "exchange_grads_0a_wait"), False)
    for n, tr in _BIG[1:]:
        big[n] = _adamw_big(landed[0][n], wts[n], mom[n], var[n], 0, big[n], "adamw_" + n + "_0", tr)
    small, loss = _adamw_small(landed, wts, mom, var)
    (landed[0]["w_in"],) = _split_wait(late_0, loss, "exchange_grads_0b_wait")
    big["w_in"] = _adamw_big(landed[0]["w_in"], wts["w_in"], mom["w_in"], var["w_in"], 0, big["w_in"],
                             "adamw_w_in_0", _BIG[0][1])

    res = {**small, **big}
    res = {n: ([jnp.transpose(a, turned[n]) for a in r] if n in turned else r) for n, r in res.items()}
    return (loss[0, 0], grad_x[None], *[res[n][0] for n in names], *[res[n][1] for n in names],
            *[res[n][2] for n in names], *[res[n][3] for n in names])
```

```python
import math

import jax
import jax.numpy as jnp
from jax import lax
from jax.experimental import pallas as pl
from jax.experimental.pallas import tpu as pltpu

F32 = jnp.float32
MM_DTYPE = jnp.bfloat16
GRAD_DTYPE = jnp.bfloat16

D_MODEL = 1024
DEPTH = 2
D_G = 256
N_GROUPS = 5
D_MIX = N_GROUPS * D_G
N_SUB = 4
HEAD_DIM = D_G // N_SUB
CONV_A = 3
CONV_D = 31
CHUNK = 128
POOL_WINDOWS = (2, 4, 8, 16)
MEM_LEN = 256
LN_EPS = 1e-5
ALPHA = (2.0 * DEPTH) ** 0.25
D_IN = 9 * D_G + D_MIX
ATT_SCALE = 1.0 / math.sqrt(HEAD_DIM)

ADAM_LR = 0.001
ADAM_B1 = 0.9
ADAM_B2 = 0.999
ADAM_EPS = 1e-08
ADAM_WD = 0.01
ADAM_STEP = 10

N_DEV = 8
W_IN_COLS = D_IN // N_DEV
CONV_CH = D_G // N_DEV
HALO = 32
TILE = 256
VMEM_LIMIT = 56 * 1024 * 1024

C_XA, C_BA, C_CA, C_U, C_V, C_XC, C_DA, C_DG, C_Q = range(9)
C_GATE = 9 * D_G

G256_ROWS = ("sg_ln_g", "sg_ln_b", "pool_scale", "cc_dw_b", "cc_ln_g", "cc_ln_b")
G1024_ROWS = ("ln_g", "ln_b")
LOSS_ROW = 2


def _mm(a, b):
    return jnp.dot(a.astype(MM_DTYPE), b.astype(MM_DTYPE), preferred_element_type=F32)


def _mm_nt(a, b):
    return lax.dot_general(a.astype(MM_DTYPE), b.astype(MM_DTYPE), (((1,), (1,)), ((), ())),
                           preferred_element_type=F32)


def _mm_tn(a, b):
    return lax.dot_general(a.astype(MM_DTYPE), b.astype(MM_DTYPE), (((0,), (0,)), ((), ())),
                           preferred_element_type=F32)


def _sigmoid(x):
    return 0.5 * jnp.tanh(0.5 * x) + 0.5


_GELU_C = math.sqrt(2.0 / math.pi)
_GELU_A = 0.044715


def _gelu(x):
    th = jnp.tanh(_GELU_C * (x + _GELU_A * (x * x * x)))
    return 0.5 * x * (1.0 + th), th


def _dgelu(x, th):
    return 0.5 * (1.0 + th) + 0.5 * x * (1.0 - th * th) * (_GELU_C * (1.0 + 3.0 * _GELU_A * (x * x)))


def _ln_fwd(x, g, b):
    mu = jnp.mean(x, axis=-1, keepdims=True)
    xc = x - mu
    var = jnp.mean(xc * xc, axis=-1, keepdims=True)
    rstd = lax.rsqrt(var + LN_EPS)
    xhat = xc * rstd
    return xhat * g + b, xhat, rstd


def _ln_bwd(dy, xhat, rstd, g):
    dxhat = dy * g
    m1 = jnp.mean(dxhat, axis=-1, keepdims=True)
    m2 = jnp.mean(dxhat * xhat, axis=-1, keepdims=True)
    return rstd * (dxhat - m1 - xhat * m2)


def _rowsum(x):
    return jnp.sum(x, axis=0, keepdims=True)


def _col(ref, k):
    return ref[:, k * D_G:(k + 1) * D_G]


def _head_of_lane(shape):
    return jnp.right_shift(lax.broadcasted_iota(jnp.int32, shape, len(shape) - 1), HEAD_DIM.bit_length() - 1)


def _pool_select(lane_grp, s2, s4, s8, s16):
    return jnp.where(lane_grp == 0, s2, jnp.where(lane_grp == 1, s4, jnp.where(lane_grp == 2, s8, s16)))


def _row_view(ref, layer):
    return ref.at[pl.ds(layer, 1)]


def _make_residues(ext_ref, res_ref):
    rows = res_ref.shape[1]
    for r in range(1, 8):
        res_ref[r - 1] = ext_ref[pl.ds(r, rows), :]


def _rows_at(ext_ref, res_ref, off, tile):
    a, r = divmod(off, 8)
    if r == 0:
        return ext_ref[pl.ds(off, tile), :]
    return res_ref[r - 1, pl.ds(8 * a, tile), :]


def _residue_scratch(tile):
    return pltpu.VMEM((7, HALO + tile - 8, D_G), F32)


def _branch_forward(p_ref, ph_ref, first, row0, km_ref, vm_ref, w, ext_a, ext_c, ext_d, res_c, res_d, tile,
                    kept_ref=None):
    r = {}
    xa, ba, ca = _col(p_ref, C_XA), _col(p_ref, C_BA), _col(p_ref, C_CA)
    g_a = ca * xa
    ext_a[0:HALO] = jnp.where(first, 0.0, _col(ph_ref, C_CA) * _col(ph_ref, C_XA))
    ext_a[HALO:HALO + tile] = g_a
    conv_a = w["conv_a"][0:1, :] * ext_a[pl.ds(HALO - 2, tile), :]
    for k in range(1, CONV_A):
        conv_a = conv_a + w["conv_a"][k:k + 1, :] * ext_a[pl.ds(HALO - 2 + k, tile), :]
    r.update(xa=xa, ba=ba, ca=ca, g_a=g_a, conv_a=conv_a)
    ya = ba * conv_a

    lane_grp = _head_of_lane((tile, D_G))
    trow = row0 + lax.broadcasted_iota(jnp.int32, (tile, D_G), 0)
    win = _pool_select(lane_grp, 2, 4, 8, 16)
    inv_cnt = 1.0 / jnp.minimum(trow + 1, win).astype(F32)
    if kept_ref is None:
        xc = _col(p_ref, C_XC)
        ext_c[0:HALO] = jnp.where(first, 0.0, _col(ph_ref, C_XC))
        ext_c[HALO:HALO + tile] = xc
        _make_residues(ext_c, res_c)
        acc = xc
        sums = {}
        for k in range(1, POOL_WINDOWS[-1]):
            acc = acc + _rows_at(ext_c, res_c, HALO - k, tile)
            if k + 1 in POOL_WINDOWS:
                sums[k + 1] = acc
        ypre = _pool_select(lane_grp, sums[2], sums[4], sums[8], sums[16]) * inv_cnt - xc
    else:
        ypre = kept_ref[:, D_G:2 * D_G]
    pool_mm = _mm(ypre, w["pool_wbd"][...])
    yc = pool_mm * w["pool_scale"][...]
    r.update(lane_grp=lane_grp, inv_cnt=inv_cnt, ypre=ypre, pool_mm=pool_mm)

    da, dg = _col(p_ref, C_DA), _col(p_ref, C_DG)
    sig_dg = _sigmoid(dg)
    hd = da * sig_dg
    if kept_ref is None:
        ext_d[0:HALO] = jnp.where(first, 0.0, _col(ph_ref, C_DA) * _sigmoid(_col(ph_ref, C_DG)))
        ext_d[HALO:HALO + tile] = hd
        _make_residues(ext_d, res_d)
        conv_d = w["cc_dw_b"][...] + w["cc_dw_w"][0:1, :] * _rows_at(ext_d, res_d, HALO - (CONV_D - 1), tile)
        for j in range(1, CONV_D):
            conv_d = conv_d + w["cc_dw_w"][j:j + 1, :] * _rows_at(ext_d, res_d, HALO - (CONV_D - 1) + j, tile)
    else:
        conv_d = kept_ref[:, 0:D_G]
    r["kept"] = (conv_d, ypre)
    ln_d, xhat_d, rstd_d = _ln_fwd(conv_d, w["cc_ln_g"][...], w["cc_ln_b"][...])
    sig_ln = _sigmoid(ln_d)
    act_d = ln_d * sig_ln
    yd = _mm(act_d, w["cc_pw_w"][...])
    r.update(da=da, sig_dg=sig_dg, hd=hd, ln_d=ln_d, xhat_d=xhat_d, rstd_d=rstd_d, sig_ln=sig_ln, act_d=act_d)

    u, v = _col(p_ref, C_U), _col(p_ref, C_V)
    ug, th_u = _gelu(u)
    vg, th_v = _gelu(v)
    vn, xhat_v, rstd_v = _ln_fwd(vg, w["sg_ln_g"][...], w["sg_ln_b"][...])
    tri = (lax.broadcasted_iota(jnp.int32, (CHUNK, CHUNK), 0)
           >= lax.broadcasted_iota(jnp.int32, (CHUNK, CHUNK), 1))
    wm = [jnp.where(tri, w["sg_w"][h], 0.0).astype(MM_DTYPE) for h in range(N_SUB)]
    lo = lax.broadcasted_iota(jnp.int32, (CHUNK, 2 * HEAD_DIM), 1) < HEAD_DIM
    chunks = []
    for c in range(tile // CHUNK):
        halves = []
        for hf in range(2):
            vh = vn[c * CHUNK:(c + 1) * CHUNK, hf * 128:(hf + 1) * 128]
            halves.append(_mm(wm[2 * hf], jnp.where(lo, vh, 0.0)) + _mm(wm[2 * hf + 1], jnp.where(lo, 0.0, vh)))
        chunks.append(jnp.concatenate(halves, axis=1) + w["sg_bias"][...])
    mixed = jnp.concatenate(chunks, axis=0)
    yb = ug * mixed
    r.update(u=u, v=v, ug=ug, th_u=th_u, th_v=th_v, vn=vn, xhat_v=xhat_v, rstd_v=rstd_v, wm=wm, lo=lo,
             mixed=mixed, tri=tri)

    q = _col(p_ref, C_Q)
    ye = jnp.zeros((tile, D_G), F32)
    probs = []
    for h in range(N_SUB):
        s = _mm_nt(q, km_ref[h]) * ATT_SCALE
        e = jnp.exp(s - jnp.max(s, axis=-1, keepdims=True))
        p = e * (1.0 / jnp.sum(e, axis=-1, keepdims=True))
        probs.append(p)
        ye = ye + _mm(p, vm_ref[h])
    r.update(q=q, probs=probs)

    gate = p_ref[:, C_GATE:C_GATE + D_MIX]
    sig_gate = _sigmoid(gate)
    concat = jnp.concatenate([ya, yb, yc, yd, ye], axis=1)
    r.update(gate=gate, sig_gate=sig_gate, concat=concat)
    return r


_BRANCH_REPL = ("sg_ln_g", "sg_ln_b", "sg_w", "sg_b", "pool_w", "pool_scale", "cc_dw_b", "cc_ln_g", "cc_ln_b")
_BRANCH_W_SCRATCH = (("conv_a", (CONV_A, D_G)), ("cc_dw_w", (CONV_D, D_G)), ("sg_bias", (CHUNK, D_G)),
                     ("pool_wbd", (D_G, D_G)), ("sgb8", (8, CHUNK)))


def _branch_weights(layer, nat, pw_ref, ca_ref, dw_ref, scr, init):
    @pl.when(init)
    def _():
        for p in range(N_DEV):
            scr["conv_a"][:, p * CONV_CH:(p + 1) * CONV_CH] = ca_ref[p, :, layer, :]
            scr["cc_dw_w"][:, p * CONV_CH:(p + 1) * CONV_CH] = dw_ref[p, :, layer, :]
        scr["sgb8"][...] = jnp.zeros((8, CHUNK), F32)
        scr["sgb8"][0:N_SUB] = nat["sg_b"][layer]
        sgb_t = scr["sgb8"][...].T
        head = _head_of_lane((CHUNK, D_G))
        bias = jnp.zeros((CHUNK, D_G), F32)
        for h in range(N_SUB):
            bias = jnp.where(head == h, sgb_t[:, h:h + 1], bias)
        scr["sg_bias"][...] = bias
        scr["pool_wbd"][...] = jnp.zeros((D_G, D_G), F32)
        for gi in range(N_SUB):
            sl = slice(gi * HEAD_DIM, (gi + 1) * HEAD_DIM)
            scr["pool_wbd"][sl, sl] = nat["pool_w"][layer, gi]

    w = {n: _row_view(nat[n], layer) for n in ("sg_ln_g", "sg_ln_b", "pool_scale", "cc_dw_b", "cc_ln_g", "cc_ln_b")}
    w.update(conv_a=scr["conv_a"], cc_dw_w=scr["cc_dw_w"], sg_bias=scr["sg_bias"], pool_wbd=scr["pool_wbd"],
             sg_w=nat["sg_w"].at[layer], cc_pw_w=pw_ref)
    return w


def _full_spec(a):
    nd = a.ndim
    return pl.BlockSpec(a.shape, lambda *_, _nd=nd: (0,) * _nd)


def _tie_specs(ties):
    return [pl.BlockSpec((8, 128), lambda *_: (0, 0)) for _ in ties]


def _params(*sem):
    return pltpu.CompilerParams(dimension_semantics=sem or None, vmem_limit_bytes=VMEM_LIMIT)


def _proj_matmul(x, wt_in, ties=(), tm=512):
    s, k = x.shape

    def body(x_ref, w_ref, *rest):
        o_ref = rest[len(ties)]
        o_ref[...] = _mm_nt(x_ref[...], w_ref[...])

    return pl.pallas_call(
        body, grid=(s // tm,),
        in_specs=[pl.BlockSpec((tm, k), lambda i: (i, 0)), _full_spec(wt_in)] + _tie_specs(ties),
        out_specs=pl.BlockSpec((tm, D_IN), lambda i: (i, 0)),
        out_shape=jax.ShapeDtypeStruct((s, D_IN), F32), name="proj_mm", compiler_params=_params("arbitrary"),
    )(x, wt_in, *ties)


def _kv_project(mem, w_kv):
    def body(mem_ref, w_ref, km_ref, vm_ref):
        kv = _mm(mem_ref[...], w_ref[...])
        k, v = kv[:, :D_G], kv[:, D_G:]
        grp = _head_of_lane((MEM_LEN, D_G))
        for h in range(N_SUB):
            km_ref[h] = jnp.where(grp == h, k, 0.0).astype(km_ref.dtype)
            vm_ref[h] = jnp.where(grp == h, v, 0.0).astype(vm_ref.dtype)

    shp = jax.ShapeDtypeStruct((N_SUB, MEM_LEN, D_G), MM_DTYPE)
    return pl.pallas_call(body, out_shape=(shp, shp), name="kv_project", compiler_params=_params())(mem, w_kv)


def _layer_fwd_fused(x, wt_in, proj, km, vm, layer, repl, pw, ca8, dw8, w_out, want_xn, ties=(), tile=TILE):
    s = x.shape[0]
    hb = tile // HALO
    nat_arrays = [repl[n] for n in _BRANCH_REPL]
    n_nat, nt = len(nat_arrays), len(ties)
    given = proj is not None

    def body(x_ref, *rest):
        if given:
            p_ref, ph_ref = rest[:2]
            rest = rest[2:]
        else:
            wt_ref = rest[0]
            rest = rest[1:]
        km_ref, vm_ref = rest[:2]
        nat = dict(zip(_BRANCH_REPL, rest[2:2 + n_nat]))
        pw_ref, ca_ref, dw_ref, wo_ref, g_ref, b_ref = rest[2 + n_nat:8 + n_nat]
        rest = rest[8 + n_nat + nt:]
        if not given:
            p_ref, rest = rest[0], rest[1:]
        h_ref, z_ref, cd_ref = rest[:3]
        rest = rest[3:]
        if want_xn:
            xn_ref, rest = rest[0], rest[1:]
        if not given:
            ph_ref, rest = rest[0], rest[1:]
        ext_a, ext_c, ext_d, res_c, res_d = rest[:5]
        scr = dict(zip([n for n, _ in _BRANCH_W_SCRATCH], rest[5:]))
        i = pl.program_id(0)
        xt = x_ref[...]
        if not given:
            @pl.when(i == 0)
            def _():
                ph_ref[...] = jnp.zeros_like(ph_ref)

            p_ref[...] = _mm_nt(xt, wt_ref[...])
        w = _branch_weights(layer, nat, pw_ref, ca_ref, dw_ref, scr, i == 0)
        r = _branch_forward(p_ref, ph_ref, i == 0, i * tile, km_ref, vm_ref, w, ext_a, ext_c, ext_d, res_c, res_d,
                            tile)
        if not given:
            ph_ref[...] = p_ref[tile - HALO:tile, :]
        h = (r["concat"] * (r["gate"] * r["sig_gate"])).astype(h_ref.dtype)
        h_ref[...] = h
        cd_ref[:, 0:D_G], cd_ref[:, D_G:2 * D_G] = r["kept"]
        z = ALPHA * xt + _mm(h, wo_ref[...])
        z_ref[...] = z
        if want_xn:
            xn_ref[...] = _ln_fwd(z, _row_view(g_ref, layer)[...], _row_view(b_ref, layer)[...])[0]

    row = lambda i: (i, 0)
    consts = [km, vm] + nat_arrays + [pw, ca8, dw8, w_out, repl["ln_g"], repl["ln_b"]]
    act = jax.ShapeDtypeStruct((s, D_MODEL), F32)
    act_spec = pl.BlockSpec((tile, D_MODEL), row)
    if given:
        lead = [proj, proj]
        lead_specs = [pl.BlockSpec((tile, D_IN), row),
                      pl.BlockSpec((HALO, D_IN), lambda i: (jnp.maximum(i * hb - 1, 0), 0))]
        out_specs, out_shape, scratch = (), (), []
    else:
        lead = [wt_in]
        lead_specs = [_full_spec(wt_in)]
        out_specs = (pl.BlockSpec((tile, D_IN), row),)
        out_shape = (jax.ShapeDtypeStruct((s, D_IN), F32),)
        scratch = [pltpu.VMEM((HALO, D_IN), F32)]
    res = pl.pallas_call(
        body, grid=(s // tile,),
        in_specs=[act_spec] + lead_specs + [_full_spec(a) for a in consts] + _tie_specs(ties),
        out_specs=out_specs + (pl.BlockSpec((tile, D_MIX), row), act_spec, pl.BlockSpec((tile, 2 * D_G), row))
        + ((act_spec,) if want_xn else ()),
        out_shape=out_shape + (jax.ShapeDtypeStruct((s, D_MIX), MM_DTYPE), act, jax.ShapeDtypeStruct((s, 2 * D_G), F32))
        + ((act,) if want_xn else ()),
        scratch_shapes=scratch + [pltpu.VMEM((HALO + tile, D_G), F32)] * 3
        + [_residue_scratch(tile)] * 2 + [pltpu.VMEM(shape, F32) for _, shape in _BRANCH_W_SCRATCH],
        name="layer_fwd_given_proj" if given else "layer_fwd", compiler_params=_params("arbitrary"),
    )(x, *lead, *consts, *ties)
    res = ((proj,) + tuple(res)) if given else tuple(res)
    return res if want_xn else res + (None,)


_BRANCH_GRADS = (("g256", (8, D_G)), ("sg_w", (N_SUB, CHUNK, CHUNK)), ("sg_b", (8, CHUNK)),
                 ("pool_w", (N_SUB, HEAD_DIM, HEAD_DIM)), ("conv_a_w", (N_DEV, CONV_A, CONV_CH)),
                 ("cc_dw_w", (N_DEV, CONV_D, CONV_CH)), ("cc_pw_w", (D_G, D_G)),
                 ("dk", (N_SUB, MEM_LEN, D_G)), ("dv", (N_SUB, MEM_LEN, D_G)))
_BRANCH_ACC = (("conv_a", (CONV_A, D_G)), ("cc_dw_w", (CONV_D, D_G)), ("pool_wbd", (D_G, D_G)),
               ("sg_bias", (CHUNK, D_G)))


def _layer_bwd_fused(up, target, z, h, proj, kept, km, vm, layer, repl, w_out, pw, ca8, dw8, ties=(), tile=TILE):
    s = proj.shape[0]
    nt = s // tile
    hb = tile // HALO
    nat_arrays = [repl[n] for n in _BRANCH_REPL]
    n_nat, n_grads, n_acc, n_ties = len(nat_arrays), len(_BRANCH_GRADS), len(_BRANCH_ACC), len(ties)
    row_of = {n: k for k, n in enumerate(G256_ROWS)}
    from_loss = target is not None

    def body(o_ref, z_ref, h_ref, p_ref, ph_ref, cd_ref, km_ref, vm_ref, *rest):
        nat = dict(zip(_BRANCH_REPL, rest[:n_nat]))
        pw_ref, ca_ref, dw_ref, lng_ref, lnb_ref, wo_ref = rest[n_nat:n_nat + 6]
        rest = rest[n_nat + 6 + n_ties:]
        dz_ref, dp_ref, gw_ref, slab_ref = rest[:4]
        g = dict(zip([n for n, _ in _BRANCH_GRADS], rest[4:4 + n_grads]))
        rest = rest[4 + n_grads:]
        ext_a, rev_a, rev_c, rev_d, res_rc, res_rd, gw_acc, lacc = rest[:8]
        acc = dict(zip([n for n, _ in _BRANCH_ACC], rest[8:8 + n_acc]))
        scr = dict(zip([n for n, _ in _BRANCH_W_SCRATCH], rest[8 + n_acc:]))
        i = pl.program_id(0)
        t = nt - 1 - i

        @pl.when(i == 0)
        def _():
            for ref in list(g.values()) + list(acc.values()) + [rev_a, rev_c, rev_d, gw_acc, slab_ref, lacc]:
                ref[...] = jnp.zeros_like(ref)

        g_ln = _row_view(lng_ref, layer)[...]
        xn, xhat, rstd = _ln_fwd(z_ref[...], g_ln, _row_view(lnb_ref, layer)[...])
        if from_loss:
            err = xn - o_ref[...]
            lacc[...] += _rowsum(err * err)
            dxn = err * (1.0 / D_MODEL)
        else:
            dxn = o_ref[...]
        slab_ref[0:1, :] += _rowsum(dxn * xhat)
        slab_ref[1:2, :] += _rowsum(dxn)
        dz = _ln_bwd(dxn, xhat, rstd, g_ln)
        dz_ref[...] = dz
        dh_all = _mm_nt(dz, wo_ref[...])
        gw_acc[...] += _mm_tn(h_ref[...], dz)

        w = _branch_weights(layer, nat, pw_ref, ca_ref, dw_ref, scr, i == 0)
        r = _branch_forward(p_ref, ph_ref, t == 0, t * tile, km_ref, vm_ref, w, ext_a, None, None, None, None,
                            tile, cd_ref)

        def put(k, val, width=D_G):
            dp_ref[:, k:k + width] = val.astype(dp_ref.dtype)

        def add_row(name, val):
            k = row_of[name]
            g["g256"][k:k + 1, :] += val

        def push_rev(rev, val):
            head = rev[0:HALO]
            rev[tile:tile + HALO] = head
            rev[0:tile] = val

        gate, sig_gate, concat = r["gate"], r["sig_gate"], r["concat"]
        put(C_GATE, dh_all * concat * (sig_gate * (1.0 + gate * (1.0 - sig_gate))), D_MIX)
        dconcat = dh_all * (gate * sig_gate)
        dya, dyb, dyc, dyd, dye = [dconcat[:, k * D_G:(k + 1) * D_G] for k in range(N_GROUPS)]

        put(C_BA * D_G, dya * r["conv_a"])
        dconv_a = dya * r["ba"]
        push_rev(rev_a, dconv_a)
        dga = jnp.zeros((tile, D_G), F32)
        for k in range(CONV_A):
            ahead = rev_a[pl.ds(CONV_A - 1 - k, tile), :]
            dga = dga + w["conv_a"][k:k + 1, :] * ahead
            acc["conv_a"][k:k + 1, :] += _rowsum(r["g_a"] * ahead)
        put(C_CA * D_G, dga * r["xa"])
        put(C_XA * D_G, dga * r["ca"])

        add_row("pool_scale", _rowsum(dyc * r["pool_mm"]))
        dmm = dyc * w["pool_scale"][...]
        acc["pool_wbd"][...] += _mm_tn(r["ypre"], dmm)
        dypre = _mm_nt(dmm, w["pool_wbd"][...])
        dws = dypre * r["inv_cnt"]
        push_rev(rev_c, dws)
        _make_residues(rev_c, res_rc)
        run = dws
        sums = {}
        for k in range(1, POOL_WINDOWS[-1]):
            run = run + _rows_at(rev_c, res_rc, k, tile)
            if k + 1 in POOL_WINDOWS:
                sums[k + 1] = run
        put(C_XC * D_G, _pool_select(r["lane_grp"], sums[2], sums[4], sums[8], sums[16]) - dypre)

        g["cc_pw_w"][...] += _mm_tn(r["act_d"], dyd)
        dact = _mm_nt(dyd, w["cc_pw_w"][...])
        sig_ln, ln_d = r["sig_ln"], r["ln_d"]
        dln = dact * (sig_ln * (1.0 + ln_d * (1.0 - sig_ln)))
        add_row("cc_ln_g", _rowsum(dln * r["xhat_d"]))
        add_row("cc_ln_b", _rowsum(dln))
        dconv_d = _ln_bwd(dln, r["xhat_d"], r["rstd_d"], w["cc_ln_g"][...])
        add_row("cc_dw_b", _rowsum(dconv_d))
        push_rev(rev_d, dconv_d)
        _make_residues(rev_d, res_rd)
        dhd = jnp.zeros((tile, D_G), F32)
        for j in range(CONV_D):
            ahead = _rows_at(rev_d, res_rd, CONV_D - 1 - j, tile)
            dhd = dhd + w["cc_dw_w"][j:j + 1, :] * ahead
            acc["cc_dw_w"][j:j + 1, :] += _rowsum(r["hd"] * ahead)
        sig_dg = r["sig_dg"]
        put(C_DA * D_G, dhd * sig_dg)
        put(C_DG * D_G, dhd * r["da"] * sig_dg * (1.0 - sig_dg))

        dug = dyb * r["mixed"]
        dmixed = dyb * r["ug"]
        wm, lo, vn = r["wm"], r["lo"], r["vn"]
        dvn_chunks = []
        for c in range(tile // CHUNK):
            rows = slice(c * CHUNK, (c + 1) * CHUNK)
            acc["sg_bias"][...] += dmixed[rows, :]
            halves = []
            for hf in range(2):
                cols = slice(hf * 128, (hf + 1) * 128)
                dm = dmixed[rows, cols]
                dm_a, dm_b = jnp.where(lo, dm, 0.0), jnp.where(lo, 0.0, dm)
                vh = vn[rows, cols]
                g["sg_w"][2 * hf] += _mm_nt(dm_a, vh)
                g["sg_w"][2 * hf + 1] += _mm_nt(dm_b, vh)
                halves.append(_mm_tn(wm[2 * hf], dm_a) + _mm_tn(wm[2 * hf + 1], dm_b))
            dvn_chunks.append(jnp.concatenate(halves, axis=1))
        dvn = jnp.concatenate(dvn_chunks, axis=0)
        add_row("sg_ln_g", _rowsum(dvn * r["xhat_v"]))
        add_row("sg_ln_b", _rowsum(dvn))
        dvg = _ln_bwd(dvn, r["xhat_v"], r["rstd_v"], w["sg_ln_g"][...])
        put(C_V * D_G, dvg * _dgelu(r["v"], r["th_v"]))
        put(C_U * D_G, dug * _dgelu(r["u"], r["th_u"]))

        q = r["q"]
        dq = jnp.zeros((tile, D_G), F32)
        for h in range(N_SUB):
            p = r["probs"][h]
            dp = _mm_nt(dye, vm_ref[h])
            g["dv"][h] += _mm_tn(p, dye)
            ds = p * (dp - jnp.sum(dp * p, axis=-1, keepdims=True)) * ATT_SCALE
            dq = dq + _mm(ds, km_ref[h])
            g["dk"][h] += _mm_tn(ds, q)
        put(C_Q * D_G, dq)

        @pl.when(i == nt - 1)
        def _():
            for h in range(N_SUB):
                g["sg_w"][h] = jnp.where(r["tri"], g["sg_w"][h], 0.0)
            lane_head = _head_of_lane((CHUNK, D_G))
            col_of = lax.broadcasted_iota(jnp.int32, (CHUNK, 8), 1)
            ba = acc["sg_bias"][...]
            sgb_t = jnp.zeros((CHUNK, 8), F32)
            for h in range(N_SUB):
                col = jnp.sum(jnp.where(lane_head == h, ba, 0.0), axis=-1, keepdims=True)
                sgb_t = jnp.where(col_of == h, col, sgb_t)
            g["sg_b"][...] = sgb_t.T
            wbd = acc["pool_wbd"][...]
            for gi in range(N_SUB):
                sl = slice(gi * HEAD_DIM, (gi + 1) * HEAD_DIM)
                g["pool_w"][gi] = wbd[sl, sl]
            ca, dw = acc["conv_a"][...], acc["cc_dw_w"][...]
            for p in range(N_DEV):
                g["conv_a_w"][p] = ca[:, p * CONV_CH:(p + 1) * CONV_CH]
                g["cc_dw_w"][p] = dw[:, p * CONV_CH:(p + 1) * CONV_CH]
            gw_ref[...] = gw_acc[...].astype(gw_ref.dtype)
            if from_loss:
                total = jnp.sum(lacc[...], axis=-1, keepdims=True) * (0.5 / D_MODEL)
                slab_ref[LOSS_ROW:LOSS_ROW + 1, :] = jnp.broadcast_to(total, (1, D_MODEL))

    rev = lambda i: (nt - 1 - i, 0)
    fixed = lambda i: (0, 0)
    act_spec = pl.BlockSpec((tile, D_MODEL), rev)
    grad_specs = tuple(pl.BlockSpec(shape, lambda i, _nd=len(shape): (0,) * _nd) for _, shape in _BRANCH_GRADS)
    grad_shapes = tuple(jax.ShapeDtypeStruct(shape, F32) for _, shape in _BRANCH_GRADS)
    consts = [km, vm] + nat_arrays + [pw, ca8, dw8, repl["ln_g"], repl["ln_b"], w_out]
    outs = pl.pallas_call(
        body, grid=(nt,),
        in_specs=[act_spec, act_spec, pl.BlockSpec((tile, D_MIX), rev), pl.BlockSpec((tile, D_IN), rev),
                  pl.BlockSpec((HALO, D_IN), lambda i: (jnp.maximum((nt - 1 - i) * hb - 1, 0), 0)),
                  pl.BlockSpec((tile, 2 * D_G), rev)]
        + [_full_spec(a) for a in consts] + _tie_specs(ties),
        out_specs=(act_spec, pl.BlockSpec((tile, D_IN), rev), pl.BlockSpec((D_MIX, D_MODEL), fixed),
                   pl.BlockSpec((8, D_MODEL), fixed)) + grad_specs,
        out_shape=(jax.ShapeDtypeStruct((s, D_MODEL), F32), jax.ShapeDtypeStruct((s, D_IN), MM_DTYPE),
                   jax.ShapeDtypeStruct((D_MIX, D_MODEL), GRAD_DTYPE), jax.ShapeDtypeStruct((8, D_MODEL), F32))
        + grad_shapes,
        scratch_shapes=[pltpu.VMEM((HALO + tile, D_G), F32)] * 4 + [_residue_scratch(tile)] * 2
        + [pltpu.VMEM((D_MIX, D_MODEL), F32), pltpu.VMEM((1, D_MODEL), F32)]
        + [pltpu.VMEM(shape, F32) for _, shape in _BRANCH_ACC + _BRANCH_W_SCRATCH],
        name="layer_bwd_loss" if from_loss else "layer_bwd", compiler_params=_params("arbitrary"),
    )(target if from_loss else up, z, h, proj, proj, kept, *consts, *ties)
    return outs[0], outs[1], outs[2], outs[3], dict(zip([n for n, _ in _BRANCH_GRADS], outs[4:]))


def _dx_matmul(dproj, wt_in, dz, ties=(), tm=512):
    s = dproj.shape[0]

    def body(dp_ref, w_ref, dz_ref, *rest):
        o_ref = rest[len(ties)]
        o_ref[...] = _mm(dp_ref[...], w_ref[...]) + ALPHA * dz_ref[...]

    row = lambda i: (i, 0)
    return pl.pallas_call(
        body, grid=(s // tm,),
        in_specs=[pl.BlockSpec((tm, D_IN), row), _full_spec(wt_in), pl.BlockSpec((tm, D_MODEL), row)]
        + _tie_specs(ties),
        out_specs=pl.BlockSpec((tm, D_MODEL), row),
        out_shape=jax.ShapeDtypeStruct((s, D_MODEL), F32), name="dx_mm", compiler_params=_params("arbitrary"),
    )(dproj, wt_in, dz, *ties)


def _dw_in_matmul(x, dproj, ties=(), tk=512):
    s = x.shape[0]
    nk = s // tk
    blk = 2 * W_IN_COLS

    def body(x_ref, dp_ref, *rest):
        o_ref, acc = rest[len(ties):]
        k = pl.program_id(0)

        @pl.when(k == 0)
        def _():
            acc[...] = jnp.zeros_like(acc)

        xb = x_ref[...].astype(MM_DTYPE)
        for j in range(D_IN // blk):
            acc[j * blk:(j + 1) * blk, :] += _mm_tn(dp_ref[:, j * blk:(j + 1) * blk], xb)

        @pl.when(k == nk - 1)
        def _():
            o_ref[...] = acc[...].astype(o_ref.dtype)

    return pl.pallas_call(
        body, grid=(nk,),
        in_specs=[pl.BlockSpec((tk, D_MODEL), lambda k: (k, 0)), pl.BlockSpec((tk, D_IN), lambda k: (k, 0))]
        + _tie_specs(ties),
        out_specs=pl.BlockSpec((D_IN, D_MODEL), lambda k: (0, 0)),
        out_shape=jax.ShapeDtypeStruct((D_IN, D_MODEL), GRAD_DTYPE),
        scratch_shapes=[pltpu.VMEM((D_IN, D_MODEL), F32)], name="dw_in_mm", compiler_params=_params("arbitrary"),
    )(x, dproj, *ties)


def _in_bwd(x, dproj, wt_in, dz, tm=512):
    s = x.shape[0]
    n_steps = s // tm

    assert wt_in.dtype == GRAD_DTYPE
    blk = 2 * W_IN_COLS

    def body(x_ref, dp_ref, w_hbm, dz_ref, o_ref, gw_hbm, w_vmem, acc, sem):
        i = pl.program_id(0)

        @pl.when(i == 0)
        def _():
            fetch = pltpu.make_async_copy(w_hbm, w_vmem, sem)
            fetch.start()
            acc[...] = jnp.zeros_like(acc)
            fetch.wait()

        o_ref[...] = _mm(dp_ref[...], w_vmem[...]) + ALPHA * dz_ref[...]
        xb = x_ref[...].astype(MM_DTYPE)
        for j in range(D_IN // blk):
            acc[j * blk:(j + 1) * blk, :] += _mm_tn(dp_ref[:, j * blk:(j + 1) * blk], xb)

        @pl.when(i == n_steps - 1)
        def _():
            w_vmem[...] = acc[...].astype(w_vmem.dtype)
            emit = pltpu.make_async_copy(w_vmem, gw_hbm, sem)
            emit.start()
            emit.wait()

    row = lambda i: (i, 0)
    any_spec = pl.BlockSpec(memory_space=pl.ANY)
    return pl.pallas_call(
        body, grid=(n_steps,),
        in_specs=[pl.BlockSpec((tm, D_MODEL), row), pl.BlockSpec((tm, D_IN), row), any_spec,
                  pl.BlockSpec((tm, D_MODEL), row)],
        out_specs=(pl.BlockSpec((tm, D_MODEL), row), any_spec),
        out_shape=(jax.ShapeDtypeStruct((s, D_MODEL), F32), jax.ShapeDtypeStruct((D_IN, D_MODEL), GRAD_DTYPE)),
        scratch_shapes=[pltpu.VMEM((D_IN, D_MODEL), wt_in.dtype), pltpu.VMEM((D_IN, D_MODEL), F32),
                        pltpu.SemaphoreType.DMA],
        name="in_bwd", compiler_params=_params("arbitrary"),
    )(x, dproj, wt_in, dz)


def _kv_bwd(mem, dk, dv):
    def body(mem_ref, dk_ref, dv_ref, o_ref):
        grp = _head_of_lane((MEM_LEN, D_G))
        dk_sum = jnp.zeros((MEM_LEN, D_G), F32)
        dv_sum = jnp.zeros((MEM_LEN, D_G), F32)
        for h in range(N_SUB):
            dk_sum = dk_sum + jnp.where(grp == h, dk_ref[h], 0.0)
            dv_sum = dv_sum + jnp.where(grp == h, dv_ref[h], 0.0)
        o_ref[...] = _mm_tn(mem_ref[...], jnp.concatenate([dk_sum, dv_sum], axis=1)).astype(o_ref.dtype)

    return pl.pallas_call(body, out_shape=jax.ShapeDtypeStruct((D_MODEL, 2 * D_G), GRAD_DTYPE), name="kv_bwd",
                          compiler_params=_params())(mem, dk, dv)


def _layer_fwd(x, mem, layer, repl, gw, proj=None, km_vm=None, ties=()):
    km, vm = _kv_project(mem, gw["w_kv"]) if km_vm is None else km_vm
    proj, h, z, kept, xn = _layer_fwd_fused(x, gw["wt_in"] if proj is None else None, proj, km, vm, layer, repl,
                                            gw["pw"], gw["ca8"], gw["dw8"], gw["w_out"], layer < DEPTH - 1, ties)
    return xn, (x, proj, h, z, km, vm, kept)


def _layer_bwd_a(up, target, mem, layer, repl, gw, saved, ties=()):
    x_in, proj, h, z, km, vm, kept = saved
    dz, dproj, g_w_out, g1024, bg = _layer_bwd_fused(up, target, z, h, proj, kept, km, vm, layer, repl, gw["w_out"],
                                                     gw["pw"], gw["ca8"], gw["dw8"], ties)
    grads = {n: bg[n] for n in ("g256", "sg_w", "sg_b", "pool_w", "conv_a_w", "cc_dw_w")}
    grads.update(w_out=g_w_out.reshape(N_DEV, D_MIX // N_DEV, D_MODEL), g1024=g1024,
                 w_kv=_kv_bwd(mem, bg["dk"], bg["dv"]).reshape(N_DEV, D_MODEL // N_DEV, 2 * D_G),
                 cc_pw_w=bg["cc_pw_w"].reshape(N_DEV, CONV_CH, D_G))
    return dz, dproj, grads


def _landing_shapes(items):
    out = []
    for a, scatter, pick in items:
        shape = a.shape if scatter else (N_DEV,) + (a.shape if pick is None else a.shape[1:])
        out.append(jax.ShapeDtypeStruct(shape, a.dtype))
    return tuple(out)


def _exchange_sems(n):
    return [pltpu.SemaphoreType.DMA(((N_DEV - 1) * n,)), pltpu.SemaphoreType.DMA(((N_DEV - 1) * n,)),
            pltpu.SemaphoreType.DMA((n,))]


def _exchange_copies(modes, ins, outs, send_sems, recv_sems, local_sems):
    n = len(ins)
    x, y, c = lax.axis_index("x"), lax.axis_index("y"), lax.axis_index("c")
    me = 4 * x + 2 * y + c

    def src_of(a, dest):
        scatter, pick = modes[a]
        if scatter:
            return ins[a].at[dest]
        return ins[a] if pick is None else ins[a].at[pick]

    local = [pltpu.make_async_copy(src_of(a, me), outs[a].at[me], local_sems.at[a]) for a in range(n)]
    sends, recvs = [], []
    for k in range(1, N_DEV):
        px = 1 - x if k & 4 else x
        py = 1 - y if k & 2 else y
        pc = 1 - c if k & 1 else c
        peer = 4 * px + 2 * py + pc
        for a in range(n):
            sems = dict(send_sem=send_sems.at[(k - 1) * n + a], recv_sem=recv_sems.at[(k - 1) * n + a],
                        device_id=(px, py, pc), device_id_type=pl.DeviceIdType.MESH)
            sends.append(pltpu.make_async_remote_copy(src_ref=src_of(a, peer), dst_ref=outs[a].at[me], **sems))
            recvs.append(pltpu.make_async_remote_copy(src_ref=src_of(a, peer), dst_ref=outs[a].at[peer], **sems))
    return local, sends, recvs


def _exchange(items, name):
    n = len(items)
    modes = [(scatter, pick) for _, scatter, pick in items]

    def body(*refs):
        local, sends, recvs = _exchange_copies(modes, refs[:n], refs[n:2 * n], *refs[2 * n:])
        for cp in local + sends:
            cp.start()
        for cp in recvs:
            cp.wait_recv()
        for cp in sends:
            cp.wait_send()
        for cp in local:
            cp.wait()

    any_spec = pl.BlockSpec(memory_space=pl.ANY)
    return pl.pallas_call(
        body, in_specs=[any_spec] * n, out_specs=(any_spec,) * n, out_shape=_landing_shapes(items),
        scratch_shapes=_exchange_sems(n), name=name,
    )(*[a for a, _, _ in items])


def _gather_two_level(items, name):
    n = len(items)
    assert not any(scatter for _, scatter, _ in items)
    picks = [pick for _, _, pick in items]

    def body(*refs):
        ins, outs = refs[:n], refs[n:2 * n]
        send_sems, recv_sems, local_sems = refs[2 * n:]
        x, y, c = lax.axis_index("x"), lax.axis_index("y"), lax.axis_index("c")
        sib = 1 - c
        chips = [(1 - x, y), (x, 1 - y), (1 - x, 1 - y)]

        def slot(a, px, py, pc):
            return outs[a].at[4 * px + 2 * py + pc]

        def copy(k, a, src, block, to):
            return pltpu.make_async_remote_copy(
                src_ref=src, dst_ref=slot(a, *block), send_sem=send_sems.at[k * n + a],
                recv_sem=recv_sems.at[k * n + a], device_id=to, device_id_type=pl.DeviceIdType.MESH)

        own = [ins[a] if picks[a] is None else ins[a].at[picks[a]] for a in range(n)]
        local = [pltpu.make_async_copy(own[a], slot(a, x, y, c), local_sems.at[a]) for a in range(n)]
        first = [copy(0, a, own[a], (x, y, c), (x, y, sib)) for a in range(n)]
        first += [copy(1 + j, a, own[a], (x, y, c), (*chip, c)) for j, chip in enumerate(chips) for a in range(n)]
        for cp in local + first:
            cp.start()
        passed = []
        for j, chip in enumerate(chips):
            for a in range(n):
                copy(1 + j, a, own[a], (*chip, c), (x, y, c)).wait_recv()
                fwd = copy(4 + j, a, slot(a, *chip, c), (*chip, c), (x, y, sib))
                fwd.start()
                passed.append(fwd)
        for a in range(n):
            copy(0, a, own[a], (x, y, sib), (x, y, c)).wait_recv()
        for j, chip in enumerate(chips):
            for a in range(n):
                copy(4 + j, a, own[a], (*chip, sib), (x, y, c)).wait_recv()
        for cp in first + passed:
            cp.wait_send()
        for cp in local:
            cp.wait()

    any_spec = pl.BlockSpec(memory_space=pl.ANY)
    return pl.pallas_call(
        body, in_specs=[any_spec] * n, out_specs=(any_spec,) * n, out_shape=_landing_shapes(items),
        scratch_shapes=[pltpu.SemaphoreType.DMA((7 * n,)), pltpu.SemaphoreType.DMA((7 * n,)),
                        pltpu.SemaphoreType.DMA((n,))],
        name=name,
    )(*[a for a, _, _ in items])


_HBM_SPEC = pl.BlockSpec(memory_space=pltpu.HBM)
_SEM_SPEC = pl.BlockSpec(memory_space=pltpu.SEMAPHORE)
_SPLIT_PARAMS = pltpu.CompilerParams(has_side_effects=pltpu.SideEffectType.DATAFLOW_SIDE_EFFECTING)


def _split_start(srcs, lands, plan, sem_shapes, name):
    n_src, n_land = len(srcs), len(lands)
    n_buf = n_src + n_land
    bufs = [pltpu.with_memory_space_constraint(a, pltpu.HBM) for a in list(srcs) + list(lands)]

    def body(*refs):
        local, sends, _ = plan(refs[:n_src], refs[n_src:n_buf], *refs[n_buf:n_buf + 3])
        for cp in local + sends:
            cp.start()
        token = refs[-1]
        token[...] = jnp.zeros_like(token)

    res = pl.pallas_call(
        body, name=name, in_specs=[_HBM_SPEC] * n_buf,
        out_shape=tuple(sem_shapes) + tuple(pltpu.HBM(a.shape, a.dtype) for a in bufs)
        + (jax.ShapeDtypeStruct((8, 128), F32),),
        out_specs=(_SEM_SPEC,) * 3 + (_HBM_SPEC,) * n_buf + (pl.BlockSpec(memory_space=pltpu.VMEM),),
        input_output_aliases={i: 3 + i for i in range(n_buf)}, compiler_params=_SPLIT_PARAMS,
    )(*bufs)
    return dict(sems=res[:3], srcs=res[3:3 + n_src], lands=res[3 + n_src:3 + n_buf], token=res[-1], plan=plan)


def _split_wait(ticket, after, name):
    n_src, n_land = len(ticket["srcs"]), len(ticket["lands"])
    n_buf = n_src + n_land
    plan = ticket["plan"]

    def body(*refs):
        local, sends, recvs = plan(refs[:n_src], refs[n_src:n_buf], *refs[n_buf:n_buf + 3])
        for cp in recvs:
            cp.wait_recv()
        for cp in sends:
            cp.wait_send()
        for cp in local:
            cp.wait()

    bufs = list(ticket["srcs"]) + list(ticket["lands"])
    res = pl.pallas_call(
        body, name=name, in_specs=[_HBM_SPEC] * n_buf + [_SEM_SPEC] * 3 + [pl.BlockSpec(memory_space=pl.ANY)],
        out_shape=tuple(pltpu.HBM(a.shape, a.dtype) for a in bufs), out_specs=(_HBM_SPEC,) * n_buf,
        input_output_aliases={i: i for i in range(n_buf)}, compiler_params=_SPLIT_PARAMS,
    )(*bufs, *ticket["sems"], after)
    return res[n_src:]


def _empty_landings(items):
    return [lax.empty(s.shape, s.dtype) for s in _landing_shapes(items)]


def _exchange_start(items, name):
    modes = [(scatter, pick) for _, scatter, pick in items]
    plan = lambda ins, outs, *sems: _exchange_copies(modes, ins, outs, *sems)
    return _split_start([a for a, _, _ in items], _empty_landings(items), plan, _exchange_sems(len(items)), name)


def _two_level_plans(picks):
    n = len(picks)

    def place():
        x, y, c = lax.axis_index("x"), lax.axis_index("y"), lax.axis_index("c")
        return x, y, c, 1 - c, [(1 - x, y), (x, 1 - y), (1 - x, 1 - y)]

    def copy(outs, send_sems, recv_sems, k, a, src, block, to):
        px, py, pc = block
        return pltpu.make_async_remote_copy(
            src_ref=src, dst_ref=outs[a].at[4 * px + 2 * py + pc], send_sem=send_sems.at[k * n + a],
            recv_sem=recv_sems.at[k * n + a], device_id=to, device_id_type=pl.DeviceIdType.MESH)

    def between_chips(ins, outs, send_sems, recv_sems, local_sems):
        x, y, c, sib, chips = place()
        own = [ins[a] if picks[a] is None else ins[a].at[picks[a]] for a in range(n)]
        mk = lambda *args: copy(outs, send_sems, recv_sems, *args)
        local = [pltpu.make_async_copy(own[a], outs[a].at[4 * x + 2 * y + c], local_sems.at[a]) for a in range(n)]
        sends = [mk(0, a, own[a], (x, y, c), (x, y, sib)) for a in range(n)]
        sends += [mk(1 + j, a, own[a], (x, y, c), (*chip, c)) for j, chip in enumerate(chips) for a in range(n)]
        recvs = [mk(0, a, own[a], (x, y, sib), (x, y, c)) for a in range(n)]
        recvs += [mk(1 + j, a, own[a], (*chip, c), (x, y, c)) for j, chip in enumerate(chips) for a in range(n)]
        return local, sends, recvs

    def within_chip(ins, outs, send_sems, recv_sems, local_sems):
        x, y, c, sib, chips = place()
        mk = lambda *args: copy(outs, send_sems, recv_sems, *args)
        slot = lambda a, px, py, pc: outs[a].at[4 * px + 2 * py + pc]
        sends = [mk(j, a, slot(a, *chip, c), (*chip, c), (x, y, sib)) for j, chip in enumerate(chips)
                 for a in range(n)]
        recvs = [mk(j, a, slot(a, *chip, c), (*chip, sib), (x, y, c)) for j, chip in enumerate(chips)
                 for a in range(n)]
        return [], sends, recvs

    sems = lambda k: [pltpu.SemaphoreType.DMA((k * n,)), pltpu.SemaphoreType.DMA((k * n,)),
                      pltpu.SemaphoreType.DMA((n,))]
    return between_chips, sems(4), within_chip, sems(3)


def _adam_math(g, w, m, v):
    m_new = ADAM_B1 * m + (1.0 - ADAM_B1) * g
    v_new = ADAM_B2 * v + (1.0 - ADAM_B2) * (g * g)
    m_hat = m_new / (1.0 - ADAM_B1 ** ADAM_STEP)
    v_hat = v_new / (1.0 - ADAM_B2 ** ADAM_STEP)
    return -ADAM_LR * (m_hat / (jnp.sqrt(v_hat) + ADAM_EPS) + ADAM_WD * w), m_new, v_new


def _adamw_big(parts, w, m, v, layer, prev, name, tr):
    depth, rows, cols = w.shape

    def body(p_ref, w_ref, m_ref, v_ref, *rest):
        g_out, d_out, m_out, v_out = rest[len(prev):]
        g = p_ref[0].astype(F32)
        for q in range(1, N_DEV):
            g = g + p_ref[q].astype(F32)
        d, m_new, v_new = _adam_math(g, w_ref[...], m_ref[...], v_ref[...])
        g_out[...] = g
        d_out[...] = d
        m_out[...] = m_new
        v_out[...] = v_new

    blk = pl.BlockSpec((None, tr, cols), lambda i: (layer, i, 0))
    shp = jax.ShapeDtypeStruct((depth, rows, cols), F32)
    return pl.pallas_call(
        body, grid=(rows // tr,),
        in_specs=[pl.BlockSpec((N_DEV, tr, cols), lambda i: (0, i, 0)), blk, blk, blk]
        + [pl.BlockSpec(memory_space=pl.ANY)] * len(prev),
        out_specs=(blk,) * 4, out_shape=(shp,) * 4,
        input_output_aliases={4 + j: j for j in range(len(prev))},
        name=name, compiler_params=_params("arbitrary"),
    )(parts, w, m, v, *prev)


_SMALL_TENSORS = (("conv_a_w", "conv_a_w", None), ("cc_dw_w", "cc_dw_w", None), ("cc_pw_w", "cc_pw_w", None),
                  ("sg_w", "sg_w", None), ("pool_w", "pool_w", None), ("sg_b", "sg_b", None)) \
    + tuple((n, "g256", k) for k, n in enumerate(G256_ROWS)) + tuple((n, "g1024", k) for k, n in enumerate(G1024_ROWS))
_SMALL_LANDINGS = ("conv_a_w", "cc_dw_w", "cc_pw_w", "sg_w", "pool_w", "sg_b", "g256", "g1024")
_TAPS_FIRST = ("conv_a_w", "cc_dw_w")


def _adamw_small(landings, wts, mom, var):
    names = [n for n, _, _ in _SMALL_TENSORS]
    n_land = DEPTH * len(_SMALL_LANDINGS)
    n_t = len(names)

    def body(*refs):
        land = [dict(zip(_SMALL_LANDINGS, refs[l * len(_SMALL_LANDINGS):(l + 1) * len(_SMALL_LANDINGS)]))
                for l in range(DEPTH)]
        w_refs = dict(zip(names, refs[n_land:n_land + n_t]))
        m_refs = dict(zip(names, refs[n_land + n_t:n_land + 2 * n_t]))
        v_refs = dict(zip(names, refs[n_land + 2 * n_t:n_land + 3 * n_t]))
        outs = refs[n_land + 3 * n_t:]
        out_refs = {n: outs[4 * k:4 * k + 4] for k, n in enumerate(names)}
        loss_ref = outs[4 * n_t]
        for name, key, row in _SMALL_TENSORS:
            for l in range(DEPTH):
                src = land[l][key]
                if row is not None:
                    part = lambda q: src[q, row:row + 1, :]
                    at = (slice(l, l + 1),)
                elif name == "sg_b":
                    part = lambda q: src[q, 0:N_SUB, :]
                    at = (l,)
                elif name in _TAPS_FIRST:
                    part = lambda q: src[q]
                    at = (slice(None), l)
                else:
                    part = lambda q: src[q]
                    at = (l,)
                g = part(0)
                for q in range(1, N_DEV):
                    g = g + part(q)
                d, m_new, v_new = _adam_math(g, w_refs[name][at], m_refs[name][at], v_refs[name][at])
                for ref, val in zip(out_refs[name], (g, d, m_new, v_new)):
                    ref[at] = val
        src = land[DEPTH - 1]["g1024"]
        loss = src[0, LOSS_ROW:LOSS_ROW + 1, 0:128]
        for q in range(1, N_DEV):
            loss = loss + src[q, LOSS_ROW:LOSS_ROW + 1, 0:128]
        loss_ref[...] = loss

    ins = [landings[l][k] for l in range(DEPTH) for k in _SMALL_LANDINGS] \
        + [src[n] for src in (wts, mom, var) for n in names]
    out_shape = tuple(jax.ShapeDtypeStruct(wts[n].shape, F32) for n in names for _ in range(4)) \
        + (jax.ShapeDtypeStruct((1, 128), F32),)
    res = pl.pallas_call(body, out_shape=out_shape, name="adamw_small", compiler_params=_params())(*ins)
    return {n: res[4 * k:4 * k + 4] for k, n in enumerate(names)}, res[4 * n_t]


_BIG = (("w_in", 64), ("w_out", 32), ("w_kv", 32))
_GRAD_ITEMS_EARLY = ("w_out", "w_kv", "cc_pw_w", "conv_a_w", "cc_dw_w")
_GRAD_ITEMS_REPL = ("g256", "sg_w", "sg_b", "pool_w", "g1024")


def _grad_items(grads, with_w_in):
    items = [(grads[n], True, None) for n in (("w_in",) if with_w_in else ()) + _GRAD_ITEMS_EARLY]
    return items + [(grads[n], False, None) for n in _GRAD_ITEMS_REPL]


def _landed(parts, with_w_in):
    names = (("w_in",) if with_w_in else ()) + _GRAD_ITEMS_EARLY + _GRAD_ITEMS_REPL
    return dict(zip(names, parts))


def _gathered_weights(wt_in8, w_kv8, w_out8, pw8, ca8, dw8):
    return dict(wt_in=wt_in8.reshape(D_IN, D_MODEL), w_kv=w_kv8.reshape(D_MODEL, 2 * D_G),
                w_out=w_out8.reshape(D_MIX, D_MODEL), pw=pw8.reshape(D_G, D_G), ca8=ca8, dw8=dw8)


def kernel(x, mem, w_in, conv_a_w, sg_ln_g, sg_ln_b, sg_w, sg_b, pool_w, pool_scale, cc_dw_w, cc_dw_b, cc_ln_g, cc_ln_b, cc_pw_w, w_kv, w_out, ln_g, ln_b, loss_target, m_w_in, m_conv_a_w, m_sg_ln_g, m_sg_ln_b, m_sg_w, m_sg_b, m_pool_w, m_pool_scale, m_cc_dw_w, m_cc_dw_b, m_cc_ln_g, m_cc_ln_b, m_cc_pw_w, m_w_kv, m_w_out, m_ln_g, m_ln_b, v_w_in, v_conv_a_w, v_sg_ln_g, v_sg_ln_b, v_sg_w, v_sg_b, v_pool_w, v_pool_scale, v_cc_dw_w, v_cc_dw_b, v_cc_ln_g, v_cc_ln_b, v_cc_pw_w, v_w_kv, v_w_out, v_ln_g, v_ln_b):
    names = ("w_in", "conv_a_w", "sg_ln_g", "sg_ln_b", "sg_w", "sg_b", "pool_w", "pool_scale", "cc_dw_w", "cc_dw_b",
             "cc_ln_g", "cc_ln_b", "cc_pw_w", "w_kv", "w_out", "ln_g", "ln_b")
    wts = dict(zip(names, (w_in, conv_a_w, sg_ln_g, sg_ln_b, sg_w, sg_b, pool_w, pool_scale, cc_dw_w, cc_dw_b,
                           cc_ln_g, cc_ln_b, cc_pw_w, w_kv, w_out, ln_g, ln_b)))
    mom = dict(zip(names, (m_w_in, m_conv_a_w, m_sg_ln_g, m_sg_ln_b, m_sg_w, m_sg_b, m_pool_w, m_pool_scale,
                           m_cc_dw_w, m_cc_dw_b, m_cc_ln_g, m_cc_ln_b, m_cc_pw_w, m_w_kv, m_w_out, m_ln_g, m_ln_b)))
    var = dict(zip(names, (v_w_in, v_conv_a_w, v_sg_ln_g, v_sg_ln_b, v_sg_w, v_sg_b, v_pool_w, v_pool_scale,
                           v_cc_dw_w, v_cc_dw_b, v_cc_ln_g, v_cc_ln_b, v_cc_pw_w, v_w_kv, v_w_out, v_ln_g, v_ln_b)))
    repl = wts
    xs, mems, tgt = x[0], mem[0], loss_target[0]
    turned = {"w_in": (0, 2, 1), "conv_a_w": (1, 0, 2), "cc_dw_w": (1, 0, 2)}
    wts, mom, var = [{n: (jnp.transpose(a, turned[n]) if n in turned else a) for n, a in src.items()}
                     for src in (wts, mom, var)]
    wb = {n: wts[n].astype(MM_DTYPE) for n in ("w_in", "w_kv", "w_out", "cc_pw_w")}

    wt8_0, wkv8_0 = _gather_two_level([(wb["w_in"], False, 0), (wb["w_kv"], False, 0)], "gather_weights_0a")
    rest_0 = _exchange_start([(wb["w_out"], False, 0), (wb["cc_pw_w"], False, 0), (wts["conv_a_w"], False, None),
                              (wts["cc_dw_w"], False, None)], "gather_weights_0b_start")
    km_vm0 = _kv_project(mems, wkv8_0.reshape(D_MODEL, 2 * D_G))
    proj0 = _proj_matmul(xs, wt8_0.reshape(D_IN, D_MODEL), (rest_0["token"],))
    wo8_0, pw8_0, ca8, dw8 = _split_wait(rest_0, proj0, "gather_weights_0b_wait")
    gw0 = _gathered_weights(wt8_0, wkv8_0, wo8_0, pw8_0, ca8, dw8)
    items_1 = [(wb[n], False, 1) for n in ("w_in", "w_kv", "w_out", "cc_pw_w")]
    between_chips, sems_a, within_chip, sems_b = _two_level_plans([1] * len(items_1))
    chips_1 = _split_start([a for a, _, _ in items_1], _empty_landings(items_1), between_chips, sems_a,
                           "gather_weights_1a_start")
    x1, saved0 = _layer_fwd(xs, mems, 0, repl, gw0, proj0, km_vm0, (chips_1["token"],))
    core_1 = _split_start([], _split_wait(chips_1, x1, "gather_weights_1a_wait"), within_chip, sems_b,
                          "gather_weights_1b_start")
    gw1 = _gathered_weights(*_split_wait(core_1, core_1["token"], "gather_weights_1b_wait"), ca8, dw8)
    _, saved1 = _layer_fwd(x1, mems, 1, repl, gw1)

    dz1, dproj1, g1 = _layer_bwd_a(None, tgt, mems, 1, repl, gw1, saved1)
    shards = lambda g: g.reshape(N_DEV, W_IN_COLS, D_MODEL)
    up, g_wt_in_1 = _in_bwd(saved1[0], dproj1, gw1["wt_in"], dz1)
    g1["w_in"] = shards(g_wt_in_1)
    grads_1 = _exchange_start(_grad_items(g1, True), "exchange_grads_1_start")
    dz0, dproj0, g0 = _layer_bwd_a(up, None, mems, 0, repl, gw0, saved0, (grads_1["token"],))
    early_0 = _exchange_start(_grad_items(g0, False), "exchange_grads_0a_start")
    g_wt_in_0 = _dw_in_matmul(saved0[0], dproj0, (early_0["token"],))
    late_0 = _exchange_start([(shards(g_wt_in_0), True, None)], "exchange_grads_0b_start")
    grad_x = _dx_matmul(dproj0, gw0["wt_in"], dz0, (late_0["token"],))

    landed = [None, _landed(_split_wait(grads_1, grad_x, "exchange_grads_1_wait"), True)]
    big = {}
    for n, tr in _BIG:
        big[n] = _adamw_big(landed[1][n], wts[n], mom[n], var[n], 1, (), "adamw_" + n + "_1", tr)
    landed[0] = _landed(_split_wait(early_0, big["w_kv"][0], "exchange_grads_0a_wait"), False)
    for n, tr in _BIG[1:]:
        big[n] = _adamw_big(landed[0][n], wts[n], mom[n], var[n], 0, big[n], "adamw_" + n + "_0", tr)
    small, loss = _adamw_small(landed, wts, mom, var)
    (landed[0]["w_in"],) = _split_wait(late_0, loss, "exchange_grads_0b_wait")
    big["w_in"] = _adamw_big(landed[0]["w_in"], wts["w_in"], mom["w_in"], var["w_in"], 0, big["w_in"],
                             "adamw_w_in_0", _BIG[0][1])

    res = {**small, **big}
    res = {n: ([jnp.transpose(a, turned[n]) for a in r] if n in turned else r) for n, r in res.items()}
    return (loss[0, 0], grad_x[None], *[res[n][0] for n in names], *[res[n][1] for n in names],
            *[res[n][2] for n in names], *[res[n][3] for n in names])
```

```python
import math

import jax
import jax.numpy as jnp
from jax import lax
from jax.experimental import pallas as pl
from jax.experimental.pallas import tpu as pltpu

F32 = jnp.float32
MM_DTYPE = jnp.bfloat16
GRAD_DTYPE = jnp.bfloat16

D_MODEL = 1024
DEPTH = 2
D_G = 256
N_GROUPS = 5
D_MIX = N_GROUPS * D_G
N_SUB = 4
HEAD_DIM = D_G // N_SUB
CONV_A = 3
CONV_D = 31
CHUNK = 128
POOL_WINDOWS = (2, 4, 8, 16)
MEM_LEN = 256
LN_EPS = 1e-5
ALPHA = (2.0 * DEPTH) ** 0.25
D_IN = 9 * D_G + D_MIX
ATT_SCALE = 1.0 / math.sqrt(HEAD_DIM)

ADAM_LR = 0.001
ADAM_B1 = 0.9
ADAM_B2 = 0.999
ADAM_EPS = 1e-08
ADAM_WD = 0.01
ADAM_STEP = 10

N_DEV = 8
W_IN_COLS = D_IN // N_DEV
CONV_CH = D_G // N_DEV
HALO = 32
TILE = 256
VMEM_LIMIT = 56 * 1024 * 1024

C_XA, C_BA, C_CA, C_U, C_V, C_XC, C_DA, C_DG, C_Q = range(9)
C_GATE = 9 * D_G

G256_ROWS = ("sg_ln_g", "sg_ln_b", "pool_scale", "cc_dw_b", "cc_ln_g", "cc_ln_b")
G1024_ROWS = ("ln_g", "ln_b")
LOSS_ROW = 2


def _mm(a, b):
    return jnp.dot(a.astype(MM_DTYPE), b.astype(MM_DTYPE), preferred_element_type=F32)


def _mm_nt(a, b):
    return lax.dot_general(a.astype(MM_DTYPE), b.astype(MM_DTYPE), (((1,), (1,)), ((), ())),
                           preferred_element_type=F32)


def _mm_tn(a, b):
    return lax.dot_general(a.astype(MM_DTYPE), b.astype(MM_DTYPE), (((0,), (0,)), ((), ())),
                           preferred_element_type=F32)


def _sigmoid(x):
    return 0.5 * jnp.tanh(0.5 * x) + 0.5


_GELU_C = math.sqrt(2.0 / math.pi)
_GELU_A = 0.044715


def _gelu(x):
    th = jnp.tanh(_GELU_C * (x + _GELU_A * (x * x * x)))
    return 0.5 * x * (1.0 + th), th


def _dgelu(x, th):
    return 0.5 * (1.0 + th) + 0.5 * x * (1.0 - th * th) * (_GELU_C * (1.0 + 3.0 * _GELU_A * (x * x)))


def _ln_fwd(x, g, b):
    mu = jnp.mean(x, axis=-1, keepdims=True)
    xc = x - mu
    var = jnp.mean(xc * xc, axis=-1, keepdims=True)
    rstd = lax.rsqrt(var + LN_EPS)
    xhat = xc * rstd
    return xhat * g + b, xhat, rstd


def _ln_bwd(dy, xhat, rstd, g):
    dxhat = dy * g
    m1 = jnp.mean(dxhat, axis=-1, keepdims=True)
    m2 = jnp.mean(dxhat * xhat, axis=-1, keepdims=True)
    return rstd * (dxhat - m1 - xhat * m2)


def _rowsum(x):
    return jnp.sum(x, axis=0, keepdims=True)


def _col(ref, k):
    return ref[:, k * D_G:(k + 1) * D_G]


def _head_of_lane(shape):
    return jnp.right_shift(lax.broadcasted_iota(jnp.int32, shape, len(shape) - 1), HEAD_DIM.bit_length() - 1)


def _pool_select(lane_grp, s2, s4, s8, s16):
    return jnp.where(lane_grp == 0, s2, jnp.where(lane_grp == 1, s4, jnp.where(lane_grp == 2, s8, s16)))


def _row_view(ref, layer):
    return ref.at[pl.ds(layer, 1)]


def _make_residues(ext_ref, res_ref):
    rows = res_ref.shape[1]
    for r in range(1, 8):
        res_ref[r - 1] = ext_ref[pl.ds(r, rows), :]


def _rows_at(ext_ref, res_ref, off, tile):
    a, r = divmod(off, 8)
    if r == 0:
        return ext_ref[pl.ds(off, tile), :]
    return res_ref[r - 1, pl.ds(8 * a, tile), :]


def _residue_scratch(tile):
    return pltpu.VMEM((7, HALO + tile - 8, D_G), F32)


def _branch_forward(p_ref, ph_ref, first, row0, km_ref, vm_ref, w, ext_a, ext_c, ext_d, res_c, res_d, tile,
                    kept_ref=None):
    r = {}
    xa, ba, ca = _col(p_ref, C_XA), _col(p_ref, C_BA), _col(p_ref, C_CA)
    g_a = ca * xa
    ext_a[0:HALO] = jnp.where(first, 0.0, _col(ph_ref, C_CA) * _col(ph_ref, C_XA))
    ext_a[HALO:HALO + tile] = g_a
    conv_a = w["conv_a"][0:1, :] * ext_a[pl.ds(HALO - 2, tile), :]
    for k in range(1, CONV_A):
        conv_a = conv_a + w["conv_a"][k:k + 1, :] * ext_a[pl.ds(HALO - 2 + k, tile), :]
    r.update(xa=xa, ba=ba, ca=ca, g_a=g_a, conv_a=conv_a)
    ya = ba * conv_a

    lane_grp = _head_of_lane((tile, D_G))
    trow = row0 + lax.broadcasted_iota(jnp.int32, (tile, D_G), 0)
    win = _pool_select(lane_grp, 2, 4, 8, 16)
    inv_cnt = 1.0 / jnp.minimum(trow + 1, win).astype(F32)
    if kept_ref is None:
        xc = _col(p_ref, C_XC)
        ext_c[0:HALO] = jnp.where(first, 0.0, _col(ph_ref, C_XC))
        ext_c[HALO:HALO + tile] = xc
        _make_residues(ext_c, res_c)
        acc = xc
        sums = {}
        for k in range(1, POOL_WINDOWS[-1]):
            acc = acc + _rows_at(ext_c, res_c, HALO - k, tile)
            if k + 1 in POOL_WINDOWS:
                sums[k + 1] = acc
        ypre = _pool_select(lane_grp, sums[2], sums[4], sums[8], sums[16]) * inv_cnt - xc
    else:
        ypre = kept_ref[:, D_G:2 * D_G]
    pool_mm = _mm(ypre, w["pool_wbd"][...])
    yc = pool_mm * w["pool_scale"][...]
    r.update(lane_grp=lane_grp, inv_cnt=inv_cnt, ypre=ypre, pool_mm=pool_mm)

    da, dg = _col(p_ref, C_DA), _col(p_ref, C_DG)
    sig_dg = _sigmoid(dg)
    hd = da * sig_dg
    if kept_ref is None:
        ext_d[0:HALO] = jnp.where(first, 0.0, _col(ph_ref, C_DA) * _sigmoid(_col(ph_ref, C_DG)))
        ext_d[HALO:HALO + tile] = hd
        _make_residues(ext_d, res_d)
        conv_d = w["cc_dw_b"][...] + w["cc_dw_w"][0:1, :] * _rows_at(ext_d, res_d, HALO - (CONV_D - 1), tile)
        for j in range(1, CONV_D):
            conv_d = conv_d + w["cc_dw_w"][j:j + 1, :] * _rows_at(ext_d, res_d, HALO - (CONV_D - 1) + j, tile)
    else:
        conv_d = kept_ref[:, 0:D_G]
    r["kept"] = (conv_d, ypre)
    ln_d, xhat_d, rstd_d = _ln_fwd(conv_d, w["cc_ln_g"][...], w["cc_ln_b"][...])
    sig_ln = _sigmoid(ln_d)
    act_d = ln_d * sig_ln
    yd = _mm(act_d, w["cc_pw_w"][...])
    r.update(da=da, sig_dg=sig_dg, hd=hd, ln_d=ln_d, xhat_d=xhat_d, rstd_d=rstd_d, sig_ln=sig_ln, act_d=act_d)

    u, v = _col(p_ref, C_U), _col(p_ref, C_V)
    ug, th_u = _gelu(u)
    vg, th_v = _gelu(v)
    vn, xhat_v, rstd_v = _ln_fwd(vg, w["sg_ln_g"][...], w["sg_ln_b"][...])
    tri = (lax.broadcasted_iota(jnp.int32, (CHUNK, CHUNK), 0)
           >= lax.broadcasted_iota(jnp.int32, (CHUNK, CHUNK), 1))
    wm = [jnp.where(tri, w["sg_w"][h], 0.0).astype(MM_DTYPE) for h in range(N_SUB)]
    lo = lax.broadcasted_iota(jnp.int32, (CHUNK, 2 * HEAD_DIM), 1) < HEAD_DIM
    chunks = []
    for c in range(tile // CHUNK):
        halves = []
        for hf in range(2):
            vh = vn[c * CHUNK:(c + 1) * CHUNK, hf * 128:(hf + 1) * 128]
            halves.append(_mm(wm[2 * hf], jnp.where(lo, vh, 0.0)) + _mm(wm[2 * hf + 1], jnp.where(lo, 0.0, vh)))
        chunks.append(jnp.concatenate(halves, axis=1) + w["sg_bias"][...])
    mixed = jnp.concatenate(chunks, axis=0)
    yb = ug * mixed
    r.update(u=u, v=v, ug=ug, th_u=th_u, th_v=th_v, vn=vn, xhat_v=xhat_v, rstd_v=rstd_v, wm=wm, lo=lo,
             mixed=mixed, tri=tri)

    q = _col(p_ref, C_Q)
    ye = jnp.zeros((tile, D_G), F32)
    probs = []
    for h in range(N_SUB):
        s = _mm_nt(q, km_ref[h]) * ATT_SCALE
        e = jnp.exp(s - jnp.max(s, axis=-1, keepdims=True))
        p = e * (1.0 / jnp.sum(e, axis=-1, keepdims=True))
        probs.append(p)
        ye = ye + _mm(p, vm_ref[h])
    r.update(q=q, probs=probs)

    gate = p_ref[:, C_GATE:C_GATE + D_MIX]
    sig_gate = _sigmoid(gate)
    concat = jnp.concatenate([ya, yb, yc, yd, ye], axis=1)
    r.update(gate=gate, sig_gate=sig_gate, concat=concat)
    return r


_BRANCH_REPL = ("sg_ln_g", "sg_ln_b", "sg_w", "sg_b", "pool_w", "pool_scale", "cc_dw_b", "cc_ln_g", "cc_ln_b")
_BRANCH_W_SCRATCH = (("conv_a", (CONV_A, D_G)), ("cc_dw_w", (CONV_D, D_G)), ("sg_bias", (CHUNK, D_G)),
                     ("pool_wbd", (D_G, D_G)), ("sgb8", (8, CHUNK)))


def _branch_weights(layer, nat, pw_ref, ca_ref, dw_ref, scr, init):
    @pl.when(init)
    def _():
        for p in range(N_DEV):
            scr["conv_a"][:, p * CONV_CH:(p + 1) * CONV_CH] = ca_ref[p, :, layer, :]
            scr["cc_dw_w"][:, p * CONV_CH:(p + 1) * CONV_CH] = dw_ref[p, :, layer, :]
        scr["sgb8"][...] = jnp.zeros((8, CHUNK), F32)
        scr["sgb8"][0:N_SUB] = nat["sg_b"][layer]
        sgb_t = scr["sgb8"][...].T
        head = _head_of_lane((CHUNK, D_G))
        bias = jnp.zeros((CHUNK, D_G), F32)
        for h in range(N_SUB):
            bias = jnp.where(head == h, sgb_t[:, h:h + 1], bias)
        scr["sg_bias"][...] = bias
        scr["pool_wbd"][...] = jnp.zeros((D_G, D_G), F32)
        for gi in range(N_SUB):
            sl = slice(gi * HEAD_DIM, (gi + 1) * HEAD_DIM)
            scr["pool_wbd"][sl, sl] = nat["pool_w"][layer, gi]

    w = {n: _row_view(nat[n], layer) for n in ("sg_ln_g", "sg_ln_b", "pool_scale", "cc_dw_b", "cc_ln_g", "cc_ln_b")}
    w.update(conv_a=scr["conv_a"], cc_dw_w=scr["cc_dw_w"], sg_bias=scr["sg_bias"], pool_wbd=scr["pool_wbd"],
             sg_w=nat["sg_w"].at[layer], cc_pw_w=pw_ref)
    return w


def _full_spec(a):
    nd = a.ndim
    return pl.BlockSpec(a.shape, lambda *_, _nd=nd: (0,) * _nd)


def _tie_specs(ties):
    return [pl.BlockSpec((8, 128), lambda *_: (0, 0)) for _ in ties]


def _params(*sem):
    return pltpu.CompilerParams(dimension_semantics=sem or None, vmem_limit_bytes=VMEM_LIMIT)


def _proj_matmul(x, wt_in, ties=(), tm=512):
    s, k = x.shape

    def body(x_ref, w_ref, *rest):
        o_ref = rest[len(ties)]
        o_ref[...] = _mm_nt(x_ref[...], w_ref[...])

    return pl.pallas_call(
        body, grid=(s // tm,),
        in_specs=[pl.BlockSpec((tm, k), lambda i: (i, 0)), _full_spec(wt_in)] + _tie_specs(ties),
        out_specs=pl.BlockSpec((tm, D_IN), lambda i: (i, 0)),
        out_shape=jax.ShapeDtypeStruct((s, D_IN), F32), name="proj_mm", compiler_params=_params("arbitrary"),
    )(x, wt_in, *ties)


def _kv_project(mem, w_kv):
    def body(mem_ref, w_ref, km_ref, vm_ref):
        kv = _mm(mem_ref[...], w_ref[...])
        k, v = kv[:, :D_G], kv[:, D_G:]
        grp = _head_of_lane((MEM_LEN, D_G))
        for h in range(N_SUB):
            km_ref[h] = jnp.where(grp == h, k, 0.0).astype(km_ref.dtype)
            vm_ref[h] = jnp.where(grp == h, v, 0.0).astype(vm_ref.dtype)

    shp = jax.ShapeDtypeStruct((N_SUB, MEM_LEN, D_G), MM_DTYPE)
    return pl.pallas_call(body, out_shape=(shp, shp), name="kv_project", compiler_params=_params())(mem, w_kv)


def _layer_fwd_fused(x, wt_in, proj, km, vm, layer, repl, pw, ca8, dw8, w_out, want_xn, ties=(), tile=TILE):
    s = x.shape[0]
    hb = tile // HALO
    nat_arrays = [repl[n] for n in _BRANCH_REPL]
    n_nat, nt = len(nat_arrays), len(ties)
    given = proj is not None

    def body(x_ref, *rest):
        if given:
            p_ref, ph_ref = rest[:2]
            rest = rest[2:]
        else:
            wt_ref = rest[0]
            rest = rest[1:]
        km_ref, vm_ref = rest[:2]
        nat = dict(zip(_BRANCH_REPL, rest[2:2 + n_nat]))
        pw_ref, ca_ref, dw_ref, wo_ref, g_ref, b_ref = rest[2 + n_nat:8 + n_nat]
        rest = rest[8 + n_nat + nt:]
        if not given:
            p_ref, rest = rest[0], rest[1:]
        h_ref, z_ref, cd_ref = rest[:3]
        rest = rest[3:]
        if want_xn:
            xn_ref, rest = rest[0], rest[1:]
        if not given:
            ph_ref, rest = rest[0], rest[1:]
        ext_a, ext_c, ext_d, res_c, res_d = rest[:5]
        scr = dict(zip([n for n, _ in _BRANCH_W_SCRATCH], rest[5:]))
        i = pl.program_id(0)
        xt = x_ref[...]
        if not given:
            @pl.when(i == 0)
            def _():
                ph_ref[...] = jnp.zeros_like(ph_ref)

            p_ref[...] = _mm_nt(xt, wt_ref[...])
        w = _branch_weights(layer, nat, pw_ref, ca_ref, dw_ref, scr, i == 0)
        r = _branch_forward(p_ref, ph_ref, i == 0, i * tile, km_ref, vm_ref, w, ext_a, ext_c, ext_d, res_c, res_d,
                            tile)
        if not given:
            ph_ref[...] = p_ref[tile - HALO:tile, :]
        h = (r["concat"] * (r["gate"] * r["sig_gate"])).astype(h_ref.dtype)
        h_ref[...] = h
        cd_ref[:, 0:D_G], cd_ref[:, D_G:2 * D_G] = r["kept"]
        z = ALPHA * xt + _mm(h, wo_ref[...])
        z_ref[...] = z
        if want_xn:
            xn_ref[...] = _ln_fwd(z, _row_view(g_ref, layer)[...], _row_view(b_ref, layer)[...])[0]

    row = lambda i: (i, 0)
    consts = [km, vm] + nat_arrays + [pw, ca8, dw8, w_out, repl["ln_g"], repl["ln_b"]]
    act = jax.ShapeDtypeStruct((s, D_MODEL), F32)
    act_spec = pl.BlockSpec((tile, D_MODEL), row)
    if given:
        lead = [proj, proj]
        lead_specs = [pl.BlockSpec((tile, D_IN), row),
                      pl.BlockSpec((HALO, D_IN), lambda i: (jnp.maximum(i * hb - 1, 0), 0))]
        out_specs, out_shape, scratch = (), (), []
    else:
        lead = [wt_in]
        lead_specs = [_full_spec(wt_in)]
        out_specs = (pl.BlockSpec((tile, D_IN), row),)
        out_shape = (jax.ShapeDtypeStruct((s, D_IN), F32),)
        scratch = [pltpu.VMEM((HALO, D_IN), F32)]
    res = pl.pallas_call(
        body, grid=(s // tile,),
        in_specs=[act_spec] + lead_specs + [_full_spec(a) for a in consts] + _tie_specs(ties),
        out_specs=out_specs + (pl.BlockSpec((tile, D_MIX), row), act_spec, pl.BlockSpec((tile, 2 * D_G), row))
        + ((act_spec,) if want_xn else ()),
        out_shape=out_shape + (jax.ShapeDtypeStruct((s, D_MIX), MM_DTYPE), act, jax.ShapeDtypeStruct((s, 2 * D_G), F32))
        + ((act,) if want_xn else ()),
        scratch_shapes=scratch + [pltpu.VMEM((HALO + tile, D_G), F32)] * 3
        + [_residue_scratch(tile)] * 2 + [pltpu.VMEM(shape, F32) for _, shape in _BRANCH_W_SCRATCH],
        name="layer_fwd_given_proj" if given else "layer_fwd", compiler_params=_params("arbitrary"),
    )(x, *lead, *consts, *ties)
    res = ((proj,) + tuple(res)) if given else tuple(res)
    return res if want_xn else res + (None,)


_BRANCH_GRADS = (("g256", (8, D_G)), ("sg_w", (N_SUB, CHUNK, CHUNK)), ("sg_b", (8, CHUNK)),
                 ("pool_w", (N_SUB, HEAD_DIM, HEAD_DIM)), ("conv_a_w", (N_DEV, CONV_A, CONV_CH)),
                 ("cc_dw_w", (N_DEV, CONV_D, CONV_CH)), ("cc_pw_w", (D_G, D_G)),
                 ("dk", (N_SUB, MEM_LEN, D_G)), ("dv", (N_SUB, MEM_LEN, D_G)))
_BRANCH_ACC = (("conv_a", (CONV_A, D_G)), ("cc_dw_w", (CONV_D, D_G)), ("pool_wbd", (D_G, D_G)),
               ("sg_bias", (CHUNK, D_G)))


def _layer_bwd_fused(up, target, z, h, proj, kept, km, vm, layer, repl, w_out, pw, ca8, dw8, ties=(), tile=TILE):
    s = proj.shape[0]
    nt = s // tile
    hb = tile // HALO
    nat_arrays = [repl[n] for n in _BRANCH_REPL]
    n_nat, n_grads, n_acc, n_ties = len(nat_arrays), len(_BRANCH_GRADS), len(_BRANCH_ACC), len(ties)
    row_of = {n: k for k, n in enumerate(G256_ROWS)}
    from_loss = target is not None

    def body(o_ref, z_ref, h_ref, p_ref, ph_ref, cd_ref, km_ref, vm_ref, *rest):
        nat = dict(zip(_BRANCH_REPL, rest[:n_nat]))
        pw_ref, ca_ref, dw_ref, lng_ref, lnb_ref, wo_ref = rest[n_nat:n_nat + 6]
        rest = rest[n_nat + 6 + n_ties:]
        dz_ref, dp_ref, gw_ref, slab_ref = rest[:4]
        g = dict(zip([n for n, _ in _BRANCH_GRADS], rest[4:4 + n_grads]))
        rest = rest[4 + n_grads:]
        ext_a, rev_a, rev_c, rev_d, res_rc, res_rd, gw_acc, lacc = rest[:8]
        acc = dict(zip([n for n, _ in _BRANCH_ACC], rest[8:8 + n_acc]))
        scr = dict(zip([n for n, _ in _BRANCH_W_SCRATCH], rest[8 + n_acc:]))
        i = pl.program_id(0)
        t = nt - 1 - i

        @pl.when(i == 0)
        def _():
            for ref in list(g.values()) + list(acc.values()) + [rev_a, rev_c, rev_d, gw_acc, slab_ref, lacc]:
                ref[...] = jnp.zeros_like(ref)

        g_ln = _row_view(lng_ref, layer)[...]
        xn, xhat, rstd = _ln_fwd(z_ref[...], g_ln, _row_view(lnb_ref, layer)[...])
        if from_loss:
            err = xn - o_ref[...]
            lacc[...] += _rowsum(err * err)
            dxn = err * (1.0 / D_MODEL)
        else:
            dxn = o_ref[...]
        slab_ref[0:1, :] += _rowsum(dxn * xhat)
        slab_ref[1:2, :] += _rowsum(dxn)
        dz = _ln_bwd(dxn, xhat, rstd, g_ln)
        dz_ref[...] = dz
        dh_all = _mm_nt(dz, wo_ref[...])
        gw_acc[...] += _mm_tn(h_ref[...], dz)

        w = _branch_weights(layer, nat, pw_ref, ca_ref, dw_ref, scr, i == 0)
        r = _branch_forward(p_ref, ph_ref, t == 0, t * tile, km_ref, vm_ref, w, ext_a, None, None, None, None,
                            tile, cd_ref)

        def put(k, val, width=D_G):
            dp_ref[:, k:k + width] = val.astype(dp_ref.dtype)

        def add_row(name, val):
            k = row_of[name]
            g["g256"][k:k + 1, :] += val

        def push_rev(rev, val):
            head = rev[0:HALO]
            rev[tile:tile + HALO] = head
            rev[0:tile] = val

        gate, sig_gate, concat = r["gate"], r["sig_gate"], r["concat"]
        put(C_GATE, dh_all * concat * (sig_gate * (1.0 + gate * (1.0 - sig_gate))), D_MIX)
        dconcat = dh_all * (gate * sig_gate)
        dya, dyb, dyc, dyd, dye = [dconcat[:, k * D_G:(k + 1) * D_G] for k in range(N_GROUPS)]

        put(C_BA * D_G, dya * r["conv_a"])
        dconv_a = dya * r["ba"]
        push_rev(rev_a, dconv_a)
        dga = jnp.zeros((tile, D_G), F32)
        for k in range(CONV_A):
            ahead = rev_a[pl.ds(CONV_A - 1 - k, tile), :]
            dga = dga + w["conv_a"][k:k + 1, :] * ahead
            acc["conv_a"][k:k + 1, :] += _rowsum(r["g_a"] * ahead)
        put(C_CA * D_G, dga * r["xa"])
        put(C_XA * D_G, dga * r["ca"])

        add_row("pool_scale", _rowsum(dyc * r["pool_mm"]))
        dmm = dyc * w["pool_scale"][...]
        acc["pool_wbd"][...] += _mm_tn(r["ypre"], dmm)
        dypre = _mm_nt(dmm, w["pool_wbd"][...])
        dws = dypre * r["inv_cnt"]
        push_rev(rev_c, dws)
        _make_residues(rev_c, res_rc)
        run = dws
        sums = {}
        for k in range(1, POOL_WINDOWS[-1]):
            run = run + _rows_at(rev_c, res_rc, k, tile)
            if k + 1 in POOL_WINDOWS:
                sums[k + 1] = run
        put(C_XC * D_G, _pool_select(r["lane_grp"], sums[2], sums[4], sums[8], sums[16]) - dypre)

        g["cc_pw_w"][...] += _mm_tn(r["act_d"], dyd)
        dact = _mm_nt(dyd, w["cc_pw_w"][...])
        sig_ln, ln_d = r["sig_ln"], r["ln_d"]
        dln = dact * (sig_ln * (1.0 + ln_d * (1.0 - sig_ln)))
        add_row("cc_ln_g", _rowsum(dln * r["xhat_d"]))
        add_row("cc_ln_b", _rowsum(dln))
        dconv_d = _ln_bwd(dln, r["xhat_d"], r["rstd_d"], w["cc_ln_g"][...])
        add_row("cc_dw_b", _rowsum(dconv_d))
        push_rev(rev_d, dconv_d)
        _make_residues(rev_d, res_rd)
        dhd = jnp.zeros((tile, D_G), F32)
        for j in range(CONV_D):
            ahead = _rows_at(rev_d, res_rd, CONV_D - 1 - j, tile)
            dhd = dhd + w["cc_dw_w"][j:j + 1, :] * ahead
            acc["cc_dw_w"][j:j + 1, :] += _rowsum(r["hd"] * ahead)
        sig_dg = r["sig_dg"]
        put(C_DA * D_G, dhd * sig_dg)
        put(C_DG * D_G, dhd * r["da"] * sig_dg * (1.0 - sig_dg))

        dug = dyb * r["mixed"]
        dmixed = dyb * r["ug"]
        wm, lo, vn = r["wm"], r["lo"], r["vn"]
        dvn_chunks = []
        for c in range(tile // CHUNK):
            rows = slice(c * CHUNK, (c + 1) * CHUNK)
            acc["sg_bias"][...] += dmixed[rows, :]
            halves = []
            for hf in range(2):
                cols = slice(hf * 128, (hf + 1) * 128)
                dm = dmixed[rows, cols]
                dm_a, dm_b = jnp.where(lo, dm, 0.0), jnp.where(lo, 0.0, dm)
                vh = vn[rows, cols]
                g["sg_w"][2 * hf] += _mm_nt(dm_a, vh)
                g["sg_w"][2 * hf + 1] += _mm_nt(dm_b, vh)
                halves.append(_mm_tn(wm[2 * hf], dm_a) + _mm_tn(wm[2 * hf + 1], dm_b))
            dvn_chunks.append(jnp.concatenate(halves, axis=1))
        dvn = jnp.concatenate(dvn_chunks, axis=0)
        add_row("sg_ln_g", _rowsum(dvn * r["xhat_v"]))
        add_row("sg_ln_b", _rowsum(dvn))
        dvg = _ln_bwd(dvn, r["xhat_v"], r["rstd_v"], w["sg_ln_g"][...])
        put(C_V * D_G, dvg * _dgelu(r["v"], r["th_v"]))
        put(C_U * D_G, dug * _dgelu(r["u"], r["th_u"]))

        q = r["q"]
        dq = jnp.zeros((tile, D_G), F32)
        for h in range(N_SUB):
            p = r["probs"][h]
            dp = _mm_nt(dye, vm_ref[h])
            g["dv"][h] += _mm_tn(p, dye)
            ds = p * (dp - jnp.sum(dp * p, axis=-1, keepdims=True)) * ATT_SCALE
            dq = dq + _mm(ds, km_ref[h])
            g["dk"][h] += _mm_tn(ds, q)
        put(C_Q * D_G, dq)

        @pl.when(i == nt - 1)
        def _():
            for h in range(N_SUB):
                g["sg_w"][h] = jnp.where(r["tri"], g["sg_w"][h], 0.0)
            lane_head = _head_of_lane((CHUNK, D_G))
            col_of = lax.broadcasted_iota(jnp.int32, (CHUNK, 8), 1)
            ba = acc["sg_bias"][...]
            sgb_t = jnp.zeros((CHUNK, 8), F32)
            for h in range(N_SUB):
                col = jnp.sum(jnp.where(lane_head == h, ba, 0.0), axis=-1, keepdims=True)
                sgb_t = jnp.where(col_of == h, col, sgb_t)
            g["sg_b"][...] = sgb_t.T
            wbd = acc["pool_wbd"][...]
            for gi in range(N_SUB):
                sl = slice(gi * HEAD_DIM, (gi + 1) * HEAD_DIM)
                g["pool_w"][gi] = wbd[sl, sl]
            ca, dw = acc["conv_a"][...], acc["cc_dw_w"][...]
            for p in range(N_DEV):
                g["conv_a_w"][p] = ca[:, p * CONV_CH:(p + 1) * CONV_CH]
                g["cc_dw_w"][p] = dw[:, p * CONV_CH:(p + 1) * CONV_CH]
            gw_ref[...] = gw_acc[...].astype(gw_ref.dtype)
            if from_loss:
                total = jnp.sum(lacc[...], axis=-1, keepdims=True) * (0.5 / D_MODEL)
                slab_ref[LOSS_ROW:LOSS_ROW + 1, :] = jnp.broadcast_to(total, (1, D_MODEL))

    rev = lambda i: (nt - 1 - i, 0)
    fixed = lambda i: (0, 0)
    act_spec = pl.BlockSpec((tile, D_MODEL), rev)
    grad_specs = tuple(pl.BlockSpec(shape, lambda i, _nd=len(shape): (0,) * _nd) for _, shape in _BRANCH_GRADS)
    grad_shapes = tuple(jax.ShapeDtypeStruct(shape, F32) for _, shape in _BRANCH_GRADS)
    consts = [km, vm] + nat_arrays + [pw, ca8, dw8, repl["ln_g"], repl["ln_b"], w_out]
    outs = pl.pallas_call(
        body, grid=(nt,),
        in_specs=[act_spec, act_spec, pl.BlockSpec((tile, D_MIX), rev), pl.BlockSpec((tile, D_IN), rev),
                  pl.BlockSpec((HALO, D_IN), lambda i: (jnp.maximum((nt - 1 - i) * hb - 1, 0), 0)),
                  pl.BlockSpec((tile, 2 * D_G), rev)]
        + [_full_spec(a) for a in consts] + _tie_specs(ties),
        out_specs=(act_spec, pl.BlockSpec((tile, D_IN), rev), pl.BlockSpec((D_MIX, D_MODEL), fixed),
                   pl.BlockSpec((8, D_MODEL), fixed)) + grad_specs,
        out_shape=(jax.ShapeDtypeStruct((s, D_MODEL), F32), jax.ShapeDtypeStruct((s, D_IN), MM_DTYPE),
                   jax.ShapeDtypeStruct((D_MIX, D_MODEL), GRAD_DTYPE), jax.ShapeDtypeStruct((8, D_MODEL), F32))
        + grad_shapes,
        scratch_shapes=[pltpu.VMEM((HALO + tile, D_G), F32)] * 4 + [_residue_scratch(tile)] * 2
        + [pltpu.VMEM((D_MIX, D_MODEL), F32), pltpu.VMEM((1, D_MODEL), F32)]
        + [pltpu.VMEM(shape, F32) for _, shape in _BRANCH_ACC + _BRANCH_W_SCRATCH],
        name="layer_bwd_loss" if from_loss else "layer_bwd", compiler_params=_params("arbitrary"),
    )(target if from_loss else up, z, h, proj, proj, kept, *consts, *ties)
    return outs[0], outs[1], outs[2], outs[3], dict(zip([n for n, _ in _BRANCH_GRADS], outs[4:]))


def _dx_matmul(dproj, wt_in, dz, ties=(), tm=512):
    s = dproj.shape[0]

    def body(dp_ref, w_ref, dz_ref, *rest):
        o_ref = rest[len(ties)]
        o_ref[...] = _mm(dp_ref[...], w_ref[...]) + ALPHA * dz_ref[...]

    row = lambda i: (i, 0)
    return pl.pallas_call(
        body, grid=(s // tm,),
        in_specs=[pl.BlockSpec((tm, D_IN), row), _full_spec(wt_in), pl.BlockSpec((tm, D_MODEL), row)]
        + _tie_specs(ties),
        out_specs=pl.BlockSpec((tm, D_MODEL), row),
        out_shape=jax.ShapeDtypeStruct((s, D_MODEL), F32), name="dx_mm", compiler_params=_params("arbitrary"),
    )(dproj, wt_in, dz, *ties)


def _dw_in_matmul(x, dproj, ties=(), tk=512):
    s = x.shape[0]
    nk = s // tk
    blk = 2 * W_IN_COLS

    def body(x_ref, dp_ref, *rest):
        o_ref, acc = rest[len(ties):]
        k = pl.program_id(0)

        @pl.when(k == 0)
        def _():
            acc[...] = jnp.zeros_like(acc)

        xb = x_ref[...].astype(MM_DTYPE)
        for j in range(D_IN // blk):
            acc[j * blk:(j + 1) * blk, :] += _mm_tn(dp_ref[:, j * blk:(j + 1) * blk], xb)

        @pl.when(k == nk - 1)
        def _():
            o_ref[...] = acc[...].astype(o_ref.dtype)

    return pl.pallas_call(
        body, grid=(nk,),
        in_specs=[pl.BlockSpec((tk, D_MODEL), lambda k: (k, 0)), pl.BlockSpec((tk, D_IN), lambda k: (k, 0))]
        + _tie_specs(ties),
        out_specs=pl.BlockSpec((D_IN, D_MODEL), lambda k: (0, 0)),
        out_shape=jax.ShapeDtypeStruct((D_IN, D_MODEL), GRAD_DTYPE),
        scratch_shapes=[pltpu.VMEM((D_IN, D_MODEL), F32)], name="dw_in_mm", compiler_params=_params("arbitrary"),
    )(x, dproj, *ties)


def _in_bwd(x, dproj, wt_in, dz, tm=512):
    s = x.shape[0]
    n_steps = s // tm

    assert wt_in.dtype == GRAD_DTYPE
    blk = 2 * W_IN_COLS

    def body(x_ref, dp_ref, w_hbm, dz_ref, o_ref, gw_hbm, w_vmem, acc, sem):
        i = pl.program_id(0)

        @pl.when(i == 0)
        def _():
            fetch = pltpu.make_async_copy(w_hbm, w_vmem, sem)
            fetch.start()
            acc[...] = jnp.zeros_like(acc)
            fetch.wait()

        o_ref[...] = _mm(dp_ref[...], w_vmem[...]) + ALPHA * dz_ref[...]
        xb = x_ref[...].astype(MM_DTYPE)
        for j in range(D_IN // blk):
            acc[j * blk:(j + 1) * blk, :] += _mm_tn(dp_ref[:, j * blk:(j + 1) * blk], xb)

        @pl.when(i == n_steps - 1)
        def _():
            w_vmem[...] = acc[...].astype(w_vmem.dtype)
            emit = pltpu.make_async_copy(w_vmem, gw_hbm, sem)
            emit.start()
            emit.wait()

    row = lambda i: (i, 0)
    any_spec = pl.BlockSpec(memory_space=pl.ANY)
    return pl.pallas_call(
        body, grid=(n_steps,),
        in_specs=[pl.BlockSpec((tm, D_MODEL), row), pl.BlockSpec((tm, D_IN), row), any_spec,
                  pl.BlockSpec((tm, D_MODEL), row)],
        out_specs=(pl.BlockSpec((tm, D_MODEL), row), any_spec),
        out_shape=(jax.ShapeDtypeStruct((s, D_MODEL), F32), jax.ShapeDtypeStruct((D_IN, D_MODEL), GRAD_DTYPE)),
        scratch_shapes=[pltpu.VMEM((D_IN, D_MODEL), wt_in.dtype), pltpu.VMEM((D_IN, D_MODEL), F32),
                        pltpu.SemaphoreType.DMA],
        name="in_bwd", compiler_params=_params("arbitrary"),
    )(x, dproj, wt_in, dz)


def _kv_bwd(mem, dk, dv):
    def body(mem_ref, dk_ref, dv_ref, o_ref):
        grp = _head_of_lane((MEM_LEN, D_G))
        dk_sum = jnp.zeros((MEM_LEN, D_G), F32)
        dv_sum = jnp.zeros((MEM_LEN, D_G), F32)
        for h in range(N_SUB):
            dk_sum = dk_sum + jnp.where(grp == h, dk_ref[h], 0.0)
            dv_sum = dv_sum + jnp.where(grp == h, dv_ref[h], 0.0)
        o_ref[...] = _mm_tn(mem_ref[...], jnp.concatenate([dk_sum, dv_sum], axis=1)).astype(o_ref.dtype)

    return pl.pallas_call(body, out_shape=jax.ShapeDtypeStruct((D_MODEL, 2 * D_G), GRAD_DTYPE), name="kv_bwd",
                          compiler_params=_params())(mem, dk, dv)


def _layer_fwd(x, mem, layer, repl, gw, proj=None, km_vm=None, ties=()):
    km, vm = _kv_project(mem, gw["w_kv"]) if km_vm is None else km_vm
    proj, h, z, kept, xn = _layer_fwd_fused(x, gw["wt_in"] if proj is None else None, proj, km, vm, layer, repl,
                                            gw["pw"], gw["ca8"], gw["dw8"], gw["w_out"], layer < DEPTH - 1, ties)
    return xn, (x, proj, h, z, km, vm, kept)


def _layer_bwd_a(up, target, mem, layer, repl, gw, saved, ties=()):
    x_in, proj, h, z, km, vm, kept = saved
    dz, dproj, g_w_out, g1024, bg = _layer_bwd_fused(up, target, z, h, proj, kept, km, vm, layer, repl, gw["w_out"],
                                                     gw["pw"], gw["ca8"], gw["dw8"], ties)
    grads = {n: bg[n] for n in ("g256", "sg_w", "sg_b", "pool_w", "conv_a_w", "cc_dw_w")}
    grads.update(w_out=g_w_out.reshape(N_DEV, D_MIX // N_DEV, D_MODEL), g1024=g1024,
                 w_kv=_kv_bwd(mem, bg["dk"], bg["dv"]).reshape(N_DEV, D_MODEL // N_DEV, 2 * D_G),
                 cc_pw_w=bg["cc_pw_w"].reshape(N_DEV, CONV_CH, D_G))
    return dz, dproj, grads


def _landing_shapes(items):
    out = []
    for a, scatter, pick in items:
        shape = a.shape if scatter else (N_DEV,) + (a.shape if pick is None else a.shape[1:])
        out.append(jax.ShapeDtypeStruct(shape, a.dtype))
    return tuple(out)


def _exchange_sems(n):
    return [pltpu.SemaphoreType.DMA(((N_DEV - 1) * n,)), pltpu.SemaphoreType.DMA(((N_DEV - 1) * n,)),
            pltpu.SemaphoreType.DMA((n,))]


def _exchange_copies(modes, ins, outs, send_sems, recv_sems, local_sems):
    n = len(ins)
    x, y, c = lax.axis_index("x"), lax.axis_index("y"), lax.axis_index("c")
    me = 4 * x + 2 * y + c

    def src_of(a, dest):
        scatter, pick = modes[a]
        if scatter:
            return ins[a].at[dest]
        return ins[a] if pick is None else ins[a].at[pick]

    local = [pltpu.make_async_copy(src_of(a, me), outs[a].at[me], local_sems.at[a]) for a in range(n)]
    sends, recvs = [], []
    for k in range(1, N_DEV):
        px = 1 - x if k & 4 else x
        py = 1 - y if k & 2 else y
        pc = 1 - c if k & 1 else c
        peer = 4 * px + 2 * py + pc
        for a in range(n):
            sems = dict(send_sem=send_sems.at[(k - 1) * n + a], recv_sem=recv_sems.at[(k - 1) * n + a],
                        device_id=(px, py, pc), device_id_type=pl.DeviceIdType.MESH)
            sends.append(pltpu.make_async_remote_copy(src_ref=src_of(a, peer), dst_ref=outs[a].at[me], **sems))
            recvs.append(pltpu.make_async_remote_copy(src_ref=src_of(a, peer), dst_ref=outs[a].at[peer], **sems))
    return local, sends, recvs


def _gather_two_level(items, name):
    n = len(items)
    assert not any(scatter for _, scatter, _ in items)
    picks = [pick for _, _, pick in items]

    def body(*refs):
        ins, outs = refs[:n], refs[n:2 * n]
        send_sems, recv_sems, local_sems = refs[2 * n:]
        x, y, c = lax.axis_index("x"), lax.axis_index("y"), lax.axis_index("c")
        sib = 1 - c
        chips = [(1 - x, y), (x, 1 - y), (1 - x, 1 - y)]

        def slot(a, px, py, pc):
            return outs[a].at[4 * px + 2 * py + pc]

        def copy(k, a, src, block, to):
            return pltpu.make_async_remote_copy(
                src_ref=src, dst_ref=slot(a, *block), send_sem=send_sems.at[k * n + a],
                recv_sem=recv_sems.at[k * n + a], device_id=to, device_id_type=pl.DeviceIdType.MESH)

        own = [ins[a] if picks[a] is None else ins[a].at[picks[a]] for a in range(n)]
        local = [pltpu.make_async_copy(own[a], slot(a, x, y, c), local_sems.at[a]) for a in range(n)]
        first = [copy(0, a, own[a], (x, y, c), (x, y, sib)) for a in range(n)]
        first += [copy(1 + j, a, own[a], (x, y, c), (*chip, c)) for j, chip in enumerate(chips[:2]) for a in range(n)]
        for cp in local + first:
            cp.start()

        def pass_on(j, a):
            chip = chips[j]
            copy(1 + j, a, own[a], (*chip, c), (x, y, c)).wait_recv()
            fwd = copy(4 + j, a, slot(a, *chip, c), (*chip, c), (x, y, sib))
            fwd.start()
            return fwd

        passed = [pass_on(j, a) for j in range(2) for a in range(n)]
        south = c == 0
        via = tuple(jnp.where(south, p, q) for p, q in zip(chips[0], chips[1]))
        blk = tuple(jnp.where(south, q, p) for p, q in zip(chips[0], chips[1]))
        relayed = [copy(3, a, slot(a, *blk, c), (*blk, c), (*via, c)) for a in range(n)]
        for cp in relayed:
            cp.start()
        passed += [pass_on(2, a) for a in range(n)]
        first += relayed
        for a in range(n):
            copy(0, a, own[a], (x, y, sib), (x, y, c)).wait_recv()
        for j, chip in enumerate(chips):
            for a in range(n):
                copy(4 + j, a, own[a], (*chip, sib), (x, y, c)).wait_recv()
        for cp in first + passed:
            cp.wait_send()
        for cp in local:
            cp.wait()

    any_spec = pl.BlockSpec(memory_space=pl.ANY)
    return pl.pallas_call(
        body, in_specs=[any_spec] * n, out_specs=(any_spec,) * n, out_shape=_landing_shapes(items),
        scratch_shapes=[pltpu.SemaphoreType.DMA((7 * n,)), pltpu.SemaphoreType.DMA((7 * n,)),
                        pltpu.SemaphoreType.DMA((n,))],
        name=name,
    )(*[a for a, _, _ in items])


_HBM_SPEC = pl.BlockSpec(memory_space=pltpu.HBM)
_SEM_SPEC = pl.BlockSpec(memory_space=pltpu.SEMAPHORE)
_SPLIT_PARAMS = pltpu.CompilerParams(has_side_effects=pltpu.SideEffectType.DATAFLOW_SIDE_EFFECTING)


def _split_start(srcs, lands, plan, sem_shapes, name):
    n_src, n_land = len(srcs), len(lands)
    n_buf = n_src + n_land
    bufs = [pltpu.with_memory_space_constraint(a, pltpu.HBM) for a in list(srcs) + list(lands)]

    def body(*refs):
        local, sends, _ = plan(refs[:n_src], refs[n_src:n_buf], *refs[n_buf:n_buf + 3])
        for cp in local + sends:
            cp.start()
        token = refs[-1]
        token[...] = jnp.zeros_like(token)

    res = pl.pallas_call(
        body, name=name, in_specs=[_HBM_SPEC] * n_buf,
        out_shape=tuple(sem_shapes) + tuple(pltpu.HBM(a.shape, a.dtype) for a in bufs)
        + (jax.ShapeDtypeStruct((8, 128), F32),),
        out_specs=(_SEM_SPEC,) * 3 + (_HBM_SPEC,) * n_buf + (pl.BlockSpec(memory_space=pltpu.VMEM),),
        input_output_aliases={i: 3 + i for i in range(n_buf)}, compiler_params=_SPLIT_PARAMS,
    )(*bufs)
    return dict(sems=res[:3], srcs=res[3:3 + n_src], lands=res[3 + n_src:3 + n_buf], token=res[-1], plan=plan)


def _split_wait(ticket, after, name):
    n_src, n_land = len(ticket["srcs"]), len(ticket["lands"])
    n_buf = n_src + n_land
    plan = ticket["plan"]

    def body(*refs):
        local, sends, recvs = plan(refs[:n_src], refs[n_src:n_buf], *refs[n_buf:n_buf + 3])
        for cp in recvs:
            cp.wait_recv()
        for cp in sends:
            cp.wait_send()
        for cp in local:
            cp.wait()

    bufs = list(ticket["srcs"]) + list(ticket["lands"])
    res = pl.pallas_call(
        body, name=name, in_specs=[_HBM_SPEC] * n_buf + [_SEM_SPEC] * 3 + [pl.BlockSpec(memory_space=pl.ANY)],
        out_shape=tuple(pltpu.HBM(a.shape, a.dtype) for a in bufs), out_specs=(_HBM_SPEC,) * n_buf,
        input_output_aliases={i: i for i in range(n_buf)}, compiler_params=_SPLIT_PARAMS,
    )(*bufs, *ticket["sems"], after)
    return res[n_src:]


def _empty_landings(items):
    return [lax.empty(s.shape, s.dtype) for s in _landing_shapes(items)]


def _exchange_start(items, name):
    modes = [(scatter, pick) for _, scatter, pick in items]
    plan = lambda ins, outs, *sems: _exchange_copies(modes, ins, outs, *sems)
    return _split_start([a for a, _, _ in items], _empty_landings(items), plan, _exchange_sems(len(items)), name)


def _two_level_plans(picks):
    n = len(picks)

    def place():
        x, y, c = lax.axis_index("x"), lax.axis_index("y"), lax.axis_index("c")
        return x, y, c, 1 - c, [(1 - x, y), (x, 1 - y), (1 - x, 1 - y)]

    def copy(outs, send_sems, recv_sems, k, a, src, block, to):
        px, py, pc = block
        return pltpu.make_async_remote_copy(
            src_ref=src, dst_ref=outs[a].at[4 * px + 2 * py + pc], send_sem=send_sems.at[k * n + a],
            recv_sem=recv_sems.at[k * n + a], device_id=to, device_id_type=pl.DeviceIdType.MESH)

    def between_chips(ins, outs, send_sems, recv_sems, local_sems):
        x, y, c, sib, chips = place()
        own = [ins[a] if picks[a] is None else ins[a].at[picks[a]] for a in range(n)]
        mk = lambda *args: copy(outs, send_sems, recv_sems, *args)
        local = [pltpu.make_async_copy(own[a], outs[a].at[4 * x + 2 * y + c], local_sems.at[a]) for a in range(n)]
        sends = [mk(0, a, own[a], (x, y, c), (x, y, sib)) for a in range(n)]
        sends += [mk(1 + j, a, own[a], (x, y, c), (*chip, c)) for j, chip in enumerate(chips) for a in range(n)]
        recvs = [mk(0, a, own[a], (x, y, sib), (x, y, c)) for a in range(n)]
        recvs += [mk(1 + j, a, own[a], (*chip, c), (x, y, c)) for j, chip in enumerate(chips) for a in range(n)]
        return local, sends, recvs

    def within_chip(ins, outs, send_sems, recv_sems, local_sems):
        x, y, c, sib, chips = place()
        mk = lambda *args: copy(outs, send_sems, recv_sems, *args)
        slot = lambda a, px, py, pc: outs[a].at[4 * px + 2 * py + pc]
        sends = [mk(j, a, slot(a, *chip, c), (*chip, c), (x, y, sib)) for j, chip in enumerate(chips)
                 for a in range(n)]
        recvs = [mk(j, a, slot(a, *chip, c), (*chip, sib), (x, y, c)) for j, chip in enumerate(chips)
                 for a in range(n)]
        return [], sends, recvs

    sems = lambda k: [pltpu.SemaphoreType.DMA((k * n,)), pltpu.SemaphoreType.DMA((k * n,)),
                      pltpu.SemaphoreType.DMA((n,))]
    return between_chips, sems(4), within_chip, sems(3)


def _adam_math(g, w, m, v):
    m_new = ADAM_B1 * m + (1.0 - ADAM_B1) * g
    v_new = ADAM_B2 * v + (1.0 - ADAM_B2) * (g * g)
    m_hat = m_new / (1.0 - ADAM_B1 ** ADAM_STEP)
    v_hat = v_new / (1.0 - ADAM_B2 ** ADAM_STEP)
    return -ADAM_LR * (m_hat / (jnp.sqrt(v_hat) + ADAM_EPS) + ADAM_WD * w), m_new, v_new


def _adamw_big(parts, w, m, v, layer, prev, name, tr):
    depth, rows, cols = w.shape

    def body(p_ref, w_ref, m_ref, v_ref, *rest):
        g_out, d_out, m_out, v_out = rest[len(prev):]
        g = p_ref[0].astype(F32)
        for q in range(1, N_DEV):
            g = g + p_ref[q].astype(F32)
        d, m_new, v_new = _adam_math(g, w_ref[...], m_ref[...], v_ref[...])
        g_out[...] = g
        d_out[...] = d
        m_out[...] = m_new
        v_out[...] = v_new

    blk = pl.BlockSpec((None, tr, cols), lambda i: (layer, i, 0))
    shp = jax.ShapeDtypeStruct((depth, rows, cols), F32)
    return pl.pallas_call(
        body, grid=(rows // tr,),
        in_specs=[pl.BlockSpec((N_DEV, tr, cols), lambda i: (0, i, 0)), blk, blk, blk]
        + [pl.BlockSpec(memory_space=pl.ANY)] * len(prev),
        out_specs=(blk,) * 4, out_shape=(shp,) * 4,
        input_output_aliases={4 + j: j for j in range(len(prev))},
        name=name, compiler_params=_params("arbitrary"),
    )(parts, w, m, v, *prev)


_SMALL_TENSORS = (("conv_a_w", "conv_a_w", None), ("cc_dw_w", "cc_dw_w", None), ("cc_pw_w", "cc_pw_w", None),
                  ("sg_w", "sg_w", None), ("pool_w", "pool_w", None), ("sg_b", "sg_b", None)) \
    + tuple((n, "g256", k) for k, n in enumerate(G256_ROWS)) + tuple((n, "g1024", k) for k, n in enumerate(G1024_ROWS))
_SMALL_LANDINGS = ("conv_a_w", "cc_dw_w", "cc_pw_w", "sg_w", "pool_w", "sg_b", "g256", "g1024")
_TAPS_FIRST = ("conv_a_w", "cc_dw_w")


def _adamw_small(landings, wts, mom, var):
    names = [n for n, _, _ in _SMALL_TENSORS]
    n_land = DEPTH * len(_SMALL_LANDINGS)
    n_t = len(names)

    def body(*refs):
        land = [dict(zip(_SMALL_LANDINGS, refs[l * len(_SMALL_LANDINGS):(l + 1) * len(_SMALL_LANDINGS)]))
                for l in range(DEPTH)]
        w_refs = dict(zip(names, refs[n_land:n_land + n_t]))
        m_refs = dict(zip(names, refs[n_land + n_t:n_land + 2 * n_t]))
        v_refs = dict(zip(names, refs[n_land + 2 * n_t:n_land + 3 * n_t]))
        outs = refs[n_land + 3 * n_t:]
        out_refs = {n: outs[4 * k:4 * k + 4] for k, n in enumerate(names)}
        loss_ref = outs[4 * n_t]
        for name, key, row in _SMALL_TENSORS:
            for l in range(DEPTH):
                src = land[l][key]
                if row is not None:
                    part = lambda q: src[q, row:row + 1, :]
                    at = (slice(l, l + 1),)
                elif name == "sg_b":
                    part = lambda q: src[q, 0:N_SUB, :]
                    at = (l,)
                elif name in _TAPS_FIRST:
                    part = lambda q: src[q]
                    at = (slice(None), l)
                else:
                    part = lambda q: src[q]
                    at = (l,)
                g = part(0)
                for q in range(1, N_DEV):
                    g = g + part(q)
                d, m_new, v_new = _adam_math(g, w_refs[name][at], m_refs[name][at], v_refs[name][at])
                for ref, val in zip(out_refs[name], (g, d, m_new, v_new)):
                    ref[at] = val
        src = land[DEPTH - 1]["g1024"]
        loss = src[0, LOSS_ROW:LOSS_ROW + 1, 0:128]
        for q in range(1, N_DEV):
            loss = loss + src[q, LOSS_ROW:LOSS_ROW + 1, 0:128]
        loss_ref[...] = loss

    ins = [landings[l][k] for l in range(DEPTH) for k in _SMALL_LANDINGS] \
        + [src[n] for src in (wts, mom, var) for n in names]
    out_shape = tuple(jax.ShapeDtypeStruct(wts[n].shape, F32) for n in names for _ in range(4)) \
        + (jax.ShapeDtypeStruct((1, 128), F32),)
    res = pl.pallas_call(body, out_shape=out_shape, name="adamw_small", compiler_params=_params())(*ins)
    return {n: res[4 * k:4 * k + 4] for k, n in enumerate(names)}, res[4 * n_t]


_BIG = (("w_in", 64), ("w_out", 32), ("w_kv", 32))
_GRAD_ITEMS_EARLY = ("w_out", "w_kv", "cc_pw_w", "conv_a_w", "cc_dw_w")
_GRAD_ITEMS_REPL = ("g256", "sg_w", "sg_b", "pool_w", "g1024")


def _grad_items(grads, with_w_in):
    items = [(grads[n], True, None) for n in (("w_in",) if with_w_in else ()) + _GRAD_ITEMS_EARLY]
    return items + [(grads[n], False, None) for n in _GRAD_ITEMS_REPL]


def _landed(parts, with_w_in):
    names = (("w_in",) if with_w_in else ()) + _GRAD_ITEMS_EARLY + _GRAD_ITEMS_REPL
    return dict(zip(names, parts))


def _gathered_weights(wt_in8, w_kv8, w_out8, pw8, ca8, dw8):
    return dict(wt_in=wt_in8.reshape(D_IN, D_MODEL), w_kv=w_kv8.reshape(D_MODEL, 2 * D_G),
                w_out=w_out8.reshape(D_MIX, D_MODEL), pw=pw8.reshape(D_G, D_G), ca8=ca8, dw8=dw8)


def kernel(x, mem, w_in, conv_a_w, sg_ln_g, sg_ln_b, sg_w, sg_b, pool_w, pool_scale, cc_dw_w, cc_dw_b, cc_ln_g, cc_ln_b, cc_pw_w, w_kv, w_out, ln_g, ln_b, loss_target, m_w_in, m_conv_a_w, m_sg_ln_g, m_sg_ln_b, m_sg_w, m_sg_b, m_pool_w, m_pool_scale, m_cc_dw_w, m_cc_dw_b, m_cc_ln_g, m_cc_ln_b, m_cc_pw_w, m_w_kv, m_w_out, m_ln_g, m_ln_b, v_w_in, v_conv_a_w, v_sg_ln_g, v_sg_ln_b, v_sg_w, v_sg_b, v_pool_w, v_pool_scale, v_cc_dw_w, v_cc_dw_b, v_cc_ln_g, v_cc_ln_b, v_cc_pw_w, v_w_kv, v_w_out, v_ln_g, v_ln_b):
    names = ("w_in", "conv_a_w", "sg_ln_g", "sg_ln_b", "sg_w", "sg_b", "pool_w", "pool_scale", "cc_dw_w", "cc_dw_b",
             "cc_ln_g", "cc_ln_b", "cc_pw_w", "w_kv", "w_out", "ln_g", "ln_b")
    wts = dict(zip(names, (w_in, conv_a_w, sg_ln_g, sg_ln_b, sg_w, sg_b, pool_w, pool_scale, cc_dw_w, cc_dw_b,
                           cc_ln_g, cc_ln_b, cc_pw_w, w_kv, w_out, ln_g, ln_b)))
    mom = dict(zip(names, (m_w_in, m_conv_a_w, m_sg_ln_g, m_sg_ln_b, m_sg_w, m_sg_b, m_pool_w, m_pool_scale,
                           m_cc_dw_w, m_cc_dw_b, m_cc_ln_g, m_cc_ln_b, m_cc_pw_w, m_w_kv, m_w_out, m_ln_g, m_ln_b)))
    var = dict(zip(names, (v_w_in, v_conv_a_w, v_sg_ln_g, v_sg_ln_b, v_sg_w, v_sg_b, v_pool_w, v_pool_scale,
                           v_cc_dw_w, v_cc_dw_b, v_cc_ln_g, v_cc_ln_b, v_cc_pw_w, v_w_kv, v_w_out, v_ln_g, v_ln_b)))
    repl = wts
    xs, mems, tgt = x[0], mem[0], loss_target[0]
    turned = {"w_in": (0, 2, 1), "conv_a_w": (1, 0, 2), "cc_dw_w": (1, 0, 2)}
    wts, mom, var = [{n: (jnp.transpose(a, turned[n]) if n in turned else a) for n, a in src.items()}
                     for src in (wts, mom, var)]
    wb = {n: wts[n].astype(MM_DTYPE) for n in ("w_in", "w_kv", "w_out", "cc_pw_w")}

    wt8_0, wkv8_0 = _gather_two_level([(wb["w_in"], False, 0), (wb["w_kv"], False, 0)], "gather_weights_0a")
    rest_0 = _exchange_start([(wb["w_out"], False, 0), (wb["cc_pw_w"], False, 0), (wts["conv_a_w"], False, None),
                              (wts["cc_dw_w"], False, None)], "gather_weights_0b_start")
    km_vm0 = _kv_project(mems, wkv8_0.reshape(D_MODEL, 2 * D_G))
    proj0 = _proj_matmul(xs, wt8_0.reshape(D_IN, D_MODEL), (rest_0["token"],))
    wo8_0, pw8_0, ca8, dw8 = _split_wait(rest_0, proj0, "gather_weights_0b_wait")
    gw0 = _gathered_weights(wt8_0, wkv8_0, wo8_0, pw8_0, ca8, dw8)
    items_1 = [(wb[n], False, 1) for n in ("w_in", "w_kv", "w_out", "cc_pw_w")]
    between_chips, sems_a, within_chip, sems_b = _two_level_plans([1] * len(items_1))
    chips_1 = _split_start([a for a, _, _ in items_1], _empty_landings(items_1), between_chips, sems_a,
                           "gather_weights_1a_start")
    x1, saved0 = _layer_fwd(xs, mems, 0, repl, gw0, proj0, km_vm0, (chips_1["token"],))
    core_1 = _split_start([], _split_wait(chips_1, x1, "gather_weights_1a_wait"), within_chip, sems_b,
                          "gather_weights_1b_start")
    gw1 = _gathered_weights(*_split_wait(core_1, core_1["token"], "gather_weights_1b_wait"), ca8, dw8)
    _, saved1 = _layer_fwd(x1, mems, 1, repl, gw1)

    dz1, dproj1, g1 = _layer_bwd_a(None, tgt, mems, 1, repl, gw1, saved1)
    shards = lambda g: g.reshape(N_DEV, W_IN_COLS, D_MODEL)
    up, g_wt_in_1 = _in_bwd(saved1[0], dproj1, gw1["wt_in"], dz1)
    g1["w_in"] = shards(g_wt_in_1)
    grads_1 = _exchange_start(_grad_items(g1, True), "exchange_grads_1_start")
    dz0, dproj0, g0 = _layer_bwd_a(up, None, mems, 0, repl, gw0, saved0, (grads_1["token"],))
    early_0 = _exchange_start(_grad_items(g0, False), "exchange_grads_0a_start")
    g_wt_in_0 = _dw_in_matmul(saved0[0], dproj0, (early_0["token"],))
    late_0 = _exchange_start([(shards(g_wt_in_0), True, None)], "exchange_grads_0b_start")
    grad_x = _dx_matmul(dproj0, gw0["wt_in"], dz0, (late_0["token"],))

    landed = [None, _landed(_split_wait(grads_1, grad_x, "exchange_grads_1_wait"), True)]
    big = {}
    for n, tr in _BIG:
        big[n] = _adamw_big(landed[1][n], wts[n], mom[n], var[n], 1, (), "adamw_" + n + "_1", tr)
    landed[0] = _landed(_split_wait(early_0, big["w_kv"][0], "exchange_grads_0a_wait"), False)
    for n, tr in _BIG[1:]:
        big[n] = _adamw_big(landed[0][n], wts[n], mom[n], var[n], 0, big[n], "adamw_" + n + "_0", tr)
    small, loss = _adamw_small(landed, wts, mom, var)
    (landed[0]["w_in"],) = _split_wait(late_0, loss, "exchange_grads_0b_wait")
    big["w_in"] = _adamw_big(landed[0]["w_in"], wts["w_in"], mom["w_in"], var["w_in"], 0, big["w_in"],
                             "adamw_w_in_0", _BIG[0][1])

    res = {**small, **big}
    res = {n: ([jnp.transpose(a, turned[n]) for a in r] if n in turned else r) for n, r in res.items()}
    return (loss[0, 0], grad_x[None], *[res[n][0] for n in names], *[res[n][1] for n in names],
            *[res[n][2] for n in names], *[res[n][3] for n in names])
```

```python
import math

import jax
import jax.numpy as jnp
from jax import lax
from jax.experimental import pallas as pl
from jax.experimental.pallas import tpu as pltpu

F32 = jnp.float32
MM_DTYPE = jnp.bfloat16
GRAD_DTYPE = jnp.bfloat16

D_MODEL = 1024
DEPTH = 2
D_G = 256
N_GROUPS = 5
D_MIX = N_GROUPS * D_G
N_SUB = 4
HEAD_DIM = D_G // N_SUB
CONV_A = 3
CONV_D = 31
CHUNK = 128
POOL_WINDOWS = (2, 4, 8, 16)
MEM_LEN = 256
LN_EPS = 1e-5
ALPHA = (2.0 * DEPTH) ** 0.25
D_IN = 9 * D_G + D_MIX
ATT_SCALE = 1.0 / math.sqrt(HEAD_DIM)

ADAM_LR = 0.001
ADAM_B1 = 0.9
ADAM_B2 = 0.999
ADAM_EPS = 1e-08
ADAM_WD = 0.01
ADAM_STEP = 10

N_DEV = 8
W_IN_COLS = D_IN // N_DEV
CONV_CH = D_G // N_DEV
HALO = 32
TILE = 256
VMEM_LIMIT = 56 * 1024 * 1024

C_XA, C_BA, C_CA, C_U, C_V, C_XC, C_DA, C_DG, C_Q = range(9)
C_GATE = 9 * D_G

G256_ROWS = ("sg_ln_g", "sg_ln_b", "pool_scale", "cc_dw_b", "cc_ln_g", "cc_ln_b")
G1024_ROWS = ("ln_g", "ln_b")
LOSS_ROW = 2


def _mm(a, b):
    return jnp.dot(a.astype(MM_DTYPE), b.astype(MM_DTYPE), preferred_element_type=F32)


def _mm_nt(a, b):
    return lax.dot_general(a.astype(MM_DTYPE), b.astype(MM_DTYPE), (((1,), (1,)), ((), ())),
                           preferred_element_type=F32)


def _mm_tn(a, b):
    return lax.dot_general(a.astype(MM_DTYPE), b.astype(MM_DTYPE), (((0,), (0,)), ((), ())),
                           preferred_element_type=F32)


def _sigmoid(x):
    return 0.5 * jnp.tanh(0.5 * x) + 0.5


_GELU_C = math.sqrt(2.0 / math.pi)
_GELU_A = 0.044715


def _gelu(x):
    th = jnp.tanh(_GELU_C * (x + _GELU_A * (x * x * x)))
    return 0.5 * x * (1.0 + th), th


def _dgelu(x, th):
    return 0.5 * (1.0 + th) + 0.5 * x * (1.0 - th * th) * (_GELU_C * (1.0 + 3.0 * _GELU_A * (x * x)))


def _ln_fwd(x, g, b):
    mu = jnp.mean(x, axis=-1, keepdims=True)
    xc = x - mu
    var = jnp.mean(xc * xc, axis=-1, keepdims=True)
    rstd = lax.rsqrt(var + LN_EPS)
    xhat = xc * rstd
    return xhat * g + b, xhat, rstd


def _ln_bwd(dy, xhat, rstd, g):
    dxhat = dy * g
    m1 = jnp.mean(dxhat, axis=-1, keepdims=True)
    m2 = jnp.mean(dxhat * xhat, axis=-1, keepdims=True)
    return rstd * (dxhat - m1 - xhat * m2)


def _rowsum(x):
    return jnp.sum(x, axis=0, keepdims=True)


def _col(ref, k):
    return ref[:, k * D_G:(k + 1) * D_G]


def _head_of_lane(shape):
    return jnp.right_shift(lax.broadcasted_iota(jnp.int32, shape, len(shape) - 1), HEAD_DIM.bit_length() - 1)


def _pool_select(lane_grp, s2, s4, s8, s16):
    return jnp.where(lane_grp == 0, s2, jnp.where(lane_grp == 1, s4, jnp.where(lane_grp == 2, s8, s16)))


def _row_view(ref, layer):
    return ref.at[pl.ds(layer, 1)]


def _make_residues(ext_ref, res_ref):
    rows = res_ref.shape[1]
    for r in range(1, 8):
        res_ref[r - 1] = ext_ref[pl.ds(r, rows), :]


def _rows_at(ext_ref, res_ref, off, tile):
    a, r = divmod(off, 8)
    if r == 0:
        return ext_ref[pl.ds(off, tile), :]
    return res_ref[r - 1, pl.ds(8 * a, tile), :]


def _residue_scratch(tile):
    return pltpu.VMEM((7, HALO + tile - 8, D_G), F32)


PROJ_SEGMENTS = ((C_XA * D_G, (C_CA + 1) * D_G), (C_XC * D_G, (C_XC + 1) * D_G), (C_DA * D_G, (C_DG + 1) * D_G),
                 (C_U * D_G, (C_V + 1) * D_G), (C_Q * D_G, (C_Q + 1) * D_G), (C_GATE, D_IN))


def _branch_forward(p_ref, ph_ref, first, row0, km_ref, vm_ref, w, ext_a, ext_c, ext_d, res_c, res_d, tile,
                    kept_ref=None, produce=None):
    r = {}
    produce = produce or (lambda: None)
    produce()
    produce()
    xa, ba, ca = _col(p_ref, C_XA), _col(p_ref, C_BA), _col(p_ref, C_CA)
    g_a = ca * xa
    ext_a[0:HALO] = jnp.where(first, 0.0, _col(ph_ref, C_CA) * _col(ph_ref, C_XA))
    ext_a[HALO:HALO + tile] = g_a
    conv_a = w["conv_a"][0:1, :] * ext_a[pl.ds(HALO - 2, tile), :]
    for k in range(1, CONV_A):
        conv_a = conv_a + w["conv_a"][k:k + 1, :] * ext_a[pl.ds(HALO - 2 + k, tile), :]
    r.update(xa=xa, ba=ba, ca=ca, g_a=g_a, conv_a=conv_a)
    ya = ba * conv_a

    produce()
    lane_grp = _head_of_lane((tile, D_G))
    trow = row0 + lax.broadcasted_iota(jnp.int32, (tile, D_G), 0)
    win = _pool_select(lane_grp, 2, 4, 8, 16)
    inv_cnt = 1.0 / jnp.minimum(trow + 1, win).astype(F32)
    if kept_ref is None:
        xc = _col(p_ref, C_XC)
        ext_c[0:HALO] = jnp.where(first, 0.0, _col(ph_ref, C_XC))
        ext_c[HALO:HALO + tile] = xc
        _make_residues(ext_c, res_c)
        acc = xc
        sums = {}
        for k in range(1, POOL_WINDOWS[-1]):
            acc = acc + _rows_at(ext_c, res_c, HALO - k, tile)
            if k + 1 in POOL_WINDOWS:
                sums[k + 1] = acc
        ypre = _pool_select(lane_grp, sums[2], sums[4], sums[8], sums[16]) * inv_cnt - xc
    else:
        ypre = kept_ref[:, D_G:2 * D_G]
    pool_mm = _mm(ypre, w["pool_wbd"][...])
    yc = pool_mm * w["pool_scale"][...]
    r.update(lane_grp=lane_grp, inv_cnt=inv_cnt, ypre=ypre, pool_mm=pool_mm)

    produce()
    da, dg = _col(p_ref, C_DA), _col(p_ref, C_DG)
    sig_dg = _sigmoid(dg)
    hd = da * sig_dg
    if kept_ref is None:
        ext_d[0:HALO] = jnp.where(first, 0.0, _col(ph_ref, C_DA) * _sigmoid(_col(ph_ref, C_DG)))
        ext_d[HALO:HALO + tile] = hd
        _make_residues(ext_d, res_d)
        conv_d = w["cc_dw_b"][...] + w["cc_dw_w"][0:1, :] * _rows_at(ext_d, res_d, HALO - (CONV_D - 1), tile)
        for j in range(1, CONV_D):
            conv_d = conv_d + w["cc_dw_w"][j:j + 1, :] * _rows_at(ext_d, res_d, HALO - (CONV_D - 1) + j, tile)
    else:
        conv_d = kept_ref[:, 0:D_G]
    r["kept"] = (conv_d, ypre)
    ln_d, xhat_d, rstd_d = _ln_fwd(conv_d, w["cc_ln_g"][...], w["cc_ln_b"][...])
    sig_ln = _sigmoid(ln_d)
    act_d = ln_d * sig_ln
    yd = _mm(act_d, w["cc_pw_w"][...])
    r.update(da=da, sig_dg=sig_dg, hd=hd, ln_d=ln_d, xhat_d=xhat_d, rstd_d=rstd_d, sig_ln=sig_ln, act_d=act_d)
    produce()

    u, v = _col(p_ref, C_U), _col(p_ref, C_V)
    ug, th_u = _gelu(u)
    vg, th_v = _gelu(v)
    vn, xhat_v, rstd_v = _ln_fwd(vg, w["sg_ln_g"][...], w["sg_ln_b"][...])
    tri = (lax.broadcasted_iota(jnp.int32, (CHUNK, CHUNK), 0)
           >= lax.broadcasted_iota(jnp.int32, (CHUNK, CHUNK), 1))
    wm = [jnp.where(tri, w["sg_w"][h], 0.0).astype(MM_DTYPE) for h in range(N_SUB)]
    lo = lax.broadcasted_iota(jnp.int32, (CHUNK, 2 * HEAD_DIM), 1) < HEAD_DIM
    chunks = []
    for c in range(tile // CHUNK):
        halves = []
        for hf in range(2):
            vh = vn[c * CHUNK:(c + 1) * CHUNK, hf * 128:(hf + 1) * 128]
            halves.append(_mm(wm[2 * hf], jnp.where(lo, vh, 0.0)) + _mm(wm[2 * hf + 1], jnp.where(lo, 0.0, vh)))
        chunks.append(jnp.concatenate(halves, axis=1) + w["sg_bias"][...])
    mixed = jnp.concatenate(chunks, axis=0)
    yb = ug * mixed
    r.update(u=u, v=v, ug=ug, th_u=th_u, th_v=th_v, vn=vn, xhat_v=xhat_v, rstd_v=rstd_v, wm=wm, lo=lo,
             mixed=mixed, tri=tri)

    produce()
    q = _col(p_ref, C_Q)
    ye = jnp.zeros((tile, D_G), F32)
    probs = []
    for h in range(N_SUB):
        s = _mm_nt(q, km_ref[h]) * ATT_SCALE
        e = jnp.exp(s - jnp.max(s, axis=-1, keepdims=True))
        p = e * (1.0 / jnp.sum(e, axis=-1, keepdims=True))
        probs.append(p)
        ye = ye + _mm(p, vm_ref[h])
    r.update(q=q, probs=probs)

    gate = p_ref[:, C_GATE:C_GATE + D_MIX]
    sig_gate = _sigmoid(gate)
    r.update(gate=gate, sig_gate=sig_gate, branch_out=(ya, yb, yc, yd, ye))
    return r


_BRANCH_REPL = ("sg_ln_g", "sg_ln_b", "sg_w", "sg_b", "pool_w", "pool_scale", "cc_dw_b", "cc_ln_g", "cc_ln_b")
_BRANCH_W_SCRATCH = (("conv_a", (CONV_A, D_G)), ("cc_dw_w", (CONV_D, D_G)), ("sg_bias", (CHUNK, D_G)),
                     ("pool_wbd", (D_G, D_G)), ("sgb8", (8, CHUNK)))


def _branch_weights(layer, nat, pw_ref, ca_ref, dw_ref, scr, init):
    @pl.when(init)
    def _():
        for p in range(N_DEV):
            scr["conv_a"][:, p * CONV_CH:(p + 1) * CONV_CH] = ca_ref[p, :, layer, :]
            scr["cc_dw_w"][:, p * CONV_CH:(p + 1) * CONV_CH] = dw_ref[p, :, layer, :]
        scr["sgb8"][...] = jnp.zeros((8, CHUNK), F32)
        scr["sgb8"][0:N_SUB] = nat["sg_b"][layer]
        sgb_t = scr["sgb8"][...].T
        head = _head_of_lane((CHUNK, D_G))
        bias = jnp.zeros((CHUNK, D_G), F32)
        for h in range(N_SUB):
            bias = jnp.where(head == h, sgb_t[:, h:h + 1], bias)
        scr["sg_bias"][...] = bias
        scr["pool_wbd"][...] = jnp.zeros((D_G, D_G), F32)
        for gi in range(N_SUB):
            sl = slice(gi * HEAD_DIM, (gi + 1) * HEAD_DIM)
            scr["pool_wbd"][sl, sl] = nat["pool_w"][layer, gi]

    w = {n: _row_view(nat[n], layer) for n in ("sg_ln_g", "sg_ln_b", "pool_scale", "cc_dw_b", "cc_ln_g", "cc_ln_b")}
    w.update(conv_a=scr["conv_a"], cc_dw_w=scr["cc_dw_w"], sg_bias=scr["sg_bias"], pool_wbd=scr["pool_wbd"],
             sg_w=nat["sg_w"].at[layer], cc_pw_w=pw_ref)
    return w


def _full_spec(a):
    nd = a.ndim
    return pl.BlockSpec(a.shape, lambda *_, _nd=nd: (0,) * _nd)


def _tie_specs(ties):
    return [pl.BlockSpec((8, 128), lambda *_: (0, 0)) for _ in ties]


def _params(*sem):
    return pltpu.CompilerParams(dimension_semantics=sem or None, vmem_limit_bytes=VMEM_LIMIT)


def _proj_matmul(x, wt_in, ties=(), tm=512):
    s, k = x.shape

    def body(x_ref, w_ref, *rest):
        o_ref = rest[len(ties)]
        o_ref[...] = _mm_nt(x_ref[...], w_ref[...])

    return pl.pallas_call(
        body, grid=(s // tm,),
        in_specs=[pl.BlockSpec((tm, k), lambda i: (i, 0)), _full_spec(wt_in)] + _tie_specs(ties),
        out_specs=pl.BlockSpec((tm, D_IN), lambda i: (i, 0)),
        out_shape=jax.ShapeDtypeStruct((s, D_IN), F32), name="proj_mm", compiler_params=_params("arbitrary"),
    )(x, wt_in, *ties)


def _kv_project(mem, w_kv):
    def body(mem_ref, w_ref, km_ref, vm_ref):
        kv = _mm(mem_ref[...], w_ref[...])
        k, v = kv[:, :D_G], kv[:, D_G:]
        grp = _head_of_lane((MEM_LEN, D_G))
        for h in range(N_SUB):
            km_ref[h] = jnp.where(grp == h, k, 0.0).astype(km_ref.dtype)
            vm_ref[h] = jnp.where(grp == h, v, 0.0).astype(vm_ref.dtype)

    shp = jax.ShapeDtypeStruct((N_SUB, MEM_LEN, D_G), MM_DTYPE)
    return pl.pallas_call(body, out_shape=(shp, shp), name="kv_project", compiler_params=_params())(mem, w_kv)


def _layer_fwd_fused(x, wt_in, proj, km, vm, layer, repl, pw, ca8, dw8, w_out, want_xn, ties=(), tile=TILE):
    s = x.shape[0]
    hb = tile // HALO
    nat_arrays = [repl[n] for n in _BRANCH_REPL]
    n_nat, nt = len(nat_arrays), len(ties)
    given = proj is not None

    def body(x_ref, *rest):
        if given:
            p_ref, ph_ref = rest[:2]
            rest = rest[2:]
        else:
            wt_ref = rest[0]
            rest = rest[1:]
        km_ref, vm_ref = rest[:2]
        nat = dict(zip(_BRANCH_REPL, rest[2:2 + n_nat]))
        pw_ref, ca_ref, dw_ref, wo_ref, g_ref, b_ref = rest[2 + n_nat:8 + n_nat]
        rest = rest[8 + n_nat + nt:]
        if not given:
            p_ref, rest = rest[0], rest[1:]
        h_ref, z_ref, cd_ref = rest[:3]
        rest = rest[3:]
        if want_xn:
            xn_ref, rest = rest[0], rest[1:]
        if not given:
            ph_ref, rest = rest[0], rest[1:]
        ext_a, ext_c, ext_d, res_c, res_d = rest[:5]
        scr = dict(zip([n for n, _ in _BRANCH_W_SCRATCH], rest[5:]))
        i = pl.program_id(0)
        xt = x_ref[...]
        produce = None
        if not given:
            @pl.when(i == 0)
            def _():
                ph_ref[...] = jnp.zeros_like(ph_ref)

            xb = xt.astype(MM_DTYPE)
            segments = iter(PROJ_SEGMENTS)

            def produce():
                lo, hi = next(segments)
                p_ref[:, lo:hi] = _mm_nt(xb, wt_ref[lo:hi, :])

        w = _branch_weights(layer, nat, pw_ref, ca_ref, dw_ref, scr, i == 0)
        r = _branch_forward(p_ref, ph_ref, i == 0, i * tile, km_ref, vm_ref, w, ext_a, ext_c, ext_d, res_c, res_d,
                            tile, None, produce)
        if not given:
            ph_ref[...] = p_ref[tile - HALO:tile, :]
        cd_ref[:, 0:D_G], cd_ref[:, D_G:2 * D_G] = r["kept"]
        h = (jnp.concatenate(r["branch_out"], axis=1) * (r["gate"] * r["sig_gate"])).astype(h_ref.dtype)
        h_ref[...] = h
        z = ALPHA * xt + _mm(h, wo_ref[...])
        z_ref[...] = z
        if want_xn:
            xn_ref[...] = _ln_fwd(z, _row_view(g_ref, layer)[...], _row_view(b_ref, layer)[...])[0]

    row = lambda i: (i, 0)
    consts = [km, vm] + nat_arrays + [pw, ca8, dw8, w_out, repl["ln_g"], repl["ln_b"]]
    act = jax.ShapeDtypeStruct((s, D_MODEL), F32)
    act_spec = pl.BlockSpec((tile, D_MODEL), row)
    if given:
        lead = [proj, proj]
        lead_specs = [pl.BlockSpec((tile, D_IN), row),
                      pl.BlockSpec((HALO, D_IN), lambda i: (jnp.maximum(i * hb - 1, 0), 0))]
        out_specs, out_shape, scratch = (), (), []
    else:
        lead = [wt_in]
        lead_specs = [_full_spec(wt_in)]
        out_specs = (pl.BlockSpec((tile, D_IN), row),)
        out_shape = (jax.ShapeDtypeStruct((s, D_IN), F32),)
        scratch = [pltpu.VMEM((HALO, D_IN), F32)]
    res = pl.pallas_call(
        body, grid=(s // tile,),
        in_specs=[act_spec] + lead_specs + [_full_spec(a) for a in consts] + _tie_specs(ties),
        out_specs=out_specs + (pl.BlockSpec((tile, D_MIX), row), act_spec, pl.BlockSpec((tile, 2 * D_G), row))
        + ((act_spec,) if want_xn else ()),
        out_shape=out_shape + (jax.ShapeDtypeStruct((s, D_MIX), MM_DTYPE), act, jax.ShapeDtypeStruct((s, 2 * D_G), F32))
        + ((act,) if want_xn else ()),
        scratch_shapes=scratch + [pltpu.VMEM((HALO + tile, D_G), F32)] * 3
        + [_residue_scratch(tile)] * 2 + [pltpu.VMEM(shape, F32) for _, shape in _BRANCH_W_SCRATCH],
        name="layer_fwd_given_proj" if given else "layer_fwd", compiler_params=_params("arbitrary"),
    )(x, *lead, *consts, *ties)
    res = ((proj,) + tuple(res)) if given else tuple(res)
    return res if want_xn else res + (None,)


_BRANCH_GRADS = (("g256", (8, D_G)), ("sg_w", (N_SUB, CHUNK, CHUNK)), ("sg_b", (8, CHUNK)),
                 ("pool_w", (N_SUB, HEAD_DIM, HEAD_DIM)), ("conv_a_w", (N_DEV, CONV_A, CONV_CH)),
                 ("cc_dw_w", (N_DEV, CONV_D, CONV_CH)), ("cc_pw_w", (D_G, D_G)),
                 ("dk", (N_SUB, MEM_LEN, D_G)), ("dv", (N_SUB, MEM_LEN, D_G)))
_BRANCH_ACC = (("conv_a", (CONV_A, D_G)), ("cc_dw_w", (CONV_D, D_G)), ("pool_wbd", (D_G, D_G)),
               ("sg_bias", (CHUNK, D_G)))


def _layer_bwd_fused(up, target, z, h, proj, kept, km, vm, layer, repl, w_out, pw, ca8, dw8, ties=(), tile=TILE):
    s = proj.shape[0]
    nt = s // tile
    hb = tile // HALO
    nat_arrays = [repl[n] for n in _BRANCH_REPL]
    n_nat, n_grads, n_acc, n_ties = len(nat_arrays), len(_BRANCH_GRADS), len(_BRANCH_ACC), len(ties)
    row_of = {n: k for k, n in enumerate(G256_ROWS)}
    from_loss = target is not None

    def body(o_ref, z_ref, h_ref, p_ref, ph_ref, cd_ref, km_ref, vm_ref, *rest):
        nat = dict(zip(_BRANCH_REPL, rest[:n_nat]))
        pw_ref, ca_ref, dw_ref, lng_ref, lnb_ref, wo_ref = rest[n_nat:n_nat + 6]
        rest = rest[n_nat + 6 + n_ties:]
        dz_ref, dp_ref, gw_ref, slab_ref = rest[:4]
        g = dict(zip([n for n, _ in _BRANCH_GRADS], rest[4:4 + n_grads]))
        rest = rest[4 + n_grads:]
        ext_a, rev_a, rev_c, rev_d, res_rc, res_rd, gw_acc, lacc = rest[:8]
        acc = dict(zip([n for n, _ in _BRANCH_ACC], rest[8:8 + n_acc]))
        scr = dict(zip([n for n, _ in _BRANCH_W_SCRATCH], rest[8 + n_acc:]))
        i = pl.program_id(0)
        t = nt - 1 - i

        @pl.when(i == 0)
        def _():
            for ref in list(g.values()) + list(acc.values()) + [rev_a, rev_c, rev_d, gw_acc, slab_ref, lacc]:
                ref[...] = jnp.zeros_like(ref)

        g_ln = _row_view(lng_ref, layer)[...]
        xn, xhat, rstd = _ln_fwd(z_ref[...], g_ln, _row_view(lnb_ref, layer)[...])
        if from_loss:
            err = xn - o_ref[...]
            lacc[...] += _rowsum(err * err)
            dxn = err * (1.0 / D_MODEL)
        else:
            dxn = o_ref[...]
        slab_ref[0:1, :] += _rowsum(dxn * xhat)
        slab_ref[1:2, :] += _rowsum(dxn)
        dz = _ln_bwd(dxn, xhat, rstd, g_ln)
        dz_ref[...] = dz
        dzb = dz.astype(MM_DTYPE)

        w = _branch_weights(layer, nat, pw_ref, ca_ref, dw_ref, scr, i == 0)
        r = _branch_forward(p_ref, ph_ref, t == 0, t * tile, km_ref, vm_ref, w, ext_a, None, None, None, None,
                            tile, cd_ref)

        def put(k, val, width=D_G):
            dp_ref[:, k:k + width] = val.astype(dp_ref.dtype)

        def add_row(name, val):
            k = row_of[name]
            g["g256"][k:k + 1, :] += val

        def push_rev(rev, val):
            head = rev[0:HALO]
            rev[tile:tile + HALO] = head
            rev[0:tile] = val

        gate, sig_gate = r["gate"], r["sig_gate"]

        def branch_grad(group):
            cols = slice(group * D_G, (group + 1) * D_G)
            dh_g = _mm_nt(dzb, wo_ref[cols, :])
            gate_g, sig_g = gate[:, cols], sig_gate[:, cols]
            put(C_GATE + group * D_G, dh_g * r["branch_out"][group] * (sig_g * (1.0 + gate_g * (1.0 - sig_g))))
            return dh_g * (gate_g * sig_g)

        dya = branch_grad(0)
        dyc = branch_grad(2)

        put(C_BA * D_G, dya * r["conv_a"])
        dconv_a = dya * r["ba"]
        push_rev(rev_a, dconv_a)
        dga = jnp.zeros((tile, D_G), F32)
        for k in range(CONV_A):
            ahead = rev_a[pl.ds(CONV_A - 1 - k, tile), :]
            dga = dga + w["conv_a"][k:k + 1, :] * ahead
            acc["conv_a"][k:k + 1, :] += _rowsum(r["g_a"] * ahead)
        put(C_CA * D_G, dga * r["xa"])
        put(C_XA * D_G, dga * r["ca"])

        dyd = branch_grad(3)
        add_row("pool_scale", _rowsum(dyc * r["pool_mm"]))
        dmm = dyc * w["pool_scale"][...]
        acc["pool_wbd"][...] += _mm_tn(r["ypre"], dmm)
        dypre = _mm_nt(dmm, w["pool_wbd"][...])
        dws = dypre * r["inv_cnt"]
        push_rev(rev_c, dws)
        _make_residues(rev_c, res_rc)
        run = dws
        sums = {}
        for k in range(1, POOL_WINDOWS[-1]):
            run = run + _rows_at(rev_c, res_rc, k, tile)
            if k + 1 in POOL_WINDOWS:
                sums[k + 1] = run
        put(C_XC * D_G, _pool_select(r["lane_grp"], sums[2], sums[4], sums[8], sums[16]) - dypre)

        dyb = branch_grad(1)
        gw_acc[...] += _mm_tn(h_ref[...], dzb)
        g["cc_pw_w"][...] += _mm_tn(r["act_d"], dyd)
        dact = _mm_nt(dyd, w["cc_pw_w"][...])
        sig_ln, ln_d = r["sig_ln"], r["ln_d"]
        dln = dact * (sig_ln * (1.0 + ln_d * (1.0 - sig_ln)))
        add_row("cc_ln_g", _rowsum(dln * r["xhat_d"]))
        add_row("cc_ln_b", _rowsum(dln))
        dconv_d = _ln_bwd(dln, r["xhat_d"], r["rstd_d"], w["cc_ln_g"][...])
        add_row("cc_dw_b", _rowsum(dconv_d))
        push_rev(rev_d, dconv_d)
        _make_residues(rev_d, res_rd)
        dhd = jnp.zeros((tile, D_G), F32)
        for j in range(CONV_D):
            ahead = _rows_at(rev_d, res_rd, CONV_D - 1 - j, tile)
            dhd = dhd + w["cc_dw_w"][j:j + 1, :] * ahead
            acc["cc_dw_w"][j:j + 1, :] += _rowsum(r["hd"] * ahead)
        sig_dg = r["sig_dg"]
        put(C_DA * D_G, dhd * sig_dg)
        put(C_DG * D_G, dhd * r["da"] * sig_dg * (1.0 - sig_dg))

        dye = branch_grad(4)
        dug = dyb * r["mixed"]
        dmixed = dyb * r["ug"]
        wm, lo, vn = r["wm"], r["lo"], r["vn"]
        dvn_chunks = []
        for c in range(tile // CHUNK):
            rows = slice(c * CHUNK, (c + 1) * CHUNK)
            acc["sg_bias"][...] += dmixed[rows, :]
            halves = []
            for hf in range(2):
                cols = slice(hf * 128, (hf + 1) * 128)
                dm = dmixed[rows, cols]
                dm_a, dm_b = jnp.where(lo, dm, 0.0), jnp.where(lo, 0.0, dm)
                vh = vn[rows, cols]
                g["sg_w"][2 * hf] += _mm_nt(dm_a, vh)
                g["sg_w"][2 * hf + 1] += _mm_nt(dm_b, vh)
                halves.append(_mm_tn(wm[2 * hf], dm_a) + _mm_tn(wm[2 * hf + 1], dm_b))
            dvn_chunks.append(jnp.concatenate(halves, axis=1))
        dvn = jnp.concatenate(dvn_chunks, axis=0)
        add_row("sg_ln_g", _rowsum(dvn * r["xhat_v"]))
        add_row("sg_ln_b", _rowsum(dvn))
        dvg = _ln_bwd(dvn, r["xhat_v"], r["rstd_v"], w["sg_ln_g"][...])
        put(C_V * D_G, dvg * _dgelu(r["v"], r["th_v"]))
        put(C_U * D_G, dug * _dgelu(r["u"], r["th_u"]))

        q = r["q"]
        dq = jnp.zeros((tile, D_G), F32)
        for h in range(N_SUB):
            p = r["probs"][h]
            dp = _mm_nt(dye, vm_ref[h])
            g["dv"][h] += _mm_tn(p, dye)
            ds = p * (dp - jnp.sum(dp * p, axis=-1, keepdims=True)) * ATT_SCALE
            dq = dq + _mm(ds, km_ref[h])
            g["dk"][h] += _mm_tn(ds, q)
        put(C_Q * D_G, dq)

        @pl.when(i == nt - 1)
        def _():
            for h in range(N_SUB):
                g["sg_w"][h] = jnp.where(r["tri"], g["sg_w"][h], 0.0)
            lane_head = _head_of_lane((CHUNK, D_G))
            col_of = lax.broadcasted_iota(jnp.int32, (CHUNK, 8), 1)
            ba = acc["sg_bias"][...]
            sgb_t = jnp.zeros((CHUNK, 8), F32)
            for h in range(N_SUB):
                col = jnp.sum(jnp.where(lane_head == h, ba, 0.0), axis=-1, keepdims=True)
                sgb_t = jnp.where(col_of == h, col, sgb_t)
            g["sg_b"][...] = sgb_t.T
            wbd = acc["pool_wbd"][...]
            for gi in range(N_SUB):
                sl = slice(gi * HEAD_DIM, (gi + 1) * HEAD_DIM)
                g["pool_w"][gi] = wbd[sl, sl]
            ca, dw = acc["conv_a"][...], acc["cc_dw_w"][...]
            for p in range(N_DEV):
                g["conv_a_w"][p] = ca[:, p * CONV_CH:(p + 1) * CONV_CH]
                g["cc_dw_w"][p] = dw[:, p * CONV_CH:(p + 1) * CONV_CH]
            gw_ref[...] = gw_acc[...].astype(gw_ref.dtype)
            if from_loss:
                total = jnp.sum(lacc[...], axis=-1, keepdims=True) * (0.5 / D_MODEL)
                slab_ref[LOSS_ROW:LOSS_ROW + 1, :] = jnp.broadcast_to(total, (1, D_MODEL))

    rev = lambda i: (nt - 1 - i, 0)
    fixed = lambda i: (0, 0)
    act_spec = pl.BlockSpec((tile, D_MODEL), rev)
    grad_specs = tuple(pl.BlockSpec(shape, lambda i, _nd=len(shape): (0,) * _nd) for _, shape in _BRANCH_GRADS)
    grad_shapes = tuple(jax.ShapeDtypeStruct(shape, F32) for _, shape in _BRANCH_GRADS)
    consts = [km, vm] + nat_arrays + [pw, ca8, dw8, repl["ln_g"], repl["ln_b"], w_out]
    outs = pl.pallas_call(
        body, grid=(nt,),
        in_specs=[act_spec, act_spec, pl.BlockSpec((tile, D_MIX), rev), pl.BlockSpec((tile, D_IN), rev),
                  pl.BlockSpec((HALO, D_IN), lambda i: (jnp.maximum((nt - 1 - i) * hb - 1, 0), 0)),
                  pl.BlockSpec((tile, 2 * D_G), rev)]
        + [_full_spec(a) for a in consts] + _tie_specs(ties),
        out_specs=(act_spec, pl.BlockSpec((tile, D_IN), rev), pl.BlockSpec((D_MIX, D_MODEL), fixed),
                   pl.BlockSpec((8, D_MODEL), fixed)) + grad_specs,
        out_shape=(jax.ShapeDtypeStruct((s, D_MODEL), F32), jax.ShapeDtypeStruct((s, D_IN), MM_DTYPE),
                   jax.ShapeDtypeStruct((D_MIX, D_MODEL), GRAD_DTYPE), jax.ShapeDtypeStruct((8, D_MODEL), F32))
        + grad_shapes,
        scratch_shapes=[pltpu.VMEM((HALO + tile, D_G), F32)] * 4 + [_residue_scratch(tile)] * 2
        + [pltpu.VMEM((D_MIX, D_MODEL), F32), pltpu.VMEM((1, D_MODEL), F32)]
        + [pltpu.VMEM(shape, F32) for _, shape in _BRANCH_ACC + _BRANCH_W_SCRATCH],
        name="layer_bwd_loss" if from_loss else "layer_bwd", compiler_params=_params("arbitrary"),
    )(target if from_loss else up, z, h, proj, proj, kept, *consts, *ties)
    return outs[0], outs[1], outs[2], outs[3], dict(zip([n for n, _ in _BRANCH_GRADS], outs[4:]))


def _dx_matmul(dproj, wt_in, dz, ties=(), tm=512):
    s = dproj.shape[0]

    def body(dp_ref, w_ref, dz_ref, *rest):
        o_ref = rest[len(ties)]
        o_ref[...] = _mm(dp_ref[...], w_ref[...]) + ALPHA * dz_ref[...]

    row = lambda i: (i, 0)
    return pl.pallas_call(
        body, grid=(s // tm,),
        in_specs=[pl.BlockSpec((tm, D_IN), row), _full_spec(wt_in), pl.BlockSpec((tm, D_MODEL), row)]
        + _tie_specs(ties),
        out_specs=pl.BlockSpec((tm, D_MODEL), row),
        out_shape=jax.ShapeDtypeStruct((s, D_MODEL), F32), name="dx_mm", compiler_params=_params("arbitrary"),
    )(dproj, wt_in, dz, *ties)


def _dw_in_matmul(x, dproj, ties=(), tk=512):
    s = x.shape[0]
    nk = s // tk
    blk = 2 * W_IN_COLS

    def body(x_ref, dp_ref, *rest):
        o_ref, acc = rest[len(ties):]
        k = pl.program_id(0)

        @pl.when(k == 0)
        def _():
            acc[...] = jnp.zeros_like(acc)

        xb = x_ref[...].astype(MM_DTYPE)
        for j in range(D_IN // blk):
            acc[j * blk:(j + 1) * blk, :] += _mm_tn(dp_ref[:, j * blk:(j + 1) * blk], xb)

        @pl.when(k == nk - 1)
        def _():
            o_ref[...] = acc[...].astype(o_ref.dtype)

    return pl.pallas_call(
        body, grid=(nk,),
        in_specs=[pl.BlockSpec((tk, D_MODEL), lambda k: (k, 0)), pl.BlockSpec((tk, D_IN), lambda k: (k, 0))]
        + _tie_specs(ties),
        out_specs=pl.BlockSpec((D_IN, D_MODEL), lambda k: (0, 0)),
        out_shape=jax.ShapeDtypeStruct((D_IN, D_MODEL), GRAD_DTYPE),
        scratch_shapes=[pltpu.VMEM((D_IN, D_MODEL), F32)], name="dw_in_mm", compiler_params=_params("arbitrary"),
    )(x, dproj, *ties)


def _in_bwd(x, dproj, wt_in, dz, tm=512):
    s = x.shape[0]
    n_steps = s // tm

    assert wt_in.dtype == GRAD_DTYPE
    blk = 2 * W_IN_COLS

    def body(x_ref, dp_ref, w_hbm, dz_ref, o_ref, gw_hbm, w_vmem, acc, sem):
        i = pl.program_id(0)

        @pl.when(i == 0)
        def _():
            fetch = pltpu.make_async_copy(w_hbm, w_vmem, sem)
            fetch.start()
            acc[...] = jnp.zeros_like(acc)
            fetch.wait()

        o_ref[...] = _mm(dp_ref[...], w_vmem[...]) + ALPHA * dz_ref[...]
        xb = x_ref[...].astype(MM_DTYPE)
        for j in range(D_IN // blk):
            acc[j * blk:(j + 1) * blk, :] += _mm_tn(dp_ref[:, j * blk:(j + 1) * blk], xb)

        @pl.when(i == n_steps - 1)
        def _():
            w_vmem[...] = acc[...].astype(w_vmem.dtype)
            emit = pltpu.make_async_copy(w_vmem, gw_hbm, sem)
            emit.start()
            emit.wait()

    row = lambda i: (i, 0)
    any_spec = pl.BlockSpec(memory_space=pl.ANY)
    return pl.pallas_call(
        body, grid=(n_steps,),
        in_specs=[pl.BlockSpec((tm, D_MODEL), row), pl.BlockSpec((tm, D_IN), row), any_spec,
                  pl.BlockSpec((tm, D_MODEL), row)],
        out_specs=(pl.BlockSpec((tm, D_MODEL), row), any_spec),
        out_shape=(jax.ShapeDtypeStruct((s, D_MODEL), F32), jax.ShapeDtypeStruct((D_IN, D_MODEL), GRAD_DTYPE)),
        scratch_shapes=[pltpu.VMEM((D_IN, D_MODEL), wt_in.dtype), pltpu.VMEM((D_IN, D_MODEL), F32),
                        pltpu.SemaphoreType.DMA],
        name="in_bwd", compiler_params=_params("arbitrary"),
    )(x, dproj, wt_in, dz)


def _kv_bwd(mem, dk, dv):
    def body(mem_ref, dk_ref, dv_ref, o_ref):
        grp = _head_of_lane((MEM_LEN, D_G))
        dk_sum = jnp.zeros((MEM_LEN, D_G), F32)
        dv_sum = jnp.zeros((MEM_LEN, D_G), F32)
        for h in range(N_SUB):
            dk_sum = dk_sum + jnp.where(grp == h, dk_ref[h], 0.0)
            dv_sum = dv_sum + jnp.where(grp == h, dv_ref[h], 0.0)
        o_ref[...] = _mm_tn(mem_ref[...], jnp.concatenate([dk_sum, dv_sum], axis=1)).astype(o_ref.dtype)

    return pl.pallas_call(body, out_shape=jax.ShapeDtypeStruct((D_MODEL, 2 * D_G), GRAD_DTYPE), name="kv_bwd",
                          compiler_params=_params())(mem, dk, dv)


def _layer_fwd(x, mem, layer, repl, gw, proj=None, km_vm=None, ties=()):
    km, vm = _kv_project(mem, gw["w_kv"]) if km_vm is None else km_vm
    proj, h, z, kept, xn = _layer_fwd_fused(x, gw["wt_in"] if proj is None else None, proj, km, vm, layer, repl,
                                            gw["pw"], gw["ca8"], gw["dw8"], gw["w_out"], layer < DEPTH - 1, ties)
    return xn, (x, proj, h, z, km, vm, kept)


def _layer_bwd_a(up, target, mem, layer, repl, gw, saved, ties=()):
    x_in, proj, h, z, km, vm, kept = saved
    dz, dproj, g_w_out, g1024, bg = _layer_bwd_fused(up, target, z, h, proj, kept, km, vm, layer, repl, gw["w_out"],
                                                     gw["pw"], gw["ca8"], gw["dw8"], ties)
    grads = {n: bg[n] for n in ("g256", "sg_w", "sg_b", "pool_w", "conv_a_w", "cc_dw_w")}
    grads.update(w_out=g_w_out.reshape(N_DEV, D_MIX // N_DEV, D_MODEL), g1024=g1024,
                 w_kv=_kv_bwd(mem, bg["dk"], bg["dv"]).reshape(N_DEV, D_MODEL // N_DEV, 2 * D_G),
                 cc_pw_w=bg["cc_pw_w"].reshape(N_DEV, CONV_CH, D_G))
    return dz, dproj, grads


def _landing_shapes(items):
    out = []
    for a, scatter, pick in items:
        shape = a.shape if scatter else (N_DEV,) + (a.shape if pick is None else a.shape[1:])
        out.append(jax.ShapeDtypeStruct(shape, a.dtype))
    return tuple(out)


def _exchange_sems(n):
    return [pltpu.SemaphoreType.DMA(((N_DEV - 1) * n,)), pltpu.SemaphoreType.DMA(((N_DEV - 1) * n,)),
            pltpu.SemaphoreType.DMA((n,))]


def _exchange_copies(modes, ins, outs, send_sems, recv_sems, local_sems):
    n = len(ins)
    x, y, c = lax.axis_index("x"), lax.axis_index("y"), lax.axis_index("c")
    me = 4 * x + 2 * y + c

    def src_of(a, dest):
        scatter, pick = modes[a]
        if scatter:
            return ins[a].at[dest]
        return ins[a] if pick is None else ins[a].at[pick]

    local = [pltpu.make_async_copy(src_of(a, me), outs[a].at[me], local_sems.at[a]) for a in range(n)]
    sends, recvs = [], []
    for k in range(1, N_DEV):
        px = 1 - x if k & 4 else x
        py = 1 - y if k & 2 else y
        pc = 1 - c if k & 1 else c
        peer = 4 * px + 2 * py + pc
        for a in range(n):
            sems = dict(send_sem=send_sems.at[(k - 1) * n + a], recv_sem=recv_sems.at[(k - 1) * n + a],
                        device_id=(px, py, pc), device_id_type=pl.DeviceIdType.MESH)
            sends.append(pltpu.make_async_remote_copy(src_ref=src_of(a, peer), dst_ref=outs[a].at[me], **sems))
            recvs.append(pltpu.make_async_remote_copy(src_ref=src_of(a, peer), dst_ref=outs[a].at[peer], **sems))
    return local, sends, recvs


def _gather_two_level(items, name):
    n = len(items)
    assert not any(scatter for _, scatter, _ in items)
    picks = [pick for _, _, pick in items]

    def body(*refs):
        ins, outs = refs[:n], refs[n:2 * n]
        send_sems, recv_sems, local_sems = refs[2 * n:]
        x, y, c = lax.axis_index("x"), lax.axis_index("y"), lax.axis_index("c")
        sib = 1 - c
        chips = [(1 - x, y), (x, 1 - y), (1 - x, 1 - y)]

        def slot(a, px, py, pc):
            return outs[a].at[4 * px + 2 * py + pc]

        def copy(k, a, src, block, to):
            return pltpu.make_async_remote_copy(
                src_ref=src, dst_ref=slot(a, *block), send_sem=send_sems.at[k * n + a],
                recv_sem=recv_sems.at[k * n + a], device_id=to, device_id_type=pl.DeviceIdType.MESH)

        own = [ins[a] if picks[a] is None else ins[a].at[picks[a]] for a in range(n)]
        local = [pltpu.make_async_copy(own[a], slot(a, x, y, c), local_sems.at[a]) for a in range(n)]
        first = [copy(0, a, own[a], (x, y, c), (x, y, sib)) for a in range(n)]
        first += [copy(1 + j, a, own[a], (x, y, c), (*chip, c)) for j, chip in enumerate(chips[:2]) for a in range(n)]
        for cp in local + first:
            cp.start()

        def pass_on(j, a):
            chip = chips[j]
            copy(1 + j, a, own[a], (*chip, c), (x, y, c)).wait_recv()
            fwd = copy(4 + j, a, slot(a, *chip, c), (*chip, c), (x, y, sib))
            fwd.start()
            return fwd

        passed = [pass_on(j, a) for j in range(2) for a in range(n)]
        south = c == 0
        via = tuple(jnp.where(south, p, q) for p, q in zip(chips[0], chips[1]))
        blk = tuple(jnp.where(south, q, p) for p, q in zip(chips[0], chips[1]))
        relayed = [copy(3, a, slot(a, *blk, c), (*blk, c), (*via, c)) for a in range(n)]
        for cp in relayed:
            cp.start()
        passed += [pass_on(2, a) for a in range(n)]
        first += relayed
        for a in range(n):
            copy(0, a, own[a], (x, y, sib), (x, y, c)).wait_recv()
        for j, chip in enumerate(chips):
            for a in range(n):
                copy(4 + j, a, own[a], (*chip, sib), (x, y, c)).wait_recv()
        for cp in first + passed:
            cp.wait_send()
        for cp in local:
            cp.wait()

    any_spec = pl.BlockSpec(memory_space=pl.ANY)
    return pl.pallas_call(
        body, in_specs=[any_spec] * n, out_specs=(any_spec,) * n, out_shape=_landing_shapes(items),
        scratch_shapes=[pltpu.SemaphoreType.DMA((7 * n,)), pltpu.SemaphoreType.DMA((7 * n,)),
                        pltpu.SemaphoreType.DMA((n,))],
        name=name,
    )(*[a for a, _, _ in items])


_HBM_SPEC = pl.BlockSpec(memory_space=pltpu.HBM)
_SEM_SPEC = pl.BlockSpec(memory_space=pltpu.SEMAPHORE)
_SPLIT_PARAMS = pltpu.CompilerParams(has_side_effects=pltpu.SideEffectType.DATAFLOW_SIDE_EFFECTING)


def _split_start(srcs, lands, plan, sem_shapes, name):
    n_src, n_land = len(srcs), len(lands)
    n_buf = n_src + n_land
    bufs = [pltpu.with_memory_space_constraint(a, pltpu.HBM) for a in list(srcs) + list(lands)]

    def body(*refs):
        local, sends, _ = plan(refs[:n_src], refs[n_src:n_buf], *refs[n_buf:n_buf + 3])
        for cp in local + sends:
            cp.start()
        token = refs[-1]
        token[...] = jnp.zeros_like(token)

    res = pl.pallas_call(
        body, name=name, in_specs=[_HBM_SPEC] * n_buf,
        out_shape=tuple(sem_shapes) + tuple(pltpu.HBM(a.shape, a.dtype) for a in bufs)
        + (jax.ShapeDtypeStruct((8, 128), F32),),
        out_specs=(_SEM_SPEC,) * 3 + (_HBM_SPEC,) * n_buf + (pl.BlockSpec(memory_space=pltpu.VMEM),),
        input_output_aliases={i: 3 + i for i in range(n_buf)}, compiler_params=_SPLIT_PARAMS,
    )(*bufs)
    return dict(sems=res[:3], srcs=res[3:3 + n_src], lands=res[3 + n_src:3 + n_buf], token=res[-1], plan=plan)


def _split_wait(ticket, after, name):
    n_src, n_land = len(ticket["srcs"]), len(ticket["lands"])
    n_buf = n_src + n_land
    plan = ticket["plan"]

    def body(*refs):
        local, sends, recvs = plan(refs[:n_src], refs[n_src:n_buf], *refs[n_buf:n_buf + 3])
        for cp in recvs:
            cp.wait_recv()
        for cp in sends:
            cp.wait_send()
        for cp in local:
            cp.wait()

    bufs = list(ticket["srcs"]) + list(ticket["lands"])
    res = pl.pallas_call(
        body, name=name, in_specs=[_HBM_SPEC] * n_buf + [_SEM_SPEC] * 3 + [pl.BlockSpec(memory_space=pl.ANY)],
        out_shape=tuple(pltpu.HBM(a.shape, a.dtype) for a in bufs), out_specs=(_HBM_SPEC,) * n_buf,
        input_output_aliases={i: i for i in range(n_buf)}, compiler_params=_SPLIT_PARAMS,
    )(*bufs, *ticket["sems"], after)
    return res[n_src:]


def _empty_landings(items):
    return [lax.empty(s.shape, s.dtype) for s in _landing_shapes(items)]


def _exchange_start(items, name):
    modes = [(scatter, pick) for _, scatter, pick in items]
    plan = lambda ins, outs, *sems: _exchange_copies(modes, ins, outs, *sems)
    return _split_start([a for a, _, _ in items], _empty_landings(items), plan, _exchange_sems(len(items)), name)


def _two_level_plans(picks):
    n = len(picks)

    def place():
        x, y, c = lax.axis_index("x"), lax.axis_index("y"), lax.axis_index("c")
        return x, y, c, 1 - c, [(1 - x, y), (x, 1 - y), (1 - x, 1 - y)]

    def copy(outs, send_sems, recv_sems, k, a, src, block, to):
        px, py, pc = block
        return pltpu.make_async_remote_copy(
            src_ref=src, dst_ref=outs[a].at[4 * px + 2 * py + pc], send_sem=send_sems.at[k * n + a],
            recv_sem=recv_sems.at[k * n + a], device_id=to, device_id_type=pl.DeviceIdType.MESH)

    def between_chips(ins, outs, send_sems, recv_sems, local_sems):
        x, y, c, sib, chips = place()
        own = [ins[a] if picks[a] is None else ins[a].at[picks[a]] for a in range(n)]
        mk = lambda *args: copy(outs, send_sems, recv_sems, *args)
        local = [pltpu.make_async_copy(own[a], outs[a].at[4 * x + 2 * y + c], local_sems.at[a]) for a in range(n)]
        sends = [mk(0, a, own[a], (x, y, c), (x, y, sib)) for a in range(n)]
        sends += [mk(1 + j, a, own[a], (x, y, c), (*chip, c)) for j, chip in enumerate(chips) for a in range(n)]
        recvs = [mk(0, a, own[a], (x, y, sib), (x, y, c)) for a in range(n)]
        recvs += [mk(1 + j, a, own[a], (*chip, c), (x, y, c)) for j, chip in enumerate(chips) for a in range(n)]
        return local, sends, recvs

    def within_chip(ins, outs, send_sems, recv_sems, local_sems):
        x, y, c, sib, chips = place()
        mk = lambda *args: copy(outs, send_sems, recv_sems, *args)
        slot = lambda a, px, py, pc: outs[a].at[4 * px + 2 * py + pc]
        sends = [mk(j, a, slot(a, *chip, c), (*chip, c), (x, y, sib)) for j, chip in enumerate(chips)
                 for a in range(n)]
        recvs = [mk(j, a, slot(a, *chip, c), (*chip, sib), (x, y, c)) for j, chip in enumerate(chips)
                 for a in range(n)]
        return [], sends, recvs

    sems = lambda k: [pltpu.SemaphoreType.DMA((k * n,)), pltpu.SemaphoreType.DMA((k * n,)),
                      pltpu.SemaphoreType.DMA((n,))]
    return between_chips, sems(4), within_chip, sems(3)


def _adam_math(g, w, m, v):
    m_new = ADAM_B1 * m + (1.0 - ADAM_B1) * g
    v_new = ADAM_B2 * v + (1.0 - ADAM_B2) * (g * g)
    m_hat = m_new / (1.0 - ADAM_B1 ** ADAM_STEP)
    v_hat = v_new / (1.0 - ADAM_B2 ** ADAM_STEP)
    return -ADAM_LR * (m_hat / (jnp.sqrt(v_hat) + ADAM_EPS) + ADAM_WD * w), m_new, v_new


def _adamw_big(parts, w, m, v, layer, prev, name, tr):
    depth, rows, cols = w.shape

    def body(p_ref, w_ref, m_ref, v_ref, *rest):
        g_out, d_out, m_out, v_out = rest[len(prev):]
        g = p_ref[0].astype(F32)
        for q in range(1, N_DEV):
            g = g + p_ref[q].astype(F32)
        d, m_new, v_new = _adam_math(g, w_ref[...], m_ref[...], v_ref[...])
        g_out[...] = g
        d_out[...] = d
        m_out[...] = m_new
        v_out[...] = v_new

    blk = pl.BlockSpec((None, tr, cols), lambda i: (layer, i, 0))
    shp = jax.ShapeDtypeStruct((depth, rows, cols), F32)
    return pl.pallas_call(
        body, grid=(rows // tr,),
        in_specs=[pl.BlockSpec((N_DEV, tr, cols), lambda i: (0, i, 0)), blk, blk, blk]
        + [pl.BlockSpec(memory_space=pl.ANY)] * len(prev),
        out_specs=(blk,) * 4, out_shape=(shp,) * 4,
        input_output_aliases={4 + j: j for j in range(len(prev))},
        name=name, compiler_params=_params("arbitrary"),
    )(parts, w, m, v, *prev)


_SMALL_TENSORS = (("conv_a_w", "conv_a_w", None), ("cc_dw_w", "cc_dw_w", None), ("cc_pw_w", "cc_pw_w", None),
                  ("sg_w", "sg_w", None), ("pool_w", "pool_w", None), ("sg_b", "sg_b", None)) \
    + tuple((n, "g256", k) for k, n in enumerate(G256_ROWS)) + tuple((n, "g1024", k) for k, n in enumerate(G1024_ROWS))
_SMALL_LANDINGS = ("conv_a_w", "cc_dw_w", "cc_pw_w", "sg_w", "pool_w", "sg_b", "g256", "g1024")
_TAPS_FIRST = ("conv_a_w", "cc_dw_w")


def _adamw_small(landings, wts, mom, var):
    names = [n for n, _, _ in _SMALL_TENSORS]
    n_land = DEPTH * len(_SMALL_LANDINGS)
    n_t = len(names)

    def body(*refs):
        land = [dict(zip(_SMALL_LANDINGS, refs[l * len(_SMALL_LANDINGS):(l + 1) * len(_SMALL_LANDINGS)]))
                for l in range(DEPTH)]
        w_refs = dict(zip(names, refs[n_land:n_land + n_t]))
        m_refs = dict(zip(names, refs[n_land + n_t:n_land + 2 * n_t]))
        v_refs = dict(zip(names, refs[n_land + 2 * n_t:n_land + 3 * n_t]))
        outs = refs[n_land + 3 * n_t:]
        out_refs = {n: outs[4 * k:4 * k + 4] for k, n in enumerate(names)}
        loss_ref = outs[4 * n_t]
        for name, key, row in _SMALL_TENSORS:
            for l in range(DEPTH):
                src = land[l][key]
                if row is not None:
                    part = lambda q: src[q, row:row + 1, :]
                    at = (slice(l, l + 1),)
                elif name == "sg_b":
                    part = lambda q: src[q, 0:N_SUB, :]
                    at = (l,)
                elif name in _TAPS_FIRST:
                    part = lambda q: src[q]
                    at = (slice(None), l)
                else:
                    part = lambda q: src[q]
                    at = (l,)
                g = part(0)
                for q in range(1, N_DEV):
                    g = g + part(q)
                d, m_new, v_new = _adam_math(g, w_refs[name][at], m_refs[name][at], v_refs[name][at])
                for ref, val in zip(out_refs[name], (g, d, m_new, v_new)):
                    ref[at] = val
        src = land[DEPTH - 1]["g1024"]
        loss = src[0, LOSS_ROW:LOSS_ROW + 1, 0:128]
        for q in range(1, N_DEV):
            loss = loss + src[q, LOSS_ROW:LOSS_ROW + 1, 0:128]
        loss_ref[...] = loss

    ins = [landings[l][k] for l in range(DEPTH) for k in _SMALL_LANDINGS] \
        + [src[n] for src in (wts, mom, var) for n in names]
    out_shape = tuple(jax.ShapeDtypeStruct(wts[n].shape, F32) for n in names for _ in range(4)) \
        + (jax.ShapeDtypeStruct((1, 128), F32),)
    res = pl.pallas_call(body, out_shape=out_shape, name="adamw_small", compiler_params=_params())(*ins)
    return {n: res[4 * k:4 * k + 4] for k, n in enumerate(names)}, res[4 * n_t]


_BIG = (("w_in", 64), ("w_out", 32), ("w_kv", 32))
_GRAD_ITEMS_EARLY = ("w_out", "w_kv", "cc_pw_w", "conv_a_w", "cc_dw_w")
_GRAD_ITEMS_REPL = ("g256", "sg_w", "sg_b", "pool_w", "g1024")


def _grad_items(grads, with_w_in):
    items = [(grads[n], True, None) for n in (("w_in",) if with_w_in else ()) + _GRAD_ITEMS_EARLY]
    return items + [(grads[n], False, None) for n in _GRAD_ITEMS_REPL]


def _landed(parts, with_w_in):
    names = (("w_in",) if with_w_in else ()) + _GRAD_ITEMS_EARLY + _GRAD_ITEMS_REPL
    return dict(zip(names, parts))


def _gathered_weights(wt_in8, w_kv8, w_out8, pw8, ca8, dw8):
    return dict(wt_in=wt_in8.reshape(D_IN, D_MODEL), w_kv=w_kv8.reshape(D_MODEL, 2 * D_G),
                w_out=w_out8.reshape(D_MIX, D_MODEL), pw=pw8.reshape(D_G, D_G), ca8=ca8, dw8=dw8)


def kernel(x, mem, w_in, conv_a_w, sg_ln_g, sg_ln_b, sg_w, sg_b, pool_w, pool_scale, cc_dw_w, cc_dw_b, cc_ln_g, cc_ln_b, cc_pw_w, w_kv, w_out, ln_g, ln_b, loss_target, m_w_in, m_conv_a_w, m_sg_ln_g, m_sg_ln_b, m_sg_w, m_sg_b, m_pool_w, m_pool_scale, m_cc_dw_w, m_cc_dw_b, m_cc_ln_g, m_cc_ln_b, m_cc_pw_w, m_w_kv, m_w_out, m_ln_g, m_ln_b, v_w_in, v_conv_a_w, v_sg_ln_g, v_sg_ln_b, v_sg_w, v_sg_b, v_pool_w, v_pool_scale, v_cc_dw_w, v_cc_dw_b, v_cc_ln_g, v_cc_ln_b, v_cc_pw_w, v_w_kv, v_w_out, v_ln_g, v_ln_b):
    names = ("w_in", "conv_a_w", "sg_ln_g", "sg_ln_b", "sg_w", "sg_b", "pool_w", "pool_scale", "cc_dw_w", "cc_dw_b",
             "cc_ln_g", "cc_ln_b", "cc_pw_w", "w_kv", "w_out", "ln_g", "ln_b")
    wts = dict(zip(names, (w_in, conv_a_w, sg_ln_g, sg_ln_b, sg_w, sg_b, pool_w, pool_scale, cc_dw_w, cc_dw_b,
                           cc_ln_g, cc_ln_b, cc_pw_w, w_kv, w_out, ln_g, ln_b)))
    mom = dict(zip(names, (m_w_in, m_conv_a_w, m_sg_ln_g, m_sg_ln_b, m_sg_w, m_sg_b, m_pool_w, m_pool_scale,
                           m_cc_dw_w, m_cc_dw_b, m_cc_ln_g, m_cc_ln_b, m_cc_pw_w, m_w_kv, m_w_out, m_ln_g, m_ln_b)))
    var = dict(zip(names, (v_w_in, v_conv_a_w, v_sg_ln_g, v_sg_ln_b, v_sg_w, v_sg_b, v_pool_w, v_pool_scale,
                           v_cc_dw_w, v_cc_dw_b, v_cc_ln_g, v_cc_ln_b, v_cc_pw_w, v_w_kv, v_w_out, v_ln_g, v_ln_b)))
    repl = wts
    xs, mems, tgt = x[0], mem[0], loss_target[0]
    turned = {"w_in": (0, 2, 1), "conv_a_w": (1, 0, 2), "cc_dw_w": (1, 0, 2)}
    wts, mom, var = [{n: (jnp.transpose(a, turned[n]) if n in turned else a) for n, a in src.items()}
                     for src in (wts, mom, var)]
    wb = {n: wts[n].astype(MM_DTYPE) for n in ("w_in", "w_kv", "w_out", "cc_pw_w")}

    wt8_0, wkv8_0 = _gather_two_level([(wb["w_in"], False, 0), (wb["w_kv"], False, 0)], "gather_weights_0a")
    rest_0 = _exchange_start([(wb["w_out"], False, 0), (wb["cc_pw_w"], False, 0), (wts["conv_a_w"], False, None),
                              (wts["cc_dw_w"], False, None)], "gather_weights_0b_start")
    km_vm0 = _kv_project(mems, wkv8_0.reshape(D_MODEL, 2 * D_G))
    proj0 = _proj_matmul(xs, wt8_0.reshape(D_IN, D_MODEL), (rest_0["token"],))
    wo8_0, pw8_0, ca8, dw8 = _split_wait(rest_0, proj0, "gather_weights_0b_wait")
    gw0 = _gathered_weights(wt8_0, wkv8_0, wo8_0, pw8_0, ca8, dw8)
    items_1 = [(wb[n], False, 1) for n in ("w_in", "w_kv", "w_out", "cc_pw_w")]
    between_chips, sems_a, within_chip, sems_b = _two_level_plans([1] * len(items_1))
    chips_1 = _split_start([a for a, _, _ in items_1], _empty_landings(items_1), between_chips, sems_a,
                           "gather_weights_1a_start")
    x1, saved0 = _layer_fwd(xs, mems, 0, repl, gw0, proj0, km_vm0, (chips_1["token"],))
    core_1 = _split_start([], _split_wait(chips_1, x1, "gather_weights_1a_wait"), within_chip, sems_b,
                          "gather_weights_1b_start")
    gw1 = _gathered_weights(*_split_wait(core_1, core_1["token"], "gather_weights_1b_wait"), ca8, dw8)
    _, saved1 = _layer_fwd(x1, mems, 1, repl, gw1)

    dz1, dproj1, g1 = _layer_bwd_a(None, tgt, mems, 1, repl, gw1, saved1)
    shards = lambda g: g.reshape(N_DEV, W_IN_COLS, D_MODEL)
    up, g_wt_in_1 = _in_bwd(saved1[0], dproj1, gw1["wt_in"], dz1)
    g1["w_in"] = shards(g_wt_in_1)
    grads_1 = _exchange_start(_grad_items(g1, True), "exchange_grads_1_start")
    dz0, dproj0, g0 = _layer_bwd_a(up, None, mems, 0, repl, gw0, saved0, (grads_1["token"],))
    early_0 = _exchange_start(_grad_items(g0, False), "exchange_grads_0a_start")
    g_wt_in_0 = _dw_in_matmul(saved0[0], dproj0, (early_0["token"],))
    late_0 = _exchange_start([(shards(g_wt_in_0), True, None)], "exchange_grads_0b_start")
    grad_x = _dx_matmul(dproj0, gw0["wt_in"], dz0, (late_0["token"],))

    landed = [None, _landed(_split_wait(grads_1, grad_x, "exchange_grads_1_wait"), True)]
    big = {}
    for n, tr in _BIG:
        big[n] = _adamw_big(landed[1][n], wts[n], mom[n], var[n], 1, (), "adamw_" + n + "_1", tr)
    landed[0] = _landed(_split_wait(early_0, big["w_kv"][0], "exchange_grads_0a_wait"), False)
    for n, tr in _BIG[1:]:
        big[n] = _adamw_big(landed[0][n], wts[n], mom[n], var[n], 0, big[n], "adamw_" + n + "_0", tr)
    small, loss = _adamw_small(landed, wts, mom, var)
    (landed[0]["w_in"],) = _split_wait(late_0, loss, "exchange_grads_0b_wait")
    big["w_in"] = _adamw_big(landed[0]["w_in"], wts["w_in"], mom["w_in"], var["w_in"], 0, big["w_in"],
                             "adamw_w_in_0", _BIG[0][1])

    res = {**small, **big}
    res = {n: ([jnp.transpose(a, turned[n]) for a in r] if n in turned else r) for n, r in res.items()}
    return (loss[0, 0], grad_x[None], *[res[n][0] for n in names], *[res[n][1] for n in names],
            *[res[n][2] for n in names], *[res[n][3] for n in names])
```

```python
import math

import jax
import jax.numpy as jnp
from jax import lax
from jax.experimental import pallas as pl
from jax.experimental.pallas import tpu as pltpu

F32 = jnp.float32
MM_DTYPE = jnp.bfloat16
GRAD_DTYPE = jnp.bfloat16

D_MODEL = 1024
DEPTH = 2
D_G = 256
N_GROUPS = 5
D_MIX = N_GROUPS * D_G
N_SUB = 4
HEAD_DIM = D_G // N_SUB
CONV_A = 3
CONV_D = 31
CHUNK = 128
POOL_WINDOWS = (2, 4, 8, 16)
MEM_LEN = 256
LN_EPS = 1e-5
ALPHA = (2.0 * DEPTH) ** 0.25
D_IN = 9 * D_G + D_MIX
ATT_SCALE = 1.0 / math.sqrt(HEAD_DIM)

ADAM_LR = 0.001
ADAM_B1 = 0.9
ADAM_B2 = 0.999
ADAM_EPS = 1e-08
ADAM_WD = 0.01
ADAM_STEP = 10

N_DEV = 8
W_IN_COLS = D_IN // N_DEV
CONV_CH = D_G // N_DEV
HALO = 32
TILE = 256
VMEM_LIMIT = 56 * 1024 * 1024

C_XA, C_BA, C_CA, C_U, C_V, C_XC, C_DA, C_DG, C_Q = range(9)
C_GATE = 9 * D_G

G256_ROWS = ("sg_ln_g", "sg_ln_b", "pool_scale", "cc_dw_b", "cc_ln_g", "cc_ln_b")
G1024_ROWS = ("ln_g", "ln_b")
LOSS_ROW = 2


def _mm(a, b):
    return jnp.dot(a.astype(MM_DTYPE), b.astype(MM_DTYPE), preferred_element_type=F32)


def _mm_nt(a, b):
    return lax.dot_general(a.astype(MM_DTYPE), b.astype(MM_DTYPE), (((1,), (1,)), ((), ())),
                           preferred_element_type=F32)


def _mm_tn(a, b):
    return lax.dot_general(a.astype(MM_DTYPE), b.astype(MM_DTYPE), (((0,), (0,)), ((), ())),
                           preferred_element_type=F32)


def _sigmoid(x):
    return 0.5 * jnp.tanh(0.5 * x) + 0.5


_GELU_C = math.sqrt(2.0 / math.pi)
_GELU_A = 0.044715


def _gelu(x):
    th = jnp.tanh(_GELU_C * (x + _GELU_A * (x * x * x)))
    return 0.5 * x * (1.0 + th), th


def _dgelu(x, th):
    return 0.5 * (1.0 + th) + 0.5 * x * (1.0 - th * th) * (_GELU_C * (1.0 + 3.0 * _GELU_A * (x * x)))


def _ln_fwd(x, g, b):
    mu = jnp.mean(x, axis=-1, keepdims=True)
    xc = x - mu
    var = jnp.mean(xc * xc, axis=-1, keepdims=True)
    rstd = lax.rsqrt(var + LN_EPS)
    xhat = xc * rstd
    return xhat * g + b, xhat, rstd


def _ln_bwd(dy, xhat, rstd, g):
    dxhat = dy * g
    m1 = jnp.mean(dxhat, axis=-1, keepdims=True)
    m2 = jnp.mean(dxhat * xhat, axis=-1, keepdims=True)
    return rstd * (dxhat - m1 - xhat * m2)


def _rowsum(x):
    return jnp.sum(x, axis=0, keepdims=True)


def _col(ref, k):
    return ref[:, k * D_G:(k + 1) * D_G]


def _head_of_lane(shape):
    return jnp.right_shift(lax.broadcasted_iota(jnp.int32, shape, len(shape) - 1), HEAD_DIM.bit_length() - 1)


def _pool_select(lane_grp, s2, s4, s8, s16):
    return jnp.where(lane_grp == 0, s2, jnp.where(lane_grp == 1, s4, jnp.where(lane_grp == 2, s8, s16)))


def _row_view(ref, layer):
    return ref.at[pl.ds(layer, 1)]


def _make_residues(ext_ref, res_ref):
    rows = res_ref.shape[1]
    for r in range(1, 8):
        res_ref[r - 1] = ext_ref[pl.ds(r, rows), :]


def _rows_at(ext_ref, res_ref, off, tile):
    a, r = divmod(off, 8)
    if r == 0:
        return ext_ref[pl.ds(off, tile), :]
    return res_ref[r - 1, pl.ds(8 * a, tile), :]


def _residue_scratch(tile):
    return pltpu.VMEM((7, HALO + tile - 8, D_G), F32)


PROJ_SEGMENTS = ((C_XA * D_G, (C_CA + 1) * D_G), (C_XC * D_G, (C_XC + 1) * D_G), (C_DA * D_G, (C_DG + 1) * D_G),
                 (C_U * D_G, (C_V + 1) * D_G), (C_Q * D_G, (C_Q + 1) * D_G), (C_GATE, D_IN))


def _branch_forward(p_ref, ph_ref, first, row0, km_ref, vm_ref, w, ext_a, ext_c, ext_d, res_c, res_d, tile,
                    kept_ref=None, produce=None):
    r = {}
    produce = produce or (lambda: None)
    produce()
    produce()
    xa, ba, ca = _col(p_ref, C_XA), _col(p_ref, C_BA), _col(p_ref, C_CA)
    g_a = ca * xa
    ext_a[0:HALO] = jnp.where(first, 0.0, _col(ph_ref, C_CA) * _col(ph_ref, C_XA))
    ext_a[HALO:HALO + tile] = g_a
    conv_a = w["conv_a"][0:1, :] * ext_a[pl.ds(HALO - 2, tile), :]
    for k in range(1, CONV_A):
        conv_a = conv_a + w["conv_a"][k:k + 1, :] * ext_a[pl.ds(HALO - 2 + k, tile), :]
    r.update(xa=xa, ba=ba, ca=ca, g_a=g_a, conv_a=conv_a)
    ya = ba * conv_a

    produce()
    lane_grp = _head_of_lane((tile, D_G))
    trow = row0 + lax.broadcasted_iota(jnp.int32, (tile, D_G), 0)
    win = _pool_select(lane_grp, 2, 4, 8, 16)
    inv_cnt = 1.0 / jnp.minimum(trow + 1, win).astype(F32)
    if kept_ref is None:
        xc = _col(p_ref, C_XC)
        ext_c[0:HALO] = jnp.where(first, 0.0, _col(ph_ref, C_XC))
        ext_c[HALO:HALO + tile] = xc
        _make_residues(ext_c, res_c)
        acc = xc
        sums = {}
        for k in range(1, POOL_WINDOWS[-1]):
            acc = acc + _rows_at(ext_c, res_c, HALO - k, tile)
            if k + 1 in POOL_WINDOWS:
                sums[k + 1] = acc
        ypre = _pool_select(lane_grp, sums[2], sums[4], sums[8], sums[16]) * inv_cnt - xc
    else:
        ypre = kept_ref[:, D_G:2 * D_G]
    pool_mm = _mm(ypre, w["pool_wbd"][...])
    yc = pool_mm * w["pool_scale"][...]
    r.update(lane_grp=lane_grp, inv_cnt=inv_cnt, ypre=ypre, pool_mm=pool_mm)

    produce()
    da, dg = _col(p_ref, C_DA), _col(p_ref, C_DG)
    sig_dg = _sigmoid(dg)
    hd = da * sig_dg
    if kept_ref is None:
        ext_d[0:HALO] = jnp.where(first, 0.0, _col(ph_ref, C_DA) * _sigmoid(_col(ph_ref, C_DG)))
        ext_d[HALO:HALO + tile] = hd
        _make_residues(ext_d, res_d)
        conv_d = w["cc_dw_b"][...] + w["cc_dw_w"][0:1, :] * _rows_at(ext_d, res_d, HALO - (CONV_D - 1), tile)
        for j in range(1, CONV_D):
            conv_d = conv_d + w["cc_dw_w"][j:j + 1, :] * _rows_at(ext_d, res_d, HALO - (CONV_D - 1) + j, tile)
    else:
        conv_d = kept_ref[:, 0:D_G]
    r["kept"] = (conv_d, ypre)
    ln_d, xhat_d, rstd_d = _ln_fwd(conv_d, w["cc_ln_g"][...], w["cc_ln_b"][...])
    sig_ln = _sigmoid(ln_d)
    act_d = ln_d * sig_ln
    yd = _mm(act_d, w["cc_pw_w"][...])
    r.update(da=da, sig_dg=sig_dg, hd=hd, ln_d=ln_d, xhat_d=xhat_d, rstd_d=rstd_d, sig_ln=sig_ln, act_d=act_d)
    produce()

    u, v = _col(p_ref, C_U), _col(p_ref, C_V)
    ug, th_u = _gelu(u)
    vg, th_v = _gelu(v)
    vn, xhat_v, rstd_v = _ln_fwd(vg, w["sg_ln_g"][...], w["sg_ln_b"][...])
    tri = (lax.broadcasted_iota(jnp.int32, (CHUNK, CHUNK), 0)
           >= lax.broadcasted_iota(jnp.int32, (CHUNK, CHUNK), 1))
    wm = [jnp.where(tri, w["sg_w"][h], 0.0).astype(MM_DTYPE) for h in range(N_SUB)]
    lo = lax.broadcasted_iota(jnp.int32, (CHUNK, 2 * HEAD_DIM), 1) < HEAD_DIM
    chunks = []
    for c in range(tile // CHUNK):
        halves = []
        for hf in range(2):
            vh = vn[c * CHUNK:(c + 1) * CHUNK, hf * 128:(hf + 1) * 128]
            halves.append(_mm(wm[2 * hf], jnp.where(lo, vh, 0.0)) + _mm(wm[2 * hf + 1], jnp.where(lo, 0.0, vh)))
        chunks.append(jnp.concatenate(halves, axis=1) + w["sg_bias"][...])
    mixed = jnp.concatenate(chunks, axis=0)
    yb = ug * mixed
    r.update(u=u, v=v, ug=ug, th_u=th_u, th_v=th_v, vn=vn, xhat_v=xhat_v, rstd_v=rstd_v, wm=wm, lo=lo,
             mixed=mixed, tri=tri)

    produce()
    q = _col(p_ref, C_Q)
    ye = jnp.zeros((tile, D_G), F32)
    probs = []
    for h in range(N_SUB):
        s = _mm_nt(q, km_ref[h]) * ATT_SCALE
        e = jnp.exp(s - jnp.max(s, axis=-1, keepdims=True))
        p = e * (1.0 / jnp.sum(e, axis=-1, keepdims=True))
        probs.append(p)
        ye = ye + _mm(p, vm_ref[h])
    r.update(q=q, probs=probs)

    gate = p_ref[:, C_GATE:C_GATE + D_MIX]
    sig_gate = _sigmoid(gate)
    r.update(gate=gate, sig_gate=sig_gate, branch_out=(ya, yb, yc, yd, ye))
    return r


_BRANCH_REPL = ("sg_ln_g", "sg_ln_b", "sg_w", "sg_b", "pool_w", "pool_scale", "cc_dw_b", "cc_ln_g", "cc_ln_b")
_BRANCH_W_SCRATCH = (("conv_a", (CONV_A, D_G)), ("cc_dw_w", (CONV_D, D_G)), ("sg_bias", (CHUNK, D_G)),
                     ("pool_wbd", (D_G, D_G)), ("sgb8", (8, CHUNK)))


def _branch_weights(layer, nat, pw_ref, ca_ref, dw_ref, scr, init):
    @pl.when(init)
    def _():
        for p in range(N_DEV):
            scr["conv_a"][:, p * CONV_CH:(p + 1) * CONV_CH] = ca_ref[p, :, layer, :]
            scr["cc_dw_w"][:, p * CONV_CH:(p + 1) * CONV_CH] = dw_ref[p, :, layer, :]
        scr["sgb8"][...] = jnp.zeros((8, CHUNK), F32)
        scr["sgb8"][0:N_SUB] = nat["sg_b"][layer]
        sgb_t = scr["sgb8"][...].T
        head = _head_of_lane((CHUNK, D_G))
        bias = jnp.zeros((CHUNK, D_G), F32)
        for h in range(N_SUB):
            bias = jnp.where(head == h, sgb_t[:, h:h + 1], bias)
        scr["sg_bias"][...] = bias
        scr["pool_wbd"][...] = jnp.zeros((D_G, D_G), F32)
        for gi in range(N_SUB):
            sl = slice(gi * HEAD_DIM, (gi + 1) * HEAD_DIM)
            scr["pool_wbd"][sl, sl] = nat["pool_w"][layer, gi]

    w = {n: _row_view(nat[n], layer) for n in ("sg_ln_g", "sg_ln_b", "pool_scale", "cc_dw_b", "cc_ln_g", "cc_ln_b")}
    w.update(conv_a=scr["conv_a"], cc_dw_w=scr["cc_dw_w"], sg_bias=scr["sg_bias"], pool_wbd=scr["pool_wbd"],
             sg_w=nat["sg_w"].at[layer], cc_pw_w=pw_ref)
    return w


def _full_spec(a):
    nd = a.ndim
    return pl.BlockSpec(a.shape, lambda *_, _nd=nd: (0,) * _nd)


def _tie_specs(ties):
    return [pl.BlockSpec((8, 128), lambda *_: (0, 0)) for _ in ties]


def _params(*sem):
    return pltpu.CompilerParams(dimension_semantics=sem or None, vmem_limit_bytes=VMEM_LIMIT)


def _proj_matmul(x, wt_in, ties=(), tm=512):
    s, k = x.shape

    def body(x_ref, w_ref, *rest):
        o_ref = rest[len(ties)]
        o_ref[...] = _mm_nt(x_ref[...], w_ref[...])

    return pl.pallas_call(
        body, grid=(s // tm,),
        in_specs=[pl.BlockSpec((tm, k), lambda i: (i, 0)), _full_spec(wt_in)] + _tie_specs(ties),
        out_specs=pl.BlockSpec((tm, D_IN), lambda i: (i, 0)),
        out_shape=jax.ShapeDtypeStruct((s, D_IN), F32), name="proj_mm", compiler_params=_params("arbitrary"),
    )(x, wt_in, *ties)


def _kv_project(mem, w_kv):
    def body(mem_ref, w_ref, km_ref, vm_ref):
        kv = _mm(mem_ref[...], w_ref[...])
        k, v = kv[:, :D_G], kv[:, D_G:]
        grp = _head_of_lane((MEM_LEN, D_G))
        for h in range(N_SUB):
            km_ref[h] = jnp.where(grp == h, k, 0.0).astype(km_ref.dtype)
            vm_ref[h] = jnp.where(grp == h, v, 0.0).astype(vm_ref.dtype)

    shp = jax.ShapeDtypeStruct((N_SUB, MEM_LEN, D_G), MM_DTYPE)
    return pl.pallas_call(body, out_shape=(shp, shp), name="kv_project", compiler_params=_params())(mem, w_kv)


def _layer_fwd_fused(x, wt_in, proj, km, vm, layer, repl, pw, ca8, dw8, w_out, want_xn, ties=(), tile=TILE):
    s = x.shape[0]
    hb = tile // HALO
    nat_arrays = [repl[n] for n in _BRANCH_REPL]
    n_nat, nt = len(nat_arrays), len(ties)
    given = proj is not None

    def body(x_ref, *rest):
        if given:
            p_ref, ph_ref = rest[:2]
            rest = rest[2:]
        else:
            wt_ref = rest[0]
            rest = rest[1:]
        km_ref, vm_ref = rest[:2]
        nat = dict(zip(_BRANCH_REPL, rest[2:2 + n_nat]))
        pw_ref, ca_ref, dw_ref, wo_ref, g_ref, b_ref = rest[2 + n_nat:8 + n_nat]
        rest = rest[8 + n_nat + nt:]
        if not given:
            p_ref, rest = rest[0], rest[1:]
        h_ref, z_ref, cd_ref = rest[:3]
        rest = rest[3:]
        if want_xn:
            xn_ref, rest = rest[0], rest[1:]
        if not given:
            ph_ref, rest = rest[0], rest[1:]
        ext_a, ext_c, ext_d, res_c, res_d = rest[:5]
        scr = dict(zip([n for n, _ in _BRANCH_W_SCRATCH], rest[5:]))
        i = pl.program_id(0)
        xt = x_ref[...]
        produce = None
        if not given:
            @pl.when(i == 0)
            def _():
                ph_ref[...] = jnp.zeros_like(ph_ref)

            xb = xt.astype(MM_DTYPE)
            segments = iter(PROJ_SEGMENTS)

            def produce():
                lo, hi = next(segments)
                p_ref[:, lo:hi] = _mm_nt(xb, wt_ref[lo:hi, :])

        w = _branch_weights(layer, nat, pw_ref, ca_ref, dw_ref, scr, i == 0)
        r = _branch_forward(p_ref, ph_ref, i == 0, i * tile, km_ref, vm_ref, w, ext_a, ext_c, ext_d, res_c, res_d,
                            tile, None, produce)
        if not given:
            ph_ref[...] = p_ref[tile - HALO:tile, :]
        cd_ref[:, 0:D_G], cd_ref[:, D_G:2 * D_G] = r["kept"]
        h = (jnp.concatenate(r["branch_out"], axis=1) * (r["gate"] * r["sig_gate"])).astype(h_ref.dtype)
        h_ref[...] = h
        z = ALPHA * xt + _mm(h, wo_ref[...])
        z_ref[...] = z
        if want_xn:
            xn_ref[...] = _ln_fwd(z, _row_view(g_ref, layer)[...], _row_view(b_ref, layer)[...])[0]

    row = lambda i: (i, 0)
    consts = [km, vm] + nat_arrays + [pw, ca8, dw8, w_out, repl["ln_g"], repl["ln_b"]]
    act = jax.ShapeDtypeStruct((s, D_MODEL), F32)
    act_spec = pl.BlockSpec((tile, D_MODEL), row)
    if given:
        lead = [proj, proj]
        lead_specs = [pl.BlockSpec((tile, D_IN), row),
                      pl.BlockSpec((HALO, D_IN), lambda i: (jnp.maximum(i * hb - 1, 0), 0))]
        out_specs, out_shape, scratch = (), (), []
    else:
        lead = [wt_in]
        lead_specs = [_full_spec(wt_in)]
        out_specs = (pl.BlockSpec((tile, D_IN), row),)
        out_shape = (jax.ShapeDtypeStruct((s, D_IN), F32),)
        scratch = [pltpu.VMEM((HALO, D_IN), F32)]
    res = pl.pallas_call(
        body, grid=(s // tile,),
        in_specs=[act_spec] + lead_specs + [_full_spec(a) for a in consts] + _tie_specs(ties),
        out_specs=out_specs + (pl.BlockSpec((tile, D_MIX), row), act_spec, pl.BlockSpec((tile, 2 * D_G), row))
        + ((act_spec,) if want_xn else ()),
        out_shape=out_shape + (jax.ShapeDtypeStruct((s, D_MIX), MM_DTYPE), act, jax.ShapeDtypeStruct((s, 2 * D_G), F32))
        + ((act,) if want_xn else ()),
        scratch_shapes=scratch + [pltpu.VMEM((HALO + tile, D_G), F32)] * 3
        + [_residue_scratch(tile)] * 2 + [pltpu.VMEM(shape, F32) for _, shape in _BRANCH_W_SCRATCH],
        name="layer_fwd_given_proj" if given else "layer_fwd", compiler_params=_params("arbitrary"),
    )(x, *lead, *consts, *ties)
    res = ((proj,) + tuple(res)) if given else tuple(res)
    return res if want_xn else res + (None,)


_BRANCH_GRADS = (("g256", (8, D_G)), ("sg_w", (N_SUB, CHUNK, CHUNK)), ("sg_b", (8, CHUNK)),
                 ("pool_w", (N_SUB, HEAD_DIM, HEAD_DIM)), ("conv_a_w", (N_DEV, CONV_A, CONV_CH)),
                 ("cc_dw_w", (N_DEV, CONV_D, CONV_CH)), ("cc_pw_w", (D_G, D_G)),
                 ("dk", (N_SUB, MEM_LEN, D_G)), ("dv", (N_SUB, MEM_LEN, D_G)))
_BRANCH_ACC = (("conv_a", (CONV_A, D_G)), ("cc_dw_w", (CONV_D, D_G)), ("pool_wbd", (D_G, D_G)),
               ("sg_bias", (CHUNK, D_G)), ("sg_w", (N_SUB, CHUNK, CHUNK)))
_BRANCH_GRADS_NARROW = ("sg_w", "pool_w")


def _layer_bwd_fused(up, target, z, h, proj, kept, km, vm, layer, repl, w_out, pw, ca8, dw8, ties=(), tile=TILE):
    s = proj.shape[0]
    nt = s // tile
    hb = tile // HALO
    nat_arrays = [repl[n] for n in _BRANCH_REPL]
    n_nat, n_grads, n_acc, n_ties = len(nat_arrays), len(_BRANCH_GRADS), len(_BRANCH_ACC), len(ties)
    row_of = {n: k for k, n in enumerate(G256_ROWS)}
    from_loss = target is not None

    def body(o_ref, z_ref, h_ref, p_ref, ph_ref, cd_ref, km_ref, vm_ref, *rest):
        nat = dict(zip(_BRANCH_REPL, rest[:n_nat]))
        pw_ref, ca_ref, dw_ref, lng_ref, lnb_ref, wo_ref = rest[n_nat:n_nat + 6]
        rest = rest[n_nat + 6 + n_ties:]
        dz_ref, dp_ref, gw_ref, slab_ref = rest[:4]
        g = dict(zip([n for n, _ in _BRANCH_GRADS], rest[4:4 + n_grads]))
        rest = rest[4 + n_grads:]
        ext_a, rev_a, rev_c, rev_d, res_rc, res_rd, gw_acc, lacc = rest[:8]
        acc = dict(zip([n for n, _ in _BRANCH_ACC], rest[8:8 + n_acc]))
        scr = dict(zip([n for n, _ in _BRANCH_W_SCRATCH], rest[8 + n_acc:]))
        i = pl.program_id(0)
        t = nt - 1 - i

        @pl.when(i == 0)
        def _():
            for ref in list(g.values()) + list(acc.values()) + [rev_a, rev_c, rev_d, gw_acc, slab_ref, lacc]:
                ref[...] = jnp.zeros_like(ref)

        g_ln = _row_view(lng_ref, layer)[...]
        xn, xhat, rstd = _ln_fwd(z_ref[...], g_ln, _row_view(lnb_ref, layer)[...])
        if from_loss:
            err = xn - o_ref[...]
            lacc[...] += _rowsum(err * err)
            dxn = err * (1.0 / D_MODEL)
        else:
            dxn = o_ref[...]
        slab_ref[0:1, :] += _rowsum(dxn * xhat)
        slab_ref[1:2, :] += _rowsum(dxn)
        dz = _ln_bwd(dxn, xhat, rstd, g_ln)
        dz_ref[...] = dz
        dzb = dz.astype(MM_DTYPE)

        w = _branch_weights(layer, nat, pw_ref, ca_ref, dw_ref, scr, i == 0)
        r = _branch_forward(p_ref, ph_ref, t == 0, t * tile, km_ref, vm_ref, w, ext_a, None, None, None, None,
                            tile, cd_ref)

        def put(k, val, width=D_G):
            dp_ref[:, k:k + width] = val.astype(dp_ref.dtype)

        def add_row(name, val):
            k = row_of[name]
            g["g256"][k:k + 1, :] += val

        def push_rev(rev, val):
            head = rev[0:HALO]
            rev[tile:tile + HALO] = head
            rev[0:tile] = val

        gate, sig_gate = r["gate"], r["sig_gate"]

        def branch_grad(group):
            cols = slice(group * D_G, (group + 1) * D_G)
            dh_g = _mm_nt(dzb, wo_ref[cols, :])
            gate_g, sig_g = gate[:, cols], sig_gate[:, cols]
            put(C_GATE + group * D_G, dh_g * r["branch_out"][group] * (sig_g * (1.0 + gate_g * (1.0 - sig_g))))
            return dh_g * (gate_g * sig_g)

        dya = branch_grad(0)
        dyc = branch_grad(2)

        put(C_BA * D_G, dya * r["conv_a"])
        dconv_a = dya * r["ba"]
        push_rev(rev_a, dconv_a)
        dga = jnp.zeros((tile, D_G), F32)
        for k in range(CONV_A):
            ahead = rev_a[pl.ds(CONV_A - 1 - k, tile), :]
            dga = dga + w["conv_a"][k:k + 1, :] * ahead
            acc["conv_a"][k:k + 1, :] += _rowsum(r["g_a"] * ahead)
        put(C_CA * D_G, dga * r["xa"])
        put(C_XA * D_G, dga * r["ca"])

        dyd = branch_grad(3)
        add_row("pool_scale", _rowsum(dyc * r["pool_mm"]))
        dmm = dyc * w["pool_scale"][...]
        acc["pool_wbd"][...] += _mm_tn(r["ypre"], dmm)
        dypre = _mm_nt(dmm, w["pool_wbd"][...])
        dws = dypre * r["inv_cnt"]
        push_rev(rev_c, dws)
        _make_residues(rev_c, res_rc)
        run = dws
        sums = {}
        for k in range(1, POOL_WINDOWS[-1]):
            run = run + _rows_at(rev_c, res_rc, k, tile)
            if k + 1 in POOL_WINDOWS:
                sums[k + 1] = run
        put(C_XC * D_G, _pool_select(r["lane_grp"], sums[2], sums[4], sums[8], sums[16]) - dypre)

        dyb = branch_grad(1)
        gw_acc[...] += _mm_tn(h_ref[...], dzb)
        g["cc_pw_w"][...] += _mm_tn(r["act_d"], dyd)
        dact = _mm_nt(dyd, w["cc_pw_w"][...])
        sig_ln, ln_d = r["sig_ln"], r["ln_d"]
        dln = dact * (sig_ln * (1.0 + ln_d * (1.0 - sig_ln)))
        add_row("cc_ln_g", _rowsum(dln * r["xhat_d"]))
        add_row("cc_ln_b", _rowsum(dln))
        dconv_d = _ln_bwd(dln, r["xhat_d"], r["rstd_d"], w["cc_ln_g"][...])
        add_row("cc_dw_b", _rowsum(dconv_d))
        push_rev(rev_d, dconv_d)
        _make_residues(rev_d, res_rd)
        dhd = jnp.zeros((tile, D_G), F32)
        for j in range(CONV_D):
            ahead = _rows_at(rev_d, res_rd, CONV_D - 1 - j, tile)
            dhd = dhd + w["cc_dw_w"][j:j + 1, :] * ahead
            acc["cc_dw_w"][j:j + 1, :] += _rowsum(r["hd"] * ahead)
        sig_dg = r["sig_dg"]
        put(C_DA * D_G, dhd * sig_dg)
        put(C_DG * D_G, dhd * r["da"] * sig_dg * (1.0 - sig_dg))

        dye = branch_grad(4)
        dug = dyb * r["mixed"]
        dmixed = dyb * r["ug"]
        wm, lo, vn = r["wm"], r["lo"], r["vn"]
        dvn_chunks = []
        for c in range(tile // CHUNK):
            rows = slice(c * CHUNK, (c + 1) * CHUNK)
            acc["sg_bias"][...] += dmixed[rows, :]
            halves = []
            for hf in range(2):
                cols = slice(hf * 128, (hf + 1) * 128)
                dm = dmixed[rows, cols]
                dm_a, dm_b = jnp.where(lo, dm, 0.0), jnp.where(lo, 0.0, dm)
                vh = vn[rows, cols]
                acc["sg_w"][2 * hf] += _mm_nt(dm_a, vh)
                acc["sg_w"][2 * hf + 1] += _mm_nt(dm_b, vh)
                halves.append(_mm_tn(wm[2 * hf], dm_a) + _mm_tn(wm[2 * hf + 1], dm_b))
            dvn_chunks.append(jnp.concatenate(halves, axis=1))
        dvn = jnp.concatenate(dvn_chunks, axis=0)
        add_row("sg_ln_g", _rowsum(dvn * r["xhat_v"]))
        add_row("sg_ln_b", _rowsum(dvn))
        dvg = _ln_bwd(dvn, r["xhat_v"], r["rstd_v"], w["sg_ln_g"][...])
        put(C_V * D_G, dvg * _dgelu(r["v"], r["th_v"]))
        put(C_U * D_G, dug * _dgelu(r["u"], r["th_u"]))

        q = r["q"]
        dq = jnp.zeros((tile, D_G), F32)
        for h in range(N_SUB):
            p = r["probs"][h]
            dp = _mm_nt(dye, vm_ref[h])
            g["dv"][h] += _mm_tn(p, dye)
            ds = p * (dp - jnp.sum(dp * p, axis=-1, keepdims=True)) * ATT_SCALE
            dq = dq + _mm(ds, km_ref[h])
            g["dk"][h] += _mm_tn(ds, q)
        put(C_Q * D_G, dq)

        @pl.when(i == nt - 1)
        def _():
            for h in range(N_SUB):
                g["sg_w"][h] = jnp.where(r["tri"], acc["sg_w"][h], 0.0).astype(g["sg_w"].dtype)
            lane_head = _head_of_lane((CHUNK, D_G))
            col_of = lax.broadcasted_iota(jnp.int32, (CHUNK, 8), 1)
            ba = acc["sg_bias"][...]
            sgb_t = jnp.zeros((CHUNK, 8), F32)
            for h in range(N_SUB):
                col = jnp.sum(jnp.where(lane_head == h, ba, 0.0), axis=-1, keepdims=True)
                sgb_t = jnp.where(col_of == h, col, sgb_t)
            g["sg_b"][...] = sgb_t.T
            wbd = acc["pool_wbd"][...]
            for gi in range(N_SUB):
                sl = slice(gi * HEAD_DIM, (gi + 1) * HEAD_DIM)
                g["pool_w"][gi] = wbd[sl, sl].astype(g["pool_w"].dtype)
            ca, dw = acc["conv_a"][...], acc["cc_dw_w"][...]
            for p in range(N_DEV):
                g["conv_a_w"][p] = ca[:, p * CONV_CH:(p + 1) * CONV_CH]
                g["cc_dw_w"][p] = dw[:, p * CONV_CH:(p + 1) * CONV_CH]
            gw_ref[...] = gw_acc[...].astype(gw_ref.dtype)
            if from_loss:
                total = jnp.sum(lacc[...], axis=-1, keepdims=True) * (0.5 / D_MODEL)
                slab_ref[LOSS_ROW:LOSS_ROW + 1, :] = jnp.broadcast_to(total, (1, D_MODEL))

    rev = lambda i: (nt - 1 - i, 0)
    fixed = lambda i: (0, 0)
    act_spec = pl.BlockSpec((tile, D_MODEL), rev)
    grad_specs = tuple(pl.BlockSpec(shape, lambda i, _nd=len(shape): (0,) * _nd) for _, shape in _BRANCH_GRADS)
    grad_shapes = tuple(jax.ShapeDtypeStruct(shape, GRAD_DTYPE if n in _BRANCH_GRADS_NARROW else F32)
                        for n, shape in _BRANCH_GRADS)
    consts = [km, vm] + nat_arrays + [pw, ca8, dw8, repl["ln_g"], repl["ln_b"], w_out]
    outs = pl.pallas_call(
        body, grid=(nt,),
        in_specs=[act_spec, act_spec, pl.BlockSpec((tile, D_MIX), rev), pl.BlockSpec((tile, D_IN), rev),
                  pl.BlockSpec((HALO, D_IN), lambda i: (jnp.maximum((nt - 1 - i) * hb - 1, 0), 0)),
                  pl.BlockSpec((tile, 2 * D_G), rev)]
        + [_full_spec(a) for a in consts] + _tie_specs(ties),
        out_specs=(act_spec, pl.BlockSpec((tile, D_IN), rev), pl.BlockSpec((D_MIX, D_MODEL), fixed),
                   pl.BlockSpec((8, D_MODEL), fixed)) + grad_specs,
        out_shape=(jax.ShapeDtypeStruct((s, D_MODEL), F32), jax.ShapeDtypeStruct((s, D_IN), MM_DTYPE),
                   jax.ShapeDtypeStruct((D_MIX, D_MODEL), GRAD_DTYPE), jax.ShapeDtypeStruct((8, D_MODEL), F32))
        + grad_shapes,
        scratch_shapes=[pltpu.VMEM((HALO + tile, D_G), F32)] * 4 + [_residue_scratch(tile)] * 2
        + [pltpu.VMEM((D_MIX, D_MODEL), F32), pltpu.VMEM((1, D_MODEL), F32)]
        + [pltpu.VMEM(shape, F32) for _, shape in _BRANCH_ACC + _BRANCH_W_SCRATCH],
        name="layer_bwd_loss" if from_loss else "layer_bwd", compiler_params=_params("arbitrary"),
    )(target if from_loss else up, z, h, proj, proj, kept, *consts, *ties)
    return outs[0], outs[1], outs[2], outs[3], dict(zip([n for n, _ in _BRANCH_GRADS], outs[4:]))


def _dx_matmul(dproj, wt_in, dz, ties=(), tm=512):
    s = dproj.shape[0]

    def body(dp_ref, w_ref, dz_ref, *rest):
        o_ref = rest[len(ties)]
        o_ref[...] = _mm(dp_ref[...], w_ref[...]) + ALPHA * dz_ref[...]

    row = lambda i: (i, 0)
    return pl.pallas_call(
        body, grid=(s // tm,),
        in_specs=[pl.BlockSpec((tm, D_IN), row), _full_spec(wt_in), pl.BlockSpec((tm, D_MODEL), row)]
        + _tie_specs(ties),
        out_specs=pl.BlockSpec((tm, D_MODEL), row),
        out_shape=jax.ShapeDtypeStruct((s, D_MODEL), F32), name="dx_mm", compiler_params=_params("arbitrary"),
    )(dproj, wt_in, dz, *ties)


def _dw_in_matmul(x, dproj, ties=(), tk=512):
    s = x.shape[0]
    nk = s // tk
    blk = 2 * W_IN_COLS

    def body(x_ref, dp_ref, *rest):
        o_ref, acc = rest[len(ties):]
        k = pl.program_id(0)

        @pl.when(k == 0)
        def _():
            acc[...] = jnp.zeros_like(acc)

        xb = x_ref[...].astype(MM_DTYPE)
        for j in range(D_IN // blk):
            acc[j * blk:(j + 1) * blk, :] += _mm_tn(dp_ref[:, j * blk:(j + 1) * blk], xb)

        @pl.when(k == nk - 1)
        def _():
            o_ref[...] = acc[...].astype(o_ref.dtype)

    return pl.pallas_call(
        body, grid=(nk,),
        in_specs=[pl.BlockSpec((tk, D_MODEL), lambda k: (k, 0)), pl.BlockSpec((tk, D_IN), lambda k: (k, 0))]
        + _tie_specs(ties),
        out_specs=pl.BlockSpec((D_IN, D_MODEL), lambda k: (0, 0)),
        out_shape=jax.ShapeDtypeStruct((D_IN, D_MODEL), GRAD_DTYPE),
        scratch_shapes=[pltpu.VMEM((D_IN, D_MODEL), F32)], name="dw_in_mm", compiler_params=_params("arbitrary"),
    )(x, dproj, *ties)


def _in_bwd(x, dproj, wt_in, dz, tm=512):
    s = x.shape[0]
    n_steps = s // tm

    assert wt_in.dtype == GRAD_DTYPE
    blk = 2 * W_IN_COLS

    def body(x_ref, dp_ref, w_hbm, dz_ref, o_ref, gw_hbm, w_vmem, acc, sem):
        i = pl.program_id(0)

        @pl.when(i == 0)
        def _():
            fetch = pltpu.make_async_copy(w_hbm, w_vmem, sem)
            fetch.start()
            acc[...] = jnp.zeros_like(acc)
            fetch.wait()

        o_ref[...] = _mm(dp_ref[...], w_vmem[...]) + ALPHA * dz_ref[...]
        xb = x_ref[...].astype(MM_DTYPE)
        for j in range(D_IN // blk):
            acc[j * blk:(j + 1) * blk, :] += _mm_tn(dp_ref[:, j * blk:(j + 1) * blk], xb)

        @pl.when(i == n_steps - 1)
        def _():
            w_vmem[...] = acc[...].astype(w_vmem.dtype)
            emit = pltpu.make_async_copy(w_vmem, gw_hbm, sem)
            emit.start()
            emit.wait()

    row = lambda i: (i, 0)
    any_spec = pl.BlockSpec(memory_space=pl.ANY)
    return pl.pallas_call(
        body, grid=(n_steps,),
        in_specs=[pl.BlockSpec((tm, D_MODEL), row), pl.BlockSpec((tm, D_IN), row), any_spec,
                  pl.BlockSpec((tm, D_MODEL), row)],
        out_specs=(pl.BlockSpec((tm, D_MODEL), row), any_spec),
        out_shape=(jax.ShapeDtypeStruct((s, D_MODEL), F32), jax.ShapeDtypeStruct((D_IN, D_MODEL), GRAD_DTYPE)),
        scratch_shapes=[pltpu.VMEM((D_IN, D_MODEL), wt_in.dtype), pltpu.VMEM((D_IN, D_MODEL), F32),
                        pltpu.SemaphoreType.DMA],
        name="in_bwd", compiler_params=_params("arbitrary"),
    )(x, dproj, wt_in, dz)


def _kv_bwd(mem, dk, dv):
    def body(mem_ref, dk_ref, dv_ref, o_ref):
        grp = _head_of_lane((MEM_LEN, D_G))
        dk_sum = jnp.zeros((MEM_LEN, D_G), F32)
        dv_sum = jnp.zeros((MEM_LEN, D_G), F32)
        for h in range(N_SUB):
            dk_sum = dk_sum + jnp.where(grp == h, dk_ref[h], 0.0)
            dv_sum = dv_sum + jnp.where(grp == h, dv_ref[h], 0.0)
        o_ref[...] = _mm_tn(mem_ref[...], jnp.concatenate([dk_sum, dv_sum], axis=1)).astype(o_ref.dtype)

    return pl.pallas_call(body, out_shape=jax.ShapeDtypeStruct((D_MODEL, 2 * D_G), GRAD_DTYPE), name="kv_bwd",
                          compiler_params=_params())(mem, dk, dv)


def _layer_fwd(x, mem, layer, repl, gw, proj=None, km_vm=None, ties=()):
    km, vm = _kv_project(mem, gw["w_kv"]) if km_vm is None else km_vm
    proj, h, z, kept, xn = _layer_fwd_fused(x, gw["wt_in"] if proj is None else None, proj, km, vm, layer, repl,
                                            gw["pw"], gw["ca8"], gw["dw8"], gw["w_out"], layer < DEPTH - 1, ties)
    return xn, (x, proj, h, z, km, vm, kept)


def _layer_bwd_a(up, target, mem, layer, repl, gw, saved, ties=()):
    x_in, proj, h, z, km, vm, kept = saved
    dz, dproj, g_w_out, g1024, bg = _layer_bwd_fused(up, target, z, h, proj, kept, km, vm, layer, repl, gw["w_out"],
                                                     gw["pw"], gw["ca8"], gw["dw8"], ties)
    grads = {n: bg[n] for n in ("g256", "sg_w", "sg_b", "pool_w", "conv_a_w", "cc_dw_w")}
    grads.update(w_out=g_w_out.reshape(N_DEV, D_MIX // N_DEV, D_MODEL), g1024=g1024,
                 w_kv=_kv_bwd(mem, bg["dk"], bg["dv"]).reshape(N_DEV, D_MODEL // N_DEV, 2 * D_G),
                 cc_pw_w=bg["cc_pw_w"].reshape(N_DEV, CONV_CH, D_G))
    return dz, dproj, grads


def _landing_shapes(items):
    out = []
    for a, scatter, pick in items:
        shape = a.shape if scatter else (N_DEV,) + (a.shape if pick is None else a.shape[1:])
        out.append(jax.ShapeDtypeStruct(shape, a.dtype))
    return tuple(out)


def _exchange_sems(n):
    return [pltpu.SemaphoreType.DMA(((N_DEV - 1) * n,)), pltpu.SemaphoreType.DMA(((N_DEV - 1) * n,)),
            pltpu.SemaphoreType.DMA((n,))]


def _exchange_copies(modes, ins, outs, send_sems, recv_sems, local_sems):
    n = len(ins)
    x, y, c = lax.axis_index("x"), lax.axis_index("y"), lax.axis_index("c")
    me = 4 * x + 2 * y + c

    def src_of(a, dest):
        scatter, pick = modes[a]
        if scatter:
            return ins[a].at[dest]
        return ins[a] if pick is None else ins[a].at[pick]

    local = [pltpu.make_async_copy(src_of(a, me), outs[a].at[me], local_sems.at[a]) for a in range(n)]
    sends, recvs = [], []
    for k in range(1, N_DEV):
        px = 1 - x if k & 4 else x
        py = 1 - y if k & 2 else y
        pc = 1 - c if k & 1 else c
        peer = 4 * px + 2 * py + pc
        for a in range(n):
            sems = dict(send_sem=send_sems.at[(k - 1) * n + a], recv_sem=recv_sems.at[(k - 1) * n + a],
                        device_id=(px, py, pc), device_id_type=pl.DeviceIdType.MESH)
            sends.append(pltpu.make_async_remote_copy(src_ref=src_of(a, peer), dst_ref=outs[a].at[me], **sems))
            recvs.append(pltpu.make_async_remote_copy(src_ref=src_of(a, peer), dst_ref=outs[a].at[peer], **sems))
    return local, sends, recvs


def _gather_two_level(items, name):
    n = len(items)
    assert not any(scatter for _, scatter, _ in items)
    picks = [pick for _, _, pick in items]

    def body(*refs):
        ins, outs = refs[:n], refs[n:2 * n]
        send_sems, recv_sems, local_sems = refs[2 * n:]
        x, y, c = lax.axis_index("x"), lax.axis_index("y"), lax.axis_index("c")
        sib = 1 - c
        chips = [(1 - x, y), (x, 1 - y), (1 - x, 1 - y)]

        def slot(a, px, py, pc):
            return outs[a].at[4 * px + 2 * py + pc]

        def copy(k, a, src, block, to):
            return pltpu.make_async_remote_copy(
                src_ref=src, dst_ref=slot(a, *block), send_sem=send_sems.at[k * n + a],
                recv_sem=recv_sems.at[k * n + a], device_id=to, device_id_type=pl.DeviceIdType.MESH)

        own = [ins[a] if picks[a] is None else ins[a].at[picks[a]] for a in range(n)]
        local = [pltpu.make_async_copy(own[a], slot(a, x, y, c), local_sems.at[a]) for a in range(n)]
        first = [copy(0, a, own[a], (x, y, c), (x, y, sib)) for a in range(n)]
        first += [copy(1 + j, a, own[a], (x, y, c), (*chip, c)) for j, chip in enumerate(chips[:2]) for a in range(n)]
        for cp in local + first:
            cp.start()

        def pass_on(j, a):
            chip = chips[j]
            copy(1 + j, a, own[a], (*chip, c), (x, y, c)).wait_recv()
            fwd = copy(4 + j, a, slot(a, *chip, c), (*chip, c), (x, y, sib))
            fwd.start()
            return fwd

        passed = [pass_on(j, a) for j in range(2) for a in range(n)]
        south = c == 0
        via = tuple(jnp.where(south, p, q) for p, q in zip(chips[0], chips[1]))
        blk = tuple(jnp.where(south, q, p) for p, q in zip(chips[0], chips[1]))
        relayed = [copy(3, a, slot(a, *blk, c), (*blk, c), (*via, c)) for a in range(n)]
        for cp in relayed:
            cp.start()
        passed += [pass_on(2, a) for a in range(n)]
        first += relayed
        for a in range(n):
            copy(0, a, own[a], (x, y, sib), (x, y, c)).wait_recv()
        for j, chip in enumerate(chips):
            for a in range(n):
                copy(4 + j, a, own[a], (*chip, sib), (x, y, c)).wait_recv()
        for cp in first + passed:
            cp.wait_send()
        for cp in local:
            cp.wait()

    any_spec = pl.BlockSpec(memory_space=pl.ANY)
    return pl.pallas_call(
        body, in_specs=[any_spec] * n, out_specs=(any_spec,) * n, out_shape=_landing_shapes(items),
        scratch_shapes=[pltpu.SemaphoreType.DMA((7 * n,)), pltpu.SemaphoreType.DMA((7 * n,)),
                        pltpu.SemaphoreType.DMA((n,))],
        name=name,
    )(*[a for a, _, _ in items])


_HBM_SPEC = pl.BlockSpec(memory_space=pltpu.HBM)
_SEM_SPEC = pl.BlockSpec(memory_space=pltpu.SEMAPHORE)
_SPLIT_PARAMS = pltpu.CompilerParams(has_side_effects=pltpu.SideEffectType.DATAFLOW_SIDE_EFFECTING)


def _split_start(srcs, lands, plan, sem_shapes, name):
    n_src, n_land = len(srcs), len(lands)
    n_buf = n_src + n_land
    bufs = [pltpu.with_memory_space_constraint(a, pltpu.HBM) for a in list(srcs) + list(lands)]

    def body(*refs):
        local, sends, _ = plan(refs[:n_src], refs[n_src:n_buf], *refs[n_buf:n_buf + 3])
        for cp in local + sends:
            cp.start()
        token = refs[-1]
        token[...] = jnp.zeros_like(token)

    res = pl.pallas_call(
        body, name=name, in_specs=[_HBM_SPEC] * n_buf,
        out_shape=tuple(sem_shapes) + tuple(pltpu.HBM(a.shape, a.dtype) for a in bufs)
        + (jax.ShapeDtypeStruct((8, 128), F32),),
        out_specs=(_SEM_SPEC,) * 3 + (_HBM_SPEC,) * n_buf + (pl.BlockSpec(memory_space=pltpu.VMEM),),
        input_output_aliases={i: 3 + i for i in range(n_buf)}, compiler_params=_SPLIT_PARAMS,
    )(*bufs)
    return dict(sems=res[:3], srcs=res[3:3 + n_src], lands=res[3 + n_src:3 + n_buf], token=res[-1], plan=plan)


def _split_wait(ticket, after, name):
    n_src, n_land = len(ticket["srcs"]), len(ticket["lands"])
    n_buf = n_src + n_land
    plan = ticket["plan"]

    def body(*refs):
        local, sends, recvs = plan(refs[:n_src], refs[n_src:n_buf], *refs[n_buf:n_buf + 3])
        for cp in recvs:
            cp.wait_recv()
        for cp in sends:
            cp.wait_send()
        for cp in local:
            cp.wait()

    bufs = list(ticket["srcs"]) + list(ticket["lands"])
    res = pl.pallas_call(
        body, name=name, in_specs=[_HBM_SPEC] * n_buf + [_SEM_SPEC] * 3 + [pl.BlockSpec(memory_space=pl.ANY)],
        out_shape=tuple(pltpu.HBM(a.shape, a.dtype) for a in bufs), out_specs=(_HBM_SPEC,) * n_buf,
        input_output_aliases={i: i for i in range(n_buf)}, compiler_params=_SPLIT_PARAMS,
    )(*bufs, *ticket["sems"], after)
    return res[n_src:]


def _empty_landings(items):
    return [lax.empty(s.shape, s.dtype) for s in _landing_shapes(items)]


def _exchange_start(items, name):
    modes = [(scatter, pick) for _, scatter, pick in items]
    plan = lambda ins, outs, *sems: _exchange_copies(modes, ins, outs, *sems)
    return _split_start([a for a, _, _ in items], _empty_landings(items), plan, _exchange_sems(len(items)), name)


def _two_level_plans(picks):
    n = len(picks)

    def place():
        x, y, c = lax.axis_index("x"), lax.axis_index("y"), lax.axis_index("c")
        return x, y, c, 1 - c, [(1 - x, y), (x, 1 - y), (1 - x, 1 - y)]

    def copy(outs, send_sems, recv_sems, k, a, src, block, to):
        px, py, pc = block
        return pltpu.make_async_remote_copy(
            src_ref=src, dst_ref=outs[a].at[4 * px + 2 * py + pc], send_sem=send_sems.at[k * n + a],
            recv_sem=recv_sems.at[k * n + a], device_id=to, device_id_type=pl.DeviceIdType.MESH)

    def between_chips(ins, outs, send_sems, recv_sems, local_sems):
        x, y, c, sib, chips = place()
        own = [ins[a] if picks[a] is None else ins[a].at[picks[a]] for a in range(n)]
        mk = lambda *args: copy(outs, send_sems, recv_sems, *args)
        local = [pltpu.make_async_copy(own[a], outs[a].at[4 * x + 2 * y + c], local_sems.at[a]) for a in range(n)]
        sends = [mk(0, a, own[a], (x, y, c), (x, y, sib)) for a in range(n)]
        sends += [mk(1 + j, a, own[a], (x, y, c), (*chip, c)) for j, chip in enumerate(chips) for a in range(n)]
        recvs = [mk(0, a, own[a], (x, y, sib), (x, y, c)) for a in range(n)]
        recvs += [mk(1 + j, a, own[a], (*chip, c), (x, y, c)) for j, chip in enumerate(chips) for a in range(n)]
        return local, sends, recvs

    def within_chip(ins, outs, send_sems, recv_sems, local_sems):
        x, y, c, sib, chips = place()
        mk = lambda *args: copy(outs, send_sems, recv_sems, *args)
        slot = lambda a, px, py, pc: outs[a].at[4 * px + 2 * py + pc]
        sends = [mk(j, a, slot(a, *chip, c), (*chip, c), (x, y, sib)) for j, chip in enumerate(chips)
                 for a in range(n)]
        recvs = [mk(j, a, slot(a, *chip, c), (*chip, sib), (x, y, c)) for j, chip in enumerate(chips)
                 for a in range(n)]
        return [], sends, recvs

    sems = lambda k: [pltpu.SemaphoreType.DMA((k * n,)), pltpu.SemaphoreType.DMA((k * n,)),
                      pltpu.SemaphoreType.DMA((n,))]
    return between_chips, sems(4), within_chip, sems(3)


def _adam_math(g, w, m, v):
    m_new = ADAM_B1 * m + (1.0 - ADAM_B1) * g
    v_new = ADAM_B2 * v + (1.0 - ADAM_B2) * (g * g)
    m_hat = m_new / (1.0 - ADAM_B1 ** ADAM_STEP)
    v_hat = v_new / (1.0 - ADAM_B2 ** ADAM_STEP)
    return -ADAM_LR * (m_hat / (jnp.sqrt(v_hat) + ADAM_EPS) + ADAM_WD * w), m_new, v_new


def _adamw_big(parts, w, m, v, layer, prev, name, tr):
    depth, rows, cols = w.shape

    def body(p_ref, w_ref, m_ref, v_ref, *rest):
        g_out, d_out, m_out, v_out = rest[len(prev):]
        g = p_ref[0].astype(F32)
        for q in range(1, N_DEV):
            g = g + p_ref[q].astype(F32)
        d, m_new, v_new = _adam_math(g, w_ref[...], m_ref[...], v_ref[...])
        g_out[...] = g
        d_out[...] = d
        m_out[...] = m_new
        v_out[...] = v_new

    blk = pl.BlockSpec((None, tr, cols), lambda i: (layer, i, 0))
    shp = jax.ShapeDtypeStruct((depth, rows, cols), F32)
    return pl.pallas_call(
        body, grid=(rows // tr,),
        in_specs=[pl.BlockSpec((N_DEV, tr, cols), lambda i: (0, i, 0)), blk, blk, blk]
        + [pl.BlockSpec(memory_space=pl.ANY)] * len(prev),
        out_specs=(blk,) * 4, out_shape=(shp,) * 4,
        input_output_aliases={4 + j: j for j in range(len(prev))},
        name=name, compiler_params=_params("arbitrary"),
    )(parts, w, m, v, *prev)


_SMALL_TENSORS = (("conv_a_w", "conv_a_w", None), ("cc_dw_w", "cc_dw_w", None), ("cc_pw_w", "cc_pw_w", None),
                  ("sg_w", "sg_w", None), ("pool_w", "pool_w", None), ("sg_b", "sg_b", None)) \
    + tuple((n, "g256", k) for k, n in enumerate(G256_ROWS)) + tuple((n, "g1024", k) for k, n in enumerate(G1024_ROWS))
_SMALL_LANDINGS = ("conv_a_w", "cc_dw_w", "cc_pw_w", "sg_w", "pool_w", "sg_b", "g256", "g1024")
_TAPS_FIRST = ("conv_a_w", "cc_dw_w")


def _adamw_small(landings, wts, mom, var):
    names = [n for n, _, _ in _SMALL_TENSORS]
    n_land = DEPTH * len(_SMALL_LANDINGS)
    n_t = len(names)

    def body(*refs):
        land = [dict(zip(_SMALL_LANDINGS, refs[l * len(_SMALL_LANDINGS):(l + 1) * len(_SMALL_LANDINGS)]))
                for l in range(DEPTH)]
        w_refs = dict(zip(names, refs[n_land:n_land + n_t]))
        m_refs = dict(zip(names, refs[n_land + n_t:n_land + 2 * n_t]))
        v_refs = dict(zip(names, refs[n_land + 2 * n_t:n_land + 3 * n_t]))
        outs = refs[n_land + 3 * n_t:]
        out_refs = {n: outs[4 * k:4 * k + 4] for k, n in enumerate(names)}
        loss_ref = outs[4 * n_t]
        for name, key, row in _SMALL_TENSORS:
            for l in range(DEPTH):
                src = land[l][key]
                if row is not None:
                    part = lambda q: src[q, row:row + 1, :]
                    at = (slice(l, l + 1),)
                elif name == "sg_b":
                    part = lambda q: src[q, 0:N_SUB, :]
                    at = (l,)
                elif name in _TAPS_FIRST:
                    part = lambda q: src[q]
                    at = (slice(None), l)
                else:
                    part = lambda q: src[q]
                    at = (l,)
                g = part(0).astype(F32)
                for q in range(1, N_DEV):
                    g = g + part(q).astype(F32)
                d, m_new, v_new = _adam_math(g, w_refs[name][at], m_refs[name][at], v_refs[name][at])
                for ref, val in zip(out_refs[name], (g, d, m_new, v_new)):
                    ref[at] = val
        src = land[DEPTH - 1]["g1024"]
        loss = src[0, LOSS_ROW:LOSS_ROW + 1, 0:128]
        for q in range(1, N_DEV):
            loss = loss + src[q, LOSS_ROW:LOSS_ROW + 1, 0:128]
        loss_ref[...] = loss

    ins = [landings[l][k] for l in range(DEPTH) for k in _SMALL_LANDINGS] \
        + [src[n] for src in (wts, mom, var) for n in names]
    out_shape = tuple(jax.ShapeDtypeStruct(wts[n].shape, F32) for n in names for _ in range(4)) \
        + (jax.ShapeDtypeStruct((1, 128), F32),)
    res = pl.pallas_call(body, out_shape=out_shape, name="adamw_small", compiler_params=_params())(*ins)
    return {n: res[4 * k:4 * k + 4] for k, n in enumerate(names)}, res[4 * n_t]


_BIG = (("w_in", 64), ("w_out", 32), ("w_kv", 32))
_GRAD_ITEMS_EARLY = ("w_out", "w_kv", "cc_pw_w", "conv_a_w", "cc_dw_w")
_GRAD_ITEMS_REPL = ("g256", "sg_w", "sg_b", "pool_w", "g1024")


def _grad_items(grads, with_w_in):
    items = [(grads[n], True, None) for n in (("w_in",) if with_w_in else ()) + _GRAD_ITEMS_EARLY]
    return items + [(grads[n], False, None) for n in _GRAD_ITEMS_REPL]


def _landed(parts, with_w_in):
    names = (("w_in",) if with_w_in else ()) + _GRAD_ITEMS_EARLY + _GRAD_ITEMS_REPL
    return dict(zip(names, parts))


def _gathered_weights(wt_in8, w_kv8, w_out8, pw8, ca8, dw8):
    return dict(wt_in=wt_in8.reshape(D_IN, D_MODEL), w_kv=w_kv8.reshape(D_MODEL, 2 * D_G),
                w_out=w_out8.reshape(D_MIX, D_MODEL), pw=pw8.reshape(D_G, D_G), ca8=ca8, dw8=dw8)


def kernel(x, mem, w_in, conv_a_w, sg_ln_g, sg_ln_b, sg_w, sg_b, pool_w, pool_scale, cc_dw_w, cc_dw_b, cc_ln_g, cc_ln_b, cc_pw_w, w_kv, w_out, ln_g, ln_b, loss_target, m_w_in, m_conv_a_w, m_sg_ln_g, m_sg_ln_b, m_sg_w, m_sg_b, m_pool_w, m_pool_scale, m_cc_dw_w, m_cc_dw_b, m_cc_ln_g, m_cc_ln_b, m_cc_pw_w, m_w_kv, m_w_out, m_ln_g, m_ln_b, v_w_in, v_conv_a_w, v_sg_ln_g, v_sg_ln_b, v_sg_w, v_sg_b, v_pool_w, v_pool_scale, v_cc_dw_w, v_cc_dw_b, v_cc_ln_g, v_cc_ln_b, v_cc_pw_w, v_w_kv, v_w_out, v_ln_g, v_ln_b):
    names = ("w_in", "conv_a_w", "sg_ln_g", "sg_ln_b", "sg_w", "sg_b", "pool_w", "pool_scale", "cc_dw_w", "cc_dw_b",
             "cc_ln_g", "cc_ln_b", "cc_pw_w", "w_kv", "w_out", "ln_g", "ln_b")
    wts = dict(zip(names, (w_in, conv_a_w, sg_ln_g, sg_ln_b, sg_w, sg_b, pool_w, pool_scale, cc_dw_w, cc_dw_b,
                           cc_ln_g, cc_ln_b, cc_pw_w, w_kv, w_out, ln_g, ln_b)))
    mom = dict(zip(names, (m_w_in, m_conv_a_w, m_sg_ln_g, m_sg_ln_b, m_sg_w, m_sg_b, m_pool_w, m_pool_scale,
                           m_cc_dw_w, m_cc_dw_b, m_cc_ln_g, m_cc_ln_b, m_cc_pw_w, m_w_kv, m_w_out, m_ln_g, m_ln_b)))
    var = dict(zip(names, (v_w_in, v_conv_a_w, v_sg_ln_g, v_sg_ln_b, v_sg_w, v_sg_b, v_pool_w, v_pool_scale,
                           v_cc_dw_w, v_cc_dw_b, v_cc_ln_g, v_cc_ln_b, v_cc_pw_w, v_w_kv, v_w_out, v_ln_g, v_ln_b)))
    repl = wts
    xs, mems, tgt = x[0], mem[0], loss_target[0]
    turned = {"w_in": (0, 2, 1), "conv_a_w": (1, 0, 2), "cc_dw_w": (1, 0, 2)}
    wts, mom, var = [{n: (jnp.transpose(a, turned[n]) if n in turned else a) for n, a in src.items()}
                     for src in (wts, mom, var)]
    wb = {n: wts[n].astype(MM_DTYPE) for n in ("w_in", "w_kv", "w_out", "cc_pw_w")}

    wt8_0, wkv8_0 = _gather_two_level([(wb["w_in"], False, 0), (wb["w_kv"], False, 0)], "gather_weights_0a")
    rest_0 = _exchange_start([(wb["w_out"], False, 0), (wb["cc_pw_w"], False, 0), (wts["conv_a_w"], False, None),
                              (wts["cc_dw_w"], False, None)], "gather_weights_0b_start")
    km_vm0 = _kv_project(mems, wkv8_0.reshape(D_MODEL, 2 * D_G))
    proj0 = _proj_matmul(xs, wt8_0.reshape(D_IN, D_MODEL), (rest_0["token"],))
    wo8_0, pw8_0, ca8, dw8 = _split_wait(rest_0, proj0, "gather_weights_0b_wait")
    gw0 = _gathered_weights(wt8_0, wkv8_0, wo8_0, pw8_0, ca8, dw8)
    items_1 = [(wb[n], False, 1) for n in ("w_in", "w_kv", "w_out", "cc_pw_w")]
    between_chips, sems_a, within_chip, sems_b = _two_level_plans([1] * len(items_1))
    chips_1 = _split_start([a for a, _, _ in items_1], _empty_landings(items_1), between_chips, sems_a,
                           "gather_weights_1a_start")
    x1, saved0 = _layer_fwd(xs, mems, 0, repl, gw0, proj0, km_vm0, (chips_1["token"],))
    core_1 = _split_start([], _split_wait(chips_1, x1, "gather_weights_1a_wait"), within_chip, sems_b,
                          "gather_weights_1b_start")
    gw1 = _gathered_weights(*_split_wait(core_1, core_1["token"], "gather_weights_1b_wait"), ca8, dw8)
    _, saved1 = _layer_fwd(x1, mems, 1, repl, gw1)

    dz1, dproj1, g1 = _layer_bwd_a(None, tgt, mems, 1, repl, gw1, saved1)
    shards = lambda g: g.reshape(N_DEV, W_IN_COLS, D_MODEL)
    up, g_wt_in_1 = _in_bwd(saved1[0], dproj1, gw1["wt_in"], dz1)
    g1["w_in"] = shards(g_wt_in_1)
    grads_1 = _exchange_start(_grad_items(g1, True), "exchange_grads_1_start")
    dz0, dproj0, g0 = _layer_bwd_a(up, None, mems, 0, repl, gw0, saved0, (grads_1["token"],))
    early_0 = _exchange_start(_grad_items(g0, False), "exchange_grads_0a_start")
    g_wt_in_0 = _dw_in_matmul(saved0[0], dproj0, (early_0["token"],))
    late_0 = _exchange_start([(shards(g_wt_in_0), True, None)], "exchange_grads_0b_start")
    grad_x = _dx_matmul(dproj0, gw0["wt_in"], dz0, (late_0["token"],))

    landed = [None, _landed(_split_wait(grads_1, grad_x, "exchange_grads_1_wait"), True)]
    big = {}
    for n, tr in _BIG:
        big[n] = _adamw_big(landed[1][n], wts[n], mom[n], var[n], 1, (), "adamw_" + n + "_1", tr)
    landed[0] = _landed(_split_wait(early_0, big["w_kv"][0], "exchange_grads_0a_wait"), False)
    for n, tr in _BIG[1:]:
        big[n] = _adamw_big(landed[0][n], wts[n], mom[n], var[n], 0, big[n], "adamw_" + n + "_0", tr)
    small, loss = _adamw_small(landed, wts, mom, var)
    (landed[0]["w_in"],) = _split_wait(late_0, loss, "exchange_grads_0b_wait")
    big["w_in"] = _adamw_big(landed[0]["w_in"], wts["w_in"], mom["w_in"], var["w_in"], 0, big["w_in"],
                             "adamw_w_in_0", _BIG[0][1])

    res = {**small, **big}
    res = {n: ([jnp.transpose(a, turned[n]) for a in r] if n in turned else r) for n, r in res.items()}
    return (loss[0, 0], grad_x[None], *[res[n][0] for n in names], *[res[n][1] for n in names],
            *[res[n][2] for n in names], *[res[n][3] for n in names])
```

```python
import math

import jax
import jax.numpy as jnp
from jax import lax
from jax.experimental import pallas as pl
from jax.experimental.pallas import tpu as pltpu

F32 = jnp.float32
MM_DTYPE = jnp.bfloat16
GRAD_DTYPE = jnp.bfloat16

D_MODEL = 1024
DEPTH = 2
D_G = 256
N_GROUPS = 5
D_MIX = N_GROUPS * D_G
N_SUB = 4
HEAD_DIM = D_G // N_SUB
CONV_A = 3
CONV_D = 31
CHUNK = 128
POOL_WINDOWS = (2, 4, 8, 16)
MEM_LEN = 256
LN_EPS = 1e-5
ALPHA = (2.0 * DEPTH) ** 0.25
D_IN = 9 * D_G + D_MIX
ATT_SCALE = 1.0 / math.sqrt(HEAD_DIM)

ADAM_LR = 0.001
ADAM_B1 = 0.9
ADAM_B2 = 0.999
ADAM_EPS = 1e-08
ADAM_WD = 0.01
ADAM_STEP = 10

N_DEV = 8
W_IN_COLS = D_IN // N_DEV
CONV_CH = D_G // N_DEV
HALO = 32
FWD_TILE = 512
BWD_TILE = 256
VMEM_LIMIT = 56 * 1024 * 1024

C_XA, C_BA, C_CA, C_U, C_V, C_XC, C_DA, C_DG, C_Q = range(9)
C_GATE = 9 * D_G

G256_ROWS = ("sg_ln_g", "sg_ln_b", "pool_scale", "cc_dw_b", "cc_ln_g", "cc_ln_b")
G1024_ROWS = ("ln_g", "ln_b")
LOSS_ROW = 2


def _mm(a, b):
    return jnp.dot(a.astype(MM_DTYPE), b.astype(MM_DTYPE), preferred_element_type=F32)


def _mm_nt(a, b):
    return lax.dot_general(a.astype(MM_DTYPE), b.astype(MM_DTYPE), (((1,), (1,)), ((), ())),
                           preferred_element_type=F32)


def _mm_tn(a, b):
    return lax.dot_general(a.astype(MM_DTYPE), b.astype(MM_DTYPE), (((0,), (0,)), ((), ())),
                           preferred_element_type=F32)


def _sigmoid(x):
    return 0.5 * jnp.tanh(0.5 * x) + 0.5


_GELU_C = math.sqrt(2.0 / math.pi)
_GELU_A = 0.044715


def _gelu(x):
    th = jnp.tanh(_GELU_C * (x + _GELU_A * (x * x * x)))
    return 0.5 * x * (1.0 + th), th


def _dgelu(x, th):
    return 0.5 * (1.0 + th) + 0.5 * x * (1.0 - th * th) * (_GELU_C * (1.0 + 3.0 * _GELU_A * (x * x)))


def _ln_fwd(x, g, b):
    mu = jnp.mean(x, axis=-1, keepdims=True)
    xc = x - mu
    var = jnp.mean(xc * xc, axis=-1, keepdims=True)
    rstd = lax.rsqrt(var + LN_EPS)
    xhat = xc * rstd
    return xhat * g + b, xhat, rstd


def _ln_bwd(dy, xhat, rstd, g):
    dxhat = dy * g
    m1 = jnp.mean(dxhat, axis=-1, keepdims=True)
    m2 = jnp.mean(dxhat * xhat, axis=-1, keepdims=True)
    return rstd * (dxhat - m1 - xhat * m2)


def _rowsum(x):
    return jnp.sum(x, axis=0, keepdims=True)


def _col(ref, k):
    return ref[:, k * D_G:(k + 1) * D_G]


def _head_of_lane(shape):
    return jnp.right_shift(lax.broadcasted_iota(jnp.int32, shape, len(shape) - 1), HEAD_DIM.bit_length() - 1)


def _pool_select(lane_grp, s2, s4, s8, s16):
    return jnp.where(lane_grp == 0, s2, jnp.where(lane_grp == 1, s4, jnp.where(lane_grp == 2, s8, s16)))


def _row_view(ref, layer):
    return ref.at[pl.ds(layer, 1)]


def _make_residues(ext_ref, res_ref):
    rows = res_ref.shape[1]
    for r in range(1, 8):
        res_ref[r - 1] = ext_ref[pl.ds(r, rows), :]


def _rows_at(ext_ref, res_ref, off, tile):
    a, r = divmod(off, 8)
    if r == 0:
        return ext_ref[pl.ds(off, tile), :]
    return res_ref[r - 1, pl.ds(8 * a, tile), :]


def _residue_scratch(tile):
    return pltpu.VMEM((7, HALO + tile - 8, D_G), F32)


PROJ_SEGMENTS = ((C_XA * D_G, (C_CA + 1) * D_G), (C_XC * D_G, (C_XC + 1) * D_G), (C_DA * D_G, (C_DG + 1) * D_G),
                 (C_U * D_G, (C_V + 1) * D_G), (C_Q * D_G, (C_Q + 1) * D_G), (C_GATE, D_IN))


def _branch_forward(p_ref, ph_ref, first, row0, km_ref, vm_ref, w, ext_a, ext_c, ext_d, res_c, res_d, tile,
                    kept_ref=None, produce=None):
    r = {}
    produce = produce or (lambda: None)
    produce()
    produce()
    xa, ba, ca = _col(p_ref, C_XA), _col(p_ref, C_BA), _col(p_ref, C_CA)
    g_a = ca * xa
    ext_a[0:HALO] = jnp.where(first, 0.0, _col(ph_ref, C_CA) * _col(ph_ref, C_XA))
    ext_a[HALO:HALO + tile] = g_a
    conv_a = w["conv_a"][0:1, :] * ext_a[pl.ds(HALO - 2, tile), :]
    for k in range(1, CONV_A):
        conv_a = conv_a + w["conv_a"][k:k + 1, :] * ext_a[pl.ds(HALO - 2 + k, tile), :]
    r.update(xa=xa, ba=ba, ca=ca, g_a=g_a, conv_a=conv_a)
    ya = ba * conv_a

    produce()
    lane_grp = _head_of_lane((tile, D_G))
    trow = row0 + lax.broadcasted_iota(jnp.int32, (tile, D_G), 0)
    win = _pool_select(lane_grp, 2, 4, 8, 16)
    inv_cnt = 1.0 / jnp.minimum(trow + 1, win).astype(F32)
    if kept_ref is None:
        xc = _col(p_ref, C_XC)
        ext_c[0:HALO] = jnp.where(first, 0.0, _col(ph_ref, C_XC))
        ext_c[HALO:HALO + tile] = xc
        _make_residues(ext_c, res_c)
        acc = xc
        sums = {}
        for k in range(1, POOL_WINDOWS[-1]):
            acc = acc + _rows_at(ext_c, res_c, HALO - k, tile)
            if k + 1 in POOL_WINDOWS:
                sums[k + 1] = acc
        ypre = _pool_select(lane_grp, sums[2], sums[4], sums[8], sums[16]) * inv_cnt - xc
    else:
        ypre = kept_ref[:, D_G:2 * D_G]
    pool_mm = _mm(ypre, w["pool_wbd"][...])
    yc = pool_mm * w["pool_scale"][...]
    r.update(lane_grp=lane_grp, inv_cnt=inv_cnt, ypre=ypre, pool_mm=pool_mm)

    produce()
    da, dg = _col(p_ref, C_DA), _col(p_ref, C_DG)
    sig_dg = _sigmoid(dg)
    hd = da * sig_dg
    if kept_ref is None:
        ext_d[0:HALO] = jnp.where(first, 0.0, _col(ph_ref, C_DA) * _sigmoid(_col(ph_ref, C_DG)))
        ext_d[HALO:HALO + tile] = hd
        _make_residues(ext_d, res_d)
        conv_d = w["cc_dw_b"][...] + w["cc_dw_w"][0:1, :] * _rows_at(ext_d, res_d, HALO - (CONV_D - 1), tile)
        for j in range(1, CONV_D):
            conv_d = conv_d + w["cc_dw_w"][j:j + 1, :] * _rows_at(ext_d, res_d, HALO - (CONV_D - 1) + j, tile)
    else:
        conv_d = kept_ref[:, 0:D_G]
    r["kept"] = (conv_d, ypre)
    ln_d, xhat_d, rstd_d = _ln_fwd(conv_d, w["cc_ln_g"][...], w["cc_ln_b"][...])
    sig_ln = _sigmoid(ln_d)
    act_d = ln_d * sig_ln
    yd = _mm(act_d, w["cc_pw_w"][...])
    r.update(da=da, sig_dg=sig_dg, hd=hd, ln_d=ln_d, xhat_d=xhat_d, rstd_d=rstd_d, sig_ln=sig_ln, act_d=act_d)
    produce()

    u, v = _col(p_ref, C_U), _col(p_ref, C_V)
    ug, th_u = _gelu(u)
    vg, th_v = _gelu(v)
    vn, xhat_v, rstd_v = _ln_fwd(vg, w["sg_ln_g"][...], w["sg_ln_b"][...])
    tri = (lax.broadcasted_iota(jnp.int32, (CHUNK, CHUNK), 0)
           >= lax.broadcasted_iota(jnp.int32, (CHUNK, CHUNK), 1))
    wm = [jnp.where(tri, w["sg_w"][h], 0.0).astype(MM_DTYPE) for h in range(N_SUB)]
    lo = lax.broadcasted_iota(jnp.int32, (CHUNK, 2 * HEAD_DIM), 1) < HEAD_DIM
    chunks = []
    for c in range(tile // CHUNK):
        halves = []
        for hf in range(2):
            vh = vn[c * CHUNK:(c + 1) * CHUNK, hf * 128:(hf + 1) * 128]
            halves.append(_mm(wm[2 * hf], jnp.where(lo, vh, 0.0)) + _mm(wm[2 * hf + 1], jnp.where(lo, 0.0, vh)))
        chunks.append(jnp.concatenate(halves, axis=1) + w["sg_bias"][...])
    mixed = jnp.concatenate(chunks, axis=0)
    yb = ug * mixed
    r.update(u=u, v=v, ug=ug, th_u=th_u, th_v=th_v, vn=vn, xhat_v=xhat_v, rstd_v=rstd_v, wm=wm, lo=lo,
             mixed=mixed, tri=tri)

    produce()
    q = _col(p_ref, C_Q)
    ye = jnp.zeros((tile, D_G), F32)
    probs = []
    for h in range(N_SUB):
        s = _mm_nt(q, km_ref[h]) * ATT_SCALE
        e = jnp.exp(s - jnp.max(s, axis=-1, keepdims=True))
        p = e * (1.0 / jnp.sum(e, axis=-1, keepdims=True))
        probs.append(p)
        ye = ye + _mm(p, vm_ref[h])
    r.update(q=q, probs=probs)

    gate = p_ref[:, C_GATE:C_GATE + D_MIX]
    sig_gate = _sigmoid(gate)
    r.update(gate=gate, sig_gate=sig_gate, branch_out=(ya, yb, yc, yd, ye))
    return r


_BRANCH_REPL = ("sg_ln_g", "sg_ln_b", "sg_w", "sg_b", "pool_w", "pool_scale", "cc_dw_b", "cc_ln_g", "cc_ln_b")
_BRANCH_W_SCRATCH = (("conv_a", (CONV_A, D_G)), ("cc_dw_w", (CONV_D, D_G)), ("sg_bias", (CHUNK, D_G)),
                     ("pool_wbd", (D_G, D_G)), ("sgb8", (8, CHUNK)))


def _branch_weights(layer, nat, pw_ref, ca_ref, dw_ref, scr, init):
    @pl.when(init)
    def _():
        for p in range(N_DEV):
            scr["conv_a"][:, p * CONV_CH:(p + 1) * CONV_CH] = ca_ref[p, :, layer, :]
            scr["cc_dw_w"][:, p * CONV_CH:(p + 1) * CONV_CH] = dw_ref[p, :, layer, :]
        scr["sgb8"][...] = jnp.zeros((8, CHUNK), F32)
        scr["sgb8"][0:N_SUB] = nat["sg_b"][layer]
        sgb_t = scr["sgb8"][...].T
        head = _head_of_lane((CHUNK, D_G))
        bias = jnp.zeros((CHUNK, D_G), F32)
        for h in range(N_SUB):
            bias = jnp.where(head == h, sgb_t[:, h:h + 1], bias)
        scr["sg_bias"][...] = bias
        scr["pool_wbd"][...] = jnp.zeros((D_G, D_G), F32)
        for gi in range(N_SUB):
            sl = slice(gi * HEAD_DIM, (gi + 1) * HEAD_DIM)
            scr["pool_wbd"][sl, sl] = nat["pool_w"][layer, gi]

    w = {n: _row_view(nat[n], layer) for n in ("sg_ln_g", "sg_ln_b", "pool_scale", "cc_dw_b", "cc_ln_g", "cc_ln_b")}
    w.update(conv_a=scr["conv_a"], cc_dw_w=scr["cc_dw_w"], sg_bias=scr["sg_bias"], pool_wbd=scr["pool_wbd"],
             sg_w=nat["sg_w"].at[layer], cc_pw_w=pw_ref)
    return w


def _full_spec(a):
    nd = a.ndim
    return pl.BlockSpec(a.shape, lambda *_, _nd=nd: (0,) * _nd)


def _tie_specs(ties):
    return [pl.BlockSpec((8, 128), lambda *_: (0, 0)) for _ in ties]


def _params(*sem):
    return pltpu.CompilerParams(dimension_semantics=sem or None, vmem_limit_bytes=VMEM_LIMIT)


def _proj_matmul(x, wt_in, ties=(), tm=512):
    s, k = x.shape

    def body(x_ref, w_ref, *rest):
        o_ref = rest[len(ties)]
        o_ref[...] = _mm_nt(x_ref[...], w_ref[...])

    return pl.pallas_call(
        body, grid=(s // tm,),
        in_specs=[pl.BlockSpec((tm, k), lambda i: (i, 0)), _full_spec(wt_in)] + _tie_specs(ties),
        out_specs=pl.BlockSpec((tm, D_IN), lambda i: (i, 0)),
        out_shape=jax.ShapeDtypeStruct((s, D_IN), F32), name="proj_mm", compiler_params=_params("arbitrary"),
    )(x, wt_in, *ties)


def _kv_project(mem, w_kv):
    def body(mem_ref, w_ref, km_ref, vm_ref):
        kv = _mm(mem_ref[...], w_ref[...])
        k, v = kv[:, :D_G], kv[:, D_G:]
        grp = _head_of_lane((MEM_LEN, D_G))
        for h in range(N_SUB):
            km_ref[h] = jnp.where(grp == h, k, 0.0).astype(km_ref.dtype)
            vm_ref[h] = jnp.where(grp == h, v, 0.0).astype(vm_ref.dtype)

    shp = jax.ShapeDtypeStruct((N_SUB, MEM_LEN, D_G), MM_DTYPE)
    return pl.pallas_call(body, out_shape=(shp, shp), name="kv_project", compiler_params=_params())(mem, w_kv)


def _layer_fwd_fused(x, wt_in, proj, km, vm, layer, repl, pw, ca8, dw8, w_out, want_xn, ties=(), tile=FWD_TILE):
    s = x.shape[0]
    hb = tile // HALO
    nat_arrays = [repl[n] for n in _BRANCH_REPL]
    n_nat, nt = len(nat_arrays), len(ties)
    given = proj is not None

    def body(x_ref, *rest):
        if given:
            p_ref, ph_ref = rest[:2]
            rest = rest[2:]
        else:
            wt_ref = rest[0]
            rest = rest[1:]
        km_ref, vm_ref = rest[:2]
        nat = dict(zip(_BRANCH_REPL, rest[2:2 + n_nat]))
        pw_ref, ca_ref, dw_ref, wo_ref, g_ref, b_ref = rest[2 + n_nat:8 + n_nat]
        rest = rest[8 + n_nat + nt:]
        if not given:
            p_ref, rest = rest[0], rest[1:]
        h_ref, z_ref, cd_ref = rest[:3]
        rest = rest[3:]
        if want_xn:
            xn_ref, rest = rest[0], rest[1:]
        if not given:
            ph_ref, rest = rest[0], rest[1:]
        ext_a, ext_c, ext_d, res_c, res_d = rest[:5]
        scr = dict(zip([n for n, _ in _BRANCH_W_SCRATCH], rest[5:]))
        i = pl.program_id(0)
        xt = x_ref[...]
        produce = None
        if not given:
            @pl.when(i == 0)
            def _():
                ph_ref[...] = jnp.zeros_like(ph_ref)

            xb = xt.astype(MM_DTYPE)
            segments = iter(PROJ_SEGMENTS)

            def produce():
                lo, hi = next(segments)
                p_ref[:, lo:hi] = _mm_nt(xb, wt_ref[lo:hi, :])

        w = _branch_weights(layer, nat, pw_ref, ca_ref, dw_ref, scr, i == 0)
        r = _branch_forward(p_ref, ph_ref, i == 0, i * tile, km_ref, vm_ref, w, ext_a, ext_c, ext_d, res_c, res_d,
                            tile, None, produce)
        if not given:
            ph_ref[...] = p_ref[tile - HALO:tile, :]
        cd_ref[:, 0:D_G], cd_ref[:, D_G:2 * D_G] = r["kept"]
        h = (jnp.concatenate(r["branch_out"], axis=1) * (r["gate"] * r["sig_gate"])).astype(h_ref.dtype)
        h_ref[...] = h
        z = ALPHA * xt + _mm(h, wo_ref[...])
        z_ref[...] = z
        if want_xn:
            xn_ref[...] = _ln_fwd(z, _row_view(g_ref, layer)[...], _row_view(b_ref, layer)[...])[0]

    row = lambda i: (i, 0)
    consts = [km, vm] + nat_arrays + [pw, ca8, dw8, w_out, repl["ln_g"], repl["ln_b"]]
    act = jax.ShapeDtypeStruct((s, D_MODEL), F32)
    act_spec = pl.BlockSpec((tile, D_MODEL), row)
    if given:
        lead = [proj, proj]
        lead_specs = [pl.BlockSpec((tile, D_IN), row),
                      pl.BlockSpec((HALO, D_IN), lambda i: (jnp.maximum(i * hb - 1, 0), 0))]
        out_specs, out_shape, scratch = (), (), []
    else:
        lead = [wt_in]
        lead_specs = [_full_spec(wt_in)]
        out_specs = (pl.BlockSpec((tile, D_IN), row),)
        out_shape = (jax.ShapeDtypeStruct((s, D_IN), F32),)
        scratch = [pltpu.VMEM((HALO, D_IN), F32)]
    res = pl.pallas_call(
        body, grid=(s // tile,),
        in_specs=[act_spec] + lead_specs + [_full_spec(a) for a in consts] + _tie_specs(ties),
        out_specs=out_specs + (pl.BlockSpec((tile, D_MIX), row), act_spec, pl.BlockSpec((tile, 2 * D_G), row))
        + ((act_spec,) if want_xn else ()),
        out_shape=out_shape + (jax.ShapeDtypeStruct((s, D_MIX), MM_DTYPE), act, jax.ShapeDtypeStruct((s, 2 * D_G), F32))
        + ((act,) if want_xn else ()),
        scratch_shapes=scratch + [pltpu.VMEM((HALO + tile, D_G), F32)] * 3
        + [_residue_scratch(tile)] * 2 + [pltpu.VMEM(shape, F32) for _, shape in _BRANCH_W_SCRATCH],
        name="layer_fwd_given_proj" if given else "layer_fwd", compiler_params=_params("arbitrary"),
    )(x, *lead, *consts, *ties)
    res = ((proj,) + tuple(res)) if given else tuple(res)
    return res if want_xn else res + (None,)


_BRANCH_GRADS = (("g256", (8, D_G)), ("sg_w", (N_SUB, CHUNK, CHUNK)), ("sg_b", (8, CHUNK)),
                 ("pool_w", (N_SUB, HEAD_DIM, HEAD_DIM)), ("conv_a_w", (N_DEV, CONV_A, CONV_CH)),
                 ("cc_dw_w", (N_DEV, CONV_D, CONV_CH)), ("cc_pw_w", (D_G, D_G)),
                 ("dk", (N_SUB, MEM_LEN, D_G)), ("dv", (N_SUB, MEM_LEN, D_G)))
_BRANCH_ACC = (("conv_a", (CONV_A, D_G)), ("cc_dw_w", (CONV_D, D_G)), ("pool_wbd", (D_G, D_G)),
               ("sg_bias", (CHUNK, D_G)), ("sg_w", (N_SUB, CHUNK, CHUNK)))
_BRANCH_GRADS_NARROW = ("sg_w", "pool_w")


def _layer_bwd_fused(up, target, z, h, proj, kept, km, vm, layer, repl, w_out, pw, ca8, dw8, ties=(), tile=BWD_TILE):
    s = proj.shape[0]
    nt = s // tile
    hb = tile // HALO
    nat_arrays = [repl[n] for n in _BRANCH_REPL]
    n_nat, n_grads, n_acc, n_ties = len(nat_arrays), len(_BRANCH_GRADS), len(_BRANCH_ACC), len(ties)
    row_of = {n: k for k, n in enumerate(G256_ROWS)}
    from_loss = target is not None

    def body(o_ref, z_ref, h_ref, p_ref, ph_ref, cd_ref, km_ref, vm_ref, *rest):
        nat = dict(zip(_BRANCH_REPL, rest[:n_nat]))
        pw_ref, ca_ref, dw_ref, lng_ref, lnb_ref, wo_ref = rest[n_nat:n_nat + 6]
        rest = rest[n_nat + 6 + n_ties:]
        dz_ref, dp_ref, gw_ref, slab_ref = rest[:4]
        g = dict(zip([n for n, _ in _BRANCH_GRADS], rest[4:4 + n_grads]))
        rest = rest[4 + n_grads:]
        ext_a, rev_a, rev_c, rev_d, res_rc, res_rd, gw_acc, lacc = rest[:8]
        acc = dict(zip([n for n, _ in _BRANCH_ACC], rest[8:8 + n_acc]))
        scr = dict(zip([n for n, _ in _BRANCH_W_SCRATCH], rest[8 + n_acc:]))
        i = pl.program_id(0)
        t = nt - 1 - i

        @pl.when(i == 0)
        def _():
            for ref in list(g.values()) + list(acc.values()) + [rev_a, rev_c, rev_d, gw_acc, slab_ref, lacc]:
                ref[...] = jnp.zeros_like(ref)

        g_ln = _row_view(lng_ref, layer)[...]
        xn, xhat, rstd = _ln_fwd(z_ref[...], g_ln, _row_view(lnb_ref, layer)[...])
        if from_loss:
            err = xn - o_ref[...]
            lacc[...] += _rowsum(err * err)
            dxn = err * (1.0 / D_MODEL)
        else:
            dxn = o_ref[...]
        slab_ref[0:1, :] += _rowsum(dxn * xhat)
        slab_ref[1:2, :] += _rowsum(dxn)
        dz = _ln_bwd(dxn, xhat, rstd, g_ln)
        dz_ref[...] = dz
        dzb = dz.astype(MM_DTYPE)

        w = _branch_weights(layer, nat, pw_ref, ca_ref, dw_ref, scr, i == 0)
        r = _branch_forward(p_ref, ph_ref, t == 0, t * tile, km_ref, vm_ref, w, ext_a, None, None, None, None,
                            tile, cd_ref)

        def put(k, val, width=D_G):
            dp_ref[:, k:k + width] = val.astype(dp_ref.dtype)

        def add_row(name, val):
            k = row_of[name]
            g["g256"][k:k + 1, :] += val

        def push_rev(rev, val):
            head = rev[0:HALO]
            rev[tile:tile + HALO] = head
            rev[0:tile] = val

        gate, sig_gate = r["gate"], r["sig_gate"]

        def branch_grad(group):
            cols = slice(group * D_G, (group + 1) * D_G)
            dh_g = _mm_nt(dzb, wo_ref[cols, :])
            gate_g, sig_g = gate[:, cols], sig_gate[:, cols]
            put(C_GATE + group * D_G, dh_g * r["branch_out"][group] * (sig_g * (1.0 + gate_g * (1.0 - sig_g))))
            return dh_g * (gate_g * sig_g)

        dya = branch_grad(0)
        dyc = branch_grad(2)

        put(C_BA * D_G, dya * r["conv_a"])
        dconv_a = dya * r["ba"]
        push_rev(rev_a, dconv_a)
        dga = jnp.zeros((tile, D_G), F32)
        for k in range(CONV_A):
            ahead = rev_a[pl.ds(CONV_A - 1 - k, tile), :]
            dga = dga + w["conv_a"][k:k + 1, :] * ahead
            acc["conv_a"][k:k + 1, :] += _rowsum(r["g_a"] * ahead)
        put(C_CA * D_G, dga * r["xa"])
        put(C_XA * D_G, dga * r["ca"])

        dyd = branch_grad(3)
        add_row("pool_scale", _rowsum(dyc * r["pool_mm"]))
        dmm = dyc * w["pool_scale"][...]
        acc["pool_wbd"][...] += _mm_tn(r["ypre"], dmm)
        dypre = _mm_nt(dmm, w["pool_wbd"][...])
        dws = dypre * r["inv_cnt"]
        push_rev(rev_c, dws)
        _make_residues(rev_c, res_rc)
        run = dws
        sums = {}
        for k in range(1, POOL_WINDOWS[-1]):
            run = run + _rows_at(rev_c, res_rc, k, tile)
            if k + 1 in POOL_WINDOWS:
                sums[k + 1] = run
        put(C_XC * D_G, _pool_select(r["lane_grp"], sums[2], sums[4], sums[8], sums[16]) - dypre)

        dyb = branch_grad(1)
        gw_acc[...] += _mm_tn(h_ref[...], dzb)
        g["cc_pw_w"][...] += _mm_tn(r["act_d"], dyd)
        dact = _mm_nt(dyd, w["cc_pw_w"][...])
        sig_ln, ln_d = r["sig_ln"], r["ln_d"]
        dln = dact * (sig_ln * (1.0 + ln_d * (1.0 - sig_ln)))
        add_row("cc_ln_g", _rowsum(dln * r["xhat_d"]))
        add_row("cc_ln_b", _rowsum(dln))
        dconv_d = _ln_bwd(dln, r["xhat_d"], r["rstd_d"], w["cc_ln_g"][...])
        add_row("cc_dw_b", _rowsum(dconv_d))
        push_rev(rev_d, dconv_d)
        _make_residues(rev_d, res_rd)
        dhd = jnp.zeros((tile, D_G), F32)
        for j in range(CONV_D):
            ahead = _rows_at(rev_d, res_rd, CONV_D - 1 - j, tile)
            dhd = dhd + w["cc_dw_w"][j:j + 1, :] * ahead
            acc["cc_dw_w"][j:j + 1, :] += _rowsum(r["hd"] * ahead)
        sig_dg = r["sig_dg"]
        put(C_DA * D_G, dhd * sig_dg)
        put(C_DG * D_G, dhd * r["da"] * sig_dg * (1.0 - sig_dg))

        dye = branch_grad(4)
        dug = dyb * r["mixed"]
        dmixed = dyb * r["ug"]
        wm, lo, vn = r["wm"], r["lo"], r["vn"]
        dvn_chunks = []
        for c in range(tile // CHUNK):
            rows = slice(c * CHUNK, (c + 1) * CHUNK)
            acc["sg_bias"][...] += dmixed[rows, :]
            halves = []
            for hf in range(2):
                cols = slice(hf * 128, (hf + 1) * 128)
                dm = dmixed[rows, cols]
                dm_a, dm_b = jnp.where(lo, dm, 0.0), jnp.where(lo, 0.0, dm)
                vh = vn[rows, cols]
                acc["sg_w"][2 * hf] += _mm_nt(dm_a, vh)
                acc["sg_w"][2 * hf + 1] += _mm_nt(dm_b, vh)
                halves.append(_mm_tn(wm[2 * hf], dm_a) + _mm_tn(wm[2 * hf + 1], dm_b))
            dvn_chunks.append(jnp.concatenate(halves, axis=1))
        dvn = jnp.concatenate(dvn_chunks, axis=0)
        add_row("sg_ln_g", _rowsum(dvn * r["xhat_v"]))
        add_row("sg_ln_b", _rowsum(dvn))
        dvg = _ln_bwd(dvn, r["xhat_v"], r["rstd_v"], w["sg_ln_g"][...])
        put(C_V * D_G, dvg * _dgelu(r["v"], r["th_v"]))
        put(C_U * D_G, dug * _dgelu(r["u"], r["th_u"]))

        q = r["q"]
        dq = jnp.zeros((tile, D_G), F32)
        for h in range(N_SUB):
            p = r["probs"][h]
            dp = _mm_nt(dye, vm_ref[h])
            g["dv"][h] += _mm_tn(p, dye)
            ds = p * (dp - jnp.sum(dp * p, axis=-1, keepdims=True)) * ATT_SCALE
            dq = dq + _mm(ds, km_ref[h])
            g["dk"][h] += _mm_tn(ds, q)
        put(C_Q * D_G, dq)

        @pl.when(i == nt - 1)
        def _():
            for h in range(N_SUB):
                g["sg_w"][h] = jnp.where(r["tri"], acc["sg_w"][h], 0.0).astype(g["sg_w"].dtype)
            lane_head = _head_of_lane((CHUNK, D_G))
            col_of = lax.broadcasted_iota(jnp.int32, (CHUNK, 8), 1)
            ba = acc["sg_bias"][...]
            sgb_t = jnp.zeros((CHUNK, 8), F32)
            for h in range(N_SUB):
                col = jnp.sum(jnp.where(lane_head == h, ba, 0.0), axis=-1, keepdims=True)
                sgb_t = jnp.where(col_of == h, col, sgb_t)
            g["sg_b"][...] = sgb_t.T
            wbd = acc["pool_wbd"][...]
            for gi in range(N_SUB):
                sl = slice(gi * HEAD_DIM, (gi + 1) * HEAD_DIM)
                g["pool_w"][gi] = wbd[sl, sl].astype(g["pool_w"].dtype)
            ca, dw = acc["conv_a"][...], acc["cc_dw_w"][...]
            for p in range(N_DEV):
                g["conv_a_w"][p] = ca[:, p * CONV_CH:(p + 1) * CONV_CH]
                g["cc_dw_w"][p] = dw[:, p * CONV_CH:(p + 1) * CONV_CH]
            gw_ref[...] = gw_acc[...].astype(gw_ref.dtype)
            if from_loss:
                total = jnp.sum(lacc[...], axis=-1, keepdims=True) * (0.5 / D_MODEL)
                slab_ref[LOSS_ROW:LOSS_ROW + 1, :] = jnp.broadcast_to(total, (1, D_MODEL))

    rev = lambda i: (nt - 1 - i, 0)
    fixed = lambda i: (0, 0)
    act_spec = pl.BlockSpec((tile, D_MODEL), rev)
    grad_specs = tuple(pl.BlockSpec(shape, lambda i, _nd=len(shape): (0,) * _nd) for _, shape in _BRANCH_GRADS)
    grad_shapes = tuple(jax.ShapeDtypeStruct(shape, GRAD_DTYPE if n in _BRANCH_GRADS_NARROW else F32)
                        for n, shape in _BRANCH_GRADS)
    consts = [km, vm] + nat_arrays + [pw, ca8, dw8, repl["ln_g"], repl["ln_b"], w_out]
    outs = pl.pallas_call(
        body, grid=(nt,),
        in_specs=[act_spec, act_spec, pl.BlockSpec((tile, D_MIX), rev), pl.BlockSpec((tile, D_IN), rev),
                  pl.BlockSpec((HALO, D_IN), lambda i: (jnp.maximum((nt - 1 - i) * hb - 1, 0), 0)),
                  pl.BlockSpec((tile, 2 * D_G), rev)]
        + [_full_spec(a) for a in consts] + _tie_specs(ties),
        out_specs=(act_spec, pl.BlockSpec((tile, D_IN), rev), pl.BlockSpec((D_MIX, D_MODEL), fixed),
                   pl.BlockSpec((8, D_MODEL), fixed)) + grad_specs,
        out_shape=(jax.ShapeDtypeStruct((s, D_MODEL), F32), jax.ShapeDtypeStruct((s, D_IN), MM_DTYPE),
                   jax.ShapeDtypeStruct((D_MIX, D_MODEL), GRAD_DTYPE), jax.ShapeDtypeStruct((8, D_MODEL), F32))
        + grad_shapes,
        scratch_shapes=[pltpu.VMEM((HALO + tile, D_G), F32)] * 4 + [_residue_scratch(tile)] * 2
        + [pltpu.VMEM((D_MIX, D_MODEL), F32), pltpu.VMEM((1, D_MODEL), F32)]
        + [pltpu.VMEM(shape, F32) for _, shape in _BRANCH_ACC + _BRANCH_W_SCRATCH],
        name="layer_bwd_loss" if from_loss else "layer_bwd", compiler_params=_params("arbitrary"),
    )(target if from_loss else up, z, h, proj, proj, kept, *consts, *ties)
    return outs[0], outs[1], outs[2], outs[3], dict(zip([n for n, _ in _BRANCH_GRADS], outs[4:]))


def _dx_matmul(dproj, wt_in, dz, ties=(), tm=512):
    s = dproj.shape[0]

    def body(dp_ref, w_ref, dz_ref, *rest):
        o_ref = rest[len(ties)]
        o_ref[...] = _mm(dp_ref[...], w_ref[...]) + ALPHA * dz_ref[...]

    row = lambda i: (i, 0)
    return pl.pallas_call(
        body, grid=(s // tm,),
        in_specs=[pl.BlockSpec((tm, D_IN), row), _full_spec(wt_in), pl.BlockSpec((tm, D_MODEL), row)]
        + _tie_specs(ties),
        out_specs=pl.BlockSpec((tm, D_MODEL), row),
        out_shape=jax.ShapeDtypeStruct((s, D_MODEL), F32), name="dx_mm", compiler_params=_params("arbitrary"),
    )(dproj, wt_in, dz, *ties)


def _dw_in_matmul(x, dproj, ties=(), tk=512):
    s = x.shape[0]
    nk = s // tk
    blk = 2 * W_IN_COLS

    def body(x_ref, dp_ref, *rest):
        o_ref, acc = rest[len(ties):]
        k = pl.program_id(0)

        @pl.when(k == 0)
        def _():
            acc[...] = jnp.zeros_like(acc)

        xb = x_ref[...].astype(MM_DTYPE)
        for j in range(D_IN // blk):
            acc[j * blk:(j + 1) * blk, :] += _mm_tn(dp_ref[:, j * blk:(j + 1) * blk], xb)

        @pl.when(k == nk - 1)
        def _():
            o_ref[...] = acc[...].astype(o_ref.dtype)

    return pl.pallas_call(
        body, grid=(nk,),
        in_specs=[pl.BlockSpec((tk, D_MODEL), lambda k: (k, 0)), pl.BlockSpec((tk, D_IN), lambda k: (k, 0))]
        + _tie_specs(ties),
        out_specs=pl.BlockSpec((D_IN, D_MODEL), lambda k: (0, 0)),
        out_shape=jax.ShapeDtypeStruct((D_IN, D_MODEL), GRAD_DTYPE),
        scratch_shapes=[pltpu.VMEM((D_IN, D_MODEL), F32)], name="dw_in_mm", compiler_params=_params("arbitrary"),
    )(x, dproj, *ties)


def _in_bwd(x, dproj, wt_in, dz, tm=512):
    s = x.shape[0]
    n_steps = s // tm

    assert wt_in.dtype == GRAD_DTYPE
    blk = 2 * W_IN_COLS

    def body(x_ref, dp_ref, w_hbm, dz_ref, o_ref, gw_hbm, w_vmem, acc, sem):
        i = pl.program_id(0)

        @pl.when(i == 0)
        def _():
            fetch = pltpu.make_async_copy(w_hbm, w_vmem, sem)
            fetch.start()
            acc[...] = jnp.zeros_like(acc)
            fetch.wait()

        o_ref[...] = _mm(dp_ref[...], w_vmem[...]) + ALPHA * dz_ref[...]
        xb = x_ref[...].astype(MM_DTYPE)
        for j in range(D_IN // blk):
            acc[j * blk:(j + 1) * blk, :] += _mm_tn(dp_ref[:, j * blk:(j + 1) * blk], xb)

        @pl.when(i == n_steps - 1)
        def _():
            w_vmem[...] = acc[...].astype(w_vmem.dtype)
            emit = pltpu.make_async_copy(w_vmem, gw_hbm, sem)
            emit.start()
            emit.wait()

    row = lambda i: (i, 0)
    any_spec = pl.BlockSpec(memory_space=pl.ANY)
    return pl.pallas_call(
        body, grid=(n_steps,),
        in_specs=[pl.BlockSpec((tm, D_MODEL), row), pl.BlockSpec((tm, D_IN), row), any_spec,
                  pl.BlockSpec((tm, D_MODEL), row)],
        out_specs=(pl.BlockSpec((tm, D_MODEL), row), any_spec),
        out_shape=(jax.ShapeDtypeStruct((s, D_MODEL), F32), jax.ShapeDtypeStruct((D_IN, D_MODEL), GRAD_DTYPE)),
        scratch_shapes=[pltpu.VMEM((D_IN, D_MODEL), wt_in.dtype), pltpu.VMEM((D_IN, D_MODEL), F32),
                        pltpu.SemaphoreType.DMA],
        name="in_bwd", compiler_params=_params("arbitrary"),
    )(x, dproj, wt_in, dz)


def _kv_bwd(mem, dk, dv):
    def body(mem_ref, dk_ref, dv_ref, o_ref):
        grp = _head_of_lane((MEM_LEN, D_G))
        dk_sum = jnp.zeros((MEM_LEN, D_G), F32)
        dv_sum = jnp.zeros((MEM_LEN, D_G), F32)
        for h in range(N_SUB):
            dk_sum = dk_sum + jnp.where(grp == h, dk_ref[h], 0.0)
            dv_sum = dv_sum + jnp.where(grp == h, dv_ref[h], 0.0)
        o_ref[...] = _mm_tn(mem_ref[...], jnp.concatenate([dk_sum, dv_sum], axis=1)).astype(o_ref.dtype)

    return pl.pallas_call(body, out_shape=jax.ShapeDtypeStruct((D_MODEL, 2 * D_G), GRAD_DTYPE), name="kv_bwd",
                          compiler_params=_params())(mem, dk, dv)


def _layer_fwd(x, mem, layer, repl, gw, proj=None, km_vm=None, ties=()):
    km, vm = _kv_project(mem, gw["w_kv"]) if km_vm is None else km_vm
    proj, h, z, kept, xn = _layer_fwd_fused(x, gw["wt_in"] if proj is None else None, proj, km, vm, layer, repl,
                                            gw["pw"], gw["ca8"], gw["dw8"], gw["w_out"], layer < DEPTH - 1, ties)
    return xn, (x, proj, h, z, km, vm, kept)


def _layer_bwd_a(up, target, mem, layer, repl, gw, saved, ties=()):
    x_in, proj, h, z, km, vm, kept = saved
    dz, dproj, g_w_out, g1024, bg = _layer_bwd_fused(up, target, z, h, proj, kept, km, vm, layer, repl, gw["w_out"],
                                                     gw["pw"], gw["ca8"], gw["dw8"], ties)
    grads = {n: bg[n] for n in ("g256", "sg_w", "sg_b", "pool_w", "conv_a_w", "cc_dw_w")}
    grads.update(w_out=g_w_out.reshape(N_DEV, D_MIX // N_DEV, D_MODEL), g1024=g1024,
                 w_kv=_kv_bwd(mem, bg["dk"], bg["dv"]).reshape(N_DEV, D_MODEL // N_DEV, 2 * D_G),
                 cc_pw_w=bg["cc_pw_w"].reshape(N_DEV, CONV_CH, D_G))
    return dz, dproj, grads


def _landing_shapes(items):
    out = []
    for a, scatter, pick in items:
        shape = a.shape if scatter else (N_DEV,) + (a.shape if pick is None else a.shape[1:])
        out.append(jax.ShapeDtypeStruct(shape, a.dtype))
    return tuple(out)


def _exchange_sems(n):
    return [pltpu.SemaphoreType.DMA(((N_DEV - 1) * n,)), pltpu.SemaphoreType.DMA(((N_DEV - 1) * n,)),
            pltpu.SemaphoreType.DMA((n,))]


def _exchange_copies(modes, ins, outs, send_sems, recv_sems, local_sems):
    n = len(ins)
    x, y, c = lax.axis_index("x"), lax.axis_index("y"), lax.axis_index("c")
    me = 4 * x + 2 * y + c

    def src_of(a, dest):
        scatter, pick = modes[a]
        if scatter:
            return ins[a].at[dest]
        return ins[a] if pick is None else ins[a].at[pick]

    local = [pltpu.make_async_copy(src_of(a, me), outs[a].at[me], local_sems.at[a]) for a in range(n)]
    sends, recvs = [], []
    for k in range(1, N_DEV):
        px = 1 - x if k & 4 else x
        py = 1 - y if k & 2 else y
        pc = 1 - c if k & 1 else c
        peer = 4 * px + 2 * py + pc
        for a in range(n):
            sems = dict(send_sem=send_sems.at[(k - 1) * n + a], recv_sem=recv_sems.at[(k - 1) * n + a],
                        device_id=(px, py, pc), device_id_type=pl.DeviceIdType.MESH)
            sends.append(pltpu.make_async_remote_copy(src_ref=src_of(a, peer), dst_ref=outs[a].at[me], **sems))
            recvs.append(pltpu.make_async_remote_copy(src_ref=src_of(a, peer), dst_ref=outs[a].at[peer], **sems))
    return local, sends, recvs


def _gather_two_level(items, name):
    n = len(items)
    assert not any(scatter for _, scatter, _ in items)
    picks = [pick for _, _, pick in items]

    def body(*refs):
        ins, outs = refs[:n], refs[n:2 * n]
        send_sems, recv_sems, local_sems = refs[2 * n:]
        x, y, c = lax.axis_index("x"), lax.axis_index("y"), lax.axis_index("c")
        sib = 1 - c
        chips = [(1 - x, y), (x, 1 - y), (1 - x, 1 - y)]

        def slot(a, px, py, pc):
            return outs[a].at[4 * px + 2 * py + pc]

        def copy(k, a, src, block, to):
            return pltpu.make_async_remote_copy(
                src_ref=src, dst_ref=slot(a, *block), send_sem=send_sems.at[k * n + a],
                recv_sem=recv_sems.at[k * n + a], device_id=to, device_id_type=pl.DeviceIdType.MESH)

        own = [ins[a] if picks[a] is None else ins[a].at[picks[a]] for a in range(n)]
        local = [pltpu.make_async_copy(own[a], slot(a, x, y, c), local_sems.at[a]) for a in range(n)]
        first = [copy(0, a, own[a], (x, y, c), (x, y, sib)) for a in range(n)]
        first += [copy(1 + j, a, own[a], (x, y, c), (*chip, c)) for j, chip in enumerate(chips[:2]) for a in range(n)]
        for cp in local + first:
            cp.start()

        def pass_on(j, a):
            chip = chips[j]
            copy(1 + j, a, own[a], (*chip, c), (x, y, c)).wait_recv()
            fwd = copy(4 + j, a, slot(a, *chip, c), (*chip, c), (x, y, sib))
            fwd.start()
            return fwd

        passed = [pass_on(j, a) for j in range(2) for a in range(n)]
        south = c == 0
        via = tuple(jnp.where(south, p, q) for p, q in zip(chips[0], chips[1]))
        blk = tuple(jnp.where(south, q, p) for p, q in zip(chips[0], chips[1]))
        relayed = [copy(3, a, slot(a, *blk, c), (*blk, c), (*via, c)) for a in range(n)]
        for cp in relayed:
            cp.start()
        passed += [pass_on(2, a) for a in range(n)]
        first += relayed
        for a in range(n):
            copy(0, a, own[a], (x, y, sib), (x, y, c)).wait_recv()
        for j, chip in enumerate(chips):
            for a in range(n):
                copy(4 + j, a, own[a], (*chip, sib), (x, y, c)).wait_recv()
        for cp in first + passed:
            cp.wait_send()
        for cp in local:
            cp.wait()

    any_spec = pl.BlockSpec(memory_space=pl.ANY)
    return pl.pallas_call(
        body, in_specs=[any_spec] * n, out_specs=(any_spec,) * n, out_shape=_landing_shapes(items),
        scratch_shapes=[pltpu.SemaphoreType.DMA((7 * n,)), pltpu.SemaphoreType.DMA((7 * n,)),
                        pltpu.SemaphoreType.DMA((n,))],
        name=name,
    )(*[a for a, _, _ in items])


_HBM_SPEC = pl.BlockSpec(memory_space=pltpu.HBM)
_SEM_SPEC = pl.BlockSpec(memory_space=pltpu.SEMAPHORE)
_SPLIT_PARAMS = pltpu.CompilerParams(has_side_effects=pltpu.SideEffectType.DATAFLOW_SIDE_EFFECTING)


def _split_start(srcs, lands, plan, sem_shapes, name):
    n_src, n_land = len(srcs), len(lands)
    n_buf = n_src + n_land
    bufs = [pltpu.with_memory_space_constraint(a, pltpu.HBM) for a in list(srcs) + list(lands)]

    def body(*refs):
        local, sends, _ = plan(refs[:n_src], refs[n_src:n_buf], *refs[n_buf:n_buf + 3])
        for cp in local + sends:
            cp.start()
        token = refs[-1]
        token[...] = jnp.zeros_like(token)

    res = pl.pallas_call(
        body, name=name, in_specs=[_HBM_SPEC] * n_buf,
        out_shape=tuple(sem_shapes) + tuple(pltpu.HBM(a.shape, a.dtype) for a in bufs)
        + (jax.ShapeDtypeStruct((8, 128), F32),),
        out_specs=(_SEM_SPEC,) * 3 + (_HBM_SPEC,) * n_buf + (pl.BlockSpec(memory_space=pltpu.VMEM),),
        input_output_aliases={i: 3 + i for i in range(n_buf)}, compiler_params=_SPLIT_PARAMS,
    )(*bufs)
    return dict(sems=res[:3], srcs=res[3:3 + n_src], lands=res[3 + n_src:3 + n_buf], token=res[-1], plan=plan)


def _split_wait(ticket, after, name):
    n_src, n_land = len(ticket["srcs"]), len(ticket["lands"])
    n_buf = n_src + n_land
    plan = ticket["plan"]

    def body(*refs):
        local, sends, recvs = plan(refs[:n_src], refs[n_src:n_buf], *refs[n_buf:n_buf + 3])
        for cp in recvs:
            cp.wait_recv()
        for cp in sends:
            cp.wait_send()
        for cp in local:
            cp.wait()

    bufs = list(ticket["srcs"]) + list(ticket["lands"])
    res = pl.pallas_call(
        body, name=name, in_specs=[_HBM_SPEC] * n_buf + [_SEM_SPEC] * 3 + [pl.BlockSpec(memory_space=pl.ANY)],
        out_shape=tuple(pltpu.HBM(a.shape, a.dtype) for a in bufs), out_specs=(_HBM_SPEC,) * n_buf,
        input_output_aliases={i: i for i in range(n_buf)}, compiler_params=_SPLIT_PARAMS,
    )(*bufs, *ticket["sems"], after)
    return res[n_src:]


def _empty_landings(items):
    return [lax.empty(s.shape, s.dtype) for s in _landing_shapes(items)]


def _exchange_start(items, name):
    modes = [(scatter, pick) for _, scatter, pick in items]
    plan = lambda ins, outs, *sems: _exchange_copies(modes, ins, outs, *sems)
    return _split_start([a for a, _, _ in items], _empty_landings(items), plan, _exchange_sems(len(items)), name)


def _two_level_plans(picks):
    n = len(picks)

    def place():
        x, y, c = lax.axis_index("x"), lax.axis_index("y"), lax.axis_index("c")
        return x, y, c, 1 - c, [(1 - x, y), (x, 1 - y), (1 - x, 1 - y)]

    def copy(outs, send_sems, recv_sems, k, a, src, block, to):
        px, py, pc = block
        return pltpu.make_async_remote_copy(
            src_ref=src, dst_ref=outs[a].at[4 * px + 2 * py + pc], send_sem=send_sems.at[k * n + a],
            recv_sem=recv_sems.at[k * n + a], device_id=to, device_id_type=pl.DeviceIdType.MESH)

    def between_chips(ins, outs, send_sems, recv_sems, local_sems):
        x, y, c, sib, chips = place()
        own = [ins[a] if picks[a] is None else ins[a].at[picks[a]] for a in range(n)]
        mk = lambda *args: copy(outs, send_sems, recv_sems, *args)
        local = [pltpu.make_async_copy(own[a], outs[a].at[4 * x + 2 * y + c], local_sems.at[a]) for a in range(n)]
        sends = [mk(0, a, own[a], (x, y, c), (x, y, sib)) for a in range(n)]
        sends += [mk(1 + j, a, own[a], (x, y, c), (*chip, c)) for j, chip in enumerate(chips) for a in range(n)]
        recvs = [mk(0, a, own[a], (x, y, sib), (x, y, c)) for a in range(n)]
        recvs += [mk(1 + j, a, own[a], (*chip, c), (x, y, c)) for j, chip in enumerate(chips) for a in range(n)]
        return local, sends, recvs

    def within_chip(ins, outs, send_sems, recv_sems, local_sems):
        x, y, c, sib, chips = place()
        mk = lambda *args: copy(outs, send_sems, recv_sems, *args)
        slot = lambda a, px, py, pc: outs[a].at[4 * px + 2 * py + pc]
        sends = [mk(j, a, slot(a, *chip, c), (*chip, c), (x, y, sib)) for j, chip in enumerate(chips)
                 for a in range(n)]
        recvs = [mk(j, a, slot(a, *chip, c), (*chip, sib), (x, y, c)) for j, chip in enumerate(chips)
                 for a in range(n)]
        return [], sends, recvs

    sems = lambda k: [pltpu.SemaphoreType.DMA((k * n,)), pltpu.SemaphoreType.DMA((k * n,)),
                      pltpu.SemaphoreType.DMA((n,))]
    return between_chips, sems(4), within_chip, sems(3)


def _adam_math(g, w, m, v):
    m_new = ADAM_B1 * m + (1.0 - ADAM_B1) * g
    v_new = ADAM_B2 * v + (1.0 - ADAM_B2) * (g * g)
    m_hat = m_new / (1.0 - ADAM_B1 ** ADAM_STEP)
    v_hat = v_new / (1.0 - ADAM_B2 ** ADAM_STEP)
    return -ADAM_LR * (m_hat / (jnp.sqrt(v_hat) + ADAM_EPS) + ADAM_WD * w), m_new, v_new


def _adamw_big(parts, w, m, v, layer, prev, name, tr):
    depth, rows, cols = w.shape

    def body(p_ref, w_ref, m_ref, v_ref, *rest):
        g_out, d_out, m_out, v_out = rest[len(prev):]
        g = p_ref[0].astype(F32)
        for q in range(1, N_DEV):
            g = g + p_ref[q].astype(F32)
        d, m_new, v_new = _adam_math(g, w_ref[...], m_ref[...], v_ref[...])
        g_out[...] = g
        d_out[...] = d
        m_out[...] = m_new
        v_out[...] = v_new

    blk = pl.BlockSpec((None, tr, cols), lambda i: (layer, i, 0))
    shp = jax.ShapeDtypeStruct((depth, rows, cols), F32)
    return pl.pallas_call(
        body, grid=(rows // tr,),
        in_specs=[pl.BlockSpec((N_DEV, tr, cols), lambda i: (0, i, 0)), blk, blk, blk]
        + [pl.BlockSpec(memory_space=pl.ANY)] * len(prev),
        out_specs=(blk,) * 4, out_shape=(shp,) * 4,
        input_output_aliases={4 + j: j for j in range(len(prev))},
        name=name, compiler_params=_params("arbitrary"),
    )(parts, w, m, v, *prev)


_SMALL_TENSORS = (("conv_a_w", "conv_a_w", None), ("cc_dw_w", "cc_dw_w", None), ("cc_pw_w", "cc_pw_w", None),
                  ("sg_w", "sg_w", None), ("pool_w", "pool_w", None), ("sg_b", "sg_b", None)) \
    + tuple((n, "g256", k) for k, n in enumerate(G256_ROWS)) + tuple((n, "g1024", k) for k, n in enumerate(G1024_ROWS))
_SMALL_LANDINGS = ("conv_a_w", "cc_dw_w", "cc_pw_w", "sg_w", "pool_w", "sg_b", "g256", "g1024")
_TAPS_FIRST = ("conv_a_w", "cc_dw_w")


def _adamw_small(landings, wts, mom, var):
    names = [n for n, _, _ in _SMALL_TENSORS]
    n_land = DEPTH * len(_SMALL_LANDINGS)
    n_t = len(names)

    def body(*refs):
        land = [dict(zip(_SMALL_LANDINGS, refs[l * len(_SMALL_LANDINGS):(l + 1) * len(_SMALL_LANDINGS)]))
                for l in range(DEPTH)]
        w_refs = dict(zip(names, refs[n_land:n_land + n_t]))
        m_refs = dict(zip(names, refs[n_land + n_t:n_land + 2 * n_t]))
        v_refs = dict(zip(names, refs[n_land + 2 * n_t:n_land + 3 * n_t]))
        outs = refs[n_land + 3 * n_t:]
        out_refs = {n: outs[4 * k:4 * k + 4] for k, n in enumerate(names)}
        loss_ref = outs[4 * n_t]
        for name, key, row in _SMALL_TENSORS:
            for l in range(DEPTH):
                src = land[l][key]
                if row is not None:
                    part = lambda q: src[q, row:row + 1, :]
                    at = (slice(l, l + 1),)
                elif name == "sg_b":
                    part = lambda q: src[q, 0:N_SUB, :]
                    at = (l,)
                elif name in _TAPS_FIRST:
                    part = lambda q: src[q]
                    at = (slice(None), l)
                else:
                    part = lambda q: src[q]
                    at = (l,)
                g = part(0).astype(F32)
                for q in range(1, N_DEV):
                    g = g + part(q).astype(F32)
                d, m_new, v_new = _adam_math(g, w_refs[name][at], m_refs[name][at], v_refs[name][at])
                for ref, val in zip(out_refs[name], (g, d, m_new, v_new)):
                    ref[at] = val
        src = land[DEPTH - 1]["g1024"]
        loss = src[0, LOSS_ROW:LOSS_ROW + 1, 0:128]
        for q in range(1, N_DEV):
            loss = loss + src[q, LOSS_ROW:LOSS_ROW + 1, 0:128]
        loss_ref[...] = loss

    ins = [landings[l][k] for l in range(DEPTH) for k in _SMALL_LANDINGS] \
        + [src[n] for src in (wts, mom, var) for n in names]
    out_shape = tuple(jax.ShapeDtypeStruct(wts[n].shape, F32) for n in names for _ in range(4)) \
        + (jax.ShapeDtypeStruct((1, 128), F32),)
    res = pl.pallas_call(body, out_shape=out_shape, name="adamw_small", compiler_params=_params())(*ins)
    return {n: res[4 * k:4 * k + 4] for k, n in enumerate(names)}, res[4 * n_t]


_BIG = (("w_in", 64), ("w_out", 32), ("w_kv", 32))
_GRAD_ITEMS_EARLY = ("w_out", "w_kv", "cc_pw_w", "conv_a_w", "cc_dw_w")
_GRAD_ITEMS_REPL = ("g256", "sg_w", "sg_b", "pool_w", "g1024")


def _grad_items(grads, with_w_in):
    items = [(grads[n], True, None) for n in (("w_in",) if with_w_in else ()) + _GRAD_ITEMS_EARLY]
    return items + [(grads[n], False, None) for n in _GRAD_ITEMS_REPL]


def _landed(parts, with_w_in):
    names = (("w_in",) if with_w_in else ()) + _GRAD_ITEMS_EARLY + _GRAD_ITEMS_REPL
    return dict(zip(names, parts))


def _gathered_weights(wt_in8, w_kv8, w_out8, pw8, ca8, dw8):
    return dict(wt_in=wt_in8.reshape(D_IN, D_MODEL), w_kv=w_kv8.reshape(D_MODEL, 2 * D_G),
                w_out=w_out8.reshape(D_MIX, D_MODEL), pw=pw8.reshape(D_G, D_G), ca8=ca8, dw8=dw8)


def kernel(x, mem, w_in, conv_a_w, sg_ln_g, sg_ln_b, sg_w, sg_b, pool_w, pool_scale, cc_dw_w, cc_dw_b, cc_ln_g, cc_ln_b, cc_pw_w, w_kv, w_out, ln_g, ln_b, loss_target, m_w_in, m_conv_a_w, m_sg_ln_g, m_sg_ln_b, m_sg_w, m_sg_b, m_pool_w, m_pool_scale, m_cc_dw_w, m_cc_dw_b, m_cc_ln_g, m_cc_ln_b, m_cc_pw_w, m_w_kv, m_w_out, m_ln_g, m_ln_b, v_w_in, v_conv_a_w, v_sg_ln_g, v_sg_ln_b, v_sg_w, v_sg_b, v_pool_w, v_pool_scale, v_cc_dw_w, v_cc_dw_b, v_cc_ln_g, v_cc_ln_b, v_cc_pw_w, v_w_kv, v_w_out, v_ln_g, v_ln_b):
    names = ("w_in", "conv_a_w", "sg_ln_g", "sg_ln_b", "sg_w", "sg_b", "pool_w", "pool_scale", "cc_dw_w", "cc_dw_b",
             "cc_ln_g", "cc_ln_b", "cc_pw_w", "w_kv", "w_out", "ln_g", "ln_b")
    wts = dict(zip(names, (w_in, conv_a_w, sg_ln_g, sg_ln_b, sg_w, sg_b, pool_w, pool_scale, cc_dw_w, cc_dw_b,
                           cc_ln_g, cc_ln_b, cc_pw_w, w_kv, w_out, ln_g, ln_b)))
    mom = dict(zip(names, (m_w_in, m_conv_a_w, m_sg_ln_g, m_sg_ln_b, m_sg_w, m_sg_b, m_pool_w, m_pool_scale,
                           m_cc_dw_w, m_cc_dw_b, m_cc_ln_g, m_cc_ln_b, m_cc_pw_w, m_w_kv, m_w_out, m_ln_g, m_ln_b)))
    var = dict(zip(names, (v_w_in, v_conv_a_w, v_sg_ln_g, v_sg_ln_b, v_sg_w, v_sg_b, v_pool_w, v_pool_scale,
                           v_cc_dw_w, v_cc_dw_b, v_cc_ln_g, v_cc_ln_b, v_cc_pw_w, v_w_kv, v_w_out, v_ln_g, v_ln_b)))
    repl = wts
    xs, mems, tgt = x[0], mem[0], loss_target[0]
    turned = {"w_in": (0, 2, 1), "conv_a_w": (1, 0, 2), "cc_dw_w": (1, 0, 2)}
    wts, mom, var = [{n: (jnp.transpose(a, turned[n]) if n in turned else a) for n, a in src.items()}
                     for src in (wts, mom, var)]
    wb = {n: wts[n].astype(MM_DTYPE) for n in ("w_in", "w_kv", "w_out", "cc_pw_w")}

    wt8_0, wkv8_0 = _gather_two_level([(wb["w_in"], False, 0), (wb["w_kv"], False, 0)], "gather_weights_0a")
    rest_0 = _exchange_start([(wb["w_out"], False, 0), (wb["cc_pw_w"], False, 0), (wts["conv_a_w"], False, None),
                              (wts["cc_dw_w"], False, None)], "gather_weights_0b_start")
    km_vm0 = _kv_project(mems, wkv8_0.reshape(D_MODEL, 2 * D_G))
    proj0 = _proj_matmul(xs, wt8_0.reshape(D_IN, D_MODEL), (rest_0["token"],))
    wo8_0, pw8_0, ca8, dw8 = _split_wait(rest_0, proj0, "gather_weights_0b_wait")
    gw0 = _gathered_weights(wt8_0, wkv8_0, wo8_0, pw8_0, ca8, dw8)
    items_1 = [(wb[n], False, 1) for n in ("w_in", "w_kv", "w_out", "cc_pw_w")]
    between_chips, sems_a, within_chip, sems_b = _two_level_plans([1] * len(items_1))
    chips_1 = _split_start([a for a, _, _ in items_1], _empty_landings(items_1), between_chips, sems_a,
                           "gather_weights_1a_start")
    x1, saved0 = _layer_fwd(xs, mems, 0, repl, gw0, proj0, km_vm0, (chips_1["token"],))
    core_1 = _split_start([], _split_wait(chips_1, x1, "gather_weights_1a_wait"), within_chip, sems_b,
                          "gather_weights_1b_start")
    gw1 = _gathered_weights(*_split_wait(core_1, core_1["token"], "gather_weights_1b_wait"), ca8, dw8)
    _, saved1 = _layer_fwd(x1, mems, 1, repl, gw1)

    dz1, dproj1, g1 = _layer_bwd_a(None, tgt, mems, 1, repl, gw1, saved1)
    shards = lambda g: g.reshape(N_DEV, W_IN_COLS, D_MODEL)
    up, g_wt_in_1 = _in_bwd(saved1[0], dproj1, gw1["wt_in"], dz1)
    g1["w_in"] = shards(g_wt_in_1)
    grads_1 = _exchange_start(_grad_items(g1, True), "exchange_grads_1_start")
    dz0, dproj0, g0 = _layer_bwd_a(up, None, mems, 0, repl, gw0, saved0, (grads_1["token"],))
    early_0 = _exchange_start(_grad_items(g0, False), "exchange_grads_0a_start")
    g_wt_in_0 = _dw_in_matmul(saved0[0], dproj0, (early_0["token"],))
    late_0 = _exchange_start([(shards(g_wt_in_0), True, None)], "exchange_grads_0b_start")
    grad_x = _dx_matmul(dproj0, gw0["wt_in"], dz0, (late_0["token"],))

    landed = [None, _landed(_split_wait(grads_1, grad_x, "exchange_grads_1_wait"), True)]
    big = {}
    for n, tr in _BIG:
        big[n] = _adamw_big(landed[1][n], wts[n], mom[n], var[n], 1, (), "adamw_" + n + "_1", tr)
    landed[0] = _landed(_split_wait(early_0, big["w_kv"][0], "exchange_grads_0a_wait"), False)
    for n, tr in _BIG[1:]:
        big[n] = _adamw_big(landed[0][n], wts[n], mom[n], var[n], 0, big[n], "adamw_" + n + "_0", tr)
    small, loss = _adamw_small(landed, wts, mom, var)
    (landed[0]["w_in"],) = _split_wait(late_0, loss, "exchange_grads_0b_wait")
    big["w_in"] = _adamw_big(landed[0]["w_in"], wts["w_in"], mom["w_in"], var["w_in"], 0, big["w_in"],
                             "adamw_w_in_0", _BIG[0][1])

    res = {**small, **big}
    res = {n: ([jnp.transpose(a, turned[n]) for a in r] if n in turned else r) for n, r in res.items()}
    return (loss[0, 0], grad_x[None], *[res[n][0] for n in names], *[res[n][1] for n in names],
            *[res[n][2] for n in names], *[res[n][3] for n in names])
```

```python
import math

import jax
import jax.numpy as jnp
from jax import lax
from jax.experimental import pallas as pl
from jax.experimental.pallas import tpu as pltpu

F32 = jnp.float32
MM_DTYPE = jnp.bfloat16
GRAD_DTYPE = jnp.bfloat16

D_MODEL = 1024
DEPTH = 2
D_G = 256
N_GROUPS = 5
D_MIX = N_GROUPS * D_G
N_SUB = 4
HEAD_DIM = D_G // N_SUB
CONV_A = 3
CONV_D = 31
CHUNK = 128
POOL_WINDOWS = (2, 4, 8, 16)
MEM_LEN = 256
LN_EPS = 1e-5
ALPHA = (2.0 * DEPTH) ** 0.25
D_IN = 9 * D_G + D_MIX
ATT_SCALE = 1.0 / math.sqrt(HEAD_DIM)

ADAM_LR = 0.001
ADAM_B1 = 0.9
ADAM_B2 = 0.999
ADAM_EPS = 1e-08
ADAM_WD = 0.01
ADAM_STEP = 10

N_DEV = 8
W_IN_COLS = D_IN // N_DEV
CONV_CH = D_G // N_DEV
HALO = 32
FWD_TILE = 512
BWD_TILE = 256
VMEM_LIMIT = 56 * 1024 * 1024

C_XA, C_BA, C_CA, C_U, C_V, C_XC, C_DA, C_DG, C_Q = range(9)
C_GATE = 9 * D_G

G256_ROWS = ("sg_ln_g", "sg_ln_b", "pool_scale", "cc_dw_b", "cc_ln_g", "cc_ln_b")
G1024_ROWS = ("ln_g", "ln_b")
LOSS_ROW = 2


def _mm(a, b):
    return jnp.dot(a.astype(MM_DTYPE), b.astype(MM_DTYPE), preferred_element_type=F32)


def _mm_nt(a, b):
    return lax.dot_general(a.astype(MM_DTYPE), b.astype(MM_DTYPE), (((1,), (1,)), ((), ())),
                           preferred_element_type=F32)


def _mm_tn(a, b):
    return lax.dot_general(a.astype(MM_DTYPE), b.astype(MM_DTYPE), (((0,), (0,)), ((), ())),
                           preferred_element_type=F32)


def _sigmoid(x):
    return 0.5 * jnp.tanh(0.5 * x) + 0.5


_GELU_C = math.sqrt(2.0 / math.pi)
_GELU_A = 0.044715


def _gelu(x):
    th = jnp.tanh(_GELU_C * (x + _GELU_A * (x * x * x)))
    return 0.5 * x * (1.0 + th), th


def _dgelu(x, th):
    return 0.5 * (1.0 + th) + 0.5 * x * (1.0 - th * th) * (_GELU_C * (1.0 + 3.0 * _GELU_A * (x * x)))


def _ln_fwd(x, g, b):
    mu = jnp.mean(x, axis=-1, keepdims=True)
    xc = x - mu
    var = jnp.mean(xc * xc, axis=-1, keepdims=True)
    rstd = lax.rsqrt(var + LN_EPS)
    xhat = xc * rstd
    return xhat * g + b, xhat, rstd


def _ln_bwd(dy, xhat, rstd, g):
    dxhat = dy * g
    m1 = jnp.mean(dxhat, axis=-1, keepdims=True)
    m2 = jnp.mean(dxhat * xhat, axis=-1, keepdims=True)
    return rstd * (dxhat - m1 - xhat * m2)


def _rowsum(x):
    return jnp.sum(x, axis=0, keepdims=True)


def _col(ref, k):
    return ref[:, k * D_G:(k + 1) * D_G]


def _head_of_lane(shape):
    return jnp.right_shift(lax.broadcasted_iota(jnp.int32, shape, len(shape) - 1), HEAD_DIM.bit_length() - 1)


def _pool_select(lane_grp, s2, s4, s8, s16):
    return jnp.where(lane_grp == 0, s2, jnp.where(lane_grp == 1, s4, jnp.where(lane_grp == 2, s8, s16)))


def _row_view(ref, layer):
    return ref.at[pl.ds(layer, 1)]


def _make_residues(ext_ref, res_ref):
    rows = res_ref.shape[1]
    for r in range(1, 8):
        res_ref[r - 1] = ext_ref[pl.ds(r, rows), :]


def _rows_at(ext_ref, res_ref, off, tile):
    a, r = divmod(off, 8)
    if r == 0:
        return ext_ref[pl.ds(off, tile), :]
    return res_ref[r - 1, pl.ds(8 * a, tile), :]


def _residue_scratch(tile):
    return pltpu.VMEM((7, HALO + tile - 8, D_G), F32)


PROJ_SEGMENTS = ((C_XA * D_G, (C_CA + 1) * D_G), (C_XC * D_G, (C_XC + 1) * D_G), (C_DA * D_G, (C_DG + 1) * D_G),
                 (C_U * D_G, (C_V + 1) * D_G), (C_Q * D_G, (C_Q + 1) * D_G), (C_GATE, D_IN))


def _branch_forward(p_ref, ph_ref, first, row0, km_ref, vm_ref, w, ext_a, ext_c, ext_d, res_c, res_d, tile,
                    kept_ref=None, produce=None):
    r = {}
    produce = produce or (lambda: None)
    produce()
    produce()
    xa, ba, ca = _col(p_ref, C_XA), _col(p_ref, C_BA), _col(p_ref, C_CA)
    g_a = ca * xa
    ext_a[0:HALO] = jnp.where(first, 0.0, _col(ph_ref, C_CA) * _col(ph_ref, C_XA))
    ext_a[HALO:HALO + tile] = g_a
    conv_a = w["conv_a"][0:1, :] * ext_a[pl.ds(HALO - 2, tile), :]
    for k in range(1, CONV_A):
        conv_a = conv_a + w["conv_a"][k:k + 1, :] * ext_a[pl.ds(HALO - 2 + k, tile), :]
    r.update(xa=xa, ba=ba, ca=ca, g_a=g_a, conv_a=conv_a)
    ya = ba * conv_a

    produce()
    lane_grp = _head_of_lane((tile, D_G))
    trow = row0 + lax.broadcasted_iota(jnp.int32, (tile, D_G), 0)
    win = _pool_select(lane_grp, 2, 4, 8, 16)
    inv_cnt = 1.0 / jnp.minimum(trow + 1, win).astype(F32)
    if kept_ref is None:
        xc = _col(p_ref, C_XC)
        ext_c[0:HALO] = jnp.where(first, 0.0, _col(ph_ref, C_XC))
        ext_c[HALO:HALO + tile] = xc
        _make_residues(ext_c, res_c)
        acc = xc
        sums = {}
        for k in range(1, POOL_WINDOWS[-1]):
            acc = acc + _rows_at(ext_c, res_c, HALO - k, tile)
            if k + 1 in POOL_WINDOWS:
                sums[k + 1] = acc
        ypre = _pool_select(lane_grp, sums[2], sums[4], sums[8], sums[16]) * inv_cnt - xc
    else:
        ypre = kept_ref[:, D_G:2 * D_G]
    pool_mm = _mm(ypre, w["pool_wbd"][...])
    yc = pool_mm * w["pool_scale"][...]
    r.update(lane_grp=lane_grp, inv_cnt=inv_cnt, ypre=ypre, pool_mm=pool_mm)

    produce()
    da, dg = _col(p_ref, C_DA), _col(p_ref, C_DG)
    sig_dg = _sigmoid(dg)
    hd = da * sig_dg
    if kept_ref is None:
        ext_d[0:HALO] = jnp.where(first, 0.0, _col(ph_ref, C_DA) * _sigmoid(_col(ph_ref, C_DG)))
        ext_d[HALO:HALO + tile] = hd
        _make_residues(ext_d, res_d)
        conv_d = w["cc_dw_b"][...] + w["cc_dw_w"][0:1, :] * _rows_at(ext_d, res_d, HALO - (CONV_D - 1), tile)
        for j in range(1, CONV_D):
            conv_d = conv_d + w["cc_dw_w"][j:j + 1, :] * _rows_at(ext_d, res_d, HALO - (CONV_D - 1) + j, tile)
    else:
        conv_d = kept_ref[:, 0:D_G]
    r["kept"] = (conv_d, ypre)
    ln_d, xhat_d, rstd_d = _ln_fwd(conv_d, w["cc_ln_g"][...], w["cc_ln_b"][...])
    sig_ln = _sigmoid(ln_d)
    act_d = ln_d * sig_ln
    yd = _mm(act_d, w["cc_pw_w"][...])
    r.update(da=da, sig_dg=sig_dg, hd=hd, ln_d=ln_d, xhat_d=xhat_d, rstd_d=rstd_d, sig_ln=sig_ln, act_d=act_d)
    produce()

    u, v = _col(p_ref, C_U), _col(p_ref, C_V)
    ug, th_u = _gelu(u)
    vg, th_v = _gelu(v)
    vn, xhat_v, rstd_v = _ln_fwd(vg, w["sg_ln_g"][...], w["sg_ln_b"][...])
    tri = (lax.broadcasted_iota(jnp.int32, (CHUNK, CHUNK), 0)
           >= lax.broadcasted_iota(jnp.int32, (CHUNK, CHUNK), 1))
    wm = [jnp.where(tri, w["sg_w"][h], 0.0).astype(MM_DTYPE) for h in range(N_SUB)]
    lo = lax.broadcasted_iota(jnp.int32, (CHUNK, 2 * HEAD_DIM), 1) < HEAD_DIM
    chunks = []
    for c in range(tile // CHUNK):
        halves = []
        for hf in range(2):
            vh = vn[c * CHUNK:(c + 1) * CHUNK, hf * 128:(hf + 1) * 128]
            halves.append(_mm(wm[2 * hf], jnp.where(lo, vh, 0.0)) + _mm(wm[2 * hf + 1], jnp.where(lo, 0.0, vh)))
        chunks.append(jnp.concatenate(halves, axis=1) + w["sg_bias"][...])
    mixed = jnp.concatenate(chunks, axis=0)
    yb = ug * mixed
    r.update(u=u, v=v, ug=ug, th_u=th_u, th_v=th_v, vn=vn, xhat_v=xhat_v, rstd_v=rstd_v, wm=wm, lo=lo,
             mixed=mixed, tri=tri)

    produce()
    q = _col(p_ref, C_Q)
    ye = jnp.zeros((tile, D_G), F32)
    probs = []
    for h in range(N_SUB):
        s = _mm_nt(q, km_ref[h]) * ATT_SCALE
        e = jnp.exp(s - jnp.max(s, axis=-1, keepdims=True))
        p = e * (1.0 / jnp.sum(e, axis=-1, keepdims=True))
        probs.append(p)
        ye = ye + _mm(p, vm_ref[h])
    r.update(q=q, probs=probs)

    gate = p_ref[:, C_GATE:C_GATE + D_MIX]
    sig_gate = _sigmoid(gate)
    r.update(gate=gate, sig_gate=sig_gate, branch_out=(ya, yb, yc, yd, ye))
    return r


_BRANCH_REPL = ("sg_ln_g", "sg_ln_b", "sg_w", "sg_b", "pool_w", "pool_scale", "cc_dw_b", "cc_ln_g", "cc_ln_b")
_BRANCH_W_SCRATCH = (("conv_a", (CONV_A, D_G)), ("cc_dw_w", (CONV_D, D_G)), ("sg_bias", (CHUNK, D_G)),
                     ("pool_wbd", (D_G, D_G)), ("sgb8", (8, CHUNK)))


def _branch_weights(layer, nat, pw_ref, ca_ref, dw_ref, scr, init):
    @pl.when(init)
    def _():
        for p in range(N_DEV):
            scr["conv_a"][:, p * CONV_CH:(p + 1) * CONV_CH] = ca_ref[p, :, layer, :]
            scr["cc_dw_w"][:, p * CONV_CH:(p + 1) * CONV_CH] = dw_ref[p, :, layer, :]
        scr["sgb8"][...] = jnp.zeros((8, CHUNK), F32)
        scr["sgb8"][0:N_SUB] = nat["sg_b"][layer]
        sgb_t = scr["sgb8"][...].T
        head = _head_of_lane((CHUNK, D_G))
        bias = jnp.zeros((CHUNK, D_G), F32)
        for h in range(N_SUB):
            bias = jnp.where(head == h, sgb_t[:, h:h + 1], bias)
        scr["sg_bias"][...] = bias
        scr["pool_wbd"][...] = jnp.zeros((D_G, D_G), F32)
        for gi in range(N_SUB):
            sl = slice(gi * HEAD_DIM, (gi + 1) * HEAD_DIM)
            scr["pool_wbd"][sl, sl] = nat["pool_w"][layer, gi]

    w = {n: _row_view(nat[n], layer) for n in ("sg_ln_g", "sg_ln_b", "pool_scale", "cc_dw_b", "cc_ln_g", "cc_ln_b")}
    w.update(conv_a=scr["conv_a"], cc_dw_w=scr["cc_dw_w"], sg_bias=scr["sg_bias"], pool_wbd=scr["pool_wbd"],
             sg_w=nat["sg_w"].at[layer], cc_pw_w=pw_ref)
    return w


def _full_spec(a):
    nd = a.ndim
    return pl.BlockSpec(a.shape, lambda *_, _nd=nd: (0,) * _nd)


def _tie_specs(ties):
    return [pl.BlockSpec((8, 128), lambda *_: (0, 0)) for _ in ties]


def _params(*sem):
    return pltpu.CompilerParams(dimension_semantics=sem or None, vmem_limit_bytes=VMEM_LIMIT)


def _proj_matmul(x, wt_in, ties=(), tm=512):
    s, k = x.shape

    def body(x_ref, w_ref, *rest):
        o_ref = rest[len(ties)]
        o_ref[...] = _mm_nt(x_ref[...], w_ref[...])

    return pl.pallas_call(
        body, grid=(s // tm,),
        in_specs=[pl.BlockSpec((tm, k), lambda i: (i, 0)), _full_spec(wt_in)] + _tie_specs(ties),
        out_specs=pl.BlockSpec((tm, D_IN), lambda i: (i, 0)),
        out_shape=jax.ShapeDtypeStruct((s, D_IN), F32), name="proj_mm", compiler_params=_params("arbitrary"),
    )(x, wt_in, *ties)


def _kv_project(mem, w_kv):
    def body(mem_ref, w_ref, km_ref, vm_ref):
        kv = _mm(mem_ref[...], w_ref[...])
        k, v = kv[:, :D_G], kv[:, D_G:]
        grp = _head_of_lane((MEM_LEN, D_G))
        for h in range(N_SUB):
            km_ref[h] = jnp.where(grp == h, k, 0.0).astype(km_ref.dtype)
            vm_ref[h] = jnp.where(grp == h, v, 0.0).astype(vm_ref.dtype)

    shp = jax.ShapeDtypeStruct((N_SUB, MEM_LEN, D_G), MM_DTYPE)
    return pl.pallas_call(body, out_shape=(shp, shp), name="kv_project", compiler_params=_params())(mem, w_kv)


def _layer_fwd_fused(x, wt_in, proj, km, vm, layer, repl, pw, ca8, dw8, w_out, want_xn, ties=(), tile=FWD_TILE):
    s = x.shape[0]
    hb = tile // HALO
    nat_arrays = [repl[n] for n in _BRANCH_REPL]
    n_nat, nt = len(nat_arrays), len(ties)
    given = proj is not None

    def body(x_ref, *rest):
        if given:
            p_ref, ph_ref = rest[:2]
            rest = rest[2:]
        else:
            wt_ref = rest[0]
            rest = rest[1:]
        km_ref, vm_ref = rest[:2]
        nat = dict(zip(_BRANCH_REPL, rest[2:2 + n_nat]))
        pw_ref, ca_ref, dw_ref, wo_ref, g_ref, b_ref = rest[2 + n_nat:8 + n_nat]
        rest = rest[8 + n_nat + nt:]
        if not given:
            p_ref, rest = rest[0], rest[1:]
        h_ref, z_ref, cd_ref = rest[:3]
        rest = rest[3:]
        if want_xn:
            xn_ref, rest = rest[0], rest[1:]
        if not given:
            ph_ref, rest = rest[0], rest[1:]
        ext_a, ext_c, ext_d, res_c, res_d = rest[:5]
        scr = dict(zip([n for n, _ in _BRANCH_W_SCRATCH], rest[5:]))
        i = pl.program_id(0)
        xt = x_ref[...]
        produce = None
        if not given:
            @pl.when(i == 0)
            def _():
                ph_ref[...] = jnp.zeros_like(ph_ref)

            xb = xt.astype(MM_DTYPE)
            segments = iter(PROJ_SEGMENTS)

            def produce():
                lo, hi = next(segments)
                p_ref[:, lo:hi] = _mm_nt(xb, wt_ref[lo:hi, :])

        w = _branch_weights(layer, nat, pw_ref, ca_ref, dw_ref, scr, i == 0)
        r = _branch_forward(p_ref, ph_ref, i == 0, i * tile, km_ref, vm_ref, w, ext_a, ext_c, ext_d, res_c, res_d,
                            tile, None, produce)
        if not given:
            ph_ref[...] = p_ref[tile - HALO:tile, :]
        cd_ref[:, 0:D_G], cd_ref[:, D_G:2 * D_G] = r["kept"]
        h = (jnp.concatenate(r["branch_out"], axis=1) * (r["gate"] * r["sig_gate"])).astype(h_ref.dtype)
        h_ref[...] = h
        z = ALPHA * xt + _mm(h, wo_ref[...])
        z_ref[...] = z
        if want_xn:
            xn_ref[...] = _ln_fwd(z, _row_view(g_ref, layer)[...], _row_view(b_ref, layer)[...])[0]

    row = lambda i: (i, 0)
    consts = [km, vm] + nat_arrays + [pw, ca8, dw8, w_out, repl["ln_g"], repl["ln_b"]]
    act = jax.ShapeDtypeStruct((s, D_MODEL), F32)
    act_spec = pl.BlockSpec((tile, D_MODEL), row)
    if given:
        lead = [proj, proj]
        lead_specs = [pl.BlockSpec((tile, D_IN), row),
                      pl.BlockSpec((HALO, D_IN), lambda i: (jnp.maximum(i * hb - 1, 0), 0))]
        out_specs, out_shape, scratch = (), (), []
    else:
        lead = [wt_in]
        lead_specs = [_full_spec(wt_in)]
        out_specs = (pl.BlockSpec((tile, D_IN), row),)
        out_shape = (jax.ShapeDtypeStruct((s, D_IN), F32),)
        scratch = [pltpu.VMEM((HALO, D_IN), F32)]
    res = pl.pallas_call(
        body, grid=(s // tile,),
        in_specs=[act_spec] + lead_specs + [_full_spec(a) for a in consts] + _tie_specs(ties),
        out_specs=out_specs + (pl.BlockSpec((tile, D_MIX), row), act_spec, pl.BlockSpec((tile, 2 * D_G), row))
        + ((act_spec,) if want_xn else ()),
        out_shape=out_shape + (jax.ShapeDtypeStruct((s, D_MIX), MM_DTYPE), act, jax.ShapeDtypeStruct((s, 2 * D_G), F32))
        + ((act,) if want_xn else ()),
        scratch_shapes=scratch + [pltpu.VMEM((HALO + tile, D_G), F32)] * 3
        + [_residue_scratch(tile)] * 2 + [pltpu.VMEM(shape, F32) for _, shape in _BRANCH_W_SCRATCH],
        name="layer_fwd_given_proj" if given else "layer_fwd", compiler_params=_params("arbitrary"),
    )(x, *lead, *consts, *ties)
    res = ((proj,) + tuple(res)) if given else tuple(res)
    return res if want_xn else res + (None,)


_BRANCH_GRADS = (("g256", (8, D_G)), ("sg_w", (N_SUB, CHUNK, CHUNK)), ("sg_b", (8, CHUNK)),
                 ("pool_w", (N_SUB, HEAD_DIM, HEAD_DIM)), ("conv_a_w", (N_DEV, CONV_A, CONV_CH)),
                 ("cc_dw_w", (N_DEV, CONV_D, CONV_CH)), ("cc_pw_w", (D_G, D_G)),
                 ("dk", (N_SUB, MEM_LEN, D_G)), ("dv", (N_SUB, MEM_LEN, D_G)))
_BRANCH_ACC = (("conv_a", (CONV_A, D_G)), ("cc_dw_w", (CONV_D, D_G)), ("pool_wbd", (D_G, D_G)),
               ("sg_bias", (CHUNK, D_G)), ("sg_w", (N_SUB, CHUNK, CHUNK)))
_BRANCH_GRADS_NARROW = ("sg_w", "pool_w")


def _layer_bwd_fused(up, target, z, h, proj, kept, km, vm, layer, repl, w_out, pw, ca8, dw8, ties=(), tile=BWD_TILE):
    s = proj.shape[0]
    nt = s // tile
    hb = tile // HALO
    nat_arrays = [repl[n] for n in _BRANCH_REPL]
    n_nat, n_grads, n_acc, n_ties = len(nat_arrays), len(_BRANCH_GRADS), len(_BRANCH_ACC), len(ties)
    row_of = {n: k for k, n in enumerate(G256_ROWS)}
    from_loss = target is not None

    def body(o_ref, z_ref, h_ref, p_ref, ph_ref, cd_ref, km_ref, vm_ref, *rest):
        nat = dict(zip(_BRANCH_REPL, rest[:n_nat]))
        pw_ref, ca_ref, dw_ref, lng_ref, lnb_ref, wo_ref = rest[n_nat:n_nat + 6]
        rest = rest[n_nat + 6 + n_ties:]
        dz_ref, dp_ref, gw_ref, slab_ref = rest[:4]
        g = dict(zip([n for n, _ in _BRANCH_GRADS], rest[4:4 + n_grads]))
        rest = rest[4 + n_grads:]
        ext_a, rev_a, rev_c, rev_d, res_rc, res_rd, gw_acc, lacc = rest[:8]
        acc = dict(zip([n for n, _ in _BRANCH_ACC], rest[8:8 + n_acc]))
        scr = dict(zip([n for n, _ in _BRANCH_W_SCRATCH], rest[8 + n_acc:]))
        i = pl.program_id(0)
        t = nt - 1 - i

        @pl.when(i == 0)
        def _():
            for ref in list(g.values()) + list(acc.values()) + [rev_a, rev_c, rev_d, gw_acc, slab_ref, lacc]:
                ref[...] = jnp.zeros_like(ref)

        g_ln = _row_view(lng_ref, layer)[...]
        xn, xhat, rstd = _ln_fwd(z_ref[...], g_ln, _row_view(lnb_ref, layer)[...])
        if from_loss:
            err = xn - o_ref[...]
            lacc[...] += _rowsum(err * err)
            dxn = err * (1.0 / D_MODEL)
        else:
            dxn = o_ref[...]
        slab_ref[0:1, :] += _rowsum(dxn * xhat)
        slab_ref[1:2, :] += _rowsum(dxn)
        dz = _ln_bwd(dxn, xhat, rstd, g_ln)
        dz_ref[...] = dz
        dzb = dz.astype(MM_DTYPE)

        w = _branch_weights(layer, nat, pw_ref, ca_ref, dw_ref, scr, i == 0)
        r = _branch_forward(p_ref, ph_ref, t == 0, t * tile, km_ref, vm_ref, w, ext_a, None, None, None, None,
                            tile, cd_ref)

        def put(k, val, width=D_G):
            dp_ref[:, k:k + width] = val.astype(dp_ref.dtype)

        def add_row(name, val):
            k = row_of[name]
            g["g256"][k:k + 1, :] += val

        def push_rev(rev, val):
            head = rev[0:HALO]
            rev[tile:tile + HALO] = head
            rev[0:tile] = val

        gate, sig_gate = r["gate"], r["sig_gate"]

        def branch_grad(group):
            cols = slice(group * D_G, (group + 1) * D_G)
            dh_g = _mm_nt(dzb, wo_ref[cols, :])
            gate_g, sig_g = gate[:, cols], sig_gate[:, cols]
            put(C_GATE + group * D_G, dh_g * r["branch_out"][group] * (sig_g * (1.0 + gate_g * (1.0 - sig_g))))
            return dh_g * (gate_g * sig_g)

        dya = branch_grad(0)
        dyc = branch_grad(2)

        put(C_BA * D_G, dya * r["conv_a"])
        dconv_a = dya * r["ba"]
        push_rev(rev_a, dconv_a)
        dga = jnp.zeros((tile, D_G), F32)
        for k in range(CONV_A):
            ahead = rev_a[pl.ds(CONV_A - 1 - k, tile), :]
            dga = dga + w["conv_a"][k:k + 1, :] * ahead
            acc["conv_a"][k:k + 1, :] += _rowsum(r["g_a"] * ahead)
        put(C_CA * D_G, dga * r["xa"])
        put(C_XA * D_G, dga * r["ca"])

        dyd = branch_grad(3)
        add_row("pool_scale", _rowsum(dyc * r["pool_mm"]))
        dmm = dyc * w["pool_scale"][...]
        acc["pool_wbd"][...] += _mm_tn(r["ypre"], dmm)
        dypre = _mm_nt(dmm, w["pool_wbd"][...])
        dws = dypre * r["inv_cnt"]
        push_rev(rev_c, dws)
        _make_residues(rev_c, res_rc)
        run = dws
        sums = {}
        for k in range(1, POOL_WINDOWS[-1]):
            run = run + _rows_at(rev_c, res_rc, k, tile)
            if k + 1 in POOL_WINDOWS:
                sums[k + 1] = run
        put(C_XC * D_G, _pool_select(r["lane_grp"], sums[2], sums[4], sums[8], sums[16]) - dypre)

        dyb = branch_grad(1)
        gw_acc[...] += _mm_tn(h_ref[...], dzb)
        g["cc_pw_w"][...] += _mm_tn(r["act_d"], dyd)
        dact = _mm_nt(dyd, w["cc_pw_w"][...])
        sig_ln, ln_d = r["sig_ln"], r["ln_d"]
        dln = dact * (sig_ln * (1.0 + ln_d * (1.0 - sig_ln)))
        add_row("cc_ln_g", _rowsum(dln * r["xhat_d"]))
        add_row("cc_ln_b", _rowsum(dln))
        dconv_d = _ln_bwd(dln, r["xhat_d"], r["rstd_d"], w["cc_ln_g"][...])
        add_row("cc_dw_b", _rowsum(dconv_d))
        push_rev(rev_d, dconv_d)
        _make_residues(rev_d, res_rd)
        dhd = jnp.zeros((tile, D_G), F32)
        for j in range(CONV_D):
            ahead = _rows_at(rev_d, res_rd, CONV_D - 1 - j, tile)
            dhd = dhd + w["cc_dw_w"][j:j + 1, :] * ahead
            acc["cc_dw_w"][j:j + 1, :] += _rowsum(r["hd"] * ahead)
        sig_dg = r["sig_dg"]
        put(C_DA * D_G, dhd * sig_dg)
        put(C_DG * D_G, dhd * r["da"] * sig_dg * (1.0 - sig_dg))

        dye = branch_grad(4)
        dug = dyb * r["mixed"]
        dmixed = dyb * r["ug"]
        wm, lo, vn = r["wm"], r["lo"], r["vn"]
        dvn_chunks = []
        for c in range(tile // CHUNK):
            rows = slice(c * CHUNK, (c + 1) * CHUNK)
            acc["sg_bias"][...] += dmixed[rows, :]
            halves = []
            for hf in range(2):
                cols = slice(hf * 128, (hf + 1) * 128)
                dm = dmixed[rows, cols]
                dm_a, dm_b = jnp.where(lo, dm, 0.0), jnp.where(lo, 0.0, dm)
                vh = vn[rows, cols]
                acc["sg_w"][2 * hf] += _mm_nt(dm_a, vh)
                acc["sg_w"][2 * hf + 1] += _mm_nt(dm_b, vh)
                halves.append(_mm_tn(wm[2 * hf], dm_a) + _mm_tn(wm[2 * hf + 1], dm_b))
            dvn_chunks.append(jnp.concatenate(halves, axis=1))
        dvn = jnp.concatenate(dvn_chunks, axis=0)
        add_row("sg_ln_g", _rowsum(dvn * r["xhat_v"]))
        add_row("sg_ln_b", _rowsum(dvn))
        dvg = _ln_bwd(dvn, r["xhat_v"], r["rstd_v"], w["sg_ln_g"][...])
        put(C_V * D_G, dvg * _dgelu(r["v"], r["th_v"]))
        put(C_U * D_G, dug * _dgelu(r["u"], r["th_u"]))

        q = r["q"]
        dq = jnp.zeros((tile, D_G), F32)
        for h in range(N_SUB):
            p = r["probs"][h]
            dp = _mm_nt(dye, vm_ref[h])
            g["dv"][h] += _mm_tn(p, dye)
            ds = p * (dp - jnp.sum(dp * p, axis=-1, keepdims=True)) * ATT_SCALE
            dq = dq + _mm(ds, km_ref[h])
            g["dk"][h] += _mm_tn(ds, q)
        put(C_Q * D_G, dq)

        @pl.when(i == nt - 1)
        def _():
            for h in range(N_SUB):
                g["sg_w"][h] = jnp.where(r["tri"], acc["sg_w"][h], 0.0).astype(g["sg_w"].dtype)
            lane_head = _head_of_lane((CHUNK, D_G))
            col_of = lax.broadcasted_iota(jnp.int32, (CHUNK, 8), 1)
            ba = acc["sg_bias"][...]
            sgb_t = jnp.zeros((CHUNK, 8), F32)
            for h in range(N_SUB):
                col = jnp.sum(jnp.where(lane_head == h, ba, 0.0), axis=-1, keepdims=True)
                sgb_t = jnp.where(col_of == h, col, sgb_t)
            g["sg_b"][...] = sgb_t.T
            wbd = acc["pool_wbd"][...]
            for gi in range(N_SUB):
                sl = slice(gi * HEAD_DIM, (gi + 1) * HEAD_DIM)
                g["pool_w"][gi] = wbd[sl, sl].astype(g["pool_w"].dtype)
            ca, dw = acc["conv_a"][...], acc["cc_dw_w"][...]
            for p in range(N_DEV):
                g["conv_a_w"][p] = ca[:, p * CONV_CH:(p + 1) * CONV_CH]
                g["cc_dw_w"][p] = dw[:, p * CONV_CH:(p + 1) * CONV_CH]
            gw_ref[...] = gw_acc[...].astype(gw_ref.dtype)
            if from_loss:
                total = jnp.sum(lacc[...], axis=-1, keepdims=True) * (0.5 / D_MODEL)
                slab_ref[LOSS_ROW:LOSS_ROW + 1, :] = jnp.broadcast_to(total, (1, D_MODEL))

    rev = lambda i: (nt - 1 - i, 0)
    fixed = lambda i: (0, 0)
    act_spec = pl.BlockSpec((tile, D_MODEL), rev)
    grad_specs = tuple(pl.BlockSpec(shape, lambda i, _nd=len(shape): (0,) * _nd) for _, shape in _BRANCH_GRADS)
    grad_shapes = tuple(jax.ShapeDtypeStruct(shape, GRAD_DTYPE if n in _BRANCH_GRADS_NARROW else F32)
                        for n, shape in _BRANCH_GRADS)
    consts = [km, vm] + nat_arrays + [pw, ca8, dw8, repl["ln_g"], repl["ln_b"], w_out]
    outs = pl.pallas_call(
        body, grid=(nt,),
        in_specs=[act_spec, act_spec, pl.BlockSpec((tile, D_MIX), rev), pl.BlockSpec((tile, D_IN), rev),
                  pl.BlockSpec((HALO, D_IN), lambda i: (jnp.maximum((nt - 1 - i) * hb - 1, 0), 0)),
                  pl.BlockSpec((tile, 2 * D_G), rev)]
        + [_full_spec(a) for a in consts] + _tie_specs(ties),
        out_specs=(act_spec, pl.BlockSpec((tile, D_IN), rev), pl.BlockSpec((D_MIX, D_MODEL), fixed),
                   pl.BlockSpec((8, D_MODEL), fixed)) + grad_specs,
        out_shape=(jax.ShapeDtypeStruct((s, D_MODEL), F32), jax.ShapeDtypeStruct((s, D_IN), MM_DTYPE),
                   jax.ShapeDtypeStruct((D_MIX, D_MODEL), GRAD_DTYPE), jax.ShapeDtypeStruct((8, D_MODEL), F32))
        + grad_shapes,
        scratch_shapes=[pltpu.VMEM((HALO + tile, D_G), F32)] * 4 + [_residue_scratch(tile)] * 2
        + [pltpu.VMEM((D_MIX, D_MODEL), F32), pltpu.VMEM((1, D_MODEL), F32)]
        + [pltpu.VMEM(shape, F32) for _, shape in _BRANCH_ACC + _BRANCH_W_SCRATCH],
        name="layer_bwd_loss" if from_loss else "layer_bwd", compiler_params=_params("arbitrary"),
    )(target if from_loss else up, z, h, proj, proj, kept, *consts, *ties)
    return outs[0], outs[1], outs[2], outs[3], dict(zip([n for n, _ in _BRANCH_GRADS], outs[4:]))


def _dx_matmul(dproj, wt_in, dz, ties=(), tm=512):
    s = dproj.shape[0]

    def body(dp_ref, w_ref, dz_ref, *rest):
        o_ref = rest[len(ties)]
        o_ref[...] = _mm(dp_ref[...], w_ref[...]) + ALPHA * dz_ref[...]

    row = lambda i: (i, 0)
    return pl.pallas_call(
        body, grid=(s // tm,),
        in_specs=[pl.BlockSpec((tm, D_IN), row), _full_spec(wt_in), pl.BlockSpec((tm, D_MODEL), row)]
        + _tie_specs(ties),
        out_specs=pl.BlockSpec((tm, D_MODEL), row),
        out_shape=jax.ShapeDtypeStruct((s, D_MODEL), F32), name="dx_mm", compiler_params=_params("arbitrary"),
    )(dproj, wt_in, dz, *ties)


def _dw_in_matmul(x, dproj, ties=(), tk=512):
    s = x.shape[0]
    nk = s // tk
    blk = 2 * W_IN_COLS

    def body(x_ref, dp_ref, *rest):
        o_ref, acc = rest[len(ties):]
        k = pl.program_id(0)

        @pl.when(k == 0)
        def _():
            acc[...] = jnp.zeros_like(acc)

        xb = x_ref[...].astype(MM_DTYPE)
        for j in range(D_IN // blk):
            acc[j * blk:(j + 1) * blk, :] += _mm_tn(dp_ref[:, j * blk:(j + 1) * blk], xb)

        @pl.when(k == nk - 1)
        def _():
            o_ref[...] = acc[...].astype(o_ref.dtype)

    return pl.pallas_call(
        body, grid=(nk,),
        in_specs=[pl.BlockSpec((tk, D_MODEL), lambda k: (k, 0)), pl.BlockSpec((tk, D_IN), lambda k: (k, 0))]
        + _tie_specs(ties),
        out_specs=pl.BlockSpec((D_IN, D_MODEL), lambda k: (0, 0)),
        out_shape=jax.ShapeDtypeStruct((D_IN, D_MODEL), GRAD_DTYPE),
        scratch_shapes=[pltpu.VMEM((D_IN, D_MODEL), F32)], name="dw_in_mm", compiler_params=_params("arbitrary"),
    )(x, dproj, *ties)


def _in_bwd(x, dproj, wt_in, dz, tm=512):
    s = x.shape[0]
    n_steps = s // tm

    assert wt_in.dtype == GRAD_DTYPE
    blk = 2 * W_IN_COLS

    def body(x_ref, dp_ref, w_hbm, dz_ref, o_ref, gw_hbm, w_vmem, acc, sem):
        i = pl.program_id(0)

        @pl.when(i == 0)
        def _():
            fetch = pltpu.make_async_copy(w_hbm, w_vmem, sem)
            fetch.start()
            acc[...] = jnp.zeros_like(acc)
            fetch.wait()

        o_ref[...] = _mm(dp_ref[...], w_vmem[...]) + ALPHA * dz_ref[...]
        xb = x_ref[...].astype(MM_DTYPE)
        for j in range(D_IN // blk):
            acc[j * blk:(j + 1) * blk, :] += _mm_tn(dp_ref[:, j * blk:(j + 1) * blk], xb)

        @pl.when(i == n_steps - 1)
        def _():
            w_vmem[...] = acc[...].astype(w_vmem.dtype)
            emit = pltpu.make_async_copy(w_vmem, gw_hbm, sem)
            emit.start()
            emit.wait()

    row = lambda i: (i, 0)
    any_spec = pl.BlockSpec(memory_space=pl.ANY)
    return pl.pallas_call(
        body, grid=(n_steps,),
        in_specs=[pl.BlockSpec((tm, D_MODEL), row), pl.BlockSpec((tm, D_IN), row), any_spec,
                  pl.BlockSpec((tm, D_MODEL), row)],
        out_specs=(pl.BlockSpec((tm, D_MODEL), row), any_spec),
        out_shape=(jax.ShapeDtypeStruct((s, D_MODEL), F32), jax.ShapeDtypeStruct((D_IN, D_MODEL), GRAD_DTYPE)),
        scratch_shapes=[pltpu.VMEM((D_IN, D_MODEL), wt_in.dtype), pltpu.VMEM((D_IN, D_MODEL), F32),
                        pltpu.SemaphoreType.DMA],
        name="in_bwd", compiler_params=_params("arbitrary"),
    )(x, dproj, wt_in, dz)


def _kv_bwd(mem, dk, dv):
    def body(mem_ref, dk_ref, dv_ref, o_ref):
        grp = _head_of_lane((MEM_LEN, D_G))
        dk_sum = jnp.zeros((MEM_LEN, D_G), F32)
        dv_sum = jnp.zeros((MEM_LEN, D_G), F32)
        for h in range(N_SUB):
            dk_sum = dk_sum + jnp.where(grp == h, dk_ref[h], 0.0)
            dv_sum = dv_sum + jnp.where(grp == h, dv_ref[h], 0.0)
        o_ref[...] = _mm_tn(mem_ref[...], jnp.concatenate([dk_sum, dv_sum], axis=1)).astype(o_ref.dtype)

    return pl.pallas_call(body, out_shape=jax.ShapeDtypeStruct((D_MODEL, 2 * D_G), GRAD_DTYPE), name="kv_bwd",
                          compiler_params=_params())(mem, dk, dv)


def _layer_fwd(x, mem, layer, repl, gw, proj=None, km_vm=None, ties=()):
    km, vm = _kv_project(mem, gw["w_kv"]) if km_vm is None else km_vm
    proj, h, z, kept, xn = _layer_fwd_fused(x, gw["wt_in"] if proj is None else None, proj, km, vm, layer, repl,
                                            gw["pw"], gw["ca8"], gw["dw8"], gw["w_out"], layer < DEPTH - 1, ties)
    return xn, (x, proj, h, z, km, vm, kept)


def _layer_bwd_a(up, target, mem, layer, repl, gw, saved, ties=()):
    x_in, proj, h, z, km, vm, kept = saved
    dz, dproj, g_w_out, g1024, bg = _layer_bwd_fused(up, target, z, h, proj, kept, km, vm, layer, repl, gw["w_out"],
                                                     gw["pw"], gw["ca8"], gw["dw8"], ties)
    grads = {n: bg[n] for n in ("g256", "sg_w", "sg_b", "pool_w", "conv_a_w", "cc_dw_w")}
    grads.update(w_out=g_w_out.reshape(N_DEV, D_MIX // N_DEV, D_MODEL), g1024=g1024,
                 w_kv=_kv_bwd(mem, bg["dk"], bg["dv"]).reshape(N_DEV, D_MODEL // N_DEV, 2 * D_G),
                 cc_pw_w=bg["cc_pw_w"].reshape(N_DEV, CONV_CH, D_G))
    return dz, dproj, grads


def _landing_shapes(items):
    out = []
    for a, scatter, pick in items:
        shape = a.shape if scatter else (N_DEV,) + (a.shape if pick is None else a.shape[1:])
        out.append(jax.ShapeDtypeStruct(shape, a.dtype))
    return tuple(out)


def _exchange_sems(n):
    return [pltpu.SemaphoreType.DMA(((N_DEV - 1) * n,)), pltpu.SemaphoreType.DMA(((N_DEV - 1) * n,)),
            pltpu.SemaphoreType.DMA((n,))]


def _exchange_copies(modes, ins, outs, send_sems, recv_sems, local_sems):
    n = len(ins)
    x, y, c = lax.axis_index("x"), lax.axis_index("y"), lax.axis_index("c")
    me = 4 * x + 2 * y + c

    def src_of(a, dest):
        scatter, pick = modes[a]
        if scatter:
            return ins[a].at[dest]
        return ins[a] if pick is None else ins[a].at[pick]

    local = [pltpu.make_async_copy(src_of(a, me), outs[a].at[me], local_sems.at[a]) for a in range(n)]
    sends, recvs = [], []
    for k in range(1, N_DEV):
        px = 1 - x if k & 4 else x
        py = 1 - y if k & 2 else y
        pc = 1 - c if k & 1 else c
        peer = 4 * px + 2 * py + pc
        for a in range(n):
            sems = dict(send_sem=send_sems.at[(k - 1) * n + a], recv_sem=recv_sems.at[(k - 1) * n + a],
                        device_id=(px, py, pc), device_id_type=pl.DeviceIdType.MESH)
            sends.append(pltpu.make_async_remote_copy(src_ref=src_of(a, peer), dst_ref=outs[a].at[me], **sems))
            recvs.append(pltpu.make_async_remote_copy(src_ref=src_of(a, peer), dst_ref=outs[a].at[peer], **sems))
    return local, sends, recvs


def _gather_two_level(items, name):
    n = len(items)
    assert not any(scatter for _, scatter, _ in items)
    picks = [pick for _, _, pick in items]

    def body(*refs):
        ins, outs = refs[:n], refs[n:2 * n]
        send_sems, recv_sems, local_sems = refs[2 * n:]
        x, y, c = lax.axis_index("x"), lax.axis_index("y"), lax.axis_index("c")
        sib = 1 - c
        chips = [(1 - x, y), (x, 1 - y), (1 - x, 1 - y)]

        def slot(a, px, py, pc):
            return outs[a].at[4 * px + 2 * py + pc]

        def copy(k, a, src, block, to):
            return pltpu.make_async_remote_copy(
                src_ref=src, dst_ref=slot(a, *block), send_sem=send_sems.at[k * n + a],
                recv_sem=recv_sems.at[k * n + a], device_id=to, device_id_type=pl.DeviceIdType.MESH)

        own = [ins[a] if picks[a] is None else ins[a].at[picks[a]] for a in range(n)]
        local = [pltpu.make_async_copy(own[a], slot(a, x, y, c), local_sems.at[a]) for a in range(n)]
        first = [copy(0, a, own[a], (x, y, c), (x, y, sib)) for a in range(n)]
        first += [copy(1 + j, a, own[a], (x, y, c), (*chip, c)) for j, chip in enumerate(chips[:2]) for a in range(n)]
        for cp in local + first:
            cp.start()

        def pass_on(j, a):
            chip = chips[j]
            copy(1 + j, a, own[a], (*chip, c), (x, y, c)).wait_recv()
            fwd = copy(4 + j, a, slot(a, *chip, c), (*chip, c), (x, y, sib))
            fwd.start()
            return fwd

        passed = [pass_on(j, a) for j in range(2) for a in range(n)]
        south = c == 0
        via = tuple(jnp.where(south, p, q) for p, q in zip(chips[0], chips[1]))
        blk = tuple(jnp.where(south, q, p) for p, q in zip(chips[0], chips[1]))
        relayed = [copy(3, a, slot(a, *blk, c), (*blk, c), (*via, c)) for a in range(n)]
        for cp in relayed:
            cp.start()
        passed += [pass_on(2, a) for a in range(n)]
        first += relayed
        for a in range(n):
            copy(0, a, own[a], (x, y, sib), (x, y, c)).wait_recv()
        for j, chip in enumerate(chips):
            for a in range(n):
                copy(4 + j, a, own[a], (*chip, sib), (x, y, c)).wait_recv()
        for cp in first + passed:
            cp.wait_send()
        for cp in local:
            cp.wait()

    any_spec = pl.BlockSpec(memory_space=pl.ANY)
    return pl.pallas_call(
        body, in_specs=[any_spec] * n, out_specs=(any_spec,) * n, out_shape=_landing_shapes(items),
        scratch_shapes=[pltpu.SemaphoreType.DMA((7 * n,)), pltpu.SemaphoreType.DMA((7 * n,)),
                        pltpu.SemaphoreType.DMA((n,))],
        name=name,
    )(*[a for a, _, _ in items])


_HBM_SPEC = pl.BlockSpec(memory_space=pltpu.HBM)
_SEM_SPEC = pl.BlockSpec(memory_space=pltpu.SEMAPHORE)
_SPLIT_PARAMS = pltpu.CompilerParams(has_side_effects=pltpu.SideEffectType.DATAFLOW_SIDE_EFFECTING)


def _split_start(srcs, lands, plan, sem_shapes, name):
    n_src, n_land = len(srcs), len(lands)
    n_buf = n_src + n_land
    bufs = [pltpu.with_memory_space_constraint(a, pltpu.HBM) for a in list(srcs) + list(lands)]

    def body(*refs):
        local, sends, _ = plan(refs[:n_src], refs[n_src:n_buf], *refs[n_buf:n_buf + 3])
        for cp in local + sends:
            cp.start()
        token = refs[-1]
        token[...] = jnp.zeros_like(token)

    res = pl.pallas_call(
        body, name=name, in_specs=[_HBM_SPEC] * n_buf,
        out_shape=tuple(sem_shapes) + tuple(pltpu.HBM(a.shape, a.dtype) for a in bufs)
        + (jax.ShapeDtypeStruct((8, 128), F32),),
        out_specs=(_SEM_SPEC,) * 3 + (_HBM_SPEC,) * n_buf + (pl.BlockSpec(memory_space=pltpu.VMEM),),
        input_output_aliases={i: 3 + i for i in range(n_buf)}, compiler_params=_SPLIT_PARAMS,
    )(*bufs)
    return dict(sems=res[:3], srcs=res[3:3 + n_src], lands=res[3 + n_src:3 + n_buf], token=res[-1], plan=plan)


def _split_wait(ticket, after, name):
    n_src, n_land = len(ticket["srcs"]), len(ticket["lands"])
    n_buf = n_src + n_land
    plan = ticket["plan"]

    def body(*refs):
        local, sends, recvs = plan(refs[:n_src], refs[n_src:n_buf], *refs[n_buf:n_buf + 3])
        for cp in recvs:
            cp.wait_recv()
        for cp in sends:
            cp.wait_send()
        for cp in local:
            cp.wait()

    bufs = list(ticket["srcs"]) + list(ticket["lands"])
    res = pl.pallas_call(
        body, name=name, in_specs=[_HBM_SPEC] * n_buf + [_SEM_SPEC] * 3 + [pl.BlockSpec(memory_space=pl.ANY)],
        out_shape=tuple(pltpu.HBM(a.shape, a.dtype) for a in bufs), out_specs=(_HBM_SPEC,) * n_buf,
        input_output_aliases={i: i for i in range(n_buf)}, compiler_params=_SPLIT_PARAMS,
    )(*bufs, *ticket["sems"], after)
    return res[n_src:]


def _empty_landings(items):
    return [lax.empty(s.shape, s.dtype) for s in _landing_shapes(items)]


def _exchange_start(items, name):
    modes = [(scatter, pick) for _, scatter, pick in items]
    plan = lambda ins, outs, *sems: _exchange_copies(modes, ins, outs, *sems)
    return _split_start([a for a, _, _ in items], _empty_landings(items), plan, _exchange_sems(len(items)), name)


def _two_level_plans(picks):
    n = len(picks)

    def place():
        x, y, c = lax.axis_index("x"), lax.axis_index("y"), lax.axis_index("c")
        return x, y, c, 1 - c, [(1 - x, y), (x, 1 - y), (1 - x, 1 - y)]

    def copy(outs, send_sems, recv_sems, k, a, src, block, to):
        px, py, pc = block
        return pltpu.make_async_remote_copy(
            src_ref=src, dst_ref=outs[a].at[4 * px + 2 * py + pc], send_sem=send_sems.at[k * n + a],
            recv_sem=recv_sems.at[k * n + a], device_id=to, device_id_type=pl.DeviceIdType.MESH)

    def between_chips(ins, outs, send_sems, recv_sems, local_sems):
        x, y, c, sib, chips = place()
        own = [ins[a] if picks[a] is None else ins[a].at[picks[a]] for a in range(n)]
        mk = lambda *args: copy(outs, send_sems, recv_sems, *args)
        local = [pltpu.make_async_copy(own[a], outs[a].at[4 * x + 2 * y + c], local_sems.at[a]) for a in range(n)]
        sends = [mk(0, a, own[a], (x, y, c), (x, y, sib)) for a in range(n)]
        sends += [mk(1 + j, a, own[a], (x, y, c), (*chip, c)) for j, chip in enumerate(chips) for a in range(n)]
        recvs = [mk(0, a, own[a], (x, y, sib), (x, y, c)) for a in range(n)]
        recvs += [mk(1 + j, a, own[a], (*chip, c), (x, y, c)) for j, chip in enumerate(chips) for a in range(n)]
        return local, sends, recvs

    def within_chip(ins, outs, send_sems, recv_sems, local_sems):
        x, y, c, sib, chips = place()
        mk = lambda *args: copy(outs, send_sems, recv_sems, *args)
        slot = lambda a, px, py, pc: outs[a].at[4 * px + 2 * py + pc]
        sends = [mk(j, a, slot(a, *chip, c), (*chip, c), (x, y, sib)) for j, chip in enumerate(chips)
                 for a in range(n)]
        recvs = [mk(j, a, slot(a, *chip, c), (*chip, sib), (x, y, c)) for j, chip in enumerate(chips)
                 for a in range(n)]
        return [], sends, recvs

    sems = lambda k: [pltpu.SemaphoreType.DMA((k * n,)), pltpu.SemaphoreType.DMA((k * n,)),
                      pltpu.SemaphoreType.DMA((n,))]
    return between_chips, sems(4), within_chip, sems(3)


def _adam_math(g, w, m, v):
    m_new = ADAM_B1 * m + (1.0 - ADAM_B1) * g
    v_new = ADAM_B2 * v + (1.0 - ADAM_B2) * (g * g)
    m_hat = m_new / (1.0 - ADAM_B1 ** ADAM_STEP)
    v_hat = v_new / (1.0 - ADAM_B2 ** ADAM_STEP)
    return -ADAM_LR * (m_hat / (jnp.sqrt(v_hat) + ADAM_EPS) + ADAM_WD * w), m_new, v_new


def _adamw_big(parts, w, m, v, layer, prev, name, tr):
    depth, rows, cols = w.shape

    def body(p_ref, w_ref, m_ref, v_ref, *rest):
        g_out, d_out, m_out, v_out = rest[len(prev):]
        g = p_ref[0].astype(F32)
        for q in range(1, N_DEV):
            g = g + p_ref[q].astype(F32)
        d, m_new, v_new = _adam_math(g, w_ref[...], m_ref[...], v_ref[...])
        g_out[...] = g
        d_out[...] = d
        m_out[...] = m_new
        v_out[...] = v_new

    blk = pl.BlockSpec((None, tr, cols), lambda i: (layer, i, 0))
    shp = jax.ShapeDtypeStruct((depth, rows, cols), F32)
    return pl.pallas_call(
        body, grid=(rows // tr,),
        in_specs=[pl.BlockSpec((N_DEV, tr, cols), lambda i: (0, i, 0)), blk, blk, blk]
        + [pl.BlockSpec(memory_space=pl.ANY)] * len(prev),
        out_specs=(blk,) * 4, out_shape=(shp,) * 4,
        input_output_aliases={4 + j: j for j in range(len(prev))},
        name=name, compiler_params=_params("arbitrary"),
    )(parts, w, m, v, *prev)


_SMALL_TENSORS = (("conv_a_w", "conv_a_w", None), ("cc_dw_w", "cc_dw_w", None), ("cc_pw_w", "cc_pw_w", None),
                  ("sg_w", "sg_w", None), ("pool_w", "pool_w", None), ("sg_b", "sg_b", None)) \
    + tuple((n, "g256", k) for k, n in enumerate(G256_ROWS)) + tuple((n, "g1024", k) for k, n in enumerate(G1024_ROWS))
_SMALL_LANDINGS = ("conv_a_w", "cc_dw_w", "cc_pw_w", "sg_w", "pool_w", "sg_b", "g256", "g1024")
_TAPS_FIRST = ("conv_a_w", "cc_dw_w")


def _adamw_small(landings, wts, mom, var):
    names = [n for n, _, _ in _SMALL_TENSORS]
    n_land = DEPTH * len(_SMALL_LANDINGS)
    n_t = len(names)

    def body(*refs):
        land = [dict(zip(_SMALL_LANDINGS, refs[l * len(_SMALL_LANDINGS):(l + 1) * len(_SMALL_LANDINGS)]))
                for l in range(DEPTH)]
        w_refs = dict(zip(names, refs[n_land:n_land + n_t]))
        m_refs = dict(zip(names, refs[n_land + n_t:n_land + 2 * n_t]))
        v_refs = dict(zip(names, refs[n_land + 2 * n_t:n_land + 3 * n_t]))
        outs = refs[n_land + 3 * n_t:]
        out_refs = {n: outs[4 * k:4 * k + 4] for k, n in enumerate(names)}
        loss_ref = outs[4 * n_t]
        for name, key, row in _SMALL_TENSORS:
            for l in range(DEPTH):
                src = land[l][key]
                if row is not None:
                    part = lambda q: src[q, row:row + 1, :]
                    at = (slice(l, l + 1),)
                elif name == "sg_b":
                    part = lambda q: src[q, 0:N_SUB, :]
                    at = (l,)
                elif name in _TAPS_FIRST:
                    part = lambda q: src[q]
                    at = (slice(None), l)
                else:
                    part = lambda q: src[q]
                    at = (l,)
                g = part(0).astype(F32)
                for q in range(1, N_DEV):
                    g = g + part(q).astype(F32)
                d, m_new, v_new = _adam_math(g, w_refs[name][at], m_refs[name][at], v_refs[name][at])
                for ref, val in zip(out_refs[name], (g, d, m_new, v_new)):
                    ref[at] = val
        src = land[DEPTH - 1]["g1024"]
        loss = src[0, LOSS_ROW:LOSS_ROW + 1, 0:128]
        for q in range(1, N_DEV):
            loss = loss + src[q, LOSS_ROW:LOSS_ROW + 1, 0:128]
        loss_ref[...] = loss

    ins = [landings[l][k] for l in range(DEPTH) for k in _SMALL_LANDINGS] \
        + [src[n] for src in (wts, mom, var) for n in names]
    out_shape = tuple(jax.ShapeDtypeStruct(wts[n].shape, F32) for n in names for _ in range(4)) \
        + (jax.ShapeDtypeStruct((1, 128), F32),)
    res = pl.pallas_call(body, out_shape=out_shape, name="adamw_small", compiler_params=_params())(*ins)
    return {n: res[4 * k:4 * k + 4] for k, n in enumerate(names)}, res[4 * n_t]


_BIG = (("w_in", 64), ("w_out", 32), ("w_kv", 32))
_GRAD_ITEMS_EARLY = ("w_out", "w_kv", "cc_pw_w", "conv_a_w", "cc_dw_w")
_GRAD_ITEMS_REPL = ("g256", "sg_w", "sg_b", "pool_w", "g1024")


def _grad_items(grads, with_w_in):
    items = [(grads[n], True, None) for n in (("w_in",) if with_w_in else ()) + _GRAD_ITEMS_EARLY]
    return items + [(grads[n], False, None) for n in _GRAD_ITEMS_REPL]


def _landed(parts, with_w_in):
    names = (("w_in",) if with_w_in else ()) + _GRAD_ITEMS_EARLY + _GRAD_ITEMS_REPL
    return dict(zip(names, parts))


def _gathered_weights(wt_in8, w_kv8, w_out8, pw8, ca8, dw8):
    return dict(wt_in=wt_in8.reshape(D_IN, D_MODEL), w_kv=w_kv8.reshape(D_MODEL, 2 * D_G),
                w_out=w_out8.reshape(D_MIX, D_MODEL), pw=pw8.reshape(D_G, D_G), ca8=ca8, dw8=dw8)


def kernel(x, mem, w_in, conv_a_w, sg_ln_g, sg_ln_b, sg_w, sg_b, pool_w, pool_scale, cc_dw_w, cc_dw_b, cc_ln_g, cc_ln_b, cc_pw_w, w_kv, w_out, ln_g, ln_b, loss_target, m_w_in, m_conv_a_w, m_sg_ln_g, m_sg_ln_b, m_sg_w, m_sg_b, m_pool_w, m_pool_scale, m_cc_dw_w, m_cc_dw_b, m_cc_ln_g, m_cc_ln_b, m_cc_pw_w, m_w_kv, m_w_out, m_ln_g, m_ln_b, v_w_in, v_conv_a_w, v_sg_ln_g, v_sg_ln_b, v_sg_w, v_sg_b, v_pool_w, v_pool_scale, v_cc_dw_w, v_cc_dw_b, v_cc_ln_g, v_cc_ln_b, v_cc_pw_w, v_w_kv, v_w_out, v_ln_g, v_ln_b):
    names = ("w_in", "conv_a_w", "sg_ln_g", "sg_ln_b", "sg_w", "sg_b", "pool_w", "pool_scale", "cc_dw_w", "cc_dw_b",
             "cc_ln_g", "cc_ln_b", "cc_pw_w", "w_kv", "w_out", "ln_g", "ln_b")
    wts = dict(zip(names, (w_in, conv_a_w, sg_ln_g, sg_ln_b, sg_w, sg_b, pool_w, pool_scale, cc_dw_w, cc_dw_b,
                           cc_ln_g, cc_ln_b, cc_pw_w, w_kv, w_out, ln_g, ln_b)))
    mom = dict(zip(names, (m_w_in, m_conv_a_w, m_sg_ln_g, m_sg_ln_b, m_sg_w, m_sg_b, m_pool_w, m_pool_scale,
                           m_cc_dw_w, m_cc_dw_b, m_cc_ln_g, m_cc_ln_b, m_cc_pw_w, m_w_kv, m_w_out, m_ln_g, m_ln_b)))
    var = dict(zip(names, (v_w_in, v_conv_a_w, v_sg_ln_g, v_sg_ln_b, v_sg_w, v_sg_b, v_pool_w, v_pool_scale,
                           v_cc_dw_w, v_cc_dw_b, v_cc_ln_g, v_cc_ln_b, v_cc_pw_w, v_w_kv, v_w_out, v_ln_g, v_ln_b)))
    repl = wts
    xs, mems, tgt = x[0], mem[0], loss_target[0]
    turned = {"w_in": (0, 2, 1), "conv_a_w": (1, 0, 2), "cc_dw_w": (1, 0, 2)}
    wts, mom, var = [{n: (jnp.transpose(a, turned[n]) if n in turned else a) for n, a in src.items()}
                     for src in (wts, mom, var)]
    wb = {n: wts[n].astype(MM_DTYPE) for n in ("w_in", "w_kv", "w_out", "cc_pw_w")}

    wt8_0, wkv8_0, wo8_0, pw8_0, ca8, dw8 = _gather_two_level(
        [(wb["w_in"], False, 0), (wb["w_kv"], False, 0), (wb["w_out"], False, 0), (wb["cc_pw_w"], False, 0),
         (wts["conv_a_w"], False, None), (wts["cc_dw_w"], False, None)], "gather_weights_0")
    gw0 = _gathered_weights(wt8_0, wkv8_0, wo8_0, pw8_0, ca8, dw8)
    items_1 = [(wb[n], False, 1) for n in ("w_in", "w_kv", "w_out", "cc_pw_w")]
    between_chips, sems_a, within_chip, sems_b = _two_level_plans([1] * len(items_1))
    chips_1 = _split_start([a for a, _, _ in items_1], _empty_landings(items_1), between_chips, sems_a,
                           "gather_weights_1a_start")
    x1, saved0 = _layer_fwd(xs, mems, 0, repl, gw0, ties=(chips_1["token"],))
    core_1 = _split_start([], _split_wait(chips_1, x1, "gather_weights_1a_wait"), within_chip, sems_b,
                          "gather_weights_1b_start")
    gw1 = _gathered_weights(*_split_wait(core_1, core_1["token"], "gather_weights_1b_wait"), ca8, dw8)
    _, saved1 = _layer_fwd(x1, mems, 1, repl, gw1)

    dz1, dproj1, g1 = _layer_bwd_a(None, tgt, mems, 1, repl, gw1, saved1)
    shards = lambda g: g.reshape(N_DEV, W_IN_COLS, D_MODEL)
    up, g_wt_in_1 = _in_bwd(saved1[0], dproj1, gw1["wt_in"], dz1)
    g1["w_in"] = shards(g_wt_in_1)
    grads_1 = _exchange_start(_grad_items(g1, True), "exchange_grads_1_start")
    dz0, dproj0, g0 = _layer_bwd_a(up, None, mems, 0, repl, gw0, saved0, (grads_1["token"],))
    early_0 = _exchange_start(_grad_items(g0, False), "exchange_grads_0a_start")
    g_wt_in_0 = _dw_in_matmul(saved0[0], dproj0, (early_0["token"],))
    late_0 = _exchange_start([(shards(g_wt_in_0), True, None)], "exchange_grads_0b_start")
    grad_x = _dx_matmul(dproj0, gw0["wt_in"], dz0, (late_0["token"],))

    landed = [None, _landed(_split_wait(grads_1, grad_x, "exchange_grads_1_wait"), True)]
    big = {}
    for n, tr in _BIG:
        big[n] = _adamw_big(landed[1][n], wts[n], mom[n], var[n], 1, (), "adamw_" + n + "_1", tr)
    landed[0] = _landed(_split_wait(early_0, big["w_kv"][0], "exchange_grads_0a_wait"), False)
    for n, tr in _BIG[1:]:
        big[n] = _adamw_big(landed[0][n], wts[n], mom[n], var[n], 0, big[n], "adamw_" + n + "_0", tr)
    small, loss = _adamw_small(landed, wts, mom, var)
    (landed[0]["w_in"],) = _split_wait(late_0, loss, "exchange_grads_0b_wait")
    big["w_in"] = _adamw_big(landed[0]["w_in"], wts["w_in"], mom["w_in"], var["w_in"], 0, big["w_in"],
                             "adamw_w_in_0", _BIG[0][1])

    res = {**small, **big}
    res = {n: ([jnp.transpose(a, turned[n]) for a in r] if n in turned else r) for n, r in res.items()}
    return (loss[0, 0], grad_x[None], *[res[n][0] for n in names], *[res[n][1] for n in names],
            *[res[n][2] for n in names], *[res[n][3] for n in names])
```

```python
import math

import jax
import jax.numpy as jnp
from jax import lax
from jax.experimental import pallas as pl
from jax.experimental.pallas import tpu as pltpu

F32 = jnp.float32
MM_DTYPE = jnp.bfloat16
GRAD_DTYPE = jnp.bfloat16

D_MODEL = 1024
DEPTH = 2
D_G = 256
N_GROUPS = 5
D_MIX = N_GROUPS * D_G
N_SUB = 4
HEAD_DIM = D_G // N_SUB
CONV_A = 3
CONV_D = 31
CHUNK = 128
POOL_WINDOWS = (2, 4, 8, 16)
MEM_LEN = 256
LN_EPS = 1e-5
ALPHA = (2.0 * DEPTH) ** 0.25
D_IN = 9 * D_G + D_MIX
ATT_SCALE = 1.0 / math.sqrt(HEAD_DIM)

ADAM_LR = 0.001
ADAM_B1 = 0.9
ADAM_B2 = 0.999
ADAM_EPS = 1e-08
ADAM_WD = 0.01
ADAM_STEP = 10

N_DEV = 8
W_IN_COLS = D_IN // N_DEV
CONV_CH = D_G // N_DEV
HALO = 32
FWD_TILE = 512
BWD_TILE = 256
VMEM_LIMIT = 56 * 1024 * 1024

C_XA, C_BA, C_CA, C_U, C_V, C_XC, C_DA, C_DG, C_Q = range(9)
C_GATE = 9 * D_G

G256_ROWS = ("sg_ln_g", "sg_ln_b", "pool_scale", "cc_dw_b", "cc_ln_g", "cc_ln_b")
G1024_ROWS = ("ln_g", "ln_b")
LOSS_ROW = 2


def _mm(a, b):
    return jnp.dot(a.astype(MM_DTYPE), b.astype(MM_DTYPE), preferred_element_type=F32)


def _mm_nt(a, b):
    return lax.dot_general(a.astype(MM_DTYPE), b.astype(MM_DTYPE), (((1,), (1,)), ((), ())),
                           preferred_element_type=F32)


def _mm_tn(a, b):
    return lax.dot_general(a.astype(MM_DTYPE), b.astype(MM_DTYPE), (((0,), (0,)), ((), ())),
                           preferred_element_type=F32)


def _sigmoid(x):
    return 0.5 * jnp.tanh(0.5 * x) + 0.5


_GELU_C = math.sqrt(2.0 / math.pi)
_GELU_A = 0.044715


def _gelu(x):
    th = jnp.tanh(_GELU_C * (x + _GELU_A * (x * x * x)))
    return 0.5 * x * (1.0 + th), th


def _dgelu(x, th):
    return 0.5 * (1.0 + th) + 0.5 * x * (1.0 - th * th) * (_GELU_C * (1.0 + 3.0 * _GELU_A * (x * x)))


def _ln_fwd(x, g, b):
    mu = jnp.mean(x, axis=-1, keepdims=True)
    xc = x - mu
    var = jnp.mean(xc * xc, axis=-1, keepdims=True)
    rstd = lax.rsqrt(var + LN_EPS)
    xhat = xc * rstd
    return xhat * g + b, xhat, rstd


def _ln_bwd(dy, xhat, rstd, g):
    dxhat = dy * g
    m1 = jnp.mean(dxhat, axis=-1, keepdims=True)
    m2 = jnp.mean(dxhat * xhat, axis=-1, keepdims=True)
    return rstd * (dxhat - m1 - xhat * m2)


def _rowsum(x):
    return jnp.sum(x, axis=0, keepdims=True)


def _col(ref, k):
    return ref[:, k * D_G:(k + 1) * D_G]


def _head_of_lane(shape):
    return jnp.right_shift(lax.broadcasted_iota(jnp.int32, shape, len(shape) - 1), HEAD_DIM.bit_length() - 1)


def _pool_select(lane_grp, s2, s4, s8, s16):
    return jnp.where(lane_grp == 0, s2, jnp.where(lane_grp == 1, s4, jnp.where(lane_grp == 2, s8, s16)))


def _row_view(ref, layer):
    return ref.at[pl.ds(layer, 1)]


def _make_residues(ext_ref, res_ref):
    rows = res_ref.shape[1]
    for r in range(1, 8):
        res_ref[r - 1] = ext_ref[pl.ds(r, rows), :]


def _rows_at(ext_ref, res_ref, off, tile):
    a, r = divmod(off, 8)
    if r == 0:
        return ext_ref[pl.ds(off, tile), :]
    return res_ref[r - 1, pl.ds(8 * a, tile), :]


def _residue_scratch(tile):
    return pltpu.VMEM((7, HALO + tile - 8, D_G), F32)


PROJ_SEGMENTS = ((C_XA * D_G, (C_CA + 1) * D_G), (C_XC * D_G, (C_XC + 1) * D_G), (C_DA * D_G, (C_DG + 1) * D_G),
                 (C_U * D_G, (C_V + 1) * D_G), (C_Q * D_G, (C_Q + 1) * D_G), (C_GATE, D_IN))


def _branch_forward(p_ref, ph_ref, first, row0, km_ref, vm_ref, w, ext_a, ext_c, ext_d, res_c, res_d, tile,
                    kept_ref=None, produce=None):
    r = {}
    produce = produce or (lambda: None)
    produce()
    produce()
    xa, ba, ca = _col(p_ref, C_XA), _col(p_ref, C_BA), _col(p_ref, C_CA)
    g_a = ca * xa
    ext_a[0:HALO] = jnp.where(first, 0.0, _col(ph_ref, C_CA) * _col(ph_ref, C_XA))
    ext_a[HALO:HALO + tile] = g_a
    conv_a = w["conv_a"][0:1, :] * ext_a[pl.ds(HALO - 2, tile), :]
    for k in range(1, CONV_A):
        conv_a = conv_a + w["conv_a"][k:k + 1, :] * ext_a[pl.ds(HALO - 2 + k, tile), :]
    r.update(xa=xa, ba=ba, ca=ca, g_a=g_a, conv_a=conv_a)
    ya = ba * conv_a

    produce()
    lane_grp = _head_of_lane((tile, D_G))
    trow = row0 + lax.broadcasted_iota(jnp.int32, (tile, D_G), 0)
    win = _pool_select(lane_grp, 2, 4, 8, 16)
    inv_cnt = 1.0 / jnp.minimum(trow + 1, win).astype(F32)
    if kept_ref is None:
        xc = _col(p_ref, C_XC)
        ext_c[0:HALO] = jnp.where(first, 0.0, _col(ph_ref, C_XC))
        ext_c[HALO:HALO + tile] = xc
        _make_residues(ext_c, res_c)
        acc = xc
        sums = {}
        for k in range(1, POOL_WINDOWS[-1]):
            acc = acc + _rows_at(ext_c, res_c, HALO - k, tile)
            if k + 1 in POOL_WINDOWS:
                sums[k + 1] = acc
        ypre = _pool_select(lane_grp, sums[2], sums[4], sums[8], sums[16]) * inv_cnt - xc
    else:
        ypre = kept_ref[:, D_G:2 * D_G]
    pool_mm = _mm(ypre, w["pool_wbd"][...])
    yc = pool_mm * w["pool_scale"][...]
    r.update(lane_grp=lane_grp, inv_cnt=inv_cnt, ypre=ypre, pool_mm=pool_mm)

    produce()
    da, dg = _col(p_ref, C_DA), _col(p_ref, C_DG)
    sig_dg = _sigmoid(dg)
    hd = da * sig_dg
    if kept_ref is None:
        ext_d[0:HALO] = jnp.where(first, 0.0, _col(ph_ref, C_DA) * _sigmoid(_col(ph_ref, C_DG)))
        ext_d[HALO:HALO + tile] = hd
        _make_residues(ext_d, res_d)
        conv_d = w["cc_dw_b"][...] + w["cc_dw_w"][0:1, :] * _rows_at(ext_d, res_d, HALO - (CONV_D - 1), tile)
        for j in range(1, CONV_D):
            conv_d = conv_d + w["cc_dw_w"][j:j + 1, :] * _rows_at(ext_d, res_d, HALO - (CONV_D - 1) + j, tile)
    else:
        conv_d = kept_ref[:, 0:D_G]
    r["kept"] = (conv_d, ypre)
    ln_d, xhat_d, rstd_d = _ln_fwd(conv_d, w["cc_ln_g"][...], w["cc_ln_b"][...])
    sig_ln = _sigmoid(ln_d)
    act_d = ln_d * sig_ln
    yd = _mm(act_d, w["cc_pw_w"][...])
    r.update(da=da, sig_dg=sig_dg, hd=hd, ln_d=ln_d, xhat_d=xhat_d, rstd_d=rstd_d, sig_ln=sig_ln, act_d=act_d)
    produce()

    u, v = _col(p_ref, C_U), _col(p_ref, C_V)
    ug, th_u = _gelu(u)
    vg, th_v = _gelu(v)
    vn, xhat_v, rstd_v = _ln_fwd(vg, w["sg_ln_g"][...], w["sg_ln_b"][...])
    tri = (lax.broadcasted_iota(jnp.int32, (CHUNK, CHUNK), 0)
           >= lax.broadcasted_iota(jnp.int32, (CHUNK, CHUNK), 1))
    wm = [jnp.where(tri, w["sg_w"][h], 0.0).astype(MM_DTYPE) for h in range(N_SUB)]
    lo = lax.broadcasted_iota(jnp.int32, (CHUNK, 2 * HEAD_DIM), 1) < HEAD_DIM
    chunks = []
    for c in range(tile // CHUNK):
        halves = []
        for hf in range(2):
            vh = vn[c * CHUNK:(c + 1) * CHUNK, hf * 128:(hf + 1) * 128]
            halves.append(_mm(wm[2 * hf], jnp.where(lo, vh, 0.0)) + _mm(wm[2 * hf + 1], jnp.where(lo, 0.0, vh)))
        chunks.append(jnp.concatenate(halves, axis=1) + w["sg_bias"][...])
    mixed = jnp.concatenate(chunks, axis=0)
    yb = ug * mixed
    r.update(u=u, v=v, ug=ug, th_u=th_u, th_v=th_v, vn=vn, xhat_v=xhat_v, rstd_v=rstd_v, wm=wm, lo=lo,
             mixed=mixed, tri=tri)

    produce()
    q = _col(p_ref, C_Q)
    ye = jnp.zeros((tile, D_G), F32)
    probs = []
    for h in range(N_SUB):
        s = _mm_nt(q, km_ref[h]) * ATT_SCALE
        e = jnp.exp(s - jnp.max(s, axis=-1, keepdims=True))
        p = e * (1.0 / jnp.sum(e, axis=-1, keepdims=True))
        probs.append(p)
        ye = ye + _mm(p, vm_ref[h])
    r.update(q=q, probs=probs)

    gate = p_ref[:, C_GATE:C_GATE + D_MIX]
    sig_gate = _sigmoid(gate)
    r.update(gate=gate, sig_gate=sig_gate, branch_out=(ya, yb, yc, yd, ye))
    return r


_BRANCH_REPL = ("sg_ln_g", "sg_ln_b", "sg_w", "sg_b", "pool_w", "pool_scale", "cc_dw_b", "cc_ln_g", "cc_ln_b")
_BRANCH_W_SCRATCH = (("conv_a", (CONV_A, D_G)), ("cc_dw_w", (CONV_D, D_G)), ("sg_bias", (CHUNK, D_G)),
                     ("pool_wbd", (D_G, D_G)), ("sgb8", (8, CHUNK)))


def _branch_weights(layer, nat, pw_ref, ca_ref, dw_ref, scr, init):
    @pl.when(init)
    def _():
        for p in range(N_DEV):
            scr["conv_a"][:, p * CONV_CH:(p + 1) * CONV_CH] = ca_ref[p, :, layer, :]
            scr["cc_dw_w"][:, p * CONV_CH:(p + 1) * CONV_CH] = dw_ref[p, :, layer, :]
        scr["sgb8"][...] = jnp.zeros((8, CHUNK), F32)
        scr["sgb8"][0:N_SUB] = nat["sg_b"][layer]
        sgb_t = scr["sgb8"][...].T
        head = _head_of_lane((CHUNK, D_G))
        bias = jnp.zeros((CHUNK, D_G), F32)
        for h in range(N_SUB):
            bias = jnp.where(head == h, sgb_t[:, h:h + 1], bias)
        scr["sg_bias"][...] = bias
        scr["pool_wbd"][...] = jnp.zeros((D_G, D_G), F32)
        for gi in range(N_SUB):
            sl = slice(gi * HEAD_DIM, (gi + 1) * HEAD_DIM)
            scr["pool_wbd"][sl, sl] = nat["pool_w"][layer, gi]

    w = {n: _row_view(nat[n], layer) for n in ("sg_ln_g", "sg_ln_b", "pool_scale", "cc_dw_b", "cc_ln_g", "cc_ln_b")}
    w.update(conv_a=scr["conv_a"], cc_dw_w=scr["cc_dw_w"], sg_bias=scr["sg_bias"], pool_wbd=scr["pool_wbd"],
             sg_w=nat["sg_w"].at[layer], cc_pw_w=pw_ref)
    return w


def _full_spec(a):
    nd = a.ndim
    return pl.BlockSpec(a.shape, lambda *_, _nd=nd: (0,) * _nd)


def _tie_specs(ties):
    return [pl.BlockSpec((8, 128), lambda *_: (0, 0)) for _ in ties]


def _params(*sem):
    return pltpu.CompilerParams(dimension_semantics=sem or None, vmem_limit_bytes=VMEM_LIMIT)


def _kv_project(mem, w_kv):
    def body(mem_ref, w_ref, km_ref, vm_ref):
        kv = _mm(mem_ref[...], w_ref[...])
        k, v = kv[:, :D_G], kv[:, D_G:]
        grp = _head_of_lane((MEM_LEN, D_G))
        for h in range(N_SUB):
            km_ref[h] = jnp.where(grp == h, k, 0.0).astype(km_ref.dtype)
            vm_ref[h] = jnp.where(grp == h, v, 0.0).astype(vm_ref.dtype)

    shp = jax.ShapeDtypeStruct((N_SUB, MEM_LEN, D_G), MM_DTYPE)
    return pl.pallas_call(body, out_shape=(shp, shp), name="kv_project", compiler_params=_params())(mem, w_kv)


def _layer_fwd_fused(x, wt_in, km, vm, layer, repl, pw, ca8, dw8, w_out, want_xn, ties=(), tile=FWD_TILE):
    s = x.shape[0]
    nat_arrays = [repl[n] for n in _BRANCH_REPL]
    n_nat, nt = len(nat_arrays), len(ties)

    def body(x_ref, wt_ref, km_ref, vm_ref, *rest):
        nat = dict(zip(_BRANCH_REPL, rest[:n_nat]))
        pw_ref, ca_ref, dw_ref, wo_ref, g_ref, b_ref = rest[n_nat:n_nat + 6]
        rest = rest[n_nat + 6 + nt:]
        p_ref, h_ref, z_ref, cd_ref = rest[:4]
        rest = rest[4:]
        if want_xn:
            xn_ref, rest = rest[0], rest[1:]
        ph_ref, ext_a, ext_c, ext_d, res_c, res_d = rest[:6]
        scr = dict(zip([n for n, _ in _BRANCH_W_SCRATCH], rest[6:]))
        i = pl.program_id(0)

        @pl.when(i == 0)
        def _():
            ph_ref[...] = jnp.zeros_like(ph_ref)

        xt = x_ref[...]
        xb = xt.astype(MM_DTYPE)
        segments = iter(PROJ_SEGMENTS)

        def produce():
            lo, hi = next(segments)
            p_ref[:, lo:hi] = _mm_nt(xb, wt_ref[lo:hi, :])

        w = _branch_weights(layer, nat, pw_ref, ca_ref, dw_ref, scr, i == 0)
        r = _branch_forward(p_ref, ph_ref, i == 0, i * tile, km_ref, vm_ref, w, ext_a, ext_c, ext_d, res_c, res_d,
                            tile, None, produce)
        ph_ref[...] = p_ref[tile - HALO:tile, :]
        cd_ref[:, 0:D_G], cd_ref[:, D_G:2 * D_G] = r["kept"]
        h = (jnp.concatenate(r["branch_out"], axis=1) * (r["gate"] * r["sig_gate"])).astype(h_ref.dtype)
        h_ref[...] = h
        z = ALPHA * xt + _mm(h, wo_ref[...])
        z_ref[...] = z
        if want_xn:
            xn_ref[...] = _ln_fwd(z, _row_view(g_ref, layer)[...], _row_view(b_ref, layer)[...])[0]

    row = lambda i: (i, 0)
    consts = [wt_in, km, vm] + nat_arrays + [pw, ca8, dw8, w_out, repl["ln_g"], repl["ln_b"]]
    act = jax.ShapeDtypeStruct((s, D_MODEL), F32)
    act_spec = pl.BlockSpec((tile, D_MODEL), row)
    res = pl.pallas_call(
        body, grid=(s // tile,),
        in_specs=[act_spec] + [_full_spec(a) for a in consts] + _tie_specs(ties),
        out_specs=(pl.BlockSpec((tile, D_IN), row), pl.BlockSpec((tile, D_MIX), row), act_spec,
                   pl.BlockSpec((tile, 2 * D_G), row)) + ((act_spec,) if want_xn else ()),
        out_shape=(jax.ShapeDtypeStruct((s, D_IN), F32), jax.ShapeDtypeStruct((s, D_MIX), MM_DTYPE), act,
                   jax.ShapeDtypeStruct((s, 2 * D_G), F32)) + ((act,) if want_xn else ()),
        scratch_shapes=[pltpu.VMEM((HALO, D_IN), F32)] + [pltpu.VMEM((HALO + tile, D_G), F32)] * 3
        + [_residue_scratch(tile)] * 2 + [pltpu.VMEM(shape, F32) for _, shape in _BRANCH_W_SCRATCH],
        name="layer_fwd", compiler_params=_params("arbitrary"),
    )(x, *consts, *ties)
    return tuple(res) if want_xn else tuple(res) + (None,)


_BRANCH_GRADS = (("g256", (8, D_G)), ("sg_w", (N_SUB, CHUNK, CHUNK)), ("sg_b", (8, CHUNK)),
                 ("pool_w", (N_SUB, HEAD_DIM, HEAD_DIM)), ("conv_a_w", (N_DEV, CONV_A, CONV_CH)),
                 ("cc_dw_w", (N_DEV, CONV_D, CONV_CH)), ("cc_pw_w", (D_G, D_G)),
                 ("dk", (N_SUB, MEM_LEN, D_G)), ("dv", (N_SUB, MEM_LEN, D_G)))
_BRANCH_ACC = (("conv_a", (CONV_A, D_G)), ("cc_dw_w", (CONV_D, D_G)), ("pool_wbd", (D_G, D_G)),
               ("sg_bias", (CHUNK, D_G)), ("sg_w", (N_SUB, CHUNK, CHUNK)))
_BRANCH_GRADS_NARROW = ("sg_w", "pool_w")


def _layer_bwd_fused(up, target, z, h, proj, kept, km, vm, layer, repl, w_out, pw, ca8, dw8, ties=(), tile=BWD_TILE):
    s = proj.shape[0]
    nt = s // tile
    hb = tile // HALO
    nat_arrays = [repl[n] for n in _BRANCH_REPL]
    n_nat, n_grads, n_acc, n_ties = len(nat_arrays), len(_BRANCH_GRADS), len(_BRANCH_ACC), len(ties)
    row_of = {n: k for k, n in enumerate(G256_ROWS)}
    from_loss = target is not None

    def body(o_ref, z_ref, h_ref, p_ref, ph_ref, cd_ref, km_ref, vm_ref, *rest):
        nat = dict(zip(_BRANCH_REPL, rest[:n_nat]))
        pw_ref, ca_ref, dw_ref, lng_ref, lnb_ref, wo_ref = rest[n_nat:n_nat + 6]
        rest = rest[n_nat + 6 + n_ties:]
        dz_ref, dp_ref, gw_ref, slab_ref = rest[:4]
        g = dict(zip([n for n, _ in _BRANCH_GRADS], rest[4:4 + n_grads]))
        rest = rest[4 + n_grads:]
        ext_a, rev_a, rev_c, rev_d, res_rc, res_rd, gw_acc, lacc = rest[:8]
        acc = dict(zip([n for n, _ in _BRANCH_ACC], rest[8:8 + n_acc]))
        scr = dict(zip([n for n, _ in _BRANCH_W_SCRATCH], rest[8 + n_acc:]))
        i = pl.program_id(0)
        t = nt - 1 - i

        @pl.when(i == 0)
        def _():
            for ref in list(g.values()) + list(acc.values()) + [rev_a, rev_c, rev_d, gw_acc, slab_ref, lacc]:
                ref[...] = jnp.zeros_like(ref)

        g_ln = _row_view(lng_ref, layer)[...]
        xn, xhat, rstd = _ln_fwd(z_ref[...], g_ln, _row_view(lnb_ref, layer)[...])
        if from_loss:
            err = xn - o_ref[...]
            lacc[...] += _rowsum(err * err)
            dxn = err * (1.0 / D_MODEL)
        else:
            dxn = o_ref[...]
        slab_ref[0:1, :] += _rowsum(dxn * xhat)
        slab_ref[1:2, :] += _rowsum(dxn)
        dz = _ln_bwd(dxn, xhat, rstd, g_ln)
        dz_ref[...] = dz
        dzb = dz.astype(MM_DTYPE)

        w = _branch_weights(layer, nat, pw_ref, ca_ref, dw_ref, scr, i == 0)
        r = _branch_forward(p_ref, ph_ref, t == 0, t * tile, km_ref, vm_ref, w, ext_a, None, None, None, None,
                            tile, cd_ref)

        def put(k, val, width=D_G):
            dp_ref[:, k:k + width] = val.astype(dp_ref.dtype)

        def add_row(name, val):
            k = row_of[name]
            g["g256"][k:k + 1, :] += val

        def push_rev(rev, val):
            head = rev[0:HALO]
            rev[tile:tile + HALO] = head
            rev[0:tile] = val

        gate, sig_gate = r["gate"], r["sig_gate"]

        def branch_grad(group):
            cols = slice(group * D_G, (group + 1) * D_G)
            dh_g = _mm_nt(dzb, wo_ref[cols, :])
            gate_g, sig_g = gate[:, cols], sig_gate[:, cols]
            put(C_GATE + group * D_G, dh_g * r["branch_out"][group] * (sig_g * (1.0 + gate_g * (1.0 - sig_g))))
            return dh_g * (gate_g * sig_g)

        dya = branch_grad(0)
        dyc = branch_grad(2)

        put(C_BA * D_G, dya * r["conv_a"])
        dconv_a = dya * r["ba"]
        push_rev(rev_a, dconv_a)
        dga = jnp.zeros((tile, D_G), F32)
        for k in range(CONV_A):
            ahead = rev_a[pl.ds(CONV_A - 1 - k, tile), :]
            dga = dga + w["conv_a"][k:k + 1, :] * ahead
            acc["conv_a"][k:k + 1, :] += _rowsum(r["g_a"] * ahead)
        put(C_CA * D_G, dga * r["xa"])
        put(C_XA * D_G, dga * r["ca"])

        dyd = branch_grad(3)
        add_row("pool_scale", _rowsum(dyc * r["pool_mm"]))
        dmm = dyc * w["pool_scale"][...]
        acc["pool_wbd"][...] += _mm_tn(r["ypre"], dmm)
        dypre = _mm_nt(dmm, w["pool_wbd"][...])
        dws = dypre * r["inv_cnt"]
        push_rev(rev_c, dws)
        _make_residues(rev_c, res_rc)
        run = dws
        sums = {}
        for k in range(1, POOL_WINDOWS[-1]):
            run = run + _rows_at(rev_c, res_rc, k, tile)
            if k + 1 in POOL_WINDOWS:
                sums[k + 1] = run
        put(C_XC * D_G, _pool_select(r["lane_grp"], sums[2], sums[4], sums[8], sums[16]) - dypre)

        dyb = branch_grad(1)
        gw_acc[...] += _mm_tn(h_ref[...], dzb)
        g["cc_pw_w"][...] += _mm_tn(r["act_d"], dyd)
        dact = _mm_nt(dyd, w["cc_pw_w"][...])
        sig_ln, ln_d = r["sig_ln"], r["ln_d"]
        dln = dact * (sig_ln * (1.0 + ln_d * (1.0 - sig_ln)))
        add_row("cc_ln_g", _rowsum(dln * r["xhat_d"]))
        add_row("cc_ln_b", _rowsum(dln))
        dconv_d = _ln_bwd(dln, r["xhat_d"], r["rstd_d"], w["cc_ln_g"][...])
        add_row("cc_dw_b", _rowsum(dconv_d))
        push_rev(rev_d, dconv_d)
        _make_residues(rev_d, res_rd)
        dhd = jnp.zeros((tile, D_G), F32)
        for j in range(CONV_D):
            ahead = _rows_at(rev_d, res_rd, CONV_D - 1 - j, tile)
            dhd = dhd + w["cc_dw_w"][j:j + 1, :] * ahead
            acc["cc_dw_w"][j:j + 1, :] += _rowsum(r["hd"] * ahead)
        sig_dg = r["sig_dg"]
        put(C_DA * D_G, dhd * sig_dg)
        put(C_DG * D_G, dhd * r["da"] * sig_dg * (1.0 - sig_dg))

        dye = branch_grad(4)
        dug = dyb * r["mixed"]
        dmixed = dyb * r["ug"]
        wm, lo, vn = r["wm"], r["lo"], r["vn"]
        dvn_chunks = []
        for c in range(tile // CHUNK):
            rows = slice(c * CHUNK, (c + 1) * CHUNK)
            acc["sg_bias"][...] += dmixed[rows, :]
            halves = []
            for hf in range(2):
                cols = slice(hf * 128, (hf + 1) * 128)
                dm = dmixed[rows, cols]
                dm_a, dm_b = jnp.where(lo, dm, 0.0), jnp.where(lo, 0.0, dm)
                vh = vn[rows, cols]
                acc["sg_w"][2 * hf] += _mm_nt(dm_a, vh)
                acc["sg_w"][2 * hf + 1] += _mm_nt(dm_b, vh)
                halves.append(_mm_tn(wm[2 * hf], dm_a) + _mm_tn(wm[2 * hf + 1], dm_b))
            dvn_chunks.append(jnp.concatenate(halves, axis=1))
        dvn = jnp.concatenate(dvn_chunks, axis=0)
        add_row("sg_ln_g", _rowsum(dvn * r["xhat_v"]))
        add_row("sg_ln_b", _rowsum(dvn))
        dvg = _ln_bwd(dvn, r["xhat_v"], r["rstd_v"], w["sg_ln_g"][...])
        put(C_V * D_G, dvg * _dgelu(r["v"], r["th_v"]))
        put(C_U * D_G, dug * _dgelu(r["u"], r["th_u"]))

        q = r["q"]
        dq = jnp.zeros((tile, D_G), F32)
        for h in range(N_SUB):
            p = r["probs"][h]
            dp = _mm_nt(dye, vm_ref[h])
            g["dv"][h] += _mm_tn(p, dye)
            ds = p * (dp - jnp.sum(dp * p, axis=-1, keepdims=True)) * ATT_SCALE
            dq = dq + _mm(ds, km_ref[h])
            g["dk"][h] += _mm_tn(ds, q)
        put(C_Q * D_G, dq)

        @pl.when(i == nt - 1)
        def _():
            for h in range(N_SUB):
                g["sg_w"][h] = jnp.where(r["tri"], acc["sg_w"][h], 0.0).astype(g["sg_w"].dtype)
            lane_head = _head_of_lane((CHUNK, D_G))
            col_of = lax.broadcasted_iota(jnp.int32, (CHUNK, 8), 1)
            ba = acc["sg_bias"][...]
            sgb_t = jnp.zeros((CHUNK, 8), F32)
            for h in range(N_SUB):
                col = jnp.sum(jnp.where(lane_head == h, ba, 0.0), axis=-1, keepdims=True)
                sgb_t = jnp.where(col_of == h, col, sgb_t)
            g["sg_b"][...] = sgb_t.T
            wbd = acc["pool_wbd"][...]
            for gi in range(N_SUB):
                sl = slice(gi * HEAD_DIM, (gi + 1) * HEAD_DIM)
                g["pool_w"][gi] = wbd[sl, sl].astype(g["pool_w"].dtype)
            ca, dw = acc["conv_a"][...], acc["cc_dw_w"][...]
            for p in range(N_DEV):
                g["conv_a_w"][p] = ca[:, p * CONV_CH:(p + 1) * CONV_CH]
                g["cc_dw_w"][p] = dw[:, p * CONV_CH:(p + 1) * CONV_CH]
            gw_ref[...] = gw_acc[...].astype(gw_ref.dtype)
            if from_loss:
                total = jnp.sum(lacc[...], axis=-1, keepdims=True) * (0.5 / D_MODEL)
                slab_ref[LOSS_ROW:LOSS_ROW + 1, :] = jnp.broadcast_to(total, (1, D_MODEL))

    rev = lambda i: (nt - 1 - i, 0)
    fixed = lambda i: (0, 0)
    act_spec = pl.BlockSpec((tile, D_MODEL), rev)
    grad_specs = tuple(pl.BlockSpec(shape, lambda i, _nd=len(shape): (0,) * _nd) for _, shape in _BRANCH_GRADS)
    grad_shapes = tuple(jax.ShapeDtypeStruct(shape, GRAD_DTYPE if n in _BRANCH_GRADS_NARROW else F32)
                        for n, shape in _BRANCH_GRADS)
    consts = [km, vm] + nat_arrays + [pw, ca8, dw8, repl["ln_g"], repl["ln_b"], w_out]
    outs = pl.pallas_call(
        body, grid=(nt,),
        in_specs=[act_spec, act_spec, pl.BlockSpec((tile, D_MIX), rev), pl.BlockSpec((tile, D_IN), rev),
                  pl.BlockSpec((HALO, D_IN), lambda i: (jnp.maximum((nt - 1 - i) * hb - 1, 0), 0)),
                  pl.BlockSpec((tile, 2 * D_G), rev)]
        + [_full_spec(a) for a in consts] + _tie_specs(ties),
        out_specs=(act_spec, pl.BlockSpec((tile, D_IN), rev), pl.BlockSpec((D_MIX, D_MODEL), fixed),
                   pl.BlockSpec((8, D_MODEL), fixed)) + grad_specs,
        out_shape=(jax.ShapeDtypeStruct((s, D_MODEL), F32), jax.ShapeDtypeStruct((s, D_IN), MM_DTYPE),
                   jax.ShapeDtypeStruct((D_MIX, D_MODEL), GRAD_DTYPE), jax.ShapeDtypeStruct((8, D_MODEL), F32))
        + grad_shapes,
        scratch_shapes=[pltpu.VMEM((HALO + tile, D_G), F32)] * 4 + [_residue_scratch(tile)] * 2
        + [pltpu.VMEM((D_MIX, D_MODEL), F32), pltpu.VMEM((1, D_MODEL), F32)]
        + [pltpu.VMEM(shape, F32) for _, shape in _BRANCH_ACC + _BRANCH_W_SCRATCH],
        name="layer_bwd_loss" if from_loss else "layer_bwd", compiler_params=_params("arbitrary"),
    )(target if from_loss else up, z, h, proj, proj, kept, *consts, *ties)
    return outs[0], outs[1], outs[2], outs[3], dict(zip([n for n, _ in _BRANCH_GRADS], outs[4:]))


def _dx_matmul(dproj, wt_in, dz, ties=(), tm=512):
    s = dproj.shape[0]

    def body(dp_ref, w_ref, dz_ref, *rest):
        o_ref = rest[len(ties)]
        o_ref[...] = _mm(dp_ref[...], w_ref[...]) + ALPHA * dz_ref[...]

    row = lambda i: (i, 0)
    return pl.pallas_call(
        body, grid=(s // tm,),
        in_specs=[pl.BlockSpec((tm, D_IN), row), _full_spec(wt_in), pl.BlockSpec((tm, D_MODEL), row)]
        + _tie_specs(ties),
        out_specs=pl.BlockSpec((tm, D_MODEL), row),
        out_shape=jax.ShapeDtypeStruct((s, D_MODEL), F32), name="dx_mm", compiler_params=_params("arbitrary"),
    )(dproj, wt_in, dz, *ties)


def _dw_in_matmul(x, dproj, ties=(), tk=512):
    s = x.shape[0]
    nk = s // tk
    blk = 2 * W_IN_COLS

    def body(x_ref, dp_ref, *rest):
        o_ref, acc = rest[len(ties):]
        k = pl.program_id(0)

        @pl.when(k == 0)
        def _():
            acc[...] = jnp.zeros_like(acc)

        xb = x_ref[...].astype(MM_DTYPE)
        for j in range(D_IN // blk):
            acc[j * blk:(j + 1) * blk, :] += _mm_tn(dp_ref[:, j * blk:(j + 1) * blk], xb)

        @pl.when(k == nk - 1)
        def _():
            o_ref[...] = acc[...].astype(o_ref.dtype)

    return pl.pallas_call(
        body, grid=(nk,),
        in_specs=[pl.BlockSpec((tk, D_MODEL), lambda k: (k, 0)), pl.BlockSpec((tk, D_IN), lambda k: (k, 0))]
        + _tie_specs(ties),
        out_specs=pl.BlockSpec((D_IN, D_MODEL), lambda k: (0, 0)),
        out_shape=jax.ShapeDtypeStruct((D_IN, D_MODEL), GRAD_DTYPE),
        scratch_shapes=[pltpu.VMEM((D_IN, D_MODEL), F32)], name="dw_in_mm", compiler_params=_params("arbitrary"),
    )(x, dproj, *ties)


def _in_bwd(x, dproj, wt_in, dz, tm=512):
    s = x.shape[0]
    n_steps = s // tm

    assert wt_in.dtype == GRAD_DTYPE
    blk = 2 * W_IN_COLS

    def body(x_ref, dp_ref, w_hbm, dz_ref, o_ref, gw_hbm, w_vmem, acc, sem):
        i = pl.program_id(0)

        @pl.when(i == 0)
        def _():
            fetch = pltpu.make_async_copy(w_hbm, w_vmem, sem)
            fetch.start()
            acc[...] = jnp.zeros_like(acc)
            fetch.wait()

        o_ref[...] = _mm(dp_ref[...], w_vmem[...]) + ALPHA * dz_ref[...]
        xb = x_ref[...].astype(MM_DTYPE)
        for j in range(D_IN // blk):
            acc[j * blk:(j + 1) * blk, :] += _mm_tn(dp_ref[:, j * blk:(j + 1) * blk], xb)

        @pl.when(i == n_steps - 1)
        def _():
            w_vmem[...] = acc[...].astype(w_vmem.dtype)
            emit = pltpu.make_async_copy(w_vmem, gw_hbm, sem)
            emit.start()
            emit.wait()

    row = lambda i: (i, 0)
    any_spec = pl.BlockSpec(memory_space=pl.ANY)
    return pl.pallas_call(
        body, grid=(n_steps,),
        in_specs=[pl.BlockSpec((tm, D_MODEL), row), pl.BlockSpec((tm, D_IN), row), any_spec,
                  pl.BlockSpec((tm, D_MODEL), row)],
        out_specs=(pl.BlockSpec((tm, D_MODEL), row), any_spec),
        out_shape=(jax.ShapeDtypeStruct((s, D_MODEL), F32), jax.ShapeDtypeStruct((D_IN, D_MODEL), GRAD_DTYPE)),
        scratch_shapes=[pltpu.VMEM((D_IN, D_MODEL), wt_in.dtype), pltpu.VMEM((D_IN, D_MODEL), F32),
                        pltpu.SemaphoreType.DMA],
        name="in_bwd", compiler_params=_params("arbitrary"),
    )(x, dproj, wt_in, dz)


def _kv_bwd(mem, dk, dv):
    def body(mem_ref, dk_ref, dv_ref, o_ref):
        grp = _head_of_lane((MEM_LEN, D_G))
        dk_sum = jnp.zeros((MEM_LEN, D_G), F32)
        dv_sum = jnp.zeros((MEM_LEN, D_G), F32)
        for h in range(N_SUB):
            dk_sum = dk_sum + jnp.where(grp == h, dk_ref[h], 0.0)
            dv_sum = dv_sum + jnp.where(grp == h, dv_ref[h], 0.0)
        o_ref[...] = _mm_tn(mem_ref[...], jnp.concatenate([dk_sum, dv_sum], axis=1)).astype(o_ref.dtype)

    return pl.pallas_call(body, out_shape=jax.ShapeDtypeStruct((D_MODEL, 2 * D_G), GRAD_DTYPE), name="kv_bwd",
                          compiler_params=_params())(mem, dk, dv)


def _layer_fwd(x, mem, layer, repl, gw, ties=()):
    km, vm = _kv_project(mem, gw["w_kv"])
    proj, h, z, kept, xn = _layer_fwd_fused(x, gw["wt_in"], km, vm, layer, repl, gw["pw"], gw["ca8"], gw["dw8"],
                                            gw["w_out"], layer < DEPTH - 1, ties)
    return xn, (x, proj, h, z, km, vm, kept)


def _layer_bwd_a(up, target, mem, layer, repl, gw, saved, ties=()):
    x_in, proj, h, z, km, vm, kept = saved
    dz, dproj, g_w_out, g1024, bg = _layer_bwd_fused(up, target, z, h, proj, kept, km, vm, layer, repl, gw["w_out"],
                                                     gw["pw"], gw["ca8"], gw["dw8"], ties)
    grads = {n: bg[n] for n in ("g256", "sg_w", "sg_b", "pool_w", "conv_a_w", "cc_dw_w")}
    grads.update(w_out=g_w_out.reshape(N_DEV, D_MIX // N_DEV, D_MODEL), g1024=g1024,
                 w_kv=_kv_bwd(mem, bg["dk"], bg["dv"]).reshape(N_DEV, D_MODEL // N_DEV, 2 * D_G),
                 cc_pw_w=bg["cc_pw_w"].reshape(N_DEV, CONV_CH, D_G))
    return dz, dproj, grads


def _landing_shapes(items):
    out = []
    for a, scatter, pick in items:
        shape = a.shape if scatter else (N_DEV,) + (a.shape if pick is None else a.shape[1:])
        out.append(jax.ShapeDtypeStruct(shape, a.dtype))
    return tuple(out)


def _exchange_sems(n):
    return [pltpu.SemaphoreType.DMA(((N_DEV - 1) * n,)), pltpu.SemaphoreType.DMA(((N_DEV - 1) * n,)),
            pltpu.SemaphoreType.DMA((n,))]


def _exchange_copies(modes, ins, outs, send_sems, recv_sems, local_sems):
    n = len(ins)
    x, y, c = lax.axis_index("x"), lax.axis_index("y"), lax.axis_index("c")
    me = 4 * x + 2 * y + c

    def src_of(a, dest):
        scatter, pick = modes[a]
        if scatter:
            return ins[a].at[dest]
        return ins[a] if pick is None else ins[a].at[pick]

    local = [pltpu.make_async_copy(src_of(a, me), outs[a].at[me], local_sems.at[a]) for a in range(n)]
    sends, recvs = [], []
    for k in range(1, N_DEV):
        px = 1 - x if k & 4 else x
        py = 1 - y if k & 2 else y
        pc = 1 - c if k & 1 else c
        peer = 4 * px + 2 * py + pc
        for a in range(n):
            sems = dict(send_sem=send_sems.at[(k - 1) * n + a], recv_sem=recv_sems.at[(k - 1) * n + a],
                        device_id=(px, py, pc), device_id_type=pl.DeviceIdType.MESH)
            sends.append(pltpu.make_async_remote_copy(src_ref=src_of(a, peer), dst_ref=outs[a].at[me], **sems))
            recvs.append(pltpu.make_async_remote_copy(src_ref=src_of(a, peer), dst_ref=outs[a].at[peer], **sems))
    return local, sends, recvs


def _gather_two_level(items, name):
    n = len(items)
    assert not any(scatter for _, scatter, _ in items)
    picks = [pick for _, _, pick in items]

    def body(*refs):
        ins, outs = refs[:n], refs[n:2 * n]
        send_sems, recv_sems, local_sems = refs[2 * n:]
        x, y, c = lax.axis_index("x"), lax.axis_index("y"), lax.axis_index("c")
        sib = 1 - c
        chips = [(1 - x, y), (x, 1 - y), (1 - x, 1 - y)]

        def slot(a, px, py, pc):
            return outs[a].at[4 * px + 2 * py + pc]

        def copy(k, a, src, block, to):
            return pltpu.make_async_remote_copy(
                src_ref=src, dst_ref=slot(a, *block), send_sem=send_sems.at[k * n + a],
                recv_sem=recv_sems.at[k * n + a], device_id=to, device_id_type=pl.DeviceIdType.MESH)

        own = [ins[a] if picks[a] is None else ins[a].at[picks[a]] for a in range(n)]
        local = [pltpu.make_async_copy(own[a], slot(a, x, y, c), local_sems.at[a]) for a in range(n)]
        first = [copy(0, a, own[a], (x, y, c), (x, y, sib)) for a in range(n)]
        first += [copy(1 + j, a, own[a], (x, y, c), (*chip, c)) for j, chip in enumerate(chips[:2]) for a in range(n)]
        for cp in local + first:
            cp.start()

        def pass_on(j, a):
            chip = chips[j]
            copy(1 + j, a, own[a], (*chip, c), (x, y, c)).wait_recv()
            fwd = copy(4 + j, a, slot(a, *chip, c), (*chip, c), (x, y, sib))
            fwd.start()
            return fwd

        passed = [pass_on(j, a) for j in range(2) for a in range(n)]
        south = c == 0
        via = tuple(jnp.where(south, p, q) for p, q in zip(chips[0], chips[1]))
        blk = tuple(jnp.where(south, q, p) for p, q in zip(chips[0], chips[1]))
        relayed = [copy(3, a, slot(a, *blk, c), (*blk, c), (*via, c)) for a in range(n)]
        for cp in relayed:
            cp.start()
        passed += [pass_on(2, a) for a in range(n)]
        first += relayed
        for a in range(n):
            copy(0, a, own[a], (x, y, sib), (x, y, c)).wait_recv()
        for j, chip in enumerate(chips):
            for a in range(n):
                copy(4 + j, a, own[a], (*chip, sib), (x, y, c)).wait_recv()
        for cp in first + passed:
            cp.wait_send()
        for cp in local:
            cp.wait()

    any_spec = pl.BlockSpec(memory_space=pl.ANY)
    return pl.pallas_call(
        body, in_specs=[any_spec] * n, out_specs=(any_spec,) * n, out_shape=_landing_shapes(items),
        scratch_shapes=[pltpu.SemaphoreType.DMA((7 * n,)), pltpu.SemaphoreType.DMA((7 * n,)),
                        pltpu.SemaphoreType.DMA((n,))],
        name=name,
    )(*[a for a, _, _ in items])


_HBM_SPEC = pl.BlockSpec(memory_space=pltpu.HBM)
_SEM_SPEC = pl.BlockSpec(memory_space=pltpu.SEMAPHORE)
_SPLIT_PARAMS = pltpu.CompilerParams(has_side_effects=pltpu.SideEffectType.DATAFLOW_SIDE_EFFECTING)


def _split_start(srcs, lands, plan, sem_shapes, name):
    n_src, n_land = len(srcs), len(lands)
    n_buf = n_src + n_land
    bufs = [pltpu.with_memory_space_constraint(a, pltpu.HBM) for a in list(srcs) + list(lands)]

    def body(*refs):
        local, sends, _ = plan(refs[:n_src], refs[n_src:n_buf], *refs[n_buf:n_buf + 3])
        for cp in local + sends:
            cp.start()
        token = refs[-1]
        token[...] = jnp.zeros_like(token)

    res = pl.pallas_call(
        body, name=name, in_specs=[_HBM_SPEC] * n_buf,
        out_shape=tuple(sem_shapes) + tuple(pltpu.HBM(a.shape, a.dtype) for a in bufs)
        + (jax.ShapeDtypeStruct((8, 128), F32),),
        out_specs=(_SEM_SPEC,) * 3 + (_HBM_SPEC,) * n_buf + (pl.BlockSpec(memory_space=pltpu.VMEM),),
        input_output_aliases={i: 3 + i for i in range(n_buf)}, compiler_params=_SPLIT_PARAMS,
    )(*bufs)
    return dict(sems=res[:3], srcs=res[3:3 + n_src], lands=res[3 + n_src:3 + n_buf], token=res[-1], plan=plan)


def _split_wait(ticket, after, name):
    n_src, n_land = len(ticket["srcs"]), len(ticket["lands"])
    n_buf = n_src + n_land
    plan = ticket["plan"]

    def body(*refs):
        local, sends, recvs = plan(refs[:n_src], refs[n_src:n_buf], *refs[n_buf:n_buf + 3])
        for cp in recvs:
            cp.wait_recv()
        for cp in sends:
            cp.wait_send()
        for cp in local:
            cp.wait()

    bufs = list(ticket["srcs"]) + list(ticket["lands"])
    res = pl.pallas_call(
        body, name=name, in_specs=[_HBM_SPEC] * n_buf + [_SEM_SPEC] * 3 + [pl.BlockSpec(memory_space=pl.ANY)],
        out_shape=tuple(pltpu.HBM(a.shape, a.dtype) for a in bufs), out_specs=(_HBM_SPEC,) * n_buf,
        input_output_aliases={i: i for i in range(n_buf)}, compiler_params=_SPLIT_PARAMS,
    )(*bufs, *ticket["sems"], after)
    return res[n_src:]


def _empty_landings(items):
    return [lax.empty(s.shape, s.dtype) for s in _landing_shapes(items)]


def _exchange_start(items, name):
    modes = [(scatter, pick) for _, scatter, pick in items]
    plan = lambda ins, outs, *sems: _exchange_copies(modes, ins, outs, *sems)
    return _split_start([a for a, _, _ in items], _empty_landings(items), plan, _exchange_sems(len(items)), name)


def _two_level_plans(picks):
    n = len(picks)

    def place():
        x, y, c = lax.axis_index("x"), lax.axis_index("y"), lax.axis_index("c")
        return x, y, c, 1 - c, [(1 - x, y), (x, 1 - y), (1 - x, 1 - y)]

    def copy(outs, send_sems, recv_sems, k, a, src, block, to):
        px, py, pc = block
        return pltpu.make_async_remote_copy(
            src_ref=src, dst_ref=outs[a].at[4 * px + 2 * py + pc], send_sem=send_sems.at[k * n + a],
            recv_sem=recv_sems.at[k * n + a], device_id=to, device_id_type=pl.DeviceIdType.MESH)

    def between_chips(ins, outs, send_sems, recv_sems, local_sems):
        x, y, c, sib, chips = place()
        own = [ins[a] if picks[a] is None else ins[a].at[picks[a]] for a in range(n)]
        mk = lambda *args: copy(outs, send_sems, recv_sems, *args)
        local = [pltpu.make_async_copy(own[a], outs[a].at[4 * x + 2 * y + c], local_sems.at[a]) for a in range(n)]
        sends = [mk(0, a, own[a], (x, y, c), (x, y, sib)) for a in range(n)]
        sends += [mk(1 + j, a, own[a], (x, y, c), (*chip, c)) for j, chip in enumerate(chips) for a in range(n)]
        recvs = [mk(0, a, own[a], (x, y, sib), (x, y, c)) for a in range(n)]
        recvs += [mk(1 + j, a, own[a], (*chip, c), (x, y, c)) for j, chip in enumerate(chips) for a in range(n)]
        return local, sends, recvs

    def within_chip(ins, outs, send_sems, recv_sems, local_sems):
        x, y, c, sib, chips = place()
        mk = lambda *args: copy(outs, send_sems, recv_sems, *args)
        slot = lambda a, px, py, pc: outs[a].at[4 * px + 2 * py + pc]
        sends = [mk(j, a, slot(a, *chip, c), (*chip, c), (x, y, sib)) for j, chip in enumerate(chips)
                 for a in range(n)]
        recvs = [mk(j, a, slot(a, *chip, c), (*chip, sib), (x, y, c)) for j, chip in enumerate(chips)
                 for a in range(n)]
        return [], sends, recvs

    sems = lambda k: [pltpu.SemaphoreType.DMA((k * n,)), pltpu.SemaphoreType.DMA((k * n,)),
                      pltpu.SemaphoreType.DMA((n,))]
    return between_chips, sems(4), within_chip, sems(3)


def _adam_math(g, w, m, v):
    m_new = ADAM_B1 * m + (1.0 - ADAM_B1) * g
    v_new = ADAM_B2 * v + (1.0 - ADAM_B2) * (g * g)
    m_hat = m_new / (1.0 - ADAM_B1 ** ADAM_STEP)
    v_hat = v_new / (1.0 - ADAM_B2 ** ADAM_STEP)
    return -ADAM_LR * (m_hat / (jnp.sqrt(v_hat) + ADAM_EPS) + ADAM_WD * w), m_new, v_new


def _adamw_big(parts, w, m, v, layer, prev, name, tr):
    depth, rows, cols = w.shape

    def body(p_ref, w_ref, m_ref, v_ref, *rest):
        g_out, d_out, m_out, v_out = rest[len(prev):]
        g = p_ref[0].astype(F32)
        for q in range(1, N_DEV):
            g = g + p_ref[q].astype(F32)
        d, m_new, v_new = _adam_math(g, w_ref[...], m_ref[...], v_ref[...])
        g_out[...] = g
        d_out[...] = d
        m_out[...] = m_new
        v_out[...] = v_new

    blk = pl.BlockSpec((None, tr, cols), lambda i: (layer, i, 0))
    shp = jax.ShapeDtypeStruct((depth, rows, cols), F32)
    return pl.pallas_call(
        body, grid=(rows // tr,),
        in_specs=[pl.BlockSpec((N_DEV, tr, cols), lambda i: (0, i, 0)), blk, blk, blk]
        + [pl.BlockSpec(memory_space=pl.ANY)] * len(prev),
        out_specs=(blk,) * 4, out_shape=(shp,) * 4,
        input_output_aliases={4 + j: j for j in range(len(prev))},
        name=name, compiler_params=_params("arbitrary"),
    )(parts, w, m, v, *prev)


_SMALL_TENSORS = (("conv_a_w", "conv_a_w", None), ("cc_dw_w", "cc_dw_w", None), ("cc_pw_w", "cc_pw_w", None),
                  ("sg_w", "sg_w", None), ("pool_w", "pool_w", None), ("sg_b", "sg_b", None)) \
    + tuple((n, "g256", k) for k, n in enumerate(G256_ROWS)) + tuple((n, "g1024", k) for k, n in enumerate(G1024_ROWS))
_SMALL_LANDINGS = ("conv_a_w", "cc_dw_w", "cc_pw_w", "sg_w", "pool_w", "sg_b", "g256", "g1024")
_TAPS_FIRST = ("conv_a_w", "cc_dw_w")


def _adamw_small(landings, wts, mom, var):
    names = [n for n, _, _ in _SMALL_TENSORS]
    n_land = DEPTH * len(_SMALL_LANDINGS)
    n_t = len(names)

    def body(*refs):
        land = [dict(zip(_SMALL_LANDINGS, refs[l * len(_SMALL_LANDINGS):(l + 1) * len(_SMALL_LANDINGS)]))
                for l in range(DEPTH)]
        w_refs = dict(zip(names, refs[n_land:n_land + n_t]))
        m_refs = dict(zip(names, refs[n_land + n_t:n_land + 2 * n_t]))
        v_refs = dict(zip(names, refs[n_land + 2 * n_t:n_land + 3 * n_t]))
        outs = refs[n_land + 3 * n_t:]
        out_refs = {n: outs[4 * k:4 * k + 4] for k, n in enumerate(names)}
        loss_ref = outs[4 * n_t]
        for name, key, row in _SMALL_TENSORS:
            for l in range(DEPTH):
                src = land[l][key]
                if row is not None:
                    part = lambda q: src[q, row:row + 1, :]
                    at = (slice(l, l + 1),)
                elif name == "sg_b":
                    part = lambda q: src[q, 0:N_SUB, :]
                    at = (l,)
                elif name in _TAPS_FIRST:
                    part = lambda q: src[q]
                    at = (slice(None), l)
                else:
                    part = lambda q: src[q]
                    at = (l,)
                g = part(0).astype(F32)
                for q in range(1, N_DEV):
                    g = g + part(q).astype(F32)
                d, m_new, v_new = _adam_math(g, w_refs[name][at], m_refs[name][at], v_refs[name][at])
                for ref, val in zip(out_refs[name], (g, d, m_new, v_new)):
                    ref[at] = val
        src = land[DEPTH - 1]["g1024"]
        loss = src[0, LOSS_ROW:LOSS_ROW + 1, 0:128]
        for q in range(1, N_DEV):
            loss = loss + src[q, LOSS_ROW:LOSS_ROW + 1, 0:128]
        loss_ref[...] = loss

    ins = [landings[l][k] for l in range(DEPTH) for k in _SMALL_LANDINGS] \
        + [src[n] for src in (wts, mom, var) for n in names]
    out_shape = tuple(jax.ShapeDtypeStruct(wts[n].shape, F32) for n in names for _ in range(4)) \
        + (jax.ShapeDtypeStruct((1, 128), F32),)
    res = pl.pallas_call(body, out_shape=out_shape, name="adamw_small", compiler_params=_params())(*ins)
    return {n: res[4 * k:4 * k + 4] for k, n in enumerate(names)}, res[4 * n_t]


_BIG = (("w_in", 64), ("w_out", 32), ("w_kv", 32))
_GRAD_ITEMS_EARLY = ("w_out", "w_kv", "cc_pw_w", "conv_a_w", "cc_dw_w")
_GRAD_ITEMS_REPL = ("g256", "sg_w", "sg_b", "pool_w", "g1024")


def _grad_items(grads, with_w_in):
    items = [(grads[n], True, None) for n in (("w_in",) if with_w_in else ()) + _GRAD_ITEMS_EARLY]
    return items + [(grads[n], False, None) for n in _GRAD_ITEMS_REPL]


def _landed(parts, with_w_in):
    names = (("w_in",) if with_w_in else ()) + _GRAD_ITEMS_EARLY + _GRAD_ITEMS_REPL
    return dict(zip(names, parts))


def _gathered_weights(wt_in8, w_kv8, w_out8, pw8, ca8, dw8):
    return dict(wt_in=wt_in8.reshape(D_IN, D_MODEL), w_kv=w_kv8.reshape(D_MODEL, 2 * D_G),
                w_out=w_out8.reshape(D_MIX, D_MODEL), pw=pw8.reshape(D_G, D_G), ca8=ca8, dw8=dw8)


def kernel(x, mem, w_in, conv_a_w, sg_ln_g, sg_ln_b, sg_w, sg_b, pool_w, pool_scale, cc_dw_w, cc_dw_b, cc_ln_g, cc_ln_b, cc_pw_w, w_kv, w_out, ln_g, ln_b, loss_target, m_w_in, m_conv_a_w, m_sg_ln_g, m_sg_ln_b, m_sg_w, m_sg_b, m_pool_w, m_pool_scale, m_cc_dw_w, m_cc_dw_b, m_cc_ln_g, m_cc_ln_b, m_cc_pw_w, m_w_kv, m_w_out, m_ln_g, m_ln_b, v_w_in, v_conv_a_w, v_sg_ln_g, v_sg_ln_b, v_sg_w, v_sg_b, v_pool_w, v_pool_scale, v_cc_dw_w, v_cc_dw_b, v_cc_ln_g, v_cc_ln_b, v_cc_pw_w, v_w_kv, v_w_out, v_ln_g, v_ln_b):
    names = ("w_in", "conv_a_w", "sg_ln_g", "sg_ln_b", "sg_w", "sg_b", "pool_w", "pool_scale", "cc_dw_w", "cc_dw_b",
             "cc_ln_g", "cc_ln_b", "cc_pw_w", "w_kv", "w_out", "ln_g", "ln_b")
    wts = dict(zip(names, (w_in, conv_a_w, sg_ln_g, sg_ln_b, sg_w, sg_b, pool_w, pool_scale, cc_dw_w, cc_dw_b,
                           cc_ln_g, cc_ln_b, cc_pw_w, w_kv, w_out, ln_g, ln_b)))
    mom = dict(zip(names, (m_w_in, m_conv_a_w, m_sg_ln_g, m_sg_ln_b, m_sg_w, m_sg_b, m_pool_w, m_pool_scale,
                           m_cc_dw_w, m_cc_dw_b, m_cc_ln_g, m_cc_ln_b, m_cc_pw_w, m_w_kv, m_w_out, m_ln_g, m_ln_b)))
    var = dict(zip(names, (v_w_in, v_conv_a_w, v_sg_ln_g, v_sg_ln_b, v_sg_w, v_sg_b, v_pool_w, v_pool_scale,
                           v_cc_dw_w, v_cc_dw_b, v_cc_ln_g, v_cc_ln_b, v_cc_pw_w, v_w_kv, v_w_out, v_ln_g, v_ln_b)))
    repl = wts
    xs, mems, tgt = x[0], mem[0], loss_target[0]
    turned = {"w_in": (0, 2, 1), "conv_a_w": (1, 0, 2), "cc_dw_w": (1, 0, 2)}
    wts, mom, var = [{n: (jnp.transpose(a, turned[n]) if n in turned else a) for n, a in src.items()}
                     for src in (wts, mom, var)]
    wb = {n: wts[n].astype(MM_DTYPE) for n in ("w_in", "w_kv", "w_out", "cc_pw_w")}

    wt8_0, wkv8_0, wo8_0, pw8_0, ca8, dw8 = _gather_two_level(
        [(wb["w_in"], False, 0), (wb["w_kv"], False, 0), (wb["w_out"], False, 0), (wb["cc_pw_w"], False, 0),
         (wts["conv_a_w"], False, None), (wts["cc_dw_w"], False, None)], "gather_weights_0")
    gw0 = _gathered_weights(wt8_0, wkv8_0, wo8_0, pw8_0, ca8, dw8)
    items_1 = [(wb[n], False, 1) for n in ("w_in", "w_kv", "w_out", "cc_pw_w")]
    between_chips, sems_a, within_chip, sems_b = _two_level_plans([1] * len(items_1))
    chips_1 = _split_start([a for a, _, _ in items_1], _empty_landings(items_1), between_chips, sems_a,
                           "gather_weights_1a_start")
    x1, saved0 = _layer_fwd(xs, mems, 0, repl, gw0, ties=(chips_1["token"],))
    core_1 = _split_start([], _split_wait(chips_1, x1, "gather_weights_1a_wait"), within_chip, sems_b,
                          "gather_weights_1b_start")
    gw1 = _gathered_weights(*_split_wait(core_1, core_1["token"], "gather_weights_1b_wait"), ca8, dw8)
    _, saved1 = _layer_fwd(x1, mems, 1, repl, gw1)

    dz1, dproj1, g1 = _layer_bwd_a(None, tgt, mems, 1, repl, gw1, saved1)
    shards = lambda g: g.reshape(N_DEV, W_IN_COLS, D_MODEL)
    up, g_wt_in_1 = _in_bwd(saved1[0], dproj1, gw1["wt_in"], dz1)
    g1["w_in"] = shards(g_wt_in_1)
    grads_1 = _exchange_start(_grad_items(g1, True), "exchange_grads_1_start")
    dz0, dproj0, g0 = _layer_bwd_a(up, None, mems, 0, repl, gw0, saved0, (grads_1["token"],))
    early_0 = _exchange_start(_grad_items(g0, False), "exchange_grads_0a_start")
    g_wt_in_0 = _dw_in_matmul(saved0[0], dproj0, (early_0["token"],))
    late_0 = _exchange_start([(shards(g_wt_in_0), True, None)], "exchange_grads_0b_start")
    grad_x = _dx_matmul(dproj0, gw0["wt_in"], dz0, (late_0["token"],))

    landed = [None, _landed(_split_wait(grads_1, grad_x, "exchange_grads_1_wait"), True)]
    big = {}
    for n, tr in _BIG:
        big[n] = _adamw_big(landed[1][n], wts[n], mom[n], var[n], 1, (), "adamw_" + n + "_1", tr)
    landed[0] = _landed(_split_wait(early_0, big["w_kv"][0], "exchange_grads_0a_wait"), False)
    for n, tr in _BIG[1:]:
        big[n] = _adamw_big(landed[0][n], wts[n], mom[n], var[n], 0, big[n], "adamw_" + n + "_0", tr)
    small, loss = _adamw_small(landed, wts, mom, var)
    (landed[0]["w_in"],) = _split_wait(late_0, loss, "exchange_grads_0b_wait")
    big["w_in"] = _adamw_big(landed[0]["w_in"], wts["w_in"], mom["w_in"], var["w_in"], 0, big["w_in"],
                             "adamw_w_in_0", _BIG[0][1])

    res = {**small, **big}
    res = {n: ([jnp.transpose(a, turned[n]) for a in r] if n in turned else r) for n, r in res.items()}
    return (loss[0, 0], grad_x[None], *[res[n][0] for n in names], *[res[n][1] for n in names],
            *[res[n][2] for n in names], *[res[n][3] for n in names])
```

```python
import math

import jax
import jax.numpy as jnp
from jax import lax
from jax.experimental import pallas as pl
from jax.experimental.pallas import tpu as pltpu

F32 = jnp.float32
MM_DTYPE = jnp.bfloat16
GRAD_DTYPE = jnp.bfloat16

D_MODEL = 1024
DEPTH = 2
D_G = 256
N_GROUPS = 5
D_MIX = N_GROUPS * D_G
N_SUB = 4
HEAD_DIM = D_G // N_SUB
CONV_A = 3
CONV_D = 31
CHUNK = 128
POOL_WINDOWS = (2, 4, 8, 16)
MEM_LEN = 256
LN_EPS = 1e-5
ALPHA = (2.0 * DEPTH) ** 0.25
D_IN = 9 * D_G + D_MIX
ATT_SCALE = 1.0 / math.sqrt(HEAD_DIM)

ADAM_LR = 0.001
ADAM_B1 = 0.9
ADAM_B2 = 0.999
ADAM_EPS = 1e-08
ADAM_WD = 0.01
ADAM_STEP = 10

N_DEV = 8
W_IN_COLS = D_IN // N_DEV
CONV_CH = D_G // N_DEV
HALO = 32
FWD_TILE = 512
BWD_TILE = 256
VMEM_LIMIT = 56 * 1024 * 1024

C_XA, C_BA, C_CA, C_U, C_V, C_XC, C_DA, C_DG, C_Q = range(9)
C_GATE = 9 * D_G

G256_ROWS = ("sg_ln_g", "sg_ln_b", "pool_scale", "cc_dw_b", "cc_ln_g", "cc_ln_b")
G1024_ROWS = ("ln_g", "ln_b")
LOSS_ROW = 2


def _mm(a, b):
    return jnp.dot(a.astype(MM_DTYPE), b.astype(MM_DTYPE), preferred_element_type=F32)


def _mm_nt(a, b):
    return lax.dot_general(a.astype(MM_DTYPE), b.astype(MM_DTYPE), (((1,), (1,)), ((), ())),
                           preferred_element_type=F32)


def _mm_tn(a, b):
    return lax.dot_general(a.astype(MM_DTYPE), b.astype(MM_DTYPE), (((0,), (0,)), ((), ())),
                           preferred_element_type=F32)


def _sigmoid(x):
    return 0.5 * jnp.tanh(0.5 * x) + 0.5


_GELU_C = math.sqrt(2.0 / math.pi)
_GELU_A = 0.044715


def _gelu(x):
    th = jnp.tanh(_GELU_C * (x + _GELU_A * (x * x * x)))
    return 0.5 * x * (1.0 + th), th


def _dgelu(x, th):
    return 0.5 * (1.0 + th) + 0.5 * x * (1.0 - th * th) * (_GELU_C * (1.0 + 3.0 * _GELU_A * (x * x)))


def _ln_fwd(x, g, b):
    mu = jnp.mean(x, axis=-1, keepdims=True)
    xc = x - mu
    var = jnp.mean(xc * xc, axis=-1, keepdims=True)
    rstd = lax.rsqrt(var + LN_EPS)
    xhat = xc * rstd
    return xhat * g + b, xhat, rstd


def _ln_bwd(dy, xhat, rstd, g):
    dxhat = dy * g
    m1 = jnp.mean(dxhat, axis=-1, keepdims=True)
    m2 = jnp.mean(dxhat * xhat, axis=-1, keepdims=True)
    return rstd * (dxhat - m1 - xhat * m2)


def _rowsum(x):
    return jnp.sum(x, axis=0, keepdims=True)


def _col(ref, k):
    return ref[:, k * D_G:(k + 1) * D_G]


def _head_of_lane(shape):
    return jnp.right_shift(lax.broadcasted_iota(jnp.int32, shape, len(shape) - 1), HEAD_DIM.bit_length() - 1)


def _pool_select(lane_grp, s2, s4, s8, s16):
    return jnp.where(lane_grp == 0, s2, jnp.where(lane_grp == 1, s4, jnp.where(lane_grp == 2, s8, s16)))


def _row_view(ref, layer):
    return ref.at[pl.ds(layer, 1)]


def _make_residues(ext_ref, res_ref):
    rows = res_ref.shape[1]
    for r in range(1, 8):
        res_ref[r - 1] = ext_ref[pl.ds(r, rows), :]


def _rows_at(ext_ref, res_ref, off, tile):
    a, r = divmod(off, 8)
    if r == 0:
        return ext_ref[pl.ds(off, tile), :]
    return res_ref[r - 1, pl.ds(8 * a, tile), :]


def _residue_scratch(tile):
    return pltpu.VMEM((7, HALO + tile - 8, D_G), F32)


PROJ_SEGMENTS = ((C_XA * D_G, (C_CA + 1) * D_G), (C_XC * D_G, (C_XC + 1) * D_G), (C_DA * D_G, (C_DG + 1) * D_G),
                 (C_U * D_G, (C_V + 1) * D_G), (C_Q * D_G, (C_Q + 1) * D_G), (C_GATE, D_IN))


def _branch_forward(p_ref, ph_ref, first, row0, km_ref, vm_ref, w, ext_a, ext_c, ext_d, res_c, res_d, tile,
                    kept_ref=None, produce=None):
    r = {}
    produce = produce or (lambda: None)
    produce()
    produce()
    xa, ba, ca = _col(p_ref, C_XA), _col(p_ref, C_BA), _col(p_ref, C_CA)
    g_a = ca * xa
    ext_a[0:HALO] = jnp.where(first, 0.0, _col(ph_ref, C_CA) * _col(ph_ref, C_XA))
    ext_a[HALO:HALO + tile] = g_a
    conv_a = w["conv_a"][0:1, :] * ext_a[pl.ds(HALO - 2, tile), :]
    for k in range(1, CONV_A):
        conv_a = conv_a + w["conv_a"][k:k + 1, :] * ext_a[pl.ds(HALO - 2 + k, tile), :]
    r.update(xa=xa, ba=ba, ca=ca, g_a=g_a, conv_a=conv_a)
    ya = ba * conv_a

    produce()
    lane_grp = _head_of_lane((tile, D_G))
    trow = row0 + lax.broadcasted_iota(jnp.int32, (tile, D_G), 0)
    win = _pool_select(lane_grp, 2, 4, 8, 16)
    inv_cnt = 1.0 / jnp.minimum(trow + 1, win).astype(F32)
    if kept_ref is None:
        xc = _col(p_ref, C_XC)
        ext_c[0:HALO] = jnp.where(first, 0.0, _col(ph_ref, C_XC))
        ext_c[HALO:HALO + tile] = xc
        _make_residues(ext_c, res_c)
        acc = xc
        sums = {}
        for k in range(1, POOL_WINDOWS[-1]):
            acc = acc + _rows_at(ext_c, res_c, HALO - k, tile)
            if k + 1 in POOL_WINDOWS:
                sums[k + 1] = acc
        ypre = _pool_select(lane_grp, sums[2], sums[4], sums[8], sums[16]) * inv_cnt - xc
    else:
        ypre = kept_ref[:, D_G:2 * D_G]
    pool_mm = _mm(ypre, w["pool_wbd"][...])
    yc = pool_mm * w["pool_scale"][...]
    r.update(lane_grp=lane_grp, inv_cnt=inv_cnt, ypre=ypre, pool_mm=pool_mm)

    produce()
    da, dg = _col(p_ref, C_DA), _col(p_ref, C_DG)
    sig_dg = _sigmoid(dg)
    hd = da * sig_dg
    if kept_ref is None:
        ext_d[0:HALO] = jnp.where(first, 0.0, _col(ph_ref, C_DA) * _sigmoid(_col(ph_ref, C_DG)))
        ext_d[HALO:HALO + tile] = hd
        _make_residues(ext_d, res_d)
        conv_d = w["cc_dw_b"][...] + w["cc_dw_w"][0:1, :] * _rows_at(ext_d, res_d, HALO - (CONV_D - 1), tile)
        for j in range(1, CONV_D):
            conv_d = conv_d + w["cc_dw_w"][j:j + 1, :] * _rows_at(ext_d, res_d, HALO - (CONV_D - 1) + j, tile)
    else:
        conv_d = kept_ref[:, 0:D_G]
    r["kept"] = (conv_d, ypre)
    ln_d, xhat_d, rstd_d = _ln_fwd(conv_d, w["cc_ln_g"][...], w["cc_ln_b"][...])
    sig_ln = _sigmoid(ln_d)
    act_d = ln_d * sig_ln
    yd = _mm(act_d, w["cc_pw_w"][...])
    r.update(da=da, sig_dg=sig_dg, hd=hd, ln_d=ln_d, xhat_d=xhat_d, rstd_d=rstd_d, sig_ln=sig_ln, act_d=act_d)
    produce()

    u, v = _col(p_ref, C_U), _col(p_ref, C_V)
    ug, th_u = _gelu(u)
    vg, th_v = _gelu(v)
    vn, xhat_v, rstd_v = _ln_fwd(vg, w["sg_ln_g"][...], w["sg_ln_b"][...])
    tri = (lax.broadcasted_iota(jnp.int32, (CHUNK, CHUNK), 0)
           >= lax.broadcasted_iota(jnp.int32, (CHUNK, CHUNK), 1))
    wm = [jnp.where(tri, w["sg_w"][h], 0.0).astype(MM_DTYPE) for h in range(N_SUB)]
    lo = lax.broadcasted_iota(jnp.int32, (CHUNK, 2 * HEAD_DIM), 1) < HEAD_DIM
    chunks = []
    for c in range(tile // CHUNK):
        halves = []
        for hf in range(2):
            vh = vn[c * CHUNK:(c + 1) * CHUNK, hf * 128:(hf + 1) * 128]
            halves.append(_mm(wm[2 * hf], jnp.where(lo, vh, 0.0)) + _mm(wm[2 * hf + 1], jnp.where(lo, 0.0, vh)))
        chunks.append(jnp.concatenate(halves, axis=1) + w["sg_bias"][...])
    mixed = jnp.concatenate(chunks, axis=0)
    yb = ug * mixed
    r.update(u=u, v=v, ug=ug, th_u=th_u, th_v=th_v, vn=vn, xhat_v=xhat_v, rstd_v=rstd_v, wm=wm, lo=lo,
             mixed=mixed, tri=tri)

    produce()
    q = _col(p_ref, C_Q)
    ye = jnp.zeros((tile, D_G), F32)
    probs = []
    for h in range(N_SUB):
        s = _mm_nt(q, km_ref[h]) * ATT_SCALE
        e = jnp.exp(s - jnp.max(s, axis=-1, keepdims=True))
        p = e * (1.0 / jnp.sum(e, axis=-1, keepdims=True))
        probs.append(p)
        ye = ye + _mm(p, vm_ref[h])
    r.update(q=q, probs=probs)

    gate = p_ref[:, C_GATE:C_GATE + D_MIX]
    sig_gate = _sigmoid(gate)
    r.update(gate=gate, sig_gate=sig_gate, branch_out=(ya, yb, yc, yd, ye))
    return r


_BRANCH_REPL = ("sg_ln_g", "sg_ln_b", "sg_w", "sg_b", "pool_w", "pool_scale", "cc_dw_b", "cc_ln_g", "cc_ln_b")
_BRANCH_W_SCRATCH = (("conv_a", (CONV_A, D_G)), ("cc_dw_w", (CONV_D, D_G)), ("sg_bias", (CHUNK, D_G)),
                     ("pool_wbd", (D_G, D_G)), ("sgb8", (8, CHUNK)))


def _branch_weights(layer, nat, pw_ref, ca_ref, dw_ref, scr, init):
    @pl.when(init)
    def _():
        for p in range(N_DEV):
            scr["conv_a"][:, p * CONV_CH:(p + 1) * CONV_CH] = ca_ref[p, :, layer, :]
            scr["cc_dw_w"][:, p * CONV_CH:(p + 1) * CONV_CH] = dw_ref[p, :, layer, :]
        scr["sgb8"][...] = jnp.zeros((8, CHUNK), F32)
        scr["sgb8"][0:N_SUB] = nat["sg_b"][layer]
        sgb_t = scr["sgb8"][...].T
        head = _head_of_lane((CHUNK, D_G))
        bias = jnp.zeros((CHUNK, D_G), F32)
        for h in range(N_SUB):
            bias = jnp.where(head == h, sgb_t[:, h:h + 1], bias)
        scr["sg_bias"][...] = bias
        scr["pool_wbd"][...] = jnp.zeros((D_G, D_G), F32)
        for gi in range(N_SUB):
            sl = slice(gi * HEAD_DIM, (gi + 1) * HEAD_DIM)
            scr["pool_wbd"][sl, sl] = nat["pool_w"][layer, gi]

    w = {n: _row_view(nat[n], layer) for n in ("sg_ln_g", "sg_ln_b", "pool_scale", "cc_dw_b", "cc_ln_g", "cc_ln_b")}
    w.update(conv_a=scr["conv_a"], cc_dw_w=scr["cc_dw_w"], sg_bias=scr["sg_bias"], pool_wbd=scr["pool_wbd"],
             sg_w=nat["sg_w"].at[layer], cc_pw_w=pw_ref)
    return w


def _full_spec(a):
    nd = a.ndim
    return pl.BlockSpec(a.shape, lambda *_, _nd=nd: (0,) * _nd)


def _tie_specs(ties):
    return [pl.BlockSpec((8, 128), lambda *_: (0, 0)) for _ in ties]


def _params(*sem):
    return pltpu.CompilerParams(dimension_semantics=sem or None, vmem_limit_bytes=VMEM_LIMIT)


def _kv_project(mem, w_kv):
    def body(mem_ref, w_ref, km_ref, vm_ref):
        kv = _mm(mem_ref[...], w_ref[...])
        k, v = kv[:, :D_G], kv[:, D_G:]
        grp = _head_of_lane((MEM_LEN, D_G))
        for h in range(N_SUB):
            km_ref[h] = jnp.where(grp == h, k, 0.0).astype(km_ref.dtype)
            vm_ref[h] = jnp.where(grp == h, v, 0.0).astype(vm_ref.dtype)

    shp = jax.ShapeDtypeStruct((N_SUB, MEM_LEN, D_G), MM_DTYPE)
    return pl.pallas_call(body, out_shape=(shp, shp), name="kv_project", compiler_params=_params())(mem, w_kv)


def _layer_fwd_fused(x, wt_in, km, vm, layer, repl, pw, ca8, dw8, w_out, want_xn, ties=(), tile=FWD_TILE):
    s = x.shape[0]
    nat_arrays = [repl[n] for n in _BRANCH_REPL]
    n_nat, nt = len(nat_arrays), len(ties)

    def body(x_ref, wt_ref, km_ref, vm_ref, *rest):
        nat = dict(zip(_BRANCH_REPL, rest[:n_nat]))
        pw_ref, ca_ref, dw_ref, wo_ref, g_ref, b_ref = rest[n_nat:n_nat + 6]
        rest = rest[n_nat + 6 + nt:]
        p_ref, h_ref, z_ref, cd_ref = rest[:4]
        rest = rest[4:]
        if want_xn:
            xn_ref, rest = rest[0], rest[1:]
        ph_ref, ext_a, ext_c, ext_d, res_c, res_d = rest[:6]
        scr = dict(zip([n for n, _ in _BRANCH_W_SCRATCH], rest[6:]))
        i = pl.program_id(0)

        @pl.when(i == 0)
        def _():
            ph_ref[...] = jnp.zeros_like(ph_ref)

        xt = x_ref[...]
        xb = xt.astype(MM_DTYPE)
        segments = iter(PROJ_SEGMENTS)

        def produce():
            lo, hi = next(segments)
            p_ref[:, lo:hi] = _mm_nt(xb, wt_ref[lo:hi, :])

        w = _branch_weights(layer, nat, pw_ref, ca_ref, dw_ref, scr, i == 0)
        r = _branch_forward(p_ref, ph_ref, i == 0, i * tile, km_ref, vm_ref, w, ext_a, ext_c, ext_d, res_c, res_d,
                            tile, None, produce)
        ph_ref[...] = p_ref[tile - HALO:tile, :]
        cd_ref[:, 0:D_G], cd_ref[:, D_G:2 * D_G] = r["kept"]
        h = (jnp.concatenate(r["branch_out"], axis=1) * (r["gate"] * r["sig_gate"])).astype(h_ref.dtype)
        h_ref[...] = h
        z = ALPHA * xt + _mm(h, wo_ref[...])
        z_ref[...] = z
        if want_xn:
            xn_ref[...] = _ln_fwd(z, _row_view(g_ref, layer)[...], _row_view(b_ref, layer)[...])[0]

    row = lambda i: (i, 0)
    consts = [wt_in, km, vm] + nat_arrays + [pw, ca8, dw8, w_out, repl["ln_g"], repl["ln_b"]]
    act = jax.ShapeDtypeStruct((s, D_MODEL), F32)
    act_spec = pl.BlockSpec((tile, D_MODEL), row)
    res = pl.pallas_call(
        body, grid=(s // tile,),
        in_specs=[act_spec] + [_full_spec(a) for a in consts] + _tie_specs(ties),
        out_specs=(pl.BlockSpec((tile, D_IN), row), pl.BlockSpec((tile, D_MIX), row), act_spec,
                   pl.BlockSpec((tile, 2 * D_G), row)) + ((act_spec,) if want_xn else ()),
        out_shape=(jax.ShapeDtypeStruct((s, D_IN), F32), jax.ShapeDtypeStruct((s, D_MIX), MM_DTYPE), act,
                   jax.ShapeDtypeStruct((s, 2 * D_G), F32)) + ((act,) if want_xn else ()),
        scratch_shapes=[pltpu.VMEM((HALO, D_IN), F32)] + [pltpu.VMEM((HALO + tile, D_G), F32)] * 3
        + [_residue_scratch(tile)] * 2 + [pltpu.VMEM(shape, F32) for _, shape in _BRANCH_W_SCRATCH],
        name="layer_fwd", compiler_params=_params("arbitrary"),
    )(x, *consts, *ties)
    return tuple(res) if want_xn else tuple(res) + (None,)


_BRANCH_GRADS = (("g256", (8, D_G)), ("sg_w", (N_SUB, CHUNK, CHUNK)), ("sg_b", (8, CHUNK)),
                 ("pool_w", (N_SUB, HEAD_DIM, HEAD_DIM)), ("conv_a_w", (N_DEV, CONV_A, CONV_CH)),
                 ("cc_dw_w", (N_DEV, CONV_D, CONV_CH)), ("cc_pw_w", (D_G, D_G)),
                 ("dk", (N_SUB, MEM_LEN, D_G)), ("dv", (N_SUB, MEM_LEN, D_G)))
_BRANCH_ACC = (("conv_a", (CONV_A, D_G)), ("cc_dw_w", (CONV_D, D_G)), ("pool_wbd", (D_G, D_G)),
               ("sg_bias", (CHUNK, D_G)), ("sg_w", (N_SUB, CHUNK, CHUNK)))
_BRANCH_GRADS_NARROW = ("sg_w", "pool_w")


def _layer_bwd_fused(up, target, z, h, proj, kept, km, vm, layer, repl, w_out, pw, ca8, dw8, ties=(), tile=BWD_TILE):
    s = proj.shape[0]
    nt = s // tile
    hb = tile // HALO
    nat_arrays = [repl[n] for n in _BRANCH_REPL]
    n_nat, n_grads, n_acc, n_ties = len(nat_arrays), len(_BRANCH_GRADS), len(_BRANCH_ACC), len(ties)
    row_of = {n: k for k, n in enumerate(G256_ROWS)}
    from_loss = target is not None

    def body(o_ref, z_ref, h_ref, p_ref, ph_ref, cd_ref, km_ref, vm_ref, *rest):
        nat = dict(zip(_BRANCH_REPL, rest[:n_nat]))
        pw_ref, ca_ref, dw_ref, lng_ref, lnb_ref, wo_ref = rest[n_nat:n_nat + 6]
        rest = rest[n_nat + 6 + n_ties:]
        dz_ref, dp_ref, gw_ref, slab_ref = rest[:4]
        g = dict(zip([n for n, _ in _BRANCH_GRADS], rest[4:4 + n_grads]))
        rest = rest[4 + n_grads:]
        ext_a, rev_a, rev_c, rev_d, res_rc, res_rd, gw_acc, lacc = rest[:8]
        acc = dict(zip([n for n, _ in _BRANCH_ACC], rest[8:8 + n_acc]))
        scr = dict(zip([n for n, _ in _BRANCH_W_SCRATCH], rest[8 + n_acc:]))
        i = pl.program_id(0)
        t = nt - 1 - i

        @pl.when(i == 0)
        def _():
            for ref in list(g.values()) + list(acc.values()) + [rev_a, rev_c, rev_d, gw_acc, slab_ref, lacc]:
                ref[...] = jnp.zeros_like(ref)

        g_ln = _row_view(lng_ref, layer)[...]
        xn, xhat, rstd = _ln_fwd(z_ref[...], g_ln, _row_view(lnb_ref, layer)[...])
        if from_loss:
            err = xn - o_ref[...]
            lacc[...] += _rowsum(err * err)
            dxn = err * (1.0 / D_MODEL)
        else:
            dxn = o_ref[...]
        slab_ref[0:1, :] += _rowsum(dxn * xhat)
        slab_ref[1:2, :] += _rowsum(dxn)
        dz = _ln_bwd(dxn, xhat, rstd, g_ln)
        dz_ref[...] = dz
        dzb = dz.astype(MM_DTYPE)

        w = _branch_weights(layer, nat, pw_ref, ca_ref, dw_ref, scr, i == 0)
        r = _branch_forward(p_ref, ph_ref, t == 0, t * tile, km_ref, vm_ref, w, ext_a, None, None, None, None,
                            tile, cd_ref)

        def put(k, val, width=D_G):
            dp_ref[:, k:k + width] = val.astype(dp_ref.dtype)

        def add_row(name, val):
            k = row_of[name]
            g["g256"][k:k + 1, :] += val

        def push_rev(rev, val):
            head = rev[0:HALO]
            rev[tile:tile + HALO] = head
            rev[0:tile] = val

        gate, sig_gate = r["gate"], r["sig_gate"]

        def branch_grad(group):
            cols = slice(group * D_G, (group + 1) * D_G)
            dh_g = _mm_nt(dzb, wo_ref[cols, :])
            gate_g, sig_g = gate[:, cols], sig_gate[:, cols]
            put(C_GATE + group * D_G, dh_g * r["branch_out"][group] * (sig_g * (1.0 + gate_g * (1.0 - sig_g))))
            return dh_g * (gate_g * sig_g)

        dya = branch_grad(0)
        dyc = branch_grad(2)

        put(C_BA * D_G, dya * r["conv_a"])
        dconv_a = dya * r["ba"]
        push_rev(rev_a, dconv_a)
        dga = jnp.zeros((tile, D_G), F32)
        for k in range(CONV_A):
            ahead = rev_a[pl.ds(CONV_A - 1 - k, tile), :]
            dga = dga + w["conv_a"][k:k + 1, :] * ahead
            acc["conv_a"][k:k + 1, :] += _rowsum(r["g_a"] * ahead)
        put(C_CA * D_G, dga * r["xa"])
        put(C_XA * D_G, dga * r["ca"])

        dyd = branch_grad(3)
        add_row("pool_scale", _rowsum(dyc * r["pool_mm"]))
        dmm = dyc * w["pool_scale"][...]
        acc["pool_wbd"][...] += _mm_tn(r["ypre"], dmm)
        dypre = _mm_nt(dmm, w["pool_wbd"][...])
        dws = dypre * r["inv_cnt"]
        push_rev(rev_c, dws)
        _make_residues(rev_c, res_rc)
        run = dws
        sums = {}
        for k in range(1, POOL_WINDOWS[-1]):
            run = run + _rows_at(rev_c, res_rc, k, tile)
            if k + 1 in POOL_WINDOWS:
                sums[k + 1] = run
        put(C_XC * D_G, _pool_select(r["lane_grp"], sums[2], sums[4], sums[8], sums[16]) - dypre)

        dyb = branch_grad(1)
        gw_acc[...] += _mm_tn(h_ref[...], dzb)
        g["cc_pw_w"][...] += _mm_tn(r["act_d"], dyd)
        dact = _mm_nt(dyd, w["cc_pw_w"][...])
        sig_ln, ln_d = r["sig_ln"], r["ln_d"]
        dln = dact * (sig_ln * (1.0 + ln_d * (1.0 - sig_ln)))
        add_row("cc_ln_g", _rowsum(dln * r["xhat_d"]))
        add_row("cc_ln_b", _rowsum(dln))
        dconv_d = _ln_bwd(dln, r["xhat_d"], r["rstd_d"], w["cc_ln_g"][...])
        add_row("cc_dw_b", _rowsum(dconv_d))
        push_rev(rev_d, dconv_d)
        _make_residues(rev_d, res_rd)
        dhd = jnp.zeros((tile, D_G), F32)
        for j in range(CONV_D):
            ahead = _rows_at(rev_d, res_rd, CONV_D - 1 - j, tile)
            dhd = dhd + w["cc_dw_w"][j:j + 1, :] * ahead
            acc["cc_dw_w"][j:j + 1, :] += _rowsum(r["hd"] * ahead)
        sig_dg = r["sig_dg"]
        put(C_DA * D_G, dhd * sig_dg)
        put(C_DG * D_G, dhd * r["da"] * sig_dg * (1.0 - sig_dg))

        dye = branch_grad(4)
        dug = dyb * r["mixed"]
        dmixed = dyb * r["ug"]
        wm, lo, vn = r["wm"], r["lo"], r["vn"]
        dvn_chunks = []
        for c in range(tile // CHUNK):
            rows = slice(c * CHUNK, (c + 1) * CHUNK)
            acc["sg_bias"][...] += dmixed[rows, :]
            halves = []
            for hf in range(2):
                cols = slice(hf * 128, (hf + 1) * 128)
                dm = dmixed[rows, cols]
                dm_a, dm_b = jnp.where(lo, dm, 0.0), jnp.where(lo, 0.0, dm)
                vh = vn[rows, cols]
                acc["sg_w"][2 * hf] += _mm_nt(dm_a, vh)
                acc["sg_w"][2 * hf + 1] += _mm_nt(dm_b, vh)
                halves.append(_mm_tn(wm[2 * hf], dm_a) + _mm_tn(wm[2 * hf + 1], dm_b))
            dvn_chunks.append(jnp.concatenate(halves, axis=1))
        dvn = jnp.concatenate(dvn_chunks, axis=0)
        add_row("sg_ln_g", _rowsum(dvn * r["xhat_v"]))
        add_row("sg_ln_b", _rowsum(dvn))
        dvg = _ln_bwd(dvn, r["xhat_v"], r["rstd_v"], w["sg_ln_g"][...])
        put(C_V * D_G, dvg * _dgelu(r["v"], r["th_v"]))
        put(C_U * D_G, dug * _dgelu(r["u"], r["th_u"]))

        q = r["q"]
        dq = jnp.zeros((tile, D_G), F32)
        for h in range(N_SUB):
            p = r["probs"][h]
            dp = _mm_nt(dye, vm_ref[h])
            g["dv"][h] += _mm_tn(p, dye)
            ds = p * (dp - jnp.sum(dp * p, axis=-1, keepdims=True)) * ATT_SCALE
            dq = dq + _mm(ds, km_ref[h])
            g["dk"][h] += _mm_tn(ds, q)
        put(C_Q * D_G, dq)

        @pl.when(i == nt - 1)
        def _():
            for h in range(N_SUB):
                g["sg_w"][h] = jnp.where(r["tri"], acc["sg_w"][h], 0.0).astype(g["sg_w"].dtype)
            lane_head = _head_of_lane((CHUNK, D_G))
            col_of = lax.broadcasted_iota(jnp.int32, (CHUNK, 8), 1)
            ba = acc["sg_bias"][...]
            sgb_t = jnp.zeros((CHUNK, 8), F32)
            for h in range(N_SUB):
                col = jnp.sum(jnp.where(lane_head == h, ba, 0.0), axis=-1, keepdims=True)
                sgb_t = jnp.where(col_of == h, col, sgb_t)
            g["sg_b"][...] = sgb_t.T
            wbd = acc["pool_wbd"][...]
            for gi in range(N_SUB):
                sl = slice(gi * HEAD_DIM, (gi + 1) * HEAD_DIM)
                g["pool_w"][gi] = wbd[sl, sl].astype(g["pool_w"].dtype)
            ca, dw = acc["conv_a"][...], acc["cc_dw_w"][...]
            for p in range(N_DEV):
                g["conv_a_w"][p] = ca[:, p * CONV_CH:(p + 1) * CONV_CH]
                g["cc_dw_w"][p] = dw[:, p * CONV_CH:(p + 1) * CONV_CH]
            gw_ref[...] = gw_acc[...].astype(gw_ref.dtype)
            if from_loss:
                total = jnp.sum(lacc[...], axis=-1, keepdims=True) * (0.5 / D_MODEL)
                slab_ref[LOSS_ROW:LOSS_ROW + 1, :] = jnp.broadcast_to(total, (1, D_MODEL))

    rev = lambda i: (nt - 1 - i, 0)
    fixed = lambda i: (0, 0)
    act_spec = pl.BlockSpec((tile, D_MODEL), rev)
    grad_specs = tuple(pl.BlockSpec(shape, lambda i, _nd=len(shape): (0,) * _nd) for _, shape in _BRANCH_GRADS)
    grad_shapes = tuple(jax.ShapeDtypeStruct(shape, GRAD_DTYPE if n in _BRANCH_GRADS_NARROW else F32)
                        for n, shape in _BRANCH_GRADS)
    consts = [km, vm] + nat_arrays + [pw, ca8, dw8, repl["ln_g"], repl["ln_b"], w_out]
    outs = pl.pallas_call(
        body, grid=(nt,),
        in_specs=[act_spec, act_spec, pl.BlockSpec((tile, D_MIX), rev), pl.BlockSpec((tile, D_IN), rev),
                  pl.BlockSpec((HALO, D_IN), lambda i: (jnp.maximum((nt - 1 - i) * hb - 1, 0), 0)),
                  pl.BlockSpec((tile, 2 * D_G), rev)]
        + [_full_spec(a) for a in consts] + _tie_specs(ties),
        out_specs=(act_spec, pl.BlockSpec((tile, D_IN), rev), pl.BlockSpec((D_MIX, D_MODEL), fixed),
                   pl.BlockSpec((8, D_MODEL), fixed)) + grad_specs,
        out_shape=(jax.ShapeDtypeStruct((s, D_MODEL), F32), jax.ShapeDtypeStruct((s, D_IN), MM_DTYPE),
                   jax.ShapeDtypeStruct((D_MIX, D_MODEL), GRAD_DTYPE), jax.ShapeDtypeStruct((8, D_MODEL), F32))
        + grad_shapes,
        scratch_shapes=[pltpu.VMEM((HALO + tile, D_G), F32)] * 4 + [_residue_scratch(tile)] * 2
        + [pltpu.VMEM((D_MIX, D_MODEL), F32), pltpu.VMEM((1, D_MODEL), F32)]
        + [pltpu.VMEM(shape, F32) for _, shape in _BRANCH_ACC + _BRANCH_W_SCRATCH],
        name="layer_bwd_loss" if from_loss else "layer_bwd", compiler_params=_params("arbitrary"),
    )(target if from_loss else up, z, h, proj, proj, kept, *consts, *ties)
    return outs[0], outs[1], outs[2], outs[3], dict(zip([n for n, _ in _BRANCH_GRADS], outs[4:]))


def _dx_matmul(dproj, wt_in, dz, ties=(), tm=512):
    s = dproj.shape[0]

    def body(dp_ref, w_ref, dz_ref, *rest):
        o_ref = rest[len(ties)]
        o_ref[...] = _mm(dp_ref[...], w_ref[...]) + ALPHA * dz_ref[...]

    row = lambda i: (i, 0)
    return pl.pallas_call(
        body, grid=(s // tm,),
        in_specs=[pl.BlockSpec((tm, D_IN), row), _full_spec(wt_in), pl.BlockSpec((tm, D_MODEL), row)]
        + _tie_specs(ties),
        out_specs=pl.BlockSpec((tm, D_MODEL), row),
        out_shape=jax.ShapeDtypeStruct((s, D_MODEL), F32), name="dx_mm", compiler_params=_params("arbitrary"),
    )(dproj, wt_in, dz, *ties)


def _dw_in_matmul(x, dproj, ties=(), tk=512):
    s = x.shape[0]
    nk = s // tk
    blk = 2 * W_IN_COLS

    def body(x_ref, dp_ref, *rest):
        o_ref, acc = rest[len(ties):]
        k = pl.program_id(0)

        @pl.when(k == 0)
        def _():
            acc[...] = jnp.zeros_like(acc)

        xb = x_ref[...].astype(MM_DTYPE)
        for j in range(D_IN // blk):
            acc[j * blk:(j + 1) * blk, :] += _mm_tn(dp_ref[:, j * blk:(j + 1) * blk], xb)

        @pl.when(k == nk - 1)
        def _():
            o_ref[...] = acc[...].astype(o_ref.dtype)

    return pl.pallas_call(
        body, grid=(nk,),
        in_specs=[pl.BlockSpec((tk, D_MODEL), lambda k: (k, 0)), pl.BlockSpec((tk, D_IN), lambda k: (k, 0))]
        + _tie_specs(ties),
        out_specs=pl.BlockSpec((D_IN, D_MODEL), lambda k: (0, 0)),
        out_shape=jax.ShapeDtypeStruct((D_IN, D_MODEL), GRAD_DTYPE),
        scratch_shapes=[pltpu.VMEM((D_IN, D_MODEL), F32)], name="dw_in_mm", compiler_params=_params("arbitrary"),
    )(x, dproj, *ties)


def _in_bwd(x, dproj, wt_in, dz, ties=(), tm=512):
    s = x.shape[0]
    n_steps = s // tm

    assert wt_in.dtype == GRAD_DTYPE
    blk = 2 * W_IN_COLS

    def body(x_ref, dp_ref, w_hbm, dz_ref, *rest):
        o_ref, gw_hbm, w_vmem, acc, sem = rest[len(ties):]
        i = pl.program_id(0)

        @pl.when(i == 0)
        def _():
            fetch = pltpu.make_async_copy(w_hbm, w_vmem, sem)
            fetch.start()
            acc[...] = jnp.zeros_like(acc)
            fetch.wait()

        o_ref[...] = _mm(dp_ref[...], w_vmem[...]) + ALPHA * dz_ref[...]
        xb = x_ref[...].astype(MM_DTYPE)
        for j in range(D_IN // blk):
            acc[j * blk:(j + 1) * blk, :] += _mm_tn(dp_ref[:, j * blk:(j + 1) * blk], xb)

        @pl.when(i == n_steps - 1)
        def _():
            w_vmem[...] = acc[...].astype(w_vmem.dtype)
            emit = pltpu.make_async_copy(w_vmem, gw_hbm, sem)
            emit.start()
            emit.wait()

    row = lambda i: (i, 0)
    any_spec = pl.BlockSpec(memory_space=pl.ANY)
    return pl.pallas_call(
        body, grid=(n_steps,),
        in_specs=[pl.BlockSpec((tm, D_MODEL), row), pl.BlockSpec((tm, D_IN), row), any_spec,
                  pl.BlockSpec((tm, D_MODEL), row)] + _tie_specs(ties),
        out_specs=(pl.BlockSpec((tm, D_MODEL), row), any_spec),
        out_shape=(jax.ShapeDtypeStruct((s, D_MODEL), F32), jax.ShapeDtypeStruct((D_IN, D_MODEL), GRAD_DTYPE)),
        scratch_shapes=[pltpu.VMEM((D_IN, D_MODEL), wt_in.dtype), pltpu.VMEM((D_IN, D_MODEL), F32),
                        pltpu.SemaphoreType.DMA],
        name="in_bwd", compiler_params=_params("arbitrary"),
    )(x, dproj, wt_in, dz, *ties)


def _kv_bwd(mem, dk, dv):
    def body(mem_ref, dk_ref, dv_ref, o_ref):
        grp = _head_of_lane((MEM_LEN, D_G))
        dk_sum = jnp.zeros((MEM_LEN, D_G), F32)
        dv_sum = jnp.zeros((MEM_LEN, D_G), F32)
        for h in range(N_SUB):
            dk_sum = dk_sum + jnp.where(grp == h, dk_ref[h], 0.0)
            dv_sum = dv_sum + jnp.where(grp == h, dv_ref[h], 0.0)
        o_ref[...] = _mm_tn(mem_ref[...], jnp.concatenate([dk_sum, dv_sum], axis=1)).astype(o_ref.dtype)

    return pl.pallas_call(body, out_shape=jax.ShapeDtypeStruct((D_MODEL, 2 * D_G), GRAD_DTYPE), name="kv_bwd",
                          compiler_params=_params())(mem, dk, dv)


def _layer_fwd(x, mem, layer, repl, gw, ties=()):
    km, vm = _kv_project(mem, gw["w_kv"])
    proj, h, z, kept, xn = _layer_fwd_fused(x, gw["wt_in"], km, vm, layer, repl, gw["pw"], gw["ca8"], gw["dw8"],
                                            gw["w_out"], layer < DEPTH - 1, ties)
    return xn, (x, proj, h, z, km, vm, kept)


def _layer_bwd_a(up, target, mem, layer, repl, gw, saved, ties=()):
    x_in, proj, h, z, km, vm, kept = saved
    dz, dproj, g_w_out, g1024, bg = _layer_bwd_fused(up, target, z, h, proj, kept, km, vm, layer, repl, gw["w_out"],
                                                     gw["pw"], gw["ca8"], gw["dw8"], ties)
    grads = {n: bg[n] for n in ("g256", "sg_w", "sg_b", "pool_w", "conv_a_w", "cc_dw_w")}
    grads.update(w_out=g_w_out.reshape(N_DEV, D_MIX // N_DEV, D_MODEL), g1024=g1024,
                 w_kv=_kv_bwd(mem, bg["dk"], bg["dv"]).reshape(N_DEV, D_MODEL // N_DEV, 2 * D_G),
                 cc_pw_w=bg["cc_pw_w"].reshape(N_DEV, CONV_CH, D_G))
    return dz, dproj, grads


def _landing_shapes(items):
    out = []
    for a, scatter, pick in items:
        shape = a.shape if scatter else (N_DEV,) + (a.shape if pick is None else a.shape[1:])
        out.append(jax.ShapeDtypeStruct(shape, a.dtype))
    return tuple(out)


def _exchange_sems(n):
    return [pltpu.SemaphoreType.DMA(((N_DEV - 1) * n,)), pltpu.SemaphoreType.DMA(((N_DEV - 1) * n,)),
            pltpu.SemaphoreType.DMA((n,))]


def _exchange_copies(modes, ins, outs, send_sems, recv_sems, local_sems):
    n = len(ins)
    x, y, c = lax.axis_index("x"), lax.axis_index("y"), lax.axis_index("c")
    me = 4 * x + 2 * y + c

    def src_of(a, dest):
        scatter, pick = modes[a]
        if scatter:
            return ins[a].at[dest]
        return ins[a] if pick is None else ins[a].at[pick]

    local = [pltpu.make_async_copy(src_of(a, me), outs[a].at[me], local_sems.at[a]) for a in range(n)]
    sends, recvs = [], []
    for k in range(1, N_DEV):
        px = 1 - x if k & 4 else x
        py = 1 - y if k & 2 else y
        pc = 1 - c if k & 1 else c
        peer = 4 * px + 2 * py + pc
        for a in range(n):
            sems = dict(send_sem=send_sems.at[(k - 1) * n + a], recv_sem=recv_sems.at[(k - 1) * n + a],
                        device_id=(px, py, pc), device_id_type=pl.DeviceIdType.MESH)
            sends.append(pltpu.make_async_remote_copy(src_ref=src_of(a, peer), dst_ref=outs[a].at[me], **sems))
            recvs.append(pltpu.make_async_remote_copy(src_ref=src_of(a, peer), dst_ref=outs[a].at[peer], **sems))
    return local, sends, recvs


def _gather_two_level(items, name):
    n = len(items)
    assert not any(scatter for _, scatter, _ in items)
    picks = [pick for _, _, pick in items]

    def body(*refs):
        ins, outs = refs[:n], refs[n:2 * n]
        send_sems, recv_sems, local_sems = refs[2 * n:]
        x, y, c = lax.axis_index("x"), lax.axis_index("y"), lax.axis_index("c")
        sib = 1 - c
        chips = [(1 - x, y), (x, 1 - y), (1 - x, 1 - y)]

        def slot(a, px, py, pc):
            return outs[a].at[4 * px + 2 * py + pc]

        def copy(k, a, src, block, to):
            return pltpu.make_async_remote_copy(
                src_ref=src, dst_ref=slot(a, *block), send_sem=send_sems.at[k * n + a],
                recv_sem=recv_sems.at[k * n + a], device_id=to, device_id_type=pl.DeviceIdType.MESH)

        own = [ins[a] if picks[a] is None else ins[a].at[picks[a]] for a in range(n)]
        local = [pltpu.make_async_copy(own[a], slot(a, x, y, c), local_sems.at[a]) for a in range(n)]
        first = [copy(0, a, own[a], (x, y, c), (x, y, sib)) for a in range(n)]
        first += [copy(1 + j, a, own[a], (x, y, c), (*chip, c)) for j, chip in enumerate(chips[:2]) for a in range(n)]
        for cp in local + first:
            cp.start()

        def pass_on(j, a):
            chip = chips[j]
            copy(1 + j, a, own[a], (*chip, c), (x, y, c)).wait_recv()
            fwd = copy(4 + j, a, slot(a, *chip, c), (*chip, c), (x, y, sib))
            fwd.start()
            return fwd

        passed = [pass_on(j, a) for j in range(2) for a in range(n)]
        south = c == 0
        via = tuple(jnp.where(south, p, q) for p, q in zip(chips[0], chips[1]))
        blk = tuple(jnp.where(south, q, p) for p, q in zip(chips[0], chips[1]))
        relayed = [copy(3, a, slot(a, *blk, c), (*blk, c), (*via, c)) for a in range(n)]
        for cp in relayed:
            cp.start()
        passed += [pass_on(2, a) for a in range(n)]
        first += relayed
        for a in range(n):
            copy(0, a, own[a], (x, y, sib), (x, y, c)).wait_recv()
        for j, chip in enumerate(chips):
            for a in range(n):
                copy(4 + j, a, own[a], (*chip, sib), (x, y, c)).wait_recv()
        for cp in first + passed:
            cp.wait_send()
        for cp in local:
            cp.wait()

    any_spec = pl.BlockSpec(memory_space=pl.ANY)
    return pl.pallas_call(
        body, in_specs=[any_spec] * n, out_specs=(any_spec,) * n, out_shape=_landing_shapes(items),
        scratch_shapes=[pltpu.SemaphoreType.DMA((7 * n,)), pltpu.SemaphoreType.DMA((7 * n,)),
                        pltpu.SemaphoreType.DMA((n,))],
        name=name,
    )(*[a for a, _, _ in items])


_HBM_SPEC = pl.BlockSpec(memory_space=pltpu.HBM)
_SEM_SPEC = pl.BlockSpec(memory_space=pltpu.SEMAPHORE)
_SPLIT_PARAMS = pltpu.CompilerParams(has_side_effects=pltpu.SideEffectType.DATAFLOW_SIDE_EFFECTING)


def _split_start(srcs, lands, plan, sem_shapes, name):
    n_src, n_land = len(srcs), len(lands)
    n_buf = n_src + n_land
    bufs = [pltpu.with_memory_space_constraint(a, pltpu.HBM) for a in list(srcs) + list(lands)]

    def body(*refs):
        local, sends, _ = plan(refs[:n_src], refs[n_src:n_buf], *refs[n_buf:n_buf + 3])
        for cp in local + sends:
            cp.start()
        token = refs[-1]
        token[...] = jnp.zeros_like(token)

    res = pl.pallas_call(
        body, name=name, in_specs=[_HBM_SPEC] * n_buf,
        out_shape=tuple(sem_shapes) + tuple(pltpu.HBM(a.shape, a.dtype) for a in bufs)
        + (jax.ShapeDtypeStruct((8, 128), F32),),
        out_specs=(_SEM_SPEC,) * 3 + (_HBM_SPEC,) * n_buf + (pl.BlockSpec(memory_space=pltpu.VMEM),),
        input_output_aliases={i: 3 + i for i in range(n_buf)}, compiler_params=_SPLIT_PARAMS,
    )(*bufs)
    return dict(sems=res[:3], srcs=res[3:3 + n_src], lands=res[3 + n_src:3 + n_buf], token=res[-1], plan=plan)


def _split_wait(ticket, after, name):
    n_src, n_land = len(ticket["srcs"]), len(ticket["lands"])
    n_buf = n_src + n_land
    plan = ticket["plan"]

    def body(*refs):
        local, sends, recvs = plan(refs[:n_src], refs[n_src:n_buf], *refs[n_buf:n_buf + 3])
        for cp in recvs:
            cp.wait_recv()
        for cp in sends:
            cp.wait_send()
        for cp in local:
            cp.wait()

    bufs = list(ticket["srcs"]) + list(ticket["lands"])
    res = pl.pallas_call(
        body, name=name, in_specs=[_HBM_SPEC] * n_buf + [_SEM_SPEC] * 3 + [pl.BlockSpec(memory_space=pl.ANY)],
        out_shape=tuple(pltpu.HBM(a.shape, a.dtype) for a in bufs), out_specs=(_HBM_SPEC,) * n_buf,
        input_output_aliases={i: i for i in range(n_buf)}, compiler_params=_SPLIT_PARAMS,
    )(*bufs, *ticket["sems"], after)
    return res[n_src:]


def _empty_landings(items):
    return [lax.empty(s.shape, s.dtype) for s in _landing_shapes(items)]


def _exchange_start(items, name):
    modes = [(scatter, pick) for _, scatter, pick in items]
    plan = lambda ins, outs, *sems: _exchange_copies(modes, ins, outs, *sems)
    return _split_start([a for a, _, _ in items], _empty_landings(items), plan, _exchange_sems(len(items)), name)


def _two_level_plans(picks):
    n = len(picks)

    def place():
        x, y, c = lax.axis_index("x"), lax.axis_index("y"), lax.axis_index("c")
        return x, y, c, 1 - c, [(1 - x, y), (x, 1 - y), (1 - x, 1 - y)]

    def copy(outs, send_sems, recv_sems, k, a, src, block, to):
        px, py, pc = block
        return pltpu.make_async_remote_copy(
            src_ref=src, dst_ref=outs[a].at[4 * px + 2 * py + pc], send_sem=send_sems.at[k * n + a],
            recv_sem=recv_sems.at[k * n + a], device_id=to, device_id_type=pl.DeviceIdType.MESH)

    def between_chips(ins, outs, send_sems, recv_sems, local_sems):
        x, y, c, sib, chips = place()
        own = [ins[a] if picks[a] is None else ins[a].at[picks[a]] for a in range(n)]
        mk = lambda *args: copy(outs, send_sems, recv_sems, *args)
        local = [pltpu.make_async_copy(own[a], outs[a].at[4 * x + 2 * y + c], local_sems.at[a]) for a in range(n)]
        sends = [mk(0, a, own[a], (x, y, c), (x, y, sib)) for a in range(n)]
        sends += [mk(1 + j, a, own[a], (x, y, c), (*chip, c)) for j, chip in enumerate(chips) for a in range(n)]
        recvs = [mk(0, a, own[a], (x, y, sib), (x, y, c)) for a in range(n)]
        recvs += [mk(1 + j, a, own[a], (*chip, c), (x, y, c)) for j, chip in enumerate(chips) for a in range(n)]
        return local, sends, recvs

    def within_chip(ins, outs, send_sems, recv_sems, local_sems):
        x, y, c, sib, chips = place()
        mk = lambda *args: copy(outs, send_sems, recv_sems, *args)
        slot = lambda a, px, py, pc: outs[a].at[4 * px + 2 * py + pc]
        sends = [mk(j, a, slot(a, *chip, c), (*chip, c), (x, y, sib)) for j, chip in enumerate(chips)
                 for a in range(n)]
        recvs = [mk(j, a, slot(a, *chip, c), (*chip, sib), (x, y, c)) for j, chip in enumerate(chips)
                 for a in range(n)]
        return [], sends, recvs

    sems = lambda k: [pltpu.SemaphoreType.DMA((k * n,)), pltpu.SemaphoreType.DMA((k * n,)),
                      pltpu.SemaphoreType.DMA((n,))]
    return between_chips, sems(4), within_chip, sems(3)


def _adam_math(g, w, m, v):
    m_new = ADAM_B1 * m + (1.0 - ADAM_B1) * g
    v_new = ADAM_B2 * v + (1.0 - ADAM_B2) * (g * g)
    m_hat = m_new / (1.0 - ADAM_B1 ** ADAM_STEP)
    v_hat = v_new / (1.0 - ADAM_B2 ** ADAM_STEP)
    return -ADAM_LR * (m_hat / (jnp.sqrt(v_hat) + ADAM_EPS) + ADAM_WD * w), m_new, v_new


def _adamw_big(parts, w, m, v, layer, prev, name, tr):
    depth, rows, cols = w.shape

    def body(p_ref, w_ref, m_ref, v_ref, *rest):
        g_out, d_out, m_out, v_out = rest[len(prev):]
        g = p_ref[0].astype(F32)
        for q in range(1, N_DEV):
            g = g + p_ref[q].astype(F32)
        d, m_new, v_new = _adam_math(g, w_ref[...], m_ref[...], v_ref[...])
        g_out[...] = g
        d_out[...] = d
        m_out[...] = m_new
        v_out[...] = v_new

    blk = pl.BlockSpec((None, tr, cols), lambda i: (layer, i, 0))
    shp = jax.ShapeDtypeStruct((depth, rows, cols), F32)
    return pl.pallas_call(
        body, grid=(rows // tr,),
        in_specs=[pl.BlockSpec((N_DEV, tr, cols), lambda i: (0, i, 0)), blk, blk, blk]
        + [pl.BlockSpec(memory_space=pl.ANY)] * len(prev),
        out_specs=(blk,) * 4, out_shape=(shp,) * 4,
        input_output_aliases={4 + j: j for j in range(len(prev))},
        name=name, compiler_params=_params("arbitrary"),
    )(parts, w, m, v, *prev)


_SMALL_TENSORS = (("conv_a_w", "conv_a_w", None), ("cc_dw_w", "cc_dw_w", None), ("cc_pw_w", "cc_pw_w", None),
                  ("sg_w", "sg_w", None), ("pool_w", "pool_w", None), ("sg_b", "sg_b", None)) \
    + tuple((n, "g256", k) for k, n in enumerate(G256_ROWS)) + tuple((n, "g1024", k) for k, n in enumerate(G1024_ROWS))
_SMALL_LANDINGS = ("conv_a_w", "cc_dw_w", "cc_pw_w", "sg_w", "pool_w", "sg_b", "g256", "g1024")
_TAPS_FIRST = ("conv_a_w", "cc_dw_w")


def _adamw_small(landings, wts, mom, var):
    names = [n for n, _, _ in _SMALL_TENSORS]
    n_land = DEPTH * len(_SMALL_LANDINGS)
    n_t = len(names)

    def body(*refs):
        land = [dict(zip(_SMALL_LANDINGS, refs[l * len(_SMALL_LANDINGS):(l + 1) * len(_SMALL_LANDINGS)]))
                for l in range(DEPTH)]
        w_refs = dict(zip(names, refs[n_land:n_land + n_t]))
        m_refs = dict(zip(names, refs[n_land + n_t:n_land + 2 * n_t]))
        v_refs = dict(zip(names, refs[n_land + 2 * n_t:n_land + 3 * n_t]))
        outs = refs[n_land + 3 * n_t:]
        out_refs = {n: outs[4 * k:4 * k + 4] for k, n in enumerate(names)}
        loss_ref = outs[4 * n_t]
        for name, key, row in _SMALL_TENSORS:
            for l in range(DEPTH):
                src = land[l][key]
                if row is not None:
                    part = lambda q: src[q, row:row + 1, :]
                    at = (slice(l, l + 1),)
                elif name == "sg_b":
                    part = lambda q: src[q, 0:N_SUB, :]
                    at = (l,)
                elif name in _TAPS_FIRST:
                    part = lambda q: src[q]
                    at = (slice(None), l)
                else:
                    part = lambda q: src[q]
                    at = (l,)
                g = part(0).astype(F32)
                for q in range(1, N_DEV):
                    g = g + part(q).astype(F32)
                d, m_new, v_new = _adam_math(g, w_refs[name][at], m_refs[name][at], v_refs[name][at])
                for ref, val in zip(out_refs[name], (g, d, m_new, v_new)):
                    ref[at] = val
        src = land[DEPTH - 1]["g1024"]
        loss = src[0, LOSS_ROW:LOSS_ROW + 1, 0:128]
        for q in range(1, N_DEV):
            loss = loss + src[q, LOSS_ROW:LOSS_ROW + 1, 0:128]
        loss_ref[...] = loss

    ins = [landings[l][k] for l in range(DEPTH) for k in _SMALL_LANDINGS] \
        + [src[n] for src in (wts, mom, var) for n in names]
    out_shape = tuple(jax.ShapeDtypeStruct(wts[n].shape, F32) for n in names for _ in range(4)) \
        + (jax.ShapeDtypeStruct((1, 128), F32),)
    res = pl.pallas_call(body, out_shape=out_shape, name="adamw_small", compiler_params=_params())(*ins)
    return {n: res[4 * k:4 * k + 4] for k, n in enumerate(names)}, res[4 * n_t]


_BIG = (("w_in", 224), ("w_out", 80), ("w_kv", 64))
_GRAD_ITEMS_EARLY = ("w_out", "w_kv", "cc_pw_w", "conv_a_w", "cc_dw_w")
_GRAD_ITEMS_REPL = ("g256", "sg_w", "sg_b", "pool_w", "g1024")


def _grad_items(grads):
    items = [(grads[n], True, None) for n in _GRAD_ITEMS_EARLY]
    return items + [(grads[n], False, None) for n in _GRAD_ITEMS_REPL]


def _landed(parts):
    return dict(zip(_GRAD_ITEMS_EARLY + _GRAD_ITEMS_REPL, parts))


def _gathered_weights(wt_in8, w_kv8, w_out8, pw8, ca8, dw8):
    return dict(wt_in=wt_in8.reshape(D_IN, D_MODEL), w_kv=w_kv8.reshape(D_MODEL, 2 * D_G),
                w_out=w_out8.reshape(D_MIX, D_MODEL), pw=pw8.reshape(D_G, D_G), ca8=ca8, dw8=dw8)


def kernel(x, mem, w_in, conv_a_w, sg_ln_g, sg_ln_b, sg_w, sg_b, pool_w, pool_scale, cc_dw_w, cc_dw_b, cc_ln_g, cc_ln_b, cc_pw_w, w_kv, w_out, ln_g, ln_b, loss_target, m_w_in, m_conv_a_w, m_sg_ln_g, m_sg_ln_b, m_sg_w, m_sg_b, m_pool_w, m_pool_scale, m_cc_dw_w, m_cc_dw_b, m_cc_ln_g, m_cc_ln_b, m_cc_pw_w, m_w_kv, m_w_out, m_ln_g, m_ln_b, v_w_in, v_conv_a_w, v_sg_ln_g, v_sg_ln_b, v_sg_w, v_sg_b, v_pool_w, v_pool_scale, v_cc_dw_w, v_cc_dw_b, v_cc_ln_g, v_cc_ln_b, v_cc_pw_w, v_w_kv, v_w_out, v_ln_g, v_ln_b):
    names = ("w_in", "conv_a_w", "sg_ln_g", "sg_ln_b", "sg_w", "sg_b", "pool_w", "pool_scale", "cc_dw_w", "cc_dw_b",
             "cc_ln_g", "cc_ln_b", "cc_pw_w", "w_kv", "w_out", "ln_g", "ln_b")
    wts = dict(zip(names, (w_in, conv_a_w, sg_ln_g, sg_ln_b, sg_w, sg_b, pool_w, pool_scale, cc_dw_w, cc_dw_b,
                           cc_ln_g, cc_ln_b, cc_pw_w, w_kv, w_out, ln_g, ln_b)))
    mom = dict(zip(names, (m_w_in, m_conv_a_w, m_sg_ln_g, m_sg_ln_b, m_sg_w, m_sg_b, m_pool_w, m_pool_scale,
                           m_cc_dw_w, m_cc_dw_b, m_cc_ln_g, m_cc_ln_b, m_cc_pw_w, m_w_kv, m_w_out, m_ln_g, m_ln_b)))
    var = dict(zip(names, (v_w_in, v_conv_a_w, v_sg_ln_g, v_sg_ln_b, v_sg_w, v_sg_b, v_pool_w, v_pool_scale,
                           v_cc_dw_w, v_cc_dw_b, v_cc_ln_g, v_cc_ln_b, v_cc_pw_w, v_w_kv, v_w_out, v_ln_g, v_ln_b)))
    repl = wts
    xs, mems, tgt = x[0], mem[0], loss_target[0]
    turned = {"w_in": (0, 2, 1), "conv_a_w": (1, 0, 2), "cc_dw_w": (1, 0, 2)}
    wts, mom, var = [{n: (jnp.transpose(a, turned[n]) if n in turned else a) for n, a in src.items()}
                     for src in (wts, mom, var)]
    wb = {n: wts[n].astype(MM_DTYPE) for n in ("w_in", "w_kv", "w_out", "cc_pw_w")}

    wt8_0, wkv8_0, wo8_0, pw8_0, ca8, dw8 = _gather_two_level(
        [(wb["w_in"], False, 0), (wb["w_kv"], False, 0), (wb["w_out"], False, 0), (wb["cc_pw_w"], False, 0),
         (wts["conv_a_w"], False, None), (wts["cc_dw_w"], False, None)], "gather_weights_0")
    gw0 = _gathered_weights(wt8_0, wkv8_0, wo8_0, pw8_0, ca8, dw8)
    items_1 = [(wb[n], False, 1) for n in ("w_in", "w_kv", "w_out", "cc_pw_w")]
    between_chips, sems_a, within_chip, sems_b = _two_level_plans([1] * len(items_1))
    chips_1 = _split_start([a for a, _, _ in items_1], _empty_landings(items_1), between_chips, sems_a,
                           "gather_weights_1a_start")
    x1, saved0 = _layer_fwd(xs, mems, 0, repl, gw0, ties=(chips_1["token"],))
    core_1 = _split_start([], _split_wait(chips_1, x1, "gather_weights_1a_wait"), within_chip, sems_b,
                          "gather_weights_1b_start")
    gw1 = _gathered_weights(*_split_wait(core_1, core_1["token"], "gather_weights_1b_wait"), ca8, dw8)
    _, saved1 = _layer_fwd(x1, mems, 1, repl, gw1)

    dz1, dproj1, g1 = _layer_bwd_a(None, tgt, mems, 1, repl, gw1, saved1)
    shards = lambda g: g.reshape(N_DEV, W_IN_COLS, D_MODEL)
    early_1 = _exchange_start(_grad_items(g1), "exchange_grads_1a_start")
    up, g_wt_in_1 = _in_bwd(saved1[0], dproj1, gw1["wt_in"], dz1, (early_1["token"],))
    late_1 = _exchange_start([(shards(g_wt_in_1), True, None)], "exchange_grads_1b_start")
    dz0, dproj0, g0 = _layer_bwd_a(up, None, mems, 0, repl, gw0, saved0, (late_1["token"],))
    early_0 = _exchange_start(_grad_items(g0), "exchange_grads_0a_start")
    g_wt_in_0 = _dw_in_matmul(saved0[0], dproj0, (early_0["token"],))
    late_0 = _exchange_start([(shards(g_wt_in_0), True, None)], "exchange_grads_0b_start")
    grad_x = _dx_matmul(dproj0, gw0["wt_in"], dz0, (late_0["token"],))

    landed = [None, _landed(_split_wait(early_1, grad_x, "exchange_grads_1a_wait"))]
    (landed[1]["w_in"],) = _split_wait(late_1, grad_x, "exchange_grads_1b_wait")
    big = {}
    for n, tr in _BIG:
        big[n] = _adamw_big(landed[1][n], wts[n], mom[n], var[n], 1, (), "adamw_" + n + "_1", tr)
    landed[0] = _landed(_split_wait(early_0, big["w_kv"][0], "exchange_grads_0a_wait"))
    for n, tr in _BIG[1:]:
        big[n] = _adamw_big(landed[0][n], wts[n], mom[n], var[n], 0, big[n], "adamw_" + n + "_0", tr)
    small, loss = _adamw_small(landed, wts, mom, var)
    (landed[0]["w_in"],) = _split_wait(late_0, loss, "exchange_grads_0b_wait")
    big["w_in"] = _adamw_big(landed[0]["w_in"], wts["w_in"], mom["w_in"], var["w_in"], 0, big["w_in"],
                             "adamw_w_in_0", _BIG[0][1])

    res = {**small, **big}
    res = {n: ([jnp.transpose(a, turned[n]) for a in r] if n in turned else r) for n, r in res.items()}
    return (loss[0, 0], grad_x[None], *[res[n][0] for n in names], *[res[n][1] for n in names],
            *[res[n][2] for n in names], *[res[n][3] for n in names])
```

```python
import math

import jax
import jax.numpy as jnp
from jax import lax
from jax.experimental import pallas as pl
from jax.experimental.pallas import tpu as pltpu

F32 = jnp.float32
MM_DTYPE = jnp.bfloat16
GRAD_DTYPE = jnp.bfloat16

D_MODEL = 1024
DEPTH = 2
D_G = 256
N_GROUPS = 5
D_MIX = N_GROUPS * D_G
N_SUB = 4
HEAD_DIM = D_G // N_SUB
CONV_A = 3
CONV_D = 31
CHUNK = 128
POOL_WINDOWS = (2, 4, 8, 16)
MEM_LEN = 256
LN_EPS = 1e-5
ALPHA = (2.0 * DEPTH) ** 0.25
D_IN = 9 * D_G + D_MIX
ATT_SCALE = 1.0 / math.sqrt(HEAD_DIM)

ADAM_LR = 0.001
ADAM_B1 = 0.9
ADAM_B2 = 0.999
ADAM_EPS = 1e-08
ADAM_WD = 0.01
ADAM_STEP = 10

N_DEV = 8
W_IN_COLS = D_IN // N_DEV
CONV_CH = D_G // N_DEV
HALO = 32
FWD_TILE = 512
BWD_TILE = 256
VMEM_LIMIT = 56 * 1024 * 1024

C_XA, C_BA, C_CA, C_U, C_V, C_XC, C_DA, C_DG, C_Q = range(9)
C_GATE = 9 * D_G

G256_ROWS = ("sg_ln_g", "sg_ln_b", "pool_scale", "cc_dw_b", "cc_ln_g", "cc_ln_b")
G1024_ROWS = ("ln_g", "ln_b")
LOSS_ROW = 2


def _mm(a, b):
    return jnp.dot(a.astype(MM_DTYPE), b.astype(MM_DTYPE), preferred_element_type=F32)


def _mm_nt(a, b):
    return lax.dot_general(a.astype(MM_DTYPE), b.astype(MM_DTYPE), (((1,), (1,)), ((), ())),
                           preferred_element_type=F32)


def _mm_tn(a, b):
    return lax.dot_general(a.astype(MM_DTYPE), b.astype(MM_DTYPE), (((0,), (0,)), ((), ())),
                           preferred_element_type=F32)


def _sigmoid(x):
    return 0.5 * jnp.tanh(0.5 * x) + 0.5


_GELU_C = math.sqrt(2.0 / math.pi)
_GELU_A = 0.044715


def _gelu(x):
    th = jnp.tanh(_GELU_C * (x + _GELU_A * (x * x * x)))
    return 0.5 * x * (1.0 + th), th


def _dgelu(x, th):
    return 0.5 * (1.0 + th) + 0.5 * x * (1.0 - th * th) * (_GELU_C * (1.0 + 3.0 * _GELU_A * (x * x)))


def _ln_fwd(x, g, b):
    mu = jnp.mean(x, axis=-1, keepdims=True)
    xc = x - mu
    var = jnp.mean(xc * xc, axis=-1, keepdims=True)
    rstd = lax.rsqrt(var + LN_EPS)
    xhat = xc * rstd
    return xhat * g + b, xhat, rstd


def _ln_bwd(dy, xhat, rstd, g):
    dxhat = dy * g
    m1 = jnp.mean(dxhat, axis=-1, keepdims=True)
    m2 = jnp.mean(dxhat * xhat, axis=-1, keepdims=True)
    return rstd * (dxhat - m1 - xhat * m2)


def _rowsum(x):
    return jnp.sum(x, axis=0, keepdims=True)


def _col(ref, k):
    return ref[:, k * D_G:(k + 1) * D_G]


def _head_of_lane(shape):
    return jnp.right_shift(lax.broadcasted_iota(jnp.int32, shape, len(shape) - 1), HEAD_DIM.bit_length() - 1)


def _pool_select(lane_grp, s2, s4, s8, s16):
    return jnp.where(lane_grp == 0, s2, jnp.where(lane_grp == 1, s4, jnp.where(lane_grp == 2, s8, s16)))


def _row_view(ref, layer):
    return ref.at[pl.ds(layer, 1)]


def _make_residues(ext_ref, res_ref):
    rows = res_ref.shape[1]
    for r in range(1, 8):
        res_ref[r - 1] = ext_ref[pl.ds(r, rows), :]


def _rows_at(ext_ref, res_ref, off, tile):
    a, r = divmod(off, 8)
    if r == 0:
        return ext_ref[pl.ds(off, tile), :]
    return res_ref[r - 1, pl.ds(8 * a, tile), :]


def _residue_scratch(tile):
    return pltpu.VMEM((7, HALO + tile - 8, D_G), F32)


PROJ_SEGMENTS = ((C_XA * D_G, (C_CA + 1) * D_G), (C_XC * D_G, (C_XC + 1) * D_G), (C_DA * D_G, (C_DG + 1) * D_G),
                 (C_U * D_G, (C_V + 1) * D_G), (C_Q * D_G, (C_Q + 1) * D_G), (C_GATE, D_IN))


def _branch_forward(p_ref, ph_ref, first, row0, km_ref, vm_ref, w, ext_a, ext_c, ext_d, res_c, res_d, tile,
                    kept_ref=None, produce=None):
    r = {}
    produce = produce or (lambda: None)
    produce()
    produce()
    xa, ba, ca = _col(p_ref, C_XA), _col(p_ref, C_BA), _col(p_ref, C_CA)
    g_a = ca * xa
    ext_a[0:HALO] = jnp.where(first, 0.0, _col(ph_ref, C_CA) * _col(ph_ref, C_XA))
    ext_a[HALO:HALO + tile] = g_a
    conv_a = w["conv_a"][0:1, :] * ext_a[pl.ds(HALO - 2, tile), :]
    for k in range(1, CONV_A):
        conv_a = conv_a + w["conv_a"][k:k + 1, :] * ext_a[pl.ds(HALO - 2 + k, tile), :]
    r.update(xa=xa, ba=ba, ca=ca, g_a=g_a, conv_a=conv_a)
    ya = ba * conv_a

    produce()
    lane_grp = _head_of_lane((tile, D_G))
    trow = row0 + lax.broadcasted_iota(jnp.int32, (tile, D_G), 0)
    win = _pool_select(lane_grp, 2, 4, 8, 16)
    inv_cnt = 1.0 / jnp.minimum(trow + 1, win).astype(F32)
    if kept_ref is None:
        xc = _col(p_ref, C_XC)
        ext_c[0:HALO] = jnp.where(first, 0.0, _col(ph_ref, C_XC))
        ext_c[HALO:HALO + tile] = xc
        _make_residues(ext_c, res_c)
        acc = xc
        sums = {}
        for k in range(1, POOL_WINDOWS[-1]):
            acc = acc + _rows_at(ext_c, res_c, HALO - k, tile)
            if k + 1 in POOL_WINDOWS:
                sums[k + 1] = acc
        ypre = _pool_select(lane_grp, sums[2], sums[4], sums[8], sums[16]) * inv_cnt - xc
    else:
        ypre = kept_ref[:, D_G:2 * D_G]
    pool_mm = _mm(ypre, w["pool_wbd"][...])
    yc = pool_mm * w["pool_scale"][...]
    r.update(lane_grp=lane_grp, inv_cnt=inv_cnt, ypre=ypre, pool_mm=pool_mm)

    produce()
    da, dg = _col(p_ref, C_DA), _col(p_ref, C_DG)
    sig_dg = _sigmoid(dg)
    hd = da * sig_dg
    if kept_ref is None:
        ext_d[0:HALO] = jnp.where(first, 0.0, _col(ph_ref, C_DA) * _sigmoid(_col(ph_ref, C_DG)))
        ext_d[HALO:HALO + tile] = hd
        _make_residues(ext_d, res_d)
        conv_d = w["cc_dw_b"][...] + w["cc_dw_w"][0:1, :] * _rows_at(ext_d, res_d, HALO - (CONV_D - 1), tile)
        for j in range(1, CONV_D):
            conv_d = conv_d + w["cc_dw_w"][j:j + 1, :] * _rows_at(ext_d, res_d, HALO - (CONV_D - 1) + j, tile)
    else:
        conv_d = kept_ref[:, 0:D_G]
    r["kept"] = (conv_d, ypre)
    ln_d, xhat_d, rstd_d = _ln_fwd(conv_d, w["cc_ln_g"][...], w["cc_ln_b"][...])
    sig_ln = _sigmoid(ln_d)
    act_d = ln_d * sig_ln
    yd = _mm(act_d, w["cc_pw_w"][...])
    r.update(da=da, sig_dg=sig_dg, hd=hd, ln_d=ln_d, xhat_d=xhat_d, rstd_d=rstd_d, sig_ln=sig_ln, act_d=act_d)
    produce()

    u, v = _col(p_ref, C_U), _col(p_ref, C_V)
    ug, th_u = _gelu(u)
    vg, th_v = _gelu(v)
    vn, xhat_v, rstd_v = _ln_fwd(vg, w["sg_ln_g"][...], w["sg_ln_b"][...])
    tri = (lax.broadcasted_iota(jnp.int32, (CHUNK, CHUNK), 0)
           >= lax.broadcasted_iota(jnp.int32, (CHUNK, CHUNK), 1))
    wm = [jnp.where(tri, w["sg_w"][h], 0.0).astype(MM_DTYPE) for h in range(N_SUB)]
    lo = lax.broadcasted_iota(jnp.int32, (CHUNK, 2 * HEAD_DIM), 1) < HEAD_DIM
    chunks = []
    for c in range(tile // CHUNK):
        halves = []
        for hf in range(2):
            vh = vn[c * CHUNK:(c + 1) * CHUNK, hf * 128:(hf + 1) * 128]
            halves.append(_mm(wm[2 * hf], jnp.where(lo, vh, 0.0)) + _mm(wm[2 * hf + 1], jnp.where(lo, 0.0, vh)))
        chunks.append(jnp.concatenate(halves, axis=1) + w["sg_bias"][...])
    mixed = jnp.concatenate(chunks, axis=0)
    yb = ug * mixed
    r.update(u=u, v=v, ug=ug, th_u=th_u, th_v=th_v, vn=vn, xhat_v=xhat_v, rstd_v=rstd_v, wm=wm, lo=lo,
             mixed=mixed, tri=tri)

    produce()
    q = _col(p_ref, C_Q)
    ye = jnp.zeros((tile, D_G), F32)
    probs = []
    for h in range(N_SUB):
        s = _mm_nt(q, km_ref[h]) * ATT_SCALE
        e = jnp.exp(s - jnp.max(s, axis=-1, keepdims=True))
        p = e * (1.0 / jnp.sum(e, axis=-1, keepdims=True))
        probs.append(p)
        ye = ye + _mm(p, vm_ref[h])
    r.update(q=q, probs=probs)

    gate = p_ref[:, C_GATE:C_GATE + D_MIX]
    sig_gate = _sigmoid(gate)
    r.update(gate=gate, sig_gate=sig_gate, branch_out=(ya, yb, yc, yd, ye))
    return r


_BRANCH_REPL = ("sg_ln_g", "sg_ln_b", "sg_w", "sg_b", "pool_w", "pool_scale", "cc_dw_b", "cc_ln_g", "cc_ln_b")
_BRANCH_W_SCRATCH = (("conv_a", (CONV_A, D_G)), ("cc_dw_w", (CONV_D, D_G)), ("sg_bias", (CHUNK, D_G)),
                     ("pool_wbd", (D_G, D_G)), ("sgb8", (8, CHUNK)))


def _branch_weights(layer, nat, pw_ref, ca_ref, dw_ref, scr, init):
    @pl.when(init)
    def _():
        for p in range(N_DEV):
            scr["conv_a"][:, p * CONV_CH:(p + 1) * CONV_CH] = ca_ref[p, :, layer, :]
            scr["cc_dw_w"][:, p * CONV_CH:(p + 1) * CONV_CH] = dw_ref[p, :, layer, :]
        scr["sgb8"][...] = jnp.zeros((8, CHUNK), F32)
        scr["sgb8"][0:N_SUB] = nat["sg_b"][layer]
        sgb_t = scr["sgb8"][...].T
        head = _head_of_lane((CHUNK, D_G))
        bias = jnp.zeros((CHUNK, D_G), F32)
        for h in range(N_SUB):
            bias = jnp.where(head == h, sgb_t[:, h:h + 1], bias)
        scr["sg_bias"][...] = bias
        scr["pool_wbd"][...] = jnp.zeros((D_G, D_G), F32)
        for gi in range(N_SUB):
            sl = slice(gi * HEAD_DIM, (gi + 1) * HEAD_DIM)
            scr["pool_wbd"][sl, sl] = nat["pool_w"][layer, gi]

    w = {n: _row_view(nat[n], layer) for n in ("sg_ln_g", "sg_ln_b", "pool_scale", "cc_dw_b", "cc_ln_g", "cc_ln_b")}
    w.update(conv_a=scr["conv_a"], cc_dw_w=scr["cc_dw_w"], sg_bias=scr["sg_bias"], pool_wbd=scr["pool_wbd"],
             sg_w=nat["sg_w"].at[layer], cc_pw_w=pw_ref)
    return w


def _full_spec(a):
    nd = a.ndim
    return pl.BlockSpec(a.shape, lambda *_, _nd=nd: (0,) * _nd)


def _tie_specs(ties):
    return [pl.BlockSpec((8, 128), lambda *_: (0, 0)) for _ in ties]


def _params(*sem):
    return pltpu.CompilerParams(dimension_semantics=sem or None, vmem_limit_bytes=VMEM_LIMIT)


def _kv_project(mem, w_kv):
    def body(mem_ref, w_ref, km_ref, vm_ref):
        kv = _mm(mem_ref[...], w_ref[...])
        k, v = kv[:, :D_G], kv[:, D_G:]
        grp = _head_of_lane((MEM_LEN, D_G))
        for h in range(N_SUB):
            km_ref[h] = jnp.where(grp == h, k, 0.0).astype(km_ref.dtype)
            vm_ref[h] = jnp.where(grp == h, v, 0.0).astype(vm_ref.dtype)

    shp = jax.ShapeDtypeStruct((N_SUB, MEM_LEN, D_G), MM_DTYPE)
    return pl.pallas_call(body, out_shape=(shp, shp), name="kv_project", compiler_params=_params())(mem, w_kv)


def _layer_fwd_fused(x, wt_in, km, vm, layer, repl, pw, ca8, dw8, w_out, want_xn, ties=(), tile=FWD_TILE):
    s = x.shape[0]
    nat_arrays = [repl[n] for n in _BRANCH_REPL]
    n_nat, nt = len(nat_arrays), len(ties)

    def body(x_ref, wt_ref, km_ref, vm_ref, *rest):
        nat = dict(zip(_BRANCH_REPL, rest[:n_nat]))
        pw_ref, ca_ref, dw_ref, wo_ref, g_ref, b_ref = rest[n_nat:n_nat + 6]
        rest = rest[n_nat + 6 + nt:]
        p_ref, h_ref, z_ref, cd_ref = rest[:4]
        rest = rest[4:]
        if want_xn:
            xn_ref, rest = rest[0], rest[1:]
        ph_ref, ext_a, ext_c, ext_d, res_c, res_d = rest[:6]
        scr = dict(zip([n for n, _ in _BRANCH_W_SCRATCH], rest[6:]))
        i = pl.program_id(0)

        @pl.when(i == 0)
        def _():
            ph_ref[...] = jnp.zeros_like(ph_ref)

        xt = x_ref[...]
        xb = xt.astype(MM_DTYPE)
        segments = iter(PROJ_SEGMENTS)

        def produce():
            lo, hi = next(segments)
            p_ref[:, lo:hi] = _mm_nt(xb, wt_ref[lo:hi, :])

        w = _branch_weights(layer, nat, pw_ref, ca_ref, dw_ref, scr, i == 0)
        r = _branch_forward(p_ref, ph_ref, i == 0, i * tile, km_ref, vm_ref, w, ext_a, ext_c, ext_d, res_c, res_d,
                            tile, None, produce)
        ph_ref[...] = p_ref[tile - HALO:tile, :]
        cd_ref[:, 0:D_G], cd_ref[:, D_G:2 * D_G] = r["kept"]
        h = (jnp.concatenate(r["branch_out"], axis=1) * (r["gate"] * r["sig_gate"])).astype(h_ref.dtype)
        h_ref[...] = h
        z = ALPHA * xt + _mm(h, wo_ref[...])
        z_ref[...] = z
        if want_xn:
            xn_ref[...] = _ln_fwd(z, _row_view(g_ref, layer)[...], _row_view(b_ref, layer)[...])[0]

    row = lambda i: (i, 0)
    consts = [wt_in, km, vm] + nat_arrays + [pw, ca8, dw8, w_out, repl["ln_g"], repl["ln_b"]]
    act = jax.ShapeDtypeStruct((s, D_MODEL), F32)
    act_spec = pl.BlockSpec((tile, D_MODEL), row)
    res = pl.pallas_call(
        body, grid=(s // tile,),
        in_specs=[act_spec] + [_full_spec(a) for a in consts] + _tie_specs(ties),
        out_specs=(pl.BlockSpec((tile, D_IN), row), pl.BlockSpec((tile, D_MIX), row), act_spec,
                   pl.BlockSpec((tile, 2 * D_G), row)) + ((act_spec,) if want_xn else ()),
        out_shape=(jax.ShapeDtypeStruct((s, D_IN), F32), jax.ShapeDtypeStruct((s, D_MIX), MM_DTYPE), act,
                   jax.ShapeDtypeStruct((s, 2 * D_G), F32)) + ((act,) if want_xn else ()),
        scratch_shapes=[pltpu.VMEM((HALO, D_IN), F32)] + [pltpu.VMEM((HALO + tile, D_G), F32)] * 3
        + [_residue_scratch(tile)] * 2 + [pltpu.VMEM(shape, F32) for _, shape in _BRANCH_W_SCRATCH],
        name="layer_fwd", compiler_params=_params("arbitrary"),
    )(x, *consts, *ties)
    return tuple(res) if want_xn else tuple(res) + (None,)


_BRANCH_GRADS = (("g256", (8, D_G)), ("sg_w", (N_SUB, CHUNK, CHUNK)), ("sg_b", (8, CHUNK)),
                 ("pool_w", (N_SUB, HEAD_DIM, HEAD_DIM)), ("conv_a_w", (N_DEV, CONV_A, CONV_CH)),
                 ("cc_dw_w", (N_DEV, CONV_D, CONV_CH)), ("cc_pw_w", (D_G, D_G)),
                 ("dk", (N_SUB, MEM_LEN, D_G)), ("dv", (N_SUB, MEM_LEN, D_G)))
_BRANCH_ACC = (("conv_a", (CONV_A, D_G)), ("cc_dw_w", (CONV_D, D_G)), ("pool_wbd", (D_G, D_G)),
               ("sg_bias", (CHUNK, D_G)), ("sg_w", (N_SUB, CHUNK, CHUNK)))
_BRANCH_GRADS_NARROW = ("sg_w", "pool_w")


def _layer_bwd_fused(up, target, z, h, proj, kept, km, vm, layer, repl, w_out, pw, ca8, dw8, ties=(), tile=BWD_TILE):
    s = proj.shape[0]
    nt = s // tile
    hb = tile // HALO
    nat_arrays = [repl[n] for n in _BRANCH_REPL]
    n_nat, n_grads, n_acc, n_ties = len(nat_arrays), len(_BRANCH_GRADS), len(_BRANCH_ACC), len(ties)
    row_of = {n: k for k, n in enumerate(G256_ROWS)}
    from_loss = target is not None

    def body(o_ref, z_ref, h_ref, p_ref, ph_ref, cd_ref, km_ref, vm_ref, *rest):
        nat = dict(zip(_BRANCH_REPL, rest[:n_nat]))
        pw_ref, ca_ref, dw_ref, lng_ref, lnb_ref, wo_ref = rest[n_nat:n_nat + 6]
        rest = rest[n_nat + 6 + n_ties:]
        dz_ref, dp_ref, gw_ref, slab_ref = rest[:4]
        g = dict(zip([n for n, _ in _BRANCH_GRADS], rest[4:4 + n_grads]))
        rest = rest[4 + n_grads:]
        ext_a, rev_a, rev_c, rev_d, res_rc, res_rd, gw_acc, lacc = rest[:8]
        acc = dict(zip([n for n, _ in _BRANCH_ACC], rest[8:8 + n_acc]))
        scr = dict(zip([n for n, _ in _BRANCH_W_SCRATCH], rest[8 + n_acc:]))
        i = pl.program_id(0)
        t = nt - 1 - i

        @pl.when(i == 0)
        def _():
            for ref in list(g.values()) + list(acc.values()) + [rev_a, rev_c, rev_d, gw_acc, slab_ref, lacc]:
                ref[...] = jnp.zeros_like(ref)

        g_ln = _row_view(lng_ref, layer)[...]
        xn, xhat, rstd = _ln_fwd(z_ref[...], g_ln, _row_view(lnb_ref, layer)[...])
        if from_loss:
            err = xn - o_ref[...]
            lacc[...] += _rowsum(err * err)
            dxn = err * (1.0 / D_MODEL)
        else:
            dxn = o_ref[...]
        slab_ref[0:1, :] += _rowsum(dxn * xhat)
        slab_ref[1:2, :] += _rowsum(dxn)
        dz = _ln_bwd(dxn, xhat, rstd, g_ln)
        dz_ref[...] = dz
        dzb = dz.astype(MM_DTYPE)

        w = _branch_weights(layer, nat, pw_ref, ca_ref, dw_ref, scr, i == 0)
        r = _branch_forward(p_ref, ph_ref, t == 0, t * tile, km_ref, vm_ref, w, ext_a, None, None, None, None,
                            tile, cd_ref)

        def put(k, val, width=D_G):
            dp_ref[:, k:k + width] = val.astype(dp_ref.dtype)

        def add_row(name, val):
            k = row_of[name]
            g["g256"][k:k + 1, :] += val

        def push_rev(rev, val):
            head = rev[0:HALO]
            rev[tile:tile + HALO] = head
            rev[0:tile] = val

        gate, sig_gate = r["gate"], r["sig_gate"]

        def branch_grad(group):
            cols = slice(group * D_G, (group + 1) * D_G)
            dh_g = _mm_nt(dzb, wo_ref[cols, :])
            gate_g, sig_g = gate[:, cols], sig_gate[:, cols]
            put(C_GATE + group * D_G, dh_g * r["branch_out"][group] * (sig_g * (1.0 + gate_g * (1.0 - sig_g))))
            return dh_g * (gate_g * sig_g)

        dya = branch_grad(0)
        dyc = branch_grad(2)

        put(C_BA * D_G, dya * r["conv_a"])
        dconv_a = dya * r["ba"]
        push_rev(rev_a, dconv_a)
        dga = jnp.zeros((tile, D_G), F32)
        for k in range(CONV_A):
            ahead = rev_a[pl.ds(CONV_A - 1 - k, tile), :]
            dga = dga + w["conv_a"][k:k + 1, :] * ahead
            acc["conv_a"][k:k + 1, :] += _rowsum(r["g_a"] * ahead)
        put(C_CA * D_G, dga * r["xa"])
        put(C_XA * D_G, dga * r["ca"])

        dyd = branch_grad(3)
        add_row("pool_scale", _rowsum(dyc * r["pool_mm"]))
        dmm = dyc * w["pool_scale"][...]
        acc["pool_wbd"][...] += _mm_tn(r["ypre"], dmm)
        dypre = _mm_nt(dmm, w["pool_wbd"][...])
        dws = dypre * r["inv_cnt"]
        push_rev(rev_c, dws)
        _make_residues(rev_c, res_rc)
        run = dws
        sums = {}
        for k in range(1, POOL_WINDOWS[-1]):
            run = run + _rows_at(rev_c, res_rc, k, tile)
            if k + 1 in POOL_WINDOWS:
                sums[k + 1] = run
        put(C_XC * D_G, _pool_select(r["lane_grp"], sums[2], sums[4], sums[8], sums[16]) - dypre)

        dyb = branch_grad(1)
        gw_acc[...] += _mm_tn(h_ref[...], dzb)
        g["cc_pw_w"][...] += _mm_tn(r["act_d"], dyd)
        dact = _mm_nt(dyd, w["cc_pw_w"][...])
        sig_ln, ln_d = r["sig_ln"], r["ln_d"]
        dln = dact * (sig_ln * (1.0 + ln_d * (1.0 - sig_ln)))
        add_row("cc_ln_g", _rowsum(dln * r["xhat_d"]))
        add_row("cc_ln_b", _rowsum(dln))
        dconv_d = _ln_bwd(dln, r["xhat_d"], r["rstd_d"], w["cc_ln_g"][...])
        add_row("cc_dw_b", _rowsum(dconv_d))
        push_rev(rev_d, dconv_d)
        _make_residues(rev_d, res_rd)
        dhd = jnp.zeros((tile, D_G), F32)
        for j in range(CONV_D):
            ahead = _rows_at(rev_d, res_rd, CONV_D - 1 - j, tile)
            dhd = dhd + w["cc_dw_w"][j:j + 1, :] * ahead
            acc["cc_dw_w"][j:j + 1, :] += _rowsum(r["hd"] * ahead)
        sig_dg = r["sig_dg"]
        put(C_DA * D_G, dhd * sig_dg)
        put(C_DG * D_G, dhd * r["da"] * sig_dg * (1.0 - sig_dg))

        dye = branch_grad(4)
        dug = dyb * r["mixed"]
        dmixed = dyb * r["ug"]
        wm, lo, vn = r["wm"], r["lo"], r["vn"]
        dvn_chunks = []
        for c in range(tile // CHUNK):
            rows = slice(c * CHUNK, (c + 1) * CHUNK)
            acc["sg_bias"][...] += dmixed[rows, :]
            halves = []
            for hf in range(2):
                cols = slice(hf * 128, (hf + 1) * 128)
                dm = dmixed[rows, cols]
                dm_a, dm_b = jnp.where(lo, dm, 0.0), jnp.where(lo, 0.0, dm)
                vh = vn[rows, cols]
                acc["sg_w"][2 * hf] += _mm_nt(dm_a, vh)
                acc["sg_w"][2 * hf + 1] += _mm_nt(dm_b, vh)
                halves.append(_mm_tn(wm[2 * hf], dm_a) + _mm_tn(wm[2 * hf + 1], dm_b))
            dvn_chunks.append(jnp.concatenate(halves, axis=1))
        dvn = jnp.concatenate(dvn_chunks, axis=0)
        add_row("sg_ln_g", _rowsum(dvn * r["xhat_v"]))
        add_row("sg_ln_b", _rowsum(dvn))
        dvg = _ln_bwd(dvn, r["xhat_v"], r["rstd_v"], w["sg_ln_g"][...])
        put(C_V * D_G, dvg * _dgelu(r["v"], r["th_v"]))
        put(C_U * D_G, dug * _dgelu(r["u"], r["th_u"]))

        q = r["q"]
        dq = jnp.zeros((tile, D_G), F32)
        for h in range(N_SUB):
            p = r["probs"][h]
            dp = _mm_nt(dye, vm_ref[h])
            g["dv"][h] += _mm_tn(p, dye)
            ds = p * (dp - jnp.sum(dp * p, axis=-1, keepdims=True)) * ATT_SCALE
            dq = dq + _mm(ds, km_ref[h])
            g["dk"][h] += _mm_tn(ds, q)
        put(C_Q * D_G, dq)

        @pl.when(i == nt - 1)
        def _():
            for h in range(N_SUB):
                g["sg_w"][h] = jnp.where(r["tri"], acc["sg_w"][h], 0.0).astype(g["sg_w"].dtype)
            lane_head = _head_of_lane((CHUNK, D_G))
            col_of = lax.broadcasted_iota(jnp.int32, (CHUNK, 8), 1)
            ba = acc["sg_bias"][...]
            sgb_t = jnp.zeros((CHUNK, 8), F32)
            for h in range(N_SUB):
                col = jnp.sum(jnp.where(lane_head == h, ba, 0.0), axis=-1, keepdims=True)
                sgb_t = jnp.where(col_of == h, col, sgb_t)
            g["sg_b"][...] = sgb_t.T
            wbd = acc["pool_wbd"][...]
            for gi in range(N_SUB):
                sl = slice(gi * HEAD_DIM, (gi + 1) * HEAD_DIM)
                g["pool_w"][gi] = wbd[sl, sl].astype(g["pool_w"].dtype)
            ca, dw = acc["conv_a"][...], acc["cc_dw_w"][...]
            for p in range(N_DEV):
                g["conv_a_w"][p] = ca[:, p * CONV_CH:(p + 1) * CONV_CH]
                g["cc_dw_w"][p] = dw[:, p * CONV_CH:(p + 1) * CONV_CH]
            gw_ref[...] = gw_acc[...].astype(gw_ref.dtype)
            if from_loss:
                total = jnp.sum(lacc[...], axis=-1, keepdims=True) * (0.5 / D_MODEL)
                slab_ref[LOSS_ROW:LOSS_ROW + 1, :] = jnp.broadcast_to(total, (1, D_MODEL))

    rev = lambda i: (nt - 1 - i, 0)
    fixed = lambda i: (0, 0)
    act_spec = pl.BlockSpec((tile, D_MODEL), rev)
    grad_specs = tuple(pl.BlockSpec(shape, lambda i, _nd=len(shape): (0,) * _nd) for _, shape in _BRANCH_GRADS)
    grad_shapes = tuple(jax.ShapeDtypeStruct(shape, GRAD_DTYPE if n in _BRANCH_GRADS_NARROW else F32)
                        for n, shape in _BRANCH_GRADS)
    consts = [km, vm] + nat_arrays + [pw, ca8, dw8, repl["ln_g"], repl["ln_b"], w_out]
    outs = pl.pallas_call(
        body, grid=(nt,),
        in_specs=[act_spec, act_spec, pl.BlockSpec((tile, D_MIX), rev), pl.BlockSpec((tile, D_IN), rev),
                  pl.BlockSpec((HALO, D_IN), lambda i: (jnp.maximum((nt - 1 - i) * hb - 1, 0), 0)),
                  pl.BlockSpec((tile, 2 * D_G), rev)]
        + [_full_spec(a) for a in consts] + _tie_specs(ties),
        out_specs=(act_spec, pl.BlockSpec((tile, D_IN), rev), pl.BlockSpec((D_MIX, D_MODEL), fixed),
                   pl.BlockSpec((8, D_MODEL), fixed)) + grad_specs,
        out_shape=(jax.ShapeDtypeStruct((s, D_MODEL), F32), jax.ShapeDtypeStruct((s, D_IN), MM_DTYPE),
                   jax.ShapeDtypeStruct((D_MIX, D_MODEL), GRAD_DTYPE), jax.ShapeDtypeStruct((8, D_MODEL), F32))
        + grad_shapes,
        scratch_shapes=[pltpu.VMEM((HALO + tile, D_G), F32)] * 4 + [_residue_scratch(tile)] * 2
        + [pltpu.VMEM((D_MIX, D_MODEL), F32), pltpu.VMEM((1, D_MODEL), F32)]
        + [pltpu.VMEM(shape, F32) for _, shape in _BRANCH_ACC + _BRANCH_W_SCRATCH],
        name="layer_bwd_loss" if from_loss else "layer_bwd", compiler_params=_params("arbitrary"),
    )(target if from_loss else up, z, h, proj, proj, kept, *consts, *ties)
    return outs[0], outs[1], outs[2], outs[3], dict(zip([n for n, _ in _BRANCH_GRADS], outs[4:]))


def _dx_matmul(dproj, wt_in, dz, ties=(), tm=512):
    s = dproj.shape[0]

    def body(dp_ref, w_ref, dz_ref, *rest):
        o_ref = rest[len(ties)]
        o_ref[...] = _mm(dp_ref[...], w_ref[...]) + ALPHA * dz_ref[...]

    row = lambda i: (i, 0)
    return pl.pallas_call(
        body, grid=(s // tm,),
        in_specs=[pl.BlockSpec((tm, D_IN), row), _full_spec(wt_in), pl.BlockSpec((tm, D_MODEL), row)]
        + _tie_specs(ties),
        out_specs=pl.BlockSpec((tm, D_MODEL), row),
        out_shape=jax.ShapeDtypeStruct((s, D_MODEL), F32), name="dx_mm", compiler_params=_params("arbitrary"),
    )(dproj, wt_in, dz, *ties)


def _dw_in_matmul(x, dproj, ties=(), tk=512):
    s = x.shape[0]
    nk = s // tk
    blk = 2 * W_IN_COLS

    def body(x_ref, dp_ref, *rest):
        o_ref, acc = rest[len(ties):]
        k = pl.program_id(0)

        @pl.when(k == 0)
        def _():
            acc[...] = jnp.zeros_like(acc)

        xb = x_ref[...].astype(MM_DTYPE)
        for j in range(D_IN // blk):
            acc[j * blk:(j + 1) * blk, :] += _mm_tn(dp_ref[:, j * blk:(j + 1) * blk], xb)

        @pl.when(k == nk - 1)
        def _():
            o_ref[...] = acc[...].astype(o_ref.dtype)

    return pl.pallas_call(
        body, grid=(nk,),
        in_specs=[pl.BlockSpec((tk, D_MODEL), lambda k: (k, 0)), pl.BlockSpec((tk, D_IN), lambda k: (k, 0))]
        + _tie_specs(ties),
        out_specs=pl.BlockSpec((D_IN, D_MODEL), lambda k: (0, 0)),
        out_shape=jax.ShapeDtypeStruct((D_IN, D_MODEL), GRAD_DTYPE),
        scratch_shapes=[pltpu.VMEM((D_IN, D_MODEL), F32)], name="dw_in_mm", compiler_params=_params("arbitrary"),
    )(x, dproj, *ties)


def _in_bwd(x, dproj, wt_in, dz, ties=(), tm=512):
    s = x.shape[0]
    n_steps = s // tm

    assert wt_in.dtype == GRAD_DTYPE
    blk = 2 * W_IN_COLS

    def body(x_ref, dp_ref, w_hbm, dz_ref, *rest):
        o_ref, gw_hbm, w_vmem, acc, sem = rest[len(ties):]
        i = pl.program_id(0)

        @pl.when(i == 0)
        def _():
            fetch = pltpu.make_async_copy(w_hbm, w_vmem, sem)
            fetch.start()
            acc[...] = jnp.zeros_like(acc)
            fetch.wait()

        o_ref[...] = _mm(dp_ref[...], w_vmem[...]) + ALPHA * dz_ref[...]
        xb = x_ref[...].astype(MM_DTYPE)
        for j in range(D_IN // blk):
            acc[j * blk:(j + 1) * blk, :] += _mm_tn(dp_ref[:, j * blk:(j + 1) * blk], xb)

        @pl.when(i == n_steps - 1)
        def _():
            w_vmem[...] = acc[...].astype(w_vmem.dtype)
            emit = pltpu.make_async_copy(w_vmem, gw_hbm, sem)
            emit.start()
            emit.wait()

    row = lambda i: (i, 0)
    any_spec = pl.BlockSpec(memory_space=pl.ANY)
    return pl.pallas_call(
        body, grid=(n_steps,),
        in_specs=[pl.BlockSpec((tm, D_MODEL), row), pl.BlockSpec((tm, D_IN), row), any_spec,
                  pl.BlockSpec((tm, D_MODEL), row)] + _tie_specs(ties),
        out_specs=(pl.BlockSpec((tm, D_MODEL), row), any_spec),
        out_shape=(jax.ShapeDtypeStruct((s, D_MODEL), F32), jax.ShapeDtypeStruct((D_IN, D_MODEL), GRAD_DTYPE)),
        scratch_shapes=[pltpu.VMEM((D_IN, D_MODEL), wt_in.dtype), pltpu.VMEM((D_IN, D_MODEL), F32),
                        pltpu.SemaphoreType.DMA],
        name="in_bwd", compiler_params=_params("arbitrary"),
    )(x, dproj, wt_in, dz, *ties)


def _kv_bwd(mem, dk, dv):
    def body(mem_ref, dk_ref, dv_ref, o_ref):
        grp = _head_of_lane((MEM_LEN, D_G))
        dk_sum = jnp.zeros((MEM_LEN, D_G), F32)
        dv_sum = jnp.zeros((MEM_LEN, D_G), F32)
        for h in range(N_SUB):
            dk_sum = dk_sum + jnp.where(grp == h, dk_ref[h], 0.0)
            dv_sum = dv_sum + jnp.where(grp == h, dv_ref[h], 0.0)
        o_ref[...] = _mm_tn(mem_ref[...], jnp.concatenate([dk_sum, dv_sum], axis=1)).astype(o_ref.dtype)

    return pl.pallas_call(body, out_shape=jax.ShapeDtypeStruct((D_MODEL, 2 * D_G), GRAD_DTYPE), name="kv_bwd",
                          compiler_params=_params())(mem, dk, dv)


def _layer_fwd(x, mem, layer, repl, gw, ties=()):
    km, vm = _kv_project(mem, gw["w_kv"])
    proj, h, z, kept, xn = _layer_fwd_fused(x, gw["wt_in"], km, vm, layer, repl, gw["pw"], gw["ca8"], gw["dw8"],
                                            gw["w_out"], layer < DEPTH - 1, ties)
    return xn, (x, proj, h, z, km, vm, kept)


def _layer_bwd_a(up, target, mem, layer, repl, gw, saved, ties=()):
    x_in, proj, h, z, km, vm, kept = saved
    dz, dproj, g_w_out, g1024, bg = _layer_bwd_fused(up, target, z, h, proj, kept, km, vm, layer, repl, gw["w_out"],
                                                     gw["pw"], gw["ca8"], gw["dw8"], ties)
    grads = {n: bg[n] for n in ("g256", "sg_w", "sg_b", "pool_w", "conv_a_w", "cc_dw_w")}
    grads.update(w_out=g_w_out.reshape(N_DEV, D_MIX // N_DEV, D_MODEL), g1024=g1024,
                 w_kv=_kv_bwd(mem, bg["dk"], bg["dv"]).reshape(N_DEV, D_MODEL // N_DEV, 2 * D_G),
                 cc_pw_w=bg["cc_pw_w"].reshape(N_DEV, CONV_CH, D_G))
    return dz, dproj, grads


def _landing_shapes(items):
    out = []
    for a, scatter, pick in items:
        shape = a.shape if scatter else (N_DEV,) + (a.shape if pick is None else a.shape[1:])
        out.append(jax.ShapeDtypeStruct(shape, a.dtype))
    return tuple(out)


def _exchange_sems(n):
    return [pltpu.SemaphoreType.DMA(((N_DEV - 1) * n,)), pltpu.SemaphoreType.DMA(((N_DEV - 1) * n,)),
            pltpu.SemaphoreType.DMA((n,))]


def _exchange_copies(modes, ins, outs, send_sems, recv_sems, local_sems):
    n = len(ins)
    x, y, c = lax.axis_index("x"), lax.axis_index("y"), lax.axis_index("c")
    me = 4 * x + 2 * y + c

    def src_of(a, dest):
        scatter, pick = modes[a]
        if scatter:
            return ins[a].at[dest]
        return ins[a] if pick is None else ins[a].at[pick]

    local = [pltpu.make_async_copy(src_of(a, me), outs[a].at[me], local_sems.at[a]) for a in range(n)]
    sends, recvs = [], []
    for k in range(1, N_DEV):
        px = 1 - x if k & 4 else x
        py = 1 - y if k & 2 else y
        pc = 1 - c if k & 1 else c
        peer = 4 * px + 2 * py + pc
        for a in range(n):
            sems = dict(send_sem=send_sems.at[(k - 1) * n + a], recv_sem=recv_sems.at[(k - 1) * n + a],
                        device_id=(px, py, pc), device_id_type=pl.DeviceIdType.MESH)
            sends.append(pltpu.make_async_remote_copy(src_ref=src_of(a, peer), dst_ref=outs[a].at[me], **sems))
            recvs.append(pltpu.make_async_remote_copy(src_ref=src_of(a, peer), dst_ref=outs[a].at[peer], **sems))
    return local, sends, recvs


def _gather_two_level(items, name):
    n = len(items)
    assert not any(scatter for _, scatter, _ in items)
    picks = [pick for _, _, pick in items]

    def body(*refs):
        ins, outs = refs[:n], refs[n:2 * n]
        send_sems, recv_sems, local_sems = refs[2 * n:]
        x, y, c = lax.axis_index("x"), lax.axis_index("y"), lax.axis_index("c")
        sib = 1 - c
        chips = [(1 - x, y), (x, 1 - y), (1 - x, 1 - y)]

        def slot(a, px, py, pc):
            return outs[a].at[4 * px + 2 * py + pc]

        def copy(k, a, src, block, to):
            return pltpu.make_async_remote_copy(
                src_ref=src, dst_ref=slot(a, *block), send_sem=send_sems.at[k * n + a],
                recv_sem=recv_sems.at[k * n + a], device_id=to, device_id_type=pl.DeviceIdType.MESH)

        own = [ins[a] if picks[a] is None else ins[a].at[picks[a]] for a in range(n)]
        local = [pltpu.make_async_copy(own[a], slot(a, x, y, c), local_sems.at[a]) for a in range(n)]
        first = [copy(0, a, own[a], (x, y, c), (x, y, sib)) for a in range(n)]
        first += [copy(1 + j, a, own[a], (x, y, c), (*chip, c)) for j, chip in enumerate(chips[:2]) for a in range(n)]
        for cp in local + first:
            cp.start()

        def pass_on(j, a):
            chip = chips[j]
            copy(1 + j, a, own[a], (*chip, c), (x, y, c)).wait_recv()
            fwd = copy(4 + j, a, slot(a, *chip, c), (*chip, c), (x, y, sib))
            fwd.start()
            return fwd

        passed = [pass_on(j, a) for j in range(2) for a in range(n)]
        south = c == 0
        via = tuple(jnp.where(south, p, q) for p, q in zip(chips[0], chips[1]))
        blk = tuple(jnp.where(south, q, p) for p, q in zip(chips[0], chips[1]))
        relayed = [copy(3, a, slot(a, *blk, c), (*blk, c), (*via, c)) for a in range(n)]
        for cp in relayed:
            cp.start()
        passed += [pass_on(2, a) for a in range(n)]
        first += relayed
        for a in range(n):
            copy(0, a, own[a], (x, y, sib), (x, y, c)).wait_recv()
        for j, chip in enumerate(chips):
            for a in range(n):
                copy(4 + j, a, own[a], (*chip, sib), (x, y, c)).wait_recv()
        for cp in first + passed:
            cp.wait_send()
        for cp in local:
            cp.wait()

    any_spec = pl.BlockSpec(memory_space=pl.ANY)
    return pl.pallas_call(
        body, in_specs=[any_spec] * n, out_specs=(any_spec,) * n, out_shape=_landing_shapes(items),
        scratch_shapes=[pltpu.SemaphoreType.DMA((7 * n,)), pltpu.SemaphoreType.DMA((7 * n,)),
                        pltpu.SemaphoreType.DMA((n,))],
        name=name,
    )(*[a for a, _, _ in items])


_HBM_SPEC = pl.BlockSpec(memory_space=pltpu.HBM)
_SEM_SPEC = pl.BlockSpec(memory_space=pltpu.SEMAPHORE)
_SPLIT_PARAMS = pltpu.CompilerParams(has_side_effects=pltpu.SideEffectType.DATAFLOW_SIDE_EFFECTING)


def _split_start(srcs, lands, plan, sem_shapes, name):
    n_src, n_land = len(srcs), len(lands)
    n_buf = n_src + n_land
    bufs = [pltpu.with_memory_space_constraint(a, pltpu.HBM) for a in list(srcs) + list(lands)]

    def body(*refs):
        local, sends, _ = plan(refs[:n_src], refs[n_src:n_buf], *refs[n_buf:n_buf + 3])
        for cp in local + sends:
            cp.start()
        token = refs[-1]
        token[...] = jnp.zeros_like(token)

    res = pl.pallas_call(
        body, name=name, in_specs=[_HBM_SPEC] * n_buf,
        out_shape=tuple(sem_shapes) + tuple(pltpu.HBM(a.shape, a.dtype) for a in bufs)
        + (jax.ShapeDtypeStruct((8, 128), F32),),
        out_specs=(_SEM_SPEC,) * 3 + (_HBM_SPEC,) * n_buf + (pl.BlockSpec(memory_space=pltpu.VMEM),),
        input_output_aliases={i: 3 + i for i in range(n_buf)}, compiler_params=_SPLIT_PARAMS,
    )(*bufs)
    return dict(sems=res[:3], srcs=res[3:3 + n_src], lands=res[3 + n_src:3 + n_buf], token=res[-1], plan=plan)


def _split_wait(ticket, after, name, with_srcs=False):
    n_src, n_land = len(ticket["srcs"]), len(ticket["lands"])
    n_buf = n_src + n_land
    plan = ticket["plan"]

    def body(*refs):
        local, sends, recvs = plan(refs[:n_src], refs[n_src:n_buf], *refs[n_buf:n_buf + 3])
        for cp in recvs:
            cp.wait_recv()
        for cp in sends:
            cp.wait_send()
        for cp in local:
            cp.wait()

    bufs = list(ticket["srcs"]) + list(ticket["lands"])
    res = pl.pallas_call(
        body, name=name,
        in_specs=[_HBM_SPEC] * n_buf + [_SEM_SPEC] * 3 + [pl.BlockSpec(memory_space=pl.ANY)] * len(after),
        out_shape=tuple(pltpu.HBM(a.shape, a.dtype) for a in bufs), out_specs=(_HBM_SPEC,) * n_buf,
        input_output_aliases={i: i for i in range(n_buf)}, compiler_params=_SPLIT_PARAMS,
    )(*bufs, *ticket["sems"], *after)
    return res if with_srcs else res[n_src:]


def _empty_landings(items):
    return [lax.empty(s.shape, s.dtype) for s in _landing_shapes(items)]


def _exchange_start(items, name):
    modes = [(scatter, pick) for _, scatter, pick in items]
    plan = lambda ins, outs, *sems: _exchange_copies(modes, ins, outs, *sems)
    return _split_start([a for a, _, _ in items], _empty_landings(items), plan, _exchange_sems(len(items)), name)


def _two_level_plans(picks):
    n = len(picks)

    def place():
        x, y, c = lax.axis_index("x"), lax.axis_index("y"), lax.axis_index("c")
        return x, y, c, 1 - c, [(1 - x, y), (x, 1 - y), (1 - x, 1 - y)]

    def copy(outs, send_sems, recv_sems, k, a, src, block, to):
        px, py, pc = block
        return pltpu.make_async_remote_copy(
            src_ref=src, dst_ref=outs[a].at[4 * px + 2 * py + pc], send_sem=send_sems.at[k * n + a],
            recv_sem=recv_sems.at[k * n + a], device_id=to, device_id_type=pl.DeviceIdType.MESH)

    def between_chips(ins, outs, send_sems, recv_sems, local_sems):
        x, y, c, sib, chips = place()
        own = [ins[a] if picks[a] is None else ins[a].at[picks[a]] for a in range(n)]
        mk = lambda *args: copy(outs, send_sems, recv_sems, *args)
        local = [pltpu.make_async_copy(own[a], outs[a].at[4 * x + 2 * y + c], local_sems.at[a]) for a in range(n)]
        sends = [mk(0, a, own[a], (x, y, c), (x, y, sib)) for a in range(n)]
        sends += [mk(1 + j, a, own[a], (x, y, c), (*chip, c)) for j, chip in enumerate(chips) for a in range(n)]
        recvs = [mk(0, a, own[a], (x, y, sib), (x, y, c)) for a in range(n)]
        recvs += [mk(1 + j, a, own[a], (*chip, c), (x, y, c)) for j, chip in enumerate(chips) for a in range(n)]
        return local, sends, recvs

    def within_chip(ins, outs, send_sems, recv_sems, local_sems):
        x, y, c, sib, chips = place()
        mk = lambda *args: copy(outs, send_sems, recv_sems, *args)
        slot = lambda a, px, py, pc: outs[a].at[4 * px + 2 * py + pc]
        sends = [mk(j, a, slot(a, *chip, c), (*chip, c), (x, y, sib)) for j, chip in enumerate(chips)
                 for a in range(n)]
        recvs = [mk(j, a, slot(a, *chip, c), (*chip, sib), (x, y, c)) for j, chip in enumerate(chips)
                 for a in range(n)]
        return [], sends, recvs

    sems = lambda k: [pltpu.SemaphoreType.DMA((k * n,)), pltpu.SemaphoreType.DMA((k * n,)),
                      pltpu.SemaphoreType.DMA((n,))]
    return between_chips, sems(4), within_chip, sems(3)


N_CHIPS = N_DEV // 2


def _pair_plans():
    def mesh_pos():
        return lax.axis_index("x"), lax.axis_index("y"), lax.axis_index("c")

    def to_sibling(ins, outs, send_sems, recv_sems, local_sems):
        x, y, c = mesh_pos()
        sib = 1 - c
        sends = [pltpu.make_async_remote_copy(
            src_ref=ins[0].at[2 * j + sib], dst_ref=outs[0].at[j], send_sem=send_sems.at[j],
            recv_sem=recv_sems.at[j], device_id=(x, y, sib), device_id_type=pl.DeviceIdType.MESH)
            for j in range(N_CHIPS)]
        return [], sends, sends

    def between_chips(ins, outs, send_sems, recv_sems, local_sems):
        x, y, c = mesh_pos()
        chip = 2 * x + y
        local = [pltpu.make_async_copy(ins[0].at[chip], outs[0].at[chip], local_sems.at[0])]
        sends, recvs = [], []
        for k in range(1, N_CHIPS):
            px = 1 - x if k & 2 else x
            py = 1 - y if k & 1 else y
            sems = dict(send_sem=send_sems.at[k - 1], recv_sem=recv_sems.at[k - 1], device_id=(px, py, c),
                        device_id_type=pl.DeviceIdType.MESH)
            sends.append(pltpu.make_async_remote_copy(src_ref=ins[0].at[2 * px + py], dst_ref=outs[0].at[chip], **sems))
            recvs.append(pltpu.make_async_remote_copy(src_ref=ins[0].at[2 * px + py], dst_ref=outs[0].at[2 * px + py],
                                                      **sems))
        return local, sends, recvs

    dma = lambda k: pltpu.SemaphoreType.DMA((k,))
    return to_sibling, [dma(N_CHIPS), dma(N_CHIPS), dma(1)], between_chips, [dma(N_CHIPS - 1), dma(N_CHIPS - 1), dma(1)]


def _pair_add(slabs, from_sibling, tr=W_IN_COLS):
    _, rows, cols = slabs.shape
    core = lax.axis_index("c").astype(jnp.int32).reshape(1)

    def body(core_ref, a_ref, b_ref, o_ref):
        o_ref[...] = (a_ref[...].astype(F32) + b_ref[...].astype(F32)).astype(o_ref.dtype)

    return pl.pallas_call(
        body,
        grid_spec=pltpu.PrefetchScalarGridSpec(
            num_scalar_prefetch=1, grid=(N_CHIPS, rows // tr),
            in_specs=[pl.BlockSpec((None, tr, cols), lambda j, i, core_ref: (2 * j + core_ref[0], i, 0)),
                      pl.BlockSpec((None, tr, cols), lambda j, i, core_ref: (j, i, 0))],
            out_specs=pl.BlockSpec((None, tr, cols), lambda j, i, core_ref: (j, i, 0))),
        out_shape=jax.ShapeDtypeStruct((N_CHIPS, rows, cols), slabs.dtype), name="pair_add",
        compiler_params=_params("arbitrary", "arbitrary"),
    )(core, slabs, from_sibling)


def _adam_math(g, w, m, v):
    m_new = ADAM_B1 * m + (1.0 - ADAM_B1) * g
    v_new = ADAM_B2 * v + (1.0 - ADAM_B2) * (g * g)
    m_hat = m_new / (1.0 - ADAM_B1 ** ADAM_STEP)
    v_hat = v_new / (1.0 - ADAM_B2 ** ADAM_STEP)
    return -ADAM_LR * (m_hat / (jnp.sqrt(v_hat) + ADAM_EPS) + ADAM_WD * w), m_new, v_new


def _adamw_big(parts, w, m, v, layer, prev, name, tr):
    depth, rows, cols = w.shape
    n_parts = parts.shape[0]

    def body(p_ref, w_ref, m_ref, v_ref, *rest):
        g_out, d_out, m_out, v_out = rest[len(prev):]
        g = p_ref[0].astype(F32)
        for q in range(1, n_parts):
            g = g + p_ref[q].astype(F32)
        d, m_new, v_new = _adam_math(g, w_ref[...], m_ref[...], v_ref[...])
        g_out[...] = g
        d_out[...] = d
        m_out[...] = m_new
        v_out[...] = v_new

    blk = pl.BlockSpec((None, tr, cols), lambda i: (layer, i, 0))
    shp = jax.ShapeDtypeStruct((depth, rows, cols), F32)
    return pl.pallas_call(
        body, grid=(rows // tr,),
        in_specs=[pl.BlockSpec((n_parts, tr, cols), lambda i: (0, i, 0)), blk, blk, blk]
        + [pl.BlockSpec(memory_space=pl.ANY)] * len(prev),
        out_specs=(blk,) * 4, out_shape=(shp,) * 4,
        input_output_aliases={4 + j: j for j in range(len(prev))},
        name=name, compiler_params=_params("arbitrary"),
    )(parts, w, m, v, *prev)


_SMALL_TENSORS = (("conv_a_w", "conv_a_w", None), ("cc_dw_w", "cc_dw_w", None), ("cc_pw_w", "cc_pw_w", None),
                  ("sg_w", "sg_w", None), ("pool_w", "pool_w", None), ("sg_b", "sg_b", None)) \
    + tuple((n, "g256", k) for k, n in enumerate(G256_ROWS)) + tuple((n, "g1024", k) for k, n in enumerate(G1024_ROWS))
_SMALL_LANDINGS = ("conv_a_w", "cc_dw_w", "cc_pw_w", "sg_w", "pool_w", "sg_b", "g256", "g1024")
_TAPS_FIRST = ("conv_a_w", "cc_dw_w")


def _adamw_small(landings, wts, mom, var):
    names = [n for n, _, _ in _SMALL_TENSORS]
    n_land = DEPTH * len(_SMALL_LANDINGS)
    n_t = len(names)

    def body(*refs):
        land = [dict(zip(_SMALL_LANDINGS, refs[l * len(_SMALL_LANDINGS):(l + 1) * len(_SMALL_LANDINGS)]))
                for l in range(DEPTH)]
        w_refs = dict(zip(names, refs[n_land:n_land + n_t]))
        m_refs = dict(zip(names, refs[n_land + n_t:n_land + 2 * n_t]))
        v_refs = dict(zip(names, refs[n_land + 2 * n_t:n_land + 3 * n_t]))
        outs = refs[n_land + 3 * n_t:]
        out_refs = {n: outs[4 * k:4 * k + 4] for k, n in enumerate(names)}
        loss_ref = outs[4 * n_t]
        for name, key, row in _SMALL_TENSORS:
            for l in range(DEPTH):
                src = land[l][key]
                if row is not None:
                    part = lambda q: src[q, row:row + 1, :]
                    at = (slice(l, l + 1),)
                elif name == "sg_b":
                    part = lambda q: src[q, 0:N_SUB, :]
                    at = (l,)
                elif name in _TAPS_FIRST:
                    part = lambda q: src[q]
                    at = (slice(None), l)
                else:
                    part = lambda q: src[q]
                    at = (l,)
                g = part(0).astype(F32)
                for q in range(1, N_DEV):
                    g = g + part(q).astype(F32)
                d, m_new, v_new = _adam_math(g, w_refs[name][at], m_refs[name][at], v_refs[name][at])
                for ref, val in zip(out_refs[name], (g, d, m_new, v_new)):
                    ref[at] = val
        src = land[DEPTH - 1]["g1024"]
        loss = src[0, LOSS_ROW:LOSS_ROW + 1, 0:128]
        for q in range(1, N_DEV):
            loss = loss + src[q, LOSS_ROW:LOSS_ROW + 1, 0:128]
        loss_ref[...] = loss

    ins = [landings[l][k] for l in range(DEPTH) for k in _SMALL_LANDINGS] \
        + [src[n] for src in (wts, mom, var) for n in names]
    out_shape = tuple(jax.ShapeDtypeStruct(wts[n].shape, F32) for n in names for _ in range(4)) \
        + (jax.ShapeDtypeStruct((1, 128), F32),)
    res = pl.pallas_call(body, out_shape=out_shape, name="adamw_small", compiler_params=_params())(*ins)
    return {n: res[4 * k:4 * k + 4] for k, n in enumerate(names)}, res[4 * n_t]


_BIG = (("w_in", 224), ("w_out", 80), ("w_kv", 64))
_GRAD_ITEMS_EARLY = ("w_out", "w_kv", "cc_pw_w", "conv_a_w", "cc_dw_w")
_GRAD_ITEMS_REPL = ("g256", "sg_w", "sg_b", "pool_w", "g1024")


def _grad_items(grads):
    items = [(grads[n], True, None) for n in _GRAD_ITEMS_EARLY]
    return items + [(grads[n], False, None) for n in _GRAD_ITEMS_REPL]


def _landed(parts):
    return dict(zip(_GRAD_ITEMS_EARLY + _GRAD_ITEMS_REPL, parts))


def _gathered_weights(wt_in8, w_kv8, w_out8, pw8, ca8, dw8):
    return dict(wt_in=wt_in8.reshape(D_IN, D_MODEL), w_kv=w_kv8.reshape(D_MODEL, 2 * D_G),
                w_out=w_out8.reshape(D_MIX, D_MODEL), pw=pw8.reshape(D_G, D_G), ca8=ca8, dw8=dw8)


def kernel(x, mem, w_in, conv_a_w, sg_ln_g, sg_ln_b, sg_w, sg_b, pool_w, pool_scale, cc_dw_w, cc_dw_b, cc_ln_g, cc_ln_b, cc_pw_w, w_kv, w_out, ln_g, ln_b, loss_target, m_w_in, m_conv_a_w, m_sg_ln_g, m_sg_ln_b, m_sg_w, m_sg_b, m_pool_w, m_pool_scale, m_cc_dw_w, m_cc_dw_b, m_cc_ln_g, m_cc_ln_b, m_cc_pw_w, m_w_kv, m_w_out, m_ln_g, m_ln_b, v_w_in, v_conv_a_w, v_sg_ln_g, v_sg_ln_b, v_sg_w, v_sg_b, v_pool_w, v_pool_scale, v_cc_dw_w, v_cc_dw_b, v_cc_ln_g, v_cc_ln_b, v_cc_pw_w, v_w_kv, v_w_out, v_ln_g, v_ln_b):
    names = ("w_in", "conv_a_w", "sg_ln_g", "sg_ln_b", "sg_w", "sg_b", "pool_w", "pool_scale", "cc_dw_w", "cc_dw_b",
             "cc_ln_g", "cc_ln_b", "cc_pw_w", "w_kv", "w_out", "ln_g", "ln_b")
    wts = dict(zip(names, (w_in, conv_a_w, sg_ln_g, sg_ln_b, sg_w, sg_b, pool_w, pool_scale, cc_dw_w, cc_dw_b,
                           cc_ln_g, cc_ln_b, cc_pw_w, w_kv, w_out, ln_g, ln_b)))
    mom = dict(zip(names, (m_w_in, m_conv_a_w, m_sg_ln_g, m_sg_ln_b, m_sg_w, m_sg_b, m_pool_w, m_pool_scale,
                           m_cc_dw_w, m_cc_dw_b, m_cc_ln_g, m_cc_ln_b, m_cc_pw_w, m_w_kv, m_w_out, m_ln_g, m_ln_b)))
    var = dict(zip(names, (v_w_in, v_conv_a_w, v_sg_ln_g, v_sg_ln_b, v_sg_w, v_sg_b, v_pool_w, v_pool_scale,
                           v_cc_dw_w, v_cc_dw_b, v_cc_ln_g, v_cc_ln_b, v_cc_pw_w, v_w_kv, v_w_out, v_ln_g, v_ln_b)))
    repl = wts
    xs, mems, tgt = x[0], mem[0], loss_target[0]
    turned = {"w_in": (0, 2, 1), "conv_a_w": (1, 0, 2), "cc_dw_w": (1, 0, 2)}
    wts, mom, var = [{n: (jnp.transpose(a, turned[n]) if n in turned else a) for n, a in src.items()}
                     for src in (wts, mom, var)]
    wb = {n: wts[n].astype(MM_DTYPE) for n in ("w_in", "w_kv", "w_out", "cc_pw_w")}

    wt8_0, wkv8_0, wo8_0, pw8_0, ca8, dw8 = _gather_two_level(
        [(wb["w_in"], False, 0), (wb["w_kv"], False, 0), (wb["w_out"], False, 0), (wb["cc_pw_w"], False, 0),
         (wts["conv_a_w"], False, None), (wts["cc_dw_w"], False, None)], "gather_weights_0")
    gw0 = _gathered_weights(wt8_0, wkv8_0, wo8_0, pw8_0, ca8, dw8)
    items_1 = [(wb[n], False, 1) for n in ("w_in", "w_kv", "w_out", "cc_pw_w")]
    between_chips, sems_a, within_chip, sems_b = _two_level_plans([1] * len(items_1))
    chips_1 = _split_start([a for a, _, _ in items_1], _empty_landings(items_1), between_chips, sems_a,
                           "gather_weights_1a_start")
    x1, saved0 = _layer_fwd(xs, mems, 0, repl, gw0, ties=(chips_1["token"],))
    core_1 = _split_start([], _split_wait(chips_1, (x1,), "gather_weights_1a_wait"), within_chip, sems_b,
                          "gather_weights_1b_start")
    gw1 = _gathered_weights(*_split_wait(core_1, (core_1["token"],), "gather_weights_1b_wait"), ca8, dw8)
    _, saved1 = _layer_fwd(x1, mems, 1, repl, gw1)

    dz1, dproj1, g1 = _layer_bwd_a(None, tgt, mems, 1, repl, gw1, saved1)
    shards = lambda g: g.reshape(N_DEV, W_IN_COLS, D_MODEL)
    early_1 = _exchange_start(_grad_items(g1), "exchange_grads_1a_start")
    up, g_wt_in_1 = _in_bwd(saved1[0], dproj1, gw1["wt_in"], dz1, (early_1["token"],))
    late_1 = _exchange_start([(shards(g_wt_in_1), True, None)], "exchange_grads_1b_start")
    dz0, dproj0, g0 = _layer_bwd_a(up, None, mems, 0, repl, gw0, saved0, (late_1["token"],))
    early_0 = _exchange_start(_grad_items(g0), "exchange_grads_0a_start")
    g_wt_in_0 = _dw_in_matmul(saved0[0], dproj0, (early_0["token"],))
    to_sibling, sems_s, between_chips, sems_c = _pair_plans()
    half = lambda: [lax.empty((N_CHIPS, W_IN_COLS, D_MODEL), GRAD_DTYPE)]
    pair_0 = _split_start([shards(g_wt_in_0)], half(), to_sibling, sems_s, "exchange_grads_0b_start")
    landed = [None, _landed(_split_wait(early_1, (pair_0["token"],), "exchange_grads_1a_wait"))]
    (landed[1]["w_in"],) = _split_wait(late_1, (pair_0["token"],), "exchange_grads_1b_wait")
    big = {}
    for n, tr in _BIG:
        big[n] = _adamw_big(landed[1][n], wts[n], mom[n], var[n], 1, (), "adamw_" + n + "_1", tr)
    own_slabs, from_sibling = _split_wait(pair_0, tuple(big[n][0] for n, _ in _BIG), "exchange_grads_0b_wait",
                                          with_srcs=True)
    late_0 = _split_start([_pair_add(own_slabs, from_sibling)], half(), between_chips, sems_c,
                          "exchange_grads_0c_start")
    grad_x = _dx_matmul(dproj0, gw0["wt_in"], dz0, (late_0["token"],))
    landed[0] = _landed(_split_wait(early_0, (grad_x,), "exchange_grads_0a_wait"))
    for n, tr in _BIG[1:]:
        big[n] = _adamw_big(landed[0][n], wts[n], mom[n], var[n], 0, big[n], "adamw_" + n + "_0", tr)
    small, loss = _adamw_small(landed, wts, mom, var)
    (landed[0]["w_in"],) = _split_wait(late_0, (loss, big["w_out"][0], big["w_kv"][0]), "exchange_grads_0c_wait")
    big["w_in"] = _adamw_big(landed[0]["w_in"], wts["w_in"], mom["w_in"], var["w_in"], 0, big["w_in"],
                             "adamw_w_in_0", _BIG[0][1])

    res = {**small, **big}
    res = {n: ([jnp.transpose(a, turned[n]) for a in r] if n in turned else r) for n, r in res.items()}
    return (loss[0, 0], grad_x[None], *[res[n][0] for n in names], *[res[n][1] for n in names],
            *[res[n][2] for n in names], *[res[n][3] for n in names])
```

```python
import math

import jax
import jax.numpy as jnp
from jax import lax
from jax.experimental import pallas as pl
from jax.experimental.pallas import tpu as pltpu

F32 = jnp.float32
MM_DTYPE = jnp.bfloat16
GRAD_DTYPE = jnp.bfloat16

D_MODEL = 1024
DEPTH = 2
D_G = 256
N_GROUPS = 5
D_MIX = N_GROUPS * D_G
N_SUB = 4
HEAD_DIM = D_G // N_SUB
CONV_A = 3
CONV_D = 31
CHUNK = 128
POOL_WINDOWS = (2, 4, 8, 16)
MEM_LEN = 256
LN_EPS = 1e-5
ALPHA = (2.0 * DEPTH) ** 0.25
D_IN = 9 * D_G + D_MIX
ATT_SCALE = 1.0 / math.sqrt(HEAD_DIM)

ADAM_LR = 0.001
ADAM_B1 = 0.9
ADAM_B2 = 0.999
ADAM_EPS = 1e-08
ADAM_WD = 0.01
ADAM_STEP = 10

N_DEV = 8
W_IN_COLS = D_IN // N_DEV
CONV_CH = D_G // N_DEV
HALO = 32
FWD_TILE = 512
BWD_TILE = 256
VMEM_LIMIT = 56 * 1024 * 1024

C_XA, C_BA, C_CA, C_U, C_V, C_XC, C_DA, C_DG, C_Q = range(9)
C_GATE = 9 * D_G

G256_ROWS = ("sg_ln_g", "sg_ln_b", "pool_scale", "cc_dw_b", "cc_ln_g", "cc_ln_b")
G1024_ROWS = ("ln_g", "ln_b")
LOSS_ROW = 2


def _mm(a, b):
    return jnp.dot(a.astype(MM_DTYPE), b.astype(MM_DTYPE), preferred_element_type=F32)


def _mm_nt(a, b):
    return lax.dot_general(a.astype(MM_DTYPE), b.astype(MM_DTYPE), (((1,), (1,)), ((), ())),
                           preferred_element_type=F32)


def _mm_tn(a, b):
    return lax.dot_general(a.astype(MM_DTYPE), b.astype(MM_DTYPE), (((0,), (0,)), ((), ())),
                           preferred_element_type=F32)


def _sigmoid(x):
    return 0.5 * jnp.tanh(0.5 * x) + 0.5


_GELU_C = math.sqrt(2.0 / math.pi)
_GELU_A = 0.044715


def _gelu(x):
    th = jnp.tanh(_GELU_C * (x + _GELU_A * (x * x * x)))
    return 0.5 * x * (1.0 + th), th


def _dgelu(x, th):
    return 0.5 * (1.0 + th) + 0.5 * x * (1.0 - th * th) * (_GELU_C * (1.0 + 3.0 * _GELU_A * (x * x)))


def _ln_fwd(x, g, b):
    mu = jnp.mean(x, axis=-1, keepdims=True)
    xc = x - mu
    var = jnp.mean(xc * xc, axis=-1, keepdims=True)
    rstd = lax.rsqrt(var + LN_EPS)
    xhat = xc * rstd
    return xhat * g + b, xhat, rstd


def _ln_bwd(dy, xhat, rstd, g):
    dxhat = dy * g
    m1 = jnp.mean(dxhat, axis=-1, keepdims=True)
    m2 = jnp.mean(dxhat * xhat, axis=-1, keepdims=True)
    return rstd * (dxhat - m1 - xhat * m2)


def _rowsum(x):
    return jnp.sum(x, axis=0, keepdims=True)


def _col(ref, k):
    return ref[:, k * D_G:(k + 1) * D_G]


def _head_of_lane(shape):
    return jnp.right_shift(lax.broadcasted_iota(jnp.int32, shape, len(shape) - 1), HEAD_DIM.bit_length() - 1)


def _pool_select(lane_grp, s2, s4, s8, s16):
    return jnp.where(lane_grp == 0, s2, jnp.where(lane_grp == 1, s4, jnp.where(lane_grp == 2, s8, s16)))


def _row_view(ref, layer):
    return ref.at[pl.ds(layer, 1)]


def _make_residues(ext_ref, res_ref):
    rows = res_ref.shape[1]
    for r in range(1, 8):
        res_ref[r - 1] = ext_ref[pl.ds(r, rows), :]


def _rows_at(ext_ref, res_ref, off, tile):
    a, r = divmod(off, 8)
    if r == 0:
        return ext_ref[pl.ds(off, tile), :]
    return res_ref[r - 1, pl.ds(8 * a, tile), :]


def _residue_scratch(tile):
    return pltpu.VMEM((7, HALO + tile - 8, D_G), F32)


PROJ_SEGMENTS = ((C_XA * D_G, (C_CA + 1) * D_G), (C_XC * D_G, (C_XC + 1) * D_G), (C_DA * D_G, (C_DG + 1) * D_G),
                 (C_U * D_G, (C_V + 1) * D_G), (C_Q * D_G, (C_Q + 1) * D_G), (C_GATE, D_IN))


def _branch_forward(p_ref, ph_ref, first, row0, km_ref, vm_ref, w, ext_a, ext_c, ext_d, res_c, res_d, tile,
                    kept_ref=None, produce=None):
    r = {}
    produce = produce or (lambda: None)
    produce()
    produce()
    xa, ba, ca = _col(p_ref, C_XA), _col(p_ref, C_BA), _col(p_ref, C_CA)
    g_a = ca * xa
    ext_a[0:HALO] = jnp.where(first, 0.0, _col(ph_ref, C_CA) * _col(ph_ref, C_XA))
    ext_a[HALO:HALO + tile] = g_a
    conv_a = w["conv_a"][0:1, :] * ext_a[pl.ds(HALO - 2, tile), :]
    for k in range(1, CONV_A):
        conv_a = conv_a + w["conv_a"][k:k + 1, :] * ext_a[pl.ds(HALO - 2 + k, tile), :]
    r.update(xa=xa, ba=ba, ca=ca, g_a=g_a, conv_a=conv_a)
    ya = ba * conv_a

    produce()
    lane_grp = _head_of_lane((tile, D_G))
    trow = row0 + lax.broadcasted_iota(jnp.int32, (tile, D_G), 0)
    win = _pool_select(lane_grp, 2, 4, 8, 16)
    inv_cnt = 1.0 / jnp.minimum(trow + 1, win).astype(F32)
    if kept_ref is None:
        xc = _col(p_ref, C_XC)
        ext_c[0:HALO] = jnp.where(first, 0.0, _col(ph_ref, C_XC))
        ext_c[HALO:HALO + tile] = xc
        _make_residues(ext_c, res_c)
        acc = xc
        sums = {}
        for k in range(1, POOL_WINDOWS[-1]):
            acc = acc + _rows_at(ext_c, res_c, HALO - k, tile)
            if k + 1 in POOL_WINDOWS:
                sums[k + 1] = acc
        ypre = _pool_select(lane_grp, sums[2], sums[4], sums[8], sums[16]) * inv_cnt - xc
    else:
        ypre = kept_ref[:, D_G:2 * D_G]
    pool_mm = _mm(ypre, w["pool_wbd"][...])
    yc = pool_mm * w["pool_scale"][...]
    r.update(lane_grp=lane_grp, inv_cnt=inv_cnt, ypre=ypre, pool_mm=pool_mm)

    produce()
    da, dg = _col(p_ref, C_DA), _col(p_ref, C_DG)
    sig_dg = _sigmoid(dg)
    hd = da * sig_dg
    if kept_ref is None:
        ext_d[0:HALO] = jnp.where(first, 0.0, _col(ph_ref, C_DA) * _sigmoid(_col(ph_ref, C_DG)))
        ext_d[HALO:HALO + tile] = hd
        _make_residues(ext_d, res_d)
        conv_d = w["cc_dw_b"][...] + w["cc_dw_w"][0:1, :] * _rows_at(ext_d, res_d, HALO - (CONV_D - 1), tile)
        for j in range(1, CONV_D):
            conv_d = conv_d + w["cc_dw_w"][j:j + 1, :] * _rows_at(ext_d, res_d, HALO - (CONV_D - 1) + j, tile)
    else:
        conv_d = kept_ref[:, 0:D_G]
    r["kept"] = (conv_d, ypre)
    ln_d, xhat_d, rstd_d = _ln_fwd(conv_d, w["cc_ln_g"][...], w["cc_ln_b"][...])
    sig_ln = _sigmoid(ln_d)
    act_d = ln_d * sig_ln
    yd = _mm(act_d, w["cc_pw_w"][...])
    r.update(da=da, sig_dg=sig_dg, hd=hd, ln_d=ln_d, xhat_d=xhat_d, rstd_d=rstd_d, sig_ln=sig_ln, act_d=act_d)
    produce()

    u, v = _col(p_ref, C_U), _col(p_ref, C_V)
    ug, th_u = _gelu(u)
    vg, th_v = _gelu(v)
    vn, xhat_v, rstd_v = _ln_fwd(vg, w["sg_ln_g"][...], w["sg_ln_b"][...])
    tri = (lax.broadcasted_iota(jnp.int32, (CHUNK, CHUNK), 0)
           >= lax.broadcasted_iota(jnp.int32, (CHUNK, CHUNK), 1))
    wm = [jnp.where(tri, w["sg_w"][h], 0.0).astype(MM_DTYPE) for h in range(N_SUB)]
    lo = lax.broadcasted_iota(jnp.int32, (CHUNK, 2 * HEAD_DIM), 1) < HEAD_DIM
    chunks = []
    for c in range(tile // CHUNK):
        halves = []
        for hf in range(2):
            vh = vn[c * CHUNK:(c + 1) * CHUNK, hf * 128:(hf + 1) * 128]
            halves.append(_mm(wm[2 * hf], jnp.where(lo, vh, 0.0)) + _mm(wm[2 * hf + 1], jnp.where(lo, 0.0, vh)))
        chunks.append(jnp.concatenate(halves, axis=1) + w["sg_bias"][...])
    mixed = jnp.concatenate(chunks, axis=0)
    yb = ug * mixed
    r.update(u=u, v=v, ug=ug, th_u=th_u, th_v=th_v, vn=vn, xhat_v=xhat_v, rstd_v=rstd_v, wm=wm, lo=lo,
             mixed=mixed, tri=tri)

    produce()
    q = _col(p_ref, C_Q)
    ye = jnp.zeros((tile, D_G), F32)
    probs = []
    for h in range(N_SUB):
        s = _mm_nt(q, km_ref[h]) * ATT_SCALE
        e = jnp.exp(s - jnp.max(s, axis=-1, keepdims=True))
        p = e * (1.0 / jnp.sum(e, axis=-1, keepdims=True))
        probs.append(p)
        ye = ye + _mm(p, vm_ref[h])
    r.update(q=q, probs=probs)

    gate = p_ref[:, C_GATE:C_GATE + D_MIX]
    sig_gate = _sigmoid(gate)
    r.update(gate=gate, sig_gate=sig_gate, branch_out=(ya, yb, yc, yd, ye))
    return r


_BRANCH_REPL = ("sg_ln_g", "sg_ln_b", "sg_w", "sg_b", "pool_w", "pool_scale", "cc_dw_b", "cc_ln_g", "cc_ln_b")
_BRANCH_W_SCRATCH = (("conv_a", (CONV_A, D_G)), ("cc_dw_w", (CONV_D, D_G)), ("sg_bias", (CHUNK, D_G)),
                     ("pool_wbd", (D_G, D_G)), ("sgb8", (8, CHUNK)))


def _branch_weights(layer, nat, pw_ref, ca_ref, dw_ref, scr, init):
    @pl.when(init)
    def _():
        for p in range(N_DEV):
            scr["conv_a"][:, p * CONV_CH:(p + 1) * CONV_CH] = ca_ref[p, :, layer, :]
            scr["cc_dw_w"][:, p * CONV_CH:(p + 1) * CONV_CH] = dw_ref[p, :, layer, :]
        scr["sgb8"][...] = jnp.zeros((8, CHUNK), F32)
        scr["sgb8"][0:N_SUB] = nat["sg_b"][layer]
        sgb_t = scr["sgb8"][...].T
        head = _head_of_lane((CHUNK, D_G))
        bias = jnp.zeros((CHUNK, D_G), F32)
        for h in range(N_SUB):
            bias = jnp.where(head == h, sgb_t[:, h:h + 1], bias)
        scr["sg_bias"][...] = bias
        scr["pool_wbd"][...] = jnp.zeros((D_G, D_G), F32)
        for gi in range(N_SUB):
            sl = slice(gi * HEAD_DIM, (gi + 1) * HEAD_DIM)
            scr["pool_wbd"][sl, sl] = nat["pool_w"][layer, gi]

    w = {n: _row_view(nat[n], layer) for n in ("sg_ln_g", "sg_ln_b", "pool_scale", "cc_dw_b", "cc_ln_g", "cc_ln_b")}
    w.update(conv_a=scr["conv_a"], cc_dw_w=scr["cc_dw_w"], sg_bias=scr["sg_bias"], pool_wbd=scr["pool_wbd"],
             sg_w=nat["sg_w"].at[layer], cc_pw_w=pw_ref)
    return w


def _full_spec(a):
    nd = a.ndim
    return pl.BlockSpec(a.shape, lambda *_, _nd=nd: (0,) * _nd)


def _tie_specs(ties):
    return [pl.BlockSpec((8, 128), lambda *_: (0, 0)) for _ in ties]


def _params(*sem):
    return pltpu.CompilerParams(dimension_semantics=sem or None, vmem_limit_bytes=VMEM_LIMIT)


def _kv_project(mem, w_kv):
    def body(mem_ref, w_ref, km_ref, vm_ref):
        kv = _mm(mem_ref[...], w_ref[...])
        k, v = kv[:, :D_G], kv[:, D_G:]
        grp = _head_of_lane((MEM_LEN, D_G))
        for h in range(N_SUB):
            km_ref[h] = jnp.where(grp == h, k, 0.0).astype(km_ref.dtype)
            vm_ref[h] = jnp.where(grp == h, v, 0.0).astype(vm_ref.dtype)

    shp = jax.ShapeDtypeStruct((N_SUB, MEM_LEN, D_G), MM_DTYPE)
    return pl.pallas_call(body, out_shape=(shp, shp), name="kv_project", compiler_params=_params())(mem, w_kv)


def _layer_fwd_fused(x, wt_in, km, vm, layer, repl, pw, ca8, dw8, w_out, want_xn, ties=(), tile=FWD_TILE):
    s = x.shape[0]
    nat_arrays = [repl[n] for n in _BRANCH_REPL]
    n_nat, nt = len(nat_arrays), len(ties)

    def body(x_ref, wt_ref, km_ref, vm_ref, *rest):
        nat = dict(zip(_BRANCH_REPL, rest[:n_nat]))
        pw_ref, ca_ref, dw_ref, wo_ref, g_ref, b_ref = rest[n_nat:n_nat + 6]
        rest = rest[n_nat + 6 + nt:]
        p_ref, h_ref, z_ref, cd_ref = rest[:4]
        rest = rest[4:]
        if want_xn:
            xn_ref, rest = rest[0], rest[1:]
        ph_ref, ext_a, ext_c, ext_d, res_c, res_d = rest[:6]
        scr = dict(zip([n for n, _ in _BRANCH_W_SCRATCH], rest[6:]))
        i = pl.program_id(0)

        @pl.when(i == 0)
        def _():
            ph_ref[...] = jnp.zeros_like(ph_ref)

        xt = x_ref[...]
        xb = xt.astype(MM_DTYPE)
        segments = iter(PROJ_SEGMENTS)

        def produce():
            lo, hi = next(segments)
            p_ref[:, lo:hi] = _mm_nt(xb, wt_ref[lo:hi, :])

        w = _branch_weights(layer, nat, pw_ref, ca_ref, dw_ref, scr, i == 0)
        r = _branch_forward(p_ref, ph_ref, i == 0, i * tile, km_ref, vm_ref, w, ext_a, ext_c, ext_d, res_c, res_d,
                            tile, None, produce)
        ph_ref[...] = p_ref[tile - HALO:tile, :]
        cd_ref[:, 0:D_G], cd_ref[:, D_G:2 * D_G] = r["kept"]
        h = (jnp.concatenate(r["branch_out"], axis=1) * (r["gate"] * r["sig_gate"])).astype(h_ref.dtype)
        h_ref[...] = h
        z = ALPHA * xt + _mm(h, wo_ref[...])
        z_ref[...] = z
        if want_xn:
            xn_ref[...] = _ln_fwd(z, _row_view(g_ref, layer)[...], _row_view(b_ref, layer)[...])[0]

    row = lambda i: (i, 0)
    consts = [wt_in, km, vm] + nat_arrays + [pw, ca8, dw8, w_out, repl["ln_g"], repl["ln_b"]]
    act = jax.ShapeDtypeStruct((s, D_MODEL), F32)
    act_spec = pl.BlockSpec((tile, D_MODEL), row)
    res = pl.pallas_call(
        body, grid=(s // tile,),
        in_specs=[act_spec] + [_full_spec(a) for a in consts] + _tie_specs(ties),
        out_specs=(pl.BlockSpec((tile, D_IN), row), pl.BlockSpec((tile, D_MIX), row), act_spec,
                   pl.BlockSpec((tile, 2 * D_G), row)) + ((act_spec,) if want_xn else ()),
        out_shape=(jax.ShapeDtypeStruct((s, D_IN), F32), jax.ShapeDtypeStruct((s, D_MIX), MM_DTYPE), act,
                   jax.ShapeDtypeStruct((s, 2 * D_G), F32)) + ((act,) if want_xn else ()),
        scratch_shapes=[pltpu.VMEM((HALO, D_IN), F32)] + [pltpu.VMEM((HALO + tile, D_G), F32)] * 3
        + [_residue_scratch(tile)] * 2 + [pltpu.VMEM(shape, F32) for _, shape in _BRANCH_W_SCRATCH],
        name="layer_fwd", compiler_params=_params("arbitrary"),
    )(x, *consts, *ties)
    return tuple(res) if want_xn else tuple(res) + (None,)


_BRANCH_GRADS = (("g256", (8, D_G)), ("sg_w", (N_SUB, CHUNK, CHUNK)), ("sg_b", (8, CHUNK)),
                 ("pool_w", (N_SUB, HEAD_DIM, HEAD_DIM)), ("conv_a_w", (N_DEV, CONV_A, CONV_CH)),
                 ("cc_dw_w", (N_DEV, CONV_D, CONV_CH)), ("cc_pw_w", (D_G, D_G)),
                 ("dk", (N_SUB, MEM_LEN, D_G)), ("dv", (N_SUB, MEM_LEN, D_G)))
_BRANCH_ACC = (("conv_a", (CONV_A, D_G)), ("cc_dw_w", (CONV_D, D_G)), ("pool_wbd", (D_G, D_G)),
               ("sg_bias", (CHUNK, D_G)), ("sg_w", (N_SUB, CHUNK, CHUNK)))
_BRANCH_GRADS_NARROW = ("sg_w", "pool_w")


def _layer_bwd_fused(up, target, z, h, proj, kept, km, vm, layer, repl, w_out, pw, ca8, dw8, ties=(), tile=BWD_TILE):
    s = proj.shape[0]
    nt = s // tile
    hb = tile // HALO
    nat_arrays = [repl[n] for n in _BRANCH_REPL]
    n_nat, n_grads, n_acc, n_ties = len(nat_arrays), len(_BRANCH_GRADS), len(_BRANCH_ACC), len(ties)
    row_of = {n: k for k, n in enumerate(G256_ROWS)}
    from_loss = target is not None

    def body(o_ref, z_ref, h_ref, p_ref, ph_ref, cd_ref, km_ref, vm_ref, *rest):
        nat = dict(zip(_BRANCH_REPL, rest[:n_nat]))
        pw_ref, ca_ref, dw_ref, lng_ref, lnb_ref, wo_ref = rest[n_nat:n_nat + 6]
        rest = rest[n_nat + 6 + n_ties:]
        dz_ref, dp_ref, gw_ref, slab_ref = rest[:4]
        g = dict(zip([n for n, _ in _BRANCH_GRADS], rest[4:4 + n_grads]))
        rest = rest[4 + n_grads:]
        ext_a, rev_a, rev_c, rev_d, res_rc, res_rd, gw_acc, lacc = rest[:8]
        acc = dict(zip([n for n, _ in _BRANCH_ACC], rest[8:8 + n_acc]))
        scr = dict(zip([n for n, _ in _BRANCH_W_SCRATCH], rest[8 + n_acc:]))
        i = pl.program_id(0)
        t = nt - 1 - i

        @pl.when(i == 0)
        def _():
            for ref in list(g.values()) + list(acc.values()) + [rev_a, rev_c, rev_d, gw_acc, slab_ref, lacc]:
                ref[...] = jnp.zeros_like(ref)

        g_ln = _row_view(lng_ref, layer)[...]
        xn, xhat, rstd = _ln_fwd(z_ref[...], g_ln, _row_view(lnb_ref, layer)[...])
        if from_loss:
            err = xn - o_ref[...]
            lacc[...] += _rowsum(err * err)
            dxn = err * (1.0 / D_MODEL)
        else:
            dxn = o_ref[...]
        slab_ref[0:1, :] += _rowsum(dxn * xhat)
        slab_ref[1:2, :] += _rowsum(dxn)
        dz = _ln_bwd(dxn, xhat, rstd, g_ln)
        dz_ref[...] = dz
        dzb = dz.astype(MM_DTYPE)

        w = _branch_weights(layer, nat, pw_ref, ca_ref, dw_ref, scr, i == 0)
        r = _branch_forward(p_ref, ph_ref, t == 0, t * tile, km_ref, vm_ref, w, ext_a, None, None, None, None,
                            tile, cd_ref)

        def put(k, val, width=D_G):
            dp_ref[:, k:k + width] = val.astype(dp_ref.dtype)

        def add_row(name, val):
            k = row_of[name]
            g["g256"][k:k + 1, :] += val

        def push_rev(rev, val):
            head = rev[0:HALO]
            rev[tile:tile + HALO] = head
            rev[0:tile] = val

        gate, sig_gate = r["gate"], r["sig_gate"]

        def branch_grad(group):
            cols = slice(group * D_G, (group + 1) * D_G)
            dh_g = _mm_nt(dzb, wo_ref[cols, :])
            gate_g, sig_g = gate[:, cols], sig_gate[:, cols]
            put(C_GATE + group * D_G, dh_g * r["branch_out"][group] * (sig_g * (1.0 + gate_g * (1.0 - sig_g))))
            return dh_g * (gate_g * sig_g)

        dya = branch_grad(0)
        dyc = branch_grad(2)

        put(C_BA * D_G, dya * r["conv_a"])
        dconv_a = dya * r["ba"]
        push_rev(rev_a, dconv_a)
        dga = jnp.zeros((tile, D_G), F32)
        for k in range(CONV_A):
            ahead = rev_a[pl.ds(CONV_A - 1 - k, tile), :]
            dga = dga + w["conv_a"][k:k + 1, :] * ahead
            acc["conv_a"][k:k + 1, :] += _rowsum(r["g_a"] * ahead)
        put(C_CA * D_G, dga * r["xa"])
        put(C_XA * D_G, dga * r["ca"])

        dyd = branch_grad(3)
        add_row("pool_scale", _rowsum(dyc * r["pool_mm"]))
        dmm = dyc * w["pool_scale"][...]
        acc["pool_wbd"][...] += _mm_tn(r["ypre"], dmm)
        dypre = _mm_nt(dmm, w["pool_wbd"][...])
        dws = dypre * r["inv_cnt"]
        push_rev(rev_c, dws)
        _make_residues(rev_c, res_rc)
        run = dws
        sums = {}
        for k in range(1, POOL_WINDOWS[-1]):
            run = run + _rows_at(rev_c, res_rc, k, tile)
            if k + 1 in POOL_WINDOWS:
                sums[k + 1] = run
        put(C_XC * D_G, _pool_select(r["lane_grp"], sums[2], sums[4], sums[8], sums[16]) - dypre)

        dyb = branch_grad(1)
        gw_acc[...] += _mm_tn(h_ref[...], dzb)
        g["cc_pw_w"][...] += _mm_tn(r["act_d"], dyd)
        dact = _mm_nt(dyd, w["cc_pw_w"][...])
        sig_ln, ln_d = r["sig_ln"], r["ln_d"]
        dln = dact * (sig_ln * (1.0 + ln_d * (1.0 - sig_ln)))
        add_row("cc_ln_g", _rowsum(dln * r["xhat_d"]))
        add_row("cc_ln_b", _rowsum(dln))
        dconv_d = _ln_bwd(dln, r["xhat_d"], r["rstd_d"], w["cc_ln_g"][...])
        add_row("cc_dw_b", _rowsum(dconv_d))
        push_rev(rev_d, dconv_d)
        _make_residues(rev_d, res_rd)
        dhd = jnp.zeros((tile, D_G), F32)
        for j in range(CONV_D):
            ahead = _rows_at(rev_d, res_rd, CONV_D - 1 - j, tile)
            dhd = dhd + w["cc_dw_w"][j:j + 1, :] * ahead
            acc["cc_dw_w"][j:j + 1, :] += _rowsum(r["hd"] * ahead)
        sig_dg = r["sig_dg"]
        put(C_DA * D_G, dhd * sig_dg)
        put(C_DG * D_G, dhd * r["da"] * sig_dg * (1.0 - sig_dg))

        dye = branch_grad(4)
        dug = dyb * r["mixed"]
        dmixed = dyb * r["ug"]
        wm, lo, vn = r["wm"], r["lo"], r["vn"]
        dvn_chunks = []
        for c in range(tile // CHUNK):
            rows = slice(c * CHUNK, (c + 1) * CHUNK)
            acc["sg_bias"][...] += dmixed[rows, :]
            halves = []
            for hf in range(2):
                cols = slice(hf * 128, (hf + 1) * 128)
                dm = dmixed[rows, cols]
                dm_a, dm_b = jnp.where(lo, dm, 0.0), jnp.where(lo, 0.0, dm)
                vh = vn[rows, cols]
                acc["sg_w"][2 * hf] += _mm_nt(dm_a, vh)
                acc["sg_w"][2 * hf + 1] += _mm_nt(dm_b, vh)
                halves.append(_mm_tn(wm[2 * hf], dm_a) + _mm_tn(wm[2 * hf + 1], dm_b))
            dvn_chunks.append(jnp.concatenate(halves, axis=1))
        dvn = jnp.concatenate(dvn_chunks, axis=0)
        add_row("sg_ln_g", _rowsum(dvn * r["xhat_v"]))
        add_row("sg_ln_b", _rowsum(dvn))
        dvg = _ln_bwd(dvn, r["xhat_v"], r["rstd_v"], w["sg_ln_g"][...])
        put(C_V * D_G, dvg * _dgelu(r["v"], r["th_v"]))
        put(C_U * D_G, dug * _dgelu(r["u"], r["th_u"]))

        q = r["q"]
        dq = jnp.zeros((tile, D_G), F32)
        for h in range(N_SUB):
            p = r["probs"][h]
            dp = _mm_nt(dye, vm_ref[h])
            g["dv"][h] += _mm_tn(p, dye)
            ds = p * (dp - jnp.sum(dp * p, axis=-1, keepdims=True)) * ATT_SCALE
            dq = dq + _mm(ds, km_ref[h])
            g["dk"][h] += _mm_tn(ds, q)
        put(C_Q * D_G, dq)

        @pl.when(i == nt - 1)
        def _():
            for h in range(N_SUB):
                g["sg_w"][h] = jnp.where(r["tri"], acc["sg_w"][h], 0.0).astype(g["sg_w"].dtype)
            lane_head = _head_of_lane((CHUNK, D_G))
            col_of = lax.broadcasted_iota(jnp.int32, (CHUNK, 8), 1)
            ba = acc["sg_bias"][...]
            sgb_t = jnp.zeros((CHUNK, 8), F32)
            for h in range(N_SUB):
                col = jnp.sum(jnp.where(lane_head == h, ba, 0.0), axis=-1, keepdims=True)
                sgb_t = jnp.where(col_of == h, col, sgb_t)
            g["sg_b"][...] = sgb_t.T
            wbd = acc["pool_wbd"][...]
            for gi in range(N_SUB):
                sl = slice(gi * HEAD_DIM, (gi + 1) * HEAD_DIM)
                g["pool_w"][gi] = wbd[sl, sl].astype(g["pool_w"].dtype)
            ca, dw = acc["conv_a"][...], acc["cc_dw_w"][...]
            for p in range(N_DEV):
                g["conv_a_w"][p] = ca[:, p * CONV_CH:(p + 1) * CONV_CH]
                g["cc_dw_w"][p] = dw[:, p * CONV_CH:(p + 1) * CONV_CH]
            gw_ref[...] = gw_acc[...].astype(gw_ref.dtype)
            if from_loss:
                total = jnp.sum(lacc[...], axis=-1, keepdims=True) * (0.5 / D_MODEL)
                slab_ref[LOSS_ROW:LOSS_ROW + 1, :] = jnp.broadcast_to(total, (1, D_MODEL))

    rev = lambda i: (nt - 1 - i, 0)
    fixed = lambda i: (0, 0)
    act_spec = pl.BlockSpec((tile, D_MODEL), rev)
    grad_specs = tuple(pl.BlockSpec(shape, lambda i, _nd=len(shape): (0,) * _nd) for _, shape in _BRANCH_GRADS)
    grad_shapes = tuple(jax.ShapeDtypeStruct(shape, GRAD_DTYPE if n in _BRANCH_GRADS_NARROW else F32)
                        for n, shape in _BRANCH_GRADS)
    consts = [km, vm] + nat_arrays + [pw, ca8, dw8, repl["ln_g"], repl["ln_b"], w_out]
    outs = pl.pallas_call(
        body, grid=(nt,),
        in_specs=[act_spec, act_spec, pl.BlockSpec((tile, D_MIX), rev), pl.BlockSpec((tile, D_IN), rev),
                  pl.BlockSpec((HALO, D_IN), lambda i: (jnp.maximum((nt - 1 - i) * hb - 1, 0), 0)),
                  pl.BlockSpec((tile, 2 * D_G), rev)]
        + [_full_spec(a) for a in consts] + _tie_specs(ties),
        out_specs=(act_spec, pl.BlockSpec((tile, D_IN), rev), pl.BlockSpec((D_MIX, D_MODEL), fixed),
                   pl.BlockSpec((8, D_MODEL), fixed)) + grad_specs,
        out_shape=(jax.ShapeDtypeStruct((s, D_MODEL), F32), jax.ShapeDtypeStruct((s, D_IN), MM_DTYPE),
                   jax.ShapeDtypeStruct((D_MIX, D_MODEL), GRAD_DTYPE), jax.ShapeDtypeStruct((8, D_MODEL), F32))
        + grad_shapes,
        scratch_shapes=[pltpu.VMEM((HALO + tile, D_G), F32)] * 4 + [_residue_scratch(tile)] * 2
        + [pltpu.VMEM((D_MIX, D_MODEL), F32), pltpu.VMEM((1, D_MODEL), F32)]
        + [pltpu.VMEM(shape, F32) for _, shape in _BRANCH_ACC + _BRANCH_W_SCRATCH],
        name="layer_bwd_loss" if from_loss else "layer_bwd", compiler_params=_params("arbitrary"),
    )(target if from_loss else up, z, h, proj, proj, kept, *consts, *ties)
    return outs[0], outs[1], outs[2], outs[3], dict(zip([n for n, _ in _BRANCH_GRADS], outs[4:]))


def _dx_matmul(dproj, wt_in, dz, ties=(), tm=512):
    s = dproj.shape[0]

    def body(dp_ref, w_ref, dz_ref, *rest):
        o_ref = rest[len(ties)]
        o_ref[...] = _mm(dp_ref[...], w_ref[...]) + ALPHA * dz_ref[...]

    row = lambda i: (i, 0)
    return pl.pallas_call(
        body, grid=(s // tm,),
        in_specs=[pl.BlockSpec((tm, D_IN), row), _full_spec(wt_in), pl.BlockSpec((tm, D_MODEL), row)]
        + _tie_specs(ties),
        out_specs=pl.BlockSpec((tm, D_MODEL), row),
        out_shape=jax.ShapeDtypeStruct((s, D_MODEL), F32), name="dx_mm", compiler_params=_params("arbitrary"),
    )(dproj, wt_in, dz, *ties)


def _dw_in_matmul(x, dproj, ties=(), tk=512):
    s = x.shape[0]
    nk = s // tk
    blk = 2 * W_IN_COLS

    def body(x_ref, dp_ref, *rest):
        o_ref, acc = rest[len(ties):]
        k = pl.program_id(0)

        @pl.when(k == 0)
        def _():
            acc[...] = jnp.zeros_like(acc)

        xb = x_ref[...].astype(MM_DTYPE)
        for j in range(D_IN // blk):
            acc[j * blk:(j + 1) * blk, :] += _mm_tn(dp_ref[:, j * blk:(j + 1) * blk], xb)

        @pl.when(k == nk - 1)
        def _():
            o_ref[...] = acc[...].astype(o_ref.dtype)

    return pl.pallas_call(
        body, grid=(nk,),
        in_specs=[pl.BlockSpec((tk, D_MODEL), lambda k: (k, 0)), pl.BlockSpec((tk, D_IN), lambda k: (k, 0))]
        + _tie_specs(ties),
        out_specs=pl.BlockSpec((D_IN, D_MODEL), lambda k: (0, 0)),
        out_shape=jax.ShapeDtypeStruct((D_IN, D_MODEL), GRAD_DTYPE),
        scratch_shapes=[pltpu.VMEM((D_IN, D_MODEL), F32)], name="dw_in_mm", compiler_params=_params("arbitrary"),
    )(x, dproj, *ties)


def _in_bwd(x, dproj, wt_in, dz, ties=(), tm=512):
    s = x.shape[0]
    n_steps = s // tm

    assert wt_in.dtype == GRAD_DTYPE
    blk = 2 * W_IN_COLS

    def body(x_ref, dp_ref, w_hbm, dz_ref, *rest):
        o_ref, gw_hbm, w_vmem, acc, sem = rest[len(ties):]
        i = pl.program_id(0)

        @pl.when(i == 0)
        def _():
            fetch = pltpu.make_async_copy(w_hbm, w_vmem, sem)
            fetch.start()
            acc[...] = jnp.zeros_like(acc)
            fetch.wait()

        o_ref[...] = _mm(dp_ref[...], w_vmem[...]) + ALPHA * dz_ref[...]
        xb = x_ref[...].astype(MM_DTYPE)
        for j in range(D_IN // blk):
            acc[j * blk:(j + 1) * blk, :] += _mm_tn(dp_ref[:, j * blk:(j + 1) * blk], xb)

        @pl.when(i == n_steps - 1)
        def _():
            w_vmem[...] = acc[...].astype(w_vmem.dtype)
            emit = pltpu.make_async_copy(w_vmem, gw_hbm, sem)
            emit.start()
            emit.wait()

    row = lambda i: (i, 0)
    any_spec = pl.BlockSpec(memory_space=pl.ANY)
    return pl.pallas_call(
        body, grid=(n_steps,),
        in_specs=[pl.BlockSpec((tm, D_MODEL), row), pl.BlockSpec((tm, D_IN), row), any_spec,
                  pl.BlockSpec((tm, D_MODEL), row)] + _tie_specs(ties),
        out_specs=(pl.BlockSpec((tm, D_MODEL), row), any_spec),
        out_shape=(jax.ShapeDtypeStruct((s, D_MODEL), F32), jax.ShapeDtypeStruct((D_IN, D_MODEL), GRAD_DTYPE)),
        scratch_shapes=[pltpu.VMEM((D_IN, D_MODEL), wt_in.dtype), pltpu.VMEM((D_IN, D_MODEL), F32),
                        pltpu.SemaphoreType.DMA],
        name="in_bwd", compiler_params=_params("arbitrary"),
    )(x, dproj, wt_in, dz, *ties)


def _kv_bwd(mem, dk, dv):
    def body(mem_ref, dk_ref, dv_ref, o_ref):
        grp = _head_of_lane((MEM_LEN, D_G))
        dk_sum = jnp.zeros((MEM_LEN, D_G), F32)
        dv_sum = jnp.zeros((MEM_LEN, D_G), F32)
        for h in range(N_SUB):
            dk_sum = dk_sum + jnp.where(grp == h, dk_ref[h], 0.0)
            dv_sum = dv_sum + jnp.where(grp == h, dv_ref[h], 0.0)
        o_ref[...] = _mm_tn(mem_ref[...], jnp.concatenate([dk_sum, dv_sum], axis=1)).astype(o_ref.dtype)

    return pl.pallas_call(body, out_shape=jax.ShapeDtypeStruct((D_MODEL, 2 * D_G), GRAD_DTYPE), name="kv_bwd",
                          compiler_params=_params())(mem, dk, dv)


def _layer_fwd(x, mem, layer, repl, gw, ties=()):
    km, vm = _kv_project(mem, gw["w_kv"])
    proj, h, z, kept, xn = _layer_fwd_fused(x, gw["wt_in"], km, vm, layer, repl, gw["pw"], gw["ca8"], gw["dw8"],
                                            gw["w_out"], layer < DEPTH - 1, ties)
    return xn, (x, proj, h, z, km, vm, kept)


def _layer_bwd_a(up, target, mem, layer, repl, gw, saved, ties=()):
    x_in, proj, h, z, km, vm, kept = saved
    dz, dproj, g_w_out, g1024, bg = _layer_bwd_fused(up, target, z, h, proj, kept, km, vm, layer, repl, gw["w_out"],
                                                     gw["pw"], gw["ca8"], gw["dw8"], ties)
    grads = {n: bg[n] for n in ("g256", "sg_w", "sg_b", "pool_w", "conv_a_w", "cc_dw_w")}
    grads.update(w_out=g_w_out.reshape(N_DEV, D_MIX // N_DEV, D_MODEL), g1024=g1024,
                 w_kv=_kv_bwd(mem, bg["dk"], bg["dv"]).reshape(N_DEV, D_MODEL // N_DEV, 2 * D_G),
                 cc_pw_w=bg["cc_pw_w"].reshape(N_DEV, CONV_CH, D_G))
    return dz, dproj, grads


def _landing_shapes(items):
    out = []
    for a, scatter, pick in items:
        shape = a.shape if scatter else (N_DEV,) + (a.shape if pick is None else a.shape[1:])
        out.append(jax.ShapeDtypeStruct(shape, a.dtype))
    return tuple(out)


def _exchange_sems(n):
    return [pltpu.SemaphoreType.DMA(((N_DEV - 1) * n,)), pltpu.SemaphoreType.DMA(((N_DEV - 1) * n,)),
            pltpu.SemaphoreType.DMA((n,))]


def _exchange_copies(modes, ins, outs, send_sems, recv_sems, local_sems):
    n = len(ins)
    x, y, c = lax.axis_index("x"), lax.axis_index("y"), lax.axis_index("c")
    me = 4 * x + 2 * y + c

    def src_of(a, dest):
        scatter, pick = modes[a]
        if scatter:
            return ins[a].at[dest]
        return ins[a] if pick is None else ins[a].at[pick]

    local = [pltpu.make_async_copy(src_of(a, me), outs[a].at[me], local_sems.at[a]) for a in range(n)]
    sends, recvs = [], []
    for k in range(1, N_DEV):
        px = 1 - x if k & 4 else x
        py = 1 - y if k & 2 else y
        pc = 1 - c if k & 1 else c
        peer = 4 * px + 2 * py + pc
        for a in range(n):
            sems = dict(send_sem=send_sems.at[(k - 1) * n + a], recv_sem=recv_sems.at[(k - 1) * n + a],
                        device_id=(px, py, pc), device_id_type=pl.DeviceIdType.MESH)
            sends.append(pltpu.make_async_remote_copy(src_ref=src_of(a, peer), dst_ref=outs[a].at[me], **sems))
            recvs.append(pltpu.make_async_remote_copy(src_ref=src_of(a, peer), dst_ref=outs[a].at[peer], **sems))
    return local, sends, recvs


def _gather_two_level(items, name):
    n = len(items)
    assert not any(scatter for _, scatter, _ in items)
    picks = [pick for _, _, pick in items]

    def body(*refs):
        ins, outs = refs[:n], refs[n:2 * n]
        send_sems, recv_sems, local_sems = refs[2 * n:]
        x, y, c = lax.axis_index("x"), lax.axis_index("y"), lax.axis_index("c")
        sib = 1 - c
        chips = [(1 - x, y), (x, 1 - y), (1 - x, 1 - y)]

        def slot(a, px, py, pc):
            return outs[a].at[4 * px + 2 * py + pc]

        def copy(k, a, src, block, to):
            return pltpu.make_async_remote_copy(
                src_ref=src, dst_ref=slot(a, *block), send_sem=send_sems.at[k * n + a],
                recv_sem=recv_sems.at[k * n + a], device_id=to, device_id_type=pl.DeviceIdType.MESH)

        own = [ins[a] if picks[a] is None else ins[a].at[picks[a]] for a in range(n)]
        local = [pltpu.make_async_copy(own[a], slot(a, x, y, c), local_sems.at[a]) for a in range(n)]
        first = [copy(0, a, own[a], (x, y, c), (x, y, sib)) for a in range(n)]
        first += [copy(1 + j, a, own[a], (x, y, c), (*chip, c)) for j, chip in enumerate(chips[:2]) for a in range(n)]
        for cp in local + first:
            cp.start()

        def pass_on(j, a):
            chip = chips[j]
            copy(1 + j, a, own[a], (*chip, c), (x, y, c)).wait_recv()
            fwd = copy(4 + j, a, slot(a, *chip, c), (*chip, c), (x, y, sib))
            fwd.start()
            return fwd

        passed = [pass_on(j, a) for j in range(2) for a in range(n)]
        south = c == 0
        via = tuple(jnp.where(south, p, q) for p, q in zip(chips[0], chips[1]))
        blk = tuple(jnp.where(south, q, p) for p, q in zip(chips[0], chips[1]))
        relayed = [copy(3, a, slot(a, *blk, c), (*blk, c), (*via, c)) for a in range(n)]
        for cp in relayed:
            cp.start()
        passed += [pass_on(2, a) for a in range(n)]
        first += relayed
        for a in range(n):
            copy(0, a, own[a], (x, y, sib), (x, y, c)).wait_recv()
        for j, chip in enumerate(chips):
            for a in range(n):
                copy(4 + j, a, own[a], (*chip, sib), (x, y, c)).wait_recv()
        for cp in first + passed:
            cp.wait_send()
        for cp in local:
            cp.wait()

    any_spec = pl.BlockSpec(memory_space=pl.ANY)
    return pl.pallas_call(
        body, in_specs=[any_spec] * n, out_specs=(any_spec,) * n, out_shape=_landing_shapes(items),
        scratch_shapes=[pltpu.SemaphoreType.DMA((7 * n,)), pltpu.SemaphoreType.DMA((7 * n,)),
                        pltpu.SemaphoreType.DMA((n,))],
        name=name,
    )(*[a for a, _, _ in items])


_HBM_SPEC = pl.BlockSpec(memory_space=pltpu.HBM)
_SEM_SPEC = pl.BlockSpec(memory_space=pltpu.SEMAPHORE)
_SPLIT_PARAMS = pltpu.CompilerParams(has_side_effects=pltpu.SideEffectType.DATAFLOW_SIDE_EFFECTING)


def _split_start(srcs, lands, plan, sem_shapes, name):
    n_src, n_land = len(srcs), len(lands)
    n_buf = n_src + n_land
    bufs = [pltpu.with_memory_space_constraint(a, pltpu.HBM) for a in list(srcs) + list(lands)]

    def body(*refs):
        local, sends, _ = plan(refs[:n_src], refs[n_src:n_buf], *refs[n_buf:n_buf + 3])
        for cp in local + sends:
            cp.start()
        token = refs[-1]
        token[...] = jnp.zeros_like(token)

    res = pl.pallas_call(
        body, name=name, in_specs=[_HBM_SPEC] * n_buf,
        out_shape=tuple(sem_shapes) + tuple(pltpu.HBM(a.shape, a.dtype) for a in bufs)
        + (jax.ShapeDtypeStruct((8, 128), F32),),
        out_specs=(_SEM_SPEC,) * 3 + (_HBM_SPEC,) * n_buf + (pl.BlockSpec(memory_space=pltpu.VMEM),),
        input_output_aliases={i: 3 + i for i in range(n_buf)}, compiler_params=_SPLIT_PARAMS,
    )(*bufs)
    return dict(sems=res[:3], srcs=res[3:3 + n_src], lands=res[3 + n_src:3 + n_buf], token=res[-1], plan=plan)


def _split_wait(ticket, after, name, with_srcs=False):
    n_src, n_land = len(ticket["srcs"]), len(ticket["lands"])
    n_buf = n_src + n_land
    plan = ticket["plan"]

    def body(*refs):
        local, sends, recvs = plan(refs[:n_src], refs[n_src:n_buf], *refs[n_buf:n_buf + 3])
        for cp in recvs:
            cp.wait_recv()
        for cp in sends:
            cp.wait_send()
        for cp in local:
            cp.wait()

    bufs = list(ticket["srcs"]) + list(ticket["lands"])
    res = pl.pallas_call(
        body, name=name,
        in_specs=[_HBM_SPEC] * n_buf + [_SEM_SPEC] * 3 + [pl.BlockSpec(memory_space=pl.ANY)] * len(after),
        out_shape=tuple(pltpu.HBM(a.shape, a.dtype) for a in bufs), out_specs=(_HBM_SPEC,) * n_buf,
        input_output_aliases={i: i for i in range(n_buf)}, compiler_params=_SPLIT_PARAMS,
    )(*bufs, *ticket["sems"], *after)
    return res if with_srcs else res[n_src:]


def _empty_landings(items):
    return [lax.empty(s.shape, s.dtype) for s in _landing_shapes(items)]


def _exchange_start(items, name):
    modes = [(scatter, pick) for _, scatter, pick in items]
    plan = lambda ins, outs, *sems: _exchange_copies(modes, ins, outs, *sems)
    return _split_start([a for a, _, _ in items], _empty_landings(items), plan, _exchange_sems(len(items)), name)


def _two_level_plans(picks):
    n = len(picks)

    def place():
        x, y, c = lax.axis_index("x"), lax.axis_index("y"), lax.axis_index("c")
        return x, y, c, 1 - c, [(1 - x, y), (x, 1 - y), (1 - x, 1 - y)]

    def copy(outs, send_sems, recv_sems, k, a, src, block, to):
        px, py, pc = block
        return pltpu.make_async_remote_copy(
            src_ref=src, dst_ref=outs[a].at[4 * px + 2 * py + pc], send_sem=send_sems.at[k * n + a],
            recv_sem=recv_sems.at[k * n + a], device_id=to, device_id_type=pl.DeviceIdType.MESH)

    def between_chips(ins, outs, send_sems, recv_sems, local_sems):
        x, y, c, sib, chips = place()
        own = [ins[a] if picks[a] is None else ins[a].at[picks[a]] for a in range(n)]
        mk = lambda *args: copy(outs, send_sems, recv_sems, *args)
        local = [pltpu.make_async_copy(own[a], outs[a].at[4 * x + 2 * y + c], local_sems.at[a]) for a in range(n)]
        sends = [mk(0, a, own[a], (x, y, c), (x, y, sib)) for a in range(n)]
        sends += [mk(1 + j, a, own[a], (x, y, c), (*chip, c)) for j, chip in enumerate(chips) for a in range(n)]
        recvs = [mk(0, a, own[a], (x, y, sib), (x, y, c)) for a in range(n)]
        recvs += [mk(1 + j, a, own[a], (*chip, c), (x, y, c)) for j, chip in enumerate(chips) for a in range(n)]
        return local, sends, recvs

    def within_chip(ins, outs, send_sems, recv_sems, local_sems):
        x, y, c, sib, chips = place()
        mk = lambda *args: copy(outs, send_sems, recv_sems, *args)
        slot = lambda a, px, py, pc: outs[a].at[4 * px + 2 * py + pc]
        sends = [mk(j, a, slot(a, *chip, c), (*chip, c), (x, y, sib)) for j, chip in enumerate(chips)
                 for a in range(n)]
        recvs = [mk(j, a, slot(a, *chip, c), (*chip, sib), (x, y, c)) for j, chip in enumerate(chips)
                 for a in range(n)]
        return [], sends, recvs

    sems = lambda k: [pltpu.SemaphoreType.DMA((k * n,)), pltpu.SemaphoreType.DMA((k * n,)),
                      pltpu.SemaphoreType.DMA((n,))]
    return between_chips, sems(4), within_chip, sems(3)


N_CHIPS = N_DEV // 2


def _pair_plans():
    def mesh_pos():
        return lax.axis_index("x"), lax.axis_index("y"), lax.axis_index("c")

    def to_sibling(ins, outs, send_sems, recv_sems, local_sems):
        x, y, c = mesh_pos()
        sib = 1 - c
        sends = [pltpu.make_async_remote_copy(
            src_ref=ins[0].at[2 * j + sib], dst_ref=outs[0].at[j], send_sem=send_sems.at[j],
            recv_sem=recv_sems.at[j], device_id=(x, y, sib), device_id_type=pl.DeviceIdType.MESH)
            for j in range(N_CHIPS)]
        return [], sends, sends

    def between_chips(ins, outs, send_sems, recv_sems, local_sems):
        x, y, c = mesh_pos()
        chip = 2 * x + y
        local = [pltpu.make_async_copy(ins[0].at[chip], outs[0].at[chip], local_sems.at[0])]
        sends, recvs = [], []
        for k in range(1, N_CHIPS):
            px = 1 - x if k & 2 else x
            py = 1 - y if k & 1 else y
            sems = dict(send_sem=send_sems.at[k - 1], recv_sem=recv_sems.at[k - 1], device_id=(px, py, c),
                        device_id_type=pl.DeviceIdType.MESH)
            sends.append(pltpu.make_async_remote_copy(src_ref=ins[0].at[2 * px + py], dst_ref=outs[0].at[chip], **sems))
            recvs.append(pltpu.make_async_remote_copy(src_ref=ins[0].at[2 * px + py], dst_ref=outs[0].at[2 * px + py],
                                                      **sems))
        return local, sends, recvs

    dma = lambda k: pltpu.SemaphoreType.DMA((k,))
    return to_sibling, [dma(N_CHIPS), dma(N_CHIPS), dma(1)], between_chips, [dma(N_CHIPS - 1), dma(N_CHIPS - 1), dma(1)]


def _pair_add(slabs, from_sibling, tr=W_IN_COLS):
    _, rows, cols = slabs.shape
    core = lax.axis_index("c").astype(jnp.int32).reshape(1)

    def body(core_ref, a_ref, b_ref, o_ref):
        o_ref[...] = (a_ref[...].astype(F32) + b_ref[...].astype(F32)).astype(o_ref.dtype)

    return pl.pallas_call(
        body,
        grid_spec=pltpu.PrefetchScalarGridSpec(
            num_scalar_prefetch=1, grid=(N_CHIPS, rows // tr),
            in_specs=[pl.BlockSpec((None, tr, cols), lambda j, i, core_ref: (2 * j + core_ref[0], i, 0)),
                      pl.BlockSpec((None, tr, cols), lambda j, i, core_ref: (j, i, 0))],
            out_specs=pl.BlockSpec((None, tr, cols), lambda j, i, core_ref: (j, i, 0))),
        out_shape=jax.ShapeDtypeStruct((N_CHIPS, rows, cols), slabs.dtype), name="pair_add",
        compiler_params=_params("arbitrary", "arbitrary"),
    )(core, slabs, from_sibling)


def _adam_math(g, w, m, v):
    m_new = ADAM_B1 * m + (1.0 - ADAM_B1) * g
    v_new = ADAM_B2 * v + (1.0 - ADAM_B2) * (g * g)
    m_hat = m_new / (1.0 - ADAM_B1 ** ADAM_STEP)
    v_hat = v_new / (1.0 - ADAM_B2 ** ADAM_STEP)
    return -ADAM_LR * (m_hat / (jnp.sqrt(v_hat) + ADAM_EPS) + ADAM_WD * w), m_new, v_new


def _adamw_big(parts, w, m, v, layer, prev, name, tr):
    depth, rows, cols = w.shape
    n_parts = parts.shape[0]

    def body(p_ref, w_ref, m_ref, v_ref, *rest):
        g_out, d_out, m_out, v_out = rest[len(prev):]
        g = p_ref[0].astype(F32)
        for q in range(1, n_parts):
            g = g + p_ref[q].astype(F32)
        d, m_new, v_new = _adam_math(g, w_ref[...], m_ref[...], v_ref[...])
        g_out[...] = g
        d_out[...] = d
        m_out[...] = m_new
        v_out[...] = v_new

    blk = pl.BlockSpec((None, tr, cols), lambda i: (layer, i, 0))
    shp = jax.ShapeDtypeStruct((depth, rows, cols), F32)
    return pl.pallas_call(
        body, grid=(rows // tr,),
        in_specs=[pl.BlockSpec((n_parts, tr, cols), lambda i: (0, i, 0)), blk, blk, blk]
        + [pl.BlockSpec(memory_space=pl.ANY)] * len(prev),
        out_specs=(blk,) * 4, out_shape=(shp,) * 4,
        input_output_aliases={4 + j: j for j in range(len(prev))},
        name=name, compiler_params=_params("arbitrary"),
    )(parts, w, m, v, *prev)


_SMALL_TENSORS = (("conv_a_w", "conv_a_w", None), ("cc_dw_w", "cc_dw_w", None), ("cc_pw_w", "cc_pw_w", None),
                  ("sg_w", "sg_w", None), ("pool_w", "pool_w", None), ("sg_b", "sg_b", None)) \
    + tuple((n, "g256", k) for k, n in enumerate(G256_ROWS)) + tuple((n, "g1024", k) for k, n in enumerate(G1024_ROWS))
_SMALL_LANDINGS = ("conv_a_w", "cc_dw_w", "cc_pw_w", "sg_w", "pool_w", "sg_b", "g256", "g1024")
_TAPS_FIRST = ("conv_a_w", "cc_dw_w")


def _adamw_small(landings, wts, mom, var):
    names = [n for n, _, _ in _SMALL_TENSORS]
    n_land = DEPTH * len(_SMALL_LANDINGS)
    n_t = len(names)

    def body(*refs):
        land = [dict(zip(_SMALL_LANDINGS, refs[l * len(_SMALL_LANDINGS):(l + 1) * len(_SMALL_LANDINGS)]))
                for l in range(DEPTH)]
        w_refs = dict(zip(names, refs[n_land:n_land + n_t]))
        m_refs = dict(zip(names, refs[n_land + n_t:n_land + 2 * n_t]))
        v_refs = dict(zip(names, refs[n_land + 2 * n_t:n_land + 3 * n_t]))
        outs = refs[n_land + 3 * n_t:]
        out_refs = {n: outs[4 * k:4 * k + 4] for k, n in enumerate(names)}
        loss_ref = outs[4 * n_t]
        for name, key, row in _SMALL_TENSORS:
            for l in range(DEPTH):
                src = land[l][key]
                if row is not None:
                    part = lambda q: src[q, row:row + 1, :]
                    at = (slice(l, l + 1),)
                elif name == "sg_b":
                    part = lambda q: src[q, 0:N_SUB, :]
                    at = (l,)
                elif name in _TAPS_FIRST:
                    part = lambda q: src[q]
                    at = (slice(None), l)
                else:
                    part = lambda q: src[q]
                    at = (l,)
                g = part(0).astype(F32)
                for q in range(1, N_DEV):
                    g = g + part(q).astype(F32)
                d, m_new, v_new = _adam_math(g, w_refs[name][at], m_refs[name][at], v_refs[name][at])
                for ref, val in zip(out_refs[name], (g, d, m_new, v_new)):
                    ref[at] = val
        src = land[DEPTH - 1]["g1024"]
        loss = src[0, LOSS_ROW:LOSS_ROW + 1, 0:128]
        for q in range(1, N_DEV):
            loss = loss + src[q, LOSS_ROW:LOSS_ROW + 1, 0:128]
        loss_ref[...] = loss

    ins = [landings[l][k] for l in range(DEPTH) for k in _SMALL_LANDINGS] \
        + [src[n] for src in (wts, mom, var) for n in names]
    out_shape = tuple(jax.ShapeDtypeStruct(wts[n].shape, F32) for n in names for _ in range(4)) \
        + (jax.ShapeDtypeStruct((1, 128), F32),)
    res = pl.pallas_call(body, out_shape=out_shape, name="adamw_small", compiler_params=_params())(*ins)
    return {n: res[4 * k:4 * k + 4] for k, n in enumerate(names)}, res[4 * n_t]


_BIG = (("w_in", 224), ("w_out", 80), ("w_kv", 64))
_GRAD_ITEMS_EARLY = ("w_out", "w_kv", "cc_pw_w", "conv_a_w", "cc_dw_w")
_GRAD_ITEMS_REPL = ("g256", "sg_w", "sg_b", "pool_w", "g1024")


def _grad_items(grads):
    items = [(grads[n], True, None) for n in _GRAD_ITEMS_EARLY]
    return items + [(grads[n], False, None) for n in _GRAD_ITEMS_REPL]


def _landed(parts):
    return dict(zip(_GRAD_ITEMS_EARLY + _GRAD_ITEMS_REPL, parts))


def _gathered_weights(wt_in8, w_kv8, w_out8, pw8, ca8, dw8):
    return dict(wt_in=wt_in8.reshape(D_IN, D_MODEL), w_kv=w_kv8.reshape(D_MODEL, 2 * D_G),
                w_out=w_out8.reshape(D_MIX, D_MODEL), pw=pw8.reshape(D_G, D_G), ca8=ca8, dw8=dw8)


def kernel(x, mem, w_in, conv_a_w, sg_ln_g, sg_ln_b, sg_w, sg_b, pool_w, pool_scale, cc_dw_w, cc_dw_b, cc_ln_g, cc_ln_b, cc_pw_w, w_kv, w_out, ln_g, ln_b, loss_target, m_w_in, m_conv_a_w, m_sg_ln_g, m_sg_ln_b, m_sg_w, m_sg_b, m_pool_w, m_pool_scale, m_cc_dw_w, m_cc_dw_b, m_cc_ln_g, m_cc_ln_b, m_cc_pw_w, m_w_kv, m_w_out, m_ln_g, m_ln_b, v_w_in, v_conv_a_w, v_sg_ln_g, v_sg_ln_b, v_sg_w, v_sg_b, v_pool_w, v_pool_scale, v_cc_dw_w, v_cc_dw_b, v_cc_ln_g, v_cc_ln_b, v_cc_pw_w, v_w_kv, v_w_out, v_ln_g, v_ln_b):
    names = ("w_in", "conv_a_w", "sg_ln_g", "sg_ln_b", "sg_w", "sg_b", "pool_w", "pool_scale", "cc_dw_w", "cc_dw_b",
             "cc_ln_g", "cc_ln_b", "cc_pw_w", "w_kv", "w_out", "ln_g", "ln_b")
    wts = dict(zip(names, (w_in, conv_a_w, sg_ln_g, sg_ln_b, sg_w, sg_b, pool_w, pool_scale, cc_dw_w, cc_dw_b,
                           cc_ln_g, cc_ln_b, cc_pw_w, w_kv, w_out, ln_g, ln_b)))
    mom = dict(zip(names, (m_w_in, m_conv_a_w, m_sg_ln_g, m_sg_ln_b, m_sg_w, m_sg_b, m_pool_w, m_pool_scale,
                           m_cc_dw_w, m_cc_dw_b, m_cc_ln_g, m_cc_ln_b, m_cc_pw_w, m_w_kv, m_w_out, m_ln_g, m_ln_b)))
    var = dict(zip(names, (v_w_in, v_conv_a_w, v_sg_ln_g, v_sg_ln_b, v_sg_w, v_sg_b, v_pool_w, v_pool_scale,
                           v_cc_dw_w, v_cc_dw_b, v_cc_ln_g, v_cc_ln_b, v_cc_pw_w, v_w_kv, v_w_out, v_ln_g, v_ln_b)))
    repl = wts
    xs, mems, tgt = x[0], mem[0], loss_target[0]
    turned = {"w_in": (0, 2, 1), "conv_a_w": (1, 0, 2), "cc_dw_w": (1, 0, 2)}
    wts, mom, var = [{n: (jnp.transpose(a, turned[n]) if n in turned else a) for n, a in src.items()}
                     for src in (wts, mom, var)]
    wb = {n: wts[n].astype(MM_DTYPE) for n in ("w_in", "w_kv", "w_out", "cc_pw_w")}

    wt8_0, wkv8_0, wo8_0, pw8_0, ca8, dw8 = _gather_two_level(
        [(wb["w_in"], False, 0), (wb["w_kv"], False, 0), (wb["w_out"], False, 0), (wb["cc_pw_w"], False, 0),
         (wts["conv_a_w"], False, None), (wts["cc_dw_w"], False, None)], "gather_weights_0")
    gw0 = _gathered_weights(wt8_0, wkv8_0, wo8_0, pw8_0, ca8, dw8)
    items_1 = [(wb[n], False, 1) for n in ("w_in", "w_kv", "w_out", "cc_pw_w")]
    between_chips, sems_a, within_chip, sems_b = _two_level_plans([1] * len(items_1))
    chips_1 = _split_start([a for a, _, _ in items_1], _empty_landings(items_1), between_chips, sems_a,
                           "gather_weights_1a_start")
    x1, saved0 = _layer_fwd(xs, mems, 0, repl, gw0, ties=(chips_1["token"],))
    core_1 = _split_start([], _split_wait(chips_1, (x1,), "gather_weights_1a_wait"), within_chip, sems_b,
                          "gather_weights_1b_start")
    gw1 = _gathered_weights(*_split_wait(core_1, (core_1["token"],), "gather_weights_1b_wait"), ca8, dw8)
    _, saved1 = _layer_fwd(x1, mems, 1, repl, gw1)

    dz1, dproj1, g1 = _layer_bwd_a(None, tgt, mems, 1, repl, gw1, saved1)
    shards = lambda g: g.reshape(N_DEV, W_IN_COLS, D_MODEL)
    up, g_wt_in_1 = _in_bwd(saved1[0], dproj1, gw1["wt_in"], dz1)
    grads_1 = _exchange_start([(shards(g_wt_in_1), True, None)] + _grad_items(g1), "exchange_grads_1_start")
    dz0, dproj0, g0 = _layer_bwd_a(up, None, mems, 0, repl, gw0, saved0, (grads_1["token"],))
    early_0 = _exchange_start(_grad_items(g0), "exchange_grads_0a_start")
    g_wt_in_0 = _dw_in_matmul(saved0[0], dproj0, (early_0["token"],))
    to_sibling, sems_s, between_chips, sems_c = _pair_plans()
    half = lambda: [lax.empty((N_CHIPS, W_IN_COLS, D_MODEL), GRAD_DTYPE)]
    pair_0 = _split_start([shards(g_wt_in_0)], half(), to_sibling, sems_s, "exchange_grads_0b_start")
    w_in_1, *rest_1 = _split_wait(grads_1, (pair_0["token"],), "exchange_grads_1_wait")
    landed = [None, dict(_landed(rest_1), w_in=w_in_1)]
    big = {}
    for n, tr in _BIG:
        big[n] = _adamw_big(landed[1][n], wts[n], mom[n], var[n], 1, (), "adamw_" + n + "_1", tr)
    own_slabs, from_sibling = _split_wait(pair_0, tuple(big[n][0] for n, _ in _BIG), "exchange_grads_0b_wait",
                                          with_srcs=True)
    late_0 = _split_start([_pair_add(own_slabs, from_sibling)], half(), between_chips, sems_c,
                          "exchange_grads_0c_start")
    grad_x = _dx_matmul(dproj0, gw0["wt_in"], dz0, (late_0["token"],))
    landed[0] = _landed(_split_wait(early_0, (grad_x,), "exchange_grads_0a_wait"))
    for n, tr in _BIG[1:]:
        big[n] = _adamw_big(landed[0][n], wts[n], mom[n], var[n], 0, big[n], "adamw_" + n + "_0", tr)
    small, loss = _adamw_small(landed, wts, mom, var)
    (landed[0]["w_in"],) = _split_wait(late_0, (loss, big["w_out"][0], big["w_kv"][0]), "exchange_grads_0c_wait")
    big["w_in"] = _adamw_big(landed[0]["w_in"], wts["w_in"], mom["w_in"], var["w_in"], 0, big["w_in"],
                             "adamw_w_in_0", _BIG[0][1])

    res = {**small, **big}
    res = {n: ([jnp.transpose(a, turned[n]) for a in r] if n in turned else r) for n, r in res.items()}
    return (loss[0, 0], grad_x[None], *[res[n][0] for n in names], *[res[n][1] for n in names],
            *[res[n][2] for n in names], *[res[n][3] for n in names])
```

```python
import math

import jax
import jax.numpy as jnp
from jax import lax
from jax.experimental import pallas as pl
from jax.experimental.pallas import tpu as pltpu

F32 = jnp.float32
MM_DTYPE = jnp.bfloat16
GRAD_DTYPE = jnp.bfloat16

D_MODEL = 1024
DEPTH = 2
D_G = 256
N_GROUPS = 5
D_MIX = N_GROUPS * D_G
N_SUB = 4
HEAD_DIM = D_G // N_SUB
CONV_A = 3
CONV_D = 31
CHUNK = 128
POOL_WINDOWS = (2, 4, 8, 16)
MEM_LEN = 256
LN_EPS = 1e-5
ALPHA = (2.0 * DEPTH) ** 0.25
D_IN = 9 * D_G + D_MIX
ATT_SCALE = 1.0 / math.sqrt(HEAD_DIM)

ADAM_LR = 0.001
ADAM_B1 = 0.9
ADAM_B2 = 0.999
ADAM_EPS = 1e-08
ADAM_WD = 0.01
ADAM_STEP = 10

N_DEV = 8
W_IN_COLS = D_IN // N_DEV
CONV_CH = D_G // N_DEV
HALO = 32
FWD_TILE = 512
BWD_TILE = 256
VMEM_LIMIT = 56 * 1024 * 1024

C_XA, C_BA, C_CA, C_U, C_V, C_XC, C_DA, C_DG, C_Q = range(9)
C_GATE = 9 * D_G

G256_ROWS = ("sg_ln_g", "sg_ln_b", "pool_scale", "cc_dw_b", "cc_ln_g", "cc_ln_b")
G1024_ROWS = ("ln_g", "ln_b")
LOSS_ROW = 2


def _mm(a, b):
    return jnp.dot(a.astype(MM_DTYPE), b.astype(MM_DTYPE), preferred_element_type=F32)


def _mm_nt(a, b):
    return lax.dot_general(a.astype(MM_DTYPE), b.astype(MM_DTYPE), (((1,), (1,)), ((), ())),
                           preferred_element_type=F32)


def _mm_tn(a, b):
    return lax.dot_general(a.astype(MM_DTYPE), b.astype(MM_DTYPE), (((0,), (0,)), ((), ())),
                           preferred_element_type=F32)


def _sigmoid(x):
    return 0.5 * jnp.tanh(0.5 * x) + 0.5


_GELU_C = math.sqrt(2.0 / math.pi)
_GELU_A = 0.044715


def _gelu(x):
    th = jnp.tanh(_GELU_C * (x + _GELU_A * (x * x * x)))
    return 0.5 * x * (1.0 + th), th


def _dgelu(x, th):
    return 0.5 * (1.0 + th) + 0.5 * x * (1.0 - th * th) * (_GELU_C * (1.0 + 3.0 * _GELU_A * (x * x)))


def _ln_fwd(x, g, b):
    mu = jnp.mean(x, axis=-1, keepdims=True)
    xc = x - mu
    var = jnp.mean(xc * xc, axis=-1, keepdims=True)
    rstd = lax.rsqrt(var + LN_EPS)
    xhat = xc * rstd
    return xhat * g + b, xhat, rstd


def _ln_bwd(dy, xhat, rstd, g):
    dxhat = dy * g
    m1 = jnp.mean(dxhat, axis=-1, keepdims=True)
    m2 = jnp.mean(dxhat * xhat, axis=-1, keepdims=True)
    return rstd * (dxhat - m1 - xhat * m2)


def _rowsum(x):
    return jnp.sum(x, axis=0, keepdims=True)


def _col(ref, k):
    return ref[:, k * D_G:(k + 1) * D_G]


def _head_of_lane(shape):
    return jnp.right_shift(lax.broadcasted_iota(jnp.int32, shape, len(shape) - 1), HEAD_DIM.bit_length() - 1)


def _pool_select(lane_grp, s2, s4, s8, s16):
    return jnp.where(lane_grp == 0, s2, jnp.where(lane_grp == 1, s4, jnp.where(lane_grp == 2, s8, s16)))


def _row_view(ref, layer):
    return ref.at[pl.ds(layer, 1)]


def _make_residues(ext_ref, res_ref):
    rows = res_ref.shape[1]
    for r in range(1, 8):
        res_ref[r - 1] = ext_ref[pl.ds(r, rows), :]


def _rows_at(ext_ref, res_ref, off, tile):
    a, r = divmod(off, 8)
    if r == 0:
        return ext_ref[pl.ds(off, tile), :]
    return res_ref[r - 1, pl.ds(8 * a, tile), :]


def _residue_scratch(tile):
    return pltpu.VMEM((7, HALO + tile - 8, D_G), F32)


PROJ_SEGMENTS = ((C_XA * D_G, (C_CA + 1) * D_G), (C_XC * D_G, (C_XC + 1) * D_G), (C_DA * D_G, (C_DG + 1) * D_G),
                 (C_U * D_G, (C_V + 1) * D_G), (C_Q * D_G, (C_Q + 1) * D_G), (C_GATE, D_IN))


def _branch_forward(p_ref, ph_ref, first, row0, km_ref, vm_ref, w, ext_a, ext_c, ext_d, res_c, res_d, tile,
                    kept_ref=None, produce=None):
    r = {}
    produce = produce or (lambda: None)
    produce()
    produce()
    xa, ba, ca = _col(p_ref, C_XA), _col(p_ref, C_BA), _col(p_ref, C_CA)
    g_a = ca * xa
    ext_a[0:HALO] = jnp.where(first, 0.0, _col(ph_ref, C_CA) * _col(ph_ref, C_XA))
    ext_a[HALO:HALO + tile] = g_a
    conv_a = w["conv_a"][0:1, :] * ext_a[pl.ds(HALO - 2, tile), :]
    for k in range(1, CONV_A):
        conv_a = conv_a + w["conv_a"][k:k + 1, :] * ext_a[pl.ds(HALO - 2 + k, tile), :]
    r.update(xa=xa, ba=ba, ca=ca, g_a=g_a, conv_a=conv_a)
    ya = ba * conv_a

    produce()
    lane_grp = _head_of_lane((tile, D_G))
    trow = row0 + lax.broadcasted_iota(jnp.int32, (tile, D_G), 0)
    win = _pool_select(lane_grp, 2, 4, 8, 16)
    inv_cnt = 1.0 / jnp.minimum(trow + 1, win).astype(F32)
    if kept_ref is None:
        xc = _col(p_ref, C_XC)
        ext_c[0:HALO] = jnp.where(first, 0.0, _col(ph_ref, C_XC))
        ext_c[HALO:HALO + tile] = xc
        _make_residues(ext_c, res_c)
        acc = xc
        sums = {}
        for k in range(1, POOL_WINDOWS[-1]):
            acc = acc + _rows_at(ext_c, res_c, HALO - k, tile)
            if k + 1 in POOL_WINDOWS:
                sums[k + 1] = acc
        ypre = _pool_select(lane_grp, sums[2], sums[4], sums[8], sums[16]) * inv_cnt - xc
    else:
        ypre = kept_ref[:, D_G:2 * D_G]
    pool_mm = _mm(ypre, w["pool_wbd"][...])
    yc = pool_mm * w["pool_scale"][...]
    r.update(lane_grp=lane_grp, inv_cnt=inv_cnt, ypre=ypre, pool_mm=pool_mm)

    produce()
    da, dg = _col(p_ref, C_DA), _col(p_ref, C_DG)
    sig_dg = _sigmoid(dg)
    hd = da * sig_dg
    if kept_ref is None:
        ext_d[0:HALO] = jnp.where(first, 0.0, _col(ph_ref, C_DA) * _sigmoid(_col(ph_ref, C_DG)))
        ext_d[HALO:HALO + tile] = hd
        _make_residues(ext_d, res_d)
        conv_d = w["cc_dw_b"][...] + w["cc_dw_w"][0:1, :] * _rows_at(ext_d, res_d, HALO - (CONV_D - 1), tile)
        for j in range(1, CONV_D):
            conv_d = conv_d + w["cc_dw_w"][j:j + 1, :] * _rows_at(ext_d, res_d, HALO - (CONV_D - 1) + j, tile)
    else:
        conv_d = kept_ref[:, 0:D_G]
    r["kept"] = (conv_d, ypre)
    ln_d, xhat_d, rstd_d = _ln_fwd(conv_d, w["cc_ln_g"][...], w["cc_ln_b"][...])
    sig_ln = _sigmoid(ln_d)
    act_d = ln_d * sig_ln
    yd = _mm(act_d, w["cc_pw_w"][...])
    r.update(da=da, sig_dg=sig_dg, hd=hd, ln_d=ln_d, xhat_d=xhat_d, rstd_d=rstd_d, sig_ln=sig_ln, act_d=act_d)
    produce()

    u, v = _col(p_ref, C_U), _col(p_ref, C_V)
    ug, th_u = _gelu(u)
    vg, th_v = _gelu(v)
    vn, xhat_v, rstd_v = _ln_fwd(vg, w["sg_ln_g"][...], w["sg_ln_b"][...])
    tri = (lax.broadcasted_iota(jnp.int32, (CHUNK, CHUNK), 0)
           >= lax.broadcasted_iota(jnp.int32, (CHUNK, CHUNK), 1))
    wm = [jnp.where(tri, w["sg_w"][h], 0.0).astype(MM_DTYPE) for h in range(N_SUB)]
    lo = lax.broadcasted_iota(jnp.int32, (CHUNK, 2 * HEAD_DIM), 1) < HEAD_DIM
    chunks = []
    for c in range(tile // CHUNK):
        halves = []
        for hf in range(2):
            vh = vn[c * CHUNK:(c + 1) * CHUNK, hf * 128:(hf + 1) * 128]
            halves.append(_mm(wm[2 * hf], jnp.where(lo, vh, 0.0)) + _mm(wm[2 * hf + 1], jnp.where(lo, 0.0, vh)))
        chunks.append(jnp.concatenate(halves, axis=1) + w["sg_bias"][...])
    mixed = jnp.concatenate(chunks, axis=0)
    yb = ug * mixed
    r.update(u=u, v=v, ug=ug, th_u=th_u, th_v=th_v, vn=vn, xhat_v=xhat_v, rstd_v=rstd_v, wm=wm, lo=lo,
             mixed=mixed, tri=tri)

    produce()
    q = _col(p_ref, C_Q)
    ye = jnp.zeros((tile, D_G), F32)
    probs = []
    for h in range(N_SUB):
        s = _mm_nt(q, km_ref[h]) * ATT_SCALE
        e = jnp.exp(s - jnp.max(s, axis=-1, keepdims=True))
        p = e * (1.0 / jnp.sum(e, axis=-1, keepdims=True))
        probs.append(p)
        ye = ye + _mm(p, vm_ref[h])
    r.update(q=q, probs=probs)

    gate = p_ref[:, C_GATE:C_GATE + D_MIX]
    sig_gate = _sigmoid(gate)
    r.update(gate=gate, sig_gate=sig_gate, branch_out=(ya, yb, yc, yd, ye))
    return r


_BRANCH_REPL = ("sg_ln_g", "sg_ln_b", "sg_w", "sg_b", "pool_w", "pool_scale", "cc_dw_b", "cc_ln_g", "cc_ln_b")
_BRANCH_W_SCRATCH = (("conv_a", (CONV_A, D_G)), ("cc_dw_w", (CONV_D, D_G)), ("sg_bias", (CHUNK, D_G)),
                     ("pool_wbd", (D_G, D_G)), ("sgb8", (8, CHUNK)))


def _branch_weights(layer, nat, pw_ref, ca_ref, dw_ref, scr, init):
    @pl.when(init)
    def _():
        for p in range(N_DEV):
            scr["conv_a"][:, p * CONV_CH:(p + 1) * CONV_CH] = ca_ref[p, :, layer, :]
            scr["cc_dw_w"][:, p * CONV_CH:(p + 1) * CONV_CH] = dw_ref[p, :, layer, :]
        scr["sgb8"][...] = jnp.zeros((8, CHUNK), F32)
        scr["sgb8"][0:N_SUB] = nat["sg_b"][layer]
        sgb_t = scr["sgb8"][...].T
        head = _head_of_lane((CHUNK, D_G))
        bias = jnp.zeros((CHUNK, D_G), F32)
        for h in range(N_SUB):
            bias = jnp.where(head == h, sgb_t[:, h:h + 1], bias)
        scr["sg_bias"][...] = bias
        scr["pool_wbd"][...] = jnp.zeros((D_G, D_G), F32)
        for gi in range(N_SUB):
            sl = slice(gi * HEAD_DIM, (gi + 1) * HEAD_DIM)
            scr["pool_wbd"][sl, sl] = nat["pool_w"][layer, gi]

    w = {n: _row_view(nat[n], layer) for n in ("sg_ln_g", "sg_ln_b", "pool_scale", "cc_dw_b", "cc_ln_g", "cc_ln_b")}
    w.update(conv_a=scr["conv_a"], cc_dw_w=scr["cc_dw_w"], sg_bias=scr["sg_bias"], pool_wbd=scr["pool_wbd"],
             sg_w=nat["sg_w"].at[layer], cc_pw_w=pw_ref)
    return w


def _full_spec(a):
    nd = a.ndim
    return pl.BlockSpec(a.shape, lambda *_, _nd=nd: (0,) * _nd)


def _tie_specs(ties):
    return [pl.BlockSpec((8, 128), lambda *_: (0, 0)) for _ in ties]


def _params(*sem):
    return pltpu.CompilerParams(dimension_semantics=sem or None, vmem_limit_bytes=VMEM_LIMIT)


def _kv_project(mem_ref, w_ref, km_ref, vm_ref):
    kv = _mm(mem_ref[...], w_ref[...])
    k, v = kv[:, :D_G], kv[:, D_G:]
    grp = _head_of_lane((MEM_LEN, D_G))
    for h in range(N_SUB):
        km_ref[h] = jnp.where(grp == h, k, 0.0).astype(km_ref.dtype)
        vm_ref[h] = jnp.where(grp == h, v, 0.0).astype(vm_ref.dtype)


def _layer_fwd_fused(x, wt_in, mem, w_kv, layer, repl, pw, ca8, dw8, w_out, want_xn, ties=(), tile=FWD_TILE):
    s = x.shape[0]
    nat_arrays = [repl[n] for n in _BRANCH_REPL]
    n_nat, nt = len(nat_arrays), len(ties)

    def body(x_ref, wt_ref, mem_ref, wkv_ref, *rest):
        nat = dict(zip(_BRANCH_REPL, rest[:n_nat]))
        pw_ref, ca_ref, dw_ref, wo_ref, g_ref, b_ref = rest[n_nat:n_nat + 6]
        rest = rest[n_nat + 6 + nt:]
        p_ref, h_ref, z_ref, cd_ref, km_ref, vm_ref = rest[:6]
        rest = rest[6:]
        if want_xn:
            xn_ref, rest = rest[0], rest[1:]
        ph_ref, ext_a, ext_c, ext_d, res_c, res_d = rest[:6]
        scr = dict(zip([n for n, _ in _BRANCH_W_SCRATCH], rest[6:]))
        i = pl.program_id(0)

        @pl.when(i == 0)
        def _():
            ph_ref[...] = jnp.zeros_like(ph_ref)
            _kv_project(mem_ref, wkv_ref, km_ref, vm_ref)

        xt = x_ref[...]
        xb = xt.astype(MM_DTYPE)
        segments = iter(PROJ_SEGMENTS)

        def produce():
            lo, hi = next(segments)
            p_ref[:, lo:hi] = _mm_nt(xb, wt_ref[lo:hi, :])

        w = _branch_weights(layer, nat, pw_ref, ca_ref, dw_ref, scr, i == 0)
        r = _branch_forward(p_ref, ph_ref, i == 0, i * tile, km_ref, vm_ref, w, ext_a, ext_c, ext_d, res_c, res_d,
                            tile, None, produce)
        ph_ref[...] = p_ref[tile - HALO:tile, :]
        cd_ref[:, 0:D_G], cd_ref[:, D_G:2 * D_G] = r["kept"]
        h = (jnp.concatenate(r["branch_out"], axis=1) * (r["gate"] * r["sig_gate"])).astype(h_ref.dtype)
        h_ref[...] = h
        z = ALPHA * xt + _mm(h, wo_ref[...])
        z_ref[...] = z
        if want_xn:
            xn_ref[...] = _ln_fwd(z, _row_view(g_ref, layer)[...], _row_view(b_ref, layer)[...])[0]

    row = lambda i: (i, 0)
    consts = [wt_in, mem, w_kv] + nat_arrays + [pw, ca8, dw8, w_out, repl["ln_g"], repl["ln_b"]]
    act = jax.ShapeDtypeStruct((s, D_MODEL), F32)
    act_spec = pl.BlockSpec((tile, D_MODEL), row)
    kv = jax.ShapeDtypeStruct((N_SUB, MEM_LEN, D_G), MM_DTYPE)
    kv_spec = pl.BlockSpec(kv.shape, lambda i: (0, 0, 0))
    res = pl.pallas_call(
        body, grid=(s // tile,),
        in_specs=[act_spec] + [_full_spec(a) for a in consts] + _tie_specs(ties),
        out_specs=(pl.BlockSpec((tile, D_IN), row), pl.BlockSpec((tile, D_MIX), row), act_spec,
                   pl.BlockSpec((tile, 2 * D_G), row), kv_spec, kv_spec) + ((act_spec,) if want_xn else ()),
        out_shape=(jax.ShapeDtypeStruct((s, D_IN), F32), jax.ShapeDtypeStruct((s, D_MIX), MM_DTYPE), act,
                   jax.ShapeDtypeStruct((s, 2 * D_G), F32), kv, kv) + ((act,) if want_xn else ()),
        scratch_shapes=[pltpu.VMEM((HALO, D_IN), F32)] + [pltpu.VMEM((HALO + tile, D_G), F32)] * 3
        + [_residue_scratch(tile)] * 2 + [pltpu.VMEM(shape, F32) for _, shape in _BRANCH_W_SCRATCH],
        name="layer_fwd", compiler_params=_params("arbitrary"),
    )(x, *consts, *ties)
    return tuple(res) if want_xn else tuple(res) + (None,)


_BRANCH_GRADS = (("g256", (8, D_G)), ("sg_w", (N_SUB, CHUNK, CHUNK)), ("sg_b", (8, CHUNK)),
                 ("pool_w", (N_SUB, HEAD_DIM, HEAD_DIM)), ("conv_a_w", (N_DEV, CONV_A, CONV_CH)),
                 ("cc_dw_w", (N_DEV, CONV_D, CONV_CH)), ("cc_pw_w", (D_G, D_G)),
                 ("dk", (N_SUB, MEM_LEN, D_G)), ("dv", (N_SUB, MEM_LEN, D_G)))
_BRANCH_ACC = (("conv_a", (CONV_A, D_G)), ("cc_dw_w", (CONV_D, D_G)), ("pool_wbd", (D_G, D_G)),
               ("sg_bias", (CHUNK, D_G)), ("sg_w", (N_SUB, CHUNK, CHUNK)))
_BRANCH_GRADS_NARROW = ("sg_w", "pool_w")


def _layer_bwd_fused(up, target, z, h, proj, kept, km, vm, mem, layer, repl, w_out, pw, ca8, dw8, ties=(),
                     tile=BWD_TILE):
    s = proj.shape[0]
    nt = s // tile
    hb = tile // HALO
    nat_arrays = [repl[n] for n in _BRANCH_REPL]
    n_nat, n_grads, n_acc, n_ties = len(nat_arrays), len(_BRANCH_GRADS), len(_BRANCH_ACC), len(ties)
    row_of = {n: k for k, n in enumerate(G256_ROWS)}
    from_loss = target is not None

    def body(o_ref, z_ref, h_ref, p_ref, ph_ref, cd_ref, km_ref, vm_ref, *rest):
        nat = dict(zip(_BRANCH_REPL, rest[:n_nat]))
        pw_ref, ca_ref, dw_ref, lng_ref, lnb_ref, wo_ref, mem_ref = rest[n_nat:n_nat + 7]
        rest = rest[n_nat + 7 + n_ties:]
        dz_ref, dp_ref, gw_ref, slab_ref, gkv_ref = rest[:5]
        g = dict(zip([n for n, _ in _BRANCH_GRADS], rest[5:5 + n_grads]))
        rest = rest[5 + n_grads:]
        ext_a, rev_a, rev_c, rev_d, res_rc, res_rd, gw_acc, lacc = rest[:8]
        acc = dict(zip([n for n, _ in _BRANCH_ACC], rest[8:8 + n_acc]))
        scr = dict(zip([n for n, _ in _BRANCH_W_SCRATCH], rest[8 + n_acc:]))
        i = pl.program_id(0)
        t = nt - 1 - i

        @pl.when(i == 0)
        def _():
            for ref in list(g.values()) + list(acc.values()) + [rev_a, rev_c, rev_d, gw_acc, slab_ref, lacc]:
                ref[...] = jnp.zeros_like(ref)

        g_ln = _row_view(lng_ref, layer)[...]
        xn, xhat, rstd = _ln_fwd(z_ref[...], g_ln, _row_view(lnb_ref, layer)[...])
        if from_loss:
            err = xn - o_ref[...]
            lacc[...] += _rowsum(err * err)
            dxn = err * (1.0 / D_MODEL)
        else:
            dxn = o_ref[...]
        slab_ref[0:1, :] += _rowsum(dxn * xhat)
        slab_ref[1:2, :] += _rowsum(dxn)
        dz = _ln_bwd(dxn, xhat, rstd, g_ln)
        dz_ref[...] = dz
        dzb = dz.astype(MM_DTYPE)

        w = _branch_weights(layer, nat, pw_ref, ca_ref, dw_ref, scr, i == 0)
        r = _branch_forward(p_ref, ph_ref, t == 0, t * tile, km_ref, vm_ref, w, ext_a, None, None, None, None,
                            tile, cd_ref)

        def put(k, val, width=D_G):
            dp_ref[:, k:k + width] = val.astype(dp_ref.dtype)

        def add_row(name, val):
            k = row_of[name]
            g["g256"][k:k + 1, :] += val

        def push_rev(rev, val):
            head = rev[0:HALO]
            rev[tile:tile + HALO] = head
            rev[0:tile] = val

        gate, sig_gate = r["gate"], r["sig_gate"]

        def branch_grad(group):
            cols = slice(group * D_G, (group + 1) * D_G)
            dh_g = _mm_nt(dzb, wo_ref[cols, :])
            gate_g, sig_g = gate[:, cols], sig_gate[:, cols]
            put(C_GATE + group * D_G, dh_g * r["branch_out"][group] * (sig_g * (1.0 + gate_g * (1.0 - sig_g))))
            return dh_g * (gate_g * sig_g)

        dya = branch_grad(0)
        dyc = branch_grad(2)

        put(C_BA * D_G, dya * r["conv_a"])
        dconv_a = dya * r["ba"]
        push_rev(rev_a, dconv_a)
        dga = jnp.zeros((tile, D_G), F32)
        for k in range(CONV_A):
            ahead = rev_a[pl.ds(CONV_A - 1 - k, tile), :]
            dga = dga + w["conv_a"][k:k + 1, :] * ahead
            acc["conv_a"][k:k + 1, :] += _rowsum(r["g_a"] * ahead)
        put(C_CA * D_G, dga * r["xa"])
        put(C_XA * D_G, dga * r["ca"])

        dyd = branch_grad(3)
        add_row("pool_scale", _rowsum(dyc * r["pool_mm"]))
        dmm = dyc * w["pool_scale"][...]
        acc["pool_wbd"][...] += _mm_tn(r["ypre"], dmm)
        dypre = _mm_nt(dmm, w["pool_wbd"][...])
        dws = dypre * r["inv_cnt"]
        push_rev(rev_c, dws)
        _make_residues(rev_c, res_rc)
        run = dws
        sums = {}
        for k in range(1, POOL_WINDOWS[-1]):
            run = run + _rows_at(rev_c, res_rc, k, tile)
            if k + 1 in POOL_WINDOWS:
                sums[k + 1] = run
        put(C_XC * D_G, _pool_select(r["lane_grp"], sums[2], sums[4], sums[8], sums[16]) - dypre)

        dyb = branch_grad(1)
        gw_acc[...] += _mm_tn(h_ref[...], dzb)
        g["cc_pw_w"][...] += _mm_tn(r["act_d"], dyd)
        dact = _mm_nt(dyd, w["cc_pw_w"][...])
        sig_ln, ln_d = r["sig_ln"], r["ln_d"]
        dln = dact * (sig_ln * (1.0 + ln_d * (1.0 - sig_ln)))
        add_row("cc_ln_g", _rowsum(dln * r["xhat_d"]))
        add_row("cc_ln_b", _rowsum(dln))
        dconv_d = _ln_bwd(dln, r["xhat_d"], r["rstd_d"], w["cc_ln_g"][...])
        add_row("cc_dw_b", _rowsum(dconv_d))
        push_rev(rev_d, dconv_d)
        _make_residues(rev_d, res_rd)
        dhd = jnp.zeros((tile, D_G), F32)
        for j in range(CONV_D):
            ahead = _rows_at(rev_d, res_rd, CONV_D - 1 - j, tile)
            dhd = dhd + w["cc_dw_w"][j:j + 1, :] * ahead
            acc["cc_dw_w"][j:j + 1, :] += _rowsum(r["hd"] * ahead)
        sig_dg = r["sig_dg"]
        put(C_DA * D_G, dhd * sig_dg)
        put(C_DG * D_G, dhd * r["da"] * sig_dg * (1.0 - sig_dg))

        dye = branch_grad(4)
        dug = dyb * r["mixed"]
        dmixed = dyb * r["ug"]
        wm, lo, vn = r["wm"], r["lo"], r["vn"]
        dvn_chunks = []
        for c in range(tile // CHUNK):
            rows = slice(c * CHUNK, (c + 1) * CHUNK)
            acc["sg_bias"][...] += dmixed[rows, :]
            halves = []
            for hf in range(2):
                cols = slice(hf * 128, (hf + 1) * 128)
                dm = dmixed[rows, cols]
                dm_a, dm_b = jnp.where(lo, dm, 0.0), jnp.where(lo, 0.0, dm)
                vh = vn[rows, cols]
                acc["sg_w"][2 * hf] += _mm_nt(dm_a, vh)
                acc["sg_w"][2 * hf + 1] += _mm_nt(dm_b, vh)
                halves.append(_mm_tn(wm[2 * hf], dm_a) + _mm_tn(wm[2 * hf + 1], dm_b))
            dvn_chunks.append(jnp.concatenate(halves, axis=1))
        dvn = jnp.concatenate(dvn_chunks, axis=0)
        add_row("sg_ln_g", _rowsum(dvn * r["xhat_v"]))
        add_row("sg_ln_b", _rowsum(dvn))
        dvg = _ln_bwd(dvn, r["xhat_v"], r["rstd_v"], w["sg_ln_g"][...])
        put(C_V * D_G, dvg * _dgelu(r["v"], r["th_v"]))
        put(C_U * D_G, dug * _dgelu(r["u"], r["th_u"]))

        q = r["q"]
        dq = jnp.zeros((tile, D_G), F32)
        for h in range(N_SUB):
            p = r["probs"][h]
            dp = _mm_nt(dye, vm_ref[h])
            g["dv"][h] += _mm_tn(p, dye)
            ds = p * (dp - jnp.sum(dp * p, axis=-1, keepdims=True)) * ATT_SCALE
            dq = dq + _mm(ds, km_ref[h])
            g["dk"][h] += _mm_tn(ds, q)
        put(C_Q * D_G, dq)

        @pl.when(i == nt - 1)
        def _():
            for h in range(N_SUB):
                g["sg_w"][h] = jnp.where(r["tri"], acc["sg_w"][h], 0.0).astype(g["sg_w"].dtype)
            lane_head = _head_of_lane((CHUNK, D_G))
            col_of = lax.broadcasted_iota(jnp.int32, (CHUNK, 8), 1)
            ba = acc["sg_bias"][...]
            sgb_t = jnp.zeros((CHUNK, 8), F32)
            for h in range(N_SUB):
                col = jnp.sum(jnp.where(lane_head == h, ba, 0.0), axis=-1, keepdims=True)
                sgb_t = jnp.where(col_of == h, col, sgb_t)
            g["sg_b"][...] = sgb_t.T
            wbd = acc["pool_wbd"][...]
            for gi in range(N_SUB):
                sl = slice(gi * HEAD_DIM, (gi + 1) * HEAD_DIM)
                g["pool_w"][gi] = wbd[sl, sl].astype(g["pool_w"].dtype)
            ca, dw = acc["conv_a"][...], acc["cc_dw_w"][...]
            for p in range(N_DEV):
                g["conv_a_w"][p] = ca[:, p * CONV_CH:(p + 1) * CONV_CH]
                g["cc_dw_w"][p] = dw[:, p * CONV_CH:(p + 1) * CONV_CH]
            gw_ref[...] = gw_acc[...].astype(gw_ref.dtype)
            grp = _head_of_lane((MEM_LEN, D_G))
            dk_sum = jnp.zeros((MEM_LEN, D_G), F32)
            dv_sum = jnp.zeros((MEM_LEN, D_G), F32)
            for h in range(N_SUB):
                dk_sum = dk_sum + jnp.where(grp == h, g["dk"][h], 0.0)
                dv_sum = dv_sum + jnp.where(grp == h, g["dv"][h], 0.0)
            gkv_ref[...] = _mm_tn(mem_ref[...], jnp.concatenate([dk_sum, dv_sum], axis=1)).astype(gkv_ref.dtype)
            if from_loss:
                total = jnp.sum(lacc[...], axis=-1, keepdims=True) * (0.5 / D_MODEL)
                slab_ref[LOSS_ROW:LOSS_ROW + 1, :] = jnp.broadcast_to(total, (1, D_MODEL))

    rev = lambda i: (nt - 1 - i, 0)
    fixed = lambda i: (0, 0)
    act_spec = pl.BlockSpec((tile, D_MODEL), rev)
    grad_specs = tuple(pl.BlockSpec(shape, lambda i, _nd=len(shape): (0,) * _nd) for _, shape in _BRANCH_GRADS)
    grad_shapes = tuple(jax.ShapeDtypeStruct(shape, GRAD_DTYPE if n in _BRANCH_GRADS_NARROW else F32)
                        for n, shape in _BRANCH_GRADS)
    consts = [km, vm] + nat_arrays + [pw, ca8, dw8, repl["ln_g"], repl["ln_b"], w_out, mem]
    outs = pl.pallas_call(
        body, grid=(nt,),
        in_specs=[act_spec, act_spec, pl.BlockSpec((tile, D_MIX), rev), pl.BlockSpec((tile, D_IN), rev),
                  pl.BlockSpec((HALO, D_IN), lambda i: (jnp.maximum((nt - 1 - i) * hb - 1, 0), 0)),
                  pl.BlockSpec((tile, 2 * D_G), rev)]
        + [_full_spec(a) for a in consts] + _tie_specs(ties),
        out_specs=(act_spec, pl.BlockSpec((tile, D_IN), rev), pl.BlockSpec((D_MIX, D_MODEL), fixed),
                   pl.BlockSpec((8, D_MODEL), fixed), pl.BlockSpec((D_MODEL, 2 * D_G), fixed)) + grad_specs,
        out_shape=(jax.ShapeDtypeStruct((s, D_MODEL), F32), jax.ShapeDtypeStruct((s, D_IN), MM_DTYPE),
                   jax.ShapeDtypeStruct((D_MIX, D_MODEL), GRAD_DTYPE), jax.ShapeDtypeStruct((8, D_MODEL), F32),
                   jax.ShapeDtypeStruct((D_MODEL, 2 * D_G), GRAD_DTYPE)) + grad_shapes,
        scratch_shapes=[pltpu.VMEM((HALO + tile, D_G), F32)] * 4 + [_residue_scratch(tile)] * 2
        + [pltpu.VMEM((D_MIX, D_MODEL), F32), pltpu.VMEM((1, D_MODEL), F32)]
        + [pltpu.VMEM(shape, F32) for _, shape in _BRANCH_ACC + _BRANCH_W_SCRATCH],
        name="layer_bwd_loss" if from_loss else "layer_bwd", compiler_params=_params("arbitrary"),
    )(target if from_loss else up, z, h, proj, proj, kept, *consts, *ties)
    return outs[0], outs[1], outs[2], outs[3], outs[4], dict(zip([n for n, _ in _BRANCH_GRADS], outs[5:]))


def _dx_matmul(dproj, wt_in, dz, ties=(), tm=512):
    s = dproj.shape[0]

    def body(dp_ref, w_ref, dz_ref, *rest):
        o_ref = rest[len(ties)]
        o_ref[...] = _mm(dp_ref[...], w_ref[...]) + ALPHA * dz_ref[...]

    row = lambda i: (i, 0)
    return pl.pallas_call(
        body, grid=(s // tm,),
        in_specs=[pl.BlockSpec((tm, D_IN), row), _full_spec(wt_in), pl.BlockSpec((tm, D_MODEL), row)]
        + _tie_specs(ties),
        out_specs=pl.BlockSpec((tm, D_MODEL), row),
        out_shape=jax.ShapeDtypeStruct((s, D_MODEL), F32), name="dx_mm", compiler_params=_params("arbitrary"),
    )(dproj, wt_in, dz, *ties)


def _dw_in_matmul(x, dproj, ties=(), tk=512):
    s = x.shape[0]
    nk = s // tk
    blk = 2 * W_IN_COLS

    def body(x_ref, dp_ref, *rest):
        o_ref, acc = rest[len(ties):]
        k = pl.program_id(0)

        @pl.when(k == 0)
        def _():
            acc[...] = jnp.zeros_like(acc)

        xb = x_ref[...].astype(MM_DTYPE)
        for j in range(D_IN // blk):
            acc[j * blk:(j + 1) * blk, :] += _mm_tn(dp_ref[:, j * blk:(j + 1) * blk], xb)

        @pl.when(k == nk - 1)
        def _():
            o_ref[...] = acc[...].astype(o_ref.dtype)

    return pl.pallas_call(
        body, grid=(nk,),
        in_specs=[pl.BlockSpec((tk, D_MODEL), lambda k: (k, 0)), pl.BlockSpec((tk, D_IN), lambda k: (k, 0))]
        + _tie_specs(ties),
        out_specs=pl.BlockSpec((D_IN, D_MODEL), lambda k: (0, 0)),
        out_shape=jax.ShapeDtypeStruct((D_IN, D_MODEL), GRAD_DTYPE),
        scratch_shapes=[pltpu.VMEM((D_IN, D_MODEL), F32)], name="dw_in_mm", compiler_params=_params("arbitrary"),
    )(x, dproj, *ties)


def _in_bwd(x, dproj, wt_in, dz, ties=(), tm=512):
    s = x.shape[0]
    n_steps = s // tm

    assert wt_in.dtype == GRAD_DTYPE
    blk = 2 * W_IN_COLS

    def body(x_ref, dp_ref, w_hbm, dz_ref, *rest):
        o_ref, gw_hbm, w_vmem, acc, sem = rest[len(ties):]
        i = pl.program_id(0)

        @pl.when(i == 0)
        def _():
            fetch = pltpu.make_async_copy(w_hbm, w_vmem, sem)
            fetch.start()
            acc[...] = jnp.zeros_like(acc)
            fetch.wait()

        o_ref[...] = _mm(dp_ref[...], w_vmem[...]) + ALPHA * dz_ref[...]
        xb = x_ref[...].astype(MM_DTYPE)
        for j in range(D_IN // blk):
            acc[j * blk:(j + 1) * blk, :] += _mm_tn(dp_ref[:, j * blk:(j + 1) * blk], xb)

        @pl.when(i == n_steps - 1)
        def _():
            w_vmem[...] = acc[...].astype(w_vmem.dtype)
            emit = pltpu.make_async_copy(w_vmem, gw_hbm, sem)
            emit.start()
            emit.wait()

    row = lambda i: (i, 0)
    any_spec = pl.BlockSpec(memory_space=pl.ANY)
    return pl.pallas_call(
        body, grid=(n_steps,),
        in_specs=[pl.BlockSpec((tm, D_MODEL), row), pl.BlockSpec((tm, D_IN), row), any_spec,
                  pl.BlockSpec((tm, D_MODEL), row)] + _tie_specs(ties),
        out_specs=(pl.BlockSpec((tm, D_MODEL), row), any_spec),
        out_shape=(jax.ShapeDtypeStruct((s, D_MODEL), F32), jax.ShapeDtypeStruct((D_IN, D_MODEL), GRAD_DTYPE)),
        scratch_shapes=[pltpu.VMEM((D_IN, D_MODEL), wt_in.dtype), pltpu.VMEM((D_IN, D_MODEL), F32),
                        pltpu.SemaphoreType.DMA],
        name="in_bwd", compiler_params=_params("arbitrary"),
    )(x, dproj, wt_in, dz, *ties)


def _layer_fwd(x, mem, layer, repl, gw, ties=()):
    proj, h, z, kept, km, vm, xn = _layer_fwd_fused(x, gw["wt_in"], mem, gw["w_kv"], layer, repl, gw["pw"],
                                                    gw["ca8"], gw["dw8"], gw["w_out"], layer < DEPTH - 1, ties)
    return xn, (x, proj, h, z, km, vm, kept)


def _layer_bwd_a(up, target, mem, layer, repl, gw, saved, ties=()):
    x_in, proj, h, z, km, vm, kept = saved
    dz, dproj, g_w_out, g1024, g_w_kv, bg = _layer_bwd_fused(up, target, z, h, proj, kept, km, vm, mem, layer, repl,
                                                             gw["w_out"], gw["pw"], gw["ca8"], gw["dw8"], ties)
    grads = {n: bg[n] for n in ("g256", "sg_w", "sg_b", "pool_w", "conv_a_w", "cc_dw_w")}
    grads.update(w_out=g_w_out.reshape(N_DEV, D_MIX // N_DEV, D_MODEL), g1024=g1024,
                 w_kv=g_w_kv.reshape(N_DEV, D_MODEL // N_DEV, 2 * D_G),
                 cc_pw_w=bg["cc_pw_w"].reshape(N_DEV, CONV_CH, D_G))
    return dz, dproj, grads


def _landing_shapes(items):
    out = []
    for a, scatter, pick in items:
        shape = a.shape if scatter else (N_DEV,) + (a.shape if pick is None else a.shape[1:])
        out.append(jax.ShapeDtypeStruct(shape, a.dtype))
    return tuple(out)


def _exchange_sems(n):
    return [pltpu.SemaphoreType.DMA(((N_DEV - 1) * n,)), pltpu.SemaphoreType.DMA(((N_DEV - 1) * n,)),
            pltpu.SemaphoreType.DMA((n,))]


def _exchange_copies(modes, ins, outs, send_sems, recv_sems, local_sems):
    n = len(ins)
    x, y, c = lax.axis_index("x"), lax.axis_index("y"), lax.axis_index("c")
    me = 4 * x + 2 * y + c

    def src_of(a, dest):
        scatter, pick = modes[a]
        if scatter:
            return ins[a].at[dest]
        return ins[a] if pick is None else ins[a].at[pick]

    local = [pltpu.make_async_copy(src_of(a, me), outs[a].at[me], local_sems.at[a]) for a in range(n)]
    sends, recvs = [], []
    for k in range(1, N_DEV):
        px = 1 - x if k & 4 else x
        py = 1 - y if k & 2 else y
        pc = 1 - c if k & 1 else c
        peer = 4 * px + 2 * py + pc
        for a in range(n):
            sems = dict(send_sem=send_sems.at[(k - 1) * n + a], recv_sem=recv_sems.at[(k - 1) * n + a],
                        device_id=(px, py, pc), device_id_type=pl.DeviceIdType.MESH)
            sends.append(pltpu.make_async_remote_copy(src_ref=src_of(a, peer), dst_ref=outs[a].at[me], **sems))
            recvs.append(pltpu.make_async_remote_copy(src_ref=src_of(a, peer), dst_ref=outs[a].at[peer], **sems))
    return local, sends, recvs


def _gather_two_level(items, name):
    n = len(items)
    assert not any(scatter for _, scatter, _ in items)
    picks = [pick for _, _, pick in items]

    def body(*refs):
        ins, outs = refs[:n], refs[n:2 * n]
        send_sems, recv_sems, local_sems = refs[2 * n:]
        x, y, c = lax.axis_index("x"), lax.axis_index("y"), lax.axis_index("c")
        sib = 1 - c
        chips = [(1 - x, y), (x, 1 - y), (1 - x, 1 - y)]

        def slot(a, px, py, pc):
            return outs[a].at[4 * px + 2 * py + pc]

        def copy(k, a, src, block, to):
            return pltpu.make_async_remote_copy(
                src_ref=src, dst_ref=slot(a, *block), send_sem=send_sems.at[k * n + a],
                recv_sem=recv_sems.at[k * n + a], device_id=to, device_id_type=pl.DeviceIdType.MESH)

        own = [ins[a] if picks[a] is None else ins[a].at[picks[a]] for a in range(n)]
        local = [pltpu.make_async_copy(own[a], slot(a, x, y, c), local_sems.at[a]) for a in range(n)]
        first = [copy(0, a, own[a], (x, y, c), (x, y, sib)) for a in range(n)]
        first += [copy(1 + j, a, own[a], (x, y, c), (*chip, c)) for j, chip in enumerate(chips[:2]) for a in range(n)]
        for cp in local + first:
            cp.start()

        def pass_on(j, a):
            chip = chips[j]
            copy(1 + j, a, own[a], (*chip, c), (x, y, c)).wait_recv()
            fwd = copy(4 + j, a, slot(a, *chip, c), (*chip, c), (x, y, sib))
            fwd.start()
            return fwd

        passed = [pass_on(j, a) for j in range(2) for a in range(n)]
        south = c == 0
        via = tuple(jnp.where(south, p, q) for p, q in zip(chips[0], chips[1]))
        blk = tuple(jnp.where(south, q, p) for p, q in zip(chips[0], chips[1]))
        relayed = [copy(3, a, slot(a, *blk, c), (*blk, c), (*via, c)) for a in range(n)]
        for cp in relayed:
            cp.start()
        passed += [pass_on(2, a) for a in range(n)]
        first += relayed
        for a in range(n):
            copy(0, a, own[a], (x, y, sib), (x, y, c)).wait_recv()
        for j, chip in enumerate(chips):
            for a in range(n):
                copy(4 + j, a, own[a], (*chip, sib), (x, y, c)).wait_recv()
        for cp in first + passed:
            cp.wait_send()
        for cp in local:
            cp.wait()

    any_spec = pl.BlockSpec(memory_space=pl.ANY)
    return pl.pallas_call(
        body, in_specs=[any_spec] * n, out_specs=(any_spec,) * n, out_shape=_landing_shapes(items),
        scratch_shapes=[pltpu.SemaphoreType.DMA((7 * n,)), pltpu.SemaphoreType.DMA((7 * n,)),
                        pltpu.SemaphoreType.DMA((n,))],
        name=name,
    )(*[a for a, _, _ in items])


_HBM_SPEC = pl.BlockSpec(memory_space=pltpu.HBM)
_SEM_SPEC = pl.BlockSpec(memory_space=pltpu.SEMAPHORE)
_SPLIT_PARAMS = pltpu.CompilerParams(has_side_effects=pltpu.SideEffectType.DATAFLOW_SIDE_EFFECTING)


def _split_start(srcs, lands, plan, sem_shapes, name):
    n_src, n_land = len(srcs), len(lands)
    n_buf = n_src + n_land
    bufs = [pltpu.with_memory_space_constraint(a, pltpu.HBM) for a in list(srcs) + list(lands)]

    def body(*refs):
        local, sends, _ = plan(refs[:n_src], refs[n_src:n_buf], *refs[n_buf:n_buf + 3])
        for cp in local + sends:
            cp.start()
        token = refs[-1]
        token[...] = jnp.zeros_like(token)

    res = pl.pallas_call(
        body, name=name, in_specs=[_HBM_SPEC] * n_buf,
        out_shape=tuple(sem_shapes) + tuple(pltpu.HBM(a.shape, a.dtype) for a in bufs)
        + (jax.ShapeDtypeStruct((8, 128), F32),),
        out_specs=(_SEM_SPEC,) * 3 + (_HBM_SPEC,) * n_buf + (pl.BlockSpec(memory_space=pltpu.VMEM),),
        input_output_aliases={i: 3 + i for i in range(n_buf)}, compiler_params=_SPLIT_PARAMS,
    )(*bufs)
    return dict(sems=res[:3], srcs=res[3:3 + n_src], lands=res[3 + n_src:3 + n_buf], token=res[-1], plan=plan)


def _split_wait(ticket, after, name, with_srcs=False):
    n_src, n_land = len(ticket["srcs"]), len(ticket["lands"])
    n_buf = n_src + n_land
    plan = ticket["plan"]

    def body(*refs):
        local, sends, recvs = plan(refs[:n_src], refs[n_src:n_buf], *refs[n_buf:n_buf + 3])
        for cp in recvs:
            cp.wait_recv()
        for cp in sends:
            cp.wait_send()
        for cp in local:
            cp.wait()

    bufs = list(ticket["srcs"]) + list(ticket["lands"])
    res = pl.pallas_call(
        body, name=name,
        in_specs=[_HBM_SPEC] * n_buf + [_SEM_SPEC] * 3 + [pl.BlockSpec(memory_space=pl.ANY)] * len(after),
        out_shape=tuple(pltpu.HBM(a.shape, a.dtype) for a in bufs), out_specs=(_HBM_SPEC,) * n_buf,
        input_output_aliases={i: i for i in range(n_buf)}, compiler_params=_SPLIT_PARAMS,
    )(*bufs, *ticket["sems"], *after)
    return res if with_srcs else res[n_src:]


def _empty_landings(items):
    return [lax.empty(s.shape, s.dtype) for s in _landing_shapes(items)]


def _exchange_start(items, name):
    modes = [(scatter, pick) for _, scatter, pick in items]
    plan = lambda ins, outs, *sems: _exchange_copies(modes, ins, outs, *sems)
    return _split_start([a for a, _, _ in items], _empty_landings(items), plan, _exchange_sems(len(items)), name)


def _two_level_plans(picks):
    n = len(picks)

    def place():
        x, y, c = lax.axis_index("x"), lax.axis_index("y"), lax.axis_index("c")
        return x, y, c, 1 - c, [(1 - x, y), (x, 1 - y), (1 - x, 1 - y)]

    def copy(outs, send_sems, recv_sems, k, a, src, block, to):
        px, py, pc = block
        return pltpu.make_async_remote_copy(
            src_ref=src, dst_ref=outs[a].at[4 * px + 2 * py + pc], send_sem=send_sems.at[k * n + a],
            recv_sem=recv_sems.at[k * n + a], device_id=to, device_id_type=pl.DeviceIdType.MESH)

    def between_chips(ins, outs, send_sems, recv_sems, local_sems):
        x, y, c, sib, chips = place()
        own = [ins[a] if picks[a] is None else ins[a].at[picks[a]] for a in range(n)]
        mk = lambda *args: copy(outs, send_sems, recv_sems, *args)
        local = [pltpu.make_async_copy(own[a], outs[a].at[4 * x + 2 * y + c], local_sems.at[a]) for a in range(n)]
        sends = [mk(0, a, own[a], (x, y, c), (x, y, sib)) for a in range(n)]
        sends += [mk(1 + j, a, own[a], (x, y, c), (*chip, c)) for j, chip in enumerate(chips) for a in range(n)]
        recvs = [mk(0, a, own[a], (x, y, sib), (x, y, c)) for a in range(n)]
        recvs += [mk(1 + j, a, own[a], (*chip, c), (x, y, c)) for j, chip in enumerate(chips) for a in range(n)]
        return local, sends, recvs

    def within_chip(ins, outs, send_sems, recv_sems, local_sems):
        x, y, c, sib, chips = place()
        mk = lambda *args: copy(outs, send_sems, recv_sems, *args)
        slot = lambda a, px, py, pc: outs[a].at[4 * px + 2 * py + pc]
        sends = [mk(j, a, slot(a, *chip, c), (*chip, c), (x, y, sib)) for j, chip in enumerate(chips)
                 for a in range(n)]
        recvs = [mk(j, a, slot(a, *chip, c), (*chip, sib), (x, y, c)) for j, chip in enumerate(chips)
                 for a in range(n)]
        return [], sends, recvs

    sems = lambda k: [pltpu.SemaphoreType.DMA((k * n,)), pltpu.SemaphoreType.DMA((k * n,)),
                      pltpu.SemaphoreType.DMA((n,))]
    return between_chips, sems(4), within_chip, sems(3)


N_CHIPS = N_DEV // 2


def _pair_plans():
    def mesh_pos():
        return lax.axis_index("x"), lax.axis_index("y"), lax.axis_index("c")

    def to_sibling(ins, outs, send_sems, recv_sems, local_sems):
        x, y, c = mesh_pos()
        sib = 1 - c
        sends = [pltpu.make_async_remote_copy(
            src_ref=ins[0].at[2 * j + sib], dst_ref=outs[0].at[j], send_sem=send_sems.at[j],
            recv_sem=recv_sems.at[j], device_id=(x, y, sib), device_id_type=pl.DeviceIdType.MESH)
            for j in range(N_CHIPS)]
        return [], sends, sends

    def between_chips(ins, outs, send_sems, recv_sems, local_sems):
        x, y, c = mesh_pos()
        chip = 2 * x + y
        local = [pltpu.make_async_copy(ins[0].at[chip], outs[0].at[chip], local_sems.at[0])]
        sends, recvs = [], []
        for k in range(1, N_CHIPS):
            px = 1 - x if k & 2 else x
            py = 1 - y if k & 1 else y
            sems = dict(send_sem=send_sems.at[k - 1], recv_sem=recv_sems.at[k - 1], device_id=(px, py, c),
                        device_id_type=pl.DeviceIdType.MESH)
            sends.append(pltpu.make_async_remote_copy(src_ref=ins[0].at[2 * px + py], dst_ref=outs[0].at[chip], **sems))
            recvs.append(pltpu.make_async_remote_copy(src_ref=ins[0].at[2 * px + py], dst_ref=outs[0].at[2 * px + py],
                                                      **sems))
        return local, sends, recvs

    dma = lambda k: pltpu.SemaphoreType.DMA((k,))
    return to_sibling, [dma(N_CHIPS), dma(N_CHIPS), dma(1)], between_chips, [dma(N_CHIPS - 1), dma(N_CHIPS - 1), dma(1)]


def _pair_add(slabs, from_sibling, tr=W_IN_COLS):
    _, rows, cols = slabs.shape
    core = lax.axis_index("c").astype(jnp.int32).reshape(1)

    def body(core_ref, a_ref, b_ref, o_ref):
        o_ref[...] = (a_ref[...].astype(F32) + b_ref[...].astype(F32)).astype(o_ref.dtype)

    return pl.pallas_call(
        body,
        grid_spec=pltpu.PrefetchScalarGridSpec(
            num_scalar_prefetch=1, grid=(N_CHIPS, rows // tr),
            in_specs=[pl.BlockSpec((None, tr, cols), lambda j, i, core_ref: (2 * j + core_ref[0], i, 0)),
                      pl.BlockSpec((None, tr, cols), lambda j, i, core_ref: (j, i, 0))],
            out_specs=pl.BlockSpec((None, tr, cols), lambda j, i, core_ref: (j, i, 0))),
        out_shape=jax.ShapeDtypeStruct((N_CHIPS, rows, cols), slabs.dtype), name="pair_add",
        compiler_params=_params("arbitrary", "arbitrary"),
    )(core, slabs, from_sibling)


def _adam_math(g, w, m, v):
    m_new = ADAM_B1 * m + (1.0 - ADAM_B1) * g
    v_new = ADAM_B2 * v + (1.0 - ADAM_B2) * (g * g)
    m_hat = m_new / (1.0 - ADAM_B1 ** ADAM_STEP)
    v_hat = v_new / (1.0 - ADAM_B2 ** ADAM_STEP)
    return -ADAM_LR * (m_hat / (jnp.sqrt(v_hat) + ADAM_EPS) + ADAM_WD * w), m_new, v_new


def _adamw_big(parts, w, m, v, layer, prev, name, tr):
    depth, rows, cols = w.shape
    n_parts = parts.shape[0]

    def body(p_ref, w_ref, m_ref, v_ref, *rest):
        g_out, d_out, m_out, v_out = rest[len(prev):]
        g = p_ref[0].astype(F32)
        for q in range(1, n_parts):
            g = g + p_ref[q].astype(F32)
        d, m_new, v_new = _adam_math(g, w_ref[...], m_ref[...], v_ref[...])
        g_out[...] = g
        d_out[...] = d
        m_out[...] = m_new
        v_out[...] = v_new

    blk = pl.BlockSpec((None, tr, cols), lambda i: (layer, i, 0))
    shp = jax.ShapeDtypeStruct((depth, rows, cols), F32)
    return pl.pallas_call(
        body, grid=(rows // tr,),
        in_specs=[pl.BlockSpec((n_parts, tr, cols), lambda i: (0, i, 0)), blk, blk, blk]
        + [pl.BlockSpec(memory_space=pl.ANY)] * len(prev),
        out_specs=(blk,) * 4, out_shape=(shp,) * 4,
        input_output_aliases={4 + j: j for j in range(len(prev))},
        name=name, compiler_params=_params("arbitrary"),
    )(parts, w, m, v, *prev)


_SMALL_TENSORS = (("conv_a_w", "conv_a_w", None), ("cc_dw_w", "cc_dw_w", None), ("cc_pw_w", "cc_pw_w", None),
                  ("sg_w", "sg_w", None), ("pool_w", "pool_w", None), ("sg_b", "sg_b", None)) \
    + tuple((n, "g256", k) for k, n in enumerate(G256_ROWS)) + tuple((n, "g1024", k) for k, n in enumerate(G1024_ROWS))
_SMALL_LANDINGS = ("conv_a_w", "cc_dw_w", "cc_pw_w", "sg_w", "pool_w", "sg_b", "g256", "g1024")
_TAPS_FIRST = ("conv_a_w", "cc_dw_w")


def _adamw_small(landings, wts, mom, var):
    names = [n for n, _, _ in _SMALL_TENSORS]
    n_land = DEPTH * len(_SMALL_LANDINGS)
    n_t = len(names)

    def body(*refs):
        land = [dict(zip(_SMALL_LANDINGS, refs[l * len(_SMALL_LANDINGS):(l + 1) * len(_SMALL_LANDINGS)]))
                for l in range(DEPTH)]
        w_refs = dict(zip(names, refs[n_land:n_land + n_t]))
        m_refs = dict(zip(names, refs[n_land + n_t:n_land + 2 * n_t]))
        v_refs = dict(zip(names, refs[n_land + 2 * n_t:n_land + 3 * n_t]))
        outs = refs[n_land + 3 * n_t:]
        out_refs = {n: outs[4 * k:4 * k + 4] for k, n in enumerate(names)}
        loss_ref = outs[4 * n_t]
        for name, key, row in _SMALL_TENSORS:
            for l in range(DEPTH):
                src = land[l][key]
                if row is not None:
                    part = lambda q: src[q, row:row + 1, :]
                    at = (slice(l, l + 1),)
                elif name == "sg_b":
                    part = lambda q: src[q, 0:N_SUB, :]
                    at = (l,)
                elif name in _TAPS_FIRST:
                    part = lambda q: src[q]
                    at = (slice(None), l)
                else:
                    part = lambda q: src[q]
                    at = (l,)
                g = part(0).astype(F32)
                for q in range(1, N_DEV):
                    g = g + part(q).astype(F32)
                d, m_new, v_new = _adam_math(g, w_refs[name][at], m_refs[name][at], v_refs[name][at])
                for ref, val in zip(out_refs[name], (g, d, m_new, v_new)):
                    ref[at] = val
        src = land[DEPTH - 1]["g1024"]
        loss = src[0, LOSS_ROW:LOSS_ROW + 1, 0:128]
        for q in range(1, N_DEV):
            loss = loss + src[q, LOSS_ROW:LOSS_ROW + 1, 0:128]
        loss_ref[...] = loss

    ins = [landings[l][k] for l in range(DEPTH) for k in _SMALL_LANDINGS] \
        + [src[n] for src in (wts, mom, var) for n in names]
    out_shape = tuple(jax.ShapeDtypeStruct(wts[n].shape, F32) for n in names for _ in range(4)) \
        + (jax.ShapeDtypeStruct((1, 128), F32),)
    res = pl.pallas_call(body, out_shape=out_shape, name="adamw_small", compiler_params=_params())(*ins)
    return {n: res[4 * k:4 * k + 4] for k, n in enumerate(names)}, res[4 * n_t]


_BIG = (("w_in", 224), ("w_out", 80), ("w_kv", 64))
_GRAD_ITEMS_EARLY = ("w_out", "w_kv", "cc_pw_w", "conv_a_w", "cc_dw_w")
_GRAD_ITEMS_REPL = ("g256", "sg_w", "sg_b", "pool_w", "g1024")


def _grad_items(grads):
    items = [(grads[n], True, None) for n in _GRAD_ITEMS_EARLY]
    return items + [(grads[n], False, None) for n in _GRAD_ITEMS_REPL]


def _landed(parts):
    return dict(zip(_GRAD_ITEMS_EARLY + _GRAD_ITEMS_REPL, parts))


def _gathered_weights(wt_in8, w_kv8, w_out8, pw8, ca8, dw8):
    return dict(wt_in=wt_in8.reshape(D_IN, D_MODEL), w_kv=w_kv8.reshape(D_MODEL, 2 * D_G),
                w_out=w_out8.reshape(D_MIX, D_MODEL), pw=pw8.reshape(D_G, D_G), ca8=ca8, dw8=dw8)


def kernel(x, mem, w_in, conv_a_w, sg_ln_g, sg_ln_b, sg_w, sg_b, pool_w, pool_scale, cc_dw_w, cc_dw_b, cc_ln_g, cc_ln_b, cc_pw_w, w_kv, w_out, ln_g, ln_b, loss_target, m_w_in, m_conv_a_w, m_sg_ln_g, m_sg_ln_b, m_sg_w, m_sg_b, m_pool_w, m_pool_scale, m_cc_dw_w, m_cc_dw_b, m_cc_ln_g, m_cc_ln_b, m_cc_pw_w, m_w_kv, m_w_out, m_ln_g, m_ln_b, v_w_in, v_conv_a_w, v_sg_ln_g, v_sg_ln_b, v_sg_w, v_sg_b, v_pool_w, v_pool_scale, v_cc_dw_w, v_cc_dw_b, v_cc_ln_g, v_cc_ln_b, v_cc_pw_w, v_w_kv, v_w_out, v_ln_g, v_ln_b):
    names = ("w_in", "conv_a_w", "sg_ln_g", "sg_ln_b", "sg_w", "sg_b", "pool_w", "pool_scale", "cc_dw_w", "cc_dw_b",
             "cc_ln_g", "cc_ln_b", "cc_pw_w", "w_kv", "w_out", "ln_g", "ln_b")
    wts = dict(zip(names, (w_in, conv_a_w, sg_ln_g, sg_ln_b, sg_w, sg_b, pool_w, pool_scale, cc_dw_w, cc_dw_b,
                           cc_ln_g, cc_ln_b, cc_pw_w, w_kv, w_out, ln_g, ln_b)))
    mom = dict(zip(names, (m_w_in, m_conv_a_w, m_sg_ln_g, m_sg_ln_b, m_sg_w, m_sg_b, m_pool_w, m_pool_scale,
                           m_cc_dw_w, m_cc_dw_b, m_cc_ln_g, m_cc_ln_b, m_cc_pw_w, m_w_kv, m_w_out, m_ln_g, m_ln_b)))
    var = dict(zip(names, (v_w_in, v_conv_a_w, v_sg_ln_g, v_sg_ln_b, v_sg_w, v_sg_b, v_pool_w, v_pool_scale,
                           v_cc_dw_w, v_cc_dw_b, v_cc_ln_g, v_cc_ln_b, v_cc_pw_w, v_w_kv, v_w_out, v_ln_g, v_ln_b)))
    repl = wts
    xs, mems, tgt = x[0], mem[0], loss_target[0]
    turned = {"w_in": (0, 2, 1), "conv_a_w": (1, 0, 2), "cc_dw_w": (1, 0, 2)}
    wts, mom, var = [{n: (jnp.transpose(a, turned[n]) if n in turned else a) for n, a in src.items()}
                     for src in (wts, mom, var)]
    wb = {n: wts[n].astype(MM_DTYPE) for n in ("w_in", "w_kv", "w_out", "cc_pw_w")}

    wt8_0, wkv8_0, wo8_0, pw8_0, ca8, dw8 = _gather_two_level(
        [(wb["w_in"], False, 0), (wb["w_kv"], False, 0), (wb["w_out"], False, 0), (wb["cc_pw_w"], False, 0),
         (wts["conv_a_w"], False, None), (wts["cc_dw_w"], False, None)], "gather_weights_0")
    gw0 = _gathered_weights(wt8_0, wkv8_0, wo8_0, pw8_0, ca8, dw8)
    items_1 = [(wb[n], False, 1) for n in ("w_in", "w_kv", "w_out", "cc_pw_w")]
    between_chips, sems_a, within_chip, sems_b = _two_level_plans([1] * len(items_1))
    chips_1 = _split_start([a for a, _, _ in items_1], _empty_landings(items_1), between_chips, sems_a,
                           "gather_weights_1a_start")
    x1, saved0 = _layer_fwd(xs, mems, 0, repl, gw0, ties=(chips_1["token"],))
    core_1 = _split_start([], _split_wait(chips_1, (x1,), "gather_weights_1a_wait"), within_chip, sems_b,
                          "gather_weights_1b_start")
    gw1 = _gathered_weights(*_split_wait(core_1, (core_1["token"],), "gather_weights_1b_wait"), ca8, dw8)
    _, saved1 = _layer_fwd(x1, mems, 1, repl, gw1)

    dz1, dproj1, g1 = _layer_bwd_a(None, tgt, mems, 1, repl, gw1, saved1)
    shards = lambda g: g.reshape(N_DEV, W_IN_COLS, D_MODEL)
    up, g_wt_in_1 = _in_bwd(saved1[0], dproj1, gw1["wt_in"], dz1)
    grads_1 = _exchange_start([(shards(g_wt_in_1), True, None)] + _grad_items(g1), "exchange_grads_1_start")
    dz0, dproj0, g0 = _layer_bwd_a(up, None, mems, 0, repl, gw0, saved0, (grads_1["token"],))
    early_0 = _exchange_start(_grad_items(g0), "exchange_grads_0a_start")
    g_wt_in_0 = _dw_in_matmul(saved0[0], dproj0, (early_0["token"],))
    to_sibling, sems_s, between_chips, sems_c = _pair_plans()
    half = lambda: [lax.empty((N_CHIPS, W_IN_COLS, D_MODEL), GRAD_DTYPE)]
    pair_0 = _split_start([shards(g_wt_in_0)], half(), to_sibling, sems_s, "exchange_grads_0b_start")
    w_in_1, *rest_1 = _split_wait(grads_1, (pair_0["token"],), "exchange_grads_1_wait")
    landed = [None, dict(_landed(rest_1), w_in=w_in_1)]
    big = {}
    for n, tr in _BIG:
        big[n] = _adamw_big(landed[1][n], wts[n], mom[n], var[n], 1, (), "adamw_" + n + "_1", tr)
    own_slabs, from_sibling = _split_wait(pair_0, tuple(big[n][0] for n, _ in _BIG), "exchange_grads_0b_wait",
                                          with_srcs=True)
    late_0 = _split_start([_pair_add(own_slabs, from_sibling)], half(), between_chips, sems_c,
                          "exchange_grads_0c_start")
    grad_x = _dx_matmul(dproj0, gw0["wt_in"], dz0, (late_0["token"],))
    landed[0] = _landed(_split_wait(early_0, (grad_x,), "exchange_grads_0a_wait"))
    for n, tr in _BIG[1:]:
        big[n] = _adamw_big(landed[0][n], wts[n], mom[n], var[n], 0, big[n], "adamw_" + n + "_0", tr)
    small, loss = _adamw_small(landed, wts, mom, var)
    (landed[0]["w_in"],) = _split_wait(late_0, (loss, big["w_out"][0], big["w_kv"][0]), "exchange_grads_0c_wait")
    big["w_in"] = _adamw_big(landed[0]["w_in"], wts["w_in"], mom["w_in"], var["w_in"], 0, big["w_in"],
                             "adamw_w_in_0", _BIG[0][1])

    res = {**small, **big}
    res = {n: ([jnp.transpose(a, turned[n]) for a in r] if n in turned else r) for n, r in res.items()}
    return (loss[0, 0], grad_x[None], *[res[n][0] for n in names], *[res[n][1] for n in names],
            *[res[n][2] for n in names], *[res[n][3] for n in names])
```

```python
import math

import jax
import jax.numpy as jnp
from jax import lax
from jax.experimental import pallas as pl
from jax.experimental.pallas import tpu as pltpu

F32 = jnp.float32
MM_DTYPE = jnp.bfloat16
GRAD_DTYPE = jnp.bfloat16

D_MODEL = 1024
DEPTH = 2
D_G = 256
N_GROUPS = 5
D_MIX = N_GROUPS * D_G
N_SUB = 4
HEAD_DIM = D_G // N_SUB
CONV_A = 3
CONV_D = 31
CHUNK = 128
POOL_WINDOWS = (2, 4, 8, 16)
MEM_LEN = 256
LN_EPS = 1e-5
ALPHA = (2.0 * DEPTH) ** 0.25
D_IN = 9 * D_G + D_MIX
ATT_SCALE = 1.0 / math.sqrt(HEAD_DIM)

ADAM_LR = 0.001
ADAM_B1 = 0.9
ADAM_B2 = 0.999
ADAM_EPS = 1e-08
ADAM_WD = 0.01
ADAM_STEP = 10

N_DEV = 8
W_IN_COLS = D_IN // N_DEV
CONV_CH = D_G // N_DEV
HALO = 32
FWD_TILE = 512
BWD_TILE = 256
VMEM_LIMIT = 56 * 1024 * 1024

C_XA, C_BA, C_CA, C_U, C_V, C_XC, C_DA, C_DG, C_Q = range(9)
C_GATE = 9 * D_G

G256_ROWS = ("sg_ln_g", "sg_ln_b", "pool_scale", "cc_dw_b", "cc_ln_g", "cc_ln_b")
G1024_ROWS = ("ln_g", "ln_b")
LOSS_ROW = 2


def _mm(a, b):
    return jnp.dot(a.astype(MM_DTYPE), b.astype(MM_DTYPE), preferred_element_type=F32)


def _mm_nt(a, b):
    return lax.dot_general(a.astype(MM_DTYPE), b.astype(MM_DTYPE), (((1,), (1,)), ((), ())),
                           preferred_element_type=F32)


def _mm_tn(a, b):
    return lax.dot_general(a.astype(MM_DTYPE), b.astype(MM_DTYPE), (((0,), (0,)), ((), ())),
                           preferred_element_type=F32)


def _sigmoid(x):
    return 0.5 * jnp.tanh(0.5 * x) + 0.5


_GELU_C = math.sqrt(2.0 / math.pi)
_GELU_A = 0.044715


def _gelu(x):
    th = jnp.tanh(_GELU_C * (x + _GELU_A * (x * x * x)))
    return 0.5 * x * (1.0 + th), th


def _dgelu(x, th):
    return 0.5 * (1.0 + th) + 0.5 * x * (1.0 - th * th) * (_GELU_C * (1.0 + 3.0 * _GELU_A * (x * x)))


def _ln_fwd(x, g, b):
    mu = jnp.mean(x, axis=-1, keepdims=True)
    xc = x - mu
    var = jnp.mean(xc * xc, axis=-1, keepdims=True)
    rstd = lax.rsqrt(var + LN_EPS)
    xhat = xc * rstd
    return xhat * g + b, xhat, rstd


def _ln_bwd(dy, xhat, rstd, g):
    dxhat = dy * g
    m1 = jnp.mean(dxhat, axis=-1, keepdims=True)
    m2 = jnp.mean(dxhat * xhat, axis=-1, keepdims=True)
    return rstd * (dxhat - m1 - xhat * m2)


def _rowsum(x):
    return jnp.sum(x, axis=0, keepdims=True)


def _col(ref, k):
    return ref[:, k * D_G:(k + 1) * D_G]


def _head_of_lane(shape):
    return jnp.right_shift(lax.broadcasted_iota(jnp.int32, shape, len(shape) - 1), HEAD_DIM.bit_length() - 1)


def _pool_select(lane_grp, s2, s4, s8, s16):
    return jnp.where(lane_grp == 0, s2, jnp.where(lane_grp == 1, s4, jnp.where(lane_grp == 2, s8, s16)))


def _row_view(ref, layer):
    return ref.at[pl.ds(layer, 1)]


def _make_residues(ext_ref, res_ref):
    rows = res_ref.shape[1]
    for r in range(1, 8):
        res_ref[r - 1] = ext_ref[pl.ds(r, rows), :]


def _rows_at(ext_ref, res_ref, off, tile):
    a, r = divmod(off, 8)
    if r == 0:
        return ext_ref[pl.ds(off, tile), :]
    return res_ref[r - 1, pl.ds(8 * a, tile), :]


def _residue_scratch(tile):
    return pltpu.VMEM((7, HALO + tile - 8, D_G), F32)


PROJ_SEGMENTS = ((C_XA * D_G, (C_CA + 1) * D_G), (C_XC * D_G, (C_XC + 1) * D_G), (C_DA * D_G, (C_DG + 1) * D_G),
                 (C_U * D_G, (C_V + 1) * D_G), (C_Q * D_G, (C_Q + 1) * D_G), (C_GATE, D_IN))


def _branch_forward(p_ref, ph_ref, first, row0, km_ref, vm_ref, w, ext_a, ext_c, ext_d, res_c, res_d, tile,
                    kept_ref=None, produce=None):
    r = {}
    produce = produce or (lambda: None)
    produce()
    produce()
    xa, ba, ca = _col(p_ref, C_XA), _col(p_ref, C_BA), _col(p_ref, C_CA)
    g_a = ca * xa
    ext_a[0:HALO] = jnp.where(first, 0.0, _col(ph_ref, C_CA) * _col(ph_ref, C_XA))
    ext_a[HALO:HALO + tile] = g_a
    conv_a = w["conv_a"][0:1, :] * ext_a[pl.ds(HALO - 2, tile), :]
    for k in range(1, CONV_A):
        conv_a = conv_a + w["conv_a"][k:k + 1, :] * ext_a[pl.ds(HALO - 2 + k, tile), :]
    r.update(xa=xa, ba=ba, ca=ca, g_a=g_a, conv_a=conv_a)
    ya = ba * conv_a

    produce()
    lane_grp = _head_of_lane((tile, D_G))
    trow = row0 + lax.broadcasted_iota(jnp.int32, (tile, D_G), 0)
    win = _pool_select(lane_grp, 2, 4, 8, 16)
    inv_cnt = 1.0 / jnp.minimum(trow + 1, win).astype(F32)
    if kept_ref is None:
        xc = _col(p_ref, C_XC)
        ext_c[0:HALO] = jnp.where(first, 0.0, _col(ph_ref, C_XC))
        ext_c[HALO:HALO + tile] = xc
        _make_residues(ext_c, res_c)
        acc = xc
        sums = {}
        for k in range(1, POOL_WINDOWS[-1]):
            acc = acc + _rows_at(ext_c, res_c, HALO - k, tile)
            if k + 1 in POOL_WINDOWS:
                sums[k + 1] = acc
        ypre = _pool_select(lane_grp, sums[2], sums[4], sums[8], sums[16]) * inv_cnt - xc
    else:
        ypre = kept_ref[:, D_G:2 * D_G]
    pool_mm = _mm(ypre, w["pool_wbd"][...])
    yc = pool_mm * w["pool_scale"][...]
    r.update(lane_grp=lane_grp, inv_cnt=inv_cnt, ypre=ypre, pool_mm=pool_mm)

    produce()
    da, dg = _col(p_ref, C_DA), _col(p_ref, C_DG)
    sig_dg = _sigmoid(dg)
    hd = da * sig_dg
    if kept_ref is None:
        ext_d[0:HALO] = jnp.where(first, 0.0, _col(ph_ref, C_DA) * _sigmoid(_col(ph_ref, C_DG)))
        ext_d[HALO:HALO + tile] = hd
        _make_residues(ext_d, res_d)
        conv_d = w["cc_dw_b"][...] + w["cc_dw_w"][0:1, :] * _rows_at(ext_d, res_d, HALO - (CONV_D - 1), tile)
        for j in range(1, CONV_D):
            conv_d = conv_d + w["cc_dw_w"][j:j + 1, :] * _rows_at(ext_d, res_d, HALO - (CONV_D - 1) + j, tile)
    else:
        conv_d = kept_ref[:, 0:D_G]
    r["kept"] = (conv_d, ypre)
    ln_d, xhat_d, rstd_d = _ln_fwd(conv_d, w["cc_ln_g"][...], w["cc_ln_b"][...])
    sig_ln = _sigmoid(ln_d)
    act_d = ln_d * sig_ln
    yd = _mm(act_d, w["cc_pw_w"][...])
    r.update(da=da, sig_dg=sig_dg, hd=hd, ln_d=ln_d, xhat_d=xhat_d, rstd_d=rstd_d, sig_ln=sig_ln, act_d=act_d)
    produce()

    u, v = _col(p_ref, C_U), _col(p_ref, C_V)
    ug, th_u = _gelu(u)
    vg, th_v = _gelu(v)
    vn, xhat_v, rstd_v = _ln_fwd(vg, w["sg_ln_g"][...], w["sg_ln_b"][...])
    tri = (lax.broadcasted_iota(jnp.int32, (CHUNK, CHUNK), 0)
           >= lax.broadcasted_iota(jnp.int32, (CHUNK, CHUNK), 1))
    wm = [jnp.where(tri, w["sg_w"][h], 0.0).astype(MM_DTYPE) for h in range(N_SUB)]
    lo = lax.broadcasted_iota(jnp.int32, (CHUNK, 2 * HEAD_DIM), 1) < HEAD_DIM
    chunks = []
    for c in range(tile // CHUNK):
        halves = []
        for hf in range(2):
            vh = vn[c * CHUNK:(c + 1) * CHUNK, hf * 128:(hf + 1) * 128]
            halves.append(_mm(wm[2 * hf], jnp.where(lo, vh, 0.0)) + _mm(wm[2 * hf + 1], jnp.where(lo, 0.0, vh)))
        chunks.append(jnp.concatenate(halves, axis=1) + w["sg_bias"][...])
    mixed = jnp.concatenate(chunks, axis=0)
    yb = ug * mixed
    r.update(u=u, v=v, ug=ug, th_u=th_u, th_v=th_v, vn=vn, xhat_v=xhat_v, rstd_v=rstd_v, wm=wm, lo=lo,
             mixed=mixed, tri=tri)

    produce()
    q = _col(p_ref, C_Q)
    ye = jnp.zeros((tile, D_G), F32)
    probs = []
    for h in range(N_SUB):
        s = _mm_nt(q, km_ref[h]) * ATT_SCALE
        e = jnp.exp(s - jnp.max(s, axis=-1, keepdims=True))
        p = e * (1.0 / jnp.sum(e, axis=-1, keepdims=True))
        probs.append(p)
        ye = ye + _mm(p, vm_ref[h])
    r.update(q=q, probs=probs)

    gate = p_ref[:, C_GATE:C_GATE + D_MIX]
    sig_gate = _sigmoid(gate)
    r.update(gate=gate, sig_gate=sig_gate, branch_out=(ya, yb, yc, yd, ye))
    return r


_BRANCH_REPL = ("sg_ln_g", "sg_ln_b", "sg_w", "sg_b", "pool_w", "pool_scale", "cc_dw_b", "cc_ln_g", "cc_ln_b")
_BRANCH_W_SCRATCH = (("conv_a", (CONV_A, D_G)), ("cc_dw_w", (CONV_D, D_G)), ("sg_bias", (CHUNK, D_G)),
                     ("pool_wbd", (D_G, D_G)), ("sgb8", (8, CHUNK)))


def _branch_weights(layer, nat, pw_ref, ca_ref, dw_ref, scr, init):
    @pl.when(init)
    def _():
        for p in range(N_DEV):
            scr["conv_a"][:, p * CONV_CH:(p + 1) * CONV_CH] = ca_ref[p, :, layer, :]
            scr["cc_dw_w"][:, p * CONV_CH:(p + 1) * CONV_CH] = dw_ref[p, :, layer, :]
        scr["sgb8"][...] = jnp.zeros((8, CHUNK), F32)
        scr["sgb8"][0:N_SUB] = nat["sg_b"][layer]
        sgb_t = scr["sgb8"][...].T
        head = _head_of_lane((CHUNK, D_G))
        bias = jnp.zeros((CHUNK, D_G), F32)
        for h in range(N_SUB):
            bias = jnp.where(head == h, sgb_t[:, h:h + 1], bias)
        scr["sg_bias"][...] = bias
        scr["pool_wbd"][...] = jnp.zeros((D_G, D_G), F32)
        for gi in range(N_SUB):
            sl = slice(gi * HEAD_DIM, (gi + 1) * HEAD_DIM)
            scr["pool_wbd"][sl, sl] = nat["pool_w"][layer, gi]

    w = {n: _row_view(nat[n], layer) for n in ("sg_ln_g", "sg_ln_b", "pool_scale", "cc_dw_b", "cc_ln_g", "cc_ln_b")}
    w.update(conv_a=scr["conv_a"], cc_dw_w=scr["cc_dw_w"], sg_bias=scr["sg_bias"], pool_wbd=scr["pool_wbd"],
             sg_w=nat["sg_w"].at[layer], cc_pw_w=pw_ref)
    return w


def _full_spec(a):
    nd = a.ndim
    return pl.BlockSpec(a.shape, lambda *_, _nd=nd: (0,) * _nd)


def _tie_specs(ties):
    return [pl.BlockSpec((8, 128), lambda *_: (0, 0)) for _ in ties]


def _params(*sem):
    return pltpu.CompilerParams(dimension_semantics=sem or None, vmem_limit_bytes=VMEM_LIMIT)


def _kv_project(mem_ref, w_ref, km_ref, vm_ref):
    kv = _mm(mem_ref[...], w_ref[...])
    k, v = kv[:, :D_G], kv[:, D_G:]
    grp = _head_of_lane((MEM_LEN, D_G))
    for h in range(N_SUB):
        km_ref[h] = jnp.where(grp == h, k, 0.0).astype(km_ref.dtype)
        vm_ref[h] = jnp.where(grp == h, v, 0.0).astype(vm_ref.dtype)


def _layer_fwd_fused(x, wt_in, mem, w_kv, layer, repl, pw, ca8, dw8, w_out, want_xn, ties=(), tile=FWD_TILE):
    s = x.shape[0]
    nat_arrays = [repl[n] for n in _BRANCH_REPL]
    n_nat, nt = len(nat_arrays), len(ties)

    def body(x_ref, wt_ref, mem_ref, wkv_ref, *rest):
        nat = dict(zip(_BRANCH_REPL, rest[:n_nat]))
        pw_ref, ca_ref, dw_ref, wo_ref, g_ref, b_ref = rest[n_nat:n_nat + 6]
        rest = rest[n_nat + 6 + nt:]
        p_ref, h_ref, z_ref, cd_ref, km_ref, vm_ref = rest[:6]
        rest = rest[6:]
        if want_xn:
            xn_ref, rest = rest[0], rest[1:]
        ph_ref, ext_a, ext_c, ext_d, res_c, res_d = rest[:6]
        scr = dict(zip([n for n, _ in _BRANCH_W_SCRATCH], rest[6:]))
        i = pl.program_id(0)

        @pl.when(i == 0)
        def _():
            ph_ref[...] = jnp.zeros_like(ph_ref)
            _kv_project(mem_ref, wkv_ref, km_ref, vm_ref)

        xt = x_ref[...]
        xb = xt.astype(MM_DTYPE)
        segments = iter(PROJ_SEGMENTS)

        def produce():
            lo, hi = next(segments)
            p_ref[:, lo:hi] = _mm_nt(xb, wt_ref[lo:hi, :])

        w = _branch_weights(layer, nat, pw_ref, ca_ref, dw_ref, scr, i == 0)
        r = _branch_forward(p_ref, ph_ref, i == 0, i * tile, km_ref, vm_ref, w, ext_a, ext_c, ext_d, res_c, res_d,
                            tile, None, produce)
        ph_ref[...] = p_ref[tile - HALO:tile, :]
        cd_ref[:, 0:D_G], cd_ref[:, D_G:2 * D_G] = r["kept"]
        h = (jnp.concatenate(r["branch_out"], axis=1) * (r["gate"] * r["sig_gate"])).astype(h_ref.dtype)
        h_ref[...] = h
        z = ALPHA * xt + _mm(h, wo_ref[...])
        z_ref[...] = z
        if want_xn:
            xn_ref[...] = _ln_fwd(z, _row_view(g_ref, layer)[...], _row_view(b_ref, layer)[...])[0]

    row = lambda i: (i, 0)
    consts = [wt_in, mem, w_kv] + nat_arrays + [pw, ca8, dw8, w_out, repl["ln_g"], repl["ln_b"]]
    act = jax.ShapeDtypeStruct((s, D_MODEL), F32)
    act_spec = pl.BlockSpec((tile, D_MODEL), row)
    kv = jax.ShapeDtypeStruct((N_SUB, MEM_LEN, D_G), MM_DTYPE)
    kv_spec = pl.BlockSpec(kv.shape, lambda i: (0, 0, 0))
    res = pl.pallas_call(
        body, grid=(s // tile,),
        in_specs=[act_spec] + [_full_spec(a) for a in consts] + _tie_specs(ties),
        out_specs=(pl.BlockSpec((tile, D_IN), row), pl.BlockSpec((tile, D_MIX), row), act_spec,
                   pl.BlockSpec((tile, 2 * D_G), row), kv_spec, kv_spec) + ((act_spec,) if want_xn else ()),
        out_shape=(jax.ShapeDtypeStruct((s, D_IN), F32), jax.ShapeDtypeStruct((s, D_MIX), MM_DTYPE), act,
                   jax.ShapeDtypeStruct((s, 2 * D_G), F32), kv, kv) + ((act,) if want_xn else ()),
        scratch_shapes=[pltpu.VMEM((HALO, D_IN), F32)] + [pltpu.VMEM((HALO + tile, D_G), F32)] * 3
        + [_residue_scratch(tile)] * 2 + [pltpu.VMEM(shape, F32) for _, shape in _BRANCH_W_SCRATCH],
        name="layer_fwd", compiler_params=_params("arbitrary"),
    )(x, *consts, *ties)
    return tuple(res) if want_xn else tuple(res) + (None,)


_BRANCH_GRADS = (("g256", (8, D_G)), ("sg_w", (N_SUB, CHUNK, CHUNK)), ("sg_b", (8, CHUNK)),
                 ("pool_w", (N_SUB, HEAD_DIM, HEAD_DIM)), ("conv_a_w", (N_DEV, CONV_A, CONV_CH)),
                 ("cc_dw_w", (N_DEV, CONV_D, CONV_CH)), ("cc_pw_w", (D_G, D_G)))
_BRANCH_ACC = (("conv_a", (CONV_A, D_G)), ("cc_dw_w", (CONV_D, D_G)), ("pool_wbd", (D_G, D_G)),
               ("sg_bias", (CHUNK, D_G)), ("sg_w", (N_SUB, CHUNK, CHUNK)),
               ("dk", (N_SUB, MEM_LEN, D_G)), ("dv", (N_SUB, MEM_LEN, D_G)))
_BRANCH_GRADS_NARROW = ("sg_w", "pool_w")


def _layer_bwd_fused(up, target, z, h, proj, kept, km, vm, mem, layer, repl, w_out, pw, ca8, dw8, ties=(),
                     tile=BWD_TILE):
    s = proj.shape[0]
    nt = s // tile
    hb = tile // HALO
    nat_arrays = [repl[n] for n in _BRANCH_REPL]
    n_nat, n_grads, n_acc, n_ties = len(nat_arrays), len(_BRANCH_GRADS), len(_BRANCH_ACC), len(ties)
    row_of = {n: k for k, n in enumerate(G256_ROWS)}
    from_loss = target is not None

    def body(o_ref, z_ref, h_ref, p_ref, ph_ref, cd_ref, km_ref, vm_ref, *rest):
        nat = dict(zip(_BRANCH_REPL, rest[:n_nat]))
        pw_ref, ca_ref, dw_ref, lng_ref, lnb_ref, wo_ref, mem_ref = rest[n_nat:n_nat + 7]
        rest = rest[n_nat + 7 + n_ties:]
        dz_ref, dp_ref, gw_ref, slab_ref, gkv_ref = rest[:5]
        g = dict(zip([n for n, _ in _BRANCH_GRADS], rest[5:5 + n_grads]))
        rest = rest[5 + n_grads:]
        ext_a, rev_a, rev_c, rev_d, res_rc, res_rd, gw_acc, lacc = rest[:8]
        acc = dict(zip([n for n, _ in _BRANCH_ACC], rest[8:8 + n_acc]))
        scr = dict(zip([n for n, _ in _BRANCH_W_SCRATCH], rest[8 + n_acc:]))
        i = pl.program_id(0)
        t = nt - 1 - i

        @pl.when(i == 0)
        def _():
            for ref in list(g.values()) + list(acc.values()) + [rev_a, rev_c, rev_d, gw_acc, slab_ref, lacc]:
                ref[...] = jnp.zeros_like(ref)

        g_ln = _row_view(lng_ref, layer)[...]
        xn, xhat, rstd = _ln_fwd(z_ref[...], g_ln, _row_view(lnb_ref, layer)[...])
        if from_loss:
            err = xn - o_ref[...]
            lacc[...] += _rowsum(err * err)
            dxn = err * (1.0 / D_MODEL)
        else:
            dxn = o_ref[...]
        slab_ref[0:1, :] += _rowsum(dxn * xhat)
        slab_ref[1:2, :] += _rowsum(dxn)
        dz = _ln_bwd(dxn, xhat, rstd, g_ln)
        dz_ref[...] = dz
        dzb = dz.astype(MM_DTYPE)

        w = _branch_weights(layer, nat, pw_ref, ca_ref, dw_ref, scr, i == 0)
        r = _branch_forward(p_ref, ph_ref, t == 0, t * tile, km_ref, vm_ref, w, ext_a, None, None, None, None,
                            tile, cd_ref)

        def put(k, val, width=D_G):
            dp_ref[:, k:k + width] = val.astype(dp_ref.dtype)

        def add_row(name, val):
            k = row_of[name]
            g["g256"][k:k + 1, :] += val

        def push_rev(rev, val):
            head = rev[0:HALO]
            rev[tile:tile + HALO] = head
            rev[0:tile] = val

        gate, sig_gate = r["gate"], r["sig_gate"]

        def branch_grad(group):
            cols = slice(group * D_G, (group + 1) * D_G)
            dh_g = _mm_nt(dzb, wo_ref[cols, :])
            gate_g, sig_g = gate[:, cols], sig_gate[:, cols]
            put(C_GATE + group * D_G, dh_g * r["branch_out"][group] * (sig_g * (1.0 + gate_g * (1.0 - sig_g))))
            return dh_g * (gate_g * sig_g)

        dya = branch_grad(0)
        dyc = branch_grad(2)

        put(C_BA * D_G, dya * r["conv_a"])
        dconv_a = dya * r["ba"]
        push_rev(rev_a, dconv_a)
        dga = jnp.zeros((tile, D_G), F32)
        for k in range(CONV_A):
            ahead = rev_a[pl.ds(CONV_A - 1 - k, tile), :]
            dga = dga + w["conv_a"][k:k + 1, :] * ahead
            acc["conv_a"][k:k + 1, :] += _rowsum(r["g_a"] * ahead)
        put(C_CA * D_G, dga * r["xa"])
        put(C_XA * D_G, dga * r["ca"])

        dyd = branch_grad(3)
        add_row("pool_scale", _rowsum(dyc * r["pool_mm"]))
        dmm = dyc * w["pool_scale"][...]
        acc["pool_wbd"][...] += _mm_tn(r["ypre"], dmm)
        dypre = _mm_nt(dmm, w["pool_wbd"][...])
        dws = dypre * r["inv_cnt"]
        push_rev(rev_c, dws)
        _make_residues(rev_c, res_rc)
        run = dws
        sums = {}
        for k in range(1, POOL_WINDOWS[-1]):
            run = run + _rows_at(rev_c, res_rc, k, tile)
            if k + 1 in POOL_WINDOWS:
                sums[k + 1] = run
        put(C_XC * D_G, _pool_select(r["lane_grp"], sums[2], sums[4], sums[8], sums[16]) - dypre)

        dyb = branch_grad(1)
        gw_acc[...] += _mm_tn(h_ref[...], dzb)
        g["cc_pw_w"][...] += _mm_tn(r["act_d"], dyd)
        dact = _mm_nt(dyd, w["cc_pw_w"][...])
        sig_ln, ln_d = r["sig_ln"], r["ln_d"]
        dln = dact * (sig_ln * (1.0 + ln_d * (1.0 - sig_ln)))
        add_row("cc_ln_g", _rowsum(dln * r["xhat_d"]))
        add_row("cc_ln_b", _rowsum(dln))
        dconv_d = _ln_bwd(dln, r["xhat_d"], r["rstd_d"], w["cc_ln_g"][...])
        add_row("cc_dw_b", _rowsum(dconv_d))
        push_rev(rev_d, dconv_d)
        _make_residues(rev_d, res_rd)
        dhd = jnp.zeros((tile, D_G), F32)
        for j in range(CONV_D):
            ahead = _rows_at(rev_d, res_rd, CONV_D - 1 - j, tile)
            dhd = dhd + w["cc_dw_w"][j:j + 1, :] * ahead
            acc["cc_dw_w"][j:j + 1, :] += _rowsum(r["hd"] * ahead)
        sig_dg = r["sig_dg"]
        put(C_DA * D_G, dhd * sig_dg)
        put(C_DG * D_G, dhd * r["da"] * sig_dg * (1.0 - sig_dg))

        dye = branch_grad(4)
        dug = dyb * r["mixed"]
        dmixed = dyb * r["ug"]
        wm, lo, vn = r["wm"], r["lo"], r["vn"]
        dvn_chunks = []
        for c in range(tile // CHUNK):
            rows = slice(c * CHUNK, (c + 1) * CHUNK)
            acc["sg_bias"][...] += dmixed[rows, :]
            halves = []
            for hf in range(2):
                cols = slice(hf * 128, (hf + 1) * 128)
                dm = dmixed[rows, cols]
                dm_a, dm_b = jnp.where(lo, dm, 0.0), jnp.where(lo, 0.0, dm)
                vh = vn[rows, cols]
                acc["sg_w"][2 * hf] += _mm_nt(dm_a, vh)
                acc["sg_w"][2 * hf + 1] += _mm_nt(dm_b, vh)
                halves.append(_mm_tn(wm[2 * hf], dm_a) + _mm_tn(wm[2 * hf + 1], dm_b))
            dvn_chunks.append(jnp.concatenate(halves, axis=1))
        dvn = jnp.concatenate(dvn_chunks, axis=0)
        add_row("sg_ln_g", _rowsum(dvn * r["xhat_v"]))
        add_row("sg_ln_b", _rowsum(dvn))
        dvg = _ln_bwd(dvn, r["xhat_v"], r["rstd_v"], w["sg_ln_g"][...])
        put(C_V * D_G, dvg * _dgelu(r["v"], r["th_v"]))
        put(C_U * D_G, dug * _dgelu(r["u"], r["th_u"]))

        q = r["q"]
        dq = jnp.zeros((tile, D_G), F32)
        for h in range(N_SUB):
            p = r["probs"][h]
            dp = _mm_nt(dye, vm_ref[h])
            acc["dv"][h] += _mm_tn(p, dye)
            ds = p * (dp - jnp.sum(dp * p, axis=-1, keepdims=True)) * ATT_SCALE
            dq = dq + _mm(ds, km_ref[h])
            acc["dk"][h] += _mm_tn(ds, q)
        put(C_Q * D_G, dq)

        @pl.when(i == nt - 1)
        def _():
            for h in range(N_SUB):
                g["sg_w"][h] = jnp.where(r["tri"], acc["sg_w"][h], 0.0).astype(g["sg_w"].dtype)
            lane_head = _head_of_lane((CHUNK, D_G))
            col_of = lax.broadcasted_iota(jnp.int32, (CHUNK, 8), 1)
            ba = acc["sg_bias"][...]
            sgb_t = jnp.zeros((CHUNK, 8), F32)
            for h in range(N_SUB):
                col = jnp.sum(jnp.where(lane_head == h, ba, 0.0), axis=-1, keepdims=True)
                sgb_t = jnp.where(col_of == h, col, sgb_t)
            g["sg_b"][...] = sgb_t.T
            wbd = acc["pool_wbd"][...]
            for gi in range(N_SUB):
                sl = slice(gi * HEAD_DIM, (gi + 1) * HEAD_DIM)
                g["pool_w"][gi] = wbd[sl, sl].astype(g["pool_w"].dtype)
            ca, dw = acc["conv_a"][...], acc["cc_dw_w"][...]
            for p in range(N_DEV):
                g["conv_a_w"][p] = ca[:, p * CONV_CH:(p + 1) * CONV_CH]
                g["cc_dw_w"][p] = dw[:, p * CONV_CH:(p + 1) * CONV_CH]
            gw_ref[...] = gw_acc[...].astype(gw_ref.dtype)
            grp = _head_of_lane((MEM_LEN, D_G))
            dk_sum = jnp.zeros((MEM_LEN, D_G), F32)
            dv_sum = jnp.zeros((MEM_LEN, D_G), F32)
            for h in range(N_SUB):
                dk_sum = dk_sum + jnp.where(grp == h, acc["dk"][h], 0.0)
                dv_sum = dv_sum + jnp.where(grp == h, acc["dv"][h], 0.0)
            gkv_ref[...] = _mm_tn(mem_ref[...], jnp.concatenate([dk_sum, dv_sum], axis=1)).astype(gkv_ref.dtype)
            if from_loss:
                total = jnp.sum(lacc[...], axis=-1, keepdims=True) * (0.5 / D_MODEL)
                slab_ref[LOSS_ROW:LOSS_ROW + 1, :] = jnp.broadcast_to(total, (1, D_MODEL))

    rev = lambda i: (nt - 1 - i, 0)
    fixed = lambda i: (0, 0)
    act_spec = pl.BlockSpec((tile, D_MODEL), rev)
    grad_specs = tuple(pl.BlockSpec(shape, lambda i, _nd=len(shape): (0,) * _nd) for _, shape in _BRANCH_GRADS)
    grad_shapes = tuple(jax.ShapeDtypeStruct(shape, GRAD_DTYPE if n in _BRANCH_GRADS_NARROW else F32)
                        for n, shape in _BRANCH_GRADS)
    consts = [km, vm] + nat_arrays + [pw, ca8, dw8, repl["ln_g"], repl["ln_b"], w_out, mem]
    outs = pl.pallas_call(
        body, grid=(nt,),
        in_specs=[act_spec, act_spec, pl.BlockSpec((tile, D_MIX), rev), pl.BlockSpec((tile, D_IN), rev),
                  pl.BlockSpec((HALO, D_IN), lambda i: (jnp.maximum((nt - 1 - i) * hb - 1, 0), 0)),
                  pl.BlockSpec((tile, 2 * D_G), rev)]
        + [_full_spec(a) for a in consts] + _tie_specs(ties),
        out_specs=(act_spec, pl.BlockSpec((tile, D_IN), rev), pl.BlockSpec((D_MIX, D_MODEL), fixed),
                   pl.BlockSpec((8, D_MODEL), fixed), pl.BlockSpec((D_MODEL, 2 * D_G), fixed)) + grad_specs,
        out_shape=(jax.ShapeDtypeStruct((s, D_MODEL), F32), jax.ShapeDtypeStruct((s, D_IN), MM_DTYPE),
                   jax.ShapeDtypeStruct((D_MIX, D_MODEL), GRAD_DTYPE), jax.ShapeDtypeStruct((8, D_MODEL), F32),
                   jax.ShapeDtypeStruct((D_MODEL, 2 * D_G), GRAD_DTYPE)) + grad_shapes,
        scratch_shapes=[pltpu.VMEM((HALO + tile, D_G), F32)] * 4 + [_residue_scratch(tile)] * 2
        + [pltpu.VMEM((D_MIX, D_MODEL), F32), pltpu.VMEM((1, D_MODEL), F32)]
        + [pltpu.VMEM(shape, F32) for _, shape in _BRANCH_ACC + _BRANCH_W_SCRATCH],
        name="layer_bwd_loss" if from_loss else "layer_bwd", compiler_params=_params("arbitrary"),
    )(target if from_loss else up, z, h, proj, proj, kept, *consts, *ties)
    return outs[0], outs[1], outs[2], outs[3], outs[4], dict(zip([n for n, _ in _BRANCH_GRADS], outs[5:]))


def _dx_matmul(dproj, wt_in, dz, ties=(), tm=512):
    s = dproj.shape[0]

    def body(dp_ref, w_ref, dz_ref, *rest):
        o_ref = rest[len(ties)]
        o_ref[...] = _mm(dp_ref[...], w_ref[...]) + ALPHA * dz_ref[...]

    row = lambda i: (i, 0)
    return pl.pallas_call(
        body, grid=(s // tm,),
        in_specs=[pl.BlockSpec((tm, D_IN), row), _full_spec(wt_in), pl.BlockSpec((tm, D_MODEL), row)]
        + _tie_specs(ties),
        out_specs=pl.BlockSpec((tm, D_MODEL), row),
        out_shape=jax.ShapeDtypeStruct((s, D_MODEL), F32), name="dx_mm", compiler_params=_params("arbitrary"),
    )(dproj, wt_in, dz, *ties)


def _dw_in_matmul(x, dproj, ties=(), tk=512):
    s = x.shape[0]
    nk = s // tk
    blk = 2 * W_IN_COLS

    def body(x_ref, dp_ref, *rest):
        o_ref, acc = rest[len(ties):]
        k = pl.program_id(0)

        @pl.when(k == 0)
        def _():
            acc[...] = jnp.zeros_like(acc)

        xb = x_ref[...].astype(MM_DTYPE)
        for j in range(D_IN // blk):
            acc[j * blk:(j + 1) * blk, :] += _mm_tn(dp_ref[:, j * blk:(j + 1) * blk], xb)

        @pl.when(k == nk - 1)
        def _():
            o_ref[...] = acc[...].astype(o_ref.dtype)

    return pl.pallas_call(
        body, grid=(nk,),
        in_specs=[pl.BlockSpec((tk, D_MODEL), lambda k: (k, 0)), pl.BlockSpec((tk, D_IN), lambda k: (k, 0))]
        + _tie_specs(ties),
        out_specs=pl.BlockSpec((D_IN, D_MODEL), lambda k: (0, 0)),
        out_shape=jax.ShapeDtypeStruct((D_IN, D_MODEL), GRAD_DTYPE),
        scratch_shapes=[pltpu.VMEM((D_IN, D_MODEL), F32)], name="dw_in_mm", compiler_params=_params("arbitrary"),
    )(x, dproj, *ties)


def _in_bwd(x, dproj, wt_in, dz, ties=(), tm=512):
    s = x.shape[0]
    n_steps = s // tm

    assert wt_in.dtype == GRAD_DTYPE
    blk = 2 * W_IN_COLS

    def body(x_ref, dp_ref, w_hbm, dz_ref, *rest):
        o_ref, gw_hbm, w_vmem, acc, sem = rest[len(ties):]
        i = pl.program_id(0)

        @pl.when(i == 0)
        def _():
            fetch = pltpu.make_async_copy(w_hbm, w_vmem, sem)
            fetch.start()
            acc[...] = jnp.zeros_like(acc)
            fetch.wait()

        o_ref[...] = _mm(dp_ref[...], w_vmem[...]) + ALPHA * dz_ref[...]
        xb = x_ref[...].astype(MM_DTYPE)
        for j in range(D_IN // blk):
            acc[j * blk:(j + 1) * blk, :] += _mm_tn(dp_ref[:, j * blk:(j + 1) * blk], xb)

        @pl.when(i == n_steps - 1)
        def _():
            w_vmem[...] = acc[...].astype(w_vmem.dtype)
            emit = pltpu.make_async_copy(w_vmem, gw_hbm, sem)
            emit.start()
            emit.wait()

    row = lambda i: (i, 0)
    any_spec = pl.BlockSpec(memory_space=pl.ANY)
    return pl.pallas_call(
        body, grid=(n_steps,),
        in_specs=[pl.BlockSpec((tm, D_MODEL), row), pl.BlockSpec((tm, D_IN), row), any_spec,
                  pl.BlockSpec((tm, D_MODEL), row)] + _tie_specs(ties),
        out_specs=(pl.BlockSpec((tm, D_MODEL), row), any_spec),
        out_shape=(jax.ShapeDtypeStruct((s, D_MODEL), F32), jax.ShapeDtypeStruct((D_IN, D_MODEL), GRAD_DTYPE)),
        scratch_shapes=[pltpu.VMEM((D_IN, D_MODEL), wt_in.dtype), pltpu.VMEM((D_IN, D_MODEL), F32),
                        pltpu.SemaphoreType.DMA],
        name="in_bwd", compiler_params=_params("arbitrary"),
    )(x, dproj, wt_in, dz, *ties)


def _layer_fwd(x, mem, layer, repl, gw, ties=()):
    proj, h, z, kept, km, vm, xn = _layer_fwd_fused(x, gw["wt_in"], mem, gw["w_kv"], layer, repl, gw["pw"],
                                                    gw["ca8"], gw["dw8"], gw["w_out"], layer < DEPTH - 1, ties)
    return xn, (x, proj, h, z, km, vm, kept)


def _layer_bwd_a(up, target, mem, layer, repl, gw, saved, ties=()):
    x_in, proj, h, z, km, vm, kept = saved
    dz, dproj, g_w_out, g1024, g_w_kv, bg = _layer_bwd_fused(up, target, z, h, proj, kept, km, vm, mem, layer, repl,
                                                             gw["w_out"], gw["pw"], gw["ca8"], gw["dw8"], ties)
    grads = {n: bg[n] for n in ("g256", "sg_w", "sg_b", "pool_w", "conv_a_w", "cc_dw_w")}
    grads.update(w_out=g_w_out.reshape(N_DEV, D_MIX // N_DEV, D_MODEL), g1024=g1024,
                 w_kv=g_w_kv.reshape(N_DEV, D_MODEL // N_DEV, 2 * D_G),
                 cc_pw_w=bg["cc_pw_w"].reshape(N_DEV, CONV_CH, D_G))
    return dz, dproj, grads


def _landing_shapes(items):
    out = []
    for a, scatter, pick in items:
        shape = a.shape if scatter else (N_DEV,) + (a.shape if pick is None else a.shape[1:])
        out.append(jax.ShapeDtypeStruct(shape, a.dtype))
    return tuple(out)


def _exchange_sems(n):
    return [pltpu.SemaphoreType.DMA(((N_DEV - 1) * n,)), pltpu.SemaphoreType.DMA(((N_DEV - 1) * n,)),
            pltpu.SemaphoreType.DMA((n,))]


def _exchange_copies(modes, ins, outs, send_sems, recv_sems, local_sems):
    n = len(ins)
    x, y, c = lax.axis_index("x"), lax.axis_index("y"), lax.axis_index("c")
    me = 4 * x + 2 * y + c

    def src_of(a, dest):
        scatter, pick = modes[a]
        if scatter:
            return ins[a].at[dest]
        return ins[a] if pick is None else ins[a].at[pick]

    local = [pltpu.make_async_copy(src_of(a, me), outs[a].at[me], local_sems.at[a]) for a in range(n)]
    sends, recvs = [], []
    for k in range(1, N_DEV):
        px = 1 - x if k & 4 else x
        py = 1 - y if k & 2 else y
        pc = 1 - c if k & 1 else c
        peer = 4 * px + 2 * py + pc
        for a in range(n):
            sems = dict(send_sem=send_sems.at[(k - 1) * n + a], recv_sem=recv_sems.at[(k - 1) * n + a],
                        device_id=(px, py, pc), device_id_type=pl.DeviceIdType.MESH)
            sends.append(pltpu.make_async_remote_copy(src_ref=src_of(a, peer), dst_ref=outs[a].at[me], **sems))
            recvs.append(pltpu.make_async_remote_copy(src_ref=src_of(a, peer), dst_ref=outs[a].at[peer], **sems))
    return local, sends, recvs


def _gather_two_level(items, name):
    n = len(items)
    assert not any(scatter for _, scatter, _ in items)
    picks = [pick for _, _, pick in items]

    def body(*refs):
        ins, outs = refs[:n], refs[n:2 * n]
        send_sems, recv_sems, local_sems = refs[2 * n:]
        x, y, c = lax.axis_index("x"), lax.axis_index("y"), lax.axis_index("c")
        sib = 1 - c
        chips = [(1 - x, y), (x, 1 - y), (1 - x, 1 - y)]

        def slot(a, px, py, pc):
            return outs[a].at[4 * px + 2 * py + pc]

        def copy(k, a, src, block, to):
            return pltpu.make_async_remote_copy(
                src_ref=src, dst_ref=slot(a, *block), send_sem=send_sems.at[k * n + a],
                recv_sem=recv_sems.at[k * n + a], device_id=to, device_id_type=pl.DeviceIdType.MESH)

        own = [ins[a] if picks[a] is None else ins[a].at[picks[a]] for a in range(n)]
        local = [pltpu.make_async_copy(own[a], slot(a, x, y, c), local_sems.at[a]) for a in range(n)]
        first = [copy(0, a, own[a], (x, y, c), (x, y, sib)) for a in range(n)]
        first += [copy(1 + j, a, own[a], (x, y, c), (*chip, c)) for j, chip in enumerate(chips[:2]) for a in range(n)]
        for cp in local + first:
            cp.start()

        def pass_on(j, a):
            chip = chips[j]
            copy(1 + j, a, own[a], (*chip, c), (x, y, c)).wait_recv()
            fwd = copy(4 + j, a, slot(a, *chip, c), (*chip, c), (x, y, sib))
            fwd.start()
            return fwd

        passed = [pass_on(j, a) for j in range(2) for a in range(n)]
        south = c == 0
        via = tuple(jnp.where(south, p, q) for p, q in zip(chips[0], chips[1]))
        blk = tuple(jnp.where(south, q, p) for p, q in zip(chips[0], chips[1]))
        relayed = [copy(3, a, slot(a, *blk, c), (*blk, c), (*via, c)) for a in range(n)]
        for cp in relayed:
            cp.start()
        passed += [pass_on(2, a) for a in range(n)]
        first += relayed
        for a in range(n):
            copy(0, a, own[a], (x, y, sib), (x, y, c)).wait_recv()
        for j, chip in enumerate(chips):
            for a in range(n):
                copy(4 + j, a, own[a], (*chip, sib), (x, y, c)).wait_recv()
        for cp in first + passed:
            cp.wait_send()
        for cp in local:
            cp.wait()

    any_spec = pl.BlockSpec(memory_space=pl.ANY)
    return pl.pallas_call(
        body, in_specs=[any_spec] * n, out_specs=(any_spec,) * n, out_shape=_landing_shapes(items),
        scratch_shapes=[pltpu.SemaphoreType.DMA((7 * n,)), pltpu.SemaphoreType.DMA((7 * n,)),
                        pltpu.SemaphoreType.DMA((n,))],
        name=name,
    )(*[a for a, _, _ in items])


_HBM_SPEC = pl.BlockSpec(memory_space=pltpu.HBM)
_SEM_SPEC = pl.BlockSpec(memory_space=pltpu.SEMAPHORE)
_SPLIT_PARAMS = pltpu.CompilerParams(has_side_effects=pltpu.SideEffectType.DATAFLOW_SIDE_EFFECTING)


def _split_start(srcs, lands, plan, sem_shapes, name):
    n_src, n_land = len(srcs), len(lands)
    n_buf = n_src + n_land
    bufs = [pltpu.with_memory_space_constraint(a, pltpu.HBM) for a in list(srcs) + list(lands)]

    def body(*refs):
        local, sends, _ = plan(refs[:n_src], refs[n_src:n_buf], *refs[n_buf:n_buf + 3])
        for cp in local + sends:
            cp.start()
        token = refs[-1]
        token[...] = jnp.zeros_like(token)

    res = pl.pallas_call(
        body, name=name, in_specs=[_HBM_SPEC] * n_buf,
        out_shape=tuple(sem_shapes) + tuple(pltpu.HBM(a.shape, a.dtype) for a in bufs)
        + (jax.ShapeDtypeStruct((8, 128), F32),),
        out_specs=(_SEM_SPEC,) * 3 + (_HBM_SPEC,) * n_buf + (pl.BlockSpec(memory_space=pltpu.VMEM),),
        input_output_aliases={i: 3 + i for i in range(n_buf)}, compiler_params=_SPLIT_PARAMS,
    )(*bufs)
    return dict(sems=res[:3], srcs=res[3:3 + n_src], lands=res[3 + n_src:3 + n_buf], token=res[-1], plan=plan)


def _split_wait(ticket, after, name, with_srcs=False):
    n_src, n_land = len(ticket["srcs"]), len(ticket["lands"])
    n_buf = n_src + n_land
    plan = ticket["plan"]

    def body(*refs):
        local, sends, recvs = plan(refs[:n_src], refs[n_src:n_buf], *refs[n_buf:n_buf + 3])
        for cp in recvs:
            cp.wait_recv()
        for cp in sends:
            cp.wait_send()
        for cp in local:
            cp.wait()

    bufs = list(ticket["srcs"]) + list(ticket["lands"])
    res = pl.pallas_call(
        body, name=name,
        in_specs=[_HBM_SPEC] * n_buf + [_SEM_SPEC] * 3 + [pl.BlockSpec(memory_space=pl.ANY)] * len(after),
        out_shape=tuple(pltpu.HBM(a.shape, a.dtype) for a in bufs), out_specs=(_HBM_SPEC,) * n_buf,
        input_output_aliases={i: i for i in range(n_buf)}, compiler_params=_SPLIT_PARAMS,
    )(*bufs, *ticket["sems"], *after)
    return res if with_srcs else res[n_src:]


def _empty_landings(items):
    return [lax.empty(s.shape, s.dtype) for s in _landing_shapes(items)]


def _exchange_start(items, name):
    modes = [(scatter, pick) for _, scatter, pick in items]
    plan = lambda ins, outs, *sems: _exchange_copies(modes, ins, outs, *sems)
    return _split_start([a for a, _, _ in items], _empty_landings(items), plan, _exchange_sems(len(items)), name)


def _two_level_plans(picks):
    n = len(picks)

    def place():
        x, y, c = lax.axis_index("x"), lax.axis_index("y"), lax.axis_index("c")
        return x, y, c, 1 - c, [(1 - x, y), (x, 1 - y), (1 - x, 1 - y)]

    def copy(outs, send_sems, recv_sems, k, a, src, block, to):
        px, py, pc = block
        return pltpu.make_async_remote_copy(
            src_ref=src, dst_ref=outs[a].at[4 * px + 2 * py + pc], send_sem=send_sems.at[k * n + a],
            recv_sem=recv_sems.at[k * n + a], device_id=to, device_id_type=pl.DeviceIdType.MESH)

    def between_chips(ins, outs, send_sems, recv_sems, local_sems):
        x, y, c, sib, chips = place()
        own = [ins[a] if picks[a] is None else ins[a].at[picks[a]] for a in range(n)]
        mk = lambda *args: copy(outs, send_sems, recv_sems, *args)
        local = [pltpu.make_async_copy(own[a], outs[a].at[4 * x + 2 * y + c], local_sems.at[a]) for a in range(n)]
        sends = [mk(0, a, own[a], (x, y, c), (x, y, sib)) for a in range(n)]
        sends += [mk(1 + j, a, own[a], (x, y, c), (*chip, c)) for j, chip in enumerate(chips) for a in range(n)]
        recvs = [mk(0, a, own[a], (x, y, sib), (x, y, c)) for a in range(n)]
        recvs += [mk(1 + j, a, own[a], (*chip, c), (x, y, c)) for j, chip in enumerate(chips) for a in range(n)]
        return local, sends, recvs

    def within_chip(ins, outs, send_sems, recv_sems, local_sems):
        x, y, c, sib, chips = place()
        mk = lambda *args: copy(outs, send_sems, recv_sems, *args)
        slot = lambda a, px, py, pc: outs[a].at[4 * px + 2 * py + pc]
        sends = [mk(j, a, slot(a, *chip, c), (*chip, c), (x, y, sib)) for j, chip in enumerate(chips)
                 for a in range(n)]
        recvs = [mk(j, a, slot(a, *chip, c), (*chip, sib), (x, y, c)) for j, chip in enumerate(chips)
                 for a in range(n)]
        return [], sends, recvs

    sems = lambda k: [pltpu.SemaphoreType.DMA((k * n,)), pltpu.SemaphoreType.DMA((k * n,)),
                      pltpu.SemaphoreType.DMA((n,))]
    return between_chips, sems(4), within_chip, sems(3)


N_CHIPS = N_DEV // 2


def _pair_plans():
    def mesh_pos():
        return lax.axis_index("x"), lax.axis_index("y"), lax.axis_index("c")

    def to_sibling(ins, outs, send_sems, recv_sems, local_sems):
        x, y, c = mesh_pos()
        sib = 1 - c
        sends = [pltpu.make_async_remote_copy(
            src_ref=ins[0].at[2 * j + sib], dst_ref=outs[0].at[j], send_sem=send_sems.at[j],
            recv_sem=recv_sems.at[j], device_id=(x, y, sib), device_id_type=pl.DeviceIdType.MESH)
            for j in range(N_CHIPS)]
        return [], sends, sends

    def between_chips(ins, outs, send_sems, recv_sems, local_sems):
        x, y, c = mesh_pos()
        chip = 2 * x + y
        local = [pltpu.make_async_copy(ins[0].at[chip], outs[0].at[chip], local_sems.at[0])]
        sends, recvs = [], []
        for k in range(1, N_CHIPS):
            px = 1 - x if k & 2 else x
            py = 1 - y if k & 1 else y
            sems = dict(send_sem=send_sems.at[k - 1], recv_sem=recv_sems.at[k - 1], device_id=(px, py, c),
                        device_id_type=pl.DeviceIdType.MESH)
            sends.append(pltpu.make_async_remote_copy(src_ref=ins[0].at[2 * px + py], dst_ref=outs[0].at[chip], **sems))
            recvs.append(pltpu.make_async_remote_copy(src_ref=ins[0].at[2 * px + py], dst_ref=outs[0].at[2 * px + py],
                                                      **sems))
        return local, sends, recvs

    dma = lambda k: pltpu.SemaphoreType.DMA((k,))
    return to_sibling, [dma(N_CHIPS), dma(N_CHIPS), dma(1)], between_chips, [dma(N_CHIPS - 1), dma(N_CHIPS - 1), dma(1)]


def _pair_add(slabs, from_sibling, tr=W_IN_COLS):
    _, rows, cols = slabs.shape
    core = lax.axis_index("c").astype(jnp.int32).reshape(1)

    def body(core_ref, a_ref, b_ref, o_ref):
        o_ref[...] = (a_ref[...].astype(F32) + b_ref[...].astype(F32)).astype(o_ref.dtype)

    return pl.pallas_call(
        body,
        grid_spec=pltpu.PrefetchScalarGridSpec(
            num_scalar_prefetch=1, grid=(N_CHIPS, rows // tr),
            in_specs=[pl.BlockSpec((None, tr, cols), lambda j, i, core_ref: (2 * j + core_ref[0], i, 0)),
                      pl.BlockSpec((None, tr, cols), lambda j, i, core_ref: (j, i, 0))],
            out_specs=pl.BlockSpec((None, tr, cols), lambda j, i, core_ref: (j, i, 0))),
        out_shape=jax.ShapeDtypeStruct((N_CHIPS, rows, cols), slabs.dtype), name="pair_add",
        compiler_params=_params("arbitrary", "arbitrary"),
    )(core, slabs, from_sibling)


def _adam_math(g, w, m, v):
    m_new = ADAM_B1 * m + (1.0 - ADAM_B1) * g
    v_new = ADAM_B2 * v + (1.0 - ADAM_B2) * (g * g)
    m_hat = m_new / (1.0 - ADAM_B1 ** ADAM_STEP)
    v_hat = v_new / (1.0 - ADAM_B2 ** ADAM_STEP)
    return -ADAM_LR * (m_hat / (jnp.sqrt(v_hat) + ADAM_EPS) + ADAM_WD * w), m_new, v_new


def _adamw_big(parts, w, m, v, layer, prev, name, tr):
    depth, rows, cols = w.shape
    n_parts = parts.shape[0]

    def body(p_ref, w_ref, m_ref, v_ref, *rest):
        g_out, d_out, m_out, v_out = rest[len(prev):]
        g = p_ref[0].astype(F32)
        for q in range(1, n_parts):
            g = g + p_ref[q].astype(F32)
        d, m_new, v_new = _adam_math(g, w_ref[...], m_ref[...], v_ref[...])
        g_out[...] = g
        d_out[...] = d
        m_out[...] = m_new
        v_out[...] = v_new

    blk = pl.BlockSpec((None, tr, cols), lambda i: (layer, i, 0))
    shp = jax.ShapeDtypeStruct((depth, rows, cols), F32)
    return pl.pallas_call(
        body, grid=(rows // tr,),
        in_specs=[pl.BlockSpec((n_parts, tr, cols), lambda i: (0, i, 0)), blk, blk, blk]
        + [pl.BlockSpec(memory_space=pl.ANY)] * len(prev),
        out_specs=(blk,) * 4, out_shape=(shp,) * 4,
        input_output_aliases={4 + j: j for j in range(len(prev))},
        name=name, compiler_params=_params("arbitrary"),
    )(parts, w, m, v, *prev)


_SMALL_TENSORS = (("conv_a_w", "conv_a_w", None), ("cc_dw_w", "cc_dw_w", None), ("cc_pw_w", "cc_pw_w", None),
                  ("sg_w", "sg_w", None), ("pool_w", "pool_w", None), ("sg_b", "sg_b", None)) \
    + tuple((n, "g256", k) for k, n in enumerate(G256_ROWS)) + tuple((n, "g1024", k) for k, n in enumerate(G1024_ROWS))
_SMALL_LANDINGS = ("conv_a_w", "cc_dw_w", "cc_pw_w", "sg_w", "pool_w", "sg_b", "g256", "g1024")
_TAPS_FIRST = ("conv_a_w", "cc_dw_w")


def _adamw_small(landings, wts, mom, var):
    names = [n for n, _, _ in _SMALL_TENSORS]
    n_land = DEPTH * len(_SMALL_LANDINGS)
    n_t = len(names)

    def body(*refs):
        land = [dict(zip(_SMALL_LANDINGS, refs[l * len(_SMALL_LANDINGS):(l + 1) * len(_SMALL_LANDINGS)]))
                for l in range(DEPTH)]
        w_refs = dict(zip(names, refs[n_land:n_land + n_t]))
        m_refs = dict(zip(names, refs[n_land + n_t:n_land + 2 * n_t]))
        v_refs = dict(zip(names, refs[n_land + 2 * n_t:n_land + 3 * n_t]))
        outs = refs[n_land + 3 * n_t:]
        out_refs = {n: outs[4 * k:4 * k + 4] for k, n in enumerate(names)}
        loss_ref = outs[4 * n_t]
        for name, key, row in _SMALL_TENSORS:
            for l in range(DEPTH):
                src = land[l][key]
                if row is not None:
                    part = lambda q: src[q, row:row + 1, :]
                    at = (slice(l, l + 1),)
                elif name == "sg_b":
                    part = lambda q: src[q, 0:N_SUB, :]
                    at = (l,)
                elif name in _TAPS_FIRST:
                    part = lambda q: src[q]
                    at = (slice(None), l)
                else:
                    part = lambda q: src[q]
                    at = (l,)
                g = part(0).astype(F32)
                for q in range(1, N_DEV):
                    g = g + part(q).astype(F32)
                d, m_new, v_new = _adam_math(g, w_refs[name][at], m_refs[name][at], v_refs[name][at])
                for ref, val in zip(out_refs[name], (g, d, m_new, v_new)):
                    ref[at] = val
        src = land[DEPTH - 1]["g1024"]
        loss = src[0, LOSS_ROW:LOSS_ROW + 1, 0:128]
        for q in range(1, N_DEV):
            loss = loss + src[q, LOSS_ROW:LOSS_ROW + 1, 0:128]
        loss_ref[...] = loss

    ins = [landings[l][k] for l in range(DEPTH) for k in _SMALL_LANDINGS] \
        + [src[n] for src in (wts, mom, var) for n in names]
    out_shape = tuple(jax.ShapeDtypeStruct(wts[n].shape, F32) for n in names for _ in range(4)) \
        + (jax.ShapeDtypeStruct((1, 128), F32),)
    res = pl.pallas_call(body, out_shape=out_shape, name="adamw_small", compiler_params=_params())(*ins)
    return {n: res[4 * k:4 * k + 4] for k, n in enumerate(names)}, res[4 * n_t]


_BIG = (("w_in", 224), ("w_out", 80), ("w_kv", 64))
_GRAD_ITEMS_EARLY = ("w_out", "w_kv", "cc_pw_w", "conv_a_w", "cc_dw_w")
_GRAD_ITEMS_REPL = ("g256", "sg_w", "sg_b", "pool_w", "g1024")


def _grad_items(grads):
    items = [(grads[n], True, None) for n in _GRAD_ITEMS_EARLY]
    return items + [(grads[n], False, None) for n in _GRAD_ITEMS_REPL]


def _landed(parts):
    return dict(zip(_GRAD_ITEMS_EARLY + _GRAD_ITEMS_REPL, parts))


def _gathered_weights(wt_in8, w_kv8, w_out8, pw8, ca8, dw8):
    return dict(wt_in=wt_in8.reshape(D_IN, D_MODEL), w_kv=w_kv8.reshape(D_MODEL, 2 * D_G),
                w_out=w_out8.reshape(D_MIX, D_MODEL), pw=pw8.reshape(D_G, D_G), ca8=ca8, dw8=dw8)


def kernel(x, mem, w_in, conv_a_w, sg_ln_g, sg_ln_b, sg_w, sg_b, pool_w, pool_scale, cc_dw_w, cc_dw_b, cc_ln_g, cc_ln_b, cc_pw_w, w_kv, w_out, ln_g, ln_b, loss_target, m_w_in, m_conv_a_w, m_sg_ln_g, m_sg_ln_b, m_sg_w, m_sg_b, m_pool_w, m_pool_scale, m_cc_dw_w, m_cc_dw_b, m_cc_ln_g, m_cc_ln_b, m_cc_pw_w, m_w_kv, m_w_out, m_ln_g, m_ln_b, v_w_in, v_conv_a_w, v_sg_ln_g, v_sg_ln_b, v_sg_w, v_sg_b, v_pool_w, v_pool_scale, v_cc_dw_w, v_cc_dw_b, v_cc_ln_g, v_cc_ln_b, v_cc_pw_w, v_w_kv, v_w_out, v_ln_g, v_ln_b):
    names = ("w_in", "conv_a_w", "sg_ln_g", "sg_ln_b", "sg_w", "sg_b", "pool_w", "pool_scale", "cc_dw_w", "cc_dw_b",
             "cc_ln_g", "cc_ln_b", "cc_pw_w", "w_kv", "w_out", "ln_g", "ln_b")
    wts = dict(zip(names, (w_in, conv_a_w, sg_ln_g, sg_ln_b, sg_w, sg_b, pool_w, pool_scale, cc_dw_w, cc_dw_b,
                           cc_ln_g, cc_ln_b, cc_pw_w, w_kv, w_out, ln_g, ln_b)))
    mom = dict(zip(names, (m_w_in, m_conv_a_w, m_sg_ln_g, m_sg_ln_b, m_sg_w, m_sg_b, m_pool_w, m_pool_scale,
                           m_cc_dw_w, m_cc_dw_b, m_cc_ln_g, m_cc_ln_b, m_cc_pw_w, m_w_kv, m_w_out, m_ln_g, m_ln_b)))
    var = dict(zip(names, (v_w_in, v_conv_a_w, v_sg_ln_g, v_sg_ln_b, v_sg_w, v_sg_b, v_pool_w, v_pool_scale,
                           v_cc_dw_w, v_cc_dw_b, v_cc_ln_g, v_cc_ln_b, v_cc_pw_w, v_w_kv, v_w_out, v_ln_g, v_ln_b)))
    repl = wts
    xs, mems, tgt = x[0], mem[0], loss_target[0]
    turned = {"w_in": (0, 2, 1), "conv_a_w": (1, 0, 2), "cc_dw_w": (1, 0, 2)}
    wts, mom, var = [{n: (jnp.transpose(a, turned[n]) if n in turned else a) for n, a in src.items()}
                     for src in (wts, mom, var)]
    wb = {n: wts[n].astype(MM_DTYPE) for n in ("w_in", "w_kv", "w_out", "cc_pw_w")}

    wt8_0, wkv8_0, wo8_0, pw8_0, ca8, dw8 = _gather_two_level(
        [(wb["w_in"], False, 0), (wb["w_kv"], False, 0), (wb["w_out"], False, 0), (wb["cc_pw_w"], False, 0),
         (wts["conv_a_w"], False, None), (wts["cc_dw_w"], False, None)], "gather_weights_0")
    gw0 = _gathered_weights(wt8_0, wkv8_0, wo8_0, pw8_0, ca8, dw8)
    items_1 = [(wb[n], False, 1) for n in ("w_in", "w_kv", "w_out", "cc_pw_w")]
    between_chips, sems_a, within_chip, sems_b = _two_level_plans([1] * len(items_1))
    chips_1 = _split_start([a for a, _, _ in items_1], _empty_landings(items_1), between_chips, sems_a,
                           "gather_weights_1a_start")
    x1, saved0 = _layer_fwd(xs, mems, 0, repl, gw0, ties=(chips_1["token"],))
    core_1 = _split_start([], _split_wait(chips_1, (x1,), "gather_weights_1a_wait"), within_chip, sems_b,
                          "gather_weights_1b_start")
    gw1 = _gathered_weights(*_split_wait(core_1, (core_1["token"],), "gather_weights_1b_wait"), ca8, dw8)
    _, saved1 = _layer_fwd(x1, mems, 1, repl, gw1)

    dz1, dproj1, g1 = _layer_bwd_a(None, tgt, mems, 1, repl, gw1, saved1)
    shards = lambda g: g.reshape(N_DEV, W_IN_COLS, D_MODEL)
    up, g_wt_in_1 = _in_bwd(saved1[0], dproj1, gw1["wt_in"], dz1)
    grads_1 = _exchange_start([(shards(g_wt_in_1), True, None)] + _grad_items(g1), "exchange_grads_1_start")
    dz0, dproj0, g0 = _layer_bwd_a(up, None, mems, 0, repl, gw0, saved0, (grads_1["token"],))
    early_0 = _exchange_start(_grad_items(g0), "exchange_grads_0a_start")
    g_wt_in_0 = _dw_in_matmul(saved0[0], dproj0, (early_0["token"],))
    to_sibling, sems_s, between_chips, sems_c = _pair_plans()
    half = lambda: [lax.empty((N_CHIPS, W_IN_COLS, D_MODEL), GRAD_DTYPE)]
    pair_0 = _split_start([shards(g_wt_in_0)], half(), to_sibling, sems_s, "exchange_grads_0b_start")
    w_in_1, *rest_1 = _split_wait(grads_1, (pair_0["token"],), "exchange_grads_1_wait")
    landed = [None, dict(_landed(rest_1), w_in=w_in_1)]
    big = {}
    for n, tr in _BIG:
        big[n] = _adamw_big(landed[1][n], wts[n], mom[n], var[n], 1, (), "adamw_" + n + "_1", tr)
    own_slabs, from_sibling = _split_wait(pair_0, tuple(big[n][0] for n, _ in _BIG), "exchange_grads_0b_wait",
                                          with_srcs=True)
    late_0 = _split_start([_pair_add(own_slabs, from_sibling)], half(), between_chips, sems_c,
                          "exchange_grads_0c_start")
    grad_x = _dx_matmul(dproj0, gw0["wt_in"], dz0, (late_0["token"],))
    landed[0] = _landed(_split_wait(early_0, (grad_x,), "exchange_grads_0a_wait"))
    for n, tr in _BIG[1:]:
        big[n] = _adamw_big(landed[0][n], wts[n], mom[n], var[n], 0, big[n], "adamw_" + n + "_0", tr)
    small, loss = _adamw_small(landed, wts, mom, var)
    (landed[0]["w_in"],) = _split_wait(late_0, (loss, big["w_out"][0], big["w_kv"][0]), "exchange_grads_0c_wait")
    big["w_in"] = _adamw_big(landed[0]["w_in"], wts["w_in"], mom["w_in"], var["w_in"], 0, big["w_in"],
                             "adamw_w_in_0", _BIG[0][1])

    res = {**small, **big}
    res = {n: ([jnp.transpose(a, turned[n]) for a in r] if n in turned else r) for n, r in res.items()}
    return (loss[0, 0], grad_x[None], *[res[n][0] for n in names], *[res[n][1] for n in names],
            *[res[n][2] for n in names], *[res[n][3] for n in names])
```
